```python
import math
import jax, jax.numpy as jnp
from jax import lax
import numpy as np

D_MODEL = 1024
BATCH = 8
SEQ = 2048
DEPTH = 1
DEC_BATCH = 128
DEC_SEQ = 8
PAST_LEN = 16384
PAGE_SIZE = 128

D_MIX = D_MODEL
D_GLA = D_MIX // 2
GLA_HEADS = 4
GLA_DV = D_GLA // GLA_HEADS
GLA_DK = GLA_DV // 2
GLA_KDIM = GLA_HEADS * GLA_DK
GLA_GATE_RANK = 16
GLA_GATE_NORM = 16.0
GLA_CHUNK = 64
D_S5 = D_MIX - D_GLA
S5_GROUP = 16
S5_GROUPS = D_S5 // S5_GROUP
S5_STATE = 64
N_META = 16
D_FF = ((-(-(8 * D_MODEL) // 3) + 255) // 256) * 256
D_IN = 2 * GLA_KDIM + 2 * D_GLA + GLA_GATE_RANK + D_S5
EPS = 1e-6

kernel_name = "hymba_gla_s5_sandwich_step"


def rms_norm(x, g):
    xf = x.astype(jnp.float32)
    y = xf * lax.rsqrt(jnp.mean(xf * xf, axis=-1, keepdims=True) + EPS)
    return (y * g.astype(jnp.float32)).astype(x.dtype)


def gla_chunked(q, k, v, lg, s0, chunk):
    bsz, length, nh, dk = q.shape
    dv = v.shape[-1]
    nc = length // chunk

    def to_chunks(t):
        return t.reshape(bsz, nc, chunk, nh, t.shape[-1]).transpose(1, 0, 3, 2, 4)

    qc, kc, vc, gc = to_chunks(q), to_chunks(k), to_chunks(v), to_chunks(lg)
    b = jnp.cumsum(gc, axis=3)
    b_last = b[:, :, :, -1:, :]
    q_dec = qc * jnp.exp(b)
    k_inv = kc * jnp.exp(-b)
    k_end = kc * jnp.exp(b_last - b)
    causal = jnp.tril(jnp.ones((chunk, chunk), dtype=bool))

    def step(S, xs):
        q_i, k_i, ke_i, v_i, bl_i = xs
        att = jnp.where(causal, jnp.einsum('bhtd,bhsd->bhts', q_i, k_i), 0.0)
        o = jnp.einsum('bhts,bhsv->bhtv', att, v_i) + jnp.einsum('bhtd,bhdv->bhtv', q_i, S)
        S = jnp.exp(bl_i[..., 0, :])[..., None] * S + jnp.einsum('bhsd,bhsv->bhdv', ke_i, v_i)
        return S, o

    S, o = lax.scan(step, s0, (q_dec, k_inv, k_end, vc, b_last))
    o = o.transpose(1, 0, 3, 2, 4).reshape(bsz, length, nh, dv)
    return o, S


def s5_scan(u, h0_re, h0_im, a_re, a_im, b_re, b_im, c_re, c_im, d_skip, log_dt):
    f32 = jnp.float32
    bsz, length, _ = u.shape
    uf = u.astype(f32)
    ug = uf.reshape(bsz, length, S5_GROUPS, S5_GROUP)
    lam_re = jnp.minimum(a_re.astype(f32), -1e-4)
    lam_im = a_im.astype(f32)
    dt = jnp.exp(log_dt.astype(f32))[:, None]
    mag = jnp.exp(lam_re * dt)
    abar_re = mag * jnp.cos(lam_im * dt)
    abar_im = mag * jnp.sin(lam_im * dt)
    den = lam_re * lam_re + lam_im * lam_im
    nr, ni = abar_re - 1.0, abar_im
    f_re = (nr * lam_re + ni * lam_im) / den
    f_im = (ni * lam_re - nr * lam_im) / den
    br, bi = b_re.astype(f32), b_im.astype(f32)
    bb_re = f_re[..., None] * br - f_im[..., None] * bi
    bb_im = f_re[..., None] * bi + f_im[..., None] * br
    bu_re = jnp.einsum('blgj,gnj->blgn', ug, bb_re)
    bu_im = jnp.einsum('blgj,gnj->blgn', ug, bb_im)
    a_r = jnp.broadcast_to(abar_re, bu_re.shape)
    a_i = jnp.broadcast_to(abar_im, bu_re.shape)

    def combine(e1, e2):
        a1r, a1i, b1r, b1i = e1
        a2r, a2i, b2r, b2i = e2
        return (a2r * a1r - a2i * a1i,
                a2r * a1i + a2i * a1r,
                a2r * b1r - a2i * b1i + b2r,
                a2r * b1i + a2i * b1r + b2i)

    pr, pi, hr, hi = lax.associative_scan(combine, (a_r, a_i, bu_re, bu_im), axis=1)
    h0r = h0_re.astype(f32)[:, None]
    h0i = h0_im.astype(f32)[:, None]
    h_re = pr * h0r - pi * h0i + hr
    h_im = pr * h0i + pi * h0r + hi
    y = (jnp.einsum('gjn,blgn->blgj', c_re.astype(f32), h_re)
         - jnp.einsum('gjn,blgn->blgj', c_im.astype(f32), h_im))
    y = y.reshape(bsz, length, D_S5) + d_skip.astype(f32) * uf
    return y, h_re[:, -1], h_im[:, -1]


def hybrid_layer(x, s_gla0, s5r0, s5i0, segments,
                 g_pre_mix, w_in, w_gk2, b_gk, gla_norm,
                 s5_a_re, s5_a_im, s5_b_re, s5_b_im, s5_c_re, s5_c_im, s5_d, s5_log_dt,
                 w_s5_glu, s5_norm, w_o, g_post_mix, g_pre_ffn, w_gate, w_up, w_down, g_post_ffn):
    f32 = jnp.float32
    bsz, length, _ = x.shape
    h = rms_norm(x, g_pre_mix)
    proj = h @ w_in
    cuts = [GLA_KDIM, 2 * GLA_KDIM, 2 * GLA_KDIM + D_GLA, 2 * GLA_KDIM + 2 * D_GLA,
            2 * GLA_KDIM + 2 * D_GLA + GLA_GATE_RANK]
    q, k, v, g, gk_lr, u = jnp.split(proj, cuts, axis=-1)

    lg = jax.nn.log_sigmoid((gk_lr @ w_gk2 + b_gk).astype(f32)) / GLA_GATE_NORM
    qh = q.astype(f32).reshape(bsz, length, GLA_HEADS, GLA_DK) * (GLA_DK ** -0.5)
    kh = k.astype(f32).reshape(bsz, length, GLA_HEADS, GLA_DK)
    vh = v.astype(f32).reshape(bsz, length, GLA_HEADS, GLA_DV)
    lgh = lg.reshape(bsz, length, GLA_HEADS, GLA_DK)
    S = s_gla0.astype(f32)
    outs = []
    start = 0
    for seg_len, chunk in segments:
        o, S = gla_chunked(qh[:, start:start + seg_len], kh[:, start:start + seg_len],
                           vh[:, start:start + seg_len], lgh[:, start:start + seg_len], S, chunk)
        outs.append(o)
        start += seg_len
    o_gla = jnp.concatenate(outs, axis=1)
    o_gla = rms_norm(o_gla, gla_norm).reshape(bsz, length, D_GLA)
    o_gla = (o_gla * jax.nn.silu(g.astype(f32))).astype(x.dtype)

    y5, h_re, h_im = s5_scan(u, s5r0, s5i0, s5_a_re, s5_a_im, s5_b_re, s5_b_im,
                             s5_c_re, s5_c_im, s5_d, s5_log_dt)
    y5 = jax.nn.gelu(y5)
    y5 = y5 * jax.nn.sigmoid(y5 @ w_s5_glu.astype(f32))
    y5 = rms_norm(y5, s5_norm).astype(x.dtype)

    mix = jnp.concatenate([o_gla, y5], axis=-1) @ w_o
    x = x + rms_norm(mix, g_post_mix)

    h = rms_norm(x, g_pre_ffn)
    f = (jax.nn.silu(h @ w_gate) * (h @ w_up)) @ w_down
    x = x + rms_norm(f, g_post_ffn)
    return x, S, h_re, h_im


def setup_inputs(seed: int = 0) -> dict:
    key = jax.random.key(seed)
    ks = iter(jax.random.split(key, 40))
    nrm = lambda shape, scale: jax.random.normal(next(ks), shape, jnp.float32) * scale
    gain = lambda shape: 1.0 + nrm(shape, 0.01)
    L = DEPTH
    n_idx = jnp.arange(S5_STATE, dtype=jnp.float32)
    return {
        "x_prompt": nrm((BATCH, SEQ, D_MODEL), 1.0),
        "x_sample": nrm((DEC_BATCH, DEC_SEQ, D_MODEL), 1.0),
        "state_gla": nrm((L, DEC_BATCH, GLA_HEADS, GLA_DK, GLA_DV), 0.3),
        "state_s5_re": nrm((L, DEC_BATCH, S5_GROUPS, S5_STATE), 0.1),
        "state_s5_im": nrm((L, DEC_BATCH, S5_GROUPS, S5_STATE), 0.1),
        "meta_tokens": nrm((N_META, D_MODEL), 1.0),
        "g_pre_mix": gain((L, D_MODEL)),
        "w_in": nrm((L, D_MODEL, D_IN), D_MODEL ** -0.5),
        "w_gk2": nrm((L, GLA_GATE_RANK, GLA_KDIM), GLA_GATE_RANK ** -0.5),
        "b_gk": nrm((L, GLA_KDIM), 0.01),
        "gla_norm": gain((L, GLA_DV)),
        "s5_a_re": -0.5 + nrm((L, S5_GROUPS, S5_STATE), 0.01),
        "s5_a_im": math.pi * n_idx + nrm((L, S5_GROUPS, S5_STATE), 0.01),
        "s5_b_re": nrm((L, S5_GROUPS, S5_STATE, S5_GROUP), (2.0 * S5_GROUP) ** -0.5),
        "s5_b_im": nrm((L, S5_GROUPS, S5_STATE, S5_GROUP), (2.0 * S5_GROUP) ** -0.5),
        "s5_c_re": nrm((L, S5_GROUPS, S5_GROUP, S5_STATE), (2.0 * S5_STATE) ** -0.5),
        "s5_c_im": nrm((L, S5_GROUPS, S5_GROUP, S5_STATE), (2.0 * S5_STATE) ** -0.5),
        "s5_d": nrm((L, D_S5), 0.5),
        "s5_log_dt": jax.random.uniform(next(ks), (L, S5_GROUPS), jnp.float32,
                                         math.log(1e-3), math.log(1e-1)),
        "w_s5_glu": nrm((L, D_S5, D_S5), D_S5 ** -0.5),
        "s5_norm": gain((L, D_S5)),
        "w_o": nrm((L, D_MIX, D_MODEL), D_MIX ** -0.5),
        "g_post_mix": gain((L, D_MODEL)),
        "g_pre_ffn": gain((L, D_MODEL)),
        "w_gate": nrm((L, D_MODEL, D_FF), D_MODEL ** -0.5),
        "w_up": nrm((L, D_MODEL, D_FF), D_MODEL ** -0.5),
        "w_down": nrm((L, D_FF, D_MODEL), D_FF ** -0.5),
        "g_post_ffn": gain((L, D_MODEL)),
    }


def reference(x_prompt, x_sample, state_gla, state_s5_re, state_s5_im, meta_tokens,
              g_pre_mix, w_in, w_gk2, b_gk, gla_norm,
              s5_a_re, s5_a_im, s5_b_re, s5_b_im, s5_c_re, s5_c_im, s5_d, s5_log_dt,
              w_s5_glu, s5_norm, w_o, g_post_mix, g_pre_ffn, w_gate, w_up, w_down, g_post_ffn):
    bp, seq_p, _ = x_prompt.shape
    bs, seq_s, _ = x_sample.shape
    meta = jnp.broadcast_to(meta_tokens.astype(x_prompt.dtype)[None], (bp, N_META, D_MODEL))
    xp = jnp.concatenate([meta, x_prompt], axis=1)
    xs = x_sample
    seg_prompt = ((N_META, N_META), (seq_p, GLA_CHUNK))
    seg_sample = ((seq_s, seq_s),)
    zeros_gla = jnp.zeros((bp, GLA_HEADS, GLA_DK, GLA_DV), jnp.float32)
    zeros_s5 = jnp.zeros((bp, S5_GROUPS, S5_STATE), jnp.float32)
    gp_l, rp_l, ip_l, gs_l, rs_l, is_l = [], [], [], [], [], []
    for l in range(DEPTH):
        w = (g_pre_mix[l], w_in[l], w_gk2[l], b_gk[l], gla_norm[l],
             s5_a_re[l], s5_a_im[l], s5_b_re[l], s5_b_im[l], s5_c_re[l], s5_c_im[l],
             s5_d[l], s5_log_dt[l], w_s5_glu[l], s5_norm[l], w_o[l], g_post_mix[l],
             g_pre_ffn[l], w_gate[l], w_up[l], w_down[l], g_post_ffn[l])
        xp, sg, sr, si = hybrid_layer(xp, zeros_gla, zeros_s5, zeros_s5, seg_prompt, *w)
        gp_l.append(sg); rp_l.append(sr); ip_l.append(si)
        xs, sg, sr, si = hybrid_layer(xs, state_gla[l], state_s5_re[l], state_s5_im[l],
                                      seg_sample, *w)
        gs_l.append(sg); rs_l.append(sr); is_l.append(si)
    y_prompt = xp[:, N_META:]
    y_sample = xs
    new_gla_prompt = jnp.stack(gp_l)
    new_s5_re_prompt = jnp.stack(rp_l)
    new_s5_im_prompt = jnp.stack(ip_l)
    new_gla_sample = jnp.stack(gs_l)
    new_s5_re_sample = jnp.stack(rs_l)
    new_s5_im_sample = jnp.stack(is_l)
    return (y_prompt, y_sample, new_gla_prompt, new_s5_re_prompt, new_s5_im_prompt,
            new_gla_sample, new_s5_re_sample, new_s5_im_sample)
```

```python
import functools

import jax
import jax.numpy as jnp
from jax import lax
from jax.experimental import pallas as pl
from jax.experimental.pallas import tpu as pltpu

F32 = jnp.float32
BF16 = jnp.bfloat16

D_MODEL = 1024
D_GLA = 512
GLA_HEADS = 4
GLA_DV = 128
GLA_DK = 64
GLA_KDIM = 256
GATE_RANK = 16
GATE_NORM = 16.0
GLA_CHUNK = 64
D_S5 = 512
S5_GROUP = 16
S5_GROUPS = 32
S5_STATE = 64
N_META = 16
D_FF = 2816
EPS = 1e-6
S5_BLOCK = 16
PROJ_W = 2176
FF_CHUNK = 256
VMEM_LIMIT = 48 * 1024 * 1024


def _rms(x, g):
    return x * lax.rsqrt(jnp.mean(x * x, axis=-1, keepdims=True) + EPS) * g


def _dot(a, b):
    return jnp.dot(a, b, preferred_element_type=F32)


def _dot_nt(a, b):
    return lax.dot_general(a, b, (((1,), (1,)), ((), ())), preferred_element_type=F32)


def _dot_tn(a, b):
    return lax.dot_general(a, b, (((0,), (0,)), ((), ())), preferred_element_type=F32)


def _const_spec(shape):
    zeros = (0,) * len(shape)
    return pl.BlockSpec(shape, lambda *_: zeros)


def _s5_prep_kernel(are_ref, aim_ref, ldt_ref, btr_ref, bti_ref, cr_ref, ci_ref, ctr_ref, cti_ref,
                    k_ref, er_ref, ei_ref, mor_ref, moi_ref, p1r_ref, p1i_ref):
    lam_re = jnp.minimum(are_ref[0], -1e-4)
    lam_im = aim_ref[0]
    dt = jnp.exp(ldt_ref[0])
    mag = jnp.exp(lam_re * dt)
    ang = lam_im * dt
    ab_re = mag * jnp.cos(ang)
    ab_im = mag * jnp.sin(ang)
    den = lam_re * lam_re + lam_im * lam_im
    nr, ni = ab_re - 1.0, ab_im
    f_re = (nr * lam_re + ni * lam_im) / den
    f_im = (ni * lam_re - nr * lam_im) / den
    bt_re, bt_im = btr_ref[0], bti_ref[0]
    bb_re = f_re * bt_re - f_im * bt_im
    bb_im = f_re * bt_im + f_im * bt_re
    lag = (lax.broadcasted_iota(jnp.int32, bt_re.shape, 1) >> 4).astype(F32)
    pm = jnp.exp(lag * (lam_re * dt))
    pa = lag * ang
    pk_re = pm * jnp.cos(pa)
    pk_im = pm * jnp.sin(pa)
    e_re = pk_re * bb_re - pk_im * bb_im
    e_im = pk_re * bb_im + pk_im * bb_re
    p1_re = pk_re * ab_re - pk_im * ab_im
    p1_im = pk_re * ab_im + pk_im * ab_re
    ct_re, ct_im = ctr_ref[0], cti_ref[0]
    mor_ref[0] = ct_re * p1_re - ct_im * p1_im
    moi_ref[0] = -(ct_re * p1_im + ct_im * p1_re)
    er_ref[0] = e_re
    ei_ref[0] = e_im
    p1r_ref[0] = p1_re
    p1i_ref[0] = p1_im
    hi = lax.Precision.HIGHEST
    k_ref[0] = (jnp.dot(cr_ref[0], e_re, precision=hi, preferred_element_type=F32)
                - jnp.dot(ci_ref[0], e_im, precision=hi, preferred_element_type=F32))


def _s5_prep(a_re, a_im, b_re, b_im, c_re, c_im, log_dt):
    g, n, j = S5_GROUPS, S5_STATE, S5_GROUP
    w = S5_BLOCK * j
    col = lambda t: t.reshape(g, n, 1)
    bt = lambda t: jnp.tile(t, (1, 1, S5_BLOCK))
    ct = lambda t: jnp.tile(jnp.swapaxes(t, 1, 2), (1, 1, S5_BLOCK))
    big = pl.BlockSpec((1, n, w), lambda i: (i, 0, 0))
    colspec = pl.BlockSpec((1, n, 1), lambda i: (i, 0, 0))
    cspec = pl.BlockSpec((1, j, n), lambda i: (i, 0, 0))
    outs = pl.pallas_call(
        _s5_prep_kernel,
        grid=(g,),
        in_specs=[colspec, colspec, pl.BlockSpec((1, 1, 1), lambda i: (i, 0, 0)),
                  big, big, cspec, cspec, big, big],
        out_specs=[pl.BlockSpec((1, j, w), lambda i: (i, 0, 0))] + [big] * 6,
        out_shape=[jax.ShapeDtypeStruct((g, j, w), F32)] + [jax.ShapeDtypeStruct((g, n, w), F32)] * 6,
        name="s5_prep",
    )(col(a_re), col(a_im), log_dt.reshape(g, 1, 1), bt(b_re), bt(b_im), c_re, c_im, ct(c_re), ct(c_im))
    kk, e_re, e_im, mo_re, mo_im, p1_re, p1_im = outs

    q = S5_BLOCK
    kp = kk.reshape(g, j, q, j).transpose(0, 2, 3, 1)
    s_idx = jnp.arange(q)[:, None]
    t_idx = jnp.arange(q)[None, :]
    lag = t_idx - s_idx
    toe = kp[:, jnp.clip(lag, 0, q - 1)]
    toe = jnp.where((lag >= 0)[None, :, :, None, None], toe, 0.0)
    toe = toe.transpose(0, 1, 3, 2, 4)

    def mats(steps):
        m_intra = toe[:, :steps, :, :steps, :].reshape(g, steps * j, steps * j).astype(BF16)
        rev = lambda e: (e.reshape(g, n, q, j)[:, :, steps - 1::-1, :]
                         .transpose(0, 2, 3, 1).reshape(g, steps * j, n).astype(BF16))
        a_re_s = p1_re[:, :, (steps - 1) * j].reshape(g, 1, n)
        a_im_s = p1_im[:, :, (steps - 1) * j].reshape(g, 1, n)
        return dict(m_intra=m_intra, m_in_re=rev(e_re), m_in_im=rev(e_im),
                    m_out_re=mo_re[:, :, :steps * j].astype(BF16),
                    m_out_im=mo_im[:, :, :steps * j].astype(BF16),
                    a_re=a_re_s, a_im=a_im_s)

    return mats


def _in_proj_kernel(x_ref, gpre_ref, w_ref, wgk_ref, bgk_ref,
                    q_ref, k_ref, v_ref, g_ref, u_ref, lg_ref):
    h = _rms(x_ref[...], gpre_ref[...])
    proj = _dot(h.astype(BF16), w_ref[...])
    q_ref[...] = proj[:, 0:256]
    k_ref[...] = proj[:, 256:512]
    v_ref[...] = proj[:, 512:1024]
    g_ref[...] = proj[:, 1024:1536]
    u_ref[...] = proj[:, 1536:2048]
    z = _dot(proj[:, 2048:PROJ_W].astype(BF16), wgk_ref[...]) + bgk_ref[...]
    lg_ref[...] = jax.nn.log_sigmoid(z) * (1.0 / GATE_NORM)


def _in_proj(x, gpre, w_p, wgk_p, bgk, tm):
    t = x.shape[0]
    row = lambda w: pl.BlockSpec((tm, w), lambda i: (i, 0))
    widths = (256, 256, 512, 512, 512, 256)
    return pl.pallas_call(
        _in_proj_kernel,
        grid=(t // tm,),
        in_specs=[row(D_MODEL), _const_spec((1, D_MODEL)), _const_spec((D_MODEL, PROJ_W)),
                  _const_spec((128, GLA_KDIM)), _const_spec((1, GLA_KDIM))],
        out_specs=[row(w) for w in widths],
        out_shape=[jax.ShapeDtypeStruct((t, w), F32) for w in widths],
        compiler_params=pltpu.CompilerParams(dimension_semantics=("parallel",),
                                             vmem_limit_bytes=VMEM_LIMIT),
        name="in_proj",
    )(x, gpre, w_p, wgk_p, bgk)


def _gla_kernel(q_ref, k_ref, v_ref, lg_ref, s0_ref, o_ref, s_ref, *, bb, chunk):
    r = bb * chunk
    shift = chunk.bit_length() - 1

    @pl.when(pl.program_id(1) == 0)
    def _():
        s_ref[...] = s0_ref[...]

    ri = lax.broadcasted_iota(jnp.int32, (r, r), 0)
    ci = lax.broadcasted_iota(jnp.int32, (r, r), 1)
    same = (ri >> shift) == (ci >> shift)
    causal = same & (ri >= ci)
    tri_bf = jnp.where(causal, 1.0, 0.0).astype(BF16)
    all_bf = jnp.where(same, 1.0, 0.0).astype(BF16)
    lane_head = lax.broadcasted_iota(jnp.int32, (r, GLA_KDIM), 1) >> 6
    eye = (lax.broadcasted_iota(jnp.int32, (GLA_KDIM, GLA_KDIM), 0)
           == lax.broadcasted_iota(jnp.int32, (GLA_KDIM, GLA_KDIM), 1))

    lg = lg_ref[...].reshape(r, GLA_KDIM)
    lg_hi = lg.astype(BF16)
    lg_lo = (lg - lg_hi.astype(F32)).astype(BF16)
    b = _dot(tri_bf, lg_hi) + _dot(tri_bf, lg_lo)
    bl = _dot(all_bf, lg_hi) + _dot(all_bf, lg_lo)
    q = q_ref[...].reshape(r, GLA_KDIM)
    k = k_ref[...].reshape(r, GLA_KDIM)
    v = v_ref[...].reshape(r, D_GLA)
    qd = q * (GLA_DK ** -0.5) * jnp.exp(b)
    ki = (k * jnp.exp(-b)).astype(BF16)
    ke = k * jnp.exp(bl - b)
    dl = jnp.exp(bl)

    qd_h, ke_h, v_h, o_intra = [], [], [], []
    for h in range(GLA_HEADS):
        hm = lane_head == h
        qd_h.append(jnp.where(hm, qd, 0.0).astype(BF16))
        ke_h.append(jnp.where(hm, ke, 0.0).astype(BF16))
        v_h.append(v[:, h * GLA_DV:(h + 1) * GLA_DV].astype(BF16))
        att = jnp.where(causal, _dot_nt(qd_h[h], ki), 0.0).astype(BF16)
        o_intra.append(_dot(att, v_h[h]))

    for i in range(bb):
        rows = slice(i * chunk, (i + 1) * chunk)
        s_old = s_ref[i]
        s_bf = s_old.astype(BF16)
        last = dl[(i + 1) * chunk - 1:(i + 1) * chunk, :]
        dcol = jnp.sum(jnp.where(eye, last, 0.0), axis=1, keepdims=True)
        acc = dcol * s_old
        for h in range(GLA_HEADS):
            o_ref[i, :, h * GLA_DV:(h + 1) * GLA_DV] = o_intra[h][rows] + _dot(qd_h[h][rows], s_bf)
            acc = acc + _dot_tn(ke_h[h][rows], v_h[h][rows])
        s_ref[i] = acc


def _gla(q, k, v, lg, s0, bb, chunk):
    b, l, _ = q.shape
    blk = lambda w: pl.BlockSpec((bb, chunk, w), lambda i, c: (i, c, 0))
    sspec = pl.BlockSpec((bb, GLA_KDIM, GLA_DV), lambda i, c: (i, 0, 0))
    return pl.pallas_call(
        functools.partial(_gla_kernel, bb=bb, chunk=chunk),
        grid=(b // bb, l // chunk),
        in_specs=[blk(GLA_KDIM), blk(GLA_KDIM), blk(D_GLA), blk(GLA_KDIM), sspec],
        out_specs=[blk(D_GLA), sspec],
        out_shape=[jax.ShapeDtypeStruct((b, l, D_GLA), F32),
                   jax.ShapeDtypeStruct((b, GLA_KDIM, GLA_DV), F32)],
        compiler_params=pltpu.CompilerParams(dimension_semantics=("parallel", "arbitrary"),
                                             vmem_limit_bytes=VMEM_LIMIT),
        name=f"gla_c{chunk}",
    )(q, k, v, lg, s0)


def _s5_kernel(u_ref, mi_ref, minr_ref, mini_ref, mor_ref, moi_ref, ar_ref, ai_ref, h0r_ref, h0i_ref,
               y_ref, hfr_ref, hfi_ref, vr_s, vi_s, hr_s, hi_s, *, n_blocks, batch):
    u = u_ref[0]
    vr_s[...] = _dot(u, minr_ref[0])
    vi_s[...] = _dot(u, mini_ref[0])
    ar = ar_ref[0]
    ai = ai_ref[0]

    def body(c, carry):
        hr, hi = carry
        rows = pl.ds(pl.multiple_of(c * batch, batch), batch)
        hr_s[rows, :] = hr
        hi_s[rows, :] = hi
        return (ar * hr - ai * hi + vr_s[rows, :], ar * hi + ai * hr + vi_s[rows, :])

    hr, hi = lax.fori_loop(0, n_blocks, body, (h0r_ref[0], h0i_ref[0]))
    hfr_ref[0] = hr
    hfi_ref[0] = hi
    y_ref[0] = (_dot(u, mi_ref[0]) + _dot(hr_s[...].astype(BF16), mor_ref[0])
                + _dot(hi_s[...].astype(BF16), moi_ref[0]))


def _s5(u_t, m, h0_re, h0_im, n_blocks, batch):
    g, r, w = u_t.shape
    n = S5_STATE
    per_g = lambda a, b_: pl.BlockSpec((1, a, b_), lambda i: (i, 0, 0))
    return pl.pallas_call(
        functools.partial(_s5_kernel, n_blocks=n_blocks, batch=batch),
        grid=(g,),
        in_specs=[per_g(r, w), per_g(w, w), per_g(w, n), per_g(w, n), per_g(n, w), per_g(n, w),
                  per_g(1, n), per_g(1, n), per_g(batch, n), per_g(batch, n)],
        out_specs=[per_g(r, w), per_g(batch, n), per_g(batch, n)],
        out_shape=[jax.ShapeDtypeStruct((g, r, w), F32),
                   jax.ShapeDtypeStruct((g, batch, n), F32),
                   jax.ShapeDtypeStruct((g, batch, n), F32)],
        scratch_shapes=[pltpu.VMEM((r, n), F32)] * 4,
        compiler_params=pltpu.CompilerParams(dimension_semantics=("parallel",),
                                             vmem_limit_bytes=VMEM_LIMIT),
        name=f"s5_w{w}",
    )(u_t, m["m_intra"], m["m_in_re"], m["m_in_im"], m["m_out_re"], m["m_out_im"],
      m["a_re"], m["a_im"], h0_re, h0_im)


def _to_blocks(u, steps):
    b, l, _ = u.shape
    t = u.reshape(b, l // steps, steps, S5_GROUPS, S5_GROUP).transpose(3, 1, 0, 2, 4)
    return t.reshape(S5_GROUPS, (l // steps) * b, steps * S5_GROUP).astype(BF16)


def _from_blocks(y, b, l, steps):
    t = y.reshape(S5_GROUPS, l // steps, b, steps, S5_GROUP).transpose(2, 1, 3, 0, 4)
    return t.reshape(b * l, D_S5)


def _mix_kernel(x_ref, o_ref, g_ref, y_ref, u_ref, gn_ref, d_ref, wglu_ref, s5n_ref,
                woa_ref, wob_ref, gpost_ref, out_ref):
    o = o_ref[...]
    gn = gn_ref[...]
    heads = []
    for h in range(GLA_HEADS):
        heads.append(_rms(o[:, h * GLA_DV:(h + 1) * GLA_DV], gn))
    og = jnp.concatenate(heads, axis=1) * jax.nn.silu(g_ref[...])
    y = jax.nn.gelu(y_ref[...] + d_ref[...] * u_ref[...])
    y = y * jax.nn.sigmoid(_dot(y.astype(BF16), wglu_ref[...]))
    y = _rms(y, s5n_ref[...])
    mix = _dot(og.astype(BF16), woa_ref[...]) + _dot(y.astype(BF16), wob_ref[...])
    out_ref[...] = x_ref[...] + _rms(mix, gpost_ref[...])


def _mix(x, o, g, y, u, gn, d, wglu, s5n, woa, wob, gpost, tm):
    t = x.shape[0]
    row = lambda w: pl.BlockSpec((tm, w), lambda i: (i, 0))
    return pl.pallas_call(
        _mix_kernel,
        grid=(t // tm,),
        in_specs=[row(D_MODEL), row(D_GLA), row(D_GLA), row(D_S5), row(D_S5),
                  _const_spec((1, GLA_DV)), _const_spec((1, D_S5)), _const_spec((D_S5, D_S5)),
                  _const_spec((1, D_S5)), _const_spec((D_GLA, D_MODEL)), _const_spec((D_S5, D_MODEL)),
                  _const_spec((1, D_MODEL))],
        out_specs=row(D_MODEL),
        out_shape=jax.ShapeDtypeStruct((t, D_MODEL), F32),
        compiler_params=pltpu.CompilerParams(dimension_semantics=("parallel",),
                                             vmem_limit_bytes=VMEM_LIMIT),
        name="mix",
    )(x, o, g, y, u, gn, d, wglu, s5n, woa, wob, gpost)


def _ffn_kernel(x_ref, gpre_ref, wg_ref, wu_ref, wd_ref, gpost_ref, out_ref):
    x = x_ref[...]
    h = _rms(x, gpre_ref[...]).astype(BF16)
    acc = jnp.zeros(x.shape, F32)
    for c in range(D_FF // FF_CHUNK):
        cols = slice(c * FF_CHUNK, (c + 1) * FF_CHUNK)
        act = jax.nn.silu(_dot(h, wg_ref[:, cols])) * _dot(h, wu_ref[:, cols])
        acc = acc + _dot(act.astype(BF16), wd_ref[cols, :])
    out_ref[...] = x + _rms(acc, gpost_ref[...])


def _ffn(x, gpre, wg, wu, wd, gpost, tm):
    t = x.shape[0]
    row = pl.BlockSpec((tm, D_MODEL), lambda i: (i, 0))
    return pl.pallas_call(
        _ffn_kernel,
        grid=(t // tm,),
        in_specs=[row, _const_spec((1, D_MODEL)), _const_spec((D_MODEL, D_FF)),
                  _const_spec((D_MODEL, D_FF)), _const_spec((D_FF, D_MODEL)), _const_spec((1, D_MODEL))],
        out_specs=row,
        out_shape=jax.ShapeDtypeStruct((t, D_MODEL), F32),
        compiler_params=pltpu.CompilerParams(dimension_semantics=("parallel",),
                                             vmem_limit_bytes=VMEM_LIMIT),
        name="ffn",
    )(x, gpre, wg, wu, wd, gpost)


def kernel(x_prompt, x_sample, state_gla, state_s5_re, state_s5_im, meta_tokens, g_pre_mix, w_in, w_gk2, b_gk, gla_norm, s5_a_re, s5_a_im, s5_b_re, s5_b_im, s5_c_re, s5_c_im, s5_d, s5_log_dt, w_s5_glu, s5_norm, w_o, g_post_mix, g_pre_ffn, w_gate, w_up, w_down, g_post_ffn):
    assert g_pre_mix.shape[0] == 1, "single-layer step"
    bp, seq_p, _ = x_prompt.shape
    bs, seq_s, _ = x_sample.shape
    row = lambda t: t[0].reshape(1, -1)

    w = w_in[0]
    c0, c1, c2, c3, c4 = 256, 512, 1024, 1536, 1536 + GATE_RANK
    w_p = jnp.concatenate([w[:, :c3], w[:, c4:], w[:, c3:c4],
                           jnp.zeros((D_MODEL, 128 - GATE_RANK), F32)], axis=1).astype(BF16)
    wgk_p = jnp.concatenate([w_gk2[0], jnp.zeros((128 - GATE_RANK, GLA_KDIM), F32)], axis=0).astype(BF16)
    wo_bf = w_o[0].astype(BF16)
    s5m = _s5_prep(s5_a_re[0], s5_a_im[0], s5_b_re[0], s5_b_im[0], s5_c_re[0], s5_c_im[0], s5_log_dt[0])

    in_proj = lambda x2d, tm: _in_proj(x2d, row(g_pre_mix), w_p, wgk_p, row(b_gk), tm)

    def mixers(x2d, b, l, chunk, steps, bb, s0, h0_re, h0_im, tm):
        q, k, v, g, u, lg = in_proj(x2d, tm)
        r3 = lambda t: t.reshape(b, l, t.shape[-1])
        o, s_new = _gla(r3(q), r3(k), r3(v), r3(lg), s0, bb, chunk)
        y_t, hf_re, hf_im = _s5(_to_blocks(r3(u), steps), s5m(steps), h0_re, h0_im, l // steps, b)
        return o.reshape(b * l, D_GLA), g, _from_blocks(y_t, b, l, steps), u, s_new, hf_re, hf_im

    def finish(x2d, o, g, y, u, tm):
        x1 = _mix(x2d, o, g, y, u, row(gla_norm), row(s5_d), w_s5_glu[0].astype(BF16), row(s5_norm),
                  wo_bf[:D_GLA], wo_bf[D_GLA:], row(g_post_mix), tm)
        return _ffn(x1, row(g_pre_ffn), w_gate[0].astype(BF16), w_up[0].astype(BF16),
                    w_down[0].astype(BF16), row(g_post_ffn), tm)

    xm = jnp.broadcast_to(meta_tokens[None], (bp, N_META, D_MODEL)).reshape(bp * N_META, D_MODEL)
    zs = jnp.zeros((bp, GLA_KDIM, GLA_DV), F32)
    zh = jnp.zeros((S5_GROUPS, bp, S5_STATE), F32)
    _, _, _, _, s_meta, hm_re, hm_im = mixers(xm, bp, N_META, N_META, S5_BLOCK, bp, zs, zh, zh, bp * N_META)

    xp = x_prompt.reshape(bp * seq_p, D_MODEL)
    o, g, y, u, s_p, hp_re, hp_im = mixers(xp, bp, seq_p, GLA_CHUNK, S5_BLOCK, 1, s_meta, hm_re, hm_im, 256)
    y_prompt = finish(xp, o, g, y, u, 256).reshape(bp, seq_p, D_MODEL)

    xs = x_sample.reshape(bs * seq_s, D_MODEL)
    to_g = lambda t: jnp.swapaxes(t[0], 0, 1)
    s0 = state_gla[0].reshape(bs, GLA_KDIM, GLA_DV)
    o, g, y, u, s_s, hs_re, hs_im = mixers(xs, bs, seq_s, seq_s, seq_s, 16, s0,
                                           to_g(state_s5_re), to_g(state_s5_im), 256)
    y_sample = finish(xs, o, g, y, u, 256).reshape(bs, seq_s, D_MODEL)

    gla_out = lambda s, b: s.reshape(1, b, GLA_HEADS, GLA_DK, GLA_DV)
    s5_out = lambda h: jnp.swapaxes(h, 0, 1)[None]
    return (y_prompt, y_sample, gla_out(s_p, bp), s5_out(hp_re), s5_out(hp_im),
            gla_out(s_s, bs), s5_out(hs_re), s5_out(hs_im))
```

```python
import functools

import jax
import jax.numpy as jnp
from jax import lax
from jax.experimental import pallas as pl
from jax.experimental.pallas import tpu as pltpu

F32 = jnp.float32
BF16 = jnp.bfloat16

D_MODEL = 1024
D_GLA = 512
GLA_HEADS = 4
GLA_DV = 128
GLA_DK = 64
GLA_KDIM = 256
GATE_RANK = 16
GATE_NORM = 16.0
GLA_CHUNK = 64
D_S5 = 512
S5_GROUP = 16
S5_GROUPS = 32
S5_STATE = 64
N_META = 16
D_FF = 2816
EPS = 1e-6
S5_BLOCK = 16
PROJ_W = 2176
FF_CHUNK = 256
VMEM_LIMIT = 48 * 1024 * 1024


def _rms(x, g):
    return x * lax.rsqrt(jnp.mean(x * x, axis=-1, keepdims=True) + EPS) * g


def _dot(a, b):
    return jnp.dot(a, b, preferred_element_type=F32)


def _dot_nt(a, b):
    return lax.dot_general(a, b, (((1,), (1,)), ((), ())), preferred_element_type=F32)


def _dot_tn(a, b):
    return lax.dot_general(a, b, (((0,), (0,)), ((), ())), preferred_element_type=F32)


def _const_spec(shape):
    zeros = (0,) * len(shape)
    return pl.BlockSpec(shape, lambda *_: zeros)


def _s5_prep_kernel(are_ref, aim_ref, ldt_ref, btr_ref, bti_ref, cr_ref, ci_ref, ctr_ref, cti_ref,
                    k_ref, er_ref, ei_ref, mor_ref, moi_ref, p1r_ref, p1i_ref):
    lam_re = jnp.minimum(are_ref[0], -1e-4)
    lam_im = aim_ref[0]
    dt = jnp.exp(ldt_ref[0])
    mag = jnp.exp(lam_re * dt)
    ang = lam_im * dt
    ab_re = mag * jnp.cos(ang)
    ab_im = mag * jnp.sin(ang)
    den = lam_re * lam_re + lam_im * lam_im
    nr, ni = ab_re - 1.0, ab_im
    f_re = (nr * lam_re + ni * lam_im) / den
    f_im = (ni * lam_re - nr * lam_im) / den
    bt_re, bt_im = btr_ref[0], bti_ref[0]
    bb_re = f_re * bt_re - f_im * bt_im
    bb_im = f_re * bt_im + f_im * bt_re
    lag = (lax.broadcasted_iota(jnp.int32, bt_re.shape, 1) >> 4).astype(F32)
    pm = jnp.exp(lag * (lam_re * dt))
    pa = lag * ang
    pk_re = pm * jnp.cos(pa)
    pk_im = pm * jnp.sin(pa)
    e_re = pk_re * bb_re - pk_im * bb_im
    e_im = pk_re * bb_im + pk_im * bb_re
    p1_re = pk_re * ab_re - pk_im * ab_im
    p1_im = pk_re * ab_im + pk_im * ab_re
    ct_re, ct_im = ctr_ref[0], cti_ref[0]
    mor_ref[0] = ct_re * p1_re - ct_im * p1_im
    moi_ref[0] = -(ct_re * p1_im + ct_im * p1_re)
    er_ref[0] = e_re
    ei_ref[0] = e_im
    p1r_ref[0] = p1_re
    p1i_ref[0] = p1_im
    hi = lax.Precision.HIGHEST
    k_ref[0] = (jnp.dot(cr_ref[0], e_re, precision=hi, preferred_element_type=F32)
                - jnp.dot(ci_ref[0], e_im, precision=hi, preferred_element_type=F32))


def _s5_prep(a_re, a_im, b_re, b_im, c_re, c_im, log_dt):
    g, n, j = S5_GROUPS, S5_STATE, S5_GROUP
    w = S5_BLOCK * j
    col = lambda t: t.reshape(g, n, 1)
    bt = lambda t: jnp.tile(t, (1, 1, S5_BLOCK))
    ct = lambda t: jnp.tile(jnp.swapaxes(t, 1, 2), (1, 1, S5_BLOCK))
    big = pl.BlockSpec((1, n, w), lambda i: (i, 0, 0))
    colspec = pl.BlockSpec((1, n, 1), lambda i: (i, 0, 0))
    cspec = pl.BlockSpec((1, j, n), lambda i: (i, 0, 0))
    outs = pl.pallas_call(
        _s5_prep_kernel,
        grid=(g,),
        in_specs=[colspec, colspec, pl.BlockSpec((1, 1, 1), lambda i: (i, 0, 0)),
                  big, big, cspec, cspec, big, big],
        out_specs=[pl.BlockSpec((1, j, w), lambda i: (i, 0, 0))] + [big] * 6,
        out_shape=[jax.ShapeDtypeStruct((g, j, w), F32)] + [jax.ShapeDtypeStruct((g, n, w), F32)] * 6,
        name="s5_prep",
    )(col(a_re), col(a_im), log_dt.reshape(g, 1, 1), bt(b_re), bt(b_im), c_re, c_im, ct(c_re), ct(c_im))
    kk, e_re, e_im, mo_re, mo_im, p1_re, p1_im = outs

    q = S5_BLOCK
    kp = kk.reshape(g, j, q, j).transpose(0, 2, 3, 1)
    s_idx = jnp.arange(q)[:, None]
    t_idx = jnp.arange(q)[None, :]
    lag = t_idx - s_idx
    toe = kp[:, jnp.clip(lag, 0, q - 1)]
    toe = jnp.where((lag >= 0)[None, :, :, None, None], toe, 0.0)
    toe = toe.transpose(0, 1, 3, 2, 4)

    def mats(steps):
        m_intra = toe[:, :steps, :, :steps, :].reshape(g, steps * j, steps * j).astype(BF16)
        rev = lambda e: (e.reshape(g, n, q, j)[:, :, steps - 1::-1, :]
                         .transpose(0, 2, 3, 1).reshape(g, steps * j, n).astype(BF16))
        a_re_s = p1_re[:, :, (steps - 1) * j].reshape(g, 1, n)
        a_im_s = p1_im[:, :, (steps - 1) * j].reshape(g, 1, n)
        return dict(m_intra=m_intra, m_in_re=rev(e_re), m_in_im=rev(e_im),
                    m_out_re=mo_re[:, :, :steps * j].astype(BF16),
                    m_out_im=mo_im[:, :, :steps * j].astype(BF16),
                    a_re=a_re_s, a_im=a_im_s)

    return mats


def _project(x, gpre_ref, w_ref, wgk_ref, bgk_ref):
    h = _rms(x, gpre_ref[...])
    proj = _dot(h.astype(BF16), w_ref[...])
    z = _dot(proj[:, 2048:PROJ_W].astype(BF16), wgk_ref[...]) + bgk_ref[...]
    lg = jax.nn.log_sigmoid(z) * (1.0 / GATE_NORM)
    return (proj[:, 0:256], proj[:, 256:512], proj[:, 512:1024], proj[:, 1024:1536], lg,
            proj[:, 1536:2048])


def _in_proj_kernel(x_ref, gpre_ref, w_ref, wgk_ref, bgk_ref, *out_refs):
    outs = _project(x_ref[...], gpre_ref, w_ref, wgk_ref, bgk_ref)
    for ref, val in zip(out_refs, outs):
        ref[...] = val


def _in_proj(x, gpre, w_p, wgk_p, bgk, tm):
    t = x.shape[0]
    row = lambda w: pl.BlockSpec((tm, w), lambda i: (i, 0))
    widths = (256, 256, 512, 512, 256, 512)
    return pl.pallas_call(
        _in_proj_kernel,
        grid=(t // tm,),
        in_specs=[row(D_MODEL), _const_spec((1, D_MODEL)), _const_spec((D_MODEL, PROJ_W)),
                  _const_spec((128, GLA_KDIM)), _const_spec((1, GLA_KDIM))],
        out_specs=[row(w) for w in widths],
        out_shape=[jax.ShapeDtypeStruct((t, w), F32) for w in widths],
        compiler_params=pltpu.CompilerParams(dimension_semantics=("parallel",),
                                             vmem_limit_bytes=VMEM_LIMIT),
        name="in_proj",
    )(x, gpre, w_p, wgk_p, bgk)


def _step_shape(nb, nc, steps, w):
    return (nb, nc, steps * w) if nc > 1 else (nb, steps * w)


def _step_view(a, nb, nc, steps):
    return a.reshape(_step_shape(nb, nc, steps, a.shape[-1]))


def _step_spec(nb, nc, cb, w):
    if nc > 1:
        return pl.BlockSpec((nb, cb, w), lambda i, t: (0, i, t))
    return pl.BlockSpec((nb, w), lambda i, t: (0, t))


def _in_proj_steps_kernel(x_ref, gpre_ref, w_ref, wgk_ref, bgk_ref,
                          q_ref, k_ref, v_ref, g_ref, lg_ref, u_ref, *, nb, cb):
    rows = nb * cb
    outs = _project(x_ref[...].reshape(rows, D_MODEL), gpre_ref, w_ref, wgk_ref, bgk_ref)
    for ref, val in zip((q_ref, k_ref, v_ref, g_ref, lg_ref), outs[:5]):
        ref[...] = val.reshape(ref.shape)
    u = outs[5]
    for s in range(D_S5 // 128):
        us = u[:, s * 128:(s + 1) * 128]
        if cb == 1:
            u_ref[s] = us
        else:
            for b in range(nb):
                u_ref[s, pl.ds(b, cb, stride=nb), :] = us[b * cb:(b + 1) * cb]


def _in_proj_steps(x, gpre, w_p, wgk_p, bgk, steps, cb):
    nb, l, _ = x.shape
    nc = l // steps
    widths = (256, 256, 512, 512, 256)
    outs = pl.pallas_call(
        functools.partial(_in_proj_steps_kernel, nb=nb, cb=cb),
        grid=(nc // cb, steps),
        in_specs=[_step_spec(nb, nc, cb, D_MODEL), _const_spec((1, D_MODEL)),
                  _const_spec((D_MODEL, PROJ_W)), _const_spec((128, GLA_KDIM)), _const_spec((1, GLA_KDIM))],
        out_specs=[_step_spec(nb, nc, cb, w) for w in widths]
                  + [pl.BlockSpec((None, D_S5 // 128, cb * nb, 128), lambda i, t: (t, 0, i, 0))],
        out_shape=[jax.ShapeDtypeStruct(_step_shape(nb, nc, steps, w), F32) for w in widths]
                  + [jax.ShapeDtypeStruct((steps, D_S5 // 128, nc * nb, 128), F32)],
        compiler_params=pltpu.CompilerParams(dimension_semantics=("parallel", "parallel"),
                                             vmem_limit_bytes=VMEM_LIMIT),
        name="in_proj_steps",
    )(_step_view(x, nb, nc, steps), gpre, w_p, wgk_p, bgk)
    return [o.reshape(nb, l, w) for o, w in zip(outs[:5], widths)] + [outs[5]]


def _gla_kernel(q_ref, k_ref, v_ref, lg_ref, s0_ref, o_ref, s_ref, *, bb, chunk):
    r = bb * chunk
    shift = chunk.bit_length() - 1

    @pl.when(pl.program_id(1) == 0)
    def _():
        s_ref[...] = s0_ref[...]

    ri = lax.broadcasted_iota(jnp.int32, (r, r), 0)
    ci = lax.broadcasted_iota(jnp.int32, (r, r), 1)
    same = (ri >> shift) == (ci >> shift)
    causal = same & (ri >= ci)
    tri_bf = jnp.where(causal, 1.0, 0.0).astype(BF16)
    all_bf = jnp.where(same, 1.0, 0.0).astype(BF16)
    lane_head = lax.broadcasted_iota(jnp.int32, (r, GLA_KDIM), 1) >> 6
    eye = (lax.broadcasted_iota(jnp.int32, (GLA_KDIM, GLA_KDIM), 0)
           == lax.broadcasted_iota(jnp.int32, (GLA_KDIM, GLA_KDIM), 1))

    lg = lg_ref[...].reshape(r, GLA_KDIM)
    lg_hi = lg.astype(BF16)
    lg_lo = (lg - lg_hi.astype(F32)).astype(BF16)
    b = _dot(tri_bf, lg_hi) + _dot(tri_bf, lg_lo)
    bl = _dot(all_bf, lg_hi) + _dot(all_bf, lg_lo)
    q = q_ref[...].reshape(r, GLA_KDIM)
    k = k_ref[...].reshape(r, GLA_KDIM)
    v = v_ref[...].reshape(r, D_GLA)
    qd = q * (GLA_DK ** -0.5) * jnp.exp(b)
    ki = (k * jnp.exp(-b)).astype(BF16)
    ke = k * jnp.exp(bl - b)
    dl = jnp.exp(bl)

    qd_h, ke_h, v_h, o_intra = [], [], [], []
    for h in range(GLA_HEADS):
        hm = lane_head == h
        qd_h.append(jnp.where(hm, qd, 0.0).astype(BF16))
        ke_h.append(jnp.where(hm, ke, 0.0).astype(BF16))
        v_h.append(v[:, h * GLA_DV:(h + 1) * GLA_DV].astype(BF16))
        att = jnp.where(causal, _dot_nt(qd_h[h], ki), 0.0).astype(BF16)
        o_intra.append(_dot(att, v_h[h]))

    for i in range(bb):
        rows = slice(i * chunk, (i + 1) * chunk)
        s_old = s_ref[i]
        s_bf = s_old.astype(BF16)
        last = dl[(i + 1) * chunk - 1:(i + 1) * chunk, :]
        dcol = jnp.sum(jnp.where(eye, last, 0.0), axis=1, keepdims=True)
        acc = dcol * s_old
        for h in range(GLA_HEADS):
            o_ref[i, :, h * GLA_DV:(h + 1) * GLA_DV] = o_intra[h][rows] + _dot(qd_h[h][rows], s_bf)
            acc = acc + _dot_tn(ke_h[h][rows], v_h[h][rows])
        s_ref[i] = acc


def _gla(q, k, v, lg, s0, bb, chunk):
    b, l, _ = q.shape
    blk = lambda w: pl.BlockSpec((bb, chunk, w), lambda i, c: (i, c, 0))
    sspec = pl.BlockSpec((bb, GLA_KDIM, GLA_DV), lambda i, c: (i, 0, 0))
    return pl.pallas_call(
        functools.partial(_gla_kernel, bb=bb, chunk=chunk),
        grid=(b // bb, l // chunk),
        in_specs=[blk(GLA_KDIM), blk(GLA_KDIM), blk(D_GLA), blk(GLA_KDIM), sspec],
        out_specs=[blk(D_GLA), sspec],
        out_shape=[jax.ShapeDtypeStruct((b, l, D_GLA), F32),
                   jax.ShapeDtypeStruct((b, GLA_KDIM, GLA_DV), F32)],
        compiler_params=pltpu.CompilerParams(dimension_semantics=("parallel", "arbitrary"),
                                             vmem_limit_bytes=VMEM_LIMIT),
        name=f"gla_c{chunk}",
    )(q, k, v, lg, s0)


SLAB_GROUPS = 128 // S5_GROUP


def _lane_block_transpose(a):
    a = list(a)
    blk = lax.broadcasted_iota(jnp.int32, a[0].shape, 1) >> 4
    for d in (4, 2, 1):
        upper = (blk & d) != 0
        for r in range(8):
            if r & d:
                continue
            lo, hi = a[r], a[r + d]
            a[r] = jnp.where(upper, pltpu.roll(hi, 16 * d, 1), lo)
            a[r + d] = jnp.where(upper, hi, pltpu.roll(lo, 128 - 16 * d, 1))
    return a


def _s5_kernel(z_ref, mi_ref, minr_ref, mini_ref, mor_ref, moi_ref, ar_ref, ai_ref, dsk_ref,
               h0r_ref, h0i_ref, y_ref, hfr_ref, hfi_ref, vr_s, vi_s, hr_s, hi_s, *, steps, batch):
    gs = SLAB_GROUPS
    n_blocks = z_ref.shape[1] // batch
    halves = steps // 8

    @pl.when(pl.program_id(1) == 0)
    def _():
        hfr_ref[...] = h0r_ref[...]
        hfi_ref[...] = h0i_ref[...]

    cols = [_lane_block_transpose([z_ref[hf * 8 + t] for t in range(8)]) for hf in range(halves)]
    uf = [cols[0][g] if halves == 1 else jnp.concatenate([c[g] for c in cols], axis=1) for g in range(gs)]
    ub = [x.astype(BF16) for x in uf]
    for g in range(gs):
        vr_s[g] = _dot(ub[g], minr_ref[g])
        vi_s[g] = _dot(ub[g], mini_ref[g])
    ar = [ar_ref[g] for g in range(gs)]
    ai = [ai_ref[g] for g in range(gs)]

    def body(c, carry):
        rows = pl.ds(pl.multiple_of(c * batch, batch), batch)
        new = []
        for g in range(gs):
            hr, hi = carry[2 * g], carry[2 * g + 1]
            hr_s[g, rows, :] = hr
            hi_s[g, rows, :] = hi
            new.append(ar[g] * hr - ai[g] * hi + vr_s[g, rows, :])
            new.append(ar[g] * hi + ai[g] * hr + vi_s[g, rows, :])
        return tuple(new)

    init = tuple(ref[g] for g in range(gs) for ref in (hfr_ref, hfi_ref))
    fin = lax.fori_loop(0, n_blocks, body, init)
    ys = []
    for g in range(gs):
        hfr_ref[g] = fin[2 * g]
        hfi_ref[g] = fin[2 * g + 1]
        ys.append(_dot(ub[g], mi_ref[g]) + _dot(hr_s[g].astype(BF16), mor_ref[g])
                  + _dot(hi_s[g].astype(BF16), moi_ref[g]) + uf[g] * dsk_ref[g])
    for hf in range(halves):
        back = _lane_block_transpose([y[:, hf * 128:(hf + 1) * 128] for y in ys])
        for t in range(8):
            y_ref[hf * 8 + t] = back[t]


def _s5(z, m, d_skip, h0_re, h0_im, batch, rblk):
    steps, slabs, r, _ = z.shape
    n, gs, w = S5_STATE, SLAB_GROUPS, steps * S5_GROUP
    zspec = pl.BlockSpec((steps, None, rblk, 128), lambda s, i: (0, s, i, 0))
    per_s = lambda a, b_: pl.BlockSpec((gs, a, b_), lambda s, i: (s, 0, 0))
    return pl.pallas_call(
        functools.partial(_s5_kernel, steps=steps, batch=batch),
        grid=(slabs, r // rblk),
        in_specs=[zspec, per_s(w, w), per_s(w, n), per_s(w, n), per_s(n, w), per_s(n, w),
                  per_s(1, n), per_s(1, n), per_s(1, w), per_s(batch, n), per_s(batch, n)],
        out_specs=[zspec, per_s(batch, n), per_s(batch, n)],
        out_shape=[jax.ShapeDtypeStruct(z.shape, F32),
                   jax.ShapeDtypeStruct((S5_GROUPS, batch, n), F32),
                   jax.ShapeDtypeStruct((S5_GROUPS, batch, n), F32)],
        scratch_shapes=[pltpu.VMEM((gs, rblk, n), F32)] * 4,
        compiler_params=pltpu.CompilerParams(dimension_semantics=("parallel", "arbitrary"),
                                             vmem_limit_bytes=VMEM_LIMIT),
        name=f"s5_w{w}",
    )(z, m["m_intra"], m["m_in_re"], m["m_in_im"], m["m_out_re"], m["m_out_im"],
      m["a_re"], m["a_im"], jnp.tile(d_skip.reshape(S5_GROUPS, 1, S5_GROUP), (1, 1, steps)), h0_re, h0_im)


def _mix_kernel(x_ref, o_ref, g_ref, y_ref, gn_ref, wglu_ref, s5n_ref,
                woa_ref, wob_ref, gpost_ref, out_ref, *, nb, cb):
    rows = nb * cb
    o = o_ref[...].reshape(rows, D_GLA)
    gn = gn_ref[...]
    heads = []
    for h in range(GLA_HEADS):
        heads.append(_rms(o[:, h * GLA_DV:(h + 1) * GLA_DV], gn))
    og = jnp.concatenate(heads, axis=1) * jax.nn.silu(g_ref[...].reshape(rows, D_GLA))
    slabs = []
    for s in range(D_S5 // 128):
        if cb == 1:
            slabs.append(y_ref[s])
        else:
            slabs.append(jnp.concatenate([y_ref[s, pl.ds(b, cb, stride=nb), :] for b in range(nb)], axis=0))
    y = jax.nn.gelu(jnp.concatenate(slabs, axis=1))
    y = y * jax.nn.sigmoid(_dot(y.astype(BF16), wglu_ref[...]))
    y = _rms(y, s5n_ref[...])
    mix = _dot(og.astype(BF16), woa_ref[...]) + _dot(y.astype(BF16), wob_ref[...])
    out = x_ref[...].reshape(rows, D_MODEL) + _rms(mix, gpost_ref[...])
    out_ref[...] = out.reshape(out_ref.shape)


def _mix(x, o, g, y, gn, wglu, s5n, woa, wob, gpost, steps, cb):
    nb, l, _ = x.shape
    nc = l // steps
    out = pl.pallas_call(
        functools.partial(_mix_kernel, nb=nb, cb=cb),
        grid=(nc // cb, steps),
        in_specs=[_step_spec(nb, nc, cb, D_MODEL), _step_spec(nb, nc, cb, D_GLA), _step_spec(nb, nc, cb, D_GLA),
                  pl.BlockSpec((None, D_S5 // 128, cb * nb, 128), lambda i, t: (t, 0, i, 0)),
                  _const_spec((1, GLA_DV)), _const_spec((D_S5, D_S5)),
                  _const_spec((1, D_S5)), _const_spec((D_GLA, D_MODEL)), _const_spec((D_S5, D_MODEL)),
                  _const_spec((1, D_MODEL))],
        out_specs=_step_spec(nb, nc, cb, D_MODEL),
        out_shape=jax.ShapeDtypeStruct(_step_shape(nb, nc, steps, D_MODEL), F32),
        compiler_params=pltpu.CompilerParams(dimension_semantics=("parallel", "parallel"),
                                             vmem_limit_bytes=VMEM_LIMIT),
        name="mix",
    )(_step_view(x, nb, nc, steps), _step_view(o, nb, nc, steps), _step_view(g, nb, nc, steps), y,
      gn, wglu, s5n, woa, wob, gpost)
    return out.reshape(nb * l, D_MODEL)


def _ffn_kernel(x_ref, gpre_ref, wg_ref, wu_ref, wd_ref, gpost_ref, out_ref):
    x = x_ref[...]
    h = _rms(x, gpre_ref[...]).astype(BF16)
    acc = jnp.zeros(x.shape, F32)
    for c in range(D_FF // FF_CHUNK):
        cols = slice(c * FF_CHUNK, (c + 1) * FF_CHUNK)
        act = jax.nn.silu(_dot(h, wg_ref[:, cols])) * _dot(h, wu_ref[:, cols])
        acc = acc + _dot(act.astype(BF16), wd_ref[cols, :])
    out_ref[...] = x + _rms(acc, gpost_ref[...])


def _ffn(x, gpre, wg, wu, wd, gpost, tm):
    t = x.shape[0]
    row = pl.BlockSpec((tm, D_MODEL), lambda i: (i, 0))
    return pl.pallas_call(
        _ffn_kernel,
        grid=(t // tm,),
        in_specs=[row, _const_spec((1, D_MODEL)), _const_spec((D_MODEL, D_FF)),
                  _const_spec((D_MODEL, D_FF)), _const_spec((D_FF, D_MODEL)), _const_spec((1, D_MODEL))],
        out_specs=row,
        out_shape=jax.ShapeDtypeStruct((t, D_MODEL), F32),
        compiler_params=pltpu.CompilerParams(dimension_semantics=("parallel",),
                                             vmem_limit_bytes=VMEM_LIMIT),
        name="ffn",
    )(x, gpre, wg, wu, wd, gpost)


def kernel(x_prompt, x_sample, state_gla, state_s5_re, state_s5_im, meta_tokens, g_pre_mix, w_in, w_gk2, b_gk, gla_norm, s5_a_re, s5_a_im, s5_b_re, s5_b_im, s5_c_re, s5_c_im, s5_d, s5_log_dt, w_s5_glu, s5_norm, w_o, g_post_mix, g_pre_ffn, w_gate, w_up, w_down, g_post_ffn):
    assert g_pre_mix.shape[0] == 1, "single-layer step"
    bp, seq_p, _ = x_prompt.shape
    bs, seq_s, _ = x_sample.shape
    row = lambda t: t[0].reshape(1, -1)

    w = w_in[0]
    c3, c4 = 1536, 1536 + GATE_RANK
    w_p = jnp.concatenate([w[:, :c3], w[:, c4:], w[:, c3:c4],
                           jnp.zeros((D_MODEL, 128 - GATE_RANK), F32)], axis=1).astype(BF16)
    wgk_p = jnp.concatenate([w_gk2[0], jnp.zeros((128 - GATE_RANK, GLA_KDIM), F32)], axis=0).astype(BF16)
    wo_bf = w_o[0].astype(BF16)
    s5m = _s5_prep(s5_a_re[0], s5_a_im[0], s5_b_re[0], s5_b_im[0], s5_c_re[0], s5_c_im[0], s5_log_dt[0])
    proj_w = (row(g_pre_mix), w_p, wgk_p, row(b_gk))

    def finish(x, o, g, y, steps, cb, tm):
        x1 = _mix(x, o, g, y, row(gla_norm), w_s5_glu[0].astype(BF16), row(s5_norm),
                  wo_bf[:D_GLA], wo_bf[D_GLA:], row(g_post_mix), steps, cb)
        return _ffn(x1, row(g_pre_ffn), w_gate[0].astype(BF16), w_up[0].astype(BF16),
                    w_down[0].astype(BF16), row(g_post_ffn), tm)

    xm = jnp.broadcast_to(meta_tokens[None], (bp, N_META, D_MODEL)).reshape(bp * N_META, D_MODEL)
    q, k, v, _, lg, u = _in_proj(xm, *proj_w, bp * N_META)
    r3 = lambda t: t.reshape(bp, N_META, t.shape[-1])
    _, s_meta = _gla(r3(q), r3(k), r3(v), r3(lg), jnp.zeros((bp, GLA_KDIM, GLA_DV), F32), bp, N_META)
    zm = u.reshape(bp, N_META, D_S5 // 128, 128).transpose(1, 2, 0, 3)
    zh = jnp.zeros((S5_GROUPS, bp, S5_STATE), F32)
    _, hm_re, hm_im = _s5(zm, s5m(S5_BLOCK), s5_d[0], zh, zh, bp, bp)

    q, k, v, g, lg, z = _in_proj_steps(x_prompt, *proj_w, S5_BLOCK, 64)
    o, s_p = _gla(q, k, v, lg, s_meta, 1, GLA_CHUNK)
    y, hp_re, hp_im = _s5(z, s5m(S5_BLOCK), s5_d[0], hm_re, hm_im, bp, 256)
    y_prompt = finish(x_prompt, o, g, y, S5_BLOCK, 64, 256).reshape(bp, seq_p, D_MODEL)

    to_g = lambda t: jnp.swapaxes(t[0], 0, 1)
    q, k, v, g, lg, z = _in_proj_steps(x_sample, *proj_w, seq_s, 1)
    o, s_s = _gla(q, k, v, lg, state_gla[0].reshape(bs, GLA_KDIM, GLA_DV), 16, seq_s)
    y, hs_re, hs_im = _s5(z, s5m(seq_s), s5_d[0], to_g(state_s5_re), to_g(state_s5_im), bs, bs)
    y_sample = finish(x_sample, o, g, y, seq_s, 1, 256).reshape(bs, seq_s, D_MODEL)

    gla_out = lambda s, b: s.reshape(1, b, GLA_HEADS, GLA_DK, GLA_DV)
    s5_out = lambda h: jnp.swapaxes(h, 0, 1)[None]
    return (y_prompt, y_sample, gla_out(s_p, bp), s5_out(hp_re), s5_out(hp_im),
            gla_out(s_s, bs), s5_out(hs_re), s5_out(hs_im))
```

```python
import functools

import jax
import jax.numpy as jnp
from jax import lax
from jax.experimental import pallas as pl
from jax.experimental.pallas import tpu as pltpu

F32 = jnp.float32
BF16 = jnp.bfloat16

D_MODEL = 1024
D_GLA = 512
GLA_HEADS = 4
GLA_DV = 128
GLA_DK = 64
GLA_KDIM = 256
GATE_RANK = 16
GATE_NORM = 16.0
GLA_CHUNK = 64
D_S5 = 512
S5_GROUP = 16
S5_GROUPS = 32
S5_STATE = 64
N_META = 16
D_FF = 2816
EPS = 1e-6
LANES = 128
S5_BLOCK = 16
SLAB_GROUPS = LANES // S5_GROUP
PROJ_W = 2176
FF_CHUNK = 256
VMEM_LIMIT = 48 * 1024 * 1024


def _rms(x, g):
    return x * lax.rsqrt(jnp.mean(x * x, axis=-1, keepdims=True) + EPS) * g


def _dot(a, b):
    return jnp.dot(a, b, preferred_element_type=F32)


def _dot_nt(a, b):
    return lax.dot_general(a, b, (((1,), (1,)), ((), ())), preferred_element_type=F32)


def _dot_tn(a, b):
    return lax.dot_general(a, b, (((0,), (0,)), ((), ())), preferred_element_type=F32)


def _const_spec(shape):
    zeros = (0,) * len(shape)
    return pl.BlockSpec(shape, lambda *_: zeros)


def _s5_prep_kernel(are_ref, aim_ref, ldt_ref, btr_ref, bti_ref, cr_ref, ci_ref, ctr_ref, cti_ref,
                    k_ref, er_ref, ei_ref, mor_ref, moi_ref, p1r_ref, p1i_ref):
    lam_re = jnp.minimum(are_ref[0], -1e-4)
    lam_im = aim_ref[0]
    dt = jnp.exp(ldt_ref[0])
    mag = jnp.exp(lam_re * dt)
    ang = lam_im * dt
    ab_re = mag * jnp.cos(ang)
    ab_im = mag * jnp.sin(ang)
    den = lam_re * lam_re + lam_im * lam_im
    nr, ni = ab_re - 1.0, ab_im
    f_re = (nr * lam_re + ni * lam_im) / den
    f_im = (ni * lam_re - nr * lam_im) / den
    bt_re, bt_im = btr_ref[0], bti_ref[0]
    bb_re = f_re * bt_re - f_im * bt_im
    bb_im = f_re * bt_im + f_im * bt_re
    lag = (lax.broadcasted_iota(jnp.int32, bt_re.shape, 1) >> 4).astype(F32)
    pm = jnp.exp(lag * (lam_re * dt))
    pa = lag * ang
    pk_re = pm * jnp.cos(pa)
    pk_im = pm * jnp.sin(pa)
    e_re = pk_re * bb_re - pk_im * bb_im
    e_im = pk_re * bb_im + pk_im * bb_re
    p1_re = pk_re * ab_re - pk_im * ab_im
    p1_im = pk_re * ab_im + pk_im * ab_re
    ct_re, ct_im = ctr_ref[0], cti_ref[0]
    mor_ref[0] = ct_re * p1_re - ct_im * p1_im
    moi_ref[0] = -(ct_re * p1_im + ct_im * p1_re)
    er_ref[0] = e_re
    ei_ref[0] = e_im
    p1r_ref[0] = p1_re
    p1i_ref[0] = p1_im
    hi = lax.Precision.HIGHEST
    k_ref[0] = (jnp.dot(cr_ref[0], e_re, precision=hi, preferred_element_type=F32)
                - jnp.dot(ci_ref[0], e_im, precision=hi, preferred_element_type=F32))


def _s5_prep(a_re, a_im, b_re, b_im, c_re, c_im, log_dt):
    g, n, j = S5_GROUPS, S5_STATE, S5_GROUP
    w = S5_BLOCK * j
    col = lambda t: t.reshape(g, n, 1)
    bt = lambda t: jnp.tile(t, (1, 1, S5_BLOCK))
    ct = lambda t: jnp.tile(jnp.swapaxes(t, 1, 2), (1, 1, S5_BLOCK))
    big = pl.BlockSpec((1, n, w), lambda i: (i, 0, 0))
    colspec = pl.BlockSpec((1, n, 1), lambda i: (i, 0, 0))
    cspec = pl.BlockSpec((1, j, n), lambda i: (i, 0, 0))
    outs = pl.pallas_call(
        _s5_prep_kernel,
        grid=(g,),
        in_specs=[colspec, colspec, pl.BlockSpec((1, 1, 1), lambda i: (i, 0, 0)),
                  big, big, cspec, cspec, big, big],
        out_specs=[pl.BlockSpec((1, j, w), lambda i: (i, 0, 0))] + [big] * 6,
        out_shape=[jax.ShapeDtypeStruct((g, j, w), F32)] + [jax.ShapeDtypeStruct((g, n, w), F32)] * 6,
        name="s5_prep",
    )(col(a_re), col(a_im), log_dt.reshape(g, 1, 1), bt(b_re), bt(b_im), c_re, c_im, ct(c_re), ct(c_im))
    kk, e_re, e_im, mo_re, mo_im, p1_re, p1_im = outs

    q = S5_BLOCK
    kp = kk.reshape(g, j, q, j).transpose(0, 2, 3, 1)
    s_idx = jnp.arange(q)[:, None]
    t_idx = jnp.arange(q)[None, :]
    lag = t_idx - s_idx
    toe = kp[:, jnp.clip(lag, 0, q - 1)]
    toe = jnp.where((lag >= 0)[None, :, :, None, None], toe, 0.0)
    toe = toe.transpose(0, 1, 3, 2, 4)
    pad_n = lambda t, axis: jnp.pad(t, [(0, LANES - n) if a == axis else (0, 0) for a in range(3)])

    def mats(steps):
        m_intra = toe[:, :steps, :, :steps, :].reshape(g, steps * j, steps * j).astype(BF16)
        rev = lambda e: pad_n(e.reshape(g, n, q, j)[:, :, steps - 1::-1, :]
                              .transpose(0, 2, 3, 1).reshape(g, steps * j, n), 2).astype(BF16)
        a_re_s = pad_n(p1_re[:, :, (steps - 1) * j].reshape(g, 1, n), 2)
        a_im_s = pad_n(p1_im[:, :, (steps - 1) * j].reshape(g, 1, n), 2)
        return dict(m_intra=m_intra, m_in_re=rev(e_re), m_in_im=rev(e_im),
                    m_out_re=pad_n(mo_re[:, :, :steps * j], 1).astype(BF16),
                    m_out_im=pad_n(mo_im[:, :, :steps * j], 1).astype(BF16),
                    a_re=a_re_s, a_im=a_im_s)

    return mats


def _in_proj_kernel(x_ref, gpre_ref, w_ref, wgk_ref, bgk_ref,
                    q_ref, k_ref, v_ref, g_ref, lg_ref, u_ref):
    h = _rms(x_ref[...], gpre_ref[...])
    proj = _dot(h.astype(BF16), w_ref[...])
    q_ref[...] = proj[:, 0:256]
    k_ref[...] = proj[:, 256:512]
    v_ref[...] = proj[:, 512:1024]
    g_ref[...] = proj[:, 1024:1536]
    u_ref[...] = proj[:, 1536:2048]
    z = _dot(proj[:, 2048:PROJ_W].astype(BF16), wgk_ref[...]) + bgk_ref[...]
    lg_ref[...] = jax.nn.log_sigmoid(z) * (1.0 / GATE_NORM)


def _in_proj(x, gpre, w_p, wgk_p, bgk, tm):
    t = x.shape[0]
    row = lambda w: pl.BlockSpec((tm, w), lambda i: (i, 0))
    widths = (256, 256, 512, 512, 256, 512)
    return pl.pallas_call(
        _in_proj_kernel,
        grid=(t // tm,),
        in_specs=[row(D_MODEL), _const_spec((1, D_MODEL)), _const_spec((D_MODEL, PROJ_W)),
                  _const_spec((LANES, GLA_KDIM)), _const_spec((1, GLA_KDIM))],
        out_specs=[row(w) for w in widths],
        out_shape=[jax.ShapeDtypeStruct((t, w), F32) for w in widths],
        compiler_params=pltpu.CompilerParams(dimension_semantics=("parallel",),
                                             vmem_limit_bytes=VMEM_LIMIT),
        name="in_proj",
    )(x, gpre, w_p, wgk_p, bgk)


def _gla_kernel(q_ref, k_ref, v_ref, lg_ref, s0_ref, o_ref, s_ref, *, bb, chunk):
    r = bb * chunk
    shift = chunk.bit_length() - 1

    @pl.when(pl.program_id(1) == 0)
    def _():
        s_ref[...] = s0_ref[...]

    ri = lax.broadcasted_iota(jnp.int32, (r, r), 0)
    ci = lax.broadcasted_iota(jnp.int32, (r, r), 1)
    same = (ri >> shift) == (ci >> shift)
    causal = same & (ri >= ci)
    tri_bf = jnp.where(causal, 1.0, 0.0).astype(BF16)
    all_bf = jnp.where(same, 1.0, 0.0).astype(BF16)
    lane_head = lax.broadcasted_iota(jnp.int32, (r, GLA_KDIM), 1) >> 6
    eye = (lax.broadcasted_iota(jnp.int32, (GLA_KDIM, GLA_KDIM), 0)
           == lax.broadcasted_iota(jnp.int32, (GLA_KDIM, GLA_KDIM), 1))

    lg = lg_ref[...].reshape(r, GLA_KDIM)
    lg_hi = lg.astype(BF16)
    lg_lo = (lg - lg_hi.astype(F32)).astype(BF16)
    b = _dot(tri_bf, lg_hi) + _dot(tri_bf, lg_lo)
    bl = _dot(all_bf, lg_hi) + _dot(all_bf, lg_lo)
    q = q_ref[...].reshape(r, GLA_KDIM)
    k = k_ref[...].reshape(r, GLA_KDIM)
    v = v_ref[...].reshape(r, D_GLA)
    qd = q * (GLA_DK ** -0.5) * jnp.exp(b)
    ki = (k * jnp.exp(-b)).astype(BF16)
    ke = k * jnp.exp(bl - b)
    dl = jnp.exp(bl)

    qd_h, ke_h, v_h, o_intra = [], [], [], []
    for h in range(GLA_HEADS):
        hm = lane_head == h
        qd_h.append(jnp.where(hm, qd, 0.0).astype(BF16))
        ke_h.append(jnp.where(hm, ke, 0.0).astype(BF16))
        v_h.append(v[:, h * GLA_DV:(h + 1) * GLA_DV].astype(BF16))
        att = jnp.where(causal, _dot_nt(qd_h[h], ki), 0.0).astype(BF16)
        o_intra.append(_dot(att, v_h[h]))

    for i in range(bb):
        rows = slice(i * chunk, (i + 1) * chunk)
        s_old = s_ref[i]
        s_bf = s_old.astype(BF16)
        last = dl[(i + 1) * chunk - 1:(i + 1) * chunk, :]
        dcol = jnp.sum(jnp.where(eye, last, 0.0), axis=1, keepdims=True)
        acc = dcol * s_old
        for h in range(GLA_HEADS):
            o_ref[i, :, h * GLA_DV:(h + 1) * GLA_DV] = o_intra[h][rows] + _dot(qd_h[h][rows], s_bf)
            acc = acc + _dot_tn(ke_h[h][rows], v_h[h][rows])
        s_ref[i] = acc


def _gla(q, k, v, lg, s0, bb, chunk):
    b, l, _ = q.shape
    blk = lambda w: pl.BlockSpec((bb, chunk, w), lambda i, c: (i, c, 0))
    sspec = pl.BlockSpec((bb, GLA_KDIM, GLA_DV), lambda i, c: (i, 0, 0))
    return pl.pallas_call(
        functools.partial(_gla_kernel, bb=bb, chunk=chunk),
        grid=(b // bb, l // chunk),
        in_specs=[blk(GLA_KDIM), blk(GLA_KDIM), blk(D_GLA), blk(GLA_KDIM), sspec],
        out_specs=[blk(D_GLA), sspec],
        out_shape=[jax.ShapeDtypeStruct((b, l, D_GLA), F32),
                   jax.ShapeDtypeStruct((b, GLA_KDIM, GLA_DV), F32)],
        compiler_params=pltpu.CompilerParams(dimension_semantics=("parallel", "arbitrary"),
                                             vmem_limit_bytes=VMEM_LIMIT),
        name=f"gla_c{chunk}",
    )(q, k, v, lg, s0)


def _lane_block_transpose(a):
    a = list(a)
    blk = lax.broadcasted_iota(jnp.int32, a[0].shape, 1) >> 4
    for d in (4, 2, 1):
        upper = (blk & d) != 0
        for r in range(8):
            if r & d:
                continue
            lo, hi = a[r], a[r + d]
            a[r] = jnp.where(upper, pltpu.roll(hi, 16 * d, 1), lo)
            a[r + d] = jnp.where(upper, hi, pltpu.roll(lo, LANES - 16 * d, 1))
    return a


def _s5_kernel(u_ref, mi_ref, minr_ref, mini_ref, mor_ref, moi_ref, ar_ref, ai_ref, dsk_ref,
               h0r_ref, h0i_ref, y_ref, hfr_ref, hfi_ref, vr_s, vi_s, hr_s, hi_s, *, steps, nb, cb):
    gs = SLAB_GROUPS
    batch = cb if nb == 1 else nb
    n_blocks = (nb * cb) // batch
    halves = steps // 8
    interleave = nb > 1 and cb > 1

    @pl.when(pl.program_id(1) == 0)
    def _():
        hfr_ref[...] = h0r_ref[...]
        hfi_ref[...] = h0i_ref[...]

    def step_rows(ref, t):
        parts = [ref[b, pl.ds(t, cb, stride=steps), :] for b in range(nb)]
        return parts[0] if nb == 1 else jnp.concatenate(parts, axis=0)

    def to_scan_order(ref, g, val):
        if not interleave:
            ref[g] = val
        else:
            for b in range(nb):
                ref[g, pl.ds(b, cb, stride=nb), :] = val[b * cb:(b + 1) * cb]

    def from_scan_order(ref, g):
        if not interleave:
            return ref[g]
        return jnp.concatenate([ref[g, pl.ds(b, cb, stride=nb), :] for b in range(nb)], axis=0)

    cols = [_lane_block_transpose([step_rows(u_ref, hf * 8 + t) for t in range(8)]) for hf in range(halves)]
    uf = [cols[0][g] if halves == 1 else jnp.concatenate([c[g] for c in cols], axis=1) for g in range(gs)]
    ub = [x.astype(BF16) for x in uf]
    for g in range(gs):
        to_scan_order(vr_s, g, _dot(ub[g], minr_ref[g]))
        to_scan_order(vi_s, g, _dot(ub[g], mini_ref[g]))
    ar = [ar_ref[g] for g in range(gs)]
    ai = [ai_ref[g] for g in range(gs)]

    def body(c, carry):
        rows = pl.ds(pl.multiple_of(c * batch, batch), batch)
        new = []
        for g in range(gs):
            hr, hi = carry[2 * g], carry[2 * g + 1]
            hr_s[g, rows, :] = hr
            hi_s[g, rows, :] = hi
            new.append(ar[g] * hr - ai[g] * hi + vr_s[g, rows, :])
            new.append(ar[g] * hi + ai[g] * hr + vi_s[g, rows, :])
        return tuple(new)

    init = tuple(ref[g] for g in range(gs) for ref in (hfr_ref, hfi_ref))
    fin = lax.fori_loop(0, n_blocks, body, init)
    ys = []
    for g in range(gs):
        hfr_ref[g] = fin[2 * g]
        hfi_ref[g] = fin[2 * g + 1]
        ys.append(_dot(ub[g], mi_ref[g]) + _dot(from_scan_order(hr_s, g).astype(BF16), mor_ref[g])
                  + _dot(from_scan_order(hi_s, g).astype(BF16), moi_ref[g]) + uf[g] * dsk_ref[g])
    for hf in range(halves):
        back = _lane_block_transpose([y[:, hf * LANES:(hf + 1) * LANES] for y in ys])
        for t in range(8):
            for b in range(nb):
                y_ref[b, pl.ds(hf * 8 + t, cb, stride=steps), :] = back[t][b * cb:(b + 1) * cb]


def _s5(u, m, d_skip, h0_re, h0_im, steps, cb):
    nb, l, _ = u.shape
    batch = cb if nb == 1 else nb
    n, gs, w = LANES, SLAB_GROUPS, steps * S5_GROUP
    uspec = pl.BlockSpec((nb, cb * steps, LANES), lambda s, i: (0, i, s))
    per_s = lambda a, b_: pl.BlockSpec((gs, a, b_), lambda s, i: (s, 0, 0))
    if nb == 1:
        hspec = pl.BlockSpec((gs, batch, n), lambda s, i: (s, i, 0))
    else:
        hspec = per_s(batch, n)
    return pl.pallas_call(
        functools.partial(_s5_kernel, steps=steps, nb=nb, cb=cb),
        grid=(D_S5 // LANES, l // (cb * steps)),
        in_specs=[uspec, per_s(w, w), per_s(w, n), per_s(w, n), per_s(n, w), per_s(n, w),
                  per_s(1, n), per_s(1, n), per_s(1, w), hspec, hspec],
        out_specs=[uspec, hspec, hspec],
        out_shape=[jax.ShapeDtypeStruct(u.shape, F32),
                   jax.ShapeDtypeStruct(h0_re.shape, F32),
                   jax.ShapeDtypeStruct(h0_im.shape, F32)],
        scratch_shapes=[pltpu.VMEM((gs, nb * cb, n), F32)] * 4,
        compiler_params=pltpu.CompilerParams(dimension_semantics=("parallel", "arbitrary"),
                                             vmem_limit_bytes=VMEM_LIMIT),
        name=f"s5_w{w}",
    )(u, m["m_intra"], m["m_in_re"], m["m_in_im"], m["m_out_re"], m["m_out_im"],
      m["a_re"], m["a_im"], jnp.tile(d_skip.reshape(S5_GROUPS, 1, S5_GROUP), (1, 1, steps)), h0_re, h0_im)


def _mix_kernel(x_ref, o_ref, g_ref, y_ref, gn_ref, wglu_ref, s5n_ref,
                woa_ref, wob_ref, gpost_ref, out_ref):
    o = o_ref[...]
    gn = gn_ref[...]
    heads = []
    for h in range(GLA_HEADS):
        heads.append(_rms(o[:, h * GLA_DV:(h + 1) * GLA_DV], gn))
    og = jnp.concatenate(heads, axis=1) * jax.nn.silu(g_ref[...])
    y = jax.nn.gelu(y_ref[...])
    y = y * jax.nn.sigmoid(_dot(y.astype(BF16), wglu_ref[...]))
    y = _rms(y, s5n_ref[...])
    mix = _dot(og.astype(BF16), woa_ref[...]) + _dot(y.astype(BF16), wob_ref[...])
    out_ref[...] = x_ref[...] + _rms(mix, gpost_ref[...])


def _mix(x, o, g, y, gn, wglu, s5n, woa, wob, gpost, tm):
    t = x.shape[0]
    row = lambda w: pl.BlockSpec((tm, w), lambda i: (i, 0))
    return pl.pallas_call(
        _mix_kernel,
        grid=(t // tm,),
        in_specs=[row(D_MODEL), row(D_GLA), row(D_GLA), row(D_S5),
                  _const_spec((1, GLA_DV)), _const_spec((D_S5, D_S5)),
                  _const_spec((1, D_S5)), _const_spec((D_GLA, D_MODEL)), _const_spec((D_S5, D_MODEL)),
                  _const_spec((1, D_MODEL))],
        out_specs=row(D_MODEL),
        out_shape=jax.ShapeDtypeStruct((t, D_MODEL), F32),
        compiler_params=pltpu.CompilerParams(dimension_semantics=("parallel",),
                                             vmem_limit_bytes=VMEM_LIMIT),
        name="mix",
    )(x, o, g, y, gn, wglu, s5n, woa, wob, gpost)


def _ffn_kernel(x_ref, gpre_ref, wg_ref, wu_ref, wd_ref, gpost_ref, out_ref):
    x = x_ref[...]
    h = _rms(x, gpre_ref[...]).astype(BF16)
    acc = jnp.zeros(x.shape, F32)
    for c in range(D_FF // FF_CHUNK):
        cols = slice(c * FF_CHUNK, (c + 1) * FF_CHUNK)
        act = jax.nn.silu(_dot(h, wg_ref[:, cols])) * _dot(h, wu_ref[:, cols])
        acc = acc + _dot(act.astype(BF16), wd_ref[cols, :])
    out_ref[...] = x + _rms(acc, gpost_ref[...])


def _ffn(x, gpre, wg, wu, wd, gpost, tm):
    t = x.shape[0]
    row = pl.BlockSpec((tm, D_MODEL), lambda i: (i, 0))
    return pl.pallas_call(
        _ffn_kernel,
        grid=(t // tm,),
        in_specs=[row, _const_spec((1, D_MODEL)), _const_spec((D_MODEL, D_FF)),
                  _const_spec((D_MODEL, D_FF)), _const_spec((D_FF, D_MODEL)), _const_spec((1, D_MODEL))],
        out_specs=row,
        out_shape=jax.ShapeDtypeStruct((t, D_MODEL), F32),
        compiler_params=pltpu.CompilerParams(dimension_semantics=("parallel",),
                                             vmem_limit_bytes=VMEM_LIMIT),
        name="ffn",
    )(x, gpre, wg, wu, wd, gpost)


def kernel(x_prompt, x_sample, state_gla, state_s5_re, state_s5_im, meta_tokens, g_pre_mix, w_in, w_gk2, b_gk, gla_norm, s5_a_re, s5_a_im, s5_b_re, s5_b_im, s5_c_re, s5_c_im, s5_d, s5_log_dt, w_s5_glu, s5_norm, w_o, g_post_mix, g_pre_ffn, w_gate, w_up, w_down, g_post_ffn):
    assert g_pre_mix.shape[0] == 1, "single-layer step"
    bp, seq_p, _ = x_prompt.shape
    bs, seq_s, _ = x_sample.shape
    row = lambda t: t[0].reshape(1, -1)

    w = w_in[0]
    c3, c4 = 1536, 1536 + GATE_RANK
    w_p = jnp.concatenate([w[:, :c3], w[:, c4:], w[:, c3:c4],
                           jnp.zeros((D_MODEL, LANES - GATE_RANK), F32)], axis=1).astype(BF16)
    wgk_p = jnp.concatenate([w_gk2[0], jnp.zeros((LANES - GATE_RANK, GLA_KDIM), F32)], axis=0).astype(BF16)
    wo_bf = w_o[0].astype(BF16)
    s5m = _s5_prep(s5_a_re[0], s5_a_im[0], s5_b_re[0], s5_b_im[0], s5_c_re[0], s5_c_im[0], s5_log_dt[0])
    proj_w = (row(g_pre_mix), w_p, wgk_p, row(b_gk))
    pad_state = lambda h: jnp.pad(h, ((0, 0), (0, 0), (0, LANES - S5_STATE)))

    def finish(x, o, g, y, tm):
        x1 = _mix(x, o, g, y, row(gla_norm), w_s5_glu[0].astype(BF16), row(s5_norm),
                  wo_bf[:D_GLA], wo_bf[D_GLA:], row(g_post_mix), tm)
        return _ffn(x1, row(g_pre_ffn), w_gate[0].astype(BF16), w_up[0].astype(BF16),
                    w_down[0].astype(BF16), row(g_post_ffn), tm)

    xm = jnp.broadcast_to(meta_tokens[None], (bp, N_META, D_MODEL)).reshape(bp * N_META, D_MODEL)
    q, k, v, _, lg, u = _in_proj(xm, *proj_w, bp * N_META)
    r3 = lambda t, b, l: t.reshape(b, l, t.shape[-1])
    _, s_meta = _gla(r3(q, bp, N_META), r3(k, bp, N_META), r3(v, bp, N_META), r3(lg, bp, N_META),
                     jnp.zeros((bp, GLA_KDIM, GLA_DV), F32), bp, N_META)
    zh = jnp.zeros((S5_GROUPS, bp, LANES), F32)
    _, hm_re, hm_im = _s5(u[None], s5m(S5_BLOCK), s5_d[0], zh, zh, S5_BLOCK, bp)

    xp = x_prompt.reshape(bp * seq_p, D_MODEL)
    q, k, v, g, lg, u = _in_proj(xp, *proj_w, 512)
    o, s_p = _gla(r3(q, bp, seq_p), r3(k, bp, seq_p), r3(v, bp, seq_p), r3(lg, bp, seq_p), s_meta, 1, GLA_CHUNK)
    y, hp_re, hp_im = _s5(r3(u, bp, seq_p), s5m(S5_BLOCK), s5_d[0], hm_re, hm_im, S5_BLOCK, 32)
    y_prompt = finish(xp, o.reshape(bp * seq_p, D_GLA), g, y.reshape(bp * seq_p, D_S5), 512)

    xs = x_sample.reshape(bs * seq_s, D_MODEL)
    to_g = lambda t: pad_state(jnp.swapaxes(t[0], 0, 1))
    q, k, v, g, lg, u = _in_proj(xs, *proj_w, 512)
    o, s_s = _gla(r3(q, bs, seq_s), r3(k, bs, seq_s), r3(v, bs, seq_s), r3(lg, bs, seq_s),
                  state_gla[0].reshape(bs, GLA_KDIM, GLA_DV), 16, seq_s)
    y, hs_re, hs_im = _s5(u[None], s5m(seq_s), s5_d[0], to_g(state_s5_re), to_g(state_s5_im), seq_s, bs)
    y_sample = finish(xs, o.reshape(bs * seq_s, D_GLA), g, y[0], 512)

    gla_out = lambda s, b: s.reshape(1, b, GLA_HEADS, GLA_DK, GLA_DV)
    s5_out = lambda h: jnp.swapaxes(h[:, :, :S5_STATE], 0, 1)[None]
    return (y_prompt.reshape(bp, seq_p, D_MODEL), y_sample.reshape(bs, seq_s, D_MODEL),
            gla_out(s_p, bp), s5_out(hp_re), s5_out(hp_im),
            gla_out(s_s, bs), s5_out(hs_re), s5_out(hs_im))
```

```python
import functools

import jax
import jax.numpy as jnp
from jax import lax
from jax.experimental import pallas as pl
from jax.experimental.pallas import tpu as pltpu

F32 = jnp.float32
BF16 = jnp.bfloat16

D_MODEL = 1024
D_GLA = 512
GLA_HEADS = 4
GLA_DV = 128
GLA_DK = 64
GLA_KDIM = 256
GATE_RANK = 16
GATE_NORM = 16.0
GLA_CHUNK = 64
D_S5 = 512
S5_GROUP = 16
S5_GROUPS = 32
S5_STATE = 64
N_META = 16
D_FF = 2816
EPS = 1e-6
LANES = 128
S5_BLOCK = 16
SLAB_GROUPS = LANES // S5_GROUP
PROJ_W = 2176
FF_CHUNK = 256
VMEM_LIMIT = 48 * 1024 * 1024


def _rms(x, g):
    return x * lax.rsqrt(jnp.mean(x * x, axis=-1, keepdims=True) + EPS) * g


def _dot(a, b):
    return jnp.dot(a, b, preferred_element_type=F32)


def _dot_nt(a, b):
    return lax.dot_general(a, b, (((1,), (1,)), ((), ())), preferred_element_type=F32)


def _dot_tn(a, b):
    return lax.dot_general(a, b, (((0,), (0,)), ((), ())), preferred_element_type=F32)


def _const_spec(shape):
    zeros = (0,) * len(shape)
    return pl.BlockSpec(shape, lambda *_: zeros)


def _s5_prep_kernel(are_ref, aim_ref, ldt_ref, btr_ref, bti_ref, cr_ref, ci_ref, ctr_ref, cti_ref,
                    k_ref, er_ref, ei_ref, mor_ref, moi_ref, p1r_ref, p1i_ref):
    lam_re = jnp.minimum(are_ref[0], -1e-4)
    lam_im = aim_ref[0]
    dt = jnp.exp(ldt_ref[0])
    mag = jnp.exp(lam_re * dt)
    ang = lam_im * dt
    ab_re = mag * jnp.cos(ang)
    ab_im = mag * jnp.sin(ang)
    den = lam_re * lam_re + lam_im * lam_im
    nr, ni = ab_re - 1.0, ab_im
    f_re = (nr * lam_re + ni * lam_im) / den
    f_im = (ni * lam_re - nr * lam_im) / den
    bt_re, bt_im = btr_ref[0], bti_ref[0]
    bb_re = f_re * bt_re - f_im * bt_im
    bb_im = f_re * bt_im + f_im * bt_re
    lag = (lax.broadcasted_iota(jnp.int32, bt_re.shape, 1) >> 4).astype(F32)
    pm = jnp.exp(lag * (lam_re * dt))
    pa = lag * ang
    pk_re = pm * jnp.cos(pa)
    pk_im = pm * jnp.sin(pa)
    e_re = pk_re * bb_re - pk_im * bb_im
    e_im = pk_re * bb_im + pk_im * bb_re
    p1_re = pk_re * ab_re - pk_im * ab_im
    p1_im = pk_re * ab_im + pk_im * ab_re
    ct_re, ct_im = ctr_ref[0], cti_ref[0]
    mor_ref[0] = ct_re * p1_re - ct_im * p1_im
    moi_ref[0] = -(ct_re * p1_im + ct_im * p1_re)
    er_ref[0] = e_re
    ei_ref[0] = e_im
    p1r_ref[0] = p1_re
    p1i_ref[0] = p1_im
    hi = lax.Precision.HIGHEST
    k_ref[0] = (jnp.dot(cr_ref[0], e_re, precision=hi, preferred_element_type=F32)
                - jnp.dot(ci_ref[0], e_im, precision=hi, preferred_element_type=F32))


def _s5_prep(a_re, a_im, b_re, b_im, c_re, c_im, log_dt):
    g, n, j = S5_GROUPS, S5_STATE, S5_GROUP
    w = S5_BLOCK * j
    col = lambda t: t.reshape(g, n, 1)
    bt = lambda t: jnp.tile(t, (1, 1, S5_BLOCK))
    ct = lambda t: jnp.tile(jnp.swapaxes(t, 1, 2), (1, 1, S5_BLOCK))
    big = pl.BlockSpec((1, n, w), lambda i: (i, 0, 0))
    colspec = pl.BlockSpec((1, n, 1), lambda i: (i, 0, 0))
    cspec = pl.BlockSpec((1, j, n), lambda i: (i, 0, 0))
    outs = pl.pallas_call(
        _s5_prep_kernel,
        grid=(g,),
        in_specs=[colspec, colspec, pl.BlockSpec((1, 1, 1), lambda i: (i, 0, 0)),
                  big, big, cspec, cspec, big, big],
        out_specs=[pl.BlockSpec((1, j, w), lambda i: (i, 0, 0))] + [big] * 6,
        out_shape=[jax.ShapeDtypeStruct((g, j, w), F32)] + [jax.ShapeDtypeStruct((g, n, w), F32)] * 6,
        name="s5_prep",
    )(col(a_re), col(a_im), log_dt.reshape(g, 1, 1), bt(b_re), bt(b_im), c_re, c_im, ct(c_re), ct(c_im))
    kk, e_re, e_im, mo_re, mo_im, p1_re, p1_im = outs

    q = S5_BLOCK
    kp = kk.reshape(g, j, q, j).transpose(0, 2, 3, 1)
    s_idx = jnp.arange(q)[:, None]
    t_idx = jnp.arange(q)[None, :]
    lag = t_idx - s_idx
    toe = kp[:, jnp.clip(lag, 0, q - 1)]
    toe = jnp.where((lag >= 0)[None, :, :, None, None], toe, 0.0)
    toe = toe.transpose(0, 1, 3, 2, 4)
    pad_n = lambda t, axis: jnp.pad(t, [(0, LANES - n) if a == axis else (0, 0) for a in range(3)])

    def mats(steps):
        m_intra = toe[:, :steps, :, :steps, :].reshape(g, steps * j, steps * j).astype(BF16)
        rev = lambda e: pad_n(e.reshape(g, n, q, j)[:, :, steps - 1::-1, :]
                              .transpose(0, 2, 3, 1).reshape(g, steps * j, n), 2).astype(BF16)
        a_re_s = pad_n(p1_re[:, :, (steps - 1) * j].reshape(g, 1, n), 2)
        a_im_s = pad_n(p1_im[:, :, (steps - 1) * j].reshape(g, 1, n), 2)
        return dict(m_intra=m_intra, m_in_re=rev(e_re), m_in_im=rev(e_im),
                    m_out_re=pad_n(mo_re[:, :, :steps * j], 1).astype(BF16),
                    m_out_im=pad_n(mo_im[:, :, :steps * j], 1).astype(BF16),
                    a_re=a_re_s, a_im=a_im_s)

    return mats


def _in_proj_kernel(x_ref, gpre_ref, w_ref, wgk_ref, bgk_ref,
                    q_ref, k_ref, v_ref, g_ref, lg_ref, u_ref):
    h = _rms(x_ref[...], gpre_ref[...])
    proj = _dot(h.astype(BF16), w_ref[...])
    q_ref[...] = proj[:, 0:256]
    k_ref[...] = proj[:, 256:512]
    v_ref[...] = proj[:, 512:1024]
    g_ref[...] = proj[:, 1024:1536]
    u_ref[...] = proj[:, 1536:2048]
    z = _dot(proj[:, 2048:PROJ_W].astype(BF16), wgk_ref[...]) + bgk_ref[...]
    lg_ref[...] = jax.nn.log_sigmoid(z) * (1.0 / GATE_NORM)


def _in_proj(x, gpre, w_p, wgk_p, bgk, tm):
    t = x.shape[0]
    row = lambda w: pl.BlockSpec((tm, w), lambda i: (i, 0))
    widths = (256, 256, 512, 512, 256, 512)
    return pl.pallas_call(
        _in_proj_kernel,
        grid=(t // tm,),
        in_specs=[row(D_MODEL), _const_spec((1, D_MODEL)), _const_spec((D_MODEL, PROJ_W)),
                  _const_spec((LANES, GLA_KDIM)), _const_spec((1, GLA_KDIM))],
        out_specs=[row(w) for w in widths],
        out_shape=[jax.ShapeDtypeStruct((t, w), F32) for w in widths],
        compiler_params=pltpu.CompilerParams(dimension_semantics=("parallel",),
                                             vmem_limit_bytes=VMEM_LIMIT),
        name="in_proj",
    )(x, gpre, w_p, wgk_p, bgk)


def _gla_kernel(q_ref, k_ref, v_ref, lg_ref, s0_ref, o_ref, s_ref, *, bb, sg, chunk):
    r = sg * chunk
    shift = chunk.bit_length() - 1

    @pl.when(pl.program_id(1) == 0)
    def _():
        s_ref[...] = s0_ref[...]

    ri = lax.broadcasted_iota(jnp.int32, (r, r), 0)
    ci = lax.broadcasted_iota(jnp.int32, (r, r), 1)
    causal = ((ri >> shift) == (ci >> shift)) & (ri >= ci)
    tri_bf = jnp.where(causal, 1.0, 0.0).astype(BF16)
    lane_head = lax.broadcasted_iota(jnp.int32, (r, GLA_KDIM), 1) >> 6
    eye = (lax.broadcasted_iota(jnp.int32, (GLA_KDIM, GLA_KDIM), 0)
           == lax.broadcasted_iota(jnp.int32, (GLA_KDIM, GLA_KDIM), 1))

    for gi in range(bb // sg):
        seqs = slice(gi * sg, (gi + 1) * sg)
        lg = lg_ref[seqs].reshape(r, GLA_KDIM)
        lg_hi = lg.astype(BF16)
        lg_lo = (lg - lg_hi.astype(F32)).astype(BF16)
        b = _dot(tri_bf, lg_hi) + _dot(tri_bf, lg_lo)
        lasts = [b[(i + 1) * chunk - 1:(i + 1) * chunk, :] for i in range(sg)]
        bl = jnp.concatenate([jnp.broadcast_to(t, (chunk, GLA_KDIM)) for t in lasts], axis=0)
        q = q_ref[seqs].reshape(r, GLA_KDIM)
        k = k_ref[seqs].reshape(r, GLA_KDIM)
        v = v_ref[seqs].reshape(r, D_GLA)
        qd = q * (GLA_DK ** -0.5) * jnp.exp(b)
        ki = (k * jnp.exp(-b)).astype(BF16)
        ke = k * jnp.exp(bl - b)

        qd_h, ke_h, v_h, o_intra = [], [], [], []
        for h in range(GLA_HEADS):
            hm = lane_head == h
            qd_h.append(jnp.where(hm, qd, 0.0).astype(BF16))
            ke_h.append(jnp.where(hm, ke, 0.0).astype(BF16))
            v_h.append(v[:, h * GLA_DV:(h + 1) * GLA_DV].astype(BF16))
            att = jnp.where(causal, _dot_nt(qd_h[h], ki), 0.0).astype(BF16)
            o_intra.append(_dot(att, v_h[h]))

        for i in range(sg):
            seq = gi * sg + i
            rows = slice(i * chunk, (i + 1) * chunk)
            s_old = s_ref[seq]
            s_bf = s_old.astype(BF16)
            dcol = jnp.sum(jnp.where(eye, jnp.exp(lasts[i]), 0.0), axis=1, keepdims=True)
            acc = dcol * s_old
            for h in range(GLA_HEADS):
                o_ref[seq, :, h * GLA_DV:(h + 1) * GLA_DV] = o_intra[h][rows] + _dot(qd_h[h][rows], s_bf)
                acc = acc + _dot_tn(ke_h[h][rows], v_h[h][rows])
            s_ref[seq] = acc


def _gla(q, k, v, lg, s0, bb, sg, chunk):
    b, l, _ = q.shape
    blk = lambda w: pl.BlockSpec((bb, chunk, w), lambda i, c: (i, c, 0))
    sspec = pl.BlockSpec((bb, GLA_KDIM, GLA_DV), lambda i, c: (i, 0, 0))
    return pl.pallas_call(
        functools.partial(_gla_kernel, bb=bb, sg=sg, chunk=chunk),
        grid=(b // bb, l // chunk),
        in_specs=[blk(GLA_KDIM), blk(GLA_KDIM), blk(D_GLA), blk(GLA_KDIM), sspec],
        out_specs=[blk(D_GLA), sspec],
        out_shape=[jax.ShapeDtypeStruct((b, l, D_GLA), F32),
                   jax.ShapeDtypeStruct((b, GLA_KDIM, GLA_DV), F32)],
        compiler_params=pltpu.CompilerParams(dimension_semantics=("parallel", "arbitrary"),
                                             vmem_limit_bytes=VMEM_LIMIT),
        name=f"gla_c{chunk}",
    )(q, k, v, lg, s0)


def _lane_block_transpose(a):
    a = list(a)
    blk = lax.broadcasted_iota(jnp.int32, a[0].shape, 1) >> 4
    for d in (4, 2, 1):
        upper = (blk & d) != 0
        for r in range(8):
            if r & d:
                continue
            lo, hi = a[r], a[r + d]
            a[r] = jnp.where(upper, pltpu.roll(hi, 16 * d, 1), lo)
            a[r + d] = jnp.where(upper, hi, pltpu.roll(lo, LANES - 16 * d, 1))
    return a


def _s5_kernel(u_ref, mi_ref, minr_ref, mini_ref, mor_ref, moi_ref, ar_ref, ai_ref, dsk_ref,
               h0r_ref, h0i_ref, y_ref, hfr_ref, hfi_ref, vr_s, vi_s, hr_s, hi_s, *, steps, nb, cb):
    gs = SLAB_GROUPS
    batch = cb if nb == 1 else nb
    n_blocks = (nb * cb) // batch
    halves = steps // 8
    interleave = nb > 1 and cb > 1

    @pl.when(pl.program_id(1) == 0)
    def _():
        hfr_ref[...] = h0r_ref[...]
        hfi_ref[...] = h0i_ref[...]

    def step_rows(ref, t):
        parts = [ref[b, pl.ds(t, cb, stride=steps), :] for b in range(nb)]
        return parts[0] if nb == 1 else jnp.concatenate(parts, axis=0)

    def to_scan_order(ref, g, val):
        if not interleave:
            ref[g] = val
        else:
            for b in range(nb):
                ref[g, pl.ds(b, cb, stride=nb), :] = val[b * cb:(b + 1) * cb]

    def from_scan_order(ref, g):
        if not interleave:
            return ref[g]
        return jnp.concatenate([ref[g, pl.ds(b, cb, stride=nb), :] for b in range(nb)], axis=0)

    cols = [_lane_block_transpose([step_rows(u_ref, hf * 8 + t) for t in range(8)]) for hf in range(halves)]
    uf = [cols[0][g] if halves == 1 else jnp.concatenate([c[g] for c in cols], axis=1) for g in range(gs)]
    ub = [x.astype(BF16) for x in uf]
    for g in range(gs):
        to_scan_order(vr_s, g, _dot(ub[g], minr_ref[g]))
        to_scan_order(vi_s, g, _dot(ub[g], mini_ref[g]))
    ar = [ar_ref[g] for g in range(gs)]
    ai = [ai_ref[g] for g in range(gs)]

    def body(c, carry):
        rows = pl.ds(pl.multiple_of(c * batch, batch), batch)
        new = []
        for g in range(gs):
            hr, hi = carry[2 * g], carry[2 * g + 1]
            hr_s[g, rows, :] = hr
            hi_s[g, rows, :] = hi
            new.append(ar[g] * hr - ai[g] * hi + vr_s[g, rows, :])
            new.append(ar[g] * hi + ai[g] * hr + vi_s[g, rows, :])
        return tuple(new)

    init = tuple(ref[g] for g in range(gs) for ref in (hfr_ref, hfi_ref))
    fin = lax.fori_loop(0, n_blocks, body, init)
    ys = []
    for g in range(gs):
        hfr_ref[g] = fin[2 * g]
        hfi_ref[g] = fin[2 * g + 1]
        ys.append(_dot(ub[g], mi_ref[g]) + _dot(from_scan_order(hr_s, g).astype(BF16), mor_ref[g])
                  + _dot(from_scan_order(hi_s, g).astype(BF16), moi_ref[g]) + uf[g] * dsk_ref[g])
    for hf in range(halves):
        back = _lane_block_transpose([y[:, hf * LANES:(hf + 1) * LANES] for y in ys])
        for t in range(8):
            for b in range(nb):
                y_ref[b, pl.ds(hf * 8 + t, cb, stride=steps), :] = back[t][b * cb:(b + 1) * cb]


def _s5(u, m, d_skip, h0_re, h0_im, steps, cb):
    nb, l, _ = u.shape
    batch = cb if nb == 1 else nb
    n, gs, w = LANES, SLAB_GROUPS, steps * S5_GROUP
    uspec = pl.BlockSpec((nb, cb * steps, LANES), lambda s, i: (0, i, s))
    per_s = lambda a, b_: pl.BlockSpec((gs, a, b_), lambda s, i: (s, 0, 0))
    if nb == 1:
        hspec = pl.BlockSpec((gs, batch, n), lambda s, i: (s, i, 0))
    else:
        hspec = per_s(batch, n)
    return pl.pallas_call(
        functools.partial(_s5_kernel, steps=steps, nb=nb, cb=cb),
        grid=(D_S5 // LANES, l // (cb * steps)),
        in_specs=[uspec, per_s(w, w), per_s(w, n), per_s(w, n), per_s(n, w), per_s(n, w),
                  per_s(1, n), per_s(1, n), per_s(1, w), hspec, hspec],
        out_specs=[uspec, hspec, hspec],
        out_shape=[jax.ShapeDtypeStruct(u.shape, F32),
                   jax.ShapeDtypeStruct(h0_re.shape, F32),
                   jax.ShapeDtypeStruct(h0_im.shape, F32)],
        scratch_shapes=[pltpu.VMEM((gs, nb * cb, n), F32)] * 4,
        compiler_params=pltpu.CompilerParams(dimension_semantics=("parallel", "arbitrary"),
                                             vmem_limit_bytes=VMEM_LIMIT),
        name=f"s5_w{w}",
    )(u, m["m_intra"], m["m_in_re"], m["m_in_im"], m["m_out_re"], m["m_out_im"],
      m["a_re"], m["a_im"], jnp.tile(d_skip.reshape(S5_GROUPS, 1, S5_GROUP), (1, 1, steps)), h0_re, h0_im)


def _mix_kernel(x_ref, o_ref, g_ref, y_ref, gn_ref, wglu_ref, s5n_ref,
                woa_ref, wob_ref, gpost_ref, out_ref):
    o = o_ref[...]
    gn = gn_ref[...]
    heads = []
    for h in range(GLA_HEADS):
        heads.append(_rms(o[:, h * GLA_DV:(h + 1) * GLA_DV], gn))
    og = jnp.concatenate(heads, axis=1) * jax.nn.silu(g_ref[...])
    y = jax.nn.gelu(y_ref[...])
    y = y * jax.nn.sigmoid(_dot(y.astype(BF16), wglu_ref[...]))
    y = _rms(y, s5n_ref[...])
    mix = _dot(og.astype(BF16), woa_ref[...]) + _dot(y.astype(BF16), wob_ref[...])
    out_ref[...] = x_ref[...] + _rms(mix, gpost_ref[...])


def _mix(x, o, g, y, gn, wglu, s5n, woa, wob, gpost, tm):
    t = x.shape[0]
    row = lambda w: pl.BlockSpec((tm, w), lambda i: (i, 0))
    return pl.pallas_call(
        _mix_kernel,
        grid=(t // tm,),
        in_specs=[row(D_MODEL), row(D_GLA), row(D_GLA), row(D_S5),
                  _const_spec((1, GLA_DV)), _const_spec((D_S5, D_S5)),
                  _const_spec((1, D_S5)), _const_spec((D_GLA, D_MODEL)), _const_spec((D_S5, D_MODEL)),
                  _const_spec((1, D_MODEL))],
        out_specs=row(D_MODEL),
        out_shape=jax.ShapeDtypeStruct((t, D_MODEL), F32),
        compiler_params=pltpu.CompilerParams(dimension_semantics=("parallel",),
                                             vmem_limit_bytes=VMEM_LIMIT),
        name="mix",
    )(x, o, g, y, gn, wglu, s5n, woa, wob, gpost)


def _ffn_kernel(x_ref, gpre_ref, wg_ref, wu_ref, wd_ref, gpost_ref, out_ref):
    x = x_ref[...]
    h = _rms(x, gpre_ref[...]).astype(BF16)
    acc = jnp.zeros(x.shape, F32)
    for c in range(D_FF // FF_CHUNK):
        cols = slice(c * FF_CHUNK, (c + 1) * FF_CHUNK)
        act = jax.nn.silu(_dot(h, wg_ref[:, cols])) * _dot(h, wu_ref[:, cols])
        acc = acc + _dot(act.astype(BF16), wd_ref[cols, :])
    out_ref[...] = x + _rms(acc, gpost_ref[...])


def _ffn(x, gpre, wg, wu, wd, gpost, tm):
    t = x.shape[0]
    row = pl.BlockSpec((tm, D_MODEL), lambda i: (i, 0))
    return pl.pallas_call(
        _ffn_kernel,
        grid=(t // tm,),
        in_specs=[row, _const_spec((1, D_MODEL)), _const_spec((D_MODEL, D_FF)),
                  _const_spec((D_MODEL, D_FF)), _const_spec((D_FF, D_MODEL)), _const_spec((1, D_MODEL))],
        out_specs=row,
        out_shape=jax.ShapeDtypeStruct((t, D_MODEL), F32),
        compiler_params=pltpu.CompilerParams(dimension_semantics=("parallel",),
                                             vmem_limit_bytes=VMEM_LIMIT),
        name="ffn",
    )(x, gpre, wg, wu, wd, gpost)


def kernel(x_prompt, x_sample, state_gla, state_s5_re, state_s5_im, meta_tokens, g_pre_mix, w_in, w_gk2, b_gk, gla_norm, s5_a_re, s5_a_im, s5_b_re, s5_b_im, s5_c_re, s5_c_im, s5_d, s5_log_dt, w_s5_glu, s5_norm, w_o, g_post_mix, g_pre_ffn, w_gate, w_up, w_down, g_post_ffn):
    assert g_pre_mix.shape[0] == 1, "single-layer step"
    bp, seq_p, _ = x_prompt.shape
    bs, seq_s, _ = x_sample.shape
    row = lambda t: t[0].reshape(1, -1)

    w = w_in[0]
    c3, c4 = 1536, 1536 + GATE_RANK
    w_p = jnp.concatenate([w[:, :c3], w[:, c4:], w[:, c3:c4],
                           jnp.zeros((D_MODEL, LANES - GATE_RANK), F32)], axis=1).astype(BF16)
    wgk_p = jnp.concatenate([w_gk2[0], jnp.zeros((LANES - GATE_RANK, GLA_KDIM), F32)], axis=0).astype(BF16)
    wo_bf = w_o[0].astype(BF16)
    s5m = _s5_prep(s5_a_re[0], s5_a_im[0], s5_b_re[0], s5_b_im[0], s5_c_re[0], s5_c_im[0], s5_log_dt[0])
    proj_w = (row(g_pre_mix), w_p, wgk_p, row(b_gk))
    pad_state = lambda h: jnp.pad(h, ((0, 0), (0, 0), (0, LANES - S5_STATE)))

    def finish(x, o, g, y, tm):
        x1 = _mix(x, o, g, y, row(gla_norm), w_s5_glu[0].astype(BF16), row(s5_norm),
                  wo_bf[:D_GLA], wo_bf[D_GLA:], row(g_post_mix), tm)
        return _ffn(x1, row(g_pre_ffn), w_gate[0].astype(BF16), w_up[0].astype(BF16),
                    w_down[0].astype(BF16), row(g_post_ffn), tm)

    xm = jnp.broadcast_to(meta_tokens[None], (bp, N_META, D_MODEL)).reshape(bp * N_META, D_MODEL)
    q, k, v, _, lg, u = _in_proj(xm, *proj_w, bp * N_META)
    r3 = lambda t, b, l: t.reshape(b, l, t.shape[-1])
    _, s_meta = _gla(r3(q, bp, N_META), r3(k, bp, N_META), r3(v, bp, N_META), r3(lg, bp, N_META),
                     jnp.zeros((bp, GLA_KDIM, GLA_DV), F32), bp, bp, N_META)
    zh = jnp.zeros((S5_GROUPS, bp, LANES), F32)
    _, hm_re, hm_im = _s5(u[None], s5m(S5_BLOCK), s5_d[0], zh, zh, S5_BLOCK, bp)

    xp = x_prompt.reshape(bp * seq_p, D_MODEL)
    q, k, v, g, lg, u = _in_proj(xp, *proj_w, 512)
    o, s_p = _gla(r3(q, bp, seq_p), r3(k, bp, seq_p), r3(v, bp, seq_p), r3(lg, bp, seq_p), s_meta, bp, 1, GLA_CHUNK)
    y, hp_re, hp_im = _s5(r3(u, bp, seq_p), s5m(S5_BLOCK), s5_d[0], hm_re, hm_im, S5_BLOCK, 32)
    y_prompt = finish(xp, o.reshape(bp * seq_p, D_GLA), g, y.reshape(bp * seq_p, D_S5), 512)

    xs = x_sample.reshape(bs * seq_s, D_MODEL)
    to_g = lambda t: pad_state(jnp.swapaxes(t[0], 0, 1))
    q, k, v, g, lg, u = _in_proj(xs, *proj_w, 512)
    o, s_s = _gla(r3(q, bs, seq_s), r3(k, bs, seq_s), r3(v, bs, seq_s), r3(lg, bs, seq_s),
                  state_gla[0].reshape(bs, GLA_KDIM, GLA_DV), 32, 16, seq_s)
    y, hs_re, hs_im = _s5(u[None], s5m(seq_s), s5_d[0], to_g(state_s5_re), to_g(state_s5_im), seq_s, bs)
    y_sample = finish(xs, o.reshape(bs * seq_s, D_GLA), g, y[0], 512)

    gla_out = lambda s, b: s.reshape(1, b, GLA_HEADS, GLA_DK, GLA_DV)
    s5_out = lambda h: jnp.swapaxes(h[:, :, :S5_STATE], 0, 1)[None]
    return (y_prompt.reshape(bp, seq_p, D_MODEL), y_sample.reshape(bs, seq_s, D_MODEL),
            gla_out(s_p, bp), s5_out(hp_re), s5_out(hp_im),
            gla_out(s_s, bs), s5_out(hs_re), s5_out(hs_im))
```

```python
import functools

import jax
import jax.numpy as jnp
from jax import lax
from jax.experimental import pallas as pl
from jax.experimental.pallas import tpu as pltpu

F32 = jnp.float32
BF16 = jnp.bfloat16

D_MODEL = 1024
D_GLA = 512
GLA_HEADS = 4
GLA_DV = 128
GLA_DK = 64
GLA_KDIM = 256
GATE_RANK = 16
GATE_NORM = 16.0
GLA_CHUNK = 64
D_S5 = 512
S5_GROUP = 16
S5_GROUPS = 32
S5_STATE = 64
N_META = 16
D_FF = 2816
EPS = 1e-6
LANES = 128
S5_BLOCK = 16
SLAB_GROUPS = LANES // S5_GROUP
PROJ_W = 2176
FF_CHUNK = 256
VMEM_LIMIT = 48 * 1024 * 1024


def _rms(x, g):
    return x * lax.rsqrt(jnp.mean(x * x, axis=-1, keepdims=True) + EPS) * g


def _dot(a, b):
    return jnp.dot(a, b, preferred_element_type=F32)


def _dot_nt(a, b):
    return lax.dot_general(a, b, (((1,), (1,)), ((), ())), preferred_element_type=F32)


def _dot_tn(a, b):
    return lax.dot_general(a, b, (((0,), (0,)), ((), ())), preferred_element_type=F32)


def _const_spec(shape):
    zeros = (0,) * len(shape)
    return pl.BlockSpec(shape, lambda *_: zeros)


PREP_GROUPS = 8


def _cmul(a, b):
    return a[0] * b[0] - a[1] * b[1], a[0] * b[1] + a[1] * b[0]


def _unit_powers(c1, s1, expo, n_bits):
    acc = (jnp.ones_like(c1), jnp.zeros_like(c1))
    base = (c1, s1)
    squares = [base]
    for bit in range(n_bits):
        take = ((expo >> bit) & 1) == 1
        nxt = _cmul(acc, base)
        acc = (jnp.where(take, nxt[0], acc[0]), jnp.where(take, nxt[1], acc[1]))
        base = _cmul(base, base)
        squares.append(base)
    return acc, squares


def _s5_prep_kernel(arc_ref, aic_ref, arr_ref, air_ref, ldt_ref, btr_ref, bti_ref, ctr_ref, cti_ref,
                    mi16_ref, mi8_ref, inr16_ref, ini16_ref, inr8_ref, ini8_ref,
                    outr16_ref, outi16_ref, outr8_ref, outi8_ref, a16r_ref, a16i_ref, a8r_ref, a8i_ref):
    n, q = S5_STATE, S5_BLOCK
    w = q * S5_GROUP
    hp = lax.Precision.HIGHEST
    lane = lax.broadcasted_iota(jnp.int32, (n, w), 1)
    t_blk = lane >> 4
    row_s = lax.broadcasted_iota(jnp.int32, (w, 1), 0) >> 4
    zpad = lambda x, axis: jnp.concatenate([x, jnp.zeros_like(x)], axis=axis)
    for g in range(PREP_GROUPS):
        dt = jnp.exp(ldt_ref[g])
        lam_re = jnp.minimum(arc_ref[g], -1e-4)
        ang = aic_ref[g] * dt
        c1, s1 = jnp.cos(ang), jnp.sin(ang)
        unit, _ = _unit_powers(c1, s1, t_blk, 4)
        pm = jnp.exp(t_blk.astype(F32) * (lam_re * dt))
        pk = (pm * unit[0], pm * unit[1])
        mag = jnp.exp(lam_re * dt)
        p1 = _cmul(pk, (mag * c1, mag * s1))
        ct = (ctr_ref[g], cti_ref[g])
        g0 = _cmul(ct, pk)
        mo = _cmul(ct, p1)
        mo_re, mo_im = zpad(mo[0], 0), zpad(-mo[1], 0)
        outr16_ref[g] = mo_re.astype(BF16)
        outi16_ref[g] = mo_im.astype(BF16)
        outr8_ref[g] = mo_re[:, :w // 2].astype(BF16)
        outi8_ref[g] = mo_im[:, :w // 2].astype(BF16)
        lam_re_r = jnp.minimum(arr_ref[g], -1e-4)
        lam_im_r = air_ref[g]
        ang_r = lam_im_r * dt
        c1r, s1r = jnp.cos(ang_r), jnp.sin(ang_r)
        mag_r = jnp.exp(lam_re_r * dt)
        ab = (mag_r * c1r, mag_r * s1r)
        den = lam_re_r * lam_re_r + lam_im_r * lam_im_r
        nr, ni = ab[0] - 1.0, ab[1]
        f = ((nr * lam_re_r + ni * lam_im_r) / den, (ni * lam_re_r - nr * lam_im_r) / den)
        bbt = _cmul(f, (btr_ref[g], bti_ref[g]))
        for steps, inr_ref, ini_ref, ar_ref, ai_ref in ((q, inr16_ref, ini16_ref, a16r_ref, a16i_ref),
                                                        (q // 2, inr8_ref, ini8_ref, a8r_ref, a8i_ref)):
            rows = steps * S5_GROUP
            expo = (steps - 1) - row_s[:rows]
            unit_r, squares = _unit_powers(c1r, s1r, expo, 4)
            pm_r = jnp.exp(expo.astype(F32) * (lam_re_r * dt))
            e = _cmul((bbt[0][:rows], bbt[1][:rows]), (pm_r * unit_r[0], pm_r * unit_r[1]))
            inr_ref[g] = zpad(e[0], 1).astype(BF16)
            ini_ref[g] = zpad(e[1], 1).astype(BF16)
            hop = squares[steps.bit_length() - 1]
            hop_m = jnp.exp(float(steps) * (lam_re_r * dt))
            ar_ref[g] = zpad(hop_m * hop[0], 1)
            ai_ref[g] = zpad(hop_m * hop[1], 1)
        t0 = (jnp.dot(bbt[0][:S5_GROUP], g0[0], precision=hp, preferred_element_type=F32)
              - jnp.dot(bbt[1][:S5_GROUP], g0[1], precision=hp, preferred_element_type=F32))
        lane_t = lax.broadcasted_iota(jnp.int32, t0.shape, 1) >> 4
        for s in range(q):
            blk = t0 if s == 0 else jnp.where(lane_t >= s, pltpu.roll(t0, S5_GROUP * s, 1), 0.0)
            mi16_ref[g, s * S5_GROUP:(s + 1) * S5_GROUP, :] = blk.astype(BF16)
            if s < q // 2:
                mi8_ref[g, s * S5_GROUP:(s + 1) * S5_GROUP, :] = blk[:, :w // 2].astype(BF16)


def _s5_prep(a_re, a_im, b_re, b_im, c_re, c_im, log_dt):
    g, n, j, q = S5_GROUPS, S5_STATE, S5_GROUP, S5_BLOCK
    w, h, pg = q * j, q * j // 2, PREP_GROUPS
    col = lambda t: t.reshape(g, n, 1)
    rowv = lambda t: t.reshape(g, 1, n)
    bt = lambda t: jnp.tile(jnp.swapaxes(t, 1, 2), (1, q, 1))
    ct = lambda t: jnp.tile(jnp.swapaxes(t, 1, 2), (1, 1, q))
    spec = lambda a, b_: pl.BlockSpec((pg, a, b_), lambda i: (i, 0, 0))
    shapes = [(w, w), (h, h), (w, LANES), (w, LANES), (h, LANES), (h, LANES),
              (LANES, w), (LANES, w), (LANES, h), (LANES, h)]
    outs = pl.pallas_call(
        _s5_prep_kernel,
        grid=(g // pg,),
        in_specs=[spec(n, 1), spec(n, 1), spec(1, n), spec(1, n), spec(1, 1),
                  spec(w, n), spec(w, n), spec(n, w), spec(n, w)],
        out_specs=[spec(*s) for s in shapes] + [spec(1, LANES)] * 4,
        out_shape=[jax.ShapeDtypeStruct((g,) + s, BF16) for s in shapes]
                  + [jax.ShapeDtypeStruct((g, 1, LANES), F32)] * 4,
        compiler_params=pltpu.CompilerParams(dimension_semantics=("parallel",)),
        name="s5_prep",
    )(col(a_re), col(a_im), rowv(a_re), rowv(a_im), log_dt.reshape(g, 1, 1),
      bt(b_re), bt(b_im), ct(c_re), ct(c_im))
    mi16, mi8, inr16, ini16, inr8, ini8, outr16, outi16, outr8, outi8, a16r, a16i, a8r, a8i = outs
    return {q: dict(m_intra=mi16, m_in_re=inr16, m_in_im=ini16, m_out_re=outr16, m_out_im=outi16,
                    a_re=a16r, a_im=a16i),
            q // 2: dict(m_intra=mi8, m_in_re=inr8, m_in_im=ini8, m_out_re=outr8, m_out_im=outi8,
                         a_re=a8r, a_im=a8i)}


def _in_proj_kernel(x_ref, gpre_ref, w_ref, wgk_ref, bgk_ref,
                    q_ref, k_ref, v_ref, g_ref, lg_ref, u_ref):
    h = _rms(x_ref[...], gpre_ref[...])
    proj = _dot(h.astype(BF16), w_ref[...])
    q_ref[...] = proj[:, 0:256]
    k_ref[...] = proj[:, 256:512]
    v_ref[...] = proj[:, 512:1024]
    g_ref[...] = proj[:, 1024:1536]
    u_ref[...] = proj[:, 1536:2048]
    z = _dot(proj[:, 2048:PROJ_W].astype(BF16), wgk_ref[...]) + bgk_ref[...]
    lg_ref[...] = jax.nn.log_sigmoid(z) * (1.0 / GATE_NORM)


def _in_proj(x, gpre, w_p, wgk_p, bgk, tm):
    t = x.shape[0]
    row = lambda w: pl.BlockSpec((tm, w), lambda i: (i, 0))
    widths = (256, 256, 512, 512, 256, 512)
    return pl.pallas_call(
        _in_proj_kernel,
        grid=(t // tm,),
        in_specs=[row(D_MODEL), _const_spec((1, D_MODEL)), _const_spec((D_MODEL, PROJ_W)),
                  _const_spec((LANES, GLA_KDIM)), _const_spec((1, GLA_KDIM))],
        out_specs=[row(w) for w in widths],
        out_shape=[jax.ShapeDtypeStruct((t, w), F32) for w in widths],
        compiler_params=pltpu.CompilerParams(dimension_semantics=("parallel",),
                                             vmem_limit_bytes=VMEM_LIMIT),
        name="in_proj",
    )(x, gpre, w_p, wgk_p, bgk)


def _gla_kernel(q_ref, k_ref, v_ref, lg_ref, s0_ref, o_ref, s_ref, *, bb, sg, chunk):
    r = sg * chunk
    shift = chunk.bit_length() - 1

    @pl.when(pl.program_id(1) == 0)
    def _():
        s_ref[...] = s0_ref[...]

    ri = lax.broadcasted_iota(jnp.int32, (r, r), 0)
    ci = lax.broadcasted_iota(jnp.int32, (r, r), 1)
    causal = ((ri >> shift) == (ci >> shift)) & (ri >= ci)
    tri_bf = jnp.where(causal, 1.0, 0.0).astype(BF16)
    lane_head = lax.broadcasted_iota(jnp.int32, (r, GLA_KDIM), 1) >> 6
    eye = (lax.broadcasted_iota(jnp.int32, (GLA_KDIM, GLA_KDIM), 0)
           == lax.broadcasted_iota(jnp.int32, (GLA_KDIM, GLA_KDIM), 1))

    for gi in range(bb // sg):
        seqs = slice(gi * sg, (gi + 1) * sg)
        lg = lg_ref[seqs].reshape(r, GLA_KDIM)
        lg_hi = lg.astype(BF16)
        lg_lo = (lg - lg_hi.astype(F32)).astype(BF16)
        b = _dot(tri_bf, lg_hi) + _dot(tri_bf, lg_lo)
        lasts = [b[(i + 1) * chunk - 1:(i + 1) * chunk, :] for i in range(sg)]
        bl = jnp.concatenate([jnp.broadcast_to(t, (chunk, GLA_KDIM)) for t in lasts], axis=0)
        q = q_ref[seqs].reshape(r, GLA_KDIM)
        k = k_ref[seqs].reshape(r, GLA_KDIM)
        v = v_ref[seqs].reshape(r, D_GLA)
        qd = q * (GLA_DK ** -0.5) * jnp.exp(b)
        ki = (k * jnp.exp(-b)).astype(BF16)
        ke = k * jnp.exp(bl - b)

        qd_h, ke_h, v_h, o_intra = [], [], [], []
        for h in range(GLA_HEADS):
            hm = lane_head == h
            qd_h.append(jnp.where(hm, qd, 0.0).astype(BF16))
            ke_h.append(jnp.where(hm, ke, 0.0).astype(BF16))
            v_h.append(v[:, h * GLA_DV:(h + 1) * GLA_DV].astype(BF16))
            att = jnp.where(causal, _dot_nt(qd_h[h], ki), 0.0).astype(BF16)
            o_intra.append(_dot(att, v_h[h]))

        for i in range(sg):
            seq = gi * sg + i
            rows = slice(i * chunk, (i + 1) * chunk)
            s_old = s_ref[seq]
            s_bf = s_old.astype(BF16)
            dcol = jnp.sum(jnp.where(eye, jnp.exp(lasts[i]), 0.0), axis=1, keepdims=True)
            acc = dcol * s_old
            for h in range(GLA_HEADS):
                o_ref[seq, :, h * GLA_DV:(h + 1) * GLA_DV] = o_intra[h][rows] + _dot(qd_h[h][rows], s_bf)
                acc = acc + _dot_tn(ke_h[h][rows], v_h[h][rows])
            s_ref[seq] = acc


def _gla(q, k, v, lg, s0, bb, sg, chunk):
    b, l, _ = q.shape
    blk = lambda w: pl.BlockSpec((bb, chunk, w), lambda i, c: (i, c, 0))
    sspec = pl.BlockSpec((bb, GLA_KDIM, GLA_DV), lambda i, c: (i, 0, 0))
    return pl.pallas_call(
        functools.partial(_gla_kernel, bb=bb, sg=sg, chunk=chunk),
        grid=(b // bb, l // chunk),
        in_specs=[blk(GLA_KDIM), blk(GLA_KDIM), blk(D_GLA), blk(GLA_KDIM), sspec],
        out_specs=[blk(D_GLA), sspec],
        out_shape=[jax.ShapeDtypeStruct((b, l, D_GLA), F32),
                   jax.ShapeDtypeStruct((b, GLA_KDIM, GLA_DV), F32)],
        compiler_params=pltpu.CompilerParams(dimension_semantics=("parallel", "arbitrary"),
                                             vmem_limit_bytes=VMEM_LIMIT),
        name=f"gla_c{chunk}",
    )(q, k, v, lg, s0)


def _lane_block_transpose(a):
    a = list(a)
    blk = lax.broadcasted_iota(jnp.int32, a[0].shape, 1) >> 4
    for d in (4, 2, 1):
        upper = (blk & d) != 0
        for r in range(8):
            if r & d:
                continue
            lo, hi = a[r], a[r + d]
            a[r] = jnp.where(upper, pltpu.roll(hi, 16 * d, 1), lo)
            a[r + d] = jnp.where(upper, hi, pltpu.roll(lo, LANES - 16 * d, 1))
    return a


def _s5_kernel(u_ref, mi_ref, minr_ref, mini_ref, mor_ref, moi_ref, ar_ref, ai_ref, dsk_ref,
               h0r_ref, h0i_ref, y_ref, hfr_ref, hfi_ref, vr_s, vi_s, hr_s, hi_s, *, steps, nb, cb):
    gs = SLAB_GROUPS
    batch = cb if nb == 1 else nb
    n_blocks = (nb * cb) // batch
    halves = steps // 8
    interleave = nb > 1 and cb > 1

    @pl.when(pl.program_id(1) == 0)
    def _():
        hfr_ref[...] = h0r_ref[...]
        hfi_ref[...] = h0i_ref[...]

    def step_rows(ref, t):
        parts = [ref[b, pl.ds(t, cb, stride=steps), :] for b in range(nb)]
        return parts[0] if nb == 1 else jnp.concatenate(parts, axis=0)

    def to_scan_order(ref, g, val):
        if not interleave:
            ref[g] = val
        else:
            for b in range(nb):
                ref[g, pl.ds(b, cb, stride=nb), :] = val[b * cb:(b + 1) * cb]

    def from_scan_order(ref, g):
        if not interleave:
            return ref[g]
        return jnp.concatenate([ref[g, pl.ds(b, cb, stride=nb), :] for b in range(nb)], axis=0)

    cols = [_lane_block_transpose([step_rows(u_ref, hf * 8 + t) for t in range(8)]) for hf in range(halves)]
    uf = [cols[0][g] if halves == 1 else jnp.concatenate([c[g] for c in cols], axis=1) for g in range(gs)]
    ub = [x.astype(BF16) for x in uf]
    for g in range(gs):
        to_scan_order(vr_s, g, _dot(ub[g], minr_ref[g]))
        to_scan_order(vi_s, g, _dot(ub[g], mini_ref[g]))
    ar = [ar_ref[g] for g in range(gs)]
    ai = [ai_ref[g] for g in range(gs)]

    def body(c, carry):
        rows = pl.ds(pl.multiple_of(c * batch, batch), batch)
        new = []
        for g in range(gs):
            hr, hi = carry[2 * g], carry[2 * g + 1]
            hr_s[g, rows, :] = hr
            hi_s[g, rows, :] = hi
            new.append(ar[g] * hr - ai[g] * hi + vr_s[g, rows, :])
            new.append(ar[g] * hi + ai[g] * hr + vi_s[g, rows, :])
        return tuple(new)

    init = tuple(ref[g] for g in range(gs) for ref in (hfr_ref, hfi_ref))
    fin = lax.fori_loop(0, n_blocks, body, init)
    ys = []
    for g in range(gs):
        hfr_ref[g] = fin[2 * g]
        hfi_ref[g] = fin[2 * g + 1]
        ys.append(_dot(ub[g], mi_ref[g]) + _dot(from_scan_order(hr_s, g).astype(BF16), mor_ref[g])
                  + _dot(from_scan_order(hi_s, g).astype(BF16), moi_ref[g]) + uf[g] * dsk_ref[g])
    for hf in range(halves):
        back = _lane_block_transpose([y[:, hf * LANES:(hf + 1) * LANES] for y in ys])
        for t in range(8):
            for b in range(nb):
                y_ref[b, pl.ds(hf * 8 + t, cb, stride=steps), :] = back[t][b * cb:(b + 1) * cb]


def _s5(u, m, d_skip, h0_re, h0_im, steps, cb):
    nb, l, _ = u.shape
    batch = cb if nb == 1 else nb
    n, gs, w = LANES, SLAB_GROUPS, steps * S5_GROUP
    uspec = pl.BlockSpec((nb, cb * steps, LANES), lambda s, i: (0, i, s))
    per_s = lambda a, b_: pl.BlockSpec((gs, a, b_), lambda s, i: (s, 0, 0))
    if nb == 1:
        hspec = pl.BlockSpec((gs, batch, n), lambda s, i: (s, i, 0))
    else:
        hspec = per_s(batch, n)
    return pl.pallas_call(
        functools.partial(_s5_kernel, steps=steps, nb=nb, cb=cb),
        grid=(D_S5 // LANES, l // (cb * steps)),
        in_specs=[uspec, per_s(w, w), per_s(w, n), per_s(w, n), per_s(n, w), per_s(n, w),
                  per_s(1, n), per_s(1, n), per_s(1, w), hspec, hspec],
        out_specs=[uspec, hspec, hspec],
        out_shape=[jax.ShapeDtypeStruct(u.shape, F32),
                   jax.ShapeDtypeStruct(h0_re.shape, F32),
                   jax.ShapeDtypeStruct(h0_im.shape, F32)],
        scratch_shapes=[pltpu.VMEM((gs, nb * cb, n), F32)] * 4,
        compiler_params=pltpu.CompilerParams(dimension_semantics=("parallel", "arbitrary"),
                                             vmem_limit_bytes=VMEM_LIMIT),
        name=f"s5_w{w}",
    )(u, m["m_intra"], m["m_in_re"], m["m_in_im"], m["m_out_re"], m["m_out_im"],
      m["a_re"], m["a_im"], jnp.tile(d_skip.reshape(S5_GROUPS, 1, S5_GROUP), (1, 1, steps)), h0_re, h0_im)


def _mix_kernel(x_ref, o_ref, g_ref, y_ref, gn_ref, wglu_ref, s5n_ref, wo_ref, gpost_ref, out_ref):
    o = o_ref[...]
    gn = gn_ref[...]
    heads = []
    for h in range(GLA_HEADS):
        heads.append(_rms(o[:, h * GLA_DV:(h + 1) * GLA_DV], gn))
    og = jnp.concatenate(heads, axis=1) * jax.nn.silu(g_ref[...])
    y = jax.nn.gelu(y_ref[...])
    y = y * jax.nn.sigmoid(_dot(y.astype(BF16), wglu_ref[...]))
    y = _rms(y, s5n_ref[...])
    mix = _dot(og.astype(BF16), wo_ref[:D_GLA, :]) + _dot(y.astype(BF16), wo_ref[D_GLA:, :])
    out_ref[...] = x_ref[...] + _rms(mix, gpost_ref[...])


def _mix(x, o, g, y, gn, wglu, s5n, wo, gpost, tm):
    t = x.shape[0]
    row = lambda w: pl.BlockSpec((tm, w), lambda i: (i, 0))
    return pl.pallas_call(
        _mix_kernel,
        grid=(t // tm,),
        in_specs=[row(D_MODEL), row(D_GLA), row(D_GLA), row(D_S5),
                  _const_spec((1, GLA_DV)), _const_spec((D_S5, D_S5)),
                  _const_spec((1, D_S5)), _const_spec((D_GLA + D_S5, D_MODEL)), _const_spec((1, D_MODEL))],
        out_specs=row(D_MODEL),
        out_shape=jax.ShapeDtypeStruct((t, D_MODEL), F32),
        compiler_params=pltpu.CompilerParams(dimension_semantics=("parallel",),
                                             vmem_limit_bytes=VMEM_LIMIT),
        name="mix",
    )(x, o, g, y, gn, wglu, s5n, wo, gpost)


def _ffn_kernel(x_ref, gpre_ref, wg_ref, wu_ref, wd_ref, gpost_ref, out_ref):
    x = x_ref[...]
    h = _rms(x, gpre_ref[...]).astype(BF16)
    acc = jnp.zeros(x.shape, F32)
    for c in range(D_FF // FF_CHUNK):
        cols = slice(c * FF_CHUNK, (c + 1) * FF_CHUNK)
        act = jax.nn.silu(_dot(h, wg_ref[:, cols])) * _dot(h, wu_ref[:, cols])
        acc = acc + _dot(act.astype(BF16), wd_ref[cols, :])
    out_ref[...] = x + _rms(acc, gpost_ref[...])


def _ffn(x, gpre, wg, wu, wd, gpost, tm):
    t = x.shape[0]
    row = pl.BlockSpec((tm, D_MODEL), lambda i: (i, 0))
    return pl.pallas_call(
        _ffn_kernel,
        grid=(t // tm,),
        in_specs=[row, _const_spec((1, D_MODEL)), _const_spec((D_MODEL, D_FF)),
                  _const_spec((D_MODEL, D_FF)), _const_spec((D_FF, D_MODEL)), _const_spec((1, D_MODEL))],
        out_specs=row,
        out_shape=jax.ShapeDtypeStruct((t, D_MODEL), F32),
        compiler_params=pltpu.CompilerParams(dimension_semantics=("parallel",),
                                             vmem_limit_bytes=VMEM_LIMIT),
        name="ffn",
    )(x, gpre, wg, wu, wd, gpost)


def kernel(x_prompt, x_sample, state_gla, state_s5_re, state_s5_im, meta_tokens, g_pre_mix, w_in, w_gk2, b_gk, gla_norm, s5_a_re, s5_a_im, s5_b_re, s5_b_im, s5_c_re, s5_c_im, s5_d, s5_log_dt, w_s5_glu, s5_norm, w_o, g_post_mix, g_pre_ffn, w_gate, w_up, w_down, g_post_ffn):
    assert g_pre_mix.shape[0] == 1, "single-layer step"
    bp, seq_p, _ = x_prompt.shape
    bs, seq_s, _ = x_sample.shape
    row = lambda t: t[0].reshape(1, -1)

    w = w_in[0]
    c3, c4 = 1536, 1536 + GATE_RANK
    w_p = jnp.concatenate([w[:, :c3], w[:, c4:], w[:, c3:c4],
                           jnp.zeros((D_MODEL, LANES - GATE_RANK), F32)], axis=1).astype(BF16)
    wgk_p = jnp.concatenate([w_gk2[0], jnp.zeros((LANES - GATE_RANK, GLA_KDIM), F32)], axis=0).astype(BF16)
    wo_bf = w_o[0].astype(BF16)
    s5m = _s5_prep(s5_a_re[0], s5_a_im[0], s5_b_re[0], s5_b_im[0], s5_c_re[0], s5_c_im[0], s5_log_dt[0])
    proj_w = (row(g_pre_mix), w_p, wgk_p, row(b_gk))
    pad_state = lambda h: jnp.pad(h, ((0, 0), (0, 0), (0, LANES - S5_STATE)))

    def finish(x, o, g, y, tm):
        x1 = _mix(x, o, g, y, row(gla_norm), w_s5_glu[0].astype(BF16), row(s5_norm),
                  wo_bf, row(g_post_mix), tm)
        return _ffn(x1, row(g_pre_ffn), w_gate[0].astype(BF16), w_up[0].astype(BF16),
                    w_down[0].astype(BF16), row(g_post_ffn), tm)

    xm = jnp.broadcast_to(meta_tokens[None], (bp, N_META, D_MODEL)).reshape(bp * N_META, D_MODEL)
    q, k, v, _, lg, u = _in_proj(xm, *proj_w, bp * N_META)
    r3 = lambda t, b, l: t.reshape(b, l, t.shape[-1])
    _, s_meta = _gla(r3(q, bp, N_META), r3(k, bp, N_META), r3(v, bp, N_META), r3(lg, bp, N_META),
                     jnp.zeros((bp, GLA_KDIM, GLA_DV), F32), bp, bp, N_META)
    zh = jnp.zeros((S5_GROUPS, bp, LANES), F32)
    _, hm_re, hm_im = _s5(u[None], s5m[S5_BLOCK], s5_d[0], zh, zh, S5_BLOCK, bp)

    xp = x_prompt.reshape(bp * seq_p, D_MODEL)
    q, k, v, g, lg, u = _in_proj(xp, *proj_w, 512)
    o, s_p = _gla(r3(q, bp, seq_p), r3(k, bp, seq_p), r3(v, bp, seq_p), r3(lg, bp, seq_p), s_meta, bp, 1, GLA_CHUNK)
    y, hp_re, hp_im = _s5(r3(u, bp, seq_p), s5m[S5_BLOCK], s5_d[0], hm_re, hm_im, S5_BLOCK, 32)
    y_prompt = finish(xp, o.reshape(bp * seq_p, D_GLA), g, y.reshape(bp * seq_p, D_S5), 512)

    xs = x_sample.reshape(bs * seq_s, D_MODEL)
    to_g = lambda t: pad_state(jnp.swapaxes(t[0], 0, 1))
    q, k, v, g, lg, u = _in_proj(xs, *proj_w, 512)
    o, s_s = _gla(r3(q, bs, seq_s), r3(k, bs, seq_s), r3(v, bs, seq_s), r3(lg, bs, seq_s),
                  state_gla[0].reshape(bs, GLA_KDIM, GLA_DV), 32, 16, seq_s)
    y, hs_re, hs_im = _s5(u[None], s5m[seq_s], s5_d[0], to_g(state_s5_re), to_g(state_s5_im), seq_s, bs)
    y_sample = finish(xs, o.reshape(bs * seq_s, D_GLA), g, y[0], 512)

    gla_out = lambda s, b: s.reshape(1, b, GLA_HEADS, GLA_DK, GLA_DV)
    s5_out = lambda h: jnp.swapaxes(h[:, :, :S5_STATE], 0, 1)[None]
    return (y_prompt.reshape(bp, seq_p, D_MODEL), y_sample.reshape(bs, seq_s, D_MODEL),
            gla_out(s_p, bp), s5_out(hp_re), s5_out(hp_im),
            gla_out(s_s, bs), s5_out(hs_re), s5_out(hs_im))
```

```python
import functools

import jax
import jax.numpy as jnp
from jax import lax
from jax.experimental import pallas as pl
from jax.experimental.pallas import tpu as pltpu

F32 = jnp.float32
BF16 = jnp.bfloat16

D_MODEL = 1024
D_GLA = 512
GLA_HEADS = 4
GLA_DV = 128
GLA_DK = 64
GLA_KDIM = 256
GATE_RANK = 16
GATE_NORM = 16.0
GLA_CHUNK = 64
D_S5 = 512
S5_GROUP = 16
S5_GROUPS = 32
S5_STATE = 64
N_META = 16
D_FF = 2816
EPS = 1e-6
LANES = 128
S5_BLOCK = 16
SLAB_GROUPS = LANES // S5_GROUP
PROJ_W = 2176
FF_CHUNK = 256
VMEM_LIMIT = 48 * 1024 * 1024


def _rms(x, g):
    return x * lax.rsqrt(jnp.mean(x * x, axis=-1, keepdims=True) + EPS) * g


def _dot(a, b):
    return jnp.dot(a, b, preferred_element_type=F32)


def _dot_nt(a, b):
    return lax.dot_general(a, b, (((1,), (1,)), ((), ())), preferred_element_type=F32)


def _dot_tn(a, b):
    return lax.dot_general(a, b, (((0,), (0,)), ((), ())), preferred_element_type=F32)


def _const_spec(shape):
    zeros = (0,) * len(shape)
    return pl.BlockSpec(shape, lambda *_: zeros)


PREP_GROUPS = 8


def _cmul(a, b):
    return a[0] * b[0] - a[1] * b[1], a[0] * b[1] + a[1] * b[0]


def _unit_powers(c1, s1, expo, n_bits):
    acc = (jnp.ones_like(c1), jnp.zeros_like(c1))
    base = (c1, s1)
    squares = [base]
    for bit in range(n_bits):
        take = ((expo >> bit) & 1) == 1
        nxt = _cmul(acc, base)
        acc = (jnp.where(take, nxt[0], acc[0]), jnp.where(take, nxt[1], acc[1]))
        base = _cmul(base, base)
        squares.append(base)
    return acc, squares


def _s5_prep_kernel(arc_ref, aic_ref, arr_ref, air_ref, ldt_ref, btr_ref, bti_ref, ctr_ref, cti_ref,
                    mi16_ref, mi8_ref, inr16_ref, ini16_ref, inr8_ref, ini8_ref,
                    outr16_ref, outi16_ref, outr8_ref, outi8_ref, a16r_ref, a16i_ref, a8r_ref, a8i_ref):
    n, q = S5_STATE, S5_BLOCK
    w = q * S5_GROUP
    hp = lax.Precision.HIGHEST
    lane = lax.broadcasted_iota(jnp.int32, (n, w), 1)
    t_blk = lane >> 4
    row_s = lax.broadcasted_iota(jnp.int32, (w, 1), 0) >> 4
    zpad = lambda x, axis: jnp.concatenate([x, jnp.zeros_like(x)], axis=axis)
    for g in range(PREP_GROUPS):
        dt = jnp.exp(ldt_ref[g])
        lam_re = jnp.minimum(arc_ref[g], -1e-4)
        ang = aic_ref[g] * dt
        c1, s1 = jnp.cos(ang), jnp.sin(ang)
        unit, _ = _unit_powers(c1, s1, t_blk, 4)
        pm = jnp.exp(t_blk.astype(F32) * (lam_re * dt))
        pk = (pm * unit[0], pm * unit[1])
        mag = jnp.exp(lam_re * dt)
        p1 = _cmul(pk, (mag * c1, mag * s1))
        ct = (ctr_ref[g], cti_ref[g])
        g0 = _cmul(ct, pk)
        mo = _cmul(ct, p1)
        mo_re, mo_im = zpad(mo[0], 0), zpad(-mo[1], 0)
        outr16_ref[g] = mo_re.astype(BF16)
        outi16_ref[g] = mo_im.astype(BF16)
        outr8_ref[g] = mo_re[:, :w // 2].astype(BF16)
        outi8_ref[g] = mo_im[:, :w // 2].astype(BF16)
        lam_re_r = jnp.minimum(arr_ref[g], -1e-4)
        lam_im_r = air_ref[g]
        ang_r = lam_im_r * dt
        c1r, s1r = jnp.cos(ang_r), jnp.sin(ang_r)
        mag_r = jnp.exp(lam_re_r * dt)
        ab = (mag_r * c1r, mag_r * s1r)
        den = lam_re_r * lam_re_r + lam_im_r * lam_im_r
        nr, ni = ab[0] - 1.0, ab[1]
        f = ((nr * lam_re_r + ni * lam_im_r) / den, (ni * lam_re_r - nr * lam_im_r) / den)
        bbt = _cmul(f, (btr_ref[g], bti_ref[g]))
        for steps, inr_ref, ini_ref, ar_ref, ai_ref in ((q, inr16_ref, ini16_ref, a16r_ref, a16i_ref),
                                                        (q // 2, inr8_ref, ini8_ref, a8r_ref, a8i_ref)):
            rows = steps * S5_GROUP
            expo = (steps - 1) - row_s[:rows]
            unit_r, squares = _unit_powers(c1r, s1r, expo, 4)
            pm_r = jnp.exp(expo.astype(F32) * (lam_re_r * dt))
            e = _cmul((bbt[0][:rows], bbt[1][:rows]), (pm_r * unit_r[0], pm_r * unit_r[1]))
            inr_ref[g] = zpad(e[0], 1).astype(BF16)
            ini_ref[g] = zpad(e[1], 1).astype(BF16)
            hop = squares[steps.bit_length() - 1]
            hop_m = jnp.exp(float(steps) * (lam_re_r * dt))
            ar_ref[g] = zpad(hop_m * hop[0], 1)
            ai_ref[g] = zpad(hop_m * hop[1], 1)
        t0 = (jnp.dot(bbt[0][:S5_GROUP], g0[0], precision=hp, preferred_element_type=F32)
              - jnp.dot(bbt[1][:S5_GROUP], g0[1], precision=hp, preferred_element_type=F32))
        lane_t = lax.broadcasted_iota(jnp.int32, t0.shape, 1) >> 4
        for s in range(q):
            blk = t0 if s == 0 else jnp.where(lane_t >= s, pltpu.roll(t0, S5_GROUP * s, 1), 0.0)
            mi16_ref[g, s * S5_GROUP:(s + 1) * S5_GROUP, :] = blk.astype(BF16)
            if s < q // 2:
                mi8_ref[g, s * S5_GROUP:(s + 1) * S5_GROUP, :] = blk[:, :w // 2].astype(BF16)


def _s5_prep(a_re, a_im, b_re, b_im, c_re, c_im, log_dt):
    g, n, j, q = S5_GROUPS, S5_STATE, S5_GROUP, S5_BLOCK
    w, h, pg = q * j, q * j // 2, PREP_GROUPS
    col = lambda t: t.reshape(g, n, 1)
    rowv = lambda t: t.reshape(g, 1, n)
    bt = lambda t: jnp.tile(jnp.swapaxes(t, 1, 2), (1, q, 1))
    ct = lambda t: jnp.tile(jnp.swapaxes(t, 1, 2), (1, 1, q))
    spec = lambda a, b_: pl.BlockSpec((pg, a, b_), lambda i: (i, 0, 0))
    shapes = [(w, w), (h, h), (w, LANES), (w, LANES), (h, LANES), (h, LANES),
              (LANES, w), (LANES, w), (LANES, h), (LANES, h)]
    outs = pl.pallas_call(
        _s5_prep_kernel,
        grid=(g // pg,),
        in_specs=[spec(n, 1), spec(n, 1), spec(1, n), spec(1, n), spec(1, 1),
                  spec(w, n), spec(w, n), spec(n, w), spec(n, w)],
        out_specs=[spec(*s) for s in shapes] + [spec(1, LANES)] * 4,
        out_shape=[jax.ShapeDtypeStruct((g,) + s, BF16) for s in shapes]
                  + [jax.ShapeDtypeStruct((g, 1, LANES), F32)] * 4,
        compiler_params=pltpu.CompilerParams(dimension_semantics=("parallel",)),
        name="s5_prep",
    )(col(a_re), col(a_im), rowv(a_re), rowv(a_im), log_dt.reshape(g, 1, 1),
      bt(b_re), bt(b_im), ct(c_re), ct(c_im))
    mi16, mi8, inr16, ini16, inr8, ini8, outr16, outi16, outr8, outi8, a16r, a16i, a8r, a8i = outs
    return {q: dict(m_intra=mi16, m_in_re=inr16, m_in_im=ini16, m_out_re=outr16, m_out_im=outi16,
                    a_re=a16r, a_im=a16i),
            q // 2: dict(m_intra=mi8, m_in_re=inr8, m_in_im=ini8, m_out_re=outr8, m_out_im=outi8,
                         a_re=a8r, a_im=a8i)}


def _in_proj_kernel(x_ref, gpre_ref, w_ref, wgk_ref, bgk_ref,
                    q_ref, k_ref, v_ref, g_ref, lg_ref, u_ref):
    h = _rms(x_ref[...], gpre_ref[...])
    proj = _dot(h.astype(BF16), w_ref[...])
    q_ref[...] = proj[:, 0:256]
    k_ref[...] = proj[:, 256:512]
    v_ref[...] = proj[:, 512:1024]
    g_ref[...] = proj[:, 1024:1536]
    u_ref[...] = proj[:, 1536:2048]
    z = _dot(proj[:, 2048:PROJ_W].astype(BF16), wgk_ref[...]) + bgk_ref[...]
    lg_ref[...] = jax.nn.log_sigmoid(z) * (1.0 / GATE_NORM)


def _in_proj(x, gpre, w_p, wgk_p, bgk, tm):
    t = x.shape[0]
    row = lambda w: pl.BlockSpec((tm, w), lambda i: (i, 0))
    widths = (256, 256, 512, 512, 256, 512)
    return pl.pallas_call(
        _in_proj_kernel,
        grid=(t // tm,),
        in_specs=[row(D_MODEL), _const_spec((1, D_MODEL)), _const_spec((D_MODEL, PROJ_W)),
                  _const_spec((LANES, GLA_KDIM)), _const_spec((1, GLA_KDIM))],
        out_specs=[row(w) for w in widths],
        out_shape=[jax.ShapeDtypeStruct((t, w), F32) for w in widths],
        compiler_params=pltpu.CompilerParams(dimension_semantics=("parallel",),
                                             vmem_limit_bytes=VMEM_LIMIT),
        name="in_proj",
    )(x, gpre, w_p, wgk_p, bgk)


def _gla_kernel(q_ref, k_ref, v_ref, lg_ref, s0_ref, o_ref, s_ref, *, bb, sg, chunk):
    r = sg * chunk
    shift = chunk.bit_length() - 1

    @pl.when(pl.program_id(1) == 0)
    def _():
        s_ref[...] = s0_ref[...]

    ri = lax.broadcasted_iota(jnp.int32, (r, r), 0)
    ci = lax.broadcasted_iota(jnp.int32, (r, r), 1)
    causal = ((ri >> shift) == (ci >> shift)) & (ri >= ci)
    tri_bf = jnp.where(causal, 1.0, 0.0).astype(BF16)
    lane_head = lax.broadcasted_iota(jnp.int32, (r, GLA_KDIM), 1) >> 6
    eye = (lax.broadcasted_iota(jnp.int32, (GLA_KDIM, GLA_KDIM), 0)
           == lax.broadcasted_iota(jnp.int32, (GLA_KDIM, GLA_KDIM), 1))

    for gi in range(bb // sg):
        seqs = slice(gi * sg, (gi + 1) * sg)
        lg = lg_ref[seqs].reshape(r, GLA_KDIM)
        lg_hi = lg.astype(BF16)
        lg_lo = (lg - lg_hi.astype(F32)).astype(BF16)
        b = _dot(tri_bf, lg_hi) + _dot(tri_bf, lg_lo)
        lasts = [b[(i + 1) * chunk - 1:(i + 1) * chunk, :] for i in range(sg)]
        bl = jnp.concatenate([jnp.broadcast_to(t, (chunk, GLA_KDIM)) for t in lasts], axis=0)
        q = q_ref[seqs].reshape(r, GLA_KDIM)
        k = k_ref[seqs].reshape(r, GLA_KDIM)
        v = v_ref[seqs].reshape(r, D_GLA)
        qd = q * (GLA_DK ** -0.5) * jnp.exp(b)
        ki = (k * jnp.exp(-b)).astype(BF16)
        ke = k * jnp.exp(bl - b)

        qd_h, ke_h, v_h, o_intra = [], [], [], []
        for h in range(GLA_HEADS):
            hm = lane_head == h
            qd_h.append(jnp.where(hm, qd, 0.0).astype(BF16))
            ke_h.append(jnp.where(hm, ke, 0.0).astype(BF16))
            v_h.append(v[:, h * GLA_DV:(h + 1) * GLA_DV].astype(BF16))
            att = jnp.where(causal, _dot_nt(qd_h[h], ki), 0.0).astype(BF16)
            o_intra.append(_dot(att, v_h[h]))

        for i in range(sg):
            seq = gi * sg + i
            rows = slice(i * chunk, (i + 1) * chunk)
            s_old = s_ref[seq]
            s_bf = s_old.astype(BF16)
            dcol = jnp.sum(jnp.where(eye, jnp.exp(lasts[i]), 0.0), axis=1, keepdims=True)
            acc = dcol * s_old
            for h in range(GLA_HEADS):
                o_ref[seq, :, h * GLA_DV:(h + 1) * GLA_DV] = o_intra[h][rows] + _dot(qd_h[h][rows], s_bf)
                acc = acc + _dot_tn(ke_h[h][rows], v_h[h][rows])
            s_ref[seq] = acc


def _gla(q, k, v, lg, s0, bb, sg, chunk):
    b, l, _ = q.shape
    blk = lambda w: pl.BlockSpec((bb, chunk, w), lambda i, c: (i, c, 0))
    sspec = pl.BlockSpec((bb, GLA_KDIM, GLA_DV), lambda i, c: (i, 0, 0))
    return pl.pallas_call(
        functools.partial(_gla_kernel, bb=bb, sg=sg, chunk=chunk),
        grid=(b // bb, l // chunk),
        in_specs=[blk(GLA_KDIM), blk(GLA_KDIM), blk(D_GLA), blk(GLA_KDIM), sspec],
        out_specs=[blk(D_GLA), sspec],
        out_shape=[jax.ShapeDtypeStruct((b, l, D_GLA), F32),
                   jax.ShapeDtypeStruct((b, GLA_KDIM, GLA_DV), F32)],
        compiler_params=pltpu.CompilerParams(dimension_semantics=("parallel", "arbitrary"),
                                             vmem_limit_bytes=VMEM_LIMIT),
        name=f"gla_c{chunk}",
    )(q, k, v, lg, s0)


def _lane_block_transpose(a):
    a = list(a)
    blk = lax.broadcasted_iota(jnp.int32, a[0].shape, 1) >> 4
    for d in (4, 2, 1):
        upper = (blk & d) != 0
        for r in range(8):
            if r & d:
                continue
            lo, hi = a[r], a[r + d]
            a[r] = jnp.where(upper, pltpu.roll(hi, 16 * d, 1), lo)
            a[r + d] = jnp.where(upper, hi, pltpu.roll(lo, LANES - 16 * d, 1))
    return a


def _s5_kernel(u_ref, mi_ref, minr_ref, mini_ref, mor_ref, moi_ref, ar_ref, ai_ref, dsk_ref,
               h0r_ref, h0i_ref, y_ref, hfr_ref, hfi_ref, vr_s, vi_s, hr_s, hi_s, *, steps, nb, cb):
    gs = SLAB_GROUPS
    batch = cb if nb == 1 else nb
    n_blocks = (nb * cb) // batch
    halves = steps // 8
    interleave = nb > 1 and cb > 1

    @pl.when(pl.program_id(1) == 0)
    def _():
        hfr_ref[...] = h0r_ref[...]
        hfi_ref[...] = h0i_ref[...]

    def step_rows(ref, t):
        parts = [ref[b, pl.ds(t, cb, stride=steps), :] for b in range(nb)]
        return parts[0] if nb == 1 else jnp.concatenate(parts, axis=0)

    def to_scan_order(ref, g, val):
        if not interleave:
            ref[g] = val
        else:
            for b in range(nb):
                ref[g, pl.ds(b, cb, stride=nb), :] = val[b * cb:(b + 1) * cb]

    def from_scan_order(ref, g):
        if not interleave:
            return ref[g]
        return jnp.concatenate([ref[g, pl.ds(b, cb, stride=nb), :] for b in range(nb)], axis=0)

    cols = [_lane_block_transpose([step_rows(u_ref, hf * 8 + t) for t in range(8)]) for hf in range(halves)]
    uf = [cols[0][g] if halves == 1 else jnp.concatenate([c[g] for c in cols], axis=1) for g in range(gs)]
    ub = [x.astype(BF16) for x in uf]
    for g in range(gs):
        to_scan_order(vr_s, g, _dot(ub[g], minr_ref[g]))
        to_scan_order(vi_s, g, _dot(ub[g], mini_ref[g]))
    ar = [ar_ref[g] for g in range(gs)]
    ai = [ai_ref[g] for g in range(gs)]

    def body(c, carry):
        rows = pl.ds(pl.multiple_of(c * batch, batch), batch)
        new = []
        for g in range(gs):
            hr, hi = carry[2 * g], carry[2 * g + 1]
            hr_s[g, rows, :] = hr
            hi_s[g, rows, :] = hi
            new.append(ar[g] * hr - ai[g] * hi + vr_s[g, rows, :])
            new.append(ar[g] * hi + ai[g] * hr + vi_s[g, rows, :])
        return tuple(new)

    init = tuple(ref[g] for g in range(gs) for ref in (hfr_ref, hfi_ref))
    fin = lax.fori_loop(0, n_blocks, body, init)
    ys = []
    for g in range(gs):
        hfr_ref[g] = fin[2 * g]
        hfi_ref[g] = fin[2 * g + 1]
        ys.append(_dot(ub[g], mi_ref[g]) + _dot(from_scan_order(hr_s, g).astype(BF16), mor_ref[g])
                  + _dot(from_scan_order(hi_s, g).astype(BF16), moi_ref[g]) + uf[g] * dsk_ref[g])
    for hf in range(halves):
        back = _lane_block_transpose([y[:, hf * LANES:(hf + 1) * LANES] for y in ys])
        for t in range(8):
            for b in range(nb):
                y_ref[b, pl.ds(hf * 8 + t, cb, stride=steps), :] = back[t][b * cb:(b + 1) * cb]


def _s5(u, m, d_skip, h0_re, h0_im, steps, cb):
    nb, l, _ = u.shape
    batch = cb if nb == 1 else nb
    n, gs, w = LANES, SLAB_GROUPS, steps * S5_GROUP
    uspec = pl.BlockSpec((nb, cb * steps, LANES), lambda s, i: (0, i, s))
    per_s = lambda a, b_: pl.BlockSpec((gs, a, b_), lambda s, i: (s, 0, 0))
    if nb == 1:
        hspec = pl.BlockSpec((gs, batch, n), lambda s, i: (s, i, 0))
    else:
        hspec = per_s(batch, n)
    return pl.pallas_call(
        functools.partial(_s5_kernel, steps=steps, nb=nb, cb=cb),
        grid=(D_S5 // LANES, l // (cb * steps)),
        in_specs=[uspec, per_s(w, w), per_s(w, n), per_s(w, n), per_s(n, w), per_s(n, w),
                  per_s(1, n), per_s(1, n), per_s(1, w), hspec, hspec],
        out_specs=[uspec, hspec, hspec],
        out_shape=[jax.ShapeDtypeStruct(u.shape, F32),
                   jax.ShapeDtypeStruct(h0_re.shape, F32),
                   jax.ShapeDtypeStruct(h0_im.shape, F32)],
        scratch_shapes=[pltpu.VMEM((gs, nb * cb, n), F32)] * 4,
        compiler_params=pltpu.CompilerParams(dimension_semantics=("parallel", "arbitrary"),
                                             vmem_limit_bytes=VMEM_LIMIT),
        name=f"s5_w{w}",
    )(u, m["m_intra"], m["m_in_re"], m["m_in_im"], m["m_out_re"], m["m_out_im"],
      m["a_re"], m["a_im"], jnp.tile(d_skip.reshape(S5_GROUPS, 1, S5_GROUP), (1, 1, steps)), h0_re, h0_im)


def _mix_ffn_kernel(x_ref, o_ref, g_ref, y_ref, gn_ref, wglu_ref, s5n_ref, wo_ref, gpost_ref,
                    gpre_ref, wg_ref, wu_ref, wd_ref, gpostf_ref, out_ref):
    o = o_ref[...]
    gn = gn_ref[...]
    heads = []
    for h in range(GLA_HEADS):
        heads.append(_rms(o[:, h * GLA_DV:(h + 1) * GLA_DV], gn))
    og = jnp.concatenate(heads, axis=1) * jax.nn.silu(g_ref[...])
    y = jax.nn.gelu(y_ref[...])
    y = y * jax.nn.sigmoid(_dot(y.astype(BF16), wglu_ref[...]))
    y = _rms(y, s5n_ref[...])
    mix = _dot(og.astype(BF16), wo_ref[:D_GLA, :]) + _dot(y.astype(BF16), wo_ref[D_GLA:, :])
    x = x_ref[...] + _rms(mix, gpost_ref[...])
    h = _rms(x, gpre_ref[...]).astype(BF16)
    acc = jnp.zeros(x.shape, F32)
    for c in range(D_FF // FF_CHUNK):
        cols = slice(c * FF_CHUNK, (c + 1) * FF_CHUNK)
        act = jax.nn.silu(_dot(h, wg_ref[:, cols])) * _dot(h, wu_ref[:, cols])
        acc = acc + _dot(act.astype(BF16), wd_ref[cols, :])
    out_ref[...] = x + _rms(acc, gpostf_ref[...])


def _mix_ffn(x, o, g, y, gn, wglu, s5n, wo, gpost, gpre, wg, wu, wd, gpostf, tm):
    t = x.shape[0]
    row = lambda w: pl.BlockSpec((tm, w), lambda i: (i, 0))
    fixed = lambda shape: pl.BlockSpec(shape, lambda i: (0,) * len(shape), pipeline_mode=pl.Buffered(1))
    return pl.pallas_call(
        _mix_ffn_kernel,
        grid=(t // tm,),
        in_specs=[row(D_MODEL), row(D_GLA), row(D_GLA), row(D_S5),
                  fixed((1, GLA_DV)), fixed((D_S5, D_S5)), fixed((1, D_S5)),
                  fixed((D_GLA + D_S5, D_MODEL)), fixed((1, D_MODEL)),
                  fixed((1, D_MODEL)), fixed((D_MODEL, D_FF)), fixed((D_MODEL, D_FF)),
                  fixed((D_FF, D_MODEL)), fixed((1, D_MODEL))],
        out_specs=row(D_MODEL),
        out_shape=jax.ShapeDtypeStruct((t, D_MODEL), F32),
        compiler_params=pltpu.CompilerParams(dimension_semantics=("parallel",),
                                             vmem_limit_bytes=VMEM_LIMIT),
        name="mix_ffn",
    )(x, o, g, y, gn, wglu, s5n, wo, gpost, gpre, wg, wu, wd, gpostf)


def kernel(x_prompt, x_sample, state_gla, state_s5_re, state_s5_im, meta_tokens, g_pre_mix, w_in, w_gk2, b_gk, gla_norm, s5_a_re, s5_a_im, s5_b_re, s5_b_im, s5_c_re, s5_c_im, s5_d, s5_log_dt, w_s5_glu, s5_norm, w_o, g_post_mix, g_pre_ffn, w_gate, w_up, w_down, g_post_ffn):
    assert g_pre_mix.shape[0] == 1, "single-layer step"
    bp, seq_p, _ = x_prompt.shape
    bs, seq_s, _ = x_sample.shape
    row = lambda t: t[0].reshape(1, -1)

    w = w_in[0]
    c3, c4 = 1536, 1536 + GATE_RANK
    w_p = jnp.concatenate([w[:, :c3], w[:, c4:], w[:, c3:c4],
                           jnp.zeros((D_MODEL, LANES - GATE_RANK), F32)], axis=1).astype(BF16)
    wgk_p = jnp.concatenate([w_gk2[0], jnp.zeros((LANES - GATE_RANK, GLA_KDIM), F32)], axis=0).astype(BF16)
    wo_bf = w_o[0].astype(BF16)
    s5m = _s5_prep(s5_a_re[0], s5_a_im[0], s5_b_re[0], s5_b_im[0], s5_c_re[0], s5_c_im[0], s5_log_dt[0])
    proj_w = (row(g_pre_mix), w_p, wgk_p, row(b_gk))
    pad_state = lambda h: jnp.pad(h, ((0, 0), (0, 0), (0, LANES - S5_STATE)))

    def finish(x, o, g, y, tm):
        return _mix_ffn(x, o, g, y, row(gla_norm), w_s5_glu[0].astype(BF16), row(s5_norm),
                        wo_bf, row(g_post_mix), row(g_pre_ffn), w_gate[0].astype(BF16),
                        w_up[0].astype(BF16), w_down[0].astype(BF16), row(g_post_ffn), tm)

    xm = jnp.broadcast_to(meta_tokens[None], (bp, N_META, D_MODEL)).reshape(bp * N_META, D_MODEL)
    q, k, v, _, lg, u = _in_proj(xm, *proj_w, bp * N_META)
    r3 = lambda t, b, l: t.reshape(b, l, t.shape[-1])
    _, s_meta = _gla(r3(q, bp, N_META), r3(k, bp, N_META), r3(v, bp, N_META), r3(lg, bp, N_META),
                     jnp.zeros((bp, GLA_KDIM, GLA_DV), F32), bp, bp, N_META)
    zh = jnp.zeros((S5_GROUPS, bp, LANES), F32)
    _, hm_re, hm_im = _s5(u[None], s5m[S5_BLOCK], s5_d[0], zh, zh, S5_BLOCK, bp)

    xp = x_prompt.reshape(bp * seq_p, D_MODEL)
    q, k, v, g, lg, u = _in_proj(xp, *proj_w, 512)
    o, s_p = _gla(r3(q, bp, seq_p), r3(k, bp, seq_p), r3(v, bp, seq_p), r3(lg, bp, seq_p), s_meta, bp, 1, GLA_CHUNK)
    y, hp_re, hp_im = _s5(r3(u, bp, seq_p), s5m[S5_BLOCK], s5_d[0], hm_re, hm_im, S5_BLOCK, 32)
    y_prompt = finish(xp, o.reshape(bp * seq_p, D_GLA), g, y.reshape(bp * seq_p, D_S5), 512)

    xs = x_sample.reshape(bs * seq_s, D_MODEL)
    to_g = lambda t: pad_state(jnp.swapaxes(t[0], 0, 1))
    q, k, v, g, lg, u = _in_proj(xs, *proj_w, 512)
    o, s_s = _gla(r3(q, bs, seq_s), r3(k, bs, seq_s), r3(v, bs, seq_s), r3(lg, bs, seq_s),
                  state_gla[0].reshape(bs, GLA_KDIM, GLA_DV), 32, 16, seq_s)
    y, hs_re, hs_im = _s5(u[None], s5m[seq_s], s5_d[0], to_g(state_s5_re), to_g(state_s5_im), seq_s, bs)
    y_sample = finish(xs, o.reshape(bs * seq_s, D_GLA), g, y[0], 512)

    gla_out = lambda s, b: s.reshape(1, b, GLA_HEADS, GLA_DK, GLA_DV)
    s5_out = lambda h: jnp.swapaxes(h[:, :, :S5_STATE], 0, 1)[None]
    return (y_prompt.reshape(bp, seq_p, D_MODEL), y_sample.reshape(bs, seq_s, D_MODEL),
            gla_out(s_p, bp), s5_out(hp_re), s5_out(hp_im),
            gla_out(s_s, bs), s5_out(hs_re), s5_out(hs_im))
```

```python
import functools

import jax
import jax.numpy as jnp
from jax import lax
from jax.experimental import pallas as pl
from jax.experimental.pallas import tpu as pltpu

F32 = jnp.float32
BF16 = jnp.bfloat16

D_MODEL = 1024
D_GLA = 512
GLA_HEADS = 4
GLA_DV = 128
GLA_DK = 64
GLA_KDIM = 256
GATE_RANK = 16
GATE_NORM = 16.0
GLA_CHUNK = 64
D_S5 = 512
S5_GROUP = 16
S5_GROUPS = 32
S5_STATE = 64
N_META = 16
D_FF = 2816
EPS = 1e-6
LANES = 128
S5_BLOCK = 16
SLAB_GROUPS = LANES // S5_GROUP
PROJ_W = 2176
FF_CHUNK = 256
VMEM_LIMIT = 48 * 1024 * 1024


def _rms(x, g):
    return x * lax.rsqrt(jnp.mean(x * x, axis=-1, keepdims=True) + EPS) * g


def _dot(a, b):
    return jnp.dot(a, b, preferred_element_type=F32)


def _dot_nt(a, b):
    return lax.dot_general(a, b, (((1,), (1,)), ((), ())), preferred_element_type=F32)


def _dot_tn(a, b):
    return lax.dot_general(a, b, (((0,), (0,)), ((), ())), preferred_element_type=F32)


def _const_spec(shape):
    zeros = (0,) * len(shape)
    return pl.BlockSpec(shape, lambda *_: zeros)


PREP_GROUPS = 8


def _cmul(a, b):
    return a[0] * b[0] - a[1] * b[1], a[0] * b[1] + a[1] * b[0]


def _unit_powers(c1, s1, expo, n_bits):
    acc = (jnp.ones_like(c1), jnp.zeros_like(c1))
    base = (c1, s1)
    squares = [base]
    for bit in range(n_bits):
        take = ((expo >> bit) & 1) == 1
        nxt = _cmul(acc, base)
        acc = (jnp.where(take, nxt[0], acc[0]), jnp.where(take, nxt[1], acc[1]))
        base = _cmul(base, base)
        squares.append(base)
    return acc, squares


def _s5_prep_kernel(arc_ref, aic_ref, arr_ref, air_ref, ldt_ref, btr_ref, bti_ref, ctr_ref, cti_ref,
                    mi16_ref, mi8_ref, inr16_ref, ini16_ref, inr8_ref, ini8_ref,
                    outr16_ref, outi16_ref, outr8_ref, outi8_ref, a16r_ref, a16i_ref, a8r_ref, a8i_ref):
    n, q = S5_STATE, S5_BLOCK
    w = q * S5_GROUP
    hp = lax.Precision.HIGHEST
    lane = lax.broadcasted_iota(jnp.int32, (n, w), 1)
    t_blk = lane >> 4
    row_s = lax.broadcasted_iota(jnp.int32, (w, 1), 0) >> 4
    zpad = lambda x, axis: jnp.concatenate([x, jnp.zeros_like(x)], axis=axis)
    for g in range(PREP_GROUPS):
        dt = jnp.exp(ldt_ref[g])
        lam_re = jnp.minimum(arc_ref[g], -1e-4)
        ang = aic_ref[g] * dt
        c1, s1 = jnp.cos(ang), jnp.sin(ang)
        unit, _ = _unit_powers(c1, s1, t_blk, 4)
        pm = jnp.exp(t_blk.astype(F32) * (lam_re * dt))
        pk = (pm * unit[0], pm * unit[1])
        mag = jnp.exp(lam_re * dt)
        p1 = _cmul(pk, (mag * c1, mag * s1))
        ct = (ctr_ref[g], cti_ref[g])
        g0 = _cmul(ct, pk)
        mo = _cmul(ct, p1)
        mo_re, mo_im = zpad(mo[0], 0), zpad(-mo[1], 0)
        outr16_ref[g] = mo_re.astype(BF16)
        outi16_ref[g] = mo_im.astype(BF16)
        outr8_ref[g] = mo_re[:, :w // 2].astype(BF16)
        outi8_ref[g] = mo_im[:, :w // 2].astype(BF16)
        lam_re_r = jnp.minimum(arr_ref[g], -1e-4)
        lam_im_r = air_ref[g]
        ang_r = lam_im_r * dt
        c1r, s1r = jnp.cos(ang_r), jnp.sin(ang_r)
        mag_r = jnp.exp(lam_re_r * dt)
        ab = (mag_r * c1r, mag_r * s1r)
        den = lam_re_r * lam_re_r + lam_im_r * lam_im_r
        nr, ni = ab[0] - 1.0, ab[1]
        f = ((nr * lam_re_r + ni * lam_im_r) / den, (ni * lam_re_r - nr * lam_im_r) / den)
        bbt = _cmul(f, (btr_ref[g], bti_ref[g]))
        for steps, inr_ref, ini_ref, ar_ref, ai_ref in ((q, inr16_ref, ini16_ref, a16r_ref, a16i_ref),
                                                        (q // 2, inr8_ref, ini8_ref, a8r_ref, a8i_ref)):
            rows = steps * S5_GROUP
            expo = (steps - 1) - row_s[:rows]
            unit_r, squares = _unit_powers(c1r, s1r, expo, 4)
            pm_r = jnp.exp(expo.astype(F32) * (lam_re_r * dt))
            e = _cmul((bbt[0][:rows], bbt[1][:rows]), (pm_r * unit_r[0], pm_r * unit_r[1]))
            inr_ref[g] = zpad(e[0], 1).astype(BF16)
            ini_ref[g] = zpad(e[1], 1).astype(BF16)
            hop = squares[steps.bit_length() - 1]
            hop_m = jnp.exp(float(steps) * (lam_re_r * dt))
            ar_ref[g] = zpad(hop_m * hop[0], 1)
            ai_ref[g] = zpad(hop_m * hop[1], 1)
        t0 = (jnp.dot(bbt[0][:S5_GROUP], g0[0], precision=hp, preferred_element_type=F32)
              - jnp.dot(bbt[1][:S5_GROUP], g0[1], precision=hp, preferred_element_type=F32))
        lane_t = lax.broadcasted_iota(jnp.int32, t0.shape, 1) >> 4
        for s in range(q):
            blk = t0 if s == 0 else jnp.where(lane_t >= s, pltpu.roll(t0, S5_GROUP * s, 1), 0.0)
            mi16_ref[g, s * S5_GROUP:(s + 1) * S5_GROUP, :] = blk.astype(BF16)
            if s < q // 2:
                mi8_ref[g, s * S5_GROUP:(s + 1) * S5_GROUP, :] = blk[:, :w // 2].astype(BF16)


def _s5_prep(a_re, a_im, b_re, b_im, c_re, c_im, log_dt):
    g, n, j, q = S5_GROUPS, S5_STATE, S5_GROUP, S5_BLOCK
    w, h, pg = q * j, q * j // 2, PREP_GROUPS
    col = lambda t: t.reshape(g, n, 1)
    rowv = lambda t: t.reshape(g, 1, n)
    bt = lambda t: jnp.tile(jnp.swapaxes(t, 1, 2), (1, q, 1))
    ct = lambda t: jnp.tile(jnp.swapaxes(t, 1, 2), (1, 1, q))
    spec = lambda a, b_: pl.BlockSpec((pg, a, b_), lambda i: (i, 0, 0))
    shapes = [(w, w), (h, h), (w, LANES), (w, LANES), (h, LANES), (h, LANES),
              (LANES, w), (LANES, w), (LANES, h), (LANES, h)]
    outs = pl.pallas_call(
        _s5_prep_kernel,
        grid=(g // pg,),
        in_specs=[spec(n, 1), spec(n, 1), spec(1, n), spec(1, n), spec(1, 1),
                  spec(w, n), spec(w, n), spec(n, w), spec(n, w)],
        out_specs=[spec(*s) for s in shapes] + [spec(1, LANES)] * 4,
        out_shape=[jax.ShapeDtypeStruct((g,) + s, BF16) for s in shapes]
                  + [jax.ShapeDtypeStruct((g, 1, LANES), F32)] * 4,
        compiler_params=pltpu.CompilerParams(dimension_semantics=("parallel",)),
        name="s5_prep",
    )(col(a_re), col(a_im), rowv(a_re), rowv(a_im), log_dt.reshape(g, 1, 1),
      bt(b_re), bt(b_im), ct(c_re), ct(c_im))
    mi16, mi8, inr16, ini16, inr8, ini8, outr16, outi16, outr8, outi8, a16r, a16i, a8r, a8i = outs
    return {q: dict(m_intra=mi16, m_in_re=inr16, m_in_im=ini16, m_out_re=outr16, m_out_im=outi16,
                    a_re=a16r, a_im=a16i),
            q // 2: dict(m_intra=mi8, m_in_re=inr8, m_in_im=ini8, m_out_re=outr8, m_out_im=outi8,
                         a_re=a8r, a_im=a8i)}


def _lane_block_transpose(a):
    a = list(a)
    blk = lax.broadcasted_iota(jnp.int32, a[0].shape, 1) >> 4
    for d in (4, 2, 1):
        upper = (blk & d) != 0
        for r in range(8):
            if r & d:
                continue
            lo, hi = a[r], a[r + d]
            a[r] = jnp.where(upper, pltpu.roll(hi, 16 * d, 1), lo)
            a[r + d] = jnp.where(upper, hi, pltpu.roll(lo, LANES - 16 * d, 1))
    return a


def _in_proj_kernel(x_ref, gpre_ref, w_ref, wgk_ref, bgk_ref,
                    q_ref, k_ref, v_ref, g_ref, lg_ref, u_ref, ug_ref, u_scr, *, steps):
    h = _rms(x_ref[...], gpre_ref[...])
    proj = _dot(h.astype(BF16), w_ref[...])
    q_ref[...] = proj[:, 0:256]
    k_ref[...] = proj[:, 256:512]
    v_ref[...] = proj[:, 512:1024]
    g_ref[...] = proj[:, 1024:1536]
    u = proj[:, 1536:2048]
    u_ref[...] = u
    z = _dot(proj[:, 2048:PROJ_W].astype(BF16), wgk_ref[...]) + bgk_ref[...]
    lg_ref[...] = jax.nn.log_sigmoid(z) * (1.0 / GATE_NORM)

    nr = x_ref.shape[0] // steps
    rt = min(nr, 16)
    for s in range(D_S5 // LANES):
        u_scr[s] = u[:, s * LANES:(s + 1) * LANES]
    for s in range(D_S5 // LANES):
        for hf in range(steps // 8):
            for r0 in range(0, nr, rt):
                a = [u_scr[s, pl.ds(r0 * steps + hf * 8 + t, rt, stride=steps), :] for t in range(8)]
                per_group = _lane_block_transpose(a)
                for g in range(SLAB_GROUPS):
                    ug_ref[s * SLAB_GROUPS + g, r0:r0 + rt, hf * LANES:(hf + 1) * LANES] = (
                        per_group[g].astype(BF16))


def _in_proj(x, gpre, w_p, wgk_p, bgk, tm, steps):
    t = x.shape[0]
    row = lambda w: pl.BlockSpec((tm, w), lambda i: (i, 0))
    widths = (256, 256, 512, 512, 256, 512)
    wg = steps * S5_GROUP
    return pl.pallas_call(
        functools.partial(_in_proj_kernel, steps=steps),
        grid=(t // tm,),
        in_specs=[row(D_MODEL), _const_spec((1, D_MODEL)), _const_spec((D_MODEL, PROJ_W)),
                  _const_spec((LANES, GLA_KDIM)), _const_spec((1, GLA_KDIM))],
        out_specs=[row(w) for w in widths]
                  + [pl.BlockSpec((S5_GROUPS, tm // steps, wg), lambda i: (0, i, 0))],
        out_shape=[jax.ShapeDtypeStruct((t, w), F32) for w in widths]
                  + [jax.ShapeDtypeStruct((S5_GROUPS, t // steps, wg), BF16)],
        scratch_shapes=[pltpu.VMEM((D_S5 // LANES, tm, LANES), F32)],
        compiler_params=pltpu.CompilerParams(dimension_semantics=("parallel",),
                                             vmem_limit_bytes=VMEM_LIMIT),
        name="in_proj",
    )(x, gpre, w_p, wgk_p, bgk)


def _gla_kernel(q_ref, k_ref, v_ref, lg_ref, s0_ref, o_ref, s_ref, *, bb, sg, chunk):
    r = sg * chunk
    shift = chunk.bit_length() - 1

    @pl.when(pl.program_id(1) == 0)
    def _():
        s_ref[...] = s0_ref[...]

    ri = lax.broadcasted_iota(jnp.int32, (r, r), 0)
    ci = lax.broadcasted_iota(jnp.int32, (r, r), 1)
    causal = ((ri >> shift) == (ci >> shift)) & (ri >= ci)
    tri_bf = jnp.where(causal, 1.0, 0.0).astype(BF16)
    lane_head = lax.broadcasted_iota(jnp.int32, (r, GLA_KDIM), 1) >> 6
    eye = (lax.broadcasted_iota(jnp.int32, (GLA_KDIM, GLA_KDIM), 0)
           == lax.broadcasted_iota(jnp.int32, (GLA_KDIM, GLA_KDIM), 1))

    for gi in range(bb // sg):
        seqs = slice(gi * sg, (gi + 1) * sg)
        lg = lg_ref[seqs].reshape(r, GLA_KDIM)
        lg_hi = lg.astype(BF16)
        lg_lo = (lg - lg_hi.astype(F32)).astype(BF16)
        b = _dot(tri_bf, lg_hi) + _dot(tri_bf, lg_lo)
        lasts = [b[(i + 1) * chunk - 1:(i + 1) * chunk, :] for i in range(sg)]
        bl = jnp.concatenate([jnp.broadcast_to(t, (chunk, GLA_KDIM)) for t in lasts], axis=0)
        q = q_ref[seqs].reshape(r, GLA_KDIM)
        k = k_ref[seqs].reshape(r, GLA_KDIM)
        v = v_ref[seqs].reshape(r, D_GLA)
        qd = q * (GLA_DK ** -0.5) * jnp.exp(b)
        ki = (k * jnp.exp(-b)).astype(BF16)
        ke = k * jnp.exp(bl - b)

        qd_h, ke_h, v_h, o_intra = [], [], [], []
        for h in range(GLA_HEADS):
            hm = lane_head == h
            qd_h.append(jnp.where(hm, qd, 0.0).astype(BF16))
            ke_h.append(jnp.where(hm, ke, 0.0).astype(BF16))
            v_h.append(v[:, h * GLA_DV:(h + 1) * GLA_DV].astype(BF16))
            att = jnp.where(causal, _dot_nt(qd_h[h], ki), 0.0).astype(BF16)
            o_intra.append(_dot(att, v_h[h]))

        for i in range(sg):
            seq = gi * sg + i
            rows = slice(i * chunk, (i + 1) * chunk)
            s_old = s_ref[seq]
            s_bf = s_old.astype(BF16)
            dcol = jnp.sum(jnp.where(eye, jnp.exp(lasts[i]), 0.0), axis=1, keepdims=True)
            acc = dcol * s_old
            for h in range(GLA_HEADS):
                o_ref[seq, :, h * GLA_DV:(h + 1) * GLA_DV] = o_intra[h][rows] + _dot(qd_h[h][rows], s_bf)
                acc = acc + _dot_tn(ke_h[h][rows], v_h[h][rows])
            s_ref[seq] = acc


def _gla(q, k, v, lg, s0, bb, sg, chunk):
    b, l, _ = q.shape
    blk = lambda w: pl.BlockSpec((bb, chunk, w), lambda i, c: (i, c, 0))
    sspec = pl.BlockSpec((bb, GLA_KDIM, GLA_DV), lambda i, c: (i, 0, 0))
    return pl.pallas_call(
        functools.partial(_gla_kernel, bb=bb, sg=sg, chunk=chunk),
        grid=(b // bb, l // chunk),
        in_specs=[blk(GLA_KDIM), blk(GLA_KDIM), blk(D_GLA), blk(GLA_KDIM), sspec],
        out_specs=[blk(D_GLA), sspec],
        out_shape=[jax.ShapeDtypeStruct((b, l, D_GLA), F32),
                   jax.ShapeDtypeStruct((b, GLA_KDIM, GLA_DV), F32)],
        compiler_params=pltpu.CompilerParams(dimension_semantics=("parallel", "arbitrary"),
                                             vmem_limit_bytes=VMEM_LIMIT),
        name=f"gla_c{chunk}",
    )(q, k, v, lg, s0)


def _s5_kernel(ug_ref, mi_ref, minr_ref, mini_ref, mor_ref, moi_ref, ar_ref, ai_ref,
               h0r_ref, h0i_ref, yg_ref, hfr_ref, hfi_ref, vr_s, vi_s, hr_s, hi_s, *, nb, cb):
    gs = SLAB_GROUPS
    rows = nb * cb
    batch = cb if nb == 1 else nb
    n_blocks = rows // batch
    interleave = nb > 1 and cb > 1

    @pl.when(pl.program_id(1) == 0)
    def _():
        hfr_ref[...] = h0r_ref[...]
        hfi_ref[...] = h0i_ref[...]

    def to_scan_order(ref, g, val):
        if not interleave:
            ref[g] = val
        else:
            for b in range(nb):
                ref[g, pl.ds(b, cb, stride=nb), :] = val[b * cb:(b + 1) * cb]

    def from_scan_order(ref, g):
        if not interleave:
            return ref[g]
        return jnp.concatenate([ref[g, pl.ds(b, cb, stride=nb), :] for b in range(nb)], axis=0)

    ub = [ug_ref[g].reshape(rows, ug_ref.shape[-1]) for g in range(gs)]
    for g in range(gs):
        to_scan_order(vr_s, g, _dot(ub[g], minr_ref[g]))
        to_scan_order(vi_s, g, _dot(ub[g], mini_ref[g]))
    ar = [ar_ref[g] for g in range(gs)]
    ai = [ai_ref[g] for g in range(gs)]

    def body(c, carry):
        rows = pl.ds(pl.multiple_of(c * batch, batch), batch)
        new = []
        for g in range(gs):
            hr, hi = carry[2 * g], carry[2 * g + 1]
            hr_s[g, rows, :] = hr
            hi_s[g, rows, :] = hi
            new.append(ar[g] * hr - ai[g] * hi + vr_s[g, rows, :])
            new.append(ar[g] * hi + ai[g] * hr + vi_s[g, rows, :])
        return tuple(new)

    init = tuple(ref[g] for g in range(gs) for ref in (hfr_ref, hfi_ref))
    fin = lax.fori_loop(0, n_blocks, body, init)
    for g in range(gs):
        hfr_ref[g] = fin[2 * g]
        hfi_ref[g] = fin[2 * g + 1]
        y = (_dot(ub[g], mi_ref[g]) + _dot(from_scan_order(hr_s, g).astype(BF16), mor_ref[g])
             + _dot(from_scan_order(hi_s, g).astype(BF16), moi_ref[g]))
        yg_ref[g] = y.reshape(yg_ref.shape[1:])


def _s5(ug, m, h0_re, h0_im, nb, cb):
    g_all, r, w = ug.shape
    nc = r // nb
    batch = cb if nb == 1 else nb
    n, gs = LANES, SLAB_GROUPS
    uspec = pl.BlockSpec((gs, nb, cb, w), lambda s, i: (s, 0, i, 0))
    per_s = lambda a, b_: pl.BlockSpec((gs, a, b_), lambda s, i: (s, 0, 0))
    if nb == 1:
        hspec = pl.BlockSpec((gs, batch, n), lambda s, i: (s, i, 0))
    else:
        hspec = per_s(batch, n)
    yg, hf_re, hf_im = pl.pallas_call(
        functools.partial(_s5_kernel, nb=nb, cb=cb),
        grid=(g_all // gs, nc // cb),
        in_specs=[uspec, per_s(w, w), per_s(w, n), per_s(w, n), per_s(n, w), per_s(n, w),
                  per_s(1, n), per_s(1, n), hspec, hspec],
        out_specs=[uspec, hspec, hspec],
        out_shape=[jax.ShapeDtypeStruct((g_all, nb, nc, w), F32),
                   jax.ShapeDtypeStruct(h0_re.shape, F32),
                   jax.ShapeDtypeStruct(h0_im.shape, F32)],
        scratch_shapes=[pltpu.VMEM((gs, nb * cb, n), F32)] * 4,
        compiler_params=pltpu.CompilerParams(dimension_semantics=("parallel", "arbitrary"),
                                             vmem_limit_bytes=VMEM_LIMIT),
        name=f"s5_w{w}",
    )(ug.reshape(g_all, nb, nc, w), m["m_intra"], m["m_in_re"], m["m_in_im"], m["m_out_re"], m["m_out_im"],
      m["a_re"], m["a_im"], h0_re, h0_im)
    return yg.reshape(g_all, r, w), hf_re, hf_im


def _mix_ffn_kernel(x_ref, o_ref, g_ref, yg_ref, u_ref, dsk_ref, gn_ref, wglu_ref, s5n_ref, wo_ref, gpost_ref,
                    gpre_ref, wg_ref, wu_ref, wd_ref, gpostf_ref, out_ref, y_scr, *, steps):
    nr = x_ref.shape[0] // steps
    rt = min(nr, 8)
    for s in range(D_S5 // LANES):
        for hf in range(steps // 8):
            for r0 in range(0, nr, rt):
                a = [yg_ref[s * SLAB_GROUPS + g, r0:r0 + rt, hf * LANES:(hf + 1) * LANES]
                     for g in range(SLAB_GROUPS)]
                per_step = _lane_block_transpose(a)
                for t in range(8):
                    y_scr[s, pl.ds(r0 * steps + hf * 8 + t, rt, stride=steps), :] = per_step[t]
    y5 = jnp.concatenate([y_scr[s] for s in range(D_S5 // LANES)], axis=1) + dsk_ref[...] * u_ref[...]
    o = o_ref[...]
    gn = gn_ref[...]
    heads = []
    for h in range(GLA_HEADS):
        heads.append(_rms(o[:, h * GLA_DV:(h + 1) * GLA_DV], gn))
    og = jnp.concatenate(heads, axis=1) * jax.nn.silu(g_ref[...])
    y = jax.nn.gelu(y5)
    y = y * jax.nn.sigmoid(_dot(y.astype(BF16), wglu_ref[...]))
    y = _rms(y, s5n_ref[...])
    mix = _dot(og.astype(BF16), wo_ref[:D_GLA, :]) + _dot(y.astype(BF16), wo_ref[D_GLA:, :])
    x = x_ref[...] + _rms(mix, gpost_ref[...])
    h = _rms(x, gpre_ref[...]).astype(BF16)
    acc = jnp.zeros(x.shape, F32)
    for c in range(D_FF // FF_CHUNK):
        cols = slice(c * FF_CHUNK, (c + 1) * FF_CHUNK)
        act = jax.nn.silu(_dot(h, wg_ref[:, cols])) * _dot(h, wu_ref[:, cols])
        acc = acc + _dot(act.astype(BF16), wd_ref[cols, :])
    out_ref[...] = x + _rms(acc, gpostf_ref[...])


def _mix_ffn(x, o, g, yg, u, dsk, gn, wglu, s5n, wo, gpost, gpre, wg, wu, wd, gpostf, tm, steps):
    t = x.shape[0]
    row = lambda w: pl.BlockSpec((tm, w), lambda i: (i, 0))
    fixed = lambda shape: pl.BlockSpec(shape, lambda i: (0,) * len(shape), pipeline_mode=pl.Buffered(1))
    return pl.pallas_call(
        functools.partial(_mix_ffn_kernel, steps=steps),
        grid=(t // tm,),
        in_specs=[row(D_MODEL), row(D_GLA), row(D_GLA),
                  pl.BlockSpec((S5_GROUPS, tm // steps, steps * S5_GROUP), lambda i: (0, i, 0)),
                  row(D_S5), fixed((1, D_S5)),
                  fixed((1, GLA_DV)), fixed((D_S5, D_S5)), fixed((1, D_S5)),
                  fixed((D_GLA + D_S5, D_MODEL)), fixed((1, D_MODEL)),
                  fixed((1, D_MODEL)), fixed((D_MODEL, D_FF)), fixed((D_MODEL, D_FF)),
                  fixed((D_FF, D_MODEL)), fixed((1, D_MODEL))],
        out_specs=row(D_MODEL),
        out_shape=jax.ShapeDtypeStruct((t, D_MODEL), F32),
        scratch_shapes=[pltpu.VMEM((D_S5 // LANES, tm, LANES), F32)],
        compiler_params=pltpu.CompilerParams(dimension_semantics=("parallel",),
                                             vmem_limit_bytes=VMEM_LIMIT),
        name="mix_ffn",
    )(x, o, g, yg, u, dsk, gn, wglu, s5n, wo, gpost, gpre, wg, wu, wd, gpostf)


def kernel(x_prompt, x_sample, state_gla, state_s5_re, state_s5_im, meta_tokens, g_pre_mix, w_in, w_gk2, b_gk, gla_norm, s5_a_re, s5_a_im, s5_b_re, s5_b_im, s5_c_re, s5_c_im, s5_d, s5_log_dt, w_s5_glu, s5_norm, w_o, g_post_mix, g_pre_ffn, w_gate, w_up, w_down, g_post_ffn):
    assert g_pre_mix.shape[0] == 1, "single-layer step"
    bp, seq_p, _ = x_prompt.shape
    bs, seq_s, _ = x_sample.shape
    row = lambda t: t[0].reshape(1, -1)

    w = w_in[0]
    c3, c4 = 1536, 1536 + GATE_RANK
    w_p = jnp.concatenate([w[:, :c3], w[:, c4:], w[:, c3:c4],
                           jnp.zeros((D_MODEL, LANES - GATE_RANK), F32)], axis=1).astype(BF16)
    wgk_p = jnp.concatenate([w_gk2[0], jnp.zeros((LANES - GATE_RANK, GLA_KDIM), F32)], axis=0).astype(BF16)
    wo_bf = w_o[0].astype(BF16)
    s5m = _s5_prep(s5_a_re[0], s5_a_im[0], s5_b_re[0], s5_b_im[0], s5_c_re[0], s5_c_im[0], s5_log_dt[0])
    proj_w = (row(g_pre_mix), w_p, wgk_p, row(b_gk))
    pad_state = lambda h: jnp.pad(h, ((0, 0), (0, 0), (0, LANES - S5_STATE)))

    def finish(x, o, g, yg, u, tm, steps):
        return _mix_ffn(x, o, g, yg, u, row(s5_d), row(gla_norm), w_s5_glu[0].astype(BF16), row(s5_norm),
                        wo_bf, row(g_post_mix), row(g_pre_ffn), w_gate[0].astype(BF16),
                        w_up[0].astype(BF16), w_down[0].astype(BF16), row(g_post_ffn), tm, steps)

    xm = jnp.broadcast_to(meta_tokens[None], (bp, N_META, D_MODEL)).reshape(bp * N_META, D_MODEL)
    q, k, v, _, lg, _, ug = _in_proj(xm, *proj_w, bp * N_META, S5_BLOCK)
    r3 = lambda t, b, l: t.reshape(b, l, t.shape[-1])
    _, s_meta = _gla(r3(q, bp, N_META), r3(k, bp, N_META), r3(v, bp, N_META), r3(lg, bp, N_META),
                     jnp.zeros((bp, GLA_KDIM, GLA_DV), F32), bp, bp, N_META)
    zh = jnp.zeros((S5_GROUPS, bp, LANES), F32)
    _, hm_re, hm_im = _s5(ug, s5m[S5_BLOCK], zh, zh, 1, bp)

    xp = x_prompt.reshape(bp * seq_p, D_MODEL)
    q, k, v, g, lg, u, ug = _in_proj(xp, *proj_w, 512, S5_BLOCK)
    o, s_p = _gla(r3(q, bp, seq_p), r3(k, bp, seq_p), r3(v, bp, seq_p), r3(lg, bp, seq_p), s_meta, bp, 1, GLA_CHUNK)
    yg, hp_re, hp_im = _s5(ug, s5m[S5_BLOCK], hm_re, hm_im, bp, 32)
    y_prompt = finish(xp, o.reshape(bp * seq_p, D_GLA), g, yg, u, 512, S5_BLOCK)

    xs = x_sample.reshape(bs * seq_s, D_MODEL)
    to_g = lambda t: pad_state(jnp.swapaxes(t[0], 0, 1))
    q, k, v, g, lg, u, ug = _in_proj(xs, *proj_w, 512, seq_s)
    o, s_s = _gla(r3(q, bs, seq_s), r3(k, bs, seq_s), r3(v, bs, seq_s), r3(lg, bs, seq_s),
                  state_gla[0].reshape(bs, GLA_KDIM, GLA_DV), 32, 16, seq_s)
    yg, hs_re, hs_im = _s5(ug, s5m[seq_s], to_g(state_s5_re), to_g(state_s5_im), 1, bs)
    y_sample = finish(xs, o.reshape(bs * seq_s, D_GLA), g, yg, u, 512, seq_s)

    gla_out = lambda s, b: s.reshape(1, b, GLA_HEADS, GLA_DK, GLA_DV)
    s5_out = lambda h: jnp.swapaxes(h[:, :, :S5_STATE], 0, 1)[None]
    return (y_prompt.reshape(bp, seq_p, D_MODEL), y_sample.reshape(bs, seq_s, D_MODEL),
            gla_out(s_p, bp), s5_out(hp_re), s5_out(hp_im),
            gla_out(s_s, bs), s5_out(hs_re), s5_out(hs_im))
```

```python
import functools

import jax
import jax.numpy as jnp
from jax import lax
from jax.experimental import pallas as pl
from jax.experimental.pallas import tpu as pltpu

F32 = jnp.float32
BF16 = jnp.bfloat16

D_MODEL = 1024
D_GLA = 512
GLA_HEADS = 4
GLA_DV = 128
GLA_DK = 64
GLA_KDIM = 256
GATE_RANK = 16
GATE_NORM = 16.0
GLA_CHUNK = 64
D_S5 = 512
S5_GROUP = 16
S5_GROUPS = 32
S5_STATE = 64
N_META = 16
D_FF = 2816
EPS = 1e-6
LANES = 128
S5_BLOCK = 16
SLAB_GROUPS = LANES // S5_GROUP
PROJ_W = 2176
FF_CHUNK = 256
VMEM_LIMIT = 48 * 1024 * 1024


def _rms(x, g):
    return x * lax.rsqrt(jnp.mean(x * x, axis=-1, keepdims=True) + EPS) * g


def _dot(a, b):
    return jnp.dot(a, b, preferred_element_type=F32)


def _dot_nt(a, b):
    return lax.dot_general(a, b, (((1,), (1,)), ((), ())), preferred_element_type=F32)


def _dot_tn(a, b):
    return lax.dot_general(a, b, (((0,), (0,)), ((), ())), preferred_element_type=F32)


def _const_spec(shape):
    zeros = (0,) * len(shape)
    return pl.BlockSpec(shape, lambda *_: zeros)


PREP_GROUPS = 8


def _cmul(a, b):
    return a[0] * b[0] - a[1] * b[1], a[0] * b[1] + a[1] * b[0]


def _unit_powers(c1, s1, expo, n_bits):
    acc = (jnp.ones_like(c1), jnp.zeros_like(c1))
    base = (c1, s1)
    squares = [base]
    for bit in range(n_bits):
        take = ((expo >> bit) & 1) == 1
        nxt = _cmul(acc, base)
        acc = (jnp.where(take, nxt[0], acc[0]), jnp.where(take, nxt[1], acc[1]))
        base = _cmul(base, base)
        squares.append(base)
    return acc, squares


def _s5_prep_kernel(arc_ref, aic_ref, arr_ref, air_ref, ldt_ref, btr_ref, bti_ref, ctr_ref, cti_ref,
                    mi16_ref, mi8_ref, inr16_ref, ini16_ref, inr8_ref, ini8_ref,
                    outr16_ref, outi16_ref, outr8_ref, outi8_ref, a16r_ref, a16i_ref, a8r_ref, a8i_ref):
    n, q = S5_STATE, S5_BLOCK
    w = q * S5_GROUP
    hp = lax.Precision.HIGHEST
    lane = lax.broadcasted_iota(jnp.int32, (n, w), 1)
    t_blk = lane >> 4
    row_s = lax.broadcasted_iota(jnp.int32, (w, 1), 0) >> 4
    zpad = lambda x, axis: jnp.concatenate([x, jnp.zeros_like(x)], axis=axis)
    for g in range(PREP_GROUPS):
        dt = jnp.exp(ldt_ref[g])
        lam_re = jnp.minimum(arc_ref[g], -1e-4)
        ang = aic_ref[g] * dt
        c1, s1 = jnp.cos(ang), jnp.sin(ang)
        unit, _ = _unit_powers(c1, s1, t_blk, 4)
        pm = jnp.exp(t_blk.astype(F32) * (lam_re * dt))
        pk = (pm * unit[0], pm * unit[1])
        mag = jnp.exp(lam_re * dt)
        p1 = _cmul(pk, (mag * c1, mag * s1))
        ct = (ctr_ref[g], cti_ref[g])
        g0 = _cmul(ct, pk)
        mo = _cmul(ct, p1)
        mo_re, mo_im = zpad(mo[0], 0), zpad(-mo[1], 0)
        outr16_ref[g] = mo_re.astype(BF16)
        outi16_ref[g] = mo_im.astype(BF16)
        outr8_ref[g] = mo_re[:, :w // 2].astype(BF16)
        outi8_ref[g] = mo_im[:, :w // 2].astype(BF16)
        lam_re_r = jnp.minimum(arr_ref[g], -1e-4)
        lam_im_r = air_ref[g]
        ang_r = lam_im_r * dt
        c1r, s1r = jnp.cos(ang_r), jnp.sin(ang_r)
        mag_r = jnp.exp(lam_re_r * dt)
        ab = (mag_r * c1r, mag_r * s1r)
        den = lam_re_r * lam_re_r + lam_im_r * lam_im_r
        nr, ni = ab[0] - 1.0, ab[1]
        f = ((nr * lam_re_r + ni * lam_im_r) / den, (ni * lam_re_r - nr * lam_im_r) / den)
        bbt = _cmul(f, (btr_ref[g], bti_ref[g]))
        for steps, inr_ref, ini_ref, ar_ref, ai_ref in ((q, inr16_ref, ini16_ref, a16r_ref, a16i_ref),
                                                        (q // 2, inr8_ref, ini8_ref, a8r_ref, a8i_ref)):
            rows = steps * S5_GROUP
            expo = (steps - 1) - row_s[:rows]
            unit_r, squares = _unit_powers(c1r, s1r, expo, 4)
            pm_r = jnp.exp(expo.astype(F32) * (lam_re_r * dt))
            e = _cmul((bbt[0][:rows], bbt[1][:rows]), (pm_r * unit_r[0], pm_r * unit_r[1]))
            inr_ref[g] = zpad(e[0], 1).astype(BF16)
            ini_ref[g] = zpad(e[1], 1).astype(BF16)
            hop = squares[steps.bit_length() - 1]
            hop_m = jnp.exp(float(steps) * (lam_re_r * dt))
            ar_ref[g] = zpad(hop_m * hop[0], 1)
            ai_ref[g] = zpad(hop_m * hop[1], 1)
        t0 = (jnp.dot(bbt[0][:S5_GROUP], g0[0], precision=hp, preferred_element_type=F32)
              - jnp.dot(bbt[1][:S5_GROUP], g0[1], precision=hp, preferred_element_type=F32))
        lane_t = lax.broadcasted_iota(jnp.int32, t0.shape, 1) >> 4
        for s in range(q):
            blk = t0 if s == 0 else jnp.where(lane_t >= s, pltpu.roll(t0, S5_GROUP * s, 1), 0.0)
            mi16_ref[g, s * S5_GROUP:(s + 1) * S5_GROUP, :] = blk.astype(BF16)
            if s < q // 2:
                mi8_ref[g, s * S5_GROUP:(s + 1) * S5_GROUP, :] = blk[:, :w // 2].astype(BF16)


def _s5_prep(a_re, a_im, b_re, b_im, c_re, c_im, log_dt):
    g, n, j, q = S5_GROUPS, S5_STATE, S5_GROUP, S5_BLOCK
    w, h, pg = q * j, q * j // 2, PREP_GROUPS
    col = lambda t: t.reshape(g, n, 1)
    rowv = lambda t: t.reshape(g, 1, n)
    bt = lambda t: jnp.tile(jnp.swapaxes(t, 1, 2), (1, q, 1))
    ct = lambda t: jnp.tile(jnp.swapaxes(t, 1, 2), (1, 1, q))
    spec = lambda a, b_: pl.BlockSpec((pg, a, b_), lambda i: (i, 0, 0))
    shapes = [(w, w), (h, h), (w, LANES), (w, LANES), (h, LANES), (h, LANES),
              (LANES, w), (LANES, w), (LANES, h), (LANES, h)]
    outs = pl.pallas_call(
        _s5_prep_kernel,
        grid=(g // pg,),
        in_specs=[spec(n, 1), spec(n, 1), spec(1, n), spec(1, n), spec(1, 1),
                  spec(w, n), spec(w, n), spec(n, w), spec(n, w)],
        out_specs=[spec(*s) for s in shapes] + [spec(1, LANES)] * 4,
        out_shape=[jax.ShapeDtypeStruct((g,) + s, BF16) for s in shapes]
                  + [jax.ShapeDtypeStruct((g, 1, LANES), F32)] * 4,
        compiler_params=pltpu.CompilerParams(dimension_semantics=("parallel",)),
        name="s5_prep",
    )(col(a_re), col(a_im), rowv(a_re), rowv(a_im), log_dt.reshape(g, 1, 1),
      bt(b_re), bt(b_im), ct(c_re), ct(c_im))
    mi16, mi8, inr16, ini16, inr8, ini8, outr16, outi16, outr8, outi8, a16r, a16i, a8r, a8i = outs
    return {q: dict(m_intra=mi16, m_in_re=inr16, m_in_im=ini16, m_out_re=outr16, m_out_im=outi16,
                    a_re=a16r, a_im=a16i),
            q // 2: dict(m_intra=mi8, m_in_re=inr8, m_in_im=ini8, m_out_re=outr8, m_out_im=outi8,
                         a_re=a8r, a_im=a8i)}


def _lane_block_transpose(a):
    a = list(a)
    blk = lax.broadcasted_iota(jnp.int32, a[0].shape, 1) >> 4
    for d in (4, 2, 1):
        upper = (blk & d) != 0
        for r in range(8):
            if r & d:
                continue
            lo, hi = a[r], a[r + d]
            a[r] = jnp.where(upper, pltpu.roll(hi, 16 * d, 1), lo)
            a[r + d] = jnp.where(upper, hi, pltpu.roll(lo, LANES - 16 * d, 1))
    return a


def _in_proj_kernel(x_ref, gpre_ref, w_ref, wgk_ref, bgk_ref,
                    q_ref, k_ref, v_ref, g_ref, lg_ref, u_ref, ug_ref, u_scr, *, steps):
    h = _rms(x_ref[...], gpre_ref[...])
    proj = _dot(h.astype(BF16), w_ref[...])
    q_ref[...] = proj[:, 0:256]
    k_ref[...] = proj[:, 256:512]
    v_ref[...] = proj[:, 512:1024]
    g_ref[...] = proj[:, 1024:1536]
    u = proj[:, 1536:2048]
    u_ref[...] = u
    z = _dot(proj[:, 2048:PROJ_W].astype(BF16), wgk_ref[...]) + bgk_ref[...]
    lg_ref[...] = jax.nn.log_sigmoid(z) * (1.0 / GATE_NORM)

    nr = x_ref.shape[0] // steps
    rt = min(nr, 16)
    for s in range(D_S5 // LANES):
        u_scr[s] = u[:, s * LANES:(s + 1) * LANES]
    for s in range(D_S5 // LANES):
        for hf in range(steps // 8):
            for r0 in range(0, nr, rt):
                a = [u_scr[s, pl.ds(r0 * steps + hf * 8 + t, rt, stride=steps), :] for t in range(8)]
                per_group = _lane_block_transpose(a)
                for g in range(SLAB_GROUPS):
                    ug_ref[s * SLAB_GROUPS + g, r0:r0 + rt, hf * LANES:(hf + 1) * LANES] = (
                        per_group[g].astype(BF16))


def _in_proj(x, gpre, w_p, wgk_p, bgk, tm, steps):
    t = x.shape[0]
    row = lambda w: pl.BlockSpec((tm, w), lambda i: (i, 0))
    widths = (256, 256, 512, 512, 256, 512)
    wg = steps * S5_GROUP
    return pl.pallas_call(
        functools.partial(_in_proj_kernel, steps=steps),
        grid=(t // tm,),
        in_specs=[row(D_MODEL), _const_spec((1, D_MODEL)), _const_spec((D_MODEL, PROJ_W)),
                  _const_spec((LANES, GLA_KDIM)), _const_spec((1, GLA_KDIM))],
        out_specs=[row(w) for w in widths]
                  + [pl.BlockSpec((S5_GROUPS, tm // steps, wg), lambda i: (0, i, 0))],
        out_shape=[jax.ShapeDtypeStruct((t, w), F32) for w in widths]
                  + [jax.ShapeDtypeStruct((S5_GROUPS, t // steps, wg), BF16)],
        scratch_shapes=[pltpu.VMEM((D_S5 // LANES, tm, LANES), F32)],
        compiler_params=pltpu.CompilerParams(dimension_semantics=("parallel",),
                                             vmem_limit_bytes=VMEM_LIMIT),
        name="in_proj",
    )(x, gpre, w_p, wgk_p, bgk)


def _gla_kernel(q_ref, k_ref, v_ref, lg_ref, s0_ref, o_ref, s_ref, *, bb, sg, chunk):
    r = sg * chunk
    shift = chunk.bit_length() - 1

    @pl.when(pl.program_id(1) == 0)
    def _():
        s_ref[...] = s0_ref[...]

    ri = lax.broadcasted_iota(jnp.int32, (r, r), 0)
    ci = lax.broadcasted_iota(jnp.int32, (r, r), 1)
    causal = ((ri >> shift) == (ci >> shift)) & (ri >= ci)
    tri_bf = jnp.where(causal, 1.0, 0.0).astype(BF16)
    lane_head = lax.broadcasted_iota(jnp.int32, (r, GLA_KDIM), 1) >> 6
    eye = (lax.broadcasted_iota(jnp.int32, (GLA_KDIM, GLA_KDIM), 0)
           == lax.broadcasted_iota(jnp.int32, (GLA_KDIM, GLA_KDIM), 1))

    for gi in range(bb // sg):
        seqs = slice(gi * sg, (gi + 1) * sg)
        lg = lg_ref[seqs].reshape(r, GLA_KDIM)
        lg_hi = lg.astype(BF16)
        lg_lo = (lg - lg_hi.astype(F32)).astype(BF16)
        b = _dot(tri_bf, lg_hi) + _dot(tri_bf, lg_lo)
        lasts = [b[(i + 1) * chunk - 1:(i + 1) * chunk, :] for i in range(sg)]
        bl = jnp.concatenate([jnp.broadcast_to(t, (chunk, GLA_KDIM)) for t in lasts], axis=0)
        q = q_ref[seqs].reshape(r, GLA_KDIM)
        k = k_ref[seqs].reshape(r, GLA_KDIM)
        v = v_ref[seqs].reshape(r, D_GLA)
        qd = q * (GLA_DK ** -0.5) * jnp.exp(b)
        ki = (k * jnp.exp(-b)).astype(BF16)
        ke = k * jnp.exp(bl - b)

        qd_h, ke_h, v_h, o_intra = [], [], [], []
        for h in range(GLA_HEADS):
            hm = lane_head == h
            qd_h.append(jnp.where(hm, qd, 0.0).astype(BF16))
            ke_h.append(jnp.where(hm, ke, 0.0).astype(BF16))
            v_h.append(v[:, h * GLA_DV:(h + 1) * GLA_DV].astype(BF16))
            att = jnp.where(causal, _dot_nt(qd_h[h], ki), 0.0).astype(BF16)
            o_intra.append(_dot(att, v_h[h]))

        for i in range(sg):
            seq = gi * sg + i
            rows = slice(i * chunk, (i + 1) * chunk)
            s_old = s_ref[seq]
            s_bf = s_old.astype(BF16)
            dcol = jnp.sum(jnp.where(eye, jnp.exp(lasts[i]), 0.0), axis=1, keepdims=True)
            acc = dcol * s_old
            for h in range(GLA_HEADS):
                o_ref[seq, :, h * GLA_DV:(h + 1) * GLA_DV] = o_intra[h][rows] + _dot(qd_h[h][rows], s_bf)
                acc = acc + _dot_tn(ke_h[h][rows], v_h[h][rows])
            s_ref[seq] = acc


def _gla(q, k, v, lg, s0, bb, sg, chunk):
    b, l, _ = q.shape
    blk = lambda w: pl.BlockSpec((bb, chunk, w), lambda i, c: (i, c, 0))
    sspec = pl.BlockSpec((bb, GLA_KDIM, GLA_DV), lambda i, c: (i, 0, 0))
    return pl.pallas_call(
        functools.partial(_gla_kernel, bb=bb, sg=sg, chunk=chunk),
        grid=(b // bb, l // chunk),
        in_specs=[blk(GLA_KDIM), blk(GLA_KDIM), blk(D_GLA), blk(GLA_KDIM), sspec],
        out_specs=[blk(D_GLA), sspec],
        out_shape=[jax.ShapeDtypeStruct((b, l, D_GLA), F32),
                   jax.ShapeDtypeStruct((b, GLA_KDIM, GLA_DV), F32)],
        compiler_params=pltpu.CompilerParams(dimension_semantics=("parallel", "arbitrary"),
                                             vmem_limit_bytes=VMEM_LIMIT),
        name=f"gla_c{chunk}",
    )(q, k, v, lg, s0)


def _s5_kernel(ug_ref, mi_ref, minr_ref, mini_ref, mor_ref, moi_ref, ar_ref, ai_ref,
               h0r_ref, h0i_ref, yg_ref, hfr_ref, hfi_ref, vr_s, vi_s, hr_s, hi_s, *, nb, cb):
    gs = SLAB_GROUPS
    rows = nb * cb
    batch = cb if nb == 1 else nb
    n_blocks = rows // batch
    interleave = nb > 1 and cb > 1

    @pl.when(pl.program_id(1) == 0)
    def _():
        hfr_ref[...] = h0r_ref[...]
        hfi_ref[...] = h0i_ref[...]

    def to_scan_order(ref, g, val):
        if not interleave:
            ref[g] = val
        else:
            for b in range(nb):
                ref[g, pl.ds(b, cb, stride=nb), :] = val[b * cb:(b + 1) * cb]

    def from_scan_order(ref, g):
        if not interleave:
            return ref[g]
        return jnp.concatenate([ref[g, pl.ds(b, cb, stride=nb), :] for b in range(nb)], axis=0)

    ub = [ug_ref[g].reshape(rows, ug_ref.shape[-1]) for g in range(gs)]
    for g in range(gs):
        to_scan_order(vr_s, g, _dot(ub[g], minr_ref[g]))
        to_scan_order(vi_s, g, _dot(ub[g], mini_ref[g]))
    ar = [ar_ref[g] for g in range(gs)]
    ai = [ai_ref[g] for g in range(gs)]

    def body(c, carry):
        rows = pl.ds(pl.multiple_of(c * batch, batch), batch)
        new = []
        for g in range(gs):
            hr, hi = carry[2 * g], carry[2 * g + 1]
            hr_s[g, rows, :] = hr
            hi_s[g, rows, :] = hi
            new.append(ar[g] * hr - ai[g] * hi + vr_s[g, rows, :])
            new.append(ar[g] * hi + ai[g] * hr + vi_s[g, rows, :])
        return tuple(new)

    init = tuple(ref[g] for g in range(gs) for ref in (hfr_ref, hfi_ref))
    fin = lax.fori_loop(0, n_blocks, body, init)
    for g in range(gs):
        hfr_ref[g] = fin[2 * g]
        hfi_ref[g] = fin[2 * g + 1]
        y = (_dot(ub[g], mi_ref[g]) + _dot(from_scan_order(hr_s, g).astype(BF16), mor_ref[g])
             + _dot(from_scan_order(hi_s, g).astype(BF16), moi_ref[g]))
        yg_ref[g] = y.reshape(yg_ref.shape[1:])


def _s5(ug, m, h0_re, h0_im, nb, cb):
    g_all, r, w = ug.shape
    nc = r // nb
    batch = cb if nb == 1 else nb
    n, gs = LANES, SLAB_GROUPS
    uspec = pl.BlockSpec((gs, nb, cb, w), lambda s, i: (s, 0, i, 0))
    per_s = lambda a, b_: pl.BlockSpec((gs, a, b_), lambda s, i: (s, 0, 0))
    if nb == 1:
        hspec = pl.BlockSpec((gs, batch, n), lambda s, i: (s, i, 0))
    else:
        hspec = per_s(batch, n)
    yg, hf_re, hf_im = pl.pallas_call(
        functools.partial(_s5_kernel, nb=nb, cb=cb),
        grid=(g_all // gs, nc // cb),
        in_specs=[uspec, per_s(w, w), per_s(w, n), per_s(w, n), per_s(n, w), per_s(n, w),
                  per_s(1, n), per_s(1, n), hspec, hspec],
        out_specs=[uspec, hspec, hspec],
        out_shape=[jax.ShapeDtypeStruct((g_all, nb, nc, w), F32),
                   jax.ShapeDtypeStruct(h0_re.shape, F32),
                   jax.ShapeDtypeStruct(h0_im.shape, F32)],
        scratch_shapes=[pltpu.VMEM((gs, nb * cb, n), F32)] * 4,
        compiler_params=pltpu.CompilerParams(dimension_semantics=("parallel", "arbitrary"),
                                             vmem_limit_bytes=VMEM_LIMIT),
        name=f"s5_w{w}",
    )(ug.reshape(g_all, nb, nc, w), m["m_intra"], m["m_in_re"], m["m_in_im"], m["m_out_re"], m["m_out_im"],
      m["a_re"], m["a_im"], h0_re, h0_im)
    return yg.reshape(g_all, r, w), hf_re, hf_im


def _mix_ffn_kernel(x_ref, o_ref, g_ref, yg_ref, u_ref, dsk_ref, gn_ref, wglu_ref, s5n_ref, wo_ref, gpost_ref,
                    gpre_ref, wg_ref, wu_ref, wd_ref, gpostf_ref, out_ref, y_scr, x1_scr, *, steps):
    i = pl.program_id(0)

    def mix_into(slot):
        nr = x_ref.shape[0] // steps
        rt = min(nr, 8)
        for s in range(D_S5 // LANES):
            for hf in range(steps // 8):
                for r0 in range(0, nr, rt):
                    a = [yg_ref[s * SLAB_GROUPS + g, r0:r0 + rt, hf * LANES:(hf + 1) * LANES]
                         for g in range(SLAB_GROUPS)]
                    per_step = _lane_block_transpose(a)
                    for t in range(8):
                        y_scr[s, pl.ds(r0 * steps + hf * 8 + t, rt, stride=steps), :] = per_step[t]
        y5 = jnp.concatenate([y_scr[s] for s in range(D_S5 // LANES)], axis=1) + dsk_ref[...] * u_ref[...]
        o = o_ref[...]
        gn = gn_ref[...]
        heads = []
        for h in range(GLA_HEADS):
            heads.append(_rms(o[:, h * GLA_DV:(h + 1) * GLA_DV], gn))
        og = jnp.concatenate(heads, axis=1) * jax.nn.silu(g_ref[...])
        y = jax.nn.gelu(y5)
        y = y * jax.nn.sigmoid(_dot(y.astype(BF16), wglu_ref[...]))
        y = _rms(y, s5n_ref[...])
        mix = _dot(og.astype(BF16), wo_ref[:D_GLA, :]) + _dot(y.astype(BF16), wo_ref[D_GLA:, :])
        x1_scr[slot] = x_ref[...] + _rms(mix, gpost_ref[...])

    def ffn_from(slot):
        x = x1_scr[slot]
        h = _rms(x, gpre_ref[...]).astype(BF16)
        acc = jnp.zeros(x.shape, F32)
        for c in range(D_FF // FF_CHUNK):
            cols = slice(c * FF_CHUNK, (c + 1) * FF_CHUNK)
            act = jax.nn.silu(_dot(h, wg_ref[:, cols])) * _dot(h, wu_ref[:, cols])
            acc = acc + _dot(act.astype(BF16), wd_ref[cols, :])
        out_ref[...] = x + _rms(acc, gpostf_ref[...])

    @pl.when(i == 0)
    def _():
        mix_into(0)

    @pl.when(i > 0)
    def _():
        ffn_from((i - 1) & 1)
        mix_into(i & 1)


def _mix_ffn(x, o, g, yg, u, dsk, gn, wglu, s5n, wo, gpost, gpre, wg, wu, wd, gpostf, tm, steps):
    t = x.shape[0]
    n = t // tm
    cur = lambda i: jnp.minimum(i, n - 1)
    row = lambda w: pl.BlockSpec((tm, w), lambda i: (cur(i), 0))
    fixed = lambda shape: pl.BlockSpec(shape, lambda i: (0,) * len(shape), pipeline_mode=pl.Buffered(1))
    return pl.pallas_call(
        functools.partial(_mix_ffn_kernel, steps=steps),
        grid=(n + 1,),
        in_specs=[row(D_MODEL), row(D_GLA), row(D_GLA),
                  pl.BlockSpec((S5_GROUPS, tm // steps, steps * S5_GROUP), lambda i: (0, cur(i), 0)),
                  row(D_S5), fixed((1, D_S5)),
                  fixed((1, GLA_DV)), fixed((D_S5, D_S5)), fixed((1, D_S5)),
                  fixed((D_GLA + D_S5, D_MODEL)), fixed((1, D_MODEL)),
                  fixed((1, D_MODEL)), fixed((D_MODEL, D_FF)), fixed((D_MODEL, D_FF)),
                  fixed((D_FF, D_MODEL)), fixed((1, D_MODEL))],
        out_specs=pl.BlockSpec((tm, D_MODEL), lambda i: (jnp.maximum(i - 1, 0), 0)),
        out_shape=jax.ShapeDtypeStruct((t, D_MODEL), F32),
        scratch_shapes=[pltpu.VMEM((D_S5 // LANES, tm, LANES), F32), pltpu.VMEM((2, tm, D_MODEL), F32)],
        compiler_params=pltpu.CompilerParams(dimension_semantics=("arbitrary",),
                                             vmem_limit_bytes=VMEM_LIMIT),
        name="mix_ffn",
    )(x, o, g, yg, u, dsk, gn, wglu, s5n, wo, gpost, gpre, wg, wu, wd, gpostf)


def kernel(x_prompt, x_sample, state_gla, state_s5_re, state_s5_im, meta_tokens, g_pre_mix, w_in, w_gk2, b_gk, gla_norm, s5_a_re, s5_a_im, s5_b_re, s5_b_im, s5_c_re, s5_c_im, s5_d, s5_log_dt, w_s5_glu, s5_norm, w_o, g_post_mix, g_pre_ffn, w_gate, w_up, w_down, g_post_ffn):
    assert g_pre_mix.shape[0] == 1, "single-layer step"
    bp, seq_p, _ = x_prompt.shape
    bs, seq_s, _ = x_sample.shape
    row = lambda t: t[0].reshape(1, -1)

    w = w_in[0]
    c3, c4 = 1536, 1536 + GATE_RANK
    w_p = jnp.concatenate([w[:, :c3], w[:, c4:], w[:, c3:c4],
                           jnp.zeros((D_MODEL, LANES - GATE_RANK), F32)], axis=1).astype(BF16)
    wgk_p = jnp.concatenate([w_gk2[0], jnp.zeros((LANES - GATE_RANK, GLA_KDIM), F32)], axis=0).astype(BF16)
    wo_bf = w_o[0].astype(BF16)
    s5m = _s5_prep(s5_a_re[0], s5_a_im[0], s5_b_re[0], s5_b_im[0], s5_c_re[0], s5_c_im[0], s5_log_dt[0])
    proj_w = (row(g_pre_mix), w_p, wgk_p, row(b_gk))
    pad_state = lambda h: jnp.pad(h, ((0, 0), (0, 0), (0, LANES - S5_STATE)))

    def finish(x, o, g, yg, u, tm, steps):
        return _mix_ffn(x, o, g, yg, u, row(s5_d), row(gla_norm), w_s5_glu[0].astype(BF16), row(s5_norm),
                        wo_bf, row(g_post_mix), row(g_pre_ffn), w_gate[0].astype(BF16),
                        w_up[0].astype(BF16), w_down[0].astype(BF16), row(g_post_ffn), tm, steps)

    xm = jnp.broadcast_to(meta_tokens[None], (bp, N_META, D_MODEL)).reshape(bp * N_META, D_MODEL)
    q, k, v, _, lg, _, ug = _in_proj(xm, *proj_w, bp * N_META, S5_BLOCK)
    r3 = lambda t, b, l: t.reshape(b, l, t.shape[-1])
    _, s_meta = _gla(r3(q, bp, N_META), r3(k, bp, N_META), r3(v, bp, N_META), r3(lg, bp, N_META),
                     jnp.zeros((bp, GLA_KDIM, GLA_DV), F32), bp, bp, N_META)
    zh = jnp.zeros((S5_GROUPS, bp, LANES), F32)
    _, hm_re, hm_im = _s5(ug, s5m[S5_BLOCK], zh, zh, 1, bp)

    xp = x_prompt.reshape(bp * seq_p, D_MODEL)
    q, k, v, g, lg, u, ug = _in_proj(xp, *proj_w, 512, S5_BLOCK)
    o, s_p = _gla(r3(q, bp, seq_p), r3(k, bp, seq_p), r3(v, bp, seq_p), r3(lg, bp, seq_p), s_meta, bp, 1, GLA_CHUNK)
    yg, hp_re, hp_im = _s5(ug, s5m[S5_BLOCK], hm_re, hm_im, bp, 32)
    y_prompt = finish(xp, o.reshape(bp * seq_p, D_GLA), g, yg, u, 512, S5_BLOCK)

    xs = x_sample.reshape(bs * seq_s, D_MODEL)
    to_g = lambda t: pad_state(jnp.swapaxes(t[0], 0, 1))
    q, k, v, g, lg, u, ug = _in_proj(xs, *proj_w, 512, seq_s)
    o, s_s = _gla(r3(q, bs, seq_s), r3(k, bs, seq_s), r3(v, bs, seq_s), r3(lg, bs, seq_s),
                  state_gla[0].reshape(bs, GLA_KDIM, GLA_DV), 32, 16, seq_s)
    yg, hs_re, hs_im = _s5(ug, s5m[seq_s], to_g(state_s5_re), to_g(state_s5_im), 1, bs)
    y_sample = finish(xs, o.reshape(bs * seq_s, D_GLA), g, yg, u, 512, seq_s)

    gla_out = lambda s, b: s.reshape(1, b, GLA_HEADS, GLA_DK, GLA_DV)
    s5_out = lambda h: jnp.swapaxes(h[:, :, :S5_STATE], 0, 1)[None]
    return (y_prompt.reshape(bp, seq_p, D_MODEL), y_sample.reshape(bs, seq_s, D_MODEL),
            gla_out(s_p, bp), s5_out(hp_re), s5_out(hp_im),
            gla_out(s_s, bs), s5_out(hs_re), s5_out(hs_im))
```

```python
import functools

import jax
import jax.numpy as jnp
from jax import lax
from jax.experimental import pallas as pl
from jax.experimental.pallas import tpu as pltpu

F32 = jnp.float32
BF16 = jnp.bfloat16

D_MODEL = 1024
D_GLA = 512
GLA_HEADS = 4
GLA_DV = 128
GLA_DK = 64
GLA_KDIM = 256
GATE_RANK = 16
GATE_NORM = 16.0
GLA_CHUNK = 64
D_S5 = 512
S5_GROUP = 16
S5_GROUPS = 32
S5_STATE = 64
N_META = 16
D_FF = 2816
EPS = 1e-6
LANES = 128
S5_BLOCK = 16
SLAB_GROUPS = LANES // S5_GROUP
PROJ_W = 2176
FF_CHUNK = 256
VMEM_LIMIT = 48 * 1024 * 1024


def _rms(x, g):
    return x * lax.rsqrt(jnp.mean(x * x, axis=-1, keepdims=True) + EPS) * g


def _dot(a, b):
    return jnp.dot(a, b, preferred_element_type=F32)


def _dot_nt(a, b):
    return lax.dot_general(a, b, (((1,), (1,)), ((), ())), preferred_element_type=F32)


def _dot_tn(a, b):
    return lax.dot_general(a, b, (((0,), (0,)), ((), ())), preferred_element_type=F32)


def _const_spec(shape):
    zeros = (0,) * len(shape)
    return pl.BlockSpec(shape, lambda *_: zeros)


PREP_GROUPS = 8


def _cmul(a, b):
    return a[0] * b[0] - a[1] * b[1], a[0] * b[1] + a[1] * b[0]


def _unit_powers(c1, s1, expo, n_bits):
    acc = (jnp.ones_like(c1), jnp.zeros_like(c1))
    base = (c1, s1)
    squares = [base]
    for bit in range(n_bits):
        take = ((expo >> bit) & 1) == 1
        nxt = _cmul(acc, base)
        acc = (jnp.where(take, nxt[0], acc[0]), jnp.where(take, nxt[1], acc[1]))
        base = _cmul(base, base)
        squares.append(base)
    return acc, squares


def _s5_prep_kernel(arc_ref, aic_ref, arr_ref, air_ref, ldt_ref, btr_ref, bti_ref, ctr_ref, cti_ref,
                    mi16_ref, mi8_ref, inr16_ref, ini16_ref, inr8_ref, ini8_ref,
                    outr16_ref, outi16_ref, outr8_ref, outi8_ref, a16r_ref, a16i_ref, a8r_ref, a8i_ref):
    n, q = S5_STATE, S5_BLOCK
    w = q * S5_GROUP
    hp = lax.Precision.HIGHEST
    lane = lax.broadcasted_iota(jnp.int32, (n, w), 1)
    t_blk = lane >> 4
    row_s = lax.broadcasted_iota(jnp.int32, (w, 1), 0) >> 4
    zpad = lambda x, axis: jnp.concatenate([x, jnp.zeros_like(x)], axis=axis)
    for g in range(PREP_GROUPS):
        dt = jnp.exp(ldt_ref[g])
        lam_re = jnp.minimum(arc_ref[g], -1e-4)
        ang = aic_ref[g] * dt
        c1, s1 = jnp.cos(ang), jnp.sin(ang)
        unit, _ = _unit_powers(c1, s1, t_blk, 4)
        pm = jnp.exp(t_blk.astype(F32) * (lam_re * dt))
        pk = (pm * unit[0], pm * unit[1])
        mag = jnp.exp(lam_re * dt)
        p1 = _cmul(pk, (mag * c1, mag * s1))
        ct = (ctr_ref[g], cti_ref[g])
        g0 = _cmul(ct, pk)
        mo = _cmul(ct, p1)
        mo_re, mo_im = zpad(mo[0], 0), zpad(-mo[1], 0)
        outr16_ref[g] = mo_re.astype(BF16)
        outi16_ref[g] = mo_im.astype(BF16)
        outr8_ref[g] = mo_re[:, :w // 2].astype(BF16)
        outi8_ref[g] = mo_im[:, :w // 2].astype(BF16)
        lam_re_r = jnp.minimum(arr_ref[g], -1e-4)
        lam_im_r = air_ref[g]
        ang_r = lam_im_r * dt
        c1r, s1r = jnp.cos(ang_r), jnp.sin(ang_r)
        mag_r = jnp.exp(lam_re_r * dt)
        ab = (mag_r * c1r, mag_r * s1r)
        den = lam_re_r * lam_re_r + lam_im_r * lam_im_r
        nr, ni = ab[0] - 1.0, ab[1]
        f = ((nr * lam_re_r + ni * lam_im_r) / den, (ni * lam_re_r - nr * lam_im_r) / den)
        bbt = _cmul(f, (btr_ref[g], bti_ref[g]))
        for steps, inr_ref, ini_ref, ar_ref, ai_ref in ((q, inr16_ref, ini16_ref, a16r_ref, a16i_ref),
                                                        (q // 2, inr8_ref, ini8_ref, a8r_ref, a8i_ref)):
            rows = steps * S5_GROUP
            expo = (steps - 1) - row_s[:rows]
            unit_r, squares = _unit_powers(c1r, s1r, expo, 4)
            pm_r = jnp.exp(expo.astype(F32) * (lam_re_r * dt))
            e = _cmul((bbt[0][:rows], bbt[1][:rows]), (pm_r * unit_r[0], pm_r * unit_r[1]))
            inr_ref[g] = zpad(e[0], 1).astype(BF16)
            ini_ref[g] = zpad(e[1], 1).astype(BF16)
            hop = squares[steps.bit_length() - 1]
            hop_m = jnp.exp(float(steps) * (lam_re_r * dt))
            ar_ref[g] = zpad(hop_m * hop[0], 1)
            ai_ref[g] = zpad(hop_m * hop[1], 1)
        t0 = (jnp.dot(bbt[0][:S5_GROUP], g0[0], precision=hp, preferred_element_type=F32)
              - jnp.dot(bbt[1][:S5_GROUP], g0[1], precision=hp, preferred_element_type=F32))
        lane_t = lax.broadcasted_iota(jnp.int32, t0.shape, 1) >> 4
        for s in range(q):
            blk = t0 if s == 0 else jnp.where(lane_t >= s, pltpu.roll(t0, S5_GROUP * s, 1), 0.0)
            mi16_ref[g, s * S5_GROUP:(s + 1) * S5_GROUP, :] = blk.astype(BF16)
            if s < q // 2:
                mi8_ref[g, s * S5_GROUP:(s + 1) * S5_GROUP, :] = blk[:, :w // 2].astype(BF16)


def _s5_prep(a_re, a_im, b_re, b_im, c_re, c_im, log_dt):
    g, n, j, q = S5_GROUPS, S5_STATE, S5_GROUP, S5_BLOCK
    w, h, pg = q * j, q * j // 2, PREP_GROUPS
    col = lambda t: t.reshape(g, n, 1)
    rowv = lambda t: t.reshape(g, 1, n)
    bt = lambda t: jnp.tile(jnp.swapaxes(t, 1, 2), (1, q, 1))
    ct = lambda t: jnp.tile(jnp.swapaxes(t, 1, 2), (1, 1, q))
    spec = lambda a, b_: pl.BlockSpec((pg, a, b_), lambda i: (i, 0, 0))
    shapes = [(w, w), (h, h), (w, LANES), (w, LANES), (h, LANES), (h, LANES),
              (LANES, w), (LANES, w), (LANES, h), (LANES, h)]
    outs = pl.pallas_call(
        _s5_prep_kernel,
        grid=(g // pg,),
        in_specs=[spec(n, 1), spec(n, 1), spec(1, n), spec(1, n), spec(1, 1),
                  spec(w, n), spec(w, n), spec(n, w), spec(n, w)],
        out_specs=[spec(*s) for s in shapes] + [spec(1, LANES)] * 4,
        out_shape=[jax.ShapeDtypeStruct((g,) + s, BF16) for s in shapes]
                  + [jax.ShapeDtypeStruct((g, 1, LANES), F32)] * 4,
        compiler_params=pltpu.CompilerParams(dimension_semantics=("parallel",)),
        name="s5_prep",
    )(col(a_re), col(a_im), rowv(a_re), rowv(a_im), log_dt.reshape(g, 1, 1),
      bt(b_re), bt(b_im), ct(c_re), ct(c_im))
    mi16, mi8, inr16, ini16, inr8, ini8, outr16, outi16, outr8, outi8, a16r, a16i, a8r, a8i = outs
    return {q: dict(m_intra=mi16, m_in_re=inr16, m_in_im=ini16, m_out_re=outr16, m_out_im=outi16,
                    a_re=a16r, a_im=a16i),
            q // 2: dict(m_intra=mi8, m_in_re=inr8, m_in_im=ini8, m_out_re=outr8, m_out_im=outi8,
                         a_re=a8r, a_im=a8i)}


def _lane_block_transpose(a):
    a = list(a)
    blk = lax.broadcasted_iota(jnp.int32, a[0].shape, 1) >> 4
    for d in (4, 2, 1):
        upper = (blk & d) != 0
        for r in range(8):
            if r & d:
                continue
            lo, hi = a[r], a[r + d]
            a[r] = jnp.where(upper, pltpu.roll(hi, 16 * d, 1), lo)
            a[r + d] = jnp.where(upper, hi, pltpu.roll(lo, LANES - 16 * d, 1))
    return a


def _in_proj_kernel(x_ref, gpre_ref, w_ref, wgk_ref, bgk_ref,
                    q_ref, k_ref, v_ref, g_ref, lg_ref, u_ref, ug_ref, u_scr, *, steps):
    h = _rms(x_ref[...], gpre_ref[...])
    proj = _dot(h.astype(BF16), w_ref[...])
    q_ref[...] = proj[:, 0:256]
    k_ref[...] = proj[:, 256:512]
    v_ref[...] = proj[:, 512:1024]
    g_ref[...] = proj[:, 1024:1536]
    u = proj[:, 1536:2048]
    u_ref[...] = u
    z = _dot(proj[:, 2048:PROJ_W].astype(BF16), wgk_ref[...]) + bgk_ref[...]
    lg_ref[...] = jax.nn.log_sigmoid(z) * (1.0 / GATE_NORM)

    nr = x_ref.shape[0] // steps
    rt = min(nr, 16)
    for s in range(D_S5 // LANES):
        u_scr[s] = u[:, s * LANES:(s + 1) * LANES]
    for s in range(D_S5 // LANES):
        for hf in range(steps // 8):
            for r0 in range(0, nr, rt):
                a = [u_scr[s, pl.ds(r0 * steps + hf * 8 + t, rt, stride=steps), :] for t in range(8)]
                per_group = _lane_block_transpose(a)
                for g in range(SLAB_GROUPS):
                    ug_ref[s * SLAB_GROUPS + g, r0:r0 + rt, hf * LANES:(hf + 1) * LANES] = (
                        per_group[g].astype(BF16))


def _in_proj(x, gpre, w_p, wgk_p, bgk, tm, steps):
    t = x.shape[0]
    row = lambda w: pl.BlockSpec((tm, w), lambda i: (i, 0))
    widths = (256, 256, 512, 512, 256, 512)
    wg = steps * S5_GROUP
    return pl.pallas_call(
        functools.partial(_in_proj_kernel, steps=steps),
        grid=(t // tm,),
        in_specs=[row(D_MODEL), _const_spec((1, D_MODEL)), _const_spec((D_MODEL, PROJ_W)),
                  _const_spec((LANES, GLA_KDIM)), _const_spec((1, GLA_KDIM))],
        out_specs=[row(w) for w in widths]
                  + [pl.BlockSpec((S5_GROUPS, tm // steps, wg), lambda i: (0, i, 0))],
        out_shape=[jax.ShapeDtypeStruct((t, w), F32) for w in widths]
                  + [jax.ShapeDtypeStruct((S5_GROUPS, t // steps, wg), BF16)],
        scratch_shapes=[pltpu.VMEM((D_S5 // LANES, tm, LANES), F32)],
        compiler_params=pltpu.CompilerParams(dimension_semantics=("parallel",),
                                             vmem_limit_bytes=VMEM_LIMIT),
        name="in_proj",
    )(x, gpre, w_p, wgk_p, bgk)


def _gla_kernel(q_ref, k_ref, v_ref, lg_ref, s0_ref, o_ref, s_ref, *, bb, sg, chunk):
    r = sg * chunk
    shift = chunk.bit_length() - 1

    @pl.when(pl.program_id(1) == 0)
    def _():
        s_ref[...] = s0_ref[...]

    hk = GLA_HEADS * r
    ri = lax.broadcasted_iota(jnp.int32, (r, r), 0)
    ci = lax.broadcasted_iota(jnp.int32, (r, r), 1)
    tri_bf = jnp.where(((ri >> shift) == (ci >> shift)) & (ri >= ci), 1.0, 0.0).astype(BF16)
    ri4 = lax.broadcasted_iota(jnp.int32, (r, hk), 0)
    ci4 = lax.broadcasted_iota(jnp.int32, (r, hk), 1) & (r - 1)
    causal4 = ((ri4 >> shift) == (ci4 >> shift)) & (ri4 >= ci4)
    assert r >= LANES or sg == 1
    stack_head = lax.broadcasted_iota(jnp.int32, (hk, 1), 0) >> (r.bit_length() - 1)
    k_head = lax.broadcasted_iota(jnp.int32, (1, GLA_KDIM), 1) >> 6
    v_head = lax.broadcasted_iota(jnp.int32, (1, D_GLA), 1) >> 7
    s_head = lax.broadcasted_iota(jnp.int32, (GLA_KDIM, 1), 0) >> 6
    k_diag, v_diag, s_diag = stack_head == k_head, stack_head == v_head, s_head == v_head
    tile4 = lambda t: jnp.concatenate([t] * GLA_HEADS, axis=0)

    for gi in range(bb // sg):
        seqs = slice(gi * sg, (gi + 1) * sg)
        lg = lg_ref[seqs].reshape(r, GLA_KDIM)
        lg_hi = lg.astype(BF16)
        lg_lo = (lg - lg_hi.astype(F32)).astype(BF16)
        b2 = _dot(tri_bf, jnp.concatenate([lg_hi, lg_lo], axis=1))
        b = b2[:, :GLA_KDIM] + b2[:, GLA_KDIM:]
        lasts = [b[(i + 1) * chunk - 1:(i + 1) * chunk, :] for i in range(sg)]
        bl = jnp.concatenate([jnp.broadcast_to(t, (chunk, GLA_KDIM)) for t in lasts], axis=0)
        k = k_ref[seqs].reshape(r, GLA_KDIM)
        v_bf = v_ref[seqs].reshape(r, D_GLA).astype(BF16)
        qd = (q_ref[seqs].reshape(r, GLA_KDIM) * (GLA_DK ** -0.5) * jnp.exp(b)).astype(BF16)
        ki = (k * jnp.exp(-b)).astype(BF16)
        ke = k * jnp.exp(bl - b)

        ki_bd = jnp.where(k_diag, tile4(ki), 0.0)
        att = jnp.where(causal4, _dot_nt(qd, ki_bd), 0.0).astype(BF16)
        v_bd = jnp.where(v_diag, tile4(v_bf), 0.0)
        o_intra = _dot(att, v_bd)

        if r < LANES:
            aug_t = jnp.concatenate([ke, jnp.broadcast_to(lasts[0], (LANES - r, GLA_KDIM))], axis=0).T
            ke_t, bl_t = aug_t, aug_t[:, r:]
        else:
            ke_t = ke.T
            bl_t = jnp.concatenate(lasts + [jnp.zeros((r - sg, GLA_KDIM), F32)], axis=0).T
        ke_t = ke_t.astype(BF16)

        for i in range(sg):
            seq = gi * sg + i
            rows = slice(i * chunk, (i + 1) * chunk)
            s_old = s_ref[seq]
            s_bd = jnp.where(s_diag, jnp.concatenate([s_old.astype(BF16)] * GLA_HEADS, axis=1), 0.0)
            o_ref[seq] = o_intra[rows] + _dot(qd[rows], s_bd)
            upd = [_dot(ke_t[h * GLA_DK:(h + 1) * GLA_DK, rows], v_bf[rows, h * GLA_DV:(h + 1) * GLA_DV])
                   for h in range(GLA_HEADS)]
            s_ref[seq] = jnp.exp(bl_t[:, i:i + 1]) * s_old + jnp.concatenate(upd, axis=0)


def _gla(q, k, v, lg, s0, bb, sg, chunk):
    b, l, _ = q.shape
    blk = lambda w: pl.BlockSpec((bb, chunk, w), lambda i, c: (i, c, 0))
    sspec = pl.BlockSpec((bb, GLA_KDIM, GLA_DV), lambda i, c: (i, 0, 0))
    return pl.pallas_call(
        functools.partial(_gla_kernel, bb=bb, sg=sg, chunk=chunk),
        grid=(b // bb, l // chunk),
        in_specs=[blk(GLA_KDIM), blk(GLA_KDIM), blk(D_GLA), blk(GLA_KDIM), sspec],
        out_specs=[blk(D_GLA), sspec],
        out_shape=[jax.ShapeDtypeStruct((b, l, D_GLA), F32),
                   jax.ShapeDtypeStruct((b, GLA_KDIM, GLA_DV), F32)],
        compiler_params=pltpu.CompilerParams(dimension_semantics=("parallel", "arbitrary"),
                                             vmem_limit_bytes=VMEM_LIMIT),
        name=f"gla_c{chunk}",
    )(q, k, v, lg, s0)


def _s5_kernel(ug_ref, mi_ref, minr_ref, mini_ref, mor_ref, moi_ref, ar_ref, ai_ref,
               h0r_ref, h0i_ref, yg_ref, hfr_ref, hfi_ref, vr_s, vi_s, hr_s, hi_s, *, nb, cb):
    gs = SLAB_GROUPS
    rows = nb * cb
    batch = cb if nb == 1 else nb
    n_blocks = rows // batch
    interleave = nb > 1 and cb > 1

    @pl.when(pl.program_id(1) == 0)
    def _():
        hfr_ref[...] = h0r_ref[...]
        hfi_ref[...] = h0i_ref[...]

    def to_scan_order(ref, g, val):
        if not interleave:
            ref[g] = val
        else:
            for b in range(nb):
                ref[g, pl.ds(b, cb, stride=nb), :] = val[b * cb:(b + 1) * cb]

    def from_scan_order(ref, g):
        if not interleave:
            return ref[g]
        return jnp.concatenate([ref[g, pl.ds(b, cb, stride=nb), :] for b in range(nb)], axis=0)

    ub = [ug_ref[g].reshape(rows, ug_ref.shape[-1]) for g in range(gs)]
    for g in range(gs):
        to_scan_order(vr_s, g, _dot(ub[g], minr_ref[g]))
        to_scan_order(vi_s, g, _dot(ub[g], mini_ref[g]))
    ar = [ar_ref[g] for g in range(gs)]
    ai = [ai_ref[g] for g in range(gs)]

    def body(c, carry):
        rows = pl.ds(pl.multiple_of(c * batch, batch), batch)
        new = []
        for g in range(gs):
            hr, hi = carry[2 * g], carry[2 * g + 1]
            hr_s[g, rows, :] = hr
            hi_s[g, rows, :] = hi
            new.append(ar[g] * hr - ai[g] * hi + vr_s[g, rows, :])
            new.append(ar[g] * hi + ai[g] * hr + vi_s[g, rows, :])
        return tuple(new)

    init = tuple(ref[g] for g in range(gs) for ref in (hfr_ref, hfi_ref))
    fin = lax.fori_loop(0, n_blocks, body, init)
    for g in range(gs):
        hfr_ref[g] = fin[2 * g]
        hfi_ref[g] = fin[2 * g + 1]
        y = (_dot(ub[g], mi_ref[g]) + _dot(from_scan_order(hr_s, g).astype(BF16), mor_ref[g])
             + _dot(from_scan_order(hi_s, g).astype(BF16), moi_ref[g]))
        yg_ref[g] = y.reshape(yg_ref.shape[1:])


def _s5(ug, m, h0_re, h0_im, nb, cb):
    g_all, r, w = ug.shape
    nc = r // nb
    batch = cb if nb == 1 else nb
    n, gs = LANES, SLAB_GROUPS
    uspec = pl.BlockSpec((gs, nb, cb, w), lambda s, i: (s, 0, i, 0))
    per_s = lambda a, b_: pl.BlockSpec((gs, a, b_), lambda s, i: (s, 0, 0))
    if nb == 1:
        hspec = pl.BlockSpec((gs, batch, n), lambda s, i: (s, i, 0))
    else:
        hspec = per_s(batch, n)
    yg, hf_re, hf_im = pl.pallas_call(
        functools.partial(_s5_kernel, nb=nb, cb=cb),
        grid=(g_all // gs, nc // cb),
        in_specs=[uspec, per_s(w, w), per_s(w, n), per_s(w, n), per_s(n, w), per_s(n, w),
                  per_s(1, n), per_s(1, n), hspec, hspec],
        out_specs=[uspec, hspec, hspec],
        out_shape=[jax.ShapeDtypeStruct((g_all, nb, nc, w), F32),
                   jax.ShapeDtypeStruct(h0_re.shape, F32),
                   jax.ShapeDtypeStruct(h0_im.shape, F32)],
        scratch_shapes=[pltpu.VMEM((gs, nb * cb, n), F32)] * 4,
        compiler_params=pltpu.CompilerParams(dimension_semantics=("parallel", "arbitrary"),
                                             vmem_limit_bytes=VMEM_LIMIT),
        name=f"s5_w{w}",
    )(ug.reshape(g_all, nb, nc, w), m["m_intra"], m["m_in_re"], m["m_in_im"], m["m_out_re"], m["m_out_im"],
      m["a_re"], m["a_im"], h0_re, h0_im)
    return yg.reshape(g_all, r, w), hf_re, hf_im


def _mix_ffn_kernel(x_ref, o_ref, g_ref, yg_ref, u_ref, dsk_ref, gn_ref, wglu_ref, s5n_ref, wo_ref, gpost_ref,
                    gpre_ref, wg_ref, wu_ref, wd_ref, gpostf_ref, out_ref, y_scr, x1_scr, *, steps):
    i = pl.program_id(0)

    def mix_into(slot):
        nr = x_ref.shape[0] // steps
        rt = min(nr, 8)
        for s in range(D_S5 // LANES):
            for hf in range(steps // 8):
                for r0 in range(0, nr, rt):
                    a = [yg_ref[s * SLAB_GROUPS + g, r0:r0 + rt, hf * LANES:(hf + 1) * LANES]
                         for g in range(SLAB_GROUPS)]
                    per_step = _lane_block_transpose(a)
                    for t in range(8):
                        y_scr[s, pl.ds(r0 * steps + hf * 8 + t, rt, stride=steps), :] = per_step[t]
        y5 = jnp.concatenate([y_scr[s] for s in range(D_S5 // LANES)], axis=1) + dsk_ref[...] * u_ref[...]
        o = o_ref[...]
        gn = gn_ref[...]
        heads = []
        for h in range(GLA_HEADS):
            heads.append(_rms(o[:, h * GLA_DV:(h + 1) * GLA_DV], gn))
        og = jnp.concatenate(heads, axis=1) * jax.nn.silu(g_ref[...])
        y = jax.nn.gelu(y5)
        y = y * jax.nn.sigmoid(_dot(y.astype(BF16), wglu_ref[...]))
        y = _rms(y, s5n_ref[...])
        mix = _dot(og.astype(BF16), wo_ref[:D_GLA, :]) + _dot(y.astype(BF16), wo_ref[D_GLA:, :])
        x1_scr[slot] = x_ref[...] + _rms(mix, gpost_ref[...])

    def ffn_from(slot):
        x = x1_scr[slot]
        h = _rms(x, gpre_ref[...]).astype(BF16)
        acc = jnp.zeros(x.shape, F32)
        for c in range(D_FF // FF_CHUNK):
            cols = slice(c * FF_CHUNK, (c + 1) * FF_CHUNK)
            act = jax.nn.silu(_dot(h, wg_ref[:, cols])) * _dot(h, wu_ref[:, cols])
            acc = acc + _dot(act.astype(BF16), wd_ref[cols, :])
        out_ref[...] = x + _rms(acc, gpostf_ref[...])

    @pl.when(i == 0)
    def _():
        mix_into(0)

    @pl.when(i > 0)
    def _():
        ffn_from((i - 1) & 1)
        mix_into(i & 1)


def _mix_ffn(x, o, g, yg, u, dsk, gn, wglu, s5n, wo, gpost, gpre, wg, wu, wd, gpostf, tm, steps):
    t = x.shape[0]
    n = t // tm
    cur = lambda i: jnp.minimum(i, n - 1)
    row = lambda w: pl.BlockSpec((tm, w), lambda i: (cur(i), 0))
    fixed = lambda shape: pl.BlockSpec(shape, lambda i: (0,) * len(shape), pipeline_mode=pl.Buffered(1))
    return pl.pallas_call(
        functools.partial(_mix_ffn_kernel, steps=steps),
        grid=(n + 1,),
        in_specs=[row(D_MODEL), row(D_GLA), row(D_GLA),
                  pl.BlockSpec((S5_GROUPS, tm // steps, steps * S5_GROUP), lambda i: (0, cur(i), 0)),
                  row(D_S5), fixed((1, D_S5)),
                  fixed((1, GLA_DV)), fixed((D_S5, D_S5)), fixed((1, D_S5)),
                  fixed((D_GLA + D_S5, D_MODEL)), fixed((1, D_MODEL)),
                  fixed((1, D_MODEL)), fixed((D_MODEL, D_FF)), fixed((D_MODEL, D_FF)),
                  fixed((D_FF, D_MODEL)), fixed((1, D_MODEL))],
        out_specs=pl.BlockSpec((tm, D_MODEL), lambda i: (jnp.maximum(i - 1, 0), 0)),
        out_shape=jax.ShapeDtypeStruct((t, D_MODEL), F32),
        scratch_shapes=[pltpu.VMEM((D_S5 // LANES, tm, LANES), F32), pltpu.VMEM((2, tm, D_MODEL), F32)],
        compiler_params=pltpu.CompilerParams(dimension_semantics=("arbitrary",),
                                             vmem_limit_bytes=VMEM_LIMIT),
        name="mix_ffn",
    )(x, o, g, yg, u, dsk, gn, wglu, s5n, wo, gpost, gpre, wg, wu, wd, gpostf)


def kernel(x_prompt, x_sample, state_gla, state_s5_re, state_s5_im, meta_tokens, g_pre_mix, w_in, w_gk2, b_gk, gla_norm, s5_a_re, s5_a_im, s5_b_re, s5_b_im, s5_c_re, s5_c_im, s5_d, s5_log_dt, w_s5_glu, s5_norm, w_o, g_post_mix, g_pre_ffn, w_gate, w_up, w_down, g_post_ffn):
    assert g_pre_mix.shape[0] == 1, "single-layer step"
    bp, seq_p, _ = x_prompt.shape
    bs, seq_s, _ = x_sample.shape
    row = lambda t: t[0].reshape(1, -1)

    w = w_in[0]
    c3, c4 = 1536, 1536 + GATE_RANK
    w_p = jnp.concatenate([w[:, :c3], w[:, c4:], w[:, c3:c4],
                           jnp.zeros((D_MODEL, LANES - GATE_RANK), F32)], axis=1).astype(BF16)
    wgk_p = jnp.concatenate([w_gk2[0], jnp.zeros((LANES - GATE_RANK, GLA_KDIM), F32)], axis=0).astype(BF16)
    wo_bf = w_o[0].astype(BF16)
    s5m = _s5_prep(s5_a_re[0], s5_a_im[0], s5_b_re[0], s5_b_im[0], s5_c_re[0], s5_c_im[0], s5_log_dt[0])
    proj_w = (row(g_pre_mix), w_p, wgk_p, row(b_gk))
    pad_state = lambda h: jnp.pad(h, ((0, 0), (0, 0), (0, LANES - S5_STATE)))

    def finish(x, o, g, yg, u, tm, steps):
        return _mix_ffn(x, o, g, yg, u, row(s5_d), row(gla_norm), w_s5_glu[0].astype(BF16), row(s5_norm),
                        wo_bf, row(g_post_mix), row(g_pre_ffn), w_gate[0].astype(BF16),
                        w_up[0].astype(BF16), w_down[0].astype(BF16), row(g_post_ffn), tm, steps)

    xm = jnp.broadcast_to(meta_tokens[None], (bp, N_META, D_MODEL)).reshape(bp * N_META, D_MODEL)
    q, k, v, _, lg, _, ug = _in_proj(xm, *proj_w, bp * N_META, S5_BLOCK)
    r3 = lambda t, b, l: t.reshape(b, l, t.shape[-1])
    _, s_meta = _gla(r3(q, bp, N_META), r3(k, bp, N_META), r3(v, bp, N_META), r3(lg, bp, N_META),
                     jnp.zeros((bp, GLA_KDIM, GLA_DV), F32), bp, bp, N_META)
    zh = jnp.zeros((S5_GROUPS, bp, LANES), F32)
    _, hm_re, hm_im = _s5(ug, s5m[S5_BLOCK], zh, zh, 1, bp)

    xp = x_prompt.reshape(bp * seq_p, D_MODEL)
    q, k, v, g, lg, u, ug = _in_proj(xp, *proj_w, 512, S5_BLOCK)
    o, s_p = _gla(r3(q, bp, seq_p), r3(k, bp, seq_p), r3(v, bp, seq_p), r3(lg, bp, seq_p), s_meta, bp, 1, GLA_CHUNK)
    yg, hp_re, hp_im = _s5(ug, s5m[S5_BLOCK], hm_re, hm_im, bp, 32)
    y_prompt = finish(xp, o.reshape(bp * seq_p, D_GLA), g, yg, u, 512, S5_BLOCK)

    xs = x_sample.reshape(bs * seq_s, D_MODEL)
    to_g = lambda t: pad_state(jnp.swapaxes(t[0], 0, 1))
    q, k, v, g, lg, u, ug = _in_proj(xs, *proj_w, 512, seq_s)
    o, s_s = _gla(r3(q, bs, seq_s), r3(k, bs, seq_s), r3(v, bs, seq_s), r3(lg, bs, seq_s),
                  state_gla[0].reshape(bs, GLA_KDIM, GLA_DV), 32, 16, seq_s)
    yg, hs_re, hs_im = _s5(ug, s5m[seq_s], to_g(state_s5_re), to_g(state_s5_im), 1, bs)
    y_sample = finish(xs, o.reshape(bs * seq_s, D_GLA), g, yg, u, 512, seq_s)

    gla_out = lambda s, b: s.reshape(1, b, GLA_HEADS, GLA_DK, GLA_DV)
    s5_out = lambda h: jnp.swapaxes(h[:, :, :S5_STATE], 0, 1)[None]
    return (y_prompt.reshape(bp, seq_p, D_MODEL), y_sample.reshape(bs, seq_s, D_MODEL),
            gla_out(s_p, bp), s5_out(hp_re), s5_out(hp_im),
            gla_out(s_s, bs), s5_out(hs_re), s5_out(hs_im))
```

```python
import functools

import jax
import jax.numpy as jnp
from jax import lax
from jax.experimental import pallas as pl
from jax.experimental.pallas import tpu as pltpu

F32 = jnp.float32
BF16 = jnp.bfloat16

D_MODEL = 1024
D_GLA = 512
GLA_HEADS = 4
GLA_DV = 128
GLA_DK = 64
GLA_KDIM = 256
GATE_RANK = 16
GATE_NORM = 16.0
GLA_CHUNK = 64
D_S5 = 512
S5_GROUP = 16
S5_GROUPS = 32
S5_STATE = 64
N_META = 16
D_FF = 2816
EPS = 1e-6
LANES = 128
S5_BLOCK = 16
SLAB_GROUPS = LANES // S5_GROUP
PROJ_W = 2176
FF_CHUNK = 256
VMEM_LIMIT = 48 * 1024 * 1024


def _rms(x, g):
    return x * lax.rsqrt(jnp.mean(x * x, axis=-1, keepdims=True) + EPS) * g


def _dot(a, b):
    return jnp.dot(a, b, preferred_element_type=F32)


def _dot_nt(a, b):
    return lax.dot_general(a, b, (((1,), (1,)), ((), ())), preferred_element_type=F32)


def _dot_tn(a, b):
    return lax.dot_general(a, b, (((0,), (0,)), ((), ())), preferred_element_type=F32)


def _const_spec(shape):
    zeros = (0,) * len(shape)
    return pl.BlockSpec(shape, lambda *_: zeros)


PREP_GROUPS = 8


def _cmul(a, b):
    return a[0] * b[0] - a[1] * b[1], a[0] * b[1] + a[1] * b[0]


def _unit_powers(c1, s1, expo, n_bits):
    acc = (jnp.ones_like(c1), jnp.zeros_like(c1))
    base = (c1, s1)
    squares = [base]
    for bit in range(n_bits):
        take = ((expo >> bit) & 1) == 1
        nxt = _cmul(acc, base)
        acc = (jnp.where(take, nxt[0], acc[0]), jnp.where(take, nxt[1], acc[1]))
        base = _cmul(base, base)
        squares.append(base)
    return acc, squares


def _s5_prep_kernel(arc_ref, aic_ref, air_ref, ldt_ref, btr_ref, bti_ref, ctr_ref, cti_ref,
                    mi16_ref, mi8_ref, inr16_ref, ini16_ref, inr8_ref, ini8_ref,
                    outr16_ref, outi16_ref, outr8_ref, outi8_ref, a16r_ref, a16i_ref, a8r_ref, a8i_ref):
    n, q = S5_STATE, S5_BLOCK
    w = q * S5_GROUP
    hp = lax.Precision.HIGHEST
    lane = lax.broadcasted_iota(jnp.int32, (n, w), 1)
    t_blk = lane >> 4
    eye = lax.broadcasted_iota(jnp.int32, (n, n), 0) == lax.broadcasted_iota(jnp.int32, (n, n), 1)
    to_col = lambda r: jnp.sum(jnp.where(eye, r, 0.0), axis=1, keepdims=True)
    zrows = lambda x: jnp.concatenate([x, jnp.zeros_like(x)], axis=0)
    lane_n = lax.broadcasted_iota(jnp.int32, (n, LANES), 1)
    for g in range(PREP_GROUPS):
        dt = jnp.exp(ldt_ref[g])
        ang_r = air_ref[g] * dt
        c1, s1 = to_col(jnp.cos(ang_r)), to_col(jnp.sin(ang_r))
        lam_re = jnp.minimum(arc_ref[g], -1e-4)
        lam_im = aic_ref[g]
        unit, squares = _unit_powers(c1, s1, t_blk, 4)
        pm = jnp.exp(t_blk.astype(F32) * (lam_re * dt))
        pk = (pm * unit[0], pm * unit[1])
        mag = jnp.exp(lam_re * dt)
        ab = (mag * c1, mag * s1)
        p1 = _cmul(pk, ab)
        ct = (ctr_ref[g], cti_ref[g])
        g0 = _cmul(ct, pk)
        mo = _cmul(ct, p1)
        mo_re, mo_im = zrows(mo[0]), zrows(-mo[1])
        outr16_ref[g] = mo_re.astype(BF16)
        outi16_ref[g] = mo_im.astype(BF16)
        outr8_ref[g] = mo_re[:, :w // 2].astype(BF16)
        outi8_ref[g] = mo_im[:, :w // 2].astype(BF16)
        den = lam_re * lam_re + lam_im * lam_im
        nr, ni = ab[0] - 1.0, ab[1]
        f = ((nr * lam_re + ni * lam_im) / den, (ni * lam_re - nr * lam_im) / den)
        e = _cmul(pk, _cmul(f, (btr_ref[g], bti_ref[g])))
        et = [zrows(x).T for x in e]
        rev = [jnp.concatenate([x[(q - 1 - s) * S5_GROUP:(q - s) * S5_GROUP] for s in range(q)], axis=0) for x in et]
        inr16_ref[g] = rev[0].astype(BF16)
        ini16_ref[g] = rev[1].astype(BF16)
        inr8_ref[g] = rev[0][w // 2:].astype(BF16)
        ini8_ref[g] = rev[1][w // 2:].astype(BF16)
        hops = []
        for steps in (q, q // 2):
            m = jnp.exp(float(steps) * (lam_re * dt))
            u = squares[steps.bit_length() - 1]
            hops += [m * u[0], m * u[1]]
        cols = jnp.zeros((n, LANES), F32)
        for idx, hcol in enumerate(hops):
            cols = jnp.where(lane_n == idx, hcol, cols)
        hop_rows = zrows(cols).T
        a16r_ref[g] = hop_rows[0:1]
        a16i_ref[g] = hop_rows[1:2]
        a8r_ref[g] = hop_rows[2:3]
        a8i_ref[g] = hop_rows[3:4]
        t0 = (jnp.dot(et[0][:S5_GROUP, :n], g0[0], precision=hp, preferred_element_type=F32)
              - jnp.dot(et[1][:S5_GROUP, :n], g0[1], precision=hp, preferred_element_type=F32))
        lane_t = lax.broadcasted_iota(jnp.int32, t0.shape, 1) >> 4
        for s in range(q):
            blk = t0 if s == 0 else jnp.where(lane_t >= s, pltpu.roll(t0, S5_GROUP * s, 1), 0.0)
            mi16_ref[g, s * S5_GROUP:(s + 1) * S5_GROUP, :] = blk.astype(BF16)
            if s < q // 2:
                mi8_ref[g, s * S5_GROUP:(s + 1) * S5_GROUP, :] = blk[:, :w // 2].astype(BF16)


def _s5_prep(a_re, a_im, b_re, b_im, c_re, c_im, log_dt):
    g, n, j, q = S5_GROUPS, S5_STATE, S5_GROUP, S5_BLOCK
    w, h, pg = q * j, q * j // 2, PREP_GROUPS
    col = lambda t: t.reshape(g, n, 1)
    rowv = lambda t: t.reshape(g, 1, n)
    bt = lambda t: jnp.tile(t, (1, 1, q))
    ct = lambda t: jnp.tile(jnp.swapaxes(t, 1, 2), (1, 1, q))
    spec = lambda a, b_: pl.BlockSpec((pg, a, b_), lambda i: (i, 0, 0))
    shapes = [(w, w), (h, h), (w, LANES), (w, LANES), (h, LANES), (h, LANES),
              (LANES, w), (LANES, w), (LANES, h), (LANES, h)]
    outs = pl.pallas_call(
        _s5_prep_kernel,
        grid=(g // pg,),
        in_specs=[spec(n, 1), spec(n, 1), spec(1, n), spec(1, 1),
                  spec(n, w), spec(n, w), spec(n, w), spec(n, w)],
        out_specs=[spec(*s) for s in shapes] + [spec(1, LANES)] * 4,
        out_shape=[jax.ShapeDtypeStruct((g,) + s, BF16) for s in shapes]
                  + [jax.ShapeDtypeStruct((g, 1, LANES), F32)] * 4,
        compiler_params=pltpu.CompilerParams(dimension_semantics=("parallel",)),
        name="s5_prep",
    )(col(a_re), col(a_im), rowv(a_im), log_dt.reshape(g, 1, 1),
      bt(b_re), bt(b_im), ct(c_re), ct(c_im))
    mi16, mi8, inr16, ini16, inr8, ini8, outr16, outi16, outr8, outi8, a16r, a16i, a8r, a8i = outs
    return {q: dict(m_intra=mi16, m_in_re=inr16, m_in_im=ini16, m_out_re=outr16, m_out_im=outi16,
                    a_re=a16r, a_im=a16i),
            q // 2: dict(m_intra=mi8, m_in_re=inr8, m_in_im=ini8, m_out_re=outr8, m_out_im=outi8,
                         a_re=a8r, a_im=a8i)}


def _lane_block_transpose(a):
    a = list(a)
    blk = lax.broadcasted_iota(jnp.int32, a[0].shape, 1) >> 4
    for d in (4, 2, 1):
        upper = (blk & d) != 0
        for r in range(8):
            if r & d:
                continue
            lo, hi = a[r], a[r + d]
            a[r] = jnp.where(upper, pltpu.roll(hi, 16 * d, 1), lo)
            a[r + d] = jnp.where(upper, hi, pltpu.roll(lo, LANES - 16 * d, 1))
    return a


def _in_proj_kernel(x_ref, gpre_ref, w_ref, wgk_ref, bgk_ref,
                    q_ref, k_ref, v_ref, g_ref, u_ref, lg_ref, ug_ref, u_scr, gk_scr, *, steps, n_tiles):
    i = pl.program_id(0)

    def project(slot):
        h = _rms(x_ref[...], gpre_ref[...])
        proj = _dot(h.astype(BF16), w_ref[...])
        q_ref[...] = proj[:, 0:256]
        k_ref[...] = proj[:, 256:512]
        v_ref[...] = proj[:, 512:1024]
        g_ref[...] = proj[:, 1024:1536]
        u = proj[:, 1536:2048]
        u_ref[...] = u
        for s in range(D_S5 // LANES):
            u_scr[slot, s] = u[:, s * LANES:(s + 1) * LANES]
        gk_scr[slot] = proj[:, 2048:PROJ_W]

    def tail(slot):
        z = _dot(gk_scr[slot].astype(BF16), wgk_ref[...]) + bgk_ref[...]
        lg_ref[...] = jax.nn.log_sigmoid(z) * (1.0 / GATE_NORM)
        nr = x_ref.shape[0] // steps
        rt = min(nr, 16)
        for s in range(D_S5 // LANES):
            for hf in range(steps // 8):
                for r0 in range(0, nr, rt):
                    a = [u_scr[slot, s, pl.ds(r0 * steps + hf * 8 + t, rt, stride=steps), :] for t in range(8)]
                    per_group = _lane_block_transpose(a)
                    for g in range(SLAB_GROUPS):
                        ug_ref[s * SLAB_GROUPS + g, r0:r0 + rt, hf * LANES:(hf + 1) * LANES] = (
                            per_group[g].astype(BF16))

    @pl.when(i == 0)
    def _():
        project(0)

    @pl.when((i > 0) & (i < n_tiles))
    def _():
        tail((i - 1) & 1)
        project(i & 1)

    @pl.when(i == n_tiles)
    def _():
        tail((i - 1) & 1)


def _in_proj(x, gpre, w_p, wgk_p, bgk, tm, steps):
    t = x.shape[0]
    n = t // tm
    cur = lambda i: jnp.minimum(i, n - 1)
    prev = lambda i: jnp.maximum(i - 1, 0)
    row = lambda w, at: pl.BlockSpec((tm, w), lambda i: (at(i), 0))
    widths = (256, 256, 512, 512, 512)
    wg = steps * S5_GROUP
    return pl.pallas_call(
        functools.partial(_in_proj_kernel, steps=steps, n_tiles=n),
        grid=(n + 1,),
        in_specs=[row(D_MODEL, cur), _const_spec((1, D_MODEL)), _const_spec((D_MODEL, PROJ_W)),
                  _const_spec((LANES, GLA_KDIM)), _const_spec((1, GLA_KDIM))],
        out_specs=[row(w, cur) for w in widths] + [row(GLA_KDIM, prev)]
                  + [pl.BlockSpec((S5_GROUPS, tm // steps, wg), lambda i: (0, prev(i), 0))],
        out_shape=[jax.ShapeDtypeStruct((t, w), F32) for w in widths]
                  + [jax.ShapeDtypeStruct((t, GLA_KDIM), F32)]
                  + [jax.ShapeDtypeStruct((S5_GROUPS, t // steps, wg), BF16)],
        scratch_shapes=[pltpu.VMEM((2, D_S5 // LANES, tm, LANES), F32), pltpu.VMEM((2, tm, LANES), F32)],
        compiler_params=pltpu.CompilerParams(dimension_semantics=("arbitrary",),
                                             vmem_limit_bytes=VMEM_LIMIT),
        name="in_proj",
    )(x, gpre, w_p, wgk_p, bgk)


def _gla_kernel(q_ref, k_ref, v_ref, lg_ref, s0_ref, o_ref, s_ref, *, bb, sg, chunk):
    r = sg * chunk
    shift = chunk.bit_length() - 1

    @pl.when(pl.program_id(1) == 0)
    def _():
        s_ref[...] = s0_ref[...]

    hk = GLA_HEADS * r
    ri = lax.broadcasted_iota(jnp.int32, (r, r), 0)
    ci = lax.broadcasted_iota(jnp.int32, (r, r), 1)
    tri_bf = jnp.where(((ri >> shift) == (ci >> shift)) & (ri >= ci), 1.0, 0.0).astype(BF16)
    ri4 = lax.broadcasted_iota(jnp.int32, (r, hk), 0)
    ci4 = lax.broadcasted_iota(jnp.int32, (r, hk), 1) & (r - 1)
    causal4 = ((ri4 >> shift) == (ci4 >> shift)) & (ri4 >= ci4)
    assert r >= LANES or sg == 1
    stack_head = lax.broadcasted_iota(jnp.int32, (hk, 1), 0) >> (r.bit_length() - 1)
    k_head = lax.broadcasted_iota(jnp.int32, (1, GLA_KDIM), 1) >> 6
    v_head = lax.broadcasted_iota(jnp.int32, (1, D_GLA), 1) >> 7
    s_head = lax.broadcasted_iota(jnp.int32, (GLA_KDIM, 1), 0) >> 6
    k_diag, v_diag, s_diag = stack_head == k_head, stack_head == v_head, s_head == v_head
    tile4 = lambda t: jnp.concatenate([t] * GLA_HEADS, axis=0)

    for gi in range(bb // sg):
        seqs = slice(gi * sg, (gi + 1) * sg)
        lg = lg_ref[seqs].reshape(r, GLA_KDIM)
        lg_hi = lg.astype(BF16)
        lg_lo = (lg - lg_hi.astype(F32)).astype(BF16)
        b2 = _dot(tri_bf, jnp.concatenate([lg_hi, lg_lo], axis=1))
        b = b2[:, :GLA_KDIM] + b2[:, GLA_KDIM:]
        lasts = [b[(i + 1) * chunk - 1:(i + 1) * chunk, :] for i in range(sg)]
        bl = jnp.concatenate([jnp.broadcast_to(t, (chunk, GLA_KDIM)) for t in lasts], axis=0)
        k = k_ref[seqs].reshape(r, GLA_KDIM)
        v_bf = v_ref[seqs].reshape(r, D_GLA).astype(BF16)
        qd = (q_ref[seqs].reshape(r, GLA_KDIM) * (GLA_DK ** -0.5) * jnp.exp(b)).astype(BF16)
        ki = (k * jnp.exp(-b)).astype(BF16)
        ke = k * jnp.exp(bl - b)

        ki_bd = jnp.where(k_diag, tile4(ki), 0.0)
        att = jnp.where(causal4, _dot_nt(qd, ki_bd), 0.0).astype(BF16)
        v_bd = jnp.where(v_diag, tile4(v_bf), 0.0)
        o_intra = _dot(att, v_bd)

        if r < LANES:
            aug_t = jnp.concatenate([ke, jnp.broadcast_to(lasts[0], (LANES - r, GLA_KDIM))], axis=0).T
            ke_t, bl_t = aug_t, aug_t[:, r:]
        else:
            ke_t = ke.T
            bl_t = jnp.concatenate(lasts + [jnp.zeros((r - sg, GLA_KDIM), F32)], axis=0).T
        ke_t = ke_t.astype(BF16)

        for i in range(sg):
            seq = gi * sg + i
            rows = slice(i * chunk, (i + 1) * chunk)
            s_old = s_ref[seq]
            s_bd = jnp.where(s_diag, jnp.concatenate([s_old.astype(BF16)] * GLA_HEADS, axis=1), 0.0)
            o_ref[seq] = o_intra[rows] + _dot(qd[rows], s_bd)
            upd = [_dot(ke_t[h * GLA_DK:(h + 1) * GLA_DK, rows], v_bf[rows, h * GLA_DV:(h + 1) * GLA_DV])
                   for h in range(GLA_HEADS)]
            s_ref[seq] = jnp.exp(bl_t[:, i:i + 1]) * s_old + jnp.concatenate(upd, axis=0)


def _gla(q, k, v, lg, s0, bb, sg, chunk):
    b, l, _ = q.shape
    blk = lambda w: pl.BlockSpec((bb, chunk, w), lambda i, c: (i, c, 0))
    sspec = pl.BlockSpec((bb, GLA_KDIM, GLA_DV), lambda i, c: (i, 0, 0))
    return pl.pallas_call(
        functools.partial(_gla_kernel, bb=bb, sg=sg, chunk=chunk),
        grid=(b // bb, l // chunk),
        in_specs=[blk(GLA_KDIM), blk(GLA_KDIM), blk(D_GLA), blk(GLA_KDIM), sspec],
        out_specs=[blk(D_GLA), sspec],
        out_shape=[jax.ShapeDtypeStruct((b, l, D_GLA), F32),
                   jax.ShapeDtypeStruct((b, GLA_KDIM, GLA_DV), F32)],
        compiler_params=pltpu.CompilerParams(dimension_semantics=("parallel", "arbitrary"),
                                             vmem_limit_bytes=VMEM_LIMIT),
        name=f"gla_c{chunk}",
    )(q, k, v, lg, s0)


def _s5_kernel(ug_ref, mi_ref, minr_ref, mini_ref, mor_ref, moi_ref, ar_ref, ai_ref,
               h0r_ref, h0i_ref, yg_ref, hfr_ref, hfi_ref, vr_s, vi_s, hr_s, hi_s, *, nb, cb):
    gs = SLAB_GROUPS
    rows = nb * cb
    batch = cb if nb == 1 else nb
    n_blocks = rows // batch
    interleave = nb > 1 and cb > 1

    @pl.when(pl.program_id(1) == 0)
    def _():
        hfr_ref[...] = h0r_ref[...]
        hfi_ref[...] = h0i_ref[...]

    def to_scan_order(ref, g, val):
        if not interleave:
            ref[g] = val
        else:
            for b in range(nb):
                ref[g, pl.ds(b, cb, stride=nb), :] = val[b * cb:(b + 1) * cb]

    def from_scan_order(ref, g):
        if not interleave:
            return ref[g]
        return jnp.concatenate([ref[g, pl.ds(b, cb, stride=nb), :] for b in range(nb)], axis=0)

    ub = [ug_ref[g].reshape(rows, ug_ref.shape[-1]) for g in range(gs)]
    for g in range(gs):
        to_scan_order(vr_s, g, _dot(ub[g], minr_ref[g]))
        to_scan_order(vi_s, g, _dot(ub[g], mini_ref[g]))
    ar = [ar_ref[g] for g in range(gs)]
    ai = [ai_ref[g] for g in range(gs)]

    def body(c, carry):
        rows = pl.ds(pl.multiple_of(c * batch, batch), batch)
        new = []
        for g in range(gs):
            hr, hi = carry[2 * g], carry[2 * g + 1]
            hr_s[g, rows, :] = hr
            hi_s[g, rows, :] = hi
            new.append(ar[g] * hr - ai[g] * hi + vr_s[g, rows, :])
            new.append(ar[g] * hi + ai[g] * hr + vi_s[g, rows, :])
        return tuple(new)

    init = tuple(ref[g] for g in range(gs) for ref in (hfr_ref, hfi_ref))
    fin = lax.fori_loop(0, n_blocks, body, init)
    for g in range(gs):
        hfr_ref[g] = fin[2 * g]
        hfi_ref[g] = fin[2 * g + 1]
        y = (_dot(ub[g], mi_ref[g]) + _dot(from_scan_order(hr_s, g).astype(BF16), mor_ref[g])
             + _dot(from_scan_order(hi_s, g).astype(BF16), moi_ref[g]))
        yg_ref[g] = y.reshape(yg_ref.shape[1:])


def _s5(ug, m, h0_re, h0_im, nb, cb):
    g_all, r, w = ug.shape
    nc = r // nb
    batch = cb if nb == 1 else nb
    n, gs = LANES, SLAB_GROUPS
    uspec = pl.BlockSpec((gs, nb, cb, w), lambda s, i: (s, 0, i, 0))
    per_s = lambda a, b_: pl.BlockSpec((gs, a, b_), lambda s, i: (s, 0, 0))
    if nb == 1:
        hspec = pl.BlockSpec((gs, batch, n), lambda s, i: (s, i, 0))
    else:
        hspec = per_s(batch, n)
    yg, hf_re, hf_im = pl.pallas_call(
        functools.partial(_s5_kernel, nb=nb, cb=cb),
        grid=(g_all // gs, nc // cb),
        in_specs=[uspec, per_s(w, w), per_s(w, n), per_s(w, n), per_s(n, w), per_s(n, w),
                  per_s(1, n), per_s(1, n), hspec, hspec],
        out_specs=[uspec, hspec, hspec],
        out_shape=[jax.ShapeDtypeStruct((g_all, nb, nc, w), F32),
                   jax.ShapeDtypeStruct(h0_re.shape, F32),
                   jax.ShapeDtypeStruct(h0_im.shape, F32)],
        scratch_shapes=[pltpu.VMEM((gs, nb * cb, n), F32)] * 4,
        compiler_params=pltpu.CompilerParams(dimension_semantics=("parallel", "arbitrary"),
                                             vmem_limit_bytes=VMEM_LIMIT),
        name=f"s5_w{w}",
    )(ug.reshape(g_all, nb, nc, w), m["m_intra"], m["m_in_re"], m["m_in_im"], m["m_out_re"], m["m_out_im"],
      m["a_re"], m["a_im"], h0_re, h0_im)
    return yg.reshape(g_all, r, w), hf_re, hf_im


def _mix_ffn_kernel(x_ref, o_ref, g_ref, yg_ref, u_ref, dsk_ref, gn_ref, wglu_ref, s5n_ref, wo_ref, gpost_ref,
                    gpre_ref, wg_ref, wu_ref, wd_ref, gpostf_ref, out_ref, y_scr, x1_scr, *, steps):
    i = pl.program_id(0)

    def mix_into(slot):
        nr = x_ref.shape[0] // steps
        rt = min(nr, 8)
        for s in range(D_S5 // LANES):
            for hf in range(steps // 8):
                for r0 in range(0, nr, rt):
                    a = [yg_ref[s * SLAB_GROUPS + g, r0:r0 + rt, hf * LANES:(hf + 1) * LANES]
                         for g in range(SLAB_GROUPS)]
                    per_step = _lane_block_transpose(a)
                    for t in range(8):
                        y_scr[s, pl.ds(r0 * steps + hf * 8 + t, rt, stride=steps), :] = per_step[t]
        y5 = jnp.concatenate([y_scr[s] for s in range(D_S5 // LANES)], axis=1) + dsk_ref[...] * u_ref[...]
        o = o_ref[...]
        gn = gn_ref[...]
        heads = []
        for h in range(GLA_HEADS):
            heads.append(_rms(o[:, h * GLA_DV:(h + 1) * GLA_DV], gn))
        og = jnp.concatenate(heads, axis=1) * jax.nn.silu(g_ref[...])
        y = jax.nn.gelu(y5)
        y = y * jax.nn.sigmoid(_dot(y.astype(BF16), wglu_ref[...]))
        y = _rms(y, s5n_ref[...])
        mix = _dot(og.astype(BF16), wo_ref[:D_GLA, :]) + _dot(y.astype(BF16), wo_ref[D_GLA:, :])
        x1_scr[slot] = x_ref[...] + _rms(mix, gpost_ref[...])

    def ffn_from(slot):
        x = x1_scr[slot]
        h = _rms(x, gpre_ref[...]).astype(BF16)
        acc = jnp.zeros(x.shape, F32)
        for c in range(D_FF // FF_CHUNK):
            cols = slice(c * FF_CHUNK, (c + 1) * FF_CHUNK)
            act = jax.nn.silu(_dot(h, wg_ref[:, cols])) * _dot(h, wu_ref[:, cols])
            acc = acc + _dot(act.astype(BF16), wd_ref[cols, :])
        out_ref[...] = x + _rms(acc, gpostf_ref[...])

    @pl.when(i == 0)
    def _():
        mix_into(0)

    @pl.when(i > 0)
    def _():
        ffn_from((i - 1) & 1)
        mix_into(i & 1)


def _mix_ffn(x, o, g, yg, u, dsk, gn, wglu, s5n, wo, gpost, gpre, wg, wu, wd, gpostf, tm, steps):
    t = x.shape[0]
    n = t // tm
    cur = lambda i: jnp.minimum(i, n - 1)
    row = lambda w: pl.BlockSpec((tm, w), lambda i: (cur(i), 0))
    fixed = lambda shape: pl.BlockSpec(shape, lambda i: (0,) * len(shape), pipeline_mode=pl.Buffered(1))
    return pl.pallas_call(
        functools.partial(_mix_ffn_kernel, steps=steps),
        grid=(n + 1,),
        in_specs=[row(D_MODEL), row(D_GLA), row(D_GLA),
                  pl.BlockSpec((S5_GROUPS, tm // steps, steps * S5_GROUP), lambda i: (0, cur(i), 0)),
                  row(D_S5), fixed((1, D_S5)),
                  fixed((1, GLA_DV)), fixed((D_S5, D_S5)), fixed((1, D_S5)),
                  fixed((D_GLA + D_S5, D_MODEL)), fixed((1, D_MODEL)),
                  fixed((1, D_MODEL)), fixed((D_MODEL, D_FF)), fixed((D_MODEL, D_FF)),
                  fixed((D_FF, D_MODEL)), fixed((1, D_MODEL))],
        out_specs=pl.BlockSpec((tm, D_MODEL), lambda i: (jnp.maximum(i - 1, 0), 0)),
        out_shape=jax.ShapeDtypeStruct((t, D_MODEL), F32),
        scratch_shapes=[pltpu.VMEM((D_S5 // LANES, tm, LANES), F32), pltpu.VMEM((2, tm, D_MODEL), F32)],
        compiler_params=pltpu.CompilerParams(dimension_semantics=("arbitrary",),
                                             vmem_limit_bytes=VMEM_LIMIT),
        name="mix_ffn",
    )(x, o, g, yg, u, dsk, gn, wglu, s5n, wo, gpost, gpre, wg, wu, wd, gpostf)


def kernel(x_prompt, x_sample, state_gla, state_s5_re, state_s5_im, meta_tokens, g_pre_mix, w_in, w_gk2, b_gk, gla_norm, s5_a_re, s5_a_im, s5_b_re, s5_b_im, s5_c_re, s5_c_im, s5_d, s5_log_dt, w_s5_glu, s5_norm, w_o, g_post_mix, g_pre_ffn, w_gate, w_up, w_down, g_post_ffn):
    assert g_pre_mix.shape[0] == 1, "single-layer step"
    bp, seq_p, _ = x_prompt.shape
    bs, seq_s, _ = x_sample.shape
    row = lambda t: t[0].reshape(1, -1)

    w = w_in[0]
    c3, c4 = 1536, 1536 + GATE_RANK
    w_p = jnp.concatenate([w[:, :c3], w[:, c4:], w[:, c3:c4],
                           jnp.zeros((D_MODEL, LANES - GATE_RANK), F32)], axis=1).astype(BF16)
    wgk_p = jnp.concatenate([w_gk2[0], jnp.zeros((LANES - GATE_RANK, GLA_KDIM), F32)], axis=0).astype(BF16)
    wo_bf = w_o[0].astype(BF16)
    s5m = _s5_prep(s5_a_re[0], s5_a_im[0], s5_b_re[0], s5_b_im[0], s5_c_re[0], s5_c_im[0], s5_log_dt[0])
    proj_w = (row(g_pre_mix), w_p, wgk_p, row(b_gk))
    pad_state = lambda h: jnp.pad(h, ((0, 0), (0, 0), (0, LANES - S5_STATE)))

    def finish(x, o, g, yg, u, tm, steps):
        return _mix_ffn(x, o, g, yg, u, row(s5_d), row(gla_norm), w_s5_glu[0].astype(BF16), row(s5_norm),
                        wo_bf, row(g_post_mix), row(g_pre_ffn), w_gate[0].astype(BF16),
                        w_up[0].astype(BF16), w_down[0].astype(BF16), row(g_post_ffn), tm, steps)

    xm = jnp.broadcast_to(meta_tokens[None], (bp, N_META, D_MODEL)).reshape(bp * N_META, D_MODEL)
    q, k, v, _, _, lg, ug = _in_proj(xm, *proj_w, bp * N_META, S5_BLOCK)
    r3 = lambda t, b, l: t.reshape(b, l, t.shape[-1])
    _, s_meta = _gla(r3(q, bp, N_META), r3(k, bp, N_META), r3(v, bp, N_META), r3(lg, bp, N_META),
                     jnp.zeros((bp, GLA_KDIM, GLA_DV), F32), bp, bp, N_META)
    zh = jnp.zeros((S5_GROUPS, bp, LANES), F32)
    _, hm_re, hm_im = _s5(ug, s5m[S5_BLOCK], zh, zh, 1, bp)

    xp = x_prompt.reshape(bp * seq_p, D_MODEL)
    q, k, v, g, u, lg, ug = _in_proj(xp, *proj_w, 512, S5_BLOCK)
    o, s_p = _gla(r3(q, bp, seq_p), r3(k, bp, seq_p), r3(v, bp, seq_p), r3(lg, bp, seq_p), s_meta, bp, 1, GLA_CHUNK)
    yg, hp_re, hp_im = _s5(ug, s5m[S5_BLOCK], hm_re, hm_im, bp, 32)
    y_prompt = finish(xp, o.reshape(bp * seq_p, D_GLA), g, yg, u, 512, S5_BLOCK)

    xs = x_sample.reshape(bs * seq_s, D_MODEL)
    to_g = lambda t: pad_state(jnp.swapaxes(t[0], 0, 1))
    q, k, v, g, u, lg, ug = _in_proj(xs, *proj_w, 512, seq_s)
    o, s_s = _gla(r3(q, bs, seq_s), r3(k, bs, seq_s), r3(v, bs, seq_s), r3(lg, bs, seq_s),
                  state_gla[0].reshape(bs, GLA_KDIM, GLA_DV), 32, 16, seq_s)
    yg, hs_re, hs_im = _s5(ug, s5m[seq_s], to_g(state_s5_re), to_g(state_s5_im), 1, bs)
    y_sample = finish(xs, o.reshape(bs * seq_s, D_GLA), g, yg, u, 512, seq_s)

    gla_out = lambda s, b: s.reshape(1, b, GLA_HEADS, GLA_DK, GLA_DV)
    s5_out = lambda h: jnp.swapaxes(h[:, :, :S5_STATE], 0, 1)[None]
    return (y_prompt.reshape(bp, seq_p, D_MODEL), y_sample.reshape(bs, seq_s, D_MODEL),
            gla_out(s_p, bp), s5_out(hp_re), s5_out(hp_im),
            gla_out(s_s, bs), s5_out(hs_re), s5_out(hs_im))
```

```python
import functools

import jax
import jax.numpy as jnp
from jax import lax
from jax.experimental import pallas as pl
from jax.experimental.pallas import tpu as pltpu

F32 = jnp.float32
BF16 = jnp.bfloat16

D_MODEL = 1024
D_GLA = 512
GLA_HEADS = 4
GLA_DV = 128
GLA_DK = 64
GLA_KDIM = 256
GATE_RANK = 16
GATE_NORM = 16.0
GLA_CHUNK = 64
D_S5 = 512
S5_GROUP = 16
S5_GROUPS = 32
S5_STATE = 64
N_META = 16
D_FF = 2816
EPS = 1e-6
LANES = 128
S5_BLOCK = 16
SLAB_GROUPS = LANES // S5_GROUP
PROJ_W = 2176
FF_CHUNK = 256
VMEM_LIMIT = 48 * 1024 * 1024


def _rms(x, g):
    return x * lax.rsqrt(jnp.mean(x * x, axis=-1, keepdims=True) + EPS) * g


def _dot(a, b):
    return jnp.dot(a, b, preferred_element_type=F32)


def _dot_nt(a, b):
    return lax.dot_general(a, b, (((1,), (1,)), ((), ())), preferred_element_type=F32)


def _dot_tn(a, b):
    return lax.dot_general(a, b, (((0,), (0,)), ((), ())), preferred_element_type=F32)


def _const_spec(shape):
    zeros = (0,) * len(shape)
    return pl.BlockSpec(shape, lambda *_: zeros)


PREP_GROUPS = 8


def _cmul(a, b):
    return a[0] * b[0] - a[1] * b[1], a[0] * b[1] + a[1] * b[0]


def _unit_powers(c1, s1, expo, n_bits):
    acc = (jnp.ones_like(c1), jnp.zeros_like(c1))
    base = (c1, s1)
    squares = [base]
    for bit in range(n_bits):
        take = ((expo >> bit) & 1) == 1
        nxt = _cmul(acc, base)
        acc = (jnp.where(take, nxt[0], acc[0]), jnp.where(take, nxt[1], acc[1]))
        base = _cmul(base, base)
        squares.append(base)
    return acc, squares


def _s5_prep_kernel(arc_ref, aic_ref, air_ref, ldt_ref, br_ref, bi_ref, cr_ref, ci_ref,
                    mi16_ref, mi8_ref, inr16_ref, ini16_ref, inr8_ref, ini8_ref,
                    outr16_ref, outi16_ref, outr8_ref, outi8_ref, a16r_ref, a16i_ref, a8r_ref, a8i_ref):
    n, q = S5_STATE, S5_BLOCK
    w = q * S5_GROUP
    hp = lax.Precision.HIGHEST
    lane = lax.broadcasted_iota(jnp.int32, (n, w), 1)
    t_blk = lane >> 4
    eye = lax.broadcasted_iota(jnp.int32, (n, n), 0) == lax.broadcasted_iota(jnp.int32, (n, n), 1)
    to_col = lambda r: jnp.sum(jnp.where(eye, r, 0.0), axis=1, keepdims=True)
    zrows = lambda x: jnp.concatenate([x, jnp.zeros_like(x)], axis=0)
    lane_n = lax.broadcasted_iota(jnp.int32, (n, LANES), 1)
    rep = jnp.where((lax.broadcasted_iota(jnp.int32, (S5_GROUP, w), 1) & (S5_GROUP - 1))
                    == lax.broadcasted_iota(jnp.int32, (S5_GROUP, w), 0), 1.0, 0.0)
    mo_g, rev_g, hop_g = [], [], []
    for g in range(PREP_GROUPS):
        dt = jnp.exp(ldt_ref[g])
        ang_r = air_ref[g] * dt
        c1, s1 = to_col(jnp.cos(ang_r)), to_col(jnp.sin(ang_r))
        lam_re = jnp.minimum(arc_ref[g], -1e-4)
        lam_im = aic_ref[g]
        unit, squares = _unit_powers(c1, s1, t_blk, 4)
        pm = jnp.exp(t_blk.astype(F32) * (lam_re * dt))
        pk = (pm * unit[0], pm * unit[1])
        mag = jnp.exp(lam_re * dt)
        ab = (mag * c1, mag * s1)
        p1 = _cmul(pk, ab)
        ct = tuple(lax.dot_general(r[g], rep, (((0,), (0,)), ((), ())), precision=hp,
                                   preferred_element_type=F32) for r in (cr_ref, ci_ref))
        bt = tuple(jnp.dot(r[g], rep, precision=hp, preferred_element_type=F32) for r in (br_ref, bi_ref))
        g0 = _cmul(ct, pk)
        mo = _cmul(ct, p1)
        mo_g.append((mo[0], -mo[1]))
        den = lam_re * lam_re + lam_im * lam_im
        nr, ni = ab[0] - 1.0, ab[1]
        f = ((nr * lam_re + ni * lam_im) / den, (ni * lam_re - nr * lam_im) / den)
        e = _cmul(pk, _cmul(f, bt))
        et = [zrows(x).T for x in e]
        rev_g.append([jnp.concatenate([x[(q - 1 - s) * S5_GROUP:(q - s) * S5_GROUP] for s in range(q)], axis=0)
                      for x in et])
        hops = []
        for steps in (q, q // 2):
            m = jnp.exp(float(steps) * (lam_re * dt))
            u = squares[steps.bit_length() - 1]
            hops += [m * u[0], m * u[1]]
        cols = jnp.zeros((n, LANES), F32)
        for idx, hcol in enumerate(hops):
            cols = jnp.where(lane_n == idx, hcol, cols)
        hop_g.append(zrows(cols).T)
        t0 = (jnp.dot(et[0][:S5_GROUP, :n], g0[0], precision=hp, preferred_element_type=F32)
              - jnp.dot(et[1][:S5_GROUP, :n], g0[1], precision=hp, preferred_element_type=F32))
        lane_t = lax.broadcasted_iota(jnp.int32, t0.shape, 1) >> 4
        for s in range(q):
            blk = t0 if s == 0 else jnp.where(lane_t >= s, pltpu.roll(t0, S5_GROUP * s, 1), 0.0)
            mi16_ref[g, s * S5_GROUP:(s + 1) * S5_GROUP, :] = blk.astype(BF16)
            if s < q // 2:
                mi8_ref[g, s * S5_GROUP:(s + 1) * S5_GROUP, :] = blk[:, :w // 2].astype(BF16)

    half = lambda x: pltpu.roll(x, n, 1)
    for p in range(PREP_GROUPS // 2):
        g0, g1 = 2 * p, 2 * p + 1
        for c, (in16, in8, out16, out8) in enumerate(((inr16_ref, inr8_ref, outr16_ref, outr8_ref),
                                                      (ini16_ref, ini8_ref, outi16_ref, outi8_ref))):
            lo, hi = rev_g[g0][c], half(rev_g[g1][c])
            in16[p] = jnp.concatenate([lo, hi], axis=0).astype(BF16)
            in8[p] = jnp.concatenate([lo[w // 2:], hi[w // 2:]], axis=0).astype(BF16)
            m0, m1 = mo_g[g0][c], mo_g[g1][c]
            z, zh = jnp.zeros_like(m0), jnp.zeros((n, w // 2), F32)
            out16[p] = jnp.concatenate([jnp.concatenate([m0, z], axis=1),
                                        jnp.concatenate([z, m1], axis=1)], axis=0).astype(BF16)
            out8[p] = jnp.concatenate([jnp.concatenate([m0[:, :w // 2], zh], axis=1),
                                       jnp.concatenate([zh, m1[:, :w // 2]], axis=1)], axis=0).astype(BF16)
        hops = hop_g[g0] + half(hop_g[g1])
        a16r_ref[p] = hops[0:1]
        a16i_ref[p] = hops[1:2]
        a8r_ref[p] = hops[2:3]
        a8i_ref[p] = hops[3:4]


def _s5_prep(a_re, a_im, b_re, b_im, c_re, c_im, log_dt):
    g, n, j, q = S5_GROUPS, S5_STATE, S5_GROUP, S5_BLOCK
    w, h, pg = q * j, q * j // 2, PREP_GROUPS
    col = lambda t: t.reshape(g, n, 1)
    rowv = lambda t: t.reshape(g, 1, n)
    spec = lambda a, b_: pl.BlockSpec((pg, a, b_), lambda i: (i, 0, 0))
    pspec = lambda a, b_: pl.BlockSpec((pg // 2, a, b_), lambda i: (i, 0, 0))
    pair_shapes = [(2 * w, LANES), (2 * w, LANES), (2 * h, LANES), (2 * h, LANES),
                   (LANES, 2 * w), (LANES, 2 * w), (LANES, 2 * h), (LANES, 2 * h)]
    outs = pl.pallas_call(
        _s5_prep_kernel,
        grid=(g // pg,),
        in_specs=[spec(n, 1), spec(n, 1), spec(1, n), spec(1, 1),
                  spec(n, j), spec(n, j), spec(j, n), spec(j, n)],
        out_specs=[spec(w, w), spec(h, h)] + [pspec(*s) for s in pair_shapes] + [pspec(1, LANES)] * 4,
        out_shape=[jax.ShapeDtypeStruct((g, w, w), BF16), jax.ShapeDtypeStruct((g, h, h), BF16)]
                  + [jax.ShapeDtypeStruct((g // 2,) + s, BF16) for s in pair_shapes]
                  + [jax.ShapeDtypeStruct((g // 2, 1, LANES), F32)] * 4,
        compiler_params=pltpu.CompilerParams(dimension_semantics=("parallel",)),
        name="s5_prep",
    )(col(a_re), col(a_im), rowv(a_im), log_dt.reshape(g, 1, 1),
      b_re, b_im, c_re, c_im)
    mi16, mi8, inr16, ini16, inr8, ini8, outr16, outi16, outr8, outi8, a16r, a16i, a8r, a8i = outs
    return {q: dict(m_intra=mi16, m_in_re=inr16, m_in_im=ini16, m_out_re=outr16, m_out_im=outi16,
                    a_re=a16r, a_im=a16i),
            q // 2: dict(m_intra=mi8, m_in_re=inr8, m_in_im=ini8, m_out_re=outr8, m_out_im=outi8,
                         a_re=a8r, a_im=a8i)}


def _lane_block_transpose(a):
    a = list(a)
    blk = lax.broadcasted_iota(jnp.int32, a[0].shape, 1) >> 4
    for d in (4, 2, 1):
        upper = (blk & d) != 0
        for r in range(8):
            if r & d:
                continue
            lo, hi = a[r], a[r + d]
            a[r] = jnp.where(upper, pltpu.roll(hi, 16 * d, 1), lo)
            a[r + d] = jnp.where(upper, hi, pltpu.roll(lo, LANES - 16 * d, 1))
    return a


def _in_proj_kernel(x_ref, gpre_ref, w_ref, wgk_ref, bgk_ref,
                    q_ref, k_ref, v_ref, g_ref, u_ref, lg_ref, ug_ref, u_scr, gk_scr, *, steps, n_tiles):
    i = pl.program_id(0)

    def project(slot):
        h = _rms(x_ref[...], gpre_ref[...])
        proj = _dot(h.astype(BF16), w_ref[...])
        q_ref[...] = proj[:, 0:256]
        k_ref[...] = proj[:, 256:512]
        v_ref[...] = proj[:, 512:1024].astype(v_ref.dtype)
        g_ref[...] = proj[:, 1024:1536]
        u = proj[:, 1536:2048]
        u_ref[...] = u
        for s in range(D_S5 // LANES):
            u_scr[slot, s] = u[:, s * LANES:(s + 1) * LANES]
        gk_scr[slot] = proj[:, 2048:PROJ_W]

    def tail(slot):
        z = _dot(gk_scr[slot].astype(BF16), wgk_ref[...]) + bgk_ref[...]
        lg_ref[...] = jax.nn.log_sigmoid(z) * (1.0 / GATE_NORM)
        nr = x_ref.shape[0] // steps
        rt = min(nr, 16)
        for s in range(D_S5 // LANES):
            for hf in range(steps // 8):
                for r0 in range(0, nr, rt):
                    a = [u_scr[slot, s, pl.ds(r0 * steps + hf * 8 + t, rt, stride=steps), :] for t in range(8)]
                    per_group = _lane_block_transpose(a)
                    for g in range(SLAB_GROUPS):
                        ug_ref[s * SLAB_GROUPS + g, r0:r0 + rt, hf * LANES:(hf + 1) * LANES] = (
                            per_group[g].astype(BF16))

    @pl.when(i == 0)
    def _():
        project(0)

    @pl.when((i > 0) & (i < n_tiles))
    def _():
        tail((i - 1) & 1)
        project(i & 1)

    @pl.when(i == n_tiles)
    def _():
        tail((i - 1) & 1)


def _in_proj(x, gpre, w_p, wgk_p, bgk, tm, steps, v_dtype):
    t = x.shape[0]
    n = t // tm
    cur = lambda i: jnp.minimum(i, n - 1)
    prev = lambda i: jnp.maximum(i - 1, 0)
    row = lambda w, at: pl.BlockSpec((tm, w), lambda i: (at(i), 0))
    widths = (256, 256, 512, 512, 512)
    wg = steps * S5_GROUP
    return pl.pallas_call(
        functools.partial(_in_proj_kernel, steps=steps, n_tiles=n),
        grid=(n + 1,),
        in_specs=[row(D_MODEL, cur), _const_spec((1, D_MODEL)), _const_spec((D_MODEL, PROJ_W)),
                  _const_spec((LANES, GLA_KDIM)), _const_spec((1, GLA_KDIM))],
        out_specs=[row(w, cur) for w in widths] + [row(GLA_KDIM, prev)]
                  + [pl.BlockSpec((S5_GROUPS, tm // steps, wg), lambda i: (0, prev(i), 0))],
        out_shape=[jax.ShapeDtypeStruct((t, w), v_dtype if idx == 2 else F32) for idx, w in enumerate(widths)]
                  + [jax.ShapeDtypeStruct((t, GLA_KDIM), F32)]
                  + [jax.ShapeDtypeStruct((S5_GROUPS, t // steps, wg), BF16)],
        scratch_shapes=[pltpu.VMEM((2, D_S5 // LANES, tm, LANES), F32), pltpu.VMEM((2, tm, LANES), F32)],
        compiler_params=pltpu.CompilerParams(dimension_semantics=("arbitrary",),
                                             vmem_limit_bytes=VMEM_LIMIT),
        name="in_proj",
    )(x, gpre, w_p, wgk_p, bgk)


def _gla_kernel(q_ref, k_ref, v_ref, lg_ref, s0_ref, o_ref, s_ref, *, bb, sg, chunk):
    r = sg * chunk
    shift = chunk.bit_length() - 1

    @pl.when(pl.program_id(1) == 0)
    def _():
        s_ref[...] = s0_ref[...]

    hk = GLA_HEADS * r
    ri = lax.broadcasted_iota(jnp.int32, (r, r), 0)
    ci = lax.broadcasted_iota(jnp.int32, (r, r), 1)
    tri_bf = jnp.where(((ri >> shift) == (ci >> shift)) & (ri >= ci), 1.0, 0.0).astype(BF16)
    ri4 = lax.broadcasted_iota(jnp.int32, (r, hk), 0)
    ci4 = lax.broadcasted_iota(jnp.int32, (r, hk), 1) & (r - 1)
    causal4 = ((ri4 >> shift) == (ci4 >> shift)) & (ri4 >= ci4)
    assert r >= LANES or sg == 1
    stack_head = lax.broadcasted_iota(jnp.int32, (hk, 1), 0) >> (r.bit_length() - 1)
    k_head = lax.broadcasted_iota(jnp.int32, (1, GLA_KDIM), 1) >> 6
    v_head = lax.broadcasted_iota(jnp.int32, (1, D_GLA), 1) >> 7
    s_head = lax.broadcasted_iota(jnp.int32, (GLA_KDIM, 1), 0) >> 6
    k_diag, v_diag, s_diag = stack_head == k_head, stack_head == v_head, s_head == v_head
    tile4 = lambda t: jnp.concatenate([t] * GLA_HEADS, axis=0)

    for gi in range(bb // sg):
        seqs = slice(gi * sg, (gi + 1) * sg)
        lg = lg_ref[seqs].reshape(r, GLA_KDIM)
        lg_hi = lg.astype(BF16)
        lg_lo = (lg - lg_hi.astype(F32)).astype(BF16)
        b2 = _dot(tri_bf, jnp.concatenate([lg_hi, lg_lo], axis=1))
        b = b2[:, :GLA_KDIM] + b2[:, GLA_KDIM:]
        lasts = [b[(i + 1) * chunk - 1:(i + 1) * chunk, :] for i in range(sg)]
        bl = jnp.concatenate([jnp.broadcast_to(t, (chunk, GLA_KDIM)) for t in lasts], axis=0)
        k = k_ref[seqs].reshape(r, GLA_KDIM)
        v_bf = v_ref[seqs].reshape(r, D_GLA).astype(BF16)
        qd = (q_ref[seqs].reshape(r, GLA_KDIM) * (GLA_DK ** -0.5) * jnp.exp(b)).astype(BF16)
        ki = (k * jnp.exp(-b)).astype(BF16)
        ke = k * jnp.exp(bl - b)

        ki_bd = jnp.where(k_diag, tile4(ki), 0.0)
        att = jnp.where(causal4, _dot_nt(qd, ki_bd), 0.0).astype(BF16)
        v_bd = jnp.where(v_diag, tile4(v_bf), 0.0)
        o_intra = _dot(att, v_bd)

        if r < LANES:
            aug_t = jnp.concatenate([ke, jnp.broadcast_to(lasts[0], (LANES - r, GLA_KDIM))], axis=0).T
            ke_t, bl_t = aug_t, aug_t[:, r:]
        else:
            ke_t = ke.T
            bl_t = jnp.concatenate(lasts + [jnp.zeros((r - sg, GLA_KDIM), F32)], axis=0).T
        ke_t = ke_t.astype(BF16)

        for i in range(sg):
            seq = gi * sg + i
            rows = slice(i * chunk, (i + 1) * chunk)
            s_old = s_ref[seq]
            s_bd = jnp.where(s_diag, jnp.concatenate([s_old.astype(BF16)] * GLA_HEADS, axis=1), 0.0)
            o_ref[seq] = o_intra[rows] + _dot(qd[rows], s_bd)
            upd = [_dot(ke_t[h * GLA_DK:(h + 1) * GLA_DK, rows], v_bf[rows, h * GLA_DV:(h + 1) * GLA_DV])
                   for h in range(GLA_HEADS)]
            s_ref[seq] = jnp.exp(bl_t[:, i:i + 1]) * s_old + jnp.concatenate(upd, axis=0)


def _gla(q, k, v, lg, s0, bb, sg, chunk):
    b, l, _ = q.shape
    blk = lambda w: pl.BlockSpec((bb, chunk, w), lambda i, c: (i, c, 0))
    sspec = pl.BlockSpec((bb, GLA_KDIM, GLA_DV), lambda i, c: (i, 0, 0))
    return pl.pallas_call(
        functools.partial(_gla_kernel, bb=bb, sg=sg, chunk=chunk),
        grid=(b // bb, l // chunk),
        in_specs=[blk(GLA_KDIM), blk(GLA_KDIM), blk(D_GLA), blk(GLA_KDIM), sspec],
        out_specs=[blk(D_GLA), sspec],
        out_shape=[jax.ShapeDtypeStruct((b, l, D_GLA), F32),
                   jax.ShapeDtypeStruct((b, GLA_KDIM, GLA_DV), F32)],
        compiler_params=pltpu.CompilerParams(dimension_semantics=("parallel", "arbitrary"),
                                             vmem_limit_bytes=VMEM_LIMIT),
        name=f"gla_c{chunk}",
    )(q, k, v, lg, s0)


def _s5_kernel(ug_ref, mi_ref, minr_ref, mini_ref, mor_ref, moi_ref, ar_ref, ai_ref,
               h0r_ref, h0i_ref, yg_ref, hfr_ref, hfi_ref, vr_s, vi_s, hr_s, hi_s, str_s, sti_s, *, nb, cb):
    gs = SLAB_GROUPS // 2
    n_pairs = S5_GROUPS // 2
    w = ug_ref.shape[-1]
    rows = nb * cb
    batch = cb if nb == 1 else nb
    n_blocks = rows // batch
    interleave = nb > 1 and cb > 1
    first_pair = pl.program_id(0) * gs
    seq0 = pl.program_id(1) * batch if nb == 1 else 0

    def state_rows(p):
        return pl.ds(seq0 * n_pairs + first_pair + p, batch, stride=n_pairs)

    def load_state():
        for p in range(gs):
            str_s[p] = h0r_ref[state_rows(p), :]
            sti_s[p] = h0i_ref[state_rows(p), :]

    if nb == 1:
        load_state()
    else:
        pl.when(pl.program_id(1) == 0)(load_state)

    def to_scan_order(ref, g, val):
        if not interleave:
            ref[g] = val
        else:
            for b in range(nb):
                ref[g, pl.ds(b, cb, stride=nb), :] = val[b * cb:(b + 1) * cb]

    def from_scan_order(ref, g):
        if not interleave:
            return ref[g]
        return jnp.concatenate([ref[g, pl.ds(b, cb, stride=nb), :] for b in range(nb)], axis=0)

    ub = [ug_ref[g].reshape(rows, w) for g in range(2 * gs)]
    for g in range(gs):
        pair_u = jnp.concatenate([ub[2 * g], ub[2 * g + 1]], axis=1)
        to_scan_order(vr_s, g, _dot(pair_u, minr_ref[g]))
        to_scan_order(vi_s, g, _dot(pair_u, mini_ref[g]))
    ar = [ar_ref[g] for g in range(gs)]
    ai = [ai_ref[g] for g in range(gs)]

    def body(c, carry):
        rows = pl.ds(pl.multiple_of(c * batch, batch), batch)
        new = []
        for g in range(gs):
            hr, hi = carry[2 * g], carry[2 * g + 1]
            hr_s[g, rows, :] = hr
            hi_s[g, rows, :] = hi
            new.append(ar[g] * hr - ai[g] * hi + vr_s[g, rows, :])
            new.append(ar[g] * hi + ai[g] * hr + vi_s[g, rows, :])
        return tuple(new)

    init = tuple(ref[g] for g in range(gs) for ref in (str_s, sti_s))
    fin = lax.fori_loop(0, n_blocks, body, init)
    for g in range(gs):
        str_s[g] = fin[2 * g]
        sti_s[g] = fin[2 * g + 1]
        hfr_ref[state_rows(g), :] = fin[2 * g]
        hfi_ref[state_rows(g), :] = fin[2 * g + 1]
        y_state = (_dot(from_scan_order(hr_s, g).astype(BF16), mor_ref[g])
                   + _dot(from_scan_order(hi_s, g).astype(BF16), moi_ref[g]))
        for k in range(2):
            y = _dot(ub[2 * g + k], mi_ref[2 * g + k]) + y_state[:, k * w:(k + 1) * w]
            yg_ref[2 * g + k] = y.reshape(yg_ref.shape[1:])


def _s5(ug, m, h0_re, h0_im, nb, cb):
    g_all, r, w = ug.shape
    nc = r // nb
    batch = cb if nb == 1 else nb
    n, gs = LANES, SLAB_GROUPS
    uspec = pl.BlockSpec((gs, nb, cb, w), lambda s, i: (s, 0, i, 0))
    per_g = lambda a, b_: pl.BlockSpec((gs, a, b_), lambda s, i: (s, 0, 0))
    per_p = lambda a, b_: pl.BlockSpec((gs // 2, a, b_), lambda s, i: (s, 0, 0))
    hspec = _const_spec(h0_re.shape)
    yg, hf_re, hf_im = pl.pallas_call(
        functools.partial(_s5_kernel, nb=nb, cb=cb),
        grid=(g_all // gs, nc // cb),
        in_specs=[uspec, per_g(w, w), per_p(2 * w, n), per_p(2 * w, n), per_p(n, 2 * w), per_p(n, 2 * w),
                  per_p(1, n), per_p(1, n), hspec, hspec],
        out_specs=[uspec, hspec, hspec],
        out_shape=[jax.ShapeDtypeStruct((g_all, nb, nc, w), F32),
                   jax.ShapeDtypeStruct(h0_re.shape, F32),
                   jax.ShapeDtypeStruct(h0_im.shape, F32)],
        scratch_shapes=[pltpu.VMEM((gs // 2, nb * cb, n), F32)] * 4 + [pltpu.VMEM((gs // 2, batch, n), F32)] * 2,
        compiler_params=pltpu.CompilerParams(dimension_semantics=("arbitrary", "arbitrary"),
                                             vmem_limit_bytes=VMEM_LIMIT),
        name=f"s5_w{w}",
    )(ug.reshape(g_all, nb, nc, w), m["m_intra"], m["m_in_re"], m["m_in_im"], m["m_out_re"], m["m_out_im"],
      m["a_re"], m["a_im"], h0_re, h0_im)
    return yg.reshape(g_all, r, w), hf_re, hf_im


def _mix_ffn_kernel(x_ref, o_ref, g_ref, yg_ref, u_ref, dsk_ref, gn_ref, wglu_ref, s5n_ref, wo_ref, gpost_ref,
                    gpre_ref, wg_ref, wu_ref, wd_ref, gpostf_ref, out_ref, y_scr, x1_scr, *, steps):
    i = pl.program_id(0)

    def mix_into(slot):
        nr = x_ref.shape[0] // steps
        rt = min(nr, 8)
        for s in range(D_S5 // LANES):
            for hf in range(steps // 8):
                for r0 in range(0, nr, rt):
                    a = [yg_ref[s * SLAB_GROUPS + g, r0:r0 + rt, hf * LANES:(hf + 1) * LANES]
                         for g in range(SLAB_GROUPS)]
                    per_step = _lane_block_transpose(a)
                    for t in range(8):
                        y_scr[s, pl.ds(r0 * steps + hf * 8 + t, rt, stride=steps), :] = per_step[t]
        y5 = jnp.concatenate([y_scr[s] for s in range(D_S5 // LANES)], axis=1) + dsk_ref[...] * u_ref[...]
        o = o_ref[...]
        gn = gn_ref[...]
        heads = []
        for h in range(GLA_HEADS):
            heads.append(_rms(o[:, h * GLA_DV:(h + 1) * GLA_DV], gn))
        og = jnp.concatenate(heads, axis=1) * jax.nn.silu(g_ref[...])
        y = jax.nn.gelu(y5)
        y = y * jax.nn.sigmoid(_dot(y.astype(BF16), wglu_ref[...]))
        y = _rms(y, s5n_ref[...])
        mix = _dot(og.astype(BF16), wo_ref[:D_GLA, :]) + _dot(y.astype(BF16), wo_ref[D_GLA:, :])
        x1_scr[slot] = x_ref[...] + _rms(mix, gpost_ref[...])

    def ffn_from(slot):
        x = x1_scr[slot]
        h = _rms(x, gpre_ref[...]).astype(BF16)
        acc = jnp.zeros(x.shape, F32)
        for c in range(D_FF // FF_CHUNK):
            cols = slice(c * FF_CHUNK, (c + 1) * FF_CHUNK)
            act = jax.nn.silu(_dot(h, wg_ref[:, cols])) * _dot(h, wu_ref[:, cols])
            acc = acc + _dot(act.astype(BF16), wd_ref[cols, :])
        out_ref[...] = x + _rms(acc, gpostf_ref[...])

    @pl.when(i == 0)
    def _():
        mix_into(0)

    @pl.when(i > 0)
    def _():
        ffn_from((i - 1) & 1)
        mix_into(i & 1)


def _mix_ffn(x, o, g, yg, u, dsk, gn, wglu, s5n, wo, gpost, gpre, wg, wu, wd, gpostf, tm, steps):
    t = x.shape[0]
    n = t // tm
    cur = lambda i: jnp.minimum(i, n - 1)
    row = lambda w: pl.BlockSpec((tm, w), lambda i: (cur(i), 0))
    fixed = lambda shape: pl.BlockSpec(shape, lambda i: (0,) * len(shape), pipeline_mode=pl.Buffered(1))
    return pl.pallas_call(
        functools.partial(_mix_ffn_kernel, steps=steps),
        grid=(n + 1,),
        in_specs=[row(D_MODEL), row(D_GLA), row(D_GLA),
                  pl.BlockSpec((S5_GROUPS, tm // steps, steps * S5_GROUP), lambda i: (0, cur(i), 0)),
                  row(D_S5), fixed((1, D_S5)),
                  fixed((1, GLA_DV)), fixed((D_S5, D_S5)), fixed((1, D_S5)),
                  fixed((D_GLA + D_S5, D_MODEL)), fixed((1, D_MODEL)),
                  fixed((1, D_MODEL)), fixed((D_MODEL, D_FF)), fixed((D_MODEL, D_FF)),
                  fixed((D_FF, D_MODEL)), fixed((1, D_MODEL))],
        out_specs=pl.BlockSpec((tm, D_MODEL), lambda i: (jnp.maximum(i - 1, 0), 0)),
        out_shape=jax.ShapeDtypeStruct((t, D_MODEL), F32),
        scratch_shapes=[pltpu.VMEM((D_S5 // LANES, tm, LANES), F32), pltpu.VMEM((2, tm, D_MODEL), F32)],
        compiler_params=pltpu.CompilerParams(dimension_semantics=("arbitrary",),
                                             vmem_limit_bytes=VMEM_LIMIT),
        name="mix_ffn",
    )(x, o, g, yg, u, dsk, gn, wglu, s5n, wo, gpost, gpre, wg, wu, wd, gpostf)


def kernel(x_prompt, x_sample, state_gla, state_s5_re, state_s5_im, meta_tokens, g_pre_mix, w_in, w_gk2, b_gk, gla_norm, s5_a_re, s5_a_im, s5_b_re, s5_b_im, s5_c_re, s5_c_im, s5_d, s5_log_dt, w_s5_glu, s5_norm, w_o, g_post_mix, g_pre_ffn, w_gate, w_up, w_down, g_post_ffn):
    assert g_pre_mix.shape[0] == 1, "single-layer step"
    bp, seq_p, _ = x_prompt.shape
    bs, seq_s, _ = x_sample.shape
    row = lambda t: t[0].reshape(1, -1)

    w = w_in[0]
    c3, c4 = 1536, 1536 + GATE_RANK
    w_p = jnp.concatenate([w[:, :c3], w[:, c4:], w[:, c3:c4],
                           jnp.zeros((D_MODEL, LANES - GATE_RANK), F32)], axis=1).astype(BF16)
    wgk_p = jnp.concatenate([w_gk2[0], jnp.zeros((LANES - GATE_RANK, GLA_KDIM), F32)], axis=0).astype(BF16)
    wo_bf = w_o[0].astype(BF16)
    s5m = _s5_prep(s5_a_re[0], s5_a_im[0], s5_b_re[0], s5_b_im[0], s5_c_re[0], s5_c_im[0], s5_log_dt[0])
    proj_w = (row(g_pre_mix), w_p, wgk_p, row(b_gk))

    def finish(x, o, g, yg, u, tm, steps):
        return _mix_ffn(x, o, g, yg, u, row(s5_d), row(gla_norm), w_s5_glu[0].astype(BF16), row(s5_norm),
                        wo_bf, row(g_post_mix), row(g_pre_ffn), w_gate[0].astype(BF16),
                        w_up[0].astype(BF16), w_down[0].astype(BF16), row(g_post_ffn), tm, steps)

    xm = jnp.broadcast_to(meta_tokens[None], (bp, N_META, D_MODEL)).reshape(bp * N_META, D_MODEL)
    q, k, v, _, _, lg, ug = _in_proj(xm, *proj_w, bp * N_META, S5_BLOCK, F32)
    r3 = lambda t, b, l: t.reshape(b, l, t.shape[-1])
    _, s_meta = _gla(r3(q, bp, N_META), r3(k, bp, N_META), r3(v, bp, N_META), r3(lg, bp, N_META),
                     jnp.zeros((bp, GLA_KDIM, GLA_DV), F32), bp, bp, N_META)
    zh = jnp.zeros((bp * S5_GROUPS // 2, LANES), F32)
    _, hm_re, hm_im = _s5(ug, s5m[S5_BLOCK], zh, zh, 1, bp)

    xp = x_prompt.reshape(bp * seq_p, D_MODEL)
    q, k, v, g, u, lg, ug = _in_proj(xp, *proj_w, 512, S5_BLOCK, BF16)
    o, s_p = _gla(r3(q, bp, seq_p), r3(k, bp, seq_p), r3(v, bp, seq_p), r3(lg, bp, seq_p), s_meta, bp, 1, GLA_CHUNK)
    yg, hp_re, hp_im = _s5(ug, s5m[S5_BLOCK], hm_re, hm_im, bp, 32)
    y_prompt = finish(xp, o.reshape(bp * seq_p, D_GLA), g, yg, u, 512, S5_BLOCK)

    xs = x_sample.reshape(bs * seq_s, D_MODEL)
    to_g = lambda t: t[0].reshape(bs * S5_GROUPS // 2, LANES)
    q, k, v, g, u, lg, ug = _in_proj(xs, *proj_w, 512, seq_s, F32)
    o, s_s = _gla(r3(q, bs, seq_s), r3(k, bs, seq_s), r3(v, bs, seq_s), r3(lg, bs, seq_s),
                  state_gla[0].reshape(bs, GLA_KDIM, GLA_DV), 32, 16, seq_s)
    yg, hs_re, hs_im = _s5(ug, s5m[seq_s], to_g(state_s5_re), to_g(state_s5_im), 1, bs)
    y_sample = finish(xs, o.reshape(bs * seq_s, D_GLA), g, yg, u, 512, seq_s)

    gla_out = lambda s, b: s.reshape(1, b, GLA_HEADS, GLA_DK, GLA_DV)
    s5_out = lambda h: h.reshape(1, -1, S5_GROUPS, S5_STATE)
    return (y_prompt.reshape(bp, seq_p, D_MODEL), y_sample.reshape(bs, seq_s, D_MODEL),
            gla_out(s_p, bp), s5_out(hp_re), s5_out(hp_im),
            gla_out(s_s, bs), s5_out(hs_re), s5_out(hs_im))
```

```python
import functools

import jax
import jax.numpy as jnp
from jax import lax
from jax.experimental import pallas as pl
from jax.experimental.pallas import tpu as pltpu

F32 = jnp.float32
BF16 = jnp.bfloat16

D_MODEL = 1024
D_GLA = 512
GLA_HEADS = 4
GLA_DV = 128
GLA_DK = 64
GLA_KDIM = 256
GATE_RANK = 16
GATE_NORM = 16.0
GLA_CHUNK = 64
D_S5 = 512
S5_GROUP = 16
S5_GROUPS = 32
S5_STATE = 64
N_META = 16
D_FF = 2816
EPS = 1e-6
LANES = 128
S5_BLOCK = 16
SLAB_GROUPS = LANES // S5_GROUP
PROJ_W = 2176
FF_CHUNK = 256
VMEM_LIMIT = 48 * 1024 * 1024


def _rms(x, g):
    return x * lax.rsqrt(jnp.mean(x * x, axis=-1, keepdims=True) + EPS) * g


def _dot(a, b):
    return jnp.dot(a, b, preferred_element_type=F32)


def _dot_nt(a, b):
    return lax.dot_general(a, b, (((1,), (1,)), ((), ())), preferred_element_type=F32)


def _dot_tn(a, b):
    return lax.dot_general(a, b, (((0,), (0,)), ((), ())), preferred_element_type=F32)


def _const_spec(shape):
    zeros = (0,) * len(shape)
    return pl.BlockSpec(shape, lambda *_: zeros)


PREP_GROUPS = 8


def _cmul(a, b):
    return a[0] * b[0] - a[1] * b[1], a[0] * b[1] + a[1] * b[0]


def _unit_powers(c1, s1, expo, n_bits):
    acc = (jnp.ones_like(c1), jnp.zeros_like(c1))
    base = (c1, s1)
    squares = [base]
    for bit in range(n_bits):
        take = ((expo >> bit) & 1) == 1
        nxt = _cmul(acc, base)
        acc = (jnp.where(take, nxt[0], acc[0]), jnp.where(take, nxt[1], acc[1]))
        base = _cmul(base, base)
        squares.append(base)
    return acc, squares


def _s5_prep_kernel(arc_ref, aic_ref, air_ref, ldt_ref, br_ref, bi_ref, cr_ref, ci_ref,
                    mi16_ref, mi8_ref, inr16_ref, ini16_ref, inr8_ref, ini8_ref,
                    outr16_ref, outi16_ref, outr8_ref, outi8_ref, a16r_ref, a16i_ref, a8r_ref, a8i_ref):
    n, q = S5_STATE, S5_BLOCK
    w = q * S5_GROUP
    hp = lax.Precision.HIGHEST
    lane = lax.broadcasted_iota(jnp.int32, (n, w), 1)
    t_blk = lane >> 4
    eye = lax.broadcasted_iota(jnp.int32, (n, n), 0) == lax.broadcasted_iota(jnp.int32, (n, n), 1)
    to_col = lambda r: jnp.sum(jnp.where(eye, r, 0.0), axis=1, keepdims=True)
    zrows = lambda x: jnp.concatenate([x, jnp.zeros_like(x)], axis=0)
    lane_n = lax.broadcasted_iota(jnp.int32, (n, LANES), 1)
    rep = jnp.where((lax.broadcasted_iota(jnp.int32, (S5_GROUP, w), 1) & (S5_GROUP - 1))
                    == lax.broadcasted_iota(jnp.int32, (S5_GROUP, w), 0), 1.0, 0.0)
    mo_g, rev_g, hop_g = [], [], []
    for g in range(PREP_GROUPS):
        dt = jnp.exp(ldt_ref[g])
        ang_r = air_ref[g] * dt
        c1, s1 = to_col(jnp.cos(ang_r)), to_col(jnp.sin(ang_r))
        lam_re = jnp.minimum(arc_ref[g], -1e-4)
        lam_im = aic_ref[g]
        unit, squares = _unit_powers(c1, s1, t_blk, 4)
        pm = jnp.exp(t_blk.astype(F32) * (lam_re * dt))
        pk = (pm * unit[0], pm * unit[1])
        mag = jnp.exp(lam_re * dt)
        ab = (mag * c1, mag * s1)
        p1 = _cmul(pk, ab)
        ct = tuple(lax.dot_general(r[g], rep, (((0,), (0,)), ((), ())), precision=hp,
                                   preferred_element_type=F32) for r in (cr_ref, ci_ref))
        bt = tuple(jnp.dot(r[g], rep, precision=hp, preferred_element_type=F32) for r in (br_ref, bi_ref))
        g0 = _cmul(ct, pk)
        mo = _cmul(ct, p1)
        mo_g.append((mo[0], -mo[1]))
        den = lam_re * lam_re + lam_im * lam_im
        nr, ni = ab[0] - 1.0, ab[1]
        f = ((nr * lam_re + ni * lam_im) / den, (ni * lam_re - nr * lam_im) / den)
        e = _cmul(pk, _cmul(f, bt))
        et = [zrows(x).T for x in e]
        rev_g.append([jnp.concatenate([x[(q - 1 - s) * S5_GROUP:(q - s) * S5_GROUP] for s in range(q)], axis=0)
                      for x in et])
        hops = []
        for steps in (q, q // 2):
            m = jnp.exp(float(steps) * (lam_re * dt))
            u = squares[steps.bit_length() - 1]
            hops += [m * u[0], m * u[1]]
        cols = jnp.zeros((n, LANES), F32)
        for idx, hcol in enumerate(hops):
            cols = jnp.where(lane_n == idx, hcol, cols)
        hop_g.append(zrows(cols).T)
        t0 = (jnp.dot(et[0][:S5_GROUP, :n], g0[0], precision=hp, preferred_element_type=F32)
              - jnp.dot(et[1][:S5_GROUP, :n], g0[1], precision=hp, preferred_element_type=F32))
        lane_t = lax.broadcasted_iota(jnp.int32, t0.shape, 1) >> 4
        for s in range(q):
            blk = t0 if s == 0 else jnp.where(lane_t >= s, pltpu.roll(t0, S5_GROUP * s, 1), 0.0)
            mi16_ref[g, s * S5_GROUP:(s + 1) * S5_GROUP, :] = blk.astype(BF16)
            if s < q // 2:
                mi8_ref[g, s * S5_GROUP:(s + 1) * S5_GROUP, :] = blk[:, :w // 2].astype(BF16)

    half = lambda x: pltpu.roll(x, n, 1)
    for p in range(PREP_GROUPS // 2):
        g0, g1 = 2 * p, 2 * p + 1
        for c, (in16, in8, out16, out8) in enumerate(((inr16_ref, inr8_ref, outr16_ref, outr8_ref),
                                                      (ini16_ref, ini8_ref, outi16_ref, outi8_ref))):
            lo, hi = rev_g[g0][c], half(rev_g[g1][c])
            in16[p] = jnp.concatenate([lo, hi], axis=0).astype(BF16)
            in8[p] = jnp.concatenate([lo[w // 2:], hi[w // 2:]], axis=0).astype(BF16)
            m0, m1 = mo_g[g0][c], mo_g[g1][c]
            z, zh = jnp.zeros_like(m0), jnp.zeros((n, w // 2), F32)
            out16[p] = jnp.concatenate([jnp.concatenate([m0, z], axis=1),
                                        jnp.concatenate([z, m1], axis=1)], axis=0).astype(BF16)
            out8[p] = jnp.concatenate([jnp.concatenate([m0[:, :w // 2], zh], axis=1),
                                       jnp.concatenate([zh, m1[:, :w // 2]], axis=1)], axis=0).astype(BF16)
        hops = hop_g[g0] + half(hop_g[g1])
        a16r_ref[p] = hops[0:1]
        a16i_ref[p] = hops[1:2]
        a8r_ref[p] = hops[2:3]
        a8i_ref[p] = hops[3:4]


def _s5_prep(a_re, a_im, b_re, b_im, c_re, c_im, log_dt):
    g, n, j, q = S5_GROUPS, S5_STATE, S5_GROUP, S5_BLOCK
    w, h, pg = q * j, q * j // 2, PREP_GROUPS
    col = lambda t: t.reshape(g, n, 1)
    rowv = lambda t: t.reshape(g, 1, n)
    spec = lambda a, b_: pl.BlockSpec((pg, a, b_), lambda i: (i, 0, 0))
    pspec = lambda a, b_: pl.BlockSpec((pg // 2, a, b_), lambda i: (i, 0, 0))
    pair_shapes = [(2 * w, LANES), (2 * w, LANES), (2 * h, LANES), (2 * h, LANES),
                   (LANES, 2 * w), (LANES, 2 * w), (LANES, 2 * h), (LANES, 2 * h)]
    outs = pl.pallas_call(
        _s5_prep_kernel,
        grid=(g // pg,),
        in_specs=[spec(n, 1), spec(n, 1), spec(1, n), spec(1, 1),
                  spec(n, j), spec(n, j), spec(j, n), spec(j, n)],
        out_specs=[spec(w, w), spec(h, h)] + [pspec(*s) for s in pair_shapes] + [pspec(1, LANES)] * 4,
        out_shape=[jax.ShapeDtypeStruct((g, w, w), BF16), jax.ShapeDtypeStruct((g, h, h), BF16)]
                  + [jax.ShapeDtypeStruct((g // 2,) + s, BF16) for s in pair_shapes]
                  + [jax.ShapeDtypeStruct((g // 2, 1, LANES), F32)] * 4,
        compiler_params=pltpu.CompilerParams(dimension_semantics=("parallel",)),
        name="s5_prep",
    )(col(a_re), col(a_im), rowv(a_im), log_dt.reshape(g, 1, 1),
      b_re, b_im, c_re, c_im)
    mi16, mi8, inr16, ini16, inr8, ini8, outr16, outi16, outr8, outi8, a16r, a16i, a8r, a8i = outs
    return {q: dict(m_intra=mi16, m_in_re=inr16, m_in_im=ini16, m_out_re=outr16, m_out_im=outi16,
                    a_re=a16r, a_im=a16i),
            q // 2: dict(m_intra=mi8, m_in_re=inr8, m_in_im=ini8, m_out_re=outr8, m_out_im=outi8,
                         a_re=a8r, a_im=a8i)}


def _lane_block_transpose(a):
    a = list(a)
    blk = lax.broadcasted_iota(jnp.int32, a[0].shape, 1) >> 4
    for d in (4, 2, 1):
        upper = (blk & d) != 0
        for r in range(8):
            if r & d:
                continue
            lo, hi = a[r], a[r + d]
            a[r] = jnp.where(upper, pltpu.roll(hi, 16 * d, 1), lo)
            a[r + d] = jnp.where(upper, hi, pltpu.roll(lo, LANES - 16 * d, 1))
    return a


def _in_proj_kernel(*refs, steps, n_tiles, tiles_per_seq):
    fused = tiles_per_seq is not None
    x_ref, gpre_ref, w_ref, wgk_ref, bgk_ref = refs[:5]
    if fused:
        (s0_ref, g_ref, u_ref, o_ref, ug_ref, sfin_ref,
         u_scr, gk_scr, q_scr, k_scr, v_scr, s_scr) = refs[5:]
    else:
        q_ref, k_ref, v_ref, g_ref, u_ref, lg_ref, ug_ref, u_scr, gk_scr = refs[5:]
    i = pl.program_id(0)
    tm = x_ref.shape[0]

    def project(slot):
        h = _rms(x_ref[...], gpre_ref[...])
        proj = _dot(h.astype(BF16), w_ref[...])
        if fused:
            q_scr[slot] = proj[:, 0:256]
            k_scr[slot] = proj[:, 256:512]
            v_scr[slot] = proj[:, 512:1024].astype(BF16)
        else:
            q_ref[...] = proj[:, 0:256]
            k_ref[...] = proj[:, 256:512]
            v_ref[...] = proj[:, 512:1024]
        g_ref[...] = proj[:, 1024:1536]
        u = proj[:, 1536:2048]
        u_ref[...] = u
        for s in range(D_S5 // LANES):
            u_scr[slot, s] = u[:, s * LANES:(s + 1) * LANES]
        gk_scr[slot] = proj[:, 2048:PROJ_W]

    def tail(slot):
        z = _dot(gk_scr[slot].astype(BF16), wgk_ref[...]) + bgk_ref[...]
        lg = jax.nn.log_sigmoid(z) * (1.0 / GATE_NORM)
        if fused:
            tile = i - 1
            seq = tile // tiles_per_seq
            state = jnp.where(tile % tiles_per_seq == 0, s0_ref[seq], s_scr[...])
            masks = _gla_masks(1, GLA_CHUNK)
            for c in range(tm // GLA_CHUNK):
                rows = slice(c * GLA_CHUNK, (c + 1) * GLA_CHUNK)
                o, (state,) = _gla_group(q_scr[slot, rows, :], k_scr[slot, rows, :], v_scr[slot, rows, :],
                                         lg[rows], [state], masks, GLA_CHUNK)
                o_ref[rows, :] = o
            s_scr[...] = state
            sfin_ref[seq] = state
        else:
            lg_ref[...] = lg
        nr = tm // steps
        rt = min(nr, 16)
        for s in range(D_S5 // LANES):
            for hf in range(steps // 8):
                for r0 in range(0, nr, rt):
                    a = [u_scr[slot, s, pl.ds(r0 * steps + hf * 8 + t, rt, stride=steps), :] for t in range(8)]
                    per_group = _lane_block_transpose(a)
                    for g in range(SLAB_GROUPS):
                        ug_ref[s * SLAB_GROUPS + g, r0:r0 + rt, hf * LANES:(hf + 1) * LANES] = (
                            per_group[g].astype(BF16))

    @pl.when(i == 0)
    def _():
        if fused:
            s_scr[...] = jnp.zeros(s_scr.shape, F32)
        project(0)

    @pl.when((i > 0) & (i < n_tiles))
    def _():
        tail((i - 1) & 1)
        project(i & 1)

    @pl.when(i == n_tiles)
    def _():
        tail((i - 1) & 1)


def _in_proj(x, gpre, w_p, wgk_p, bgk, tm, steps, gla_s0=None, seq_len=None):
    t = x.shape[0]
    n = t // tm
    cur = lambda i: jnp.minimum(i, n - 1)
    prev = lambda i: jnp.maximum(i - 1, 0)
    row = lambda w, at: pl.BlockSpec((tm, w), lambda i: (at(i), 0))
    wg = steps * S5_GROUP
    ug_spec = pl.BlockSpec((S5_GROUPS, tm // steps, wg), lambda i: (0, prev(i), 0))
    ug_shape = jax.ShapeDtypeStruct((S5_GROUPS, t // steps, wg), BF16)
    f32 = lambda w: jax.ShapeDtypeStruct((t, w), F32)
    in_specs = [row(D_MODEL, cur), _const_spec((1, D_MODEL)), _const_spec((D_MODEL, PROJ_W)),
                _const_spec((LANES, GLA_KDIM)), _const_spec((1, GLA_KDIM))]
    scratch = [pltpu.VMEM((2, D_S5 // LANES, tm, LANES), F32), pltpu.VMEM((2, tm, LANES), F32)]
    args = (x, gpre, w_p, wgk_p, bgk)
    if gla_s0 is None:
        tiles_per_seq = None
        out_specs = [row(w, cur) for w in (256, 256, 512, 512, 512)] + [row(GLA_KDIM, prev), ug_spec]
        out_shape = [f32(w) for w in (256, 256, 512, 512, 512)] + [f32(GLA_KDIM), ug_shape]
    else:
        tiles_per_seq = seq_len // tm
        assert seq_len % tm == 0 and tm % GLA_CHUNK == 0
        in_specs.append(_const_spec(gla_s0.shape))
        args += (gla_s0,)
        out_specs = [row(D_GLA, cur), row(D_S5, cur), row(D_GLA, prev), ug_spec, _const_spec(gla_s0.shape)]
        out_shape = [f32(D_GLA), f32(D_S5), f32(D_GLA), ug_shape, jax.ShapeDtypeStruct(gla_s0.shape, F32)]
        scratch += [pltpu.VMEM((2, tm, GLA_KDIM), F32), pltpu.VMEM((2, tm, GLA_KDIM), F32),
                    pltpu.VMEM((2, tm, D_GLA), BF16), pltpu.VMEM(gla_s0.shape[1:], F32)]
    return pl.pallas_call(
        functools.partial(_in_proj_kernel, steps=steps, n_tiles=n, tiles_per_seq=tiles_per_seq),
        grid=(n + 1,),
        in_specs=in_specs,
        out_specs=out_specs,
        out_shape=out_shape,
        scratch_shapes=scratch,
        compiler_params=pltpu.CompilerParams(dimension_semantics=("arbitrary",),
                                             vmem_limit_bytes=VMEM_LIMIT),
        name="in_proj" if gla_s0 is None else "in_proj_gla",
    )(*args)


def _gla_masks(sg, chunk):
    r = sg * chunk
    assert r >= LANES or sg == 1
    shift = chunk.bit_length() - 1
    hk = GLA_HEADS * r
    ri = lax.broadcasted_iota(jnp.int32, (r, r), 0)
    ci = lax.broadcasted_iota(jnp.int32, (r, r), 1)
    ri4 = lax.broadcasted_iota(jnp.int32, (r, hk), 0)
    ci4 = lax.broadcasted_iota(jnp.int32, (r, hk), 1) & (r - 1)
    stack_head = lax.broadcasted_iota(jnp.int32, (hk, 1), 0) >> (r.bit_length() - 1)
    k_head = lax.broadcasted_iota(jnp.int32, (1, GLA_KDIM), 1) >> 6
    v_head = lax.broadcasted_iota(jnp.int32, (1, D_GLA), 1) >> 7
    s_head = lax.broadcasted_iota(jnp.int32, (GLA_KDIM, 1), 0) >> 6
    return dict(
        tri_bf=jnp.where(((ri >> shift) == (ci >> shift)) & (ri >= ci), 1.0, 0.0).astype(BF16),
        causal4=((ri4 >> shift) == (ci4 >> shift)) & (ri4 >= ci4),
        k_diag=stack_head == k_head, v_diag=stack_head == v_head, s_diag=s_head == v_head)


def _gla_group(q, k, v_bf, lg, states, m, chunk):
    sg = len(states)
    r = sg * chunk
    tile4 = lambda t: jnp.concatenate([t] * GLA_HEADS, axis=0)
    lg_hi = lg.astype(BF16)
    lg_lo = (lg - lg_hi.astype(F32)).astype(BF16)
    b2 = _dot(m["tri_bf"], jnp.concatenate([lg_hi, lg_lo], axis=1))
    b = b2[:, :GLA_KDIM] + b2[:, GLA_KDIM:]
    lasts = [b[(i + 1) * chunk - 1:(i + 1) * chunk, :] for i in range(sg)]
    bl = jnp.concatenate([jnp.broadcast_to(t, (chunk, GLA_KDIM)) for t in lasts], axis=0)
    qd = (q * (GLA_DK ** -0.5) * jnp.exp(b)).astype(BF16)
    ki = (k * jnp.exp(-b)).astype(BF16)
    ke = k * jnp.exp(bl - b)

    ki_bd = jnp.where(m["k_diag"], tile4(ki), 0.0)
    att = jnp.where(m["causal4"], _dot_nt(qd, ki_bd), 0.0).astype(BF16)
    v_bd = jnp.where(m["v_diag"], tile4(v_bf), 0.0)
    o_intra = _dot(att, v_bd)

    if r < LANES:
        aug_t = jnp.concatenate([ke, jnp.broadcast_to(lasts[0], (LANES - r, GLA_KDIM))], axis=0).T
        ke_t, bl_t = aug_t, aug_t[:, r:]
    else:
        ke_t = ke.T
        bl_t = jnp.concatenate(lasts + [jnp.zeros((r - sg, GLA_KDIM), F32)], axis=0).T
    ke_t = ke_t.astype(BF16)

    outs, new_states = [], []
    for i, s_old in enumerate(states):
        rows = slice(i * chunk, (i + 1) * chunk)
        s_bd = jnp.where(m["s_diag"], jnp.concatenate([s_old.astype(BF16)] * GLA_HEADS, axis=1), 0.0)
        outs.append(o_intra[rows] + _dot(qd[rows], s_bd))
        upd = [_dot(ke_t[h * GLA_DK:(h + 1) * GLA_DK, rows], v_bf[rows, h * GLA_DV:(h + 1) * GLA_DV])
               for h in range(GLA_HEADS)]
        new_states.append(jnp.exp(bl_t[:, i:i + 1]) * s_old + jnp.concatenate(upd, axis=0))
    return (outs[0] if sg == 1 else jnp.concatenate(outs, axis=0)), new_states


def _gla_kernel(q_ref, k_ref, v_ref, lg_ref, s0_ref, o_ref, s_ref, *, bb, sg, chunk):
    r = sg * chunk

    @pl.when(pl.program_id(1) == 0)
    def _():
        s_ref[...] = s0_ref[...]

    masks = _gla_masks(sg, chunk)
    for gi in range(bb // sg):
        seqs = slice(gi * sg, (gi + 1) * sg)
        o, new = _gla_group(q_ref[seqs].reshape(r, GLA_KDIM), k_ref[seqs].reshape(r, GLA_KDIM),
                            v_ref[seqs].reshape(r, D_GLA).astype(BF16), lg_ref[seqs].reshape(r, GLA_KDIM),
                            [s_ref[gi * sg + i] for i in range(sg)], masks, chunk)
        o_ref[seqs] = o.reshape(sg, chunk, D_GLA)
        for i in range(sg):
            s_ref[gi * sg + i] = new[i]


def _gla(q, k, v, lg, s0, bb, sg, chunk):
    b, l, _ = q.shape
    blk = lambda w: pl.BlockSpec((bb, chunk, w), lambda i, c: (i, c, 0))
    sspec = pl.BlockSpec((bb, GLA_KDIM, GLA_DV), lambda i, c: (i, 0, 0))
    return pl.pallas_call(
        functools.partial(_gla_kernel, bb=bb, sg=sg, chunk=chunk),
        grid=(b // bb, l // chunk),
        in_specs=[blk(GLA_KDIM), blk(GLA_KDIM), blk(D_GLA), blk(GLA_KDIM), sspec],
        out_specs=[blk(D_GLA), sspec],
        out_shape=[jax.ShapeDtypeStruct((b, l, D_GLA), F32),
                   jax.ShapeDtypeStruct((b, GLA_KDIM, GLA_DV), F32)],
        compiler_params=pltpu.CompilerParams(dimension_semantics=("parallel", "arbitrary"),
                                             vmem_limit_bytes=VMEM_LIMIT),
        name=f"gla_c{chunk}",
    )(q, k, v, lg, s0)


def _s5_kernel(ug_ref, mi_ref, minr_ref, mini_ref, mor_ref, moi_ref, ar_ref, ai_ref,
               h0r_ref, h0i_ref, yg_ref, hfr_ref, hfi_ref, vr_s, vi_s, hr_s, hi_s, str_s, sti_s, *, nb, cb):
    gs = SLAB_GROUPS // 2
    n_pairs = S5_GROUPS // 2
    w = ug_ref.shape[-1]
    rows = nb * cb
    batch = cb if nb == 1 else nb
    n_blocks = rows // batch
    interleave = nb > 1 and cb > 1
    first_pair = pl.program_id(0) * gs
    seq0 = pl.program_id(1) * batch if nb == 1 else 0

    def state_rows(p):
        return pl.ds(seq0 * n_pairs + first_pair + p, batch, stride=n_pairs)

    def load_state():
        for p in range(gs):
            str_s[p] = h0r_ref[state_rows(p), :]
            sti_s[p] = h0i_ref[state_rows(p), :]

    if nb == 1:
        load_state()
    else:
        pl.when(pl.program_id(1) == 0)(load_state)

    def to_scan_order(ref, g, val):
        if not interleave:
            ref[g] = val
        else:
            for b in range(nb):
                ref[g, pl.ds(b, cb, stride=nb), :] = val[b * cb:(b + 1) * cb]

    def from_scan_order(ref, g):
        if not interleave:
            return ref[g]
        return jnp.concatenate([ref[g, pl.ds(b, cb, stride=nb), :] for b in range(nb)], axis=0)

    ub = [ug_ref[g].reshape(rows, w) for g in range(2 * gs)]
    for g in range(gs):
        pair_u = jnp.concatenate([ub[2 * g], ub[2 * g + 1]], axis=1)
        to_scan_order(vr_s, g, _dot(pair_u, minr_ref[g]))
        to_scan_order(vi_s, g, _dot(pair_u, mini_ref[g]))
    ar = [ar_ref[g] for g in range(gs)]
    ai = [ai_ref[g] for g in range(gs)]

    def body(c, carry):
        rows = pl.ds(pl.multiple_of(c * batch, batch), batch)
        new = []
        for g in range(gs):
            hr, hi = carry[2 * g], carry[2 * g + 1]
            hr_s[g, rows, :] = hr
            hi_s[g, rows, :] = hi
            new.append(ar[g] * hr - ai[g] * hi + vr_s[g, rows, :])
            new.append(ar[g] * hi + ai[g] * hr + vi_s[g, rows, :])
        return tuple(new)

    init = tuple(ref[g] for g in range(gs) for ref in (str_s, sti_s))
    fin = lax.fori_loop(0, n_blocks, body, init)
    for g in range(gs):
        str_s[g] = fin[2 * g]
        sti_s[g] = fin[2 * g + 1]
        hfr_ref[state_rows(g), :] = fin[2 * g]
        hfi_ref[state_rows(g), :] = fin[2 * g + 1]
        y_state = (_dot(from_scan_order(hr_s, g).astype(BF16), mor_ref[g])
                   + _dot(from_scan_order(hi_s, g).astype(BF16), moi_ref[g]))
        for k in range(2):
            y = _dot(ub[2 * g + k], mi_ref[2 * g + k]) + y_state[:, k * w:(k + 1) * w]
            yg_ref[2 * g + k] = y.reshape(yg_ref.shape[1:])


def _s5(ug, m, h0_re, h0_im, nb, cb):
    g_all, r, w = ug.shape
    nc = r // nb
    batch = cb if nb == 1 else nb
    n, gs = LANES, SLAB_GROUPS
    uspec = pl.BlockSpec((gs, nb, cb, w), lambda s, i: (s, 0, i, 0))
    per_g = lambda a, b_: pl.BlockSpec((gs, a, b_), lambda s, i: (s, 0, 0))
    per_p = lambda a, b_: pl.BlockSpec((gs // 2, a, b_), lambda s, i: (s, 0, 0))
    hspec = _const_spec(h0_re.shape)
    yg, hf_re, hf_im = pl.pallas_call(
        functools.partial(_s5_kernel, nb=nb, cb=cb),
        grid=(g_all // gs, nc // cb),
        in_specs=[uspec, per_g(w, w), per_p(2 * w, n), per_p(2 * w, n), per_p(n, 2 * w), per_p(n, 2 * w),
                  per_p(1, n), per_p(1, n), hspec, hspec],
        out_specs=[uspec, hspec, hspec],
        out_shape=[jax.ShapeDtypeStruct((g_all, nb, nc, w), F32),
                   jax.ShapeDtypeStruct(h0_re.shape, F32),
                   jax.ShapeDtypeStruct(h0_im.shape, F32)],
        scratch_shapes=[pltpu.VMEM((gs // 2, nb * cb, n), F32)] * 4 + [pltpu.VMEM((gs // 2, batch, n), F32)] * 2,
        compiler_params=pltpu.CompilerParams(dimension_semantics=("arbitrary", "arbitrary"),
                                             vmem_limit_bytes=VMEM_LIMIT),
        name=f"s5_w{w}",
    )(ug.reshape(g_all, nb, nc, w), m["m_intra"], m["m_in_re"], m["m_in_im"], m["m_out_re"], m["m_out_im"],
      m["a_re"], m["a_im"], h0_re, h0_im)
    return yg.reshape(g_all, r, w), hf_re, hf_im


def _mix_ffn_kernel(x_ref, o_ref, g_ref, yg_ref, u_ref, dsk_ref, gn_ref, wglu_ref, s5n_ref, wo_ref, gpost_ref,
                    gpre_ref, wg_ref, wu_ref, wd_ref, gpostf_ref, out_ref, y_scr, x1_scr, *, steps):
    i = pl.program_id(0)

    def mix_into(slot):
        nr = x_ref.shape[0] // steps
        rt = min(nr, 8)
        for s in range(D_S5 // LANES):
            for hf in range(steps // 8):
                for r0 in range(0, nr, rt):
                    a = [yg_ref[s * SLAB_GROUPS + g, r0:r0 + rt, hf * LANES:(hf + 1) * LANES]
                         for g in range(SLAB_GROUPS)]
                    per_step = _lane_block_transpose(a)
                    for t in range(8):
                        y_scr[s, pl.ds(r0 * steps + hf * 8 + t, rt, stride=steps), :] = per_step[t]
        y5 = jnp.concatenate([y_scr[s] for s in range(D_S5 // LANES)], axis=1) + dsk_ref[...] * u_ref[...]
        o = o_ref[...]
        gn = gn_ref[...]
        heads = []
        for h in range(GLA_HEADS):
            heads.append(_rms(o[:, h * GLA_DV:(h + 1) * GLA_DV], gn))
        og = jnp.concatenate(heads, axis=1) * jax.nn.silu(g_ref[...])
        y = jax.nn.gelu(y5)
        y = y * jax.nn.sigmoid(_dot(y.astype(BF16), wglu_ref[...]))
        y = _rms(y, s5n_ref[...])
        mix = _dot(og.astype(BF16), wo_ref[:D_GLA, :]) + _dot(y.astype(BF16), wo_ref[D_GLA:, :])
        x1_scr[slot] = x_ref[...] + _rms(mix, gpost_ref[...])

    def ffn_from(slot):
        x = x1_scr[slot]
        h = _rms(x, gpre_ref[...]).astype(BF16)
        acc = jnp.zeros(x.shape, F32)
        for c in range(D_FF // FF_CHUNK):
            cols = slice(c * FF_CHUNK, (c + 1) * FF_CHUNK)
            act = jax.nn.silu(_dot(h, wg_ref[:, cols])) * _dot(h, wu_ref[:, cols])
            acc = acc + _dot(act.astype(BF16), wd_ref[cols, :])
        out_ref[...] = x + _rms(acc, gpostf_ref[...])

    @pl.when(i == 0)
    def _():
        mix_into(0)

    @pl.when(i > 0)
    def _():
        ffn_from((i - 1) & 1)
        mix_into(i & 1)


def _mix_ffn(x, o, g, yg, u, dsk, gn, wglu, s5n, wo, gpost, gpre, wg, wu, wd, gpostf, tm, steps):
    t = x.shape[0]
    n = t // tm
    cur = lambda i: jnp.minimum(i, n - 1)
    row = lambda w: pl.BlockSpec((tm, w), lambda i: (cur(i), 0))
    fixed = lambda shape: pl.BlockSpec(shape, lambda i: (0,) * len(shape), pipeline_mode=pl.Buffered(1))
    return pl.pallas_call(
        functools.partial(_mix_ffn_kernel, steps=steps),
        grid=(n + 1,),
        in_specs=[row(D_MODEL), row(D_GLA), row(D_GLA),
                  pl.BlockSpec((S5_GROUPS, tm // steps, steps * S5_GROUP), lambda i: (0, cur(i), 0)),
                  row(D_S5), fixed((1, D_S5)),
                  fixed((1, GLA_DV)), fixed((D_S5, D_S5)), fixed((1, D_S5)),
                  fixed((D_GLA + D_S5, D_MODEL)), fixed((1, D_MODEL)),
                  fixed((1, D_MODEL)), fixed((D_MODEL, D_FF)), fixed((D_MODEL, D_FF)),
                  fixed((D_FF, D_MODEL)), fixed((1, D_MODEL))],
        out_specs=pl.BlockSpec((tm, D_MODEL), lambda i: (jnp.maximum(i - 1, 0), 0)),
        out_shape=jax.ShapeDtypeStruct((t, D_MODEL), F32),
        scratch_shapes=[pltpu.VMEM((D_S5 // LANES, tm, LANES), F32), pltpu.VMEM((2, tm, D_MODEL), F32)],
        compiler_params=pltpu.CompilerParams(dimension_semantics=("arbitrary",),
                                             vmem_limit_bytes=VMEM_LIMIT),
        name="mix_ffn",
    )(x, o, g, yg, u, dsk, gn, wglu, s5n, wo, gpost, gpre, wg, wu, wd, gpostf)


def kernel(x_prompt, x_sample, state_gla, state_s5_re, state_s5_im, meta_tokens, g_pre_mix, w_in, w_gk2, b_gk, gla_norm, s5_a_re, s5_a_im, s5_b_re, s5_b_im, s5_c_re, s5_c_im, s5_d, s5_log_dt, w_s5_glu, s5_norm, w_o, g_post_mix, g_pre_ffn, w_gate, w_up, w_down, g_post_ffn):
    assert g_pre_mix.shape[0] == 1, "single-layer step"
    bp, seq_p, _ = x_prompt.shape
    bs, seq_s, _ = x_sample.shape
    row = lambda t: t[0].reshape(1, -1)

    w = w_in[0]
    c3, c4 = 1536, 1536 + GATE_RANK
    w_p = jnp.concatenate([w[:, :c3], w[:, c4:], w[:, c3:c4],
                           jnp.zeros((D_MODEL, LANES - GATE_RANK), F32)], axis=1).astype(BF16)
    wgk_p = jnp.concatenate([w_gk2[0], jnp.zeros((LANES - GATE_RANK, GLA_KDIM), F32)], axis=0).astype(BF16)
    wo_bf = w_o[0].astype(BF16)
    s5m = _s5_prep(s5_a_re[0], s5_a_im[0], s5_b_re[0], s5_b_im[0], s5_c_re[0], s5_c_im[0], s5_log_dt[0])
    proj_w = (row(g_pre_mix), w_p, wgk_p, row(b_gk))

    def finish(x, o, g, yg, u, tm, steps):
        return _mix_ffn(x, o, g, yg, u, row(s5_d), row(gla_norm), w_s5_glu[0].astype(BF16), row(s5_norm),
                        wo_bf, row(g_post_mix), row(g_pre_ffn), w_gate[0].astype(BF16),
                        w_up[0].astype(BF16), w_down[0].astype(BF16), row(g_post_ffn), tm, steps)

    xm = jnp.broadcast_to(meta_tokens[None], (bp, N_META, D_MODEL)).reshape(bp * N_META, D_MODEL)
    q, k, v, _, _, lg, ug = _in_proj(xm, *proj_w, bp * N_META, S5_BLOCK)
    r3 = lambda t, b, l: t.reshape(b, l, t.shape[-1])
    _, s_meta = _gla(r3(q, bp, N_META), r3(k, bp, N_META), r3(v, bp, N_META), r3(lg, bp, N_META),
                     jnp.zeros((bp, GLA_KDIM, GLA_DV), F32), bp, bp, N_META)
    zh = jnp.zeros((bp * S5_GROUPS // 2, LANES), F32)
    _, hm_re, hm_im = _s5(ug, s5m[S5_BLOCK], zh, zh, 1, bp)

    xp = x_prompt.reshape(bp * seq_p, D_MODEL)
    g, u, o, ug, s_p = _in_proj(xp, *proj_w, 512, S5_BLOCK, gla_s0=s_meta, seq_len=seq_p)
    yg, hp_re, hp_im = _s5(ug, s5m[S5_BLOCK], hm_re, hm_im, bp, 32)
    y_prompt = finish(xp, o, g, yg, u, 512, S5_BLOCK)

    xs = x_sample.reshape(bs * seq_s, D_MODEL)
    to_g = lambda t: t[0].reshape(bs * S5_GROUPS // 2, LANES)
    q, k, v, g, u, lg, ug = _in_proj(xs, *proj_w, 512, seq_s)
    o, s_s = _gla(r3(q, bs, seq_s), r3(k, bs, seq_s), r3(v, bs, seq_s), r3(lg, bs, seq_s),
                  state_gla[0].reshape(bs, GLA_KDIM, GLA_DV), 32, 16, seq_s)
    yg, hs_re, hs_im = _s5(ug, s5m[seq_s], to_g(state_s5_re), to_g(state_s5_im), 1, bs)
    y_sample = finish(xs, o.reshape(bs * seq_s, D_GLA), g, yg, u, 512, seq_s)

    gla_out = lambda s, b: s.reshape(1, b, GLA_HEADS, GLA_DK, GLA_DV)
    s5_out = lambda h: h.reshape(1, -1, S5_GROUPS, S5_STATE)
    return (y_prompt.reshape(bp, seq_p, D_MODEL), y_sample.reshape(bs, seq_s, D_MODEL),
            gla_out(s_p, bp), s5_out(hp_re), s5_out(hp_im),
            gla_out(s_s, bs), s5_out(hs_re), s5_out(hs_im))
```

```python
import functools

import jax
import jax.numpy as jnp
from jax import lax
from jax.experimental import pallas as pl
from jax.experimental.pallas import tpu as pltpu

F32 = jnp.float32
BF16 = jnp.bfloat16

D_MODEL = 1024
D_GLA = 512
GLA_HEADS = 4
GLA_DV = 128
GLA_DK = 64
GLA_KDIM = 256
GATE_RANK = 16
GATE_NORM = 16.0
GLA_CHUNK = 64
D_S5 = 512
S5_GROUP = 16
S5_GROUPS = 32
S5_STATE = 64
N_META = 16
D_FF = 2816
EPS = 1e-6
LANES = 128
S5_BLOCK = 16
SLAB_GROUPS = LANES // S5_GROUP
PROJ_W = 2176
FF_CHUNK = 256
VMEM_LIMIT = 48 * 1024 * 1024


def _rms(x, g):
    return x * lax.rsqrt(jnp.mean(x * x, axis=-1, keepdims=True) + EPS) * g


def _dot(a, b):
    return jnp.dot(a, b, preferred_element_type=F32)


def _dot_nt(a, b):
    return lax.dot_general(a, b, (((1,), (1,)), ((), ())), preferred_element_type=F32)


def _dot_tn(a, b):
    return lax.dot_general(a, b, (((0,), (0,)), ((), ())), preferred_element_type=F32)


def _const_spec(shape):
    zeros = (0,) * len(shape)
    return pl.BlockSpec(shape, lambda *_: zeros)


PREP_GROUPS = 8


def _cmul(a, b):
    return a[0] * b[0] - a[1] * b[1], a[0] * b[1] + a[1] * b[0]


def _unit_powers(c1, s1, expo, n_bits):
    acc = (jnp.ones_like(c1), jnp.zeros_like(c1))
    base = (c1, s1)
    squares = [base]
    for bit in range(n_bits):
        take = ((expo >> bit) & 1) == 1
        nxt = _cmul(acc, base)
        acc = (jnp.where(take, nxt[0], acc[0]), jnp.where(take, nxt[1], acc[1]))
        base = _cmul(base, base)
        squares.append(base)
    return acc, squares


def _s5_prep_kernel(arc_ref, aic_ref, air_ref, ldt_ref, br_ref, bi_ref, cr_ref, ci_ref,
                    mi16_ref, mi8_ref, inr16_ref, ini16_ref, inr8_ref, ini8_ref,
                    outr16_ref, outi16_ref, outr8_ref, outi8_ref, a16r_ref, a16i_ref, a8r_ref, a8i_ref):
    n, q = S5_STATE, S5_BLOCK
    w = q * S5_GROUP
    hp = lax.Precision.HIGHEST
    lane = lax.broadcasted_iota(jnp.int32, (n, w), 1)
    t_blk = lane >> 4
    eye = lax.broadcasted_iota(jnp.int32, (n, n), 0) == lax.broadcasted_iota(jnp.int32, (n, n), 1)
    to_col = lambda r: jnp.sum(jnp.where(eye, r, 0.0), axis=1, keepdims=True)
    zrows = lambda x: jnp.concatenate([x, jnp.zeros_like(x)], axis=0)
    lane_n = lax.broadcasted_iota(jnp.int32, (n, LANES), 1)
    rep = jnp.where((lax.broadcasted_iota(jnp.int32, (S5_GROUP, w), 1) & (S5_GROUP - 1))
                    == lax.broadcasted_iota(jnp.int32, (S5_GROUP, w), 0), 1.0, 0.0)
    mo_g, rev_g, hop_g = [], [], []
    for g in range(PREP_GROUPS):
        dt = jnp.exp(ldt_ref[g])
        ang_r = air_ref[g] * dt
        c1, s1 = to_col(jnp.cos(ang_r)), to_col(jnp.sin(ang_r))
        lam_re = jnp.minimum(arc_ref[g], -1e-4)
        lam_im = aic_ref[g]
        unit, squares = _unit_powers(c1, s1, t_blk, 4)
        pm = jnp.exp(t_blk.astype(F32) * (lam_re * dt))
        pk = (pm * unit[0], pm * unit[1])
        mag = jnp.exp(lam_re * dt)
        ab = (mag * c1, mag * s1)
        p1 = _cmul(pk, ab)
        ct = tuple(lax.dot_general(r[g], rep, (((0,), (0,)), ((), ())), precision=hp,
                                   preferred_element_type=F32) for r in (cr_ref, ci_ref))
        bt = tuple(jnp.dot(r[g], rep, precision=hp, preferred_element_type=F32) for r in (br_ref, bi_ref))
        g0 = _cmul(ct, pk)
        mo = _cmul(ct, p1)
        mo_g.append((mo[0], -mo[1]))
        den = lam_re * lam_re + lam_im * lam_im
        nr, ni = ab[0] - 1.0, ab[1]
        f = ((nr * lam_re + ni * lam_im) / den, (ni * lam_re - nr * lam_im) / den)
        e = _cmul(pk, _cmul(f, bt))
        et = [zrows(x).T for x in e]
        rev_g.append([jnp.concatenate([x[(q - 1 - s) * S5_GROUP:(q - s) * S5_GROUP] for s in range(q)], axis=0)
                      for x in et])
        hops = []
        for steps in (q, q // 2):
            m = jnp.exp(float(steps) * (lam_re * dt))
            u = squares[steps.bit_length() - 1]
            hops += [m * u[0], m * u[1]]
        cols = jnp.zeros((n, LANES), F32)
        for idx, hcol in enumerate(hops):
            cols = jnp.where(lane_n == idx, hcol, cols)
        hop_g.append(zrows(cols).T)
        t0 = (jnp.dot(et[0][:S5_GROUP, :n], g0[0], precision=hp, preferred_element_type=F32)
              - jnp.dot(et[1][:S5_GROUP, :n], g0[1], precision=hp, preferred_element_type=F32))
        lane_t = lax.broadcasted_iota(jnp.int32, t0.shape, 1) >> 4
        for s in range(q):
            blk = t0 if s == 0 else jnp.where(lane_t >= s, pltpu.roll(t0, S5_GROUP * s, 1), 0.0)
            mi16_ref[g, s * S5_GROUP:(s + 1) * S5_GROUP, :] = blk.astype(BF16)
            if s < q // 2:
                mi8_ref[g, s * S5_GROUP:(s + 1) * S5_GROUP, :] = blk[:, :w // 2].astype(BF16)

    half = lambda x: pltpu.roll(x, n, 1)
    for p in range(PREP_GROUPS // 2):
        g0, g1 = 2 * p, 2 * p + 1
        for c, (in16, in8, out16, out8) in enumerate(((inr16_ref, inr8_ref, outr16_ref, outr8_ref),
                                                      (ini16_ref, ini8_ref, outi16_ref, outi8_ref))):
            lo, hi = rev_g[g0][c], half(rev_g[g1][c])
            in16[p] = jnp.concatenate([lo, hi], axis=0).astype(BF16)
            in8[p] = jnp.concatenate([lo[w // 2:], hi[w // 2:]], axis=0).astype(BF16)
            m0, m1 = mo_g[g0][c], mo_g[g1][c]
            z, zh = jnp.zeros_like(m0), jnp.zeros((n, w // 2), F32)
            out16[p] = jnp.concatenate([jnp.concatenate([m0, z], axis=1),
                                        jnp.concatenate([z, m1], axis=1)], axis=0).astype(BF16)
            out8[p] = jnp.concatenate([jnp.concatenate([m0[:, :w // 2], zh], axis=1),
                                       jnp.concatenate([zh, m1[:, :w // 2]], axis=1)], axis=0).astype(BF16)
        hops = hop_g[g0] + half(hop_g[g1])
        a16r_ref[p] = hops[0:1]
        a16i_ref[p] = hops[1:2]
        a8r_ref[p] = hops[2:3]
        a8i_ref[p] = hops[3:4]


def _s5_prep(a_re, a_im, b_re, b_im, c_re, c_im, log_dt):
    g, n, j, q = S5_GROUPS, S5_STATE, S5_GROUP, S5_BLOCK
    w, h, pg = q * j, q * j // 2, PREP_GROUPS
    col = lambda t: t.reshape(g, n, 1)
    rowv = lambda t: t.reshape(g, 1, n)
    spec = lambda a, b_: pl.BlockSpec((pg, a, b_), lambda i: (i, 0, 0))
    pspec = lambda a, b_: pl.BlockSpec((pg // 2, a, b_), lambda i: (i, 0, 0))
    pair_shapes = [(2 * w, LANES), (2 * w, LANES), (2 * h, LANES), (2 * h, LANES),
                   (LANES, 2 * w), (LANES, 2 * w), (LANES, 2 * h), (LANES, 2 * h)]
    outs = pl.pallas_call(
        _s5_prep_kernel,
        grid=(g // pg,),
        in_specs=[spec(n, 1), spec(n, 1), spec(1, n), spec(1, 1),
                  spec(n, j), spec(n, j), spec(j, n), spec(j, n)],
        out_specs=[spec(w, w), spec(h, h)] + [pspec(*s) for s in pair_shapes] + [pspec(1, LANES)] * 4,
        out_shape=[jax.ShapeDtypeStruct((g, w, w), BF16), jax.ShapeDtypeStruct((g, h, h), BF16)]
                  + [jax.ShapeDtypeStruct((g // 2,) + s, BF16) for s in pair_shapes]
                  + [jax.ShapeDtypeStruct((g // 2, 1, LANES), F32)] * 4,
        compiler_params=pltpu.CompilerParams(dimension_semantics=("parallel",)),
        name="s5_prep",
    )(col(a_re), col(a_im), rowv(a_im), log_dt.reshape(g, 1, 1),
      b_re, b_im, c_re, c_im)
    mi16, mi8, inr16, ini16, inr8, ini8, outr16, outi16, outr8, outi8, a16r, a16i, a8r, a8i = outs
    return {q: dict(m_intra=mi16, m_in_re=inr16, m_in_im=ini16, m_out_re=outr16, m_out_im=outi16,
                    a_re=a16r, a_im=a16i),
            q // 2: dict(m_intra=mi8, m_in_re=inr8, m_in_im=ini8, m_out_re=outr8, m_out_im=outi8,
                         a_re=a8r, a_im=a8i)}


def _lane_block_transpose(a):
    a = list(a)
    blk = lax.broadcasted_iota(jnp.int32, a[0].shape, 1) >> 4
    for d in (4, 2, 1):
        upper = (blk & d) != 0
        for r in range(8):
            if r & d:
                continue
            lo, hi = a[r], a[r + d]
            a[r] = jnp.where(upper, pltpu.roll(hi, 16 * d, 1), lo)
            a[r + d] = jnp.where(upper, hi, pltpu.roll(lo, LANES - 16 * d, 1))
    return a


def _in_proj_kernel(*refs, steps, n_tiles, tiles_per_seq):
    fused = tiles_per_seq is not None
    x_ref, gpre_ref, win_ref, wgk2_ref, bgk_ref = refs[:5]
    if fused:
        (s0_ref, g_ref, u_ref, o_ref, ug_ref, sfin_ref,
         w_ref, wgk_ref, u_scr, gk_scr, q_scr, k_scr, v_scr, s_scr) = refs[5:]
    else:
        q_ref, k_ref, v_ref, g_ref, u_ref, lg_ref, ug_ref, w_ref, wgk_ref, u_scr, gk_scr = refs[5:]
    i = pl.program_id(0)
    tm = x_ref.shape[0]

    def stage_weights():
        c_gk = 2 * GLA_KDIM + 2 * D_GLA
        w_ref[:, :c_gk] = win_ref[:, :c_gk].astype(BF16)
        w_ref[:, c_gk:c_gk + D_S5] = win_ref[:, c_gk + GATE_RANK:].astype(BF16)
        slab = win_ref[:, c_gk:c_gk + LANES]
        keep = lax.broadcasted_iota(jnp.int32, slab.shape, 1) < GATE_RANK
        w_ref[:, c_gk + D_S5:] = jnp.where(keep, slab, 0.0).astype(BF16)
        wgk_ref[...] = jnp.concatenate(
            [wgk2_ref[...], jnp.zeros((LANES - GATE_RANK, GLA_KDIM), F32)], axis=0).astype(BF16)

    def project(slot):
        h = _rms(x_ref[...], gpre_ref[...])
        proj = _dot(h.astype(BF16), w_ref[...])
        if fused:
            q_scr[slot] = proj[:, 0:256]
            k_scr[slot] = proj[:, 256:512]
            v_scr[slot] = proj[:, 512:1024].astype(BF16)
        else:
            q_ref[...] = proj[:, 0:256]
            k_ref[...] = proj[:, 256:512]
            v_ref[...] = proj[:, 512:1024]
        g_ref[...] = proj[:, 1024:1536]
        u = proj[:, 1536:2048]
        u_ref[...] = u
        for s in range(D_S5 // LANES):
            u_scr[slot, s] = u[:, s * LANES:(s + 1) * LANES]
        gk_scr[slot] = proj[:, 2048:PROJ_W]

    def tail(slot):
        z = _dot(gk_scr[slot].astype(BF16), wgk_ref[...]) + bgk_ref[...]
        lg = jax.nn.log_sigmoid(z) * (1.0 / GATE_NORM)
        if fused:
            tile = i - 1
            seq = tile // tiles_per_seq
            state = jnp.where(tile % tiles_per_seq == 0, s0_ref[seq], s_scr[...])
            masks = _gla_masks(1, GLA_CHUNK)
            for c in range(tm // GLA_CHUNK):
                rows = slice(c * GLA_CHUNK, (c + 1) * GLA_CHUNK)
                o, (state,) = _gla_group(q_scr[slot, rows, :], k_scr[slot, rows, :], v_scr[slot, rows, :],
                                         lg[rows], [state], masks, GLA_CHUNK)
                o_ref[rows, :] = o
            s_scr[...] = state
            sfin_ref[seq] = state
        else:
            lg_ref[...] = lg
        nr = tm // steps
        rt = min(nr, 16)
        for s in range(D_S5 // LANES):
            for hf in range(steps // 8):
                for r0 in range(0, nr, rt):
                    a = [u_scr[slot, s, pl.ds(r0 * steps + hf * 8 + t, rt, stride=steps), :] for t in range(8)]
                    per_group = _lane_block_transpose(a)
                    for g in range(SLAB_GROUPS):
                        ug_ref[s * SLAB_GROUPS + g, r0:r0 + rt, hf * LANES:(hf + 1) * LANES] = (
                            per_group[g].astype(BF16))

    @pl.when(i == 0)
    def _():
        stage_weights()
        if fused:
            s_scr[...] = jnp.zeros(s_scr.shape, F32)
        project(0)

    @pl.when((i > 0) & (i < n_tiles))
    def _():
        tail((i - 1) & 1)
        project(i & 1)

    @pl.when(i == n_tiles)
    def _():
        tail((i - 1) & 1)


def _in_proj(x, gpre, w_in, w_gk2, bgk, tm, steps, gla_s0=None, seq_len=None):
    t = x.shape[0]
    n = t // tm
    cur = lambda i: jnp.minimum(i, n - 1)
    prev = lambda i: jnp.maximum(i - 1, 0)
    row = lambda w, at: pl.BlockSpec((tm, w), lambda i: (at(i), 0))
    wg = steps * S5_GROUP
    ug_spec = pl.BlockSpec((S5_GROUPS, tm // steps, wg), lambda i: (0, prev(i), 0))
    ug_shape = jax.ShapeDtypeStruct((S5_GROUPS, t // steps, wg), BF16)
    f32 = lambda w: jax.ShapeDtypeStruct((t, w), F32)
    once = lambda shape: pl.BlockSpec(shape, lambda i: (0,) * len(shape), pipeline_mode=pl.Buffered(1))
    in_specs = [row(D_MODEL, cur), _const_spec((1, D_MODEL)), once(w_in.shape), once(w_gk2.shape),
                _const_spec((1, GLA_KDIM))]
    scratch = [pltpu.VMEM((D_MODEL, PROJ_W), BF16), pltpu.VMEM((LANES, GLA_KDIM), BF16),
               pltpu.VMEM((2, D_S5 // LANES, tm, LANES), F32), pltpu.VMEM((2, tm, LANES), F32)]
    args = (x, gpre, w_in, w_gk2, bgk)
    if gla_s0 is None:
        tiles_per_seq = None
        out_specs = [row(w, cur) for w in (256, 256, 512, 512, 512)] + [row(GLA_KDIM, prev), ug_spec]
        out_shape = [f32(w) for w in (256, 256, 512, 512, 512)] + [f32(GLA_KDIM), ug_shape]
    else:
        tiles_per_seq = seq_len // tm
        assert seq_len % tm == 0 and tm % GLA_CHUNK == 0
        in_specs.append(_const_spec(gla_s0.shape))
        args += (gla_s0,)
        out_specs = [row(D_GLA, cur), row(D_S5, cur), row(D_GLA, prev), ug_spec, _const_spec(gla_s0.shape)]
        out_shape = [f32(D_GLA), f32(D_S5), f32(D_GLA), ug_shape, jax.ShapeDtypeStruct(gla_s0.shape, F32)]
        scratch += [pltpu.VMEM((2, tm, GLA_KDIM), F32), pltpu.VMEM((2, tm, GLA_KDIM), F32),
                    pltpu.VMEM((2, tm, D_GLA), BF16), pltpu.VMEM(gla_s0.shape[1:], F32)]
    return pl.pallas_call(
        functools.partial(_in_proj_kernel, steps=steps, n_tiles=n, tiles_per_seq=tiles_per_seq),
        grid=(n + 1,),
        in_specs=in_specs,
        out_specs=out_specs,
        out_shape=out_shape,
        scratch_shapes=scratch,
        compiler_params=pltpu.CompilerParams(dimension_semantics=("arbitrary",),
                                             vmem_limit_bytes=VMEM_LIMIT),
        name="in_proj" if gla_s0 is None else "in_proj_gla",
    )(*args)


def _gla_masks(sg, chunk):
    r = sg * chunk
    assert r >= LANES or sg == 1
    shift = chunk.bit_length() - 1
    hk = GLA_HEADS * r
    ri = lax.broadcasted_iota(jnp.int32, (r, r), 0)
    ci = lax.broadcasted_iota(jnp.int32, (r, r), 1)
    ri4 = lax.broadcasted_iota(jnp.int32, (r, hk), 0)
    ci4 = lax.broadcasted_iota(jnp.int32, (r, hk), 1) & (r - 1)
    stack_head = lax.broadcasted_iota(jnp.int32, (hk, 1), 0) >> (r.bit_length() - 1)
    k_head = lax.broadcasted_iota(jnp.int32, (1, GLA_KDIM), 1) >> 6
    v_head = lax.broadcasted_iota(jnp.int32, (1, D_GLA), 1) >> 7
    s_head = lax.broadcasted_iota(jnp.int32, (GLA_KDIM, 1), 0) >> 6
    return dict(
        tri_bf=jnp.where(((ri >> shift) == (ci >> shift)) & (ri >= ci), 1.0, 0.0).astype(BF16),
        causal4=((ri4 >> shift) == (ci4 >> shift)) & (ri4 >= ci4),
        k_diag=stack_head == k_head, v_diag=stack_head == v_head, s_diag=s_head == v_head)


def _gla_group(q, k, v_bf, lg, states, m, chunk):
    sg = len(states)
    r = sg * chunk
    tile4 = lambda t: jnp.concatenate([t] * GLA_HEADS, axis=0)
    lg_hi = lg.astype(BF16)
    lg_lo = (lg - lg_hi.astype(F32)).astype(BF16)
    b2 = _dot(m["tri_bf"], jnp.concatenate([lg_hi, lg_lo], axis=1))
    b = b2[:, :GLA_KDIM] + b2[:, GLA_KDIM:]
    lasts = [b[(i + 1) * chunk - 1:(i + 1) * chunk, :] for i in range(sg)]
    bl = jnp.concatenate([jnp.broadcast_to(t, (chunk, GLA_KDIM)) for t in lasts], axis=0)
    qd = (q * (GLA_DK ** -0.5) * jnp.exp(b)).astype(BF16)
    ki = (k * jnp.exp(-b)).astype(BF16)
    ke = k * jnp.exp(bl - b)

    ki_bd = jnp.where(m["k_diag"], tile4(ki), 0.0)
    att = jnp.where(m["causal4"], _dot_nt(qd, ki_bd), 0.0).astype(BF16)
    v_bd = jnp.where(m["v_diag"], tile4(v_bf), 0.0)
    o_intra = _dot(att, v_bd)

    if r < LANES:
        aug_t = jnp.concatenate([ke, jnp.broadcast_to(lasts[0], (LANES - r, GLA_KDIM))], axis=0).T
        ke_t, bl_t = aug_t, aug_t[:, r:]
    else:
        ke_t = ke.T
        bl_t = jnp.concatenate(lasts + [jnp.zeros((r - sg, GLA_KDIM), F32)], axis=0).T
    ke_t = ke_t.astype(BF16)

    outs, new_states = [], []
    for i, s_old in enumerate(states):
        rows = slice(i * chunk, (i + 1) * chunk)
        s_bd = jnp.where(m["s_diag"], jnp.concatenate([s_old.astype(BF16)] * GLA_HEADS, axis=1), 0.0)
        outs.append(o_intra[rows] + _dot(qd[rows], s_bd))
        upd = [_dot(ke_t[h * GLA_DK:(h + 1) * GLA_DK, rows], v_bf[rows, h * GLA_DV:(h + 1) * GLA_DV])
               for h in range(GLA_HEADS)]
        new_states.append(jnp.exp(bl_t[:, i:i + 1]) * s_old + jnp.concatenate(upd, axis=0))
    return (outs[0] if sg == 1 else jnp.concatenate(outs, axis=0)), new_states


def _gla_kernel(q_ref, k_ref, v_ref, lg_ref, s0_ref, o_ref, s_ref, *, bb, sg, chunk):
    r = sg * chunk

    @pl.when(pl.program_id(1) == 0)
    def _():
        s_ref[...] = s0_ref[...]

    masks = _gla_masks(sg, chunk)
    for gi in range(bb // sg):
        seqs = slice(gi * sg, (gi + 1) * sg)
        o, new = _gla_group(q_ref[seqs].reshape(r, GLA_KDIM), k_ref[seqs].reshape(r, GLA_KDIM),
                            v_ref[seqs].reshape(r, D_GLA).astype(BF16), lg_ref[seqs].reshape(r, GLA_KDIM),
                            [s_ref[gi * sg + i] for i in range(sg)], masks, chunk)
        o_ref[seqs] = o.reshape(sg, chunk, D_GLA)
        for i in range(sg):
            s_ref[gi * sg + i] = new[i]


def _gla(q, k, v, lg, s0, bb, sg, chunk):
    b, l, _ = q.shape
    blk = lambda w: pl.BlockSpec((bb, chunk, w), lambda i, c: (i, c, 0))
    sspec = pl.BlockSpec((bb, GLA_KDIM, GLA_DV), lambda i, c: (i, 0, 0))
    return pl.pallas_call(
        functools.partial(_gla_kernel, bb=bb, sg=sg, chunk=chunk),
        grid=(b // bb, l // chunk),
        in_specs=[blk(GLA_KDIM), blk(GLA_KDIM), blk(D_GLA), blk(GLA_KDIM), sspec],
        out_specs=[blk(D_GLA), sspec],
        out_shape=[jax.ShapeDtypeStruct((b, l, D_GLA), F32),
                   jax.ShapeDtypeStruct((b, GLA_KDIM, GLA_DV), F32)],
        compiler_params=pltpu.CompilerParams(dimension_semantics=("parallel", "arbitrary"),
                                             vmem_limit_bytes=VMEM_LIMIT),
        name=f"gla_c{chunk}",
    )(q, k, v, lg, s0)


def _s5_kernel(ug_ref, mi_ref, minr_ref, mini_ref, mor_ref, moi_ref, ar_ref, ai_ref,
               h0r_ref, h0i_ref, yg_ref, hfr_ref, hfi_ref, vr_s, vi_s, hr_s, hi_s, str_s, sti_s, *, nb, cb):
    gs = SLAB_GROUPS // 2
    n_pairs = S5_GROUPS // 2
    w = ug_ref.shape[-1]
    rows = nb * cb
    batch = cb if nb == 1 else nb
    n_blocks = rows // batch
    interleave = nb > 1 and cb > 1
    first_pair = pl.program_id(0) * gs
    seq0 = pl.program_id(1) * batch if nb == 1 else 0

    def state_rows(p):
        return pl.ds(seq0 * n_pairs + first_pair + p, batch, stride=n_pairs)

    def load_state():
        for p in range(gs):
            str_s[p] = h0r_ref[state_rows(p), :]
            sti_s[p] = h0i_ref[state_rows(p), :]

    if nb == 1:
        load_state()
    else:
        pl.when(pl.program_id(1) == 0)(load_state)

    def to_scan_order(ref, g, val):
        if not interleave:
            ref[g] = val
        else:
            for b in range(nb):
                ref[g, pl.ds(b, cb, stride=nb), :] = val[b * cb:(b + 1) * cb]

    def from_scan_order(ref, g):
        if not interleave:
            return ref[g]
        return jnp.concatenate([ref[g, pl.ds(b, cb, stride=nb), :] for b in range(nb)], axis=0)

    ub = [ug_ref[g].reshape(rows, w) for g in range(2 * gs)]
    for g in range(gs):
        pair_u = jnp.concatenate([ub[2 * g], ub[2 * g + 1]], axis=1)
        to_scan_order(vr_s, g, _dot(pair_u, minr_ref[g]))
        to_scan_order(vi_s, g, _dot(pair_u, mini_ref[g]))
    ar = [ar_ref[g] for g in range(gs)]
    ai = [ai_ref[g] for g in range(gs)]

    def body(c, carry):
        rows = pl.ds(pl.multiple_of(c * batch, batch), batch)
        new = []
        for g in range(gs):
            hr, hi = carry[2 * g], carry[2 * g + 1]
            hr_s[g, rows, :] = hr
            hi_s[g, rows, :] = hi
            new.append(ar[g] * hr - ai[g] * hi + vr_s[g, rows, :])
            new.append(ar[g] * hi + ai[g] * hr + vi_s[g, rows, :])
        return tuple(new)

    init = tuple(ref[g] for g in range(gs) for ref in (str_s, sti_s))
    fin = lax.fori_loop(0, n_blocks, body, init)
    for g in range(gs):
        str_s[g] = fin[2 * g]
        sti_s[g] = fin[2 * g + 1]
        hfr_ref[state_rows(g), :] = fin[2 * g]
        hfi_ref[state_rows(g), :] = fin[2 * g + 1]
        y_state = (_dot(from_scan_order(hr_s, g).astype(BF16), mor_ref[g])
                   + _dot(from_scan_order(hi_s, g).astype(BF16), moi_ref[g]))
        for k in range(2):
            y = _dot(ub[2 * g + k], mi_ref[2 * g + k]) + y_state[:, k * w:(k + 1) * w]
            yg_ref[2 * g + k] = y.reshape(yg_ref.shape[1:])


def _s5(ug, m, h0_re, h0_im, nb, cb):
    g_all, r, w = ug.shape
    nc = r // nb
    batch = cb if nb == 1 else nb
    n, gs = LANES, SLAB_GROUPS
    uspec = pl.BlockSpec((gs, nb, cb, w), lambda s, i: (s, 0, i, 0))
    per_g = lambda a, b_: pl.BlockSpec((gs, a, b_), lambda s, i: (s, 0, 0))
    per_p = lambda a, b_: pl.BlockSpec((gs // 2, a, b_), lambda s, i: (s, 0, 0))
    hspec = _const_spec(h0_re.shape)
    yg, hf_re, hf_im = pl.pallas_call(
        functools.partial(_s5_kernel, nb=nb, cb=cb),
        grid=(g_all // gs, nc // cb),
        in_specs=[uspec, per_g(w, w), per_p(2 * w, n), per_p(2 * w, n), per_p(n, 2 * w), per_p(n, 2 * w),
                  per_p(1, n), per_p(1, n), hspec, hspec],
        out_specs=[uspec, hspec, hspec],
        out_shape=[jax.ShapeDtypeStruct((g_all, nb, nc, w), F32),
                   jax.ShapeDtypeStruct(h0_re.shape, F32),
                   jax.ShapeDtypeStruct(h0_im.shape, F32)],
        scratch_shapes=[pltpu.VMEM((gs // 2, nb * cb, n), F32)] * 4 + [pltpu.VMEM((gs // 2, batch, n), F32)] * 2,
        compiler_params=pltpu.CompilerParams(dimension_semantics=("arbitrary", "arbitrary"),
                                             vmem_limit_bytes=VMEM_LIMIT),
        name=f"s5_w{w}",
    )(ug.reshape(g_all, nb, nc, w), m["m_intra"], m["m_in_re"], m["m_in_im"], m["m_out_re"], m["m_out_im"],
      m["a_re"], m["a_im"], h0_re, h0_im)
    return yg.reshape(g_all, r, w), hf_re, hf_im


def _mix_ffn_kernel(x_ref, o_ref, g_ref, yg_ref, u_ref, dsk_ref, gn_ref, wglu_ref, s5n_ref, wo_ref, gpost_ref,
                    gpre_ref, wg_ref, wu_ref, wd_ref, gpostf_ref, out_ref, y_scr, x1_scr, *, steps):
    i = pl.program_id(0)

    def mix_into(slot):
        nr = x_ref.shape[0] // steps
        rt = min(nr, 8)
        for s in range(D_S5 // LANES):
            for hf in range(steps // 8):
                for r0 in range(0, nr, rt):
                    a = [yg_ref[s * SLAB_GROUPS + g, r0:r0 + rt, hf * LANES:(hf + 1) * LANES]
                         for g in range(SLAB_GROUPS)]
                    per_step = _lane_block_transpose(a)
                    for t in range(8):
                        y_scr[s, pl.ds(r0 * steps + hf * 8 + t, rt, stride=steps), :] = per_step[t]
        y5 = jnp.concatenate([y_scr[s] for s in range(D_S5 // LANES)], axis=1) + dsk_ref[...] * u_ref[...]
        o = o_ref[...]
        gn = gn_ref[...]
        heads = []
        for h in range(GLA_HEADS):
            heads.append(_rms(o[:, h * GLA_DV:(h + 1) * GLA_DV], gn))
        og = jnp.concatenate(heads, axis=1) * jax.nn.silu(g_ref[...])
        y = jax.nn.gelu(y5)
        y = y * jax.nn.sigmoid(_dot(y.astype(BF16), wglu_ref[...]))
        y = _rms(y, s5n_ref[...])
        mix = _dot(og.astype(BF16), wo_ref[:D_GLA, :]) + _dot(y.astype(BF16), wo_ref[D_GLA:, :])
        x1_scr[slot] = x_ref[...] + _rms(mix, gpost_ref[...])

    def ffn_from(slot):
        x = x1_scr[slot]
        h = _rms(x, gpre_ref[...]).astype(BF16)
        acc = jnp.zeros(x.shape, F32)
        for c in range(D_FF // FF_CHUNK):
            cols = slice(c * FF_CHUNK, (c + 1) * FF_CHUNK)
            act = jax.nn.silu(_dot(h, wg_ref[:, cols])) * _dot(h, wu_ref[:, cols])
            acc = acc + _dot(act.astype(BF16), wd_ref[cols, :])
        out_ref[...] = x + _rms(acc, gpostf_ref[...])

    @pl.when(i == 0)
    def _():
        mix_into(0)

    @pl.when(i > 0)
    def _():
        ffn_from((i - 1) & 1)
        mix_into(i & 1)


def _mix_ffn(x, o, g, yg, u, dsk, gn, wglu, s5n, wo, gpost, gpre, wg, wu, wd, gpostf, tm, steps):
    t = x.shape[0]
    n = t // tm
    cur = lambda i: jnp.minimum(i, n - 1)
    row = lambda w: pl.BlockSpec((tm, w), lambda i: (cur(i), 0))
    fixed = lambda shape: pl.BlockSpec(shape, lambda i: (0,) * len(shape), pipeline_mode=pl.Buffered(1))
    return pl.pallas_call(
        functools.partial(_mix_ffn_kernel, steps=steps),
        grid=(n + 1,),
        in_specs=[row(D_MODEL), row(D_GLA), row(D_GLA),
                  pl.BlockSpec((S5_GROUPS, tm // steps, steps * S5_GROUP), lambda i: (0, cur(i), 0)),
                  row(D_S5), fixed((1, D_S5)),
                  fixed((1, GLA_DV)), fixed((D_S5, D_S5)), fixed((1, D_S5)),
                  fixed((D_GLA + D_S5, D_MODEL)), fixed((1, D_MODEL)),
                  fixed((1, D_MODEL)), fixed((D_MODEL, D_FF)), fixed((D_MODEL, D_FF)),
                  fixed((D_FF, D_MODEL)), fixed((1, D_MODEL))],
        out_specs=pl.BlockSpec((tm, D_MODEL), lambda i: (jnp.maximum(i - 1, 0), 0)),
        out_shape=jax.ShapeDtypeStruct((t, D_MODEL), F32),
        scratch_shapes=[pltpu.VMEM((D_S5 // LANES, tm, LANES), F32), pltpu.VMEM((2, tm, D_MODEL), F32)],
        compiler_params=pltpu.CompilerParams(dimension_semantics=("arbitrary",),
                                             vmem_limit_bytes=VMEM_LIMIT),
        name="mix_ffn",
    )(x, o, g, yg, u, dsk, gn, wglu, s5n, wo, gpost, gpre, wg, wu, wd, gpostf)


def kernel(x_prompt, x_sample, state_gla, state_s5_re, state_s5_im, meta_tokens, g_pre_mix, w_in, w_gk2, b_gk, gla_norm, s5_a_re, s5_a_im, s5_b_re, s5_b_im, s5_c_re, s5_c_im, s5_d, s5_log_dt, w_s5_glu, s5_norm, w_o, g_post_mix, g_pre_ffn, w_gate, w_up, w_down, g_post_ffn):
    assert g_pre_mix.shape[0] == 1, "single-layer step"
    bp, seq_p, _ = x_prompt.shape
    bs, seq_s, _ = x_sample.shape
    row = lambda t: t[0].reshape(1, -1)

    wo_bf = w_o[0].astype(BF16)
    s5m = _s5_prep(s5_a_re[0], s5_a_im[0], s5_b_re[0], s5_b_im[0], s5_c_re[0], s5_c_im[0], s5_log_dt[0])
    proj_w = (row(g_pre_mix), w_in[0], w_gk2[0], row(b_gk))

    def finish(x, o, g, yg, u, tm, steps):
        return _mix_ffn(x, o, g, yg, u, row(s5_d), row(gla_norm), w_s5_glu[0].astype(BF16), row(s5_norm),
                        wo_bf, row(g_post_mix), row(g_pre_ffn), w_gate[0].astype(BF16),
                        w_up[0].astype(BF16), w_down[0].astype(BF16), row(g_post_ffn), tm, steps)

    xm = jnp.broadcast_to(meta_tokens[None], (bp, N_META, D_MODEL)).reshape(bp * N_META, D_MODEL)
    q, k, v, _, _, lg, ug = _in_proj(xm, *proj_w, bp * N_META, S5_BLOCK)
    r3 = lambda t, b, l: t.reshape(b, l, t.shape[-1])
    _, s_meta = _gla(r3(q, bp, N_META), r3(k, bp, N_META), r3(v, bp, N_META), r3(lg, bp, N_META),
                     jnp.zeros((bp, GLA_KDIM, GLA_DV), F32), bp, bp, N_META)
    zh = jnp.zeros((bp * S5_GROUPS // 2, LANES), F32)
    _, hm_re, hm_im = _s5(ug, s5m[S5_BLOCK], zh, zh, 1, bp)

    xp = x_prompt.reshape(bp * seq_p, D_MODEL)
    g, u, o, ug, s_p = _in_proj(xp, *proj_w, 512, S5_BLOCK, gla_s0=s_meta, seq_len=seq_p)
    yg, hp_re, hp_im = _s5(ug, s5m[S5_BLOCK], hm_re, hm_im, bp, 32)
    y_prompt = finish(xp, o, g, yg, u, 512, S5_BLOCK)

    xs = x_sample.reshape(bs * seq_s, D_MODEL)
    to_g = lambda t: t[0].reshape(bs * S5_GROUPS // 2, LANES)
    q, k, v, g, u, lg, ug = _in_proj(xs, *proj_w, 512, seq_s)
    o, s_s = _gla(r3(q, bs, seq_s), r3(k, bs, seq_s), r3(v, bs, seq_s), r3(lg, bs, seq_s),
                  state_gla[0].reshape(bs, GLA_KDIM, GLA_DV), 32, 16, seq_s)
    yg, hs_re, hs_im = _s5(ug, s5m[seq_s], to_g(state_s5_re), to_g(state_s5_im), 1, bs)
    y_sample = finish(xs, o.reshape(bs * seq_s, D_GLA), g, yg, u, 512, seq_s)

    gla_out = lambda s, b: s.reshape(1, b, GLA_HEADS, GLA_DK, GLA_DV)
    s5_out = lambda h: h.reshape(1, -1, S5_GROUPS, S5_STATE)
    return (y_prompt.reshape(bp, seq_p, D_MODEL), y_sample.reshape(bs, seq_s, D_MODEL),
            gla_out(s_p, bp), s5_out(hp_re), s5_out(hp_im),
            gla_out(s_s, bs), s5_out(hs_re), s5_out(hs_im))
```

```python
import functools

import jax
import jax.numpy as jnp
from jax import lax
from jax.experimental import pallas as pl
from jax.experimental.pallas import tpu as pltpu

F32 = jnp.float32
BF16 = jnp.bfloat16

D_MODEL = 1024
D_GLA = 512
GLA_HEADS = 4
GLA_DV = 128
GLA_DK = 64
GLA_KDIM = 256
GATE_RANK = 16
GATE_NORM = 16.0
GLA_CHUNK = 64
D_S5 = 512
S5_GROUP = 16
S5_GROUPS = 32
S5_STATE = 64
N_META = 16
D_FF = 2816
EPS = 1e-6
LANES = 128
S5_BLOCK = 16
SLAB_GROUPS = LANES // S5_GROUP
PROJ_W = 2176
FF_CHUNK = 256
TOKEN_TILE = 512
S5_ROW_BLOCKS = 32
GLA_SAMPLE_GROUP = 16
GLA_SAMPLE_SEQS = 32
VMEM_LIMIT = 48 * 1024 * 1024


def _rms(x, g):
    return x * lax.rsqrt(jnp.mean(x * x, axis=-1, keepdims=True) + EPS) * g


def _dot(a, b):
    return jnp.dot(a, b, preferred_element_type=F32)


def _dot_nt(a, b):
    return lax.dot_general(a, b, (((1,), (1,)), ((), ())), preferred_element_type=F32)


def _dot_tn(a, b):
    return lax.dot_general(a, b, (((0,), (0,)), ((), ())), preferred_element_type=F32)


def _const_spec(shape):
    zeros = (0,) * len(shape)
    return pl.BlockSpec(shape, lambda *_: zeros)


PREP_GROUPS = 8


def _cmul(a, b):
    return a[0] * b[0] - a[1] * b[1], a[0] * b[1] + a[1] * b[0]


def _unit_powers(c1, s1, expo, n_bits):
    acc = (jnp.ones_like(c1), jnp.zeros_like(c1))
    base = (c1, s1)
    squares = [base]
    for bit in range(n_bits):
        take = ((expo >> bit) & 1) == 1
        nxt = _cmul(acc, base)
        acc = (jnp.where(take, nxt[0], acc[0]), jnp.where(take, nxt[1], acc[1]))
        base = _cmul(base, base)
        squares.append(base)
    return acc, squares


def _s5_prep_kernel(arc_ref, aic_ref, air_ref, ldt_ref, br_ref, bi_ref, cr_ref, ci_ref,
                    mi16_ref, mi8_ref, inr16_ref, ini16_ref, inr8_ref, ini8_ref,
                    outr16_ref, outi16_ref, outr8_ref, outi8_ref, a16r_ref, a16i_ref, a8r_ref, a8i_ref):
    n, q = S5_STATE, S5_BLOCK
    w = q * S5_GROUP
    hp = lax.Precision.HIGHEST
    lane = lax.broadcasted_iota(jnp.int32, (n, w), 1)
    t_blk = lane >> 4
    eye = lax.broadcasted_iota(jnp.int32, (n, n), 0) == lax.broadcasted_iota(jnp.int32, (n, n), 1)
    to_col = lambda r: jnp.sum(jnp.where(eye, r, 0.0), axis=1, keepdims=True)
    zrows = lambda x: jnp.concatenate([x, jnp.zeros_like(x)], axis=0)
    lane_n = lax.broadcasted_iota(jnp.int32, (n, LANES), 1)
    rep = jnp.where((lax.broadcasted_iota(jnp.int32, (S5_GROUP, w), 1) & (S5_GROUP - 1))
                    == lax.broadcasted_iota(jnp.int32, (S5_GROUP, w), 0), 1.0, 0.0)
    mo_g, rev_g, hop_g = [], [], []
    for g in range(PREP_GROUPS):
        dt = jnp.exp(ldt_ref[g])
        ang_r = air_ref[g] * dt
        c1, s1 = to_col(jnp.cos(ang_r)), to_col(jnp.sin(ang_r))
        lam_re = jnp.minimum(arc_ref[g], -1e-4)
        lam_im = aic_ref[g]
        unit, squares = _unit_powers(c1, s1, t_blk, 4)
        pm = jnp.exp(t_blk.astype(F32) * (lam_re * dt))
        pk = (pm * unit[0], pm * unit[1])
        mag = jnp.exp(lam_re * dt)
        ab = (mag * c1, mag * s1)
        p1 = _cmul(pk, ab)
        ct = tuple(lax.dot_general(r[g], rep, (((0,), (0,)), ((), ())), precision=hp,
                                   preferred_element_type=F32) for r in (cr_ref, ci_ref))
        bt = tuple(jnp.dot(r[g], rep, precision=hp, preferred_element_type=F32) for r in (br_ref, bi_ref))
        g0 = _cmul(ct, pk)
        mo = _cmul(ct, p1)
        mo_g.append((mo[0], -mo[1]))
        den = lam_re * lam_re + lam_im * lam_im
        nr, ni = ab[0] - 1.0, ab[1]
        f = ((nr * lam_re + ni * lam_im) / den, (ni * lam_re - nr * lam_im) / den)
        e = _cmul(pk, _cmul(f, bt))
        et = [zrows(x).T for x in e]
        rev_g.append([jnp.concatenate([x[(q - 1 - s) * S5_GROUP:(q - s) * S5_GROUP] for s in range(q)], axis=0)
                      for x in et])
        hops = []
        for steps in (q, q // 2):
            m = jnp.exp(float(steps) * (lam_re * dt))
            u = squares[steps.bit_length() - 1]
            hops += [m * u[0], m * u[1]]
        cols = jnp.zeros((n, LANES), F32)
        for idx, hcol in enumerate(hops):
            cols = jnp.where(lane_n == idx, hcol, cols)
        hop_g.append(zrows(cols).T)
        t0 = (jnp.dot(et[0][:S5_GROUP, :n], g0[0], precision=hp, preferred_element_type=F32)
              - jnp.dot(et[1][:S5_GROUP, :n], g0[1], precision=hp, preferred_element_type=F32))
        lane_t = lax.broadcasted_iota(jnp.int32, t0.shape, 1) >> 4
        for s in range(q):
            blk = t0 if s == 0 else jnp.where(lane_t >= s, pltpu.roll(t0, S5_GROUP * s, 1), 0.0)
            mi16_ref[g, s * S5_GROUP:(s + 1) * S5_GROUP, :] = blk.astype(BF16)
            if s < q // 2:
                mi8_ref[g, s * S5_GROUP:(s + 1) * S5_GROUP, :] = blk[:, :w // 2].astype(BF16)

    half = lambda x: pltpu.roll(x, n, 1)
    for p in range(PREP_GROUPS // 2):
        g0, g1 = 2 * p, 2 * p + 1
        for c, (in16, in8, out16, out8) in enumerate(((inr16_ref, inr8_ref, outr16_ref, outr8_ref),
                                                      (ini16_ref, ini8_ref, outi16_ref, outi8_ref))):
            lo, hi = rev_g[g0][c], half(rev_g[g1][c])
            in16[p] = jnp.concatenate([lo, hi], axis=0).astype(BF16)
            in8[p] = jnp.concatenate([lo[w // 2:], hi[w // 2:]], axis=0).astype(BF16)
            m0, m1 = mo_g[g0][c], mo_g[g1][c]
            z, zh = jnp.zeros_like(m0), jnp.zeros((n, w // 2), F32)
            out16[p] = jnp.concatenate([jnp.concatenate([m0, z], axis=1),
                                        jnp.concatenate([z, m1], axis=1)], axis=0).astype(BF16)
            out8[p] = jnp.concatenate([jnp.concatenate([m0[:, :w // 2], zh], axis=1),
                                       jnp.concatenate([zh, m1[:, :w // 2]], axis=1)], axis=0).astype(BF16)
        hops = hop_g[g0] + half(hop_g[g1])
        a16r_ref[p] = hops[0:1]
        a16i_ref[p] = hops[1:2]
        a8r_ref[p] = hops[2:3]
        a8i_ref[p] = hops[3:4]


def _s5_prep(a_re, a_im, b_re, b_im, c_re, c_im, log_dt):
    g, n, j, q = S5_GROUPS, S5_STATE, S5_GROUP, S5_BLOCK
    w, h, pg = q * j, q * j // 2, PREP_GROUPS
    col = lambda t: t.reshape(g, n, 1)
    rowv = lambda t: t.reshape(g, 1, n)
    spec = lambda a, b_: pl.BlockSpec((pg, a, b_), lambda i: (i, 0, 0))
    pspec = lambda a, b_: pl.BlockSpec((pg // 2, a, b_), lambda i: (i, 0, 0))
    pair_shapes = [(2 * w, LANES), (2 * w, LANES), (2 * h, LANES), (2 * h, LANES),
                   (LANES, 2 * w), (LANES, 2 * w), (LANES, 2 * h), (LANES, 2 * h)]
    outs = pl.pallas_call(
        _s5_prep_kernel,
        grid=(g // pg,),
        in_specs=[spec(n, 1), spec(n, 1), spec(1, n), spec(1, 1),
                  spec(n, j), spec(n, j), spec(j, n), spec(j, n)],
        out_specs=[spec(w, w), spec(h, h)] + [pspec(*s) for s in pair_shapes] + [pspec(1, LANES)] * 4,
        out_shape=[jax.ShapeDtypeStruct((g, w, w), BF16), jax.ShapeDtypeStruct((g, h, h), BF16)]
                  + [jax.ShapeDtypeStruct((g // 2,) + s, BF16) for s in pair_shapes]
                  + [jax.ShapeDtypeStruct((g // 2, 1, LANES), F32)] * 4,
        compiler_params=pltpu.CompilerParams(dimension_semantics=("parallel",)),
        name="s5_prep",
    )(col(a_re), col(a_im), rowv(a_im), log_dt.reshape(g, 1, 1),
      b_re, b_im, c_re, c_im)
    mi16, mi8, inr16, ini16, inr8, ini8, outr16, outi16, outr8, outi8, a16r, a16i, a8r, a8i = outs
    return {q: dict(m_intra=mi16, m_in_re=inr16, m_in_im=ini16, m_out_re=outr16, m_out_im=outi16,
                    a_re=a16r, a_im=a16i),
            q // 2: dict(m_intra=mi8, m_in_re=inr8, m_in_im=ini8, m_out_re=outr8, m_out_im=outi8,
                         a_re=a8r, a_im=a8i)}


def _lane_block_transpose(a):
    a = list(a)
    blk = lax.broadcasted_iota(jnp.int32, a[0].shape, 1) >> 4
    for d in (4, 2, 1):
        upper = (blk & d) != 0
        for r in range(8):
            if r & d:
                continue
            lo, hi = a[r], a[r + d]
            a[r] = jnp.where(upper, pltpu.roll(hi, 16 * d, 1), lo)
            a[r + d] = jnp.where(upper, hi, pltpu.roll(lo, LANES - 16 * d, 1))
    return a


def _in_proj_kernel(*refs, steps, n_tiles, tiles_per_seq):
    fused = tiles_per_seq is not None
    x_ref, gpre_ref, w_ref, wgk_ref, bgk_ref = refs[:5]
    if fused:
        (s0_ref, g_ref, u_ref, o_ref, ug_ref, sfin_ref,
         u_scr, gk_scr, q_scr, k_scr, v_scr, s_scr) = refs[5:]
    else:
        q_ref, k_ref, v_ref, g_ref, u_ref, lg_ref, ug_ref, u_scr, gk_scr = refs[5:]
    i = pl.program_id(0)
    tm = x_ref.shape[0]

    def project(slot):
        h = _rms(x_ref[...], gpre_ref[...])
        proj = _dot(h.astype(BF16), w_ref[...])
        if fused:
            q_scr[slot] = proj[:, 0:256]
            k_scr[slot] = proj[:, 256:512]
            v_scr[slot] = proj[:, 512:1024].astype(BF16)
        else:
            q_ref[...] = proj[:, 0:256]
            k_ref[...] = proj[:, 256:512]
            v_ref[...] = proj[:, 512:1024]
        g_ref[...] = proj[:, 1024:1536]
        u = proj[:, 1536:2048]
        u_ref[...] = u
        for s in range(D_S5 // LANES):
            u_scr[slot, s] = u[:, s * LANES:(s + 1) * LANES]
        gk_scr[slot] = proj[:, 2048:PROJ_W]

    def tail(slot):
        z = _dot(gk_scr[slot].astype(BF16), wgk_ref[...]) + bgk_ref[...]
        lg = jax.nn.log_sigmoid(z) * (1.0 / GATE_NORM)
        if fused:
            tile = i - 1
            seq = tile // tiles_per_seq
            state = jnp.where(tile % tiles_per_seq == 0, s0_ref[seq], s_scr[...])
            masks = _gla_masks(1, GLA_CHUNK)
            for c in range(tm // GLA_CHUNK):
                rows = slice(c * GLA_CHUNK, (c + 1) * GLA_CHUNK)
                o, (state,) = _gla_group(q_scr[slot, rows, :], k_scr[slot, rows, :], v_scr[slot, rows, :],
                                         lg[rows], [state], masks, GLA_CHUNK)
                o_ref[rows, :] = o
            s_scr[...] = state
            sfin_ref[seq] = state
        else:
            lg_ref[...] = lg
        nr = tm // steps
        rt = min(nr, 16)
        for s in range(D_S5 // LANES):
            for hf in range(steps // 8):
                for r0 in range(0, nr, rt):
                    a = [u_scr[slot, s, pl.ds(r0 * steps + hf * 8 + t, rt, stride=steps), :] for t in range(8)]
                    per_group = _lane_block_transpose(a)
                    for g in range(SLAB_GROUPS):
                        ug_ref[s * SLAB_GROUPS + g, r0:r0 + rt, hf * LANES:(hf + 1) * LANES] = (
                            per_group[g].astype(BF16))

    @pl.when(i == 0)
    def _():
        if fused:
            s_scr[...] = jnp.zeros(s_scr.shape, F32)
        project(0)

    @pl.when((i > 0) & (i < n_tiles))
    def _():
        tail((i - 1) & 1)
        project(i & 1)

    @pl.when(i == n_tiles)
    def _():
        tail((i - 1) & 1)


def _in_proj(x, gpre, w_p, wgk_p, bgk, tm, steps, gla_s0=None, seq_len=None):
    t = x.shape[0]
    n = t // tm
    cur = lambda i: jnp.minimum(i, n - 1)
    prev = lambda i: jnp.maximum(i - 1, 0)
    row = lambda w, at: pl.BlockSpec((tm, w), lambda i: (at(i), 0))
    wg = steps * S5_GROUP
    ug_spec = pl.BlockSpec((S5_GROUPS, tm // steps, wg), lambda i: (0, prev(i), 0))
    ug_shape = jax.ShapeDtypeStruct((S5_GROUPS, t // steps, wg), BF16)
    f32 = lambda w: jax.ShapeDtypeStruct((t, w), F32)
    in_specs = [row(D_MODEL, cur), _const_spec((1, D_MODEL)), _const_spec((D_MODEL, PROJ_W)),
                _const_spec((LANES, GLA_KDIM)), _const_spec((1, GLA_KDIM))]
    scratch = [pltpu.VMEM((2, D_S5 // LANES, tm, LANES), F32), pltpu.VMEM((2, tm, LANES), F32)]
    args = (x, gpre, w_p, wgk_p, bgk)
    if gla_s0 is None:
        tiles_per_seq = None
        out_specs = [row(w, cur) for w in (256, 256, 512, 512, 512)] + [row(GLA_KDIM, prev), ug_spec]
        out_shape = [f32(w) for w in (256, 256, 512, 512, 512)] + [f32(GLA_KDIM), ug_shape]
    else:
        tiles_per_seq = seq_len // tm
        assert seq_len % tm == 0 and tm % GLA_CHUNK == 0
        in_specs.append(_const_spec(gla_s0.shape))
        args += (gla_s0,)
        out_specs = [row(D_GLA, cur), row(D_S5, cur), row(D_GLA, prev), ug_spec, _const_spec(gla_s0.shape)]
        out_shape = [f32(D_GLA), f32(D_S5), f32(D_GLA), ug_shape, jax.ShapeDtypeStruct(gla_s0.shape, F32)]
        scratch += [pltpu.VMEM((2, tm, GLA_KDIM), F32), pltpu.VMEM((2, tm, GLA_KDIM), F32),
                    pltpu.VMEM((2, tm, D_GLA), BF16), pltpu.VMEM(gla_s0.shape[1:], F32)]
    return pl.pallas_call(
        functools.partial(_in_proj_kernel, steps=steps, n_tiles=n, tiles_per_seq=tiles_per_seq),
        grid=(n + 1,),
        in_specs=in_specs,
        out_specs=out_specs,
        out_shape=out_shape,
        scratch_shapes=scratch,
        compiler_params=pltpu.CompilerParams(dimension_semantics=("arbitrary",),
                                             vmem_limit_bytes=VMEM_LIMIT),
        name="in_proj" if gla_s0 is None else "in_proj_gla",
    )(*args)


def _gla_masks(sg, chunk):
    r = sg * chunk
    assert r >= LANES or sg == 1
    shift = chunk.bit_length() - 1
    hk = GLA_HEADS * r
    ri = lax.broadcasted_iota(jnp.int32, (r, r), 0)
    ci = lax.broadcasted_iota(jnp.int32, (r, r), 1)
    ri4 = lax.broadcasted_iota(jnp.int32, (r, hk), 0)
    ci4 = lax.broadcasted_iota(jnp.int32, (r, hk), 1) & (r - 1)
    stack_head = lax.broadcasted_iota(jnp.int32, (hk, 1), 0) >> (r.bit_length() - 1)
    k_head = lax.broadcasted_iota(jnp.int32, (1, GLA_KDIM), 1) >> 6
    v_head = lax.broadcasted_iota(jnp.int32, (1, D_GLA), 1) >> 7
    s_head = lax.broadcasted_iota(jnp.int32, (GLA_KDIM, 1), 0) >> 6
    return dict(
        tri_bf=jnp.where(((ri >> shift) == (ci >> shift)) & (ri >= ci), 1.0, 0.0).astype(BF16),
        causal4=((ri4 >> shift) == (ci4 >> shift)) & (ri4 >= ci4),
        k_diag=stack_head == k_head, v_diag=stack_head == v_head, s_diag=s_head == v_head)


def _gla_group(q, k, v_bf, lg, states, m, chunk):
    sg = len(states)
    r = sg * chunk
    tile4 = lambda t: jnp.concatenate([t] * GLA_HEADS, axis=0)
    lg_hi = lg.astype(BF16)
    lg_lo = (lg - lg_hi.astype(F32)).astype(BF16)
    b2 = _dot(m["tri_bf"], jnp.concatenate([lg_hi, lg_lo], axis=1))
    b = b2[:, :GLA_KDIM] + b2[:, GLA_KDIM:]
    lasts = [b[(i + 1) * chunk - 1:(i + 1) * chunk, :] for i in range(sg)]
    bl = jnp.concatenate([jnp.broadcast_to(t, (chunk, GLA_KDIM)) for t in lasts], axis=0)
    qd = (q * (GLA_DK ** -0.5) * jnp.exp(b)).astype(BF16)
    ki = (k * jnp.exp(-b)).astype(BF16)
    ke = k * jnp.exp(bl - b)

    ki_bd = jnp.where(m["k_diag"], tile4(ki), 0.0)
    att = jnp.where(m["causal4"], _dot_nt(qd, ki_bd), 0.0).astype(BF16)
    v_bd = jnp.where(m["v_diag"], tile4(v_bf), 0.0)
    o_intra = _dot(att, v_bd)

    if r < LANES:
        aug_t = jnp.concatenate([ke, jnp.broadcast_to(lasts[0], (LANES - r, GLA_KDIM))], axis=0).T
        ke_t, bl_t = aug_t, aug_t[:, r:]
    else:
        ke_t = ke.T
        bl_t = jnp.concatenate(lasts + [jnp.zeros((r - sg, GLA_KDIM), F32)], axis=0).T
    ke_t = ke_t.astype(BF16)

    outs, new_states = [], []
    for i, s_old in enumerate(states):
        rows = slice(i * chunk, (i + 1) * chunk)
        s_bd = jnp.where(m["s_diag"], jnp.concatenate([s_old.astype(BF16)] * GLA_HEADS, axis=1), 0.0)
        outs.append(o_intra[rows] + _dot(qd[rows], s_bd))
        upd = [_dot(ke_t[h * GLA_DK:(h + 1) * GLA_DK, rows], v_bf[rows, h * GLA_DV:(h + 1) * GLA_DV])
               for h in range(GLA_HEADS)]
        new_states.append(jnp.exp(bl_t[:, i:i + 1]) * s_old + jnp.concatenate(upd, axis=0))
    return (outs[0] if sg == 1 else jnp.concatenate(outs, axis=0)), new_states


def _gla_kernel(q_ref, k_ref, v_ref, lg_ref, s0_ref, o_ref, s_ref, *, bb, sg, chunk):
    r = sg * chunk

    @pl.when(pl.program_id(1) == 0)
    def _():
        s_ref[...] = s0_ref[...]

    masks = _gla_masks(sg, chunk)
    for gi in range(bb // sg):
        seqs = slice(gi * sg, (gi + 1) * sg)
        o, new = _gla_group(q_ref[seqs].reshape(r, GLA_KDIM), k_ref[seqs].reshape(r, GLA_KDIM),
                            v_ref[seqs].reshape(r, D_GLA).astype(BF16), lg_ref[seqs].reshape(r, GLA_KDIM),
                            [s_ref[gi * sg + i] for i in range(sg)], masks, chunk)
        o_ref[seqs] = o.reshape(sg, chunk, D_GLA)
        for i in range(sg):
            s_ref[gi * sg + i] = new[i]


def _gla(q, k, v, lg, s0, bb, sg, chunk):
    b, l, _ = q.shape
    blk = lambda w: pl.BlockSpec((bb, chunk, w), lambda i, c: (i, c, 0))
    sspec = pl.BlockSpec((bb, GLA_KDIM, GLA_DV), lambda i, c: (i, 0, 0))
    return pl.pallas_call(
        functools.partial(_gla_kernel, bb=bb, sg=sg, chunk=chunk),
        grid=(b // bb, l // chunk),
        in_specs=[blk(GLA_KDIM), blk(GLA_KDIM), blk(D_GLA), blk(GLA_KDIM), sspec],
        out_specs=[blk(D_GLA), sspec],
        out_shape=[jax.ShapeDtypeStruct((b, l, D_GLA), F32),
                   jax.ShapeDtypeStruct((b, GLA_KDIM, GLA_DV), F32)],
        compiler_params=pltpu.CompilerParams(dimension_semantics=("parallel", "arbitrary"),
                                             vmem_limit_bytes=VMEM_LIMIT),
        name=f"gla_c{chunk}",
    )(q, k, v, lg, s0)


def _s5_kernel(ug_ref, mi_ref, minr_ref, mini_ref, mor_ref, moi_ref, ar_ref, ai_ref,
               h0r_ref, h0i_ref, yg_ref, hfr_ref, hfi_ref, vr_s, vi_s, hr_s, hi_s, str_s, sti_s, *, nb, cb):
    gs = SLAB_GROUPS // 2
    n_pairs = S5_GROUPS // 2
    w = ug_ref.shape[-1]
    rows = nb * cb
    batch = cb if nb == 1 else nb
    n_blocks = rows // batch
    interleave = nb > 1 and cb > 1
    first_pair = pl.program_id(0) * gs
    seq0 = pl.program_id(1) * batch if nb == 1 else 0

    def state_rows(p):
        return pl.ds(seq0 * n_pairs + first_pair + p, batch, stride=n_pairs)

    def load_state():
        for p in range(gs):
            str_s[p] = h0r_ref[state_rows(p), :]
            sti_s[p] = h0i_ref[state_rows(p), :]

    if nb == 1:
        load_state()
    else:
        pl.when(pl.program_id(1) == 0)(load_state)

    def to_scan_order(ref, g, val):
        if not interleave:
            ref[g] = val
        else:
            for b in range(nb):
                ref[g, pl.ds(b, cb, stride=nb), :] = val[b * cb:(b + 1) * cb]

    def from_scan_order(ref, g):
        if not interleave:
            return ref[g]
        return jnp.concatenate([ref[g, pl.ds(b, cb, stride=nb), :] for b in range(nb)], axis=0)

    ub = [ug_ref[g].reshape(rows, w) for g in range(2 * gs)]
    for g in range(gs):
        pair_u = jnp.concatenate([ub[2 * g], ub[2 * g + 1]], axis=1)
        to_scan_order(vr_s, g, _dot(pair_u, minr_ref[g]))
        to_scan_order(vi_s, g, _dot(pair_u, mini_ref[g]))
    ar = [ar_ref[g] for g in range(gs)]
    ai = [ai_ref[g] for g in range(gs)]

    def body(c, carry):
        rows = pl.ds(pl.multiple_of(c * batch, batch), batch)
        new = []
        for g in range(gs):
            hr, hi = carry[2 * g], carry[2 * g + 1]
            hr_s[g, rows, :] = hr
            hi_s[g, rows, :] = hi
            new.append(ar[g] * hr - ai[g] * hi + vr_s[g, rows, :])
            new.append(ar[g] * hi + ai[g] * hr + vi_s[g, rows, :])
        return tuple(new)

    init = tuple(ref[g] for g in range(gs) for ref in (str_s, sti_s))
    fin = lax.fori_loop(0, n_blocks, body, init)
    for g in range(gs):
        str_s[g] = fin[2 * g]
        sti_s[g] = fin[2 * g + 1]
        hfr_ref[state_rows(g), :] = fin[2 * g]
        hfi_ref[state_rows(g), :] = fin[2 * g + 1]
        y_state = (_dot(from_scan_order(hr_s, g).astype(BF16), mor_ref[g])
                   + _dot(from_scan_order(hi_s, g).astype(BF16), moi_ref[g]))
        for k in range(2):
            y = _dot(ub[2 * g + k], mi_ref[2 * g + k]) + y_state[:, k * w:(k + 1) * w]
            yg_ref[2 * g + k] = y.reshape(yg_ref.shape[1:])


def _s5(ug, m, h0_re, h0_im, nb, cb):
    g_all, r, w = ug.shape
    nc = r // nb
    batch = cb if nb == 1 else nb
    n, gs = LANES, SLAB_GROUPS
    uspec = pl.BlockSpec((gs, nb, cb, w), lambda s, i: (s, 0, i, 0))
    per_g = lambda a, b_: pl.BlockSpec((gs, a, b_), lambda s, i: (s, 0, 0))
    per_p = lambda a, b_: pl.BlockSpec((gs // 2, a, b_), lambda s, i: (s, 0, 0))
    hspec = _const_spec(h0_re.shape)
    yg, hf_re, hf_im = pl.pallas_call(
        functools.partial(_s5_kernel, nb=nb, cb=cb),
        grid=(g_all // gs, nc // cb),
        in_specs=[uspec, per_g(w, w), per_p(2 * w, n), per_p(2 * w, n), per_p(n, 2 * w), per_p(n, 2 * w),
                  per_p(1, n), per_p(1, n), hspec, hspec],
        out_specs=[uspec, hspec, hspec],
        out_shape=[jax.ShapeDtypeStruct((g_all, nb, nc, w), F32),
                   jax.ShapeDtypeStruct(h0_re.shape, F32),
                   jax.ShapeDtypeStruct(h0_im.shape, F32)],
        scratch_shapes=[pltpu.VMEM((gs // 2, nb * cb, n), F32)] * 4 + [pltpu.VMEM((gs // 2, batch, n), F32)] * 2,
        compiler_params=pltpu.CompilerParams(dimension_semantics=("arbitrary", "arbitrary"),
                                             vmem_limit_bytes=VMEM_LIMIT),
        name=f"s5_w{w}",
    )(ug.reshape(g_all, nb, nc, w), m["m_intra"], m["m_in_re"], m["m_in_im"], m["m_out_re"], m["m_out_im"],
      m["a_re"], m["a_im"], h0_re, h0_im)
    return yg.reshape(g_all, r, w), hf_re, hf_im


def _mix_ffn_kernel(x_ref, o_ref, g_ref, yg_ref, u_ref, dsk_ref, gn_ref, wglu_ref, s5n_ref, wo_ref, gpost_ref,
                    gpre_ref, wg_ref, wu_ref, wd_ref, gpostf_ref, out_ref, y_scr, x1_scr, *, steps):
    i = pl.program_id(0)

    def mix_into(slot):
        nr = x_ref.shape[0] // steps
        rt = min(nr, 8)
        for s in range(D_S5 // LANES):
            for hf in range(steps // 8):
                for r0 in range(0, nr, rt):
                    a = [yg_ref[s * SLAB_GROUPS + g, r0:r0 + rt, hf * LANES:(hf + 1) * LANES]
                         for g in range(SLAB_GROUPS)]
                    per_step = _lane_block_transpose(a)
                    for t in range(8):
                        y_scr[s, pl.ds(r0 * steps + hf * 8 + t, rt, stride=steps), :] = per_step[t]
        y5 = jnp.concatenate([y_scr[s] for s in range(D_S5 // LANES)], axis=1) + dsk_ref[...] * u_ref[...]
        o = o_ref[...]
        gn = gn_ref[...]
        heads = []
        for h in range(GLA_HEADS):
            heads.append(_rms(o[:, h * GLA_DV:(h + 1) * GLA_DV], gn))
        og = jnp.concatenate(heads, axis=1) * jax.nn.silu(g_ref[...])
        y = jax.nn.gelu(y5)
        y = y * jax.nn.sigmoid(_dot(y.astype(BF16), wglu_ref[...]))
        y = _rms(y, s5n_ref[...])
        mix = _dot(og.astype(BF16), wo_ref[:D_GLA, :]) + _dot(y.astype(BF16), wo_ref[D_GLA:, :])
        x1_scr[slot] = x_ref[...] + _rms(mix, gpost_ref[...])

    def ffn_from(slot):
        x = x1_scr[slot]
        h = _rms(x, gpre_ref[...]).astype(BF16)
        acc = jnp.zeros(x.shape, F32)
        for c in range(D_FF // FF_CHUNK):
            cols = slice(c * FF_CHUNK, (c + 1) * FF_CHUNK)
            act = jax.nn.silu(_dot(h, wg_ref[:, cols])) * _dot(h, wu_ref[:, cols])
            acc = acc + _dot(act.astype(BF16), wd_ref[cols, :])
        out_ref[...] = x + _rms(acc, gpostf_ref[...])

    @pl.when(i == 0)
    def _():
        mix_into(0)

    @pl.when(i > 0)
    def _():
        ffn_from((i - 1) & 1)
        mix_into(i & 1)


def _mix_ffn(x, o, g, yg, u, dsk, gn, wglu, s5n, wo, gpost, gpre, wg, wu, wd, gpostf, tm, steps):
    t = x.shape[0]
    n = t // tm
    cur = lambda i: jnp.minimum(i, n - 1)
    row = lambda w: pl.BlockSpec((tm, w), lambda i: (cur(i), 0))
    fixed = lambda shape: pl.BlockSpec(shape, lambda i: (0,) * len(shape), pipeline_mode=pl.Buffered(1))
    return pl.pallas_call(
        functools.partial(_mix_ffn_kernel, steps=steps),
        grid=(n + 1,),
        in_specs=[row(D_MODEL), row(D_GLA), row(D_GLA),
                  pl.BlockSpec((S5_GROUPS, tm // steps, steps * S5_GROUP), lambda i: (0, cur(i), 0)),
                  row(D_S5), fixed((1, D_S5)),
                  fixed((1, GLA_DV)), fixed((D_S5, D_S5)), fixed((1, D_S5)),
                  fixed((D_GLA + D_S5, D_MODEL)), fixed((1, D_MODEL)),
                  fixed((1, D_MODEL)), fixed((D_MODEL, D_FF)), fixed((D_MODEL, D_FF)),
                  fixed((D_FF, D_MODEL)), fixed((1, D_MODEL))],
        out_specs=pl.BlockSpec((tm, D_MODEL), lambda i: (jnp.maximum(i - 1, 0), 0)),
        out_shape=jax.ShapeDtypeStruct((t, D_MODEL), F32),
        scratch_shapes=[pltpu.VMEM((D_S5 // LANES, tm, LANES), F32), pltpu.VMEM((2, tm, D_MODEL), F32)],
        compiler_params=pltpu.CompilerParams(dimension_semantics=("arbitrary",),
                                             vmem_limit_bytes=VMEM_LIMIT),
        name="mix_ffn",
    )(x, o, g, yg, u, dsk, gn, wglu, s5n, wo, gpost, gpre, wg, wu, wd, gpostf)


def kernel(x_prompt, x_sample, state_gla, state_s5_re, state_s5_im, meta_tokens, g_pre_mix, w_in, w_gk2, b_gk, gla_norm, s5_a_re, s5_a_im, s5_b_re, s5_b_im, s5_c_re, s5_c_im, s5_d, s5_log_dt, w_s5_glu, s5_norm, w_o, g_post_mix, g_pre_ffn, w_gate, w_up, w_down, g_post_ffn):
    assert g_pre_mix.shape[0] == 1, "single-layer step"
    bp, seq_p, _ = x_prompt.shape
    bs, seq_s, _ = x_sample.shape
    row = lambda t: t[0].reshape(1, -1)

    w = w_in[0]
    c3, c4 = 1536, 1536 + GATE_RANK
    w_p = jnp.concatenate([w[:, :c3], w[:, c4:], w[:, c3:c4],
                           jnp.zeros((D_MODEL, LANES - GATE_RANK), F32)], axis=1).astype(BF16)
    wgk_p = jnp.concatenate([w_gk2[0], jnp.zeros((LANES - GATE_RANK, GLA_KDIM), F32)], axis=0).astype(BF16)
    wo_bf = w_o[0].astype(BF16)
    s5m = _s5_prep(s5_a_re[0], s5_a_im[0], s5_b_re[0], s5_b_im[0], s5_c_re[0], s5_c_im[0], s5_log_dt[0])
    proj_w = (row(g_pre_mix), w_p, wgk_p, row(b_gk))

    def finish(x, o, g, yg, u, tm, steps):
        return _mix_ffn(x, o, g, yg, u, row(s5_d), row(gla_norm), w_s5_glu[0].astype(BF16), row(s5_norm),
                        wo_bf, row(g_post_mix), row(g_pre_ffn), w_gate[0].astype(BF16),
                        w_up[0].astype(BF16), w_down[0].astype(BF16), row(g_post_ffn), tm, steps)

    xm = jnp.broadcast_to(meta_tokens[None], (bp, N_META, D_MODEL)).reshape(bp * N_META, D_MODEL)
    q, k, v, _, _, lg, ug = _in_proj(xm, *proj_w, bp * N_META, S5_BLOCK)
    r3 = lambda t, b, l: t.reshape(b, l, t.shape[-1])
    _, s_meta = _gla(r3(q, bp, N_META), r3(k, bp, N_META), r3(v, bp, N_META), r3(lg, bp, N_META),
                     jnp.zeros((bp, GLA_KDIM, GLA_DV), F32), bp, bp, N_META)
    zh = jnp.zeros((bp * S5_GROUPS // 2, LANES), F32)
    _, hm_re, hm_im = _s5(ug, s5m[S5_BLOCK], zh, zh, 1, bp)

    xp = x_prompt.reshape(bp * seq_p, D_MODEL)
    g, u, o, ug, s_p = _in_proj(xp, *proj_w, TOKEN_TILE, S5_BLOCK, gla_s0=s_meta, seq_len=seq_p)
    yg, hp_re, hp_im = _s5(ug, s5m[S5_BLOCK], hm_re, hm_im, bp, S5_ROW_BLOCKS)
    y_prompt = finish(xp, o, g, yg, u, TOKEN_TILE, S5_BLOCK)

    xs = x_sample.reshape(bs * seq_s, D_MODEL)
    to_g = lambda t: t[0].reshape(bs * S5_GROUPS // 2, LANES)
    q, k, v, g, u, lg, ug = _in_proj(xs, *proj_w, TOKEN_TILE, seq_s)
    o, s_s = _gla(r3(q, bs, seq_s), r3(k, bs, seq_s), r3(v, bs, seq_s), r3(lg, bs, seq_s),
                  state_gla[0].reshape(bs, GLA_KDIM, GLA_DV), GLA_SAMPLE_SEQS, GLA_SAMPLE_GROUP, seq_s)
    yg, hs_re, hs_im = _s5(ug, s5m[seq_s], to_g(state_s5_re), to_g(state_s5_im), 1, bs)
    y_sample = finish(xs, o.reshape(bs * seq_s, D_GLA), g, yg, u, TOKEN_TILE, seq_s)

    gla_out = lambda s, b: s.reshape(1, b, GLA_HEADS, GLA_DK, GLA_DV)
    s5_out = lambda h: h.reshape(1, -1, S5_GROUPS, S5_STATE)
    return (y_prompt.reshape(bp, seq_p, D_MODEL), y_sample.reshape(bs, seq_s, D_MODEL),
            gla_out(s_p, bp), s5_out(hp_re), s5_out(hp_im),
            gla_out(s_s, bs), s5_out(hs_re), s5_out(hs_im))
```

```python
import functools

import jax
import jax.numpy as jnp
from jax import lax
from jax.experimental import pallas as pl
from jax.experimental.pallas import tpu as pltpu

F32 = jnp.float32
BF16 = jnp.bfloat16

D_MODEL = 1024
D_GLA = 512
GLA_HEADS = 4
GLA_DV = 128
GLA_DK = 64
GLA_KDIM = 256
GATE_RANK = 16
GATE_NORM = 16.0
GLA_CHUNK = 64
D_S5 = 512
S5_GROUP = 16
S5_GROUPS = 32
S5_STATE = 64
N_META = 16
D_FF = 2816
EPS = 1e-6
LANES = 128
S5_BLOCK = 16
SLAB_GROUPS = LANES // S5_GROUP
FF_CHUNK = 256
TOKEN_TILE = 512
S5_ROW_BLOCKS = 32
GLA_SAMPLE_GROUP = 16
GLA_SAMPLE_SEQS = 32
VMEM_LIMIT = 48 * 1024 * 1024


def _rms(x, g):
    return x * lax.rsqrt(jnp.mean(x * x, axis=-1, keepdims=True) + EPS) * g


def _dot(a, b):
    return jnp.dot(a, b, preferred_element_type=F32)


def _dot_nt(a, b):
    return lax.dot_general(a, b, (((1,), (1,)), ((), ())), preferred_element_type=F32)


def _dot_tn(a, b):
    return lax.dot_general(a, b, (((0,), (0,)), ((), ())), preferred_element_type=F32)


def _const_spec(shape):
    zeros = (0,) * len(shape)
    return pl.BlockSpec(shape, lambda *_: zeros)


PREP_GROUPS = 8


def _cmul(a, b):
    return a[0] * b[0] - a[1] * b[1], a[0] * b[1] + a[1] * b[0]


def _unit_powers(c1, s1, expo, n_bits):
    acc = (jnp.ones_like(c1), jnp.zeros_like(c1))
    base = (c1, s1)
    squares = [base]
    for bit in range(n_bits):
        take = ((expo >> bit) & 1) == 1
        nxt = _cmul(acc, base)
        acc = (jnp.where(take, nxt[0], acc[0]), jnp.where(take, nxt[1], acc[1]))
        base = _cmul(base, base)
        squares.append(base)
    return acc, squares


def _s5_prep_kernel(arc_ref, aic_ref, air_ref, ldt_ref, br_ref, bi_ref, cr_ref, ci_ref,
                    mi16_ref, mi8_ref, inr16_ref, ini16_ref, inr8_ref, ini8_ref,
                    outr16_ref, outi16_ref, outr8_ref, outi8_ref, a16r_ref, a16i_ref, a8r_ref, a8i_ref):
    n, q = S5_STATE, S5_BLOCK
    w = q * S5_GROUP
    hp = lax.Precision.HIGHEST
    lane = lax.broadcasted_iota(jnp.int32, (n, w), 1)
    t_blk = lane >> 4
    eye = lax.broadcasted_iota(jnp.int32, (n, n), 0) == lax.broadcasted_iota(jnp.int32, (n, n), 1)
    to_col = lambda r: jnp.sum(jnp.where(eye, r, 0.0), axis=1, keepdims=True)
    zrows = lambda x: jnp.concatenate([x, jnp.zeros_like(x)], axis=0)
    lane_n = lax.broadcasted_iota(jnp.int32, (n, LANES), 1)
    rep = jnp.where((lax.broadcasted_iota(jnp.int32, (S5_GROUP, w), 1) & (S5_GROUP - 1))
                    == lax.broadcasted_iota(jnp.int32, (S5_GROUP, w), 0), 1.0, 0.0)
    mo_g, rev_g, hop_g = [], [], []
    for g in range(PREP_GROUPS):
        dt = jnp.exp(ldt_ref[g])
        ang_r = air_ref[g] * dt
        c1, s1 = to_col(jnp.cos(ang_r)), to_col(jnp.sin(ang_r))
        lam_re = jnp.minimum(arc_ref[g], -1e-4)
        lam_im = aic_ref[g]
        unit, squares = _unit_powers(c1, s1, t_blk, 4)
        pm = jnp.exp(t_blk.astype(F32) * (lam_re * dt))
        pk = (pm * unit[0], pm * unit[1])
        mag = jnp.exp(lam_re * dt)
        ab = (mag * c1, mag * s1)
        p1 = _cmul(pk, ab)
        ct = tuple(lax.dot_general(r[g], rep, (((0,), (0,)), ((), ())), precision=hp,
                                   preferred_element_type=F32) for r in (cr_ref, ci_ref))
        bt = tuple(jnp.dot(r[g], rep, precision=hp, preferred_element_type=F32) for r in (br_ref, bi_ref))
        g0 = _cmul(ct, pk)
        mo = _cmul(ct, p1)
        mo_g.append((mo[0], -mo[1]))
        den = lam_re * lam_re + lam_im * lam_im
        nr, ni = ab[0] - 1.0, ab[1]
        f = ((nr * lam_re + ni * lam_im) / den, (ni * lam_re - nr * lam_im) / den)
        e = _cmul(pk, _cmul(f, bt))
        et = [zrows(x).T for x in e]
        rev_g.append([jnp.concatenate([x[(q - 1 - s) * S5_GROUP:(q - s) * S5_GROUP] for s in range(q)], axis=0)
                      for x in et])
        hops = []
        for steps in (q, q // 2):
            m = jnp.exp(float(steps) * (lam_re * dt))
            u = squares[steps.bit_length() - 1]
            hops += [m * u[0], m * u[1]]
        cols = jnp.zeros((n, LANES), F32)
        for idx, hcol in enumerate(hops):
            cols = jnp.where(lane_n == idx, hcol, cols)
        hop_g.append(zrows(cols).T)
        t0 = (jnp.dot(et[0][:S5_GROUP, :n], g0[0], precision=hp, preferred_element_type=F32)
              - jnp.dot(et[1][:S5_GROUP, :n], g0[1], precision=hp, preferred_element_type=F32))
        lane_t = lax.broadcasted_iota(jnp.int32, t0.shape, 1) >> 4
        for s in range(q):
            blk = t0 if s == 0 else jnp.where(lane_t >= s, pltpu.roll(t0, S5_GROUP * s, 1), 0.0)
            mi16_ref[g, s * S5_GROUP:(s + 1) * S5_GROUP, :] = blk.astype(BF16)
            if s < q // 2:
                mi8_ref[g, s * S5_GROUP:(s + 1) * S5_GROUP, :] = blk[:, :w // 2].astype(BF16)

    half = lambda x: pltpu.roll(x, n, 1)
    for p in range(PREP_GROUPS // 2):
        g0, g1 = 2 * p, 2 * p + 1
        for c, (in16, in8, out16, out8) in enumerate(((inr16_ref, inr8_ref, outr16_ref, outr8_ref),
                                                      (ini16_ref, ini8_ref, outi16_ref, outi8_ref))):
            lo, hi = rev_g[g0][c], half(rev_g[g1][c])
            in16[p] = jnp.concatenate([lo, hi], axis=0).astype(BF16)
            in8[p] = jnp.concatenate([lo[w // 2:], hi[w // 2:]], axis=0).astype(BF16)
            m0, m1 = mo_g[g0][c], mo_g[g1][c]
            z, zh = jnp.zeros_like(m0), jnp.zeros((n, w // 2), F32)
            out16[p] = jnp.concatenate([jnp.concatenate([m0, z], axis=1),
                                        jnp.concatenate([z, m1], axis=1)], axis=0).astype(BF16)
            out8[p] = jnp.concatenate([jnp.concatenate([m0[:, :w // 2], zh], axis=1),
                                       jnp.concatenate([zh, m1[:, :w // 2]], axis=1)], axis=0).astype(BF16)
        hops = hop_g[g0] + half(hop_g[g1])
        a16r_ref[p] = hops[0:1]
        a16i_ref[p] = hops[1:2]
        a8r_ref[p] = hops[2:3]
        a8i_ref[p] = hops[3:4]


def _s5_prep(a_re, a_im, b_re, b_im, c_re, c_im, log_dt):
    g, n, j, q = S5_GROUPS, S5_STATE, S5_GROUP, S5_BLOCK
    w, h, pg = q * j, q * j // 2, PREP_GROUPS
    col = lambda t: t.reshape(g, n, 1)
    rowv = lambda t: t.reshape(g, 1, n)
    spec = lambda a, b_: pl.BlockSpec((pg, a, b_), lambda i: (i, 0, 0))
    pspec = lambda a, b_: pl.BlockSpec((pg // 2, a, b_), lambda i: (i, 0, 0))
    pair_shapes = [(2 * w, LANES), (2 * w, LANES), (2 * h, LANES), (2 * h, LANES),
                   (LANES, 2 * w), (LANES, 2 * w), (LANES, 2 * h), (LANES, 2 * h)]
    outs = pl.pallas_call(
        _s5_prep_kernel,
        grid=(g // pg,),
        in_specs=[spec(n, 1), spec(n, 1), spec(1, n), spec(1, 1),
                  spec(n, j), spec(n, j), spec(j, n), spec(j, n)],
        out_specs=[spec(w, w), spec(h, h)] + [pspec(*s) for s in pair_shapes] + [pspec(1, LANES)] * 4,
        out_shape=[jax.ShapeDtypeStruct((g, w, w), BF16), jax.ShapeDtypeStruct((g, h, h), BF16)]
                  + [jax.ShapeDtypeStruct((g // 2,) + s, BF16) for s in pair_shapes]
                  + [jax.ShapeDtypeStruct((g // 2, 1, LANES), F32)] * 4,
        compiler_params=pltpu.CompilerParams(dimension_semantics=("parallel",)),
        name="s5_prep",
    )(col(a_re), col(a_im), rowv(a_im), log_dt.reshape(g, 1, 1),
      b_re, b_im, c_re, c_im)
    mi16, mi8, inr16, ini16, inr8, ini8, outr16, outi16, outr8, outi8, a16r, a16i, a8r, a8i = outs
    return {q: dict(m_intra=mi16, m_in_re=inr16, m_in_im=ini16, m_out_re=outr16, m_out_im=outi16,
                    a_re=a16r, a_im=a16i),
            q // 2: dict(m_intra=mi8, m_in_re=inr8, m_in_im=ini8, m_out_re=outr8, m_out_im=outi8,
                         a_re=a8r, a_im=a8i)}


def _lane_block_transpose(a):
    a = list(a)
    blk = lax.broadcasted_iota(jnp.int32, a[0].shape, 1) >> 4
    for d in (4, 2, 1):
        upper = (blk & d) != 0
        for r in range(8):
            if r & d:
                continue
            lo, hi = a[r], a[r + d]
            a[r] = jnp.where(upper, pltpu.roll(hi, 16 * d, 1), lo)
            a[r + d] = jnp.where(upper, hi, pltpu.roll(lo, LANES - 16 * d, 1))
    return a


def _in_proj_kernel(*refs, steps, n_tiles, tiles_per_seq):
    fused = tiles_per_seq is not None
    x_ref, gpre_ref, wa_ref, wb_ref, wgk_ref, bgk_ref = refs[:6]
    if fused:
        (s0_ref, g_ref, u_ref, o_ref, ug_ref, sfin_ref,
         u_scr, gk_scr, q_scr, k_scr, v_scr, s_scr) = refs[6:]
    else:
        q_ref, k_ref, v_ref, g_ref, u_ref, lg_ref, ug_ref, u_scr, gk_scr = refs[6:]
    i = pl.program_id(0)
    tm = x_ref.shape[0]

    def project(slot):
        h = _rms(x_ref[...], gpre_ref[...]).astype(BF16)
        proj = _dot(h, wa_ref[...])
        tail_proj = _dot(h, wb_ref[...])
        if fused:
            q_scr[slot] = proj[:, 0:256]
            k_scr[slot] = proj[:, 256:512]
            v_scr[slot] = proj[:, 512:1024].astype(BF16)
        else:
            q_ref[...] = proj[:, 0:256]
            k_ref[...] = proj[:, 256:512]
            v_ref[...] = proj[:, 512:1024]
        g_ref[...] = proj[:, 1024:1536]
        u = tail_proj[:, :D_S5]
        u_ref[...] = u
        for s in range(D_S5 // LANES):
            u_scr[slot, s] = u[:, s * LANES:(s + 1) * LANES]
        gk_scr[slot] = tail_proj[:, D_S5:]

    def tail(slot):
        z = _dot(gk_scr[slot].astype(BF16), wgk_ref[...]) + bgk_ref[...]
        lg = jax.nn.log_sigmoid(z) * (1.0 / GATE_NORM)
        if fused:
            tile = i - 1
            seq = tile // tiles_per_seq
            state = jnp.where(tile % tiles_per_seq == 0, s0_ref[seq], s_scr[...])
            masks = _gla_masks(1, GLA_CHUNK)
            for c in range(tm // GLA_CHUNK):
                rows = slice(c * GLA_CHUNK, (c + 1) * GLA_CHUNK)
                o, (state,) = _gla_group(q_scr[slot, rows, :], k_scr[slot, rows, :], v_scr[slot, rows, :],
                                         lg[rows], [state], masks, GLA_CHUNK)
                o_ref[rows, :] = o
            s_scr[...] = state
            sfin_ref[seq] = state
        else:
            lg_ref[...] = lg
        nr = tm // steps
        rt = min(nr, 16)
        for s in range(D_S5 // LANES):
            for hf in range(steps // 8):
                for r0 in range(0, nr, rt):
                    a = [u_scr[slot, s, pl.ds(r0 * steps + hf * 8 + t, rt, stride=steps), :] for t in range(8)]
                    per_group = _lane_block_transpose(a)
                    for g in range(SLAB_GROUPS):
                        ug_ref[s * SLAB_GROUPS + g, r0:r0 + rt, hf * LANES:(hf + 1) * LANES] = (
                            per_group[g].astype(BF16))

    @pl.when(i == 0)
    def _():
        if fused:
            s_scr[...] = jnp.zeros(s_scr.shape, F32)
        project(0)

    @pl.when((i > 0) & (i < n_tiles))
    def _():
        tail((i - 1) & 1)
        project(i & 1)

    @pl.when(i == n_tiles)
    def _():
        tail((i - 1) & 1)


def _in_proj(x, gpre, w_a, w_b, wgk_p, bgk, tm, steps, gla_s0=None, seq_len=None):
    t = x.shape[0]
    n = t // tm
    cur = lambda i: jnp.minimum(i, n - 1)
    prev = lambda i: jnp.maximum(i - 1, 0)
    row = lambda w, at: pl.BlockSpec((tm, w), lambda i: (at(i), 0))
    wg = steps * S5_GROUP
    ug_spec = pl.BlockSpec((S5_GROUPS, tm // steps, wg), lambda i: (0, prev(i), 0))
    ug_shape = jax.ShapeDtypeStruct((S5_GROUPS, t // steps, wg), BF16)
    f32 = lambda w: jax.ShapeDtypeStruct((t, w), F32)
    in_specs = [row(D_MODEL, cur), _const_spec((1, D_MODEL)), _const_spec(w_a.shape), _const_spec(w_b.shape),
                _const_spec((LANES, GLA_KDIM)), _const_spec((1, GLA_KDIM))]
    scratch = [pltpu.VMEM((2, D_S5 // LANES, tm, LANES), F32), pltpu.VMEM((2, tm, LANES), F32)]
    args = (x, gpre, w_a, w_b, wgk_p, bgk)
    if gla_s0 is None:
        tiles_per_seq = None
        out_specs = [row(w, cur) for w in (256, 256, 512, 512, 512)] + [row(GLA_KDIM, prev), ug_spec]
        out_shape = [f32(w) for w in (256, 256, 512, 512, 512)] + [f32(GLA_KDIM), ug_shape]
    else:
        tiles_per_seq = seq_len // tm
        assert seq_len % tm == 0 and tm % GLA_CHUNK == 0
        in_specs.append(_const_spec(gla_s0.shape))
        args += (gla_s0,)
        out_specs = [row(D_GLA, cur), row(D_S5, cur), row(D_GLA, prev), ug_spec, _const_spec(gla_s0.shape)]
        out_shape = [f32(D_GLA), f32(D_S5), f32(D_GLA), ug_shape, jax.ShapeDtypeStruct(gla_s0.shape, F32)]
        scratch += [pltpu.VMEM((2, tm, GLA_KDIM), F32), pltpu.VMEM((2, tm, GLA_KDIM), F32),
                    pltpu.VMEM((2, tm, D_GLA), BF16), pltpu.VMEM(gla_s0.shape[1:], F32)]
    return pl.pallas_call(
        functools.partial(_in_proj_kernel, steps=steps, n_tiles=n, tiles_per_seq=tiles_per_seq),
        grid=(n + 1,),
        in_specs=in_specs,
        out_specs=out_specs,
        out_shape=out_shape,
        scratch_shapes=scratch,
        compiler_params=pltpu.CompilerParams(dimension_semantics=("arbitrary",),
                                             vmem_limit_bytes=VMEM_LIMIT),
        name="in_proj" if gla_s0 is None else "in_proj_gla",
    )(*args)


def _gla_masks(sg, chunk):
    r = sg * chunk
    assert r >= LANES or sg == 1
    shift = chunk.bit_length() - 1
    hk = GLA_HEADS * r
    ri = lax.broadcasted_iota(jnp.int32, (r, r), 0)
    ci = lax.broadcasted_iota(jnp.int32, (r, r), 1)
    ri4 = lax.broadcasted_iota(jnp.int32, (r, hk), 0)
    ci4 = lax.broadcasted_iota(jnp.int32, (r, hk), 1) & (r - 1)
    stack_head = lax.broadcasted_iota(jnp.int32, (hk, 1), 0) >> (r.bit_length() - 1)
    k_head = lax.broadcasted_iota(jnp.int32, (1, GLA_KDIM), 1) >> 6
    v_head = lax.broadcasted_iota(jnp.int32, (1, D_GLA), 1) >> 7
    s_head = lax.broadcasted_iota(jnp.int32, (GLA_KDIM, 1), 0) >> 6
    return dict(
        tri_bf=jnp.where(((ri >> shift) == (ci >> shift)) & (ri >= ci), 1.0, 0.0).astype(BF16),
        causal4=((ri4 >> shift) == (ci4 >> shift)) & (ri4 >= ci4),
        k_diag=stack_head == k_head, v_diag=stack_head == v_head, s_diag=s_head == v_head)


def _gla_group(q, k, v_bf, lg, states, m, chunk):
    sg = len(states)
    r = sg * chunk
    tile4 = lambda t: jnp.concatenate([t] * GLA_HEADS, axis=0)
    lg_hi = lg.astype(BF16)
    lg_lo = (lg - lg_hi.astype(F32)).astype(BF16)
    b2 = _dot(m["tri_bf"], jnp.concatenate([lg_hi, lg_lo], axis=1))
    b = b2[:, :GLA_KDIM] + b2[:, GLA_KDIM:]
    lasts = [b[(i + 1) * chunk - 1:(i + 1) * chunk, :] for i in range(sg)]
    bl = jnp.concatenate([jnp.broadcast_to(t, (chunk, GLA_KDIM)) for t in lasts], axis=0)
    qd = (q * (GLA_DK ** -0.5) * jnp.exp(b)).astype(BF16)
    ki = (k * jnp.exp(-b)).astype(BF16)
    ke = k * jnp.exp(bl - b)

    ki_bd = jnp.where(m["k_diag"], tile4(ki), 0.0)
    att = jnp.where(m["causal4"], _dot_nt(qd, ki_bd), 0.0).astype(BF16)
    v_bd = jnp.where(m["v_diag"], tile4(v_bf), 0.0)
    o_intra = _dot(att, v_bd)

    if r < LANES:
        aug_t = jnp.concatenate([ke, jnp.broadcast_to(lasts[0], (LANES - r, GLA_KDIM))], axis=0).T
        ke_t, bl_t = aug_t, aug_t[:, r:]
    else:
        ke_t = ke.T
        bl_t = jnp.concatenate(lasts + [jnp.zeros((r - sg, GLA_KDIM), F32)], axis=0).T
    ke_t = ke_t.astype(BF16)

    outs, new_states = [], []
    for i, s_old in enumerate(states):
        rows = slice(i * chunk, (i + 1) * chunk)
        s_bd = jnp.where(m["s_diag"], jnp.concatenate([s_old.astype(BF16)] * GLA_HEADS, axis=1), 0.0)
        outs.append(o_intra[rows] + _dot(qd[rows], s_bd))
        upd = [_dot(ke_t[h * GLA_DK:(h + 1) * GLA_DK, rows], v_bf[rows, h * GLA_DV:(h + 1) * GLA_DV])
               for h in range(GLA_HEADS)]
        new_states.append(jnp.exp(bl_t[:, i:i + 1]) * s_old + jnp.concatenate(upd, axis=0))
    return (outs[0] if sg == 1 else jnp.concatenate(outs, axis=0)), new_states


def _gla_kernel(q_ref, k_ref, v_ref, lg_ref, s0_ref, o_ref, s_ref, *, bb, sg, chunk):
    r = sg * chunk

    @pl.when(pl.program_id(1) == 0)
    def _():
        s_ref[...] = s0_ref[...]

    masks = _gla_masks(sg, chunk)
    for gi in range(bb // sg):
        seqs = slice(gi * sg, (gi + 1) * sg)
        o, new = _gla_group(q_ref[seqs].reshape(r, GLA_KDIM), k_ref[seqs].reshape(r, GLA_KDIM),
                            v_ref[seqs].reshape(r, D_GLA).astype(BF16), lg_ref[seqs].reshape(r, GLA_KDIM),
                            [s_ref[gi * sg + i] for i in range(sg)], masks, chunk)
        o_ref[seqs] = o.reshape(sg, chunk, D_GLA)
        for i in range(sg):
            s_ref[gi * sg + i] = new[i]


def _gla(q, k, v, lg, s0, bb, sg, chunk):
    b, l, _ = q.shape
    blk = lambda w: pl.BlockSpec((bb, chunk, w), lambda i, c: (i, c, 0))
    sspec = pl.BlockSpec((bb, GLA_KDIM, GLA_DV), lambda i, c: (i, 0, 0))
    return pl.pallas_call(
        functools.partial(_gla_kernel, bb=bb, sg=sg, chunk=chunk),
        grid=(b // bb, l // chunk),
        in_specs=[blk(GLA_KDIM), blk(GLA_KDIM), blk(D_GLA), blk(GLA_KDIM), sspec],
        out_specs=[blk(D_GLA), sspec],
        out_shape=[jax.ShapeDtypeStruct((b, l, D_GLA), F32),
                   jax.ShapeDtypeStruct((b, GLA_KDIM, GLA_DV), F32)],
        compiler_params=pltpu.CompilerParams(dimension_semantics=("parallel", "arbitrary"),
                                             vmem_limit_bytes=VMEM_LIMIT),
        name=f"gla_c{chunk}",
    )(q, k, v, lg, s0)


def _s5_kernel(ug_ref, mi_ref, minr_ref, mini_ref, mor_ref, moi_ref, ar_ref, ai_ref,
               h0r_ref, h0i_ref, yg_ref, hfr_ref, hfi_ref, vr_s, vi_s, hr_s, hi_s, str_s, sti_s, *, nb, cb):
    gs = SLAB_GROUPS // 2
    n_pairs = S5_GROUPS // 2
    w = ug_ref.shape[-1]
    rows = nb * cb
    batch = cb if nb == 1 else nb
    n_blocks = rows // batch
    interleave = nb > 1 and cb > 1
    first_pair = pl.program_id(0) * gs
    seq0 = pl.program_id(1) * batch if nb == 1 else 0

    def state_rows(p):
        return pl.ds(seq0 * n_pairs + first_pair + p, batch, stride=n_pairs)

    def load_state():
        for p in range(gs):
            str_s[p] = h0r_ref[state_rows(p), :]
            sti_s[p] = h0i_ref[state_rows(p), :]

    if nb == 1:
        load_state()
    else:
        pl.when(pl.program_id(1) == 0)(load_state)

    def to_scan_order(ref, g, val):
        if not interleave:
            ref[g] = val
        else:
            for b in range(nb):
                ref[g, pl.ds(b, cb, stride=nb), :] = val[b * cb:(b + 1) * cb]

    def from_scan_order(ref, g):
        if not interleave:
            return ref[g]
        return jnp.concatenate([ref[g, pl.ds(b, cb, stride=nb), :] for b in range(nb)], axis=0)

    ub = [ug_ref[g].reshape(rows, w) for g in range(2 * gs)]
    for g in range(gs):
        pair_u = jnp.concatenate([ub[2 * g], ub[2 * g + 1]], axis=1)
        to_scan_order(vr_s, g, _dot(pair_u, minr_ref[g]))
        to_scan_order(vi_s, g, _dot(pair_u, mini_ref[g]))
    ar = [ar_ref[g] for g in range(gs)]
    ai = [ai_ref[g] for g in range(gs)]

    def body(c, carry):
        rows = pl.ds(pl.multiple_of(c * batch, batch), batch)
        new = []
        for g in range(gs):
            hr, hi = carry[2 * g], carry[2 * g + 1]
            hr_s[g, rows, :] = hr
            hi_s[g, rows, :] = hi
            new.append(ar[g] * hr - ai[g] * hi + vr_s[g, rows, :])
            new.append(ar[g] * hi + ai[g] * hr + vi_s[g, rows, :])
        return tuple(new)

    init = tuple(ref[g] for g in range(gs) for ref in (str_s, sti_s))
    fin = lax.fori_loop(0, n_blocks, body, init)
    for g in range(gs):
        str_s[g] = fin[2 * g]
        sti_s[g] = fin[2 * g + 1]
        hfr_ref[state_rows(g), :] = fin[2 * g]
        hfi_ref[state_rows(g), :] = fin[2 * g + 1]
        y_state = (_dot(from_scan_order(hr_s, g).astype(BF16), mor_ref[g])
                   + _dot(from_scan_order(hi_s, g).astype(BF16), moi_ref[g]))
        for k in range(2):
            y = _dot(ub[2 * g + k], mi_ref[2 * g + k]) + y_state[:, k * w:(k + 1) * w]
            yg_ref[2 * g + k] = y.reshape(yg_ref.shape[1:])


def _s5(ug, m, h0_re, h0_im, nb, cb):
    g_all, r, w = ug.shape
    nc = r // nb
    batch = cb if nb == 1 else nb
    n, gs = LANES, SLAB_GROUPS
    uspec = pl.BlockSpec((gs, nb, cb, w), lambda s, i: (s, 0, i, 0))
    per_g = lambda a, b_: pl.BlockSpec((gs, a, b_), lambda s, i: (s, 0, 0))
    per_p = lambda a, b_: pl.BlockSpec((gs // 2, a, b_), lambda s, i: (s, 0, 0))
    hspec = _const_spec(h0_re.shape)
    yg, hf_re, hf_im = pl.pallas_call(
        functools.partial(_s5_kernel, nb=nb, cb=cb),
        grid=(g_all // gs, nc // cb),
        in_specs=[uspec, per_g(w, w), per_p(2 * w, n), per_p(2 * w, n), per_p(n, 2 * w), per_p(n, 2 * w),
                  per_p(1, n), per_p(1, n), hspec, hspec],
        out_specs=[uspec, hspec, hspec],
        out_shape=[jax.ShapeDtypeStruct((g_all, nb, nc, w), F32),
                   jax.ShapeDtypeStruct(h0_re.shape, F32),
                   jax.ShapeDtypeStruct(h0_im.shape, F32)],
        scratch_shapes=[pltpu.VMEM((gs // 2, nb * cb, n), F32)] * 4 + [pltpu.VMEM((gs // 2, batch, n), F32)] * 2,
        compiler_params=pltpu.CompilerParams(dimension_semantics=("arbitrary", "arbitrary"),
                                             vmem_limit_bytes=VMEM_LIMIT),
        name=f"s5_w{w}",
    )(ug.reshape(g_all, nb, nc, w), m["m_intra"], m["m_in_re"], m["m_in_im"], m["m_out_re"], m["m_out_im"],
      m["a_re"], m["a_im"], h0_re, h0_im)
    return yg.reshape(g_all, r, w), hf_re, hf_im


def _mix_ffn_kernel(x_ref, o_ref, g_ref, yg_ref, u_ref, dsk_ref, gn_ref, wglu_ref, s5n_ref, wo_ref, gpost_ref,
                    gpre_ref, wg_ref, wu_ref, wd_ref, gpostf_ref, out_ref, y_scr, x1_scr, *, steps):
    i = pl.program_id(0)

    def mix_into(slot):
        nr = x_ref.shape[0] // steps
        rt = min(nr, 8)
        for s in range(D_S5 // LANES):
            for hf in range(steps // 8):
                for r0 in range(0, nr, rt):
                    a = [yg_ref[s * SLAB_GROUPS + g, r0:r0 + rt, hf * LANES:(hf + 1) * LANES]
                         for g in range(SLAB_GROUPS)]
                    per_step = _lane_block_transpose(a)
                    for t in range(8):
                        y_scr[s, pl.ds(r0 * steps + hf * 8 + t, rt, stride=steps), :] = per_step[t]
        y5 = jnp.concatenate([y_scr[s] for s in range(D_S5 // LANES)], axis=1) + dsk_ref[...] * u_ref[...]
        o = o_ref[...]
        gn = gn_ref[...]
        heads = []
        for h in range(GLA_HEADS):
            heads.append(_rms(o[:, h * GLA_DV:(h + 1) * GLA_DV], gn))
        og = jnp.concatenate(heads, axis=1) * jax.nn.silu(g_ref[...])
        y = jax.nn.gelu(y5)
        y = y * jax.nn.sigmoid(_dot(y.astype(BF16), wglu_ref[...]))
        y = _rms(y, s5n_ref[...])
        mix = _dot(og.astype(BF16), wo_ref[:D_GLA, :]) + _dot(y.astype(BF16), wo_ref[D_GLA:, :])
        x1_scr[slot] = x_ref[...] + _rms(mix, gpost_ref[...])

    def ffn_from(slot):
        x = x1_scr[slot]
        h = _rms(x, gpre_ref[...]).astype(BF16)
        acc = jnp.zeros(x.shape, F32)
        for c in range(D_FF // FF_CHUNK):
            cols = slice(c * FF_CHUNK, (c + 1) * FF_CHUNK)
            act = jax.nn.silu(_dot(h, wg_ref[:, cols])) * _dot(h, wu_ref[:, cols])
            acc = acc + _dot(act.astype(BF16), wd_ref[cols, :])
        out_ref[...] = x + _rms(acc, gpostf_ref[...])

    @pl.when(i == 0)
    def _():
        mix_into(0)

    @pl.when(i > 0)
    def _():
        ffn_from((i - 1) & 1)
        mix_into(i & 1)


def _mix_ffn(x, o, g, yg, u, dsk, gn, wglu, s5n, wo, gpost, gpre, wg, wu, wd, gpostf, tm, steps):
    t = x.shape[0]
    n = t // tm
    cur = lambda i: jnp.minimum(i, n - 1)
    row = lambda w: pl.BlockSpec((tm, w), lambda i: (cur(i), 0))
    fixed = lambda shape: pl.BlockSpec(shape, lambda i: (0,) * len(shape), pipeline_mode=pl.Buffered(1))
    return pl.pallas_call(
        functools.partial(_mix_ffn_kernel, steps=steps),
        grid=(n + 1,),
        in_specs=[row(D_MODEL), row(D_GLA), row(D_GLA),
                  pl.BlockSpec((S5_GROUPS, tm // steps, steps * S5_GROUP), lambda i: (0, cur(i), 0)),
                  row(D_S5), fixed((1, D_S5)),
                  fixed((1, GLA_DV)), fixed((D_S5, D_S5)), fixed((1, D_S5)),
                  fixed((D_GLA + D_S5, D_MODEL)), fixed((1, D_MODEL)),
                  fixed((1, D_MODEL)), fixed((D_MODEL, D_FF)), fixed((D_MODEL, D_FF)),
                  fixed((D_FF, D_MODEL)), fixed((1, D_MODEL))],
        out_specs=pl.BlockSpec((tm, D_MODEL), lambda i: (jnp.maximum(i - 1, 0), 0)),
        out_shape=jax.ShapeDtypeStruct((t, D_MODEL), F32),
        scratch_shapes=[pltpu.VMEM((D_S5 // LANES, tm, LANES), F32), pltpu.VMEM((2, tm, D_MODEL), F32)],
        compiler_params=pltpu.CompilerParams(dimension_semantics=("arbitrary",),
                                             vmem_limit_bytes=VMEM_LIMIT),
        name="mix_ffn",
    )(x, o, g, yg, u, dsk, gn, wglu, s5n, wo, gpost, gpre, wg, wu, wd, gpostf)


def kernel(x_prompt, x_sample, state_gla, state_s5_re, state_s5_im, meta_tokens, g_pre_mix, w_in, w_gk2, b_gk, gla_norm, s5_a_re, s5_a_im, s5_b_re, s5_b_im, s5_c_re, s5_c_im, s5_d, s5_log_dt, w_s5_glu, s5_norm, w_o, g_post_mix, g_pre_ffn, w_gate, w_up, w_down, g_post_ffn):
    assert g_pre_mix.shape[0] == 1, "single-layer step"
    bp, seq_p, _ = x_prompt.shape
    bs, seq_s, _ = x_sample.shape
    row = lambda t: t[0].reshape(1, -1)

    w = w_in[0]
    c3, c4 = 2 * GLA_KDIM + 2 * D_GLA, 2 * GLA_KDIM + 2 * D_GLA + GATE_RANK
    w_a = w[:, :c3].astype(BF16)
    w_b = jnp.concatenate([w[:, c4:].astype(BF16), w[:, c3:c4].astype(BF16),
                           jnp.zeros((D_MODEL, LANES - GATE_RANK), BF16)], axis=1)
    wgk_p = jnp.concatenate([w_gk2[0], jnp.zeros((LANES - GATE_RANK, GLA_KDIM), F32)], axis=0).astype(BF16)
    wo_bf = w_o[0].astype(BF16)
    s5m = _s5_prep(s5_a_re[0], s5_a_im[0], s5_b_re[0], s5_b_im[0], s5_c_re[0], s5_c_im[0], s5_log_dt[0])
    proj_w = (row(g_pre_mix), w_a, w_b, wgk_p, row(b_gk))

    def finish(x, o, g, yg, u, tm, steps):
        return _mix_ffn(x, o, g, yg, u, row(s5_d), row(gla_norm), w_s5_glu[0].astype(BF16), row(s5_norm),
                        wo_bf, row(g_post_mix), row(g_pre_ffn), w_gate[0].astype(BF16),
                        w_up[0].astype(BF16), w_down[0].astype(BF16), row(g_post_ffn), tm, steps)

    xm = jnp.broadcast_to(meta_tokens[None], (bp, N_META, D_MODEL)).reshape(bp * N_META, D_MODEL)
    q, k, v, _, _, lg, ug = _in_proj(xm, *proj_w, bp * N_META, S5_BLOCK)
    r3 = lambda t, b, l: t.reshape(b, l, t.shape[-1])
    _, s_meta = _gla(r3(q, bp, N_META), r3(k, bp, N_META), r3(v, bp, N_META), r3(lg, bp, N_META),
                     jnp.zeros((bp, GLA_KDIM, GLA_DV), F32), bp, bp, N_META)
    zh = jnp.zeros((bp * S5_GROUPS // 2, LANES), F32)
    _, hm_re, hm_im = _s5(ug, s5m[S5_BLOCK], zh, zh, 1, bp)

    xp = x_prompt.reshape(bp * seq_p, D_MODEL)
    g, u, o, ug, s_p = _in_proj(xp, *proj_w, TOKEN_TILE, S5_BLOCK, gla_s0=s_meta, seq_len=seq_p)
    yg, hp_re, hp_im = _s5(ug, s5m[S5_BLOCK], hm_re, hm_im, bp, S5_ROW_BLOCKS)
    y_prompt = finish(xp, o, g, yg, u, TOKEN_TILE, S5_BLOCK)

    xs = x_sample.reshape(bs * seq_s, D_MODEL)
    to_g = lambda t: t[0].reshape(bs * S5_GROUPS // 2, LANES)
    q, k, v, g, u, lg, ug = _in_proj(xs, *proj_w, TOKEN_TILE, seq_s)
    o, s_s = _gla(r3(q, bs, seq_s), r3(k, bs, seq_s), r3(v, bs, seq_s), r3(lg, bs, seq_s),
                  state_gla[0].reshape(bs, GLA_KDIM, GLA_DV), GLA_SAMPLE_SEQS, GLA_SAMPLE_GROUP, seq_s)
    yg, hs_re, hs_im = _s5(ug, s5m[seq_s], to_g(state_s5_re), to_g(state_s5_im), 1, bs)
    y_sample = finish(xs, o.reshape(bs * seq_s, D_GLA), g, yg, u, TOKEN_TILE, seq_s)

    gla_out = lambda s, b: s.reshape(1, b, GLA_HEADS, GLA_DK, GLA_DV)
    s5_out = lambda h: h.reshape(1, -1, S5_GROUPS, S5_STATE)
    return (y_prompt.reshape(bp, seq_p, D_MODEL), y_sample.reshape(bs, seq_s, D_MODEL),
            gla_out(s_p, bp), s5_out(hp_re), s5_out(hp_im),
            gla_out(s_s, bs), s5_out(hs_re), s5_out(hs_im))
```

```python
import functools

import jax
import jax.numpy as jnp
from jax import lax
from jax.experimental import pallas as pl
from jax.experimental.pallas import tpu as pltpu

F32 = jnp.float32
BF16 = jnp.bfloat16

D_MODEL = 1024
D_GLA = 512
GLA_HEADS = 4
GLA_DV = 128
GLA_DK = 64
GLA_KDIM = 256
GATE_RANK = 16
GATE_NORM = 16.0
GLA_CHUNK = 64
D_S5 = 512
S5_GROUP = 16
S5_GROUPS = 32
S5_STATE = 64
N_META = 16
D_FF = 2816
EPS = 1e-6
LANES = 128
S5_BLOCK = 16
SLAB_GROUPS = LANES // S5_GROUP
FF_CHUNK = 256
TOKEN_TILE = 512
S5_ROW_BLOCKS = 32
GLA_SAMPLE_GROUP = 16
GLA_SAMPLE_SEQS = 32
VMEM_LIMIT = 48 * 1024 * 1024


def _rms(x, g):
    return x * lax.rsqrt(jnp.mean(x * x, axis=-1, keepdims=True) + EPS) * g


def _dot(a, b):
    return jnp.dot(a, b, preferred_element_type=F32)


def _dot_nt(a, b):
    return lax.dot_general(a, b, (((1,), (1,)), ((), ())), preferred_element_type=F32)


def _dot_tn(a, b):
    return lax.dot_general(a, b, (((0,), (0,)), ((), ())), preferred_element_type=F32)


def _const_spec(shape):
    zeros = (0,) * len(shape)
    return pl.BlockSpec(shape, lambda *_: zeros)


PREP_GROUPS = 8


def _cmul(a, b):
    return a[0] * b[0] - a[1] * b[1], a[0] * b[1] + a[1] * b[0]


def _unit_powers(c1, s1, expo, n_bits):
    acc = (jnp.ones_like(c1), jnp.zeros_like(c1))
    base = (c1, s1)
    squares = [base]
    for bit in range(n_bits):
        take = ((expo >> bit) & 1) == 1
        nxt = _cmul(acc, base)
        acc = (jnp.where(take, nxt[0], acc[0]), jnp.where(take, nxt[1], acc[1]))
        base = _cmul(base, base)
        squares.append(base)
    return acc, squares


def _s5_prep_kernel(arc_ref, aic_ref, air_ref, ldt_ref, br_ref, bi_ref, cr_ref, ci_ref,
                    mi16_ref, mi8_ref, inr16_ref, ini16_ref, inr8_ref, ini8_ref,
                    outr16_ref, outi16_ref, outr8_ref, outi8_ref, a16r_ref, a16i_ref, a8r_ref, a8i_ref):
    n, q = S5_STATE, S5_BLOCK
    w = q * S5_GROUP
    hp = lax.Precision.HIGHEST
    lane = lax.broadcasted_iota(jnp.int32, (n, w), 1)
    t_blk = lane >> 4
    eye = lax.broadcasted_iota(jnp.int32, (n, n), 0) == lax.broadcasted_iota(jnp.int32, (n, n), 1)
    to_col = lambda r: jnp.sum(jnp.where(eye, r, 0.0), axis=1, keepdims=True)
    zrows = lambda x: jnp.concatenate([x, jnp.zeros_like(x)], axis=0)
    lane_n = lax.broadcasted_iota(jnp.int32, (n, LANES), 1)
    rep = jnp.where((lax.broadcasted_iota(jnp.int32, (S5_GROUP, w), 1) & (S5_GROUP - 1))
                    == lax.broadcasted_iota(jnp.int32, (S5_GROUP, w), 0), 1.0, 0.0)
    mo_g, rev_g, hop_g = [], [], []
    for g in range(PREP_GROUPS):
        dt = jnp.exp(ldt_ref[g])
        ang_r = air_ref[g] * dt
        c1, s1 = to_col(jnp.cos(ang_r)), to_col(jnp.sin(ang_r))
        lam_re = jnp.minimum(arc_ref[g], -1e-4)
        lam_im = aic_ref[g]
        unit, squares = _unit_powers(c1, s1, t_blk, 4)
        pm = jnp.exp(t_blk.astype(F32) * (lam_re * dt))
        pk = (pm * unit[0], pm * unit[1])
        mag = jnp.exp(lam_re * dt)
        ab = (mag * c1, mag * s1)
        p1 = _cmul(pk, ab)
        ct = tuple(lax.dot_general(r[g], rep, (((0,), (0,)), ((), ())), precision=hp,
                                   preferred_element_type=F32) for r in (cr_ref, ci_ref))
        bt = tuple(jnp.dot(r[g], rep, precision=hp, preferred_element_type=F32) for r in (br_ref, bi_ref))
        g0 = _cmul(ct, pk)
        mo = _cmul(ct, p1)
        mo_g.append((mo[0], -mo[1]))
        den = lam_re * lam_re + lam_im * lam_im
        nr, ni = ab[0] - 1.0, ab[1]
        f = ((nr * lam_re + ni * lam_im) / den, (ni * lam_re - nr * lam_im) / den)
        e = _cmul(pk, _cmul(f, bt))
        et = [zrows(x).T for x in e]
        rev_g.append([jnp.concatenate([x[(q - 1 - s) * S5_GROUP:(q - s) * S5_GROUP] for s in range(q)], axis=0)
                      for x in et])
        hops = []
        for steps in (q, q // 2):
            m = jnp.exp(float(steps) * (lam_re * dt))
            u = squares[steps.bit_length() - 1]
            hops += [m * u[0], m * u[1]]
        cols = jnp.zeros((n, LANES), F32)
        for idx, hcol in enumerate(hops):
            cols = jnp.where(lane_n == idx, hcol, cols)
        hop_g.append(zrows(cols).T)
        t0 = (jnp.dot(et[0][:S5_GROUP, :n], g0[0], precision=hp, preferred_element_type=F32)
              - jnp.dot(et[1][:S5_GROUP, :n], g0[1], precision=hp, preferred_element_type=F32))
        lane_t = lax.broadcasted_iota(jnp.int32, t0.shape, 1) >> 4
        for s in range(q):
            blk = t0 if s == 0 else jnp.where(lane_t >= s, pltpu.roll(t0, S5_GROUP * s, 1), 0.0)
            mi16_ref[g, s * S5_GROUP:(s + 1) * S5_GROUP, :] = blk.astype(BF16)
            if s < q // 2:
                mi8_ref[g, s * S5_GROUP:(s + 1) * S5_GROUP, :] = blk[:, :w // 2].astype(BF16)

    half = lambda x: pltpu.roll(x, n, 1)
    for p in range(PREP_GROUPS // 2):
        g0, g1 = 2 * p, 2 * p + 1
        for c, (in16, in8, out16, out8) in enumerate(((inr16_ref, inr8_ref, outr16_ref, outr8_ref),
                                                      (ini16_ref, ini8_ref, outi16_ref, outi8_ref))):
            lo, hi = rev_g[g0][c], half(rev_g[g1][c])
            in16[p] = jnp.concatenate([lo, hi], axis=0).astype(BF16)
            in8[p] = jnp.concatenate([lo[w // 2:], hi[w // 2:]], axis=0).astype(BF16)
            m0, m1 = mo_g[g0][c], mo_g[g1][c]
            z, zh = jnp.zeros_like(m0), jnp.zeros((n, w // 2), F32)
            out16[p] = jnp.concatenate([jnp.concatenate([m0, z], axis=1),
                                        jnp.concatenate([z, m1], axis=1)], axis=0).astype(BF16)
            out8[p] = jnp.concatenate([jnp.concatenate([m0[:, :w // 2], zh], axis=1),
                                       jnp.concatenate([zh, m1[:, :w // 2]], axis=1)], axis=0).astype(BF16)
        hops = hop_g[g0] + half(hop_g[g1])
        a16r_ref[p] = hops[0:1]
        a16i_ref[p] = hops[1:2]
        a8r_ref[p] = hops[2:3]
        a8i_ref[p] = hops[3:4]


def _s5_prep(a_re, a_im, b_re, b_im, c_re, c_im, log_dt):
    g, n, j, q = S5_GROUPS, S5_STATE, S5_GROUP, S5_BLOCK
    w, h, pg = q * j, q * j // 2, PREP_GROUPS
    col = lambda t: t.reshape(g, n, 1)
    rowv = lambda t: t.reshape(g, 1, n)
    spec = lambda a, b_: pl.BlockSpec((pg, a, b_), lambda i: (i, 0, 0))
    pspec = lambda a, b_: pl.BlockSpec((pg // 2, a, b_), lambda i: (i, 0, 0))
    pair_shapes = [(2 * w, LANES), (2 * w, LANES), (2 * h, LANES), (2 * h, LANES),
                   (LANES, 2 * w), (LANES, 2 * w), (LANES, 2 * h), (LANES, 2 * h)]
    outs = pl.pallas_call(
        _s5_prep_kernel,
        grid=(g // pg,),
        in_specs=[spec(n, 1), spec(n, 1), spec(1, n), spec(1, 1),
                  spec(n, j), spec(n, j), spec(j, n), spec(j, n)],
        out_specs=[spec(w, w), spec(h, h)] + [pspec(*s) for s in pair_shapes] + [pspec(1, LANES)] * 4,
        out_shape=[jax.ShapeDtypeStruct((g, w, w), BF16), jax.ShapeDtypeStruct((g, h, h), BF16)]
                  + [jax.ShapeDtypeStruct((g // 2,) + s, BF16) for s in pair_shapes]
                  + [jax.ShapeDtypeStruct((g // 2, 1, LANES), F32)] * 4,
        compiler_params=pltpu.CompilerParams(dimension_semantics=("parallel",)),
        name="s5_prep",
    )(col(a_re), col(a_im), rowv(a_im), log_dt.reshape(g, 1, 1),
      b_re, b_im, c_re, c_im)
    mi16, mi8, inr16, ini16, inr8, ini8, outr16, outi16, outr8, outi8, a16r, a16i, a8r, a8i = outs
    return {q: dict(m_intra=mi16, m_in_re=inr16, m_in_im=ini16, m_out_re=outr16, m_out_im=outi16,
                    a_re=a16r, a_im=a16i),
            q // 2: dict(m_intra=mi8, m_in_re=inr8, m_in_im=ini8, m_out_re=outr8, m_out_im=outi8,
                         a_re=a8r, a_im=a8i)}


def _lane_block_transpose(a):
    a = list(a)
    blk = lax.broadcasted_iota(jnp.int32, a[0].shape, 1) >> 4
    for d in (4, 2, 1):
        upper = (blk & d) != 0
        for r in range(8):
            if r & d:
                continue
            lo, hi = a[r], a[r + d]
            a[r] = jnp.where(upper, pltpu.roll(hi, 16 * d, 1), lo)
            a[r + d] = jnp.where(upper, hi, pltpu.roll(lo, LANES - 16 * d, 1))
    return a


def _in_proj_kernel(*refs, steps, n_tiles, tiles_per_seq):
    fused = tiles_per_seq is not None
    x_ref, gpre_ref, wa_ref, wb_ref, wgk_ref, bgk_ref = refs[:6]
    if fused:
        (s0_ref, g_ref, u_ref, o_ref, ug_ref, sfin_ref,
         u_scr, gk_scr, q_scr, k_scr, v_scr, s_scr) = refs[6:]
    else:
        q_ref, k_ref, v_ref, g_ref, u_ref, lg_ref, ug_ref, u_scr, gk_scr = refs[6:]
    i = pl.program_id(0)
    tm = x_ref.shape[0]

    def project(slot):
        h = _rms(x_ref[...], gpre_ref[...]).astype(BF16)
        proj = _dot(h, wa_ref[...])
        tail_proj = _dot(h, wb_ref[...])
        if fused:
            q_scr[slot] = proj[:, 0:256]
            k_scr[slot] = proj[:, 256:512]
            v_scr[slot] = proj[:, 512:1024].astype(BF16)
        else:
            q_ref[...] = proj[:, 0:256]
            k_ref[...] = proj[:, 256:512]
            v_ref[...] = proj[:, 512:1024]
        g_ref[...] = proj[:, 1024:1536]
        u = tail_proj[:, :D_S5]
        u_ref[...] = u
        for s in range(D_S5 // LANES):
            u_scr[slot, s] = u[:, s * LANES:(s + 1) * LANES]
        gk_scr[slot] = tail_proj[:, D_S5:]

    def tail(slot):
        z = _dot(gk_scr[slot].astype(BF16), wgk_ref[...]) + bgk_ref[...]
        lg = jax.nn.log_sigmoid(z) * (1.0 / GATE_NORM)
        if fused:
            tile = i - 1
            seq = tile // tiles_per_seq
            state = jnp.where(tile % tiles_per_seq == 0, s0_ref[seq], s_scr[...])
            masks = _gla_masks(1, GLA_CHUNK)
            for c in range(tm // GLA_CHUNK):
                rows = slice(c * GLA_CHUNK, (c + 1) * GLA_CHUNK)
                o, (state,) = _gla_group(q_scr[slot, rows, :], k_scr[slot, rows, :], v_scr[slot, rows, :],
                                         lg[rows], [state], masks, GLA_CHUNK)
                o_ref[rows, :] = o
            s_scr[...] = state
            sfin_ref[seq] = state
        else:
            lg_ref[...] = lg
        nr = tm // steps
        rt = min(nr, 16)
        for s in range(D_S5 // LANES):
            for hf in range(steps // 8):
                for r0 in range(0, nr, rt):
                    a = [u_scr[slot, s, pl.ds(r0 * steps + hf * 8 + t, rt, stride=steps), :] for t in range(8)]
                    per_group = _lane_block_transpose(a)
                    for g in range(SLAB_GROUPS):
                        ug_ref[s * SLAB_GROUPS + g, r0:r0 + rt, hf * LANES:(hf + 1) * LANES] = (
                            per_group[g].astype(BF16))

    @pl.when(i == 0)
    def _():
        if fused:
            s_scr[...] = jnp.zeros(s_scr.shape, F32)
        project(0)

    @pl.when((i > 0) & (i < n_tiles))
    def _():
        tail((i - 1) & 1)
        project(i & 1)

    @pl.when(i == n_tiles)
    def _():
        tail((i - 1) & 1)


def _in_proj(x, gpre, w_a, w_b, wgk_p, bgk, tm, steps, gla_s0=None, seq_len=None):
    t = x.shape[0]
    n = t // tm
    cur = lambda i: jnp.minimum(i, n - 1)
    prev = lambda i: jnp.maximum(i - 1, 0)
    row = lambda w, at: pl.BlockSpec((tm, w), lambda i: (at(i), 0))
    wg = steps * S5_GROUP
    ug_spec = pl.BlockSpec((S5_GROUPS, tm // steps, wg), lambda i: (0, prev(i), 0))
    ug_shape = jax.ShapeDtypeStruct((S5_GROUPS, t // steps, wg), BF16)
    f32 = lambda w: jax.ShapeDtypeStruct((t, w), F32)
    in_specs = [row(D_MODEL, cur), _const_spec((1, D_MODEL)), _const_spec(w_a.shape), _const_spec(w_b.shape),
                _const_spec((LANES, GLA_KDIM)), _const_spec((1, GLA_KDIM))]
    scratch = [pltpu.VMEM((2, D_S5 // LANES, tm, LANES), F32), pltpu.VMEM((2, tm, LANES), F32)]
    args = (x, gpre, w_a, w_b, wgk_p, bgk)
    if gla_s0 is None:
        tiles_per_seq = None
        out_specs = [row(w, cur) for w in (256, 256, 512, 512, 512)] + [row(GLA_KDIM, prev), ug_spec]
        out_shape = [f32(w) for w in (256, 256, 512, 512, 512)] + [f32(GLA_KDIM), ug_shape]
    else:
        tiles_per_seq = seq_len // tm
        assert seq_len % tm == 0 and tm % GLA_CHUNK == 0
        in_specs.append(_const_spec(gla_s0.shape))
        args += (gla_s0,)
        out_specs = [row(D_GLA, cur), row(D_S5, cur), row(D_GLA, prev), ug_spec, _const_spec(gla_s0.shape)]
        out_shape = [f32(D_GLA), f32(D_S5), f32(D_GLA), ug_shape, jax.ShapeDtypeStruct(gla_s0.shape, F32)]
        scratch += [pltpu.VMEM((2, tm, GLA_KDIM), F32), pltpu.VMEM((2, tm, GLA_KDIM), F32),
                    pltpu.VMEM((2, tm, D_GLA), BF16), pltpu.VMEM(gla_s0.shape[1:], F32)]
    return pl.pallas_call(
        functools.partial(_in_proj_kernel, steps=steps, n_tiles=n, tiles_per_seq=tiles_per_seq),
        grid=(n + 1,),
        in_specs=in_specs,
        out_specs=out_specs,
        out_shape=out_shape,
        scratch_shapes=scratch,
        compiler_params=pltpu.CompilerParams(dimension_semantics=("arbitrary",),
                                             vmem_limit_bytes=VMEM_LIMIT),
        name="in_proj" if gla_s0 is None else "in_proj_gla",
    )(*args)


def _gla_masks(sg, chunk):
    r = sg * chunk
    assert r >= LANES or sg == 1
    shift = chunk.bit_length() - 1
    hk = GLA_HEADS * r
    ri = lax.broadcasted_iota(jnp.int32, (r, r), 0)
    ci = lax.broadcasted_iota(jnp.int32, (r, r), 1)
    ri4 = lax.broadcasted_iota(jnp.int32, (r, hk), 0)
    ci4 = lax.broadcasted_iota(jnp.int32, (r, hk), 1) & (r - 1)
    stack_head = lax.broadcasted_iota(jnp.int32, (hk, 1), 0) >> (r.bit_length() - 1)
    k_head = lax.broadcasted_iota(jnp.int32, (1, GLA_KDIM), 1) >> 6
    v_head = lax.broadcasted_iota(jnp.int32, (1, D_GLA), 1) >> 7
    s_head = lax.broadcasted_iota(jnp.int32, (GLA_KDIM, 1), 0) >> 6
    return dict(
        tri_bf=jnp.where(((ri >> shift) == (ci >> shift)) & (ri >= ci), 1.0, 0.0).astype(BF16),
        causal4=((ri4 >> shift) == (ci4 >> shift)) & (ri4 >= ci4),
        k_diag=stack_head == k_head, v_diag=stack_head == v_head, s_diag=s_head == v_head)


def _gla_group(q, k, v_bf, lg, states, m, chunk):
    sg = len(states)
    r = sg * chunk
    tile4 = lambda t: jnp.concatenate([t] * GLA_HEADS, axis=0)
    lg_hi = lg.astype(BF16)
    lg_lo = (lg - lg_hi.astype(F32)).astype(BF16)
    b2 = _dot(m["tri_bf"], jnp.concatenate([lg_hi, lg_lo], axis=1))
    b = b2[:, :GLA_KDIM] + b2[:, GLA_KDIM:]
    lasts = [b[(i + 1) * chunk - 1:(i + 1) * chunk, :] for i in range(sg)]
    bl = jnp.concatenate([jnp.broadcast_to(t, (chunk, GLA_KDIM)) for t in lasts], axis=0)
    qd = (q * (GLA_DK ** -0.5) * jnp.exp(b)).astype(BF16)
    ki = (k * jnp.exp(-b)).astype(BF16)
    ke = k * jnp.exp(bl - b)

    ki_bd = jnp.where(m["k_diag"], tile4(ki), 0.0)
    att = jnp.where(m["causal4"], _dot_nt(qd, ki_bd), 0.0).astype(BF16)
    v_bd = jnp.where(m["v_diag"], tile4(v_bf), 0.0)
    o_intra = _dot(att, v_bd)

    if r < LANES:
        aug_t = jnp.concatenate([ke, jnp.broadcast_to(lasts[0], (LANES - r, GLA_KDIM))], axis=0).T
        ke_t, bl_t = aug_t, aug_t[:, r:]
    else:
        ke_t = ke.T
        bl_t = jnp.concatenate(lasts + [jnp.zeros((r - sg, GLA_KDIM), F32)], axis=0).T
    ke_t = ke_t.astype(BF16)

    outs, new_states = [], []
    for i, s_old in enumerate(states):
        rows = slice(i * chunk, (i + 1) * chunk)
        s_bd = jnp.where(m["s_diag"], jnp.concatenate([s_old.astype(BF16)] * GLA_HEADS, axis=1), 0.0)
        outs.append(o_intra[rows] + _dot(qd[rows], s_bd))
        upd = [_dot(ke_t[h * GLA_DK:(h + 1) * GLA_DK, rows], v_bf[rows, h * GLA_DV:(h + 1) * GLA_DV])
               for h in range(GLA_HEADS)]
        new_states.append(jnp.exp(bl_t[:, i:i + 1]) * s_old + jnp.concatenate(upd, axis=0))
    return (outs[0] if sg == 1 else jnp.concatenate(outs, axis=0)), new_states


def _gla_kernel(q_ref, k_ref, v_ref, lg_ref, s0_ref, o_ref, s_ref, *, bb, sg, chunk):
    r = sg * chunk

    @pl.when(pl.program_id(1) == 0)
    def _():
        s_ref[...] = s0_ref[...]

    masks = _gla_masks(sg, chunk)
    for gi in range(bb // sg):
        seqs = slice(gi * sg, (gi + 1) * sg)
        o, new = _gla_group(q_ref[seqs].reshape(r, GLA_KDIM), k_ref[seqs].reshape(r, GLA_KDIM),
                            v_ref[seqs].reshape(r, D_GLA).astype(BF16), lg_ref[seqs].reshape(r, GLA_KDIM),
                            [s_ref[gi * sg + i] for i in range(sg)], masks, chunk)
        o_ref[seqs] = o.reshape(sg, chunk, D_GLA)
        for i in range(sg):
            s_ref[gi * sg + i] = new[i]


def _gla(q, k, v, lg, s0, bb, sg, chunk):
    b, l, _ = q.shape
    blk = lambda w: pl.BlockSpec((bb, chunk, w), lambda i, c: (i, c, 0))
    sspec = pl.BlockSpec((bb, GLA_KDIM, GLA_DV), lambda i, c: (i, 0, 0))
    return pl.pallas_call(
        functools.partial(_gla_kernel, bb=bb, sg=sg, chunk=chunk),
        grid=(b // bb, l // chunk),
        in_specs=[blk(GLA_KDIM), blk(GLA_KDIM), blk(D_GLA), blk(GLA_KDIM), sspec],
        out_specs=[blk(D_GLA), sspec],
        out_shape=[jax.ShapeDtypeStruct((b, l, D_GLA), F32),
                   jax.ShapeDtypeStruct((b, GLA_KDIM, GLA_DV), F32)],
        compiler_params=pltpu.CompilerParams(dimension_semantics=("parallel", "arbitrary"),
                                             vmem_limit_bytes=VMEM_LIMIT),
        name=f"gla_c{chunk}",
    )(q, k, v, lg, s0)


def _s5_kernel(ug_ref, mi_ref, minr_ref, mini_ref, mor_ref, moi_ref, ar_ref, ai_ref,
               h0r_ref, h0i_ref, yg_ref, hfr_ref, hfi_ref, vr_s, vi_s, hr_s, hi_s, str_s, sti_s, *, nb, cb):
    gs = SLAB_GROUPS // 2
    n_pairs = S5_GROUPS // 2
    w = ug_ref.shape[-1]
    rows = nb * cb
    batch = cb if nb == 1 else nb
    n_blocks = rows // batch
    interleave = nb > 1 and cb > 1
    first_pair = pl.program_id(0) * gs
    seq0 = pl.program_id(1) * batch if nb == 1 else 0

    def state_rows(p):
        return pl.ds(seq0 * n_pairs + first_pair + p, batch, stride=n_pairs)

    def load_state():
        for p in range(gs):
            str_s[p] = h0r_ref[state_rows(p), :]
            sti_s[p] = h0i_ref[state_rows(p), :]

    if nb == 1:
        load_state()
    else:
        pl.when(pl.program_id(1) == 0)(load_state)

    def to_scan_order(ref, g, val):
        if not interleave:
            ref[g] = val
        else:
            for b in range(nb):
                ref[g, pl.ds(b, cb, stride=nb), :] = val[b * cb:(b + 1) * cb]

    def from_scan_order(ref, g):
        if not interleave:
            return ref[g]
        return jnp.concatenate([ref[g, pl.ds(b, cb, stride=nb), :] for b in range(nb)], axis=0)

    ub = [ug_ref[g].reshape(rows, w) for g in range(2 * gs)]
    for g in range(gs):
        pair_u = jnp.concatenate([ub[2 * g], ub[2 * g + 1]], axis=1)
        to_scan_order(vr_s, g, _dot(pair_u, minr_ref[g]))
        to_scan_order(vi_s, g, _dot(pair_u, mini_ref[g]))
    ar = [ar_ref[g] for g in range(gs)]
    ai = [ai_ref[g] for g in range(gs)]

    def body(c, carry):
        rows = pl.ds(pl.multiple_of(c * batch, batch), batch)
        new = []
        for g in range(gs):
            hr, hi = carry[2 * g], carry[2 * g + 1]
            hr_s[g, rows, :] = hr
            hi_s[g, rows, :] = hi
            new.append(ar[g] * hr - ai[g] * hi + vr_s[g, rows, :])
            new.append(ar[g] * hi + ai[g] * hr + vi_s[g, rows, :])
        return tuple(new)

    init = tuple(ref[g] for g in range(gs) for ref in (str_s, sti_s))
    fin = lax.fori_loop(0, n_blocks, body, init, unroll=min(n_blocks, 4))
    for g in range(gs):
        str_s[g] = fin[2 * g]
        sti_s[g] = fin[2 * g + 1]
        hfr_ref[state_rows(g), :] = fin[2 * g]
        hfi_ref[state_rows(g), :] = fin[2 * g + 1]
        y_state = (_dot(from_scan_order(hr_s, g).astype(BF16), mor_ref[g])
                   + _dot(from_scan_order(hi_s, g).astype(BF16), moi_ref[g]))
        for k in range(2):
            y = _dot(ub[2 * g + k], mi_ref[2 * g + k]) + y_state[:, k * w:(k + 1) * w]
            yg_ref[2 * g + k] = y.reshape(yg_ref.shape[1:])


def _s5(ug, m, h0_re, h0_im, nb, cb):
    g_all, r, w = ug.shape
    nc = r // nb
    batch = cb if nb == 1 else nb
    n, gs = LANES, SLAB_GROUPS
    uspec = pl.BlockSpec((gs, nb, cb, w), lambda s, i: (s, 0, i, 0))
    per_g = lambda a, b_: pl.BlockSpec((gs, a, b_), lambda s, i: (s, 0, 0))
    per_p = lambda a, b_: pl.BlockSpec((gs // 2, a, b_), lambda s, i: (s, 0, 0))
    hspec = _const_spec(h0_re.shape)
    yg, hf_re, hf_im = pl.pallas_call(
        functools.partial(_s5_kernel, nb=nb, cb=cb),
        grid=(g_all // gs, nc // cb),
        in_specs=[uspec, per_g(w, w), per_p(2 * w, n), per_p(2 * w, n), per_p(n, 2 * w), per_p(n, 2 * w),
                  per_p(1, n), per_p(1, n), hspec, hspec],
        out_specs=[uspec, hspec, hspec],
        out_shape=[jax.ShapeDtypeStruct((g_all, nb, nc, w), F32),
                   jax.ShapeDtypeStruct(h0_re.shape, F32),
                   jax.ShapeDtypeStruct(h0_im.shape, F32)],
        scratch_shapes=[pltpu.VMEM((gs // 2, nb * cb, n), F32)] * 4 + [pltpu.VMEM((gs // 2, batch, n), F32)] * 2,
        compiler_params=pltpu.CompilerParams(dimension_semantics=("arbitrary", "arbitrary"),
                                             vmem_limit_bytes=VMEM_LIMIT),
        name=f"s5_w{w}",
    )(ug.reshape(g_all, nb, nc, w), m["m_intra"], m["m_in_re"], m["m_in_im"], m["m_out_re"], m["m_out_im"],
      m["a_re"], m["a_im"], h0_re, h0_im)
    return yg.reshape(g_all, r, w), hf_re, hf_im


def _mix_ffn_kernel(x_ref, o_ref, g_ref, yg_ref, u_ref, dsk_ref, gn_ref, wglu_ref, s5n_ref, wo_ref, gpost_ref,
                    gpre_ref, wg_ref, wu_ref, wd_ref, gpostf_ref, out_ref, y_scr, x1_scr, *, steps, n_tiles):
    i = pl.program_id(0)

    def mix_into(slot):
        nr = x_ref.shape[0] // steps
        rt = min(nr, 8)
        for s in range(D_S5 // LANES):
            for hf in range(steps // 8):
                for r0 in range(0, nr, rt):
                    a = [yg_ref[s * SLAB_GROUPS + g, r0:r0 + rt, hf * LANES:(hf + 1) * LANES]
                         for g in range(SLAB_GROUPS)]
                    per_step = _lane_block_transpose(a)
                    for t in range(8):
                        y_scr[s, pl.ds(r0 * steps + hf * 8 + t, rt, stride=steps), :] = per_step[t]
        y5 = jnp.concatenate([y_scr[s] for s in range(D_S5 // LANES)], axis=1) + dsk_ref[...] * u_ref[...]
        o = o_ref[...]
        gn = gn_ref[...]
        heads = []
        for h in range(GLA_HEADS):
            heads.append(_rms(o[:, h * GLA_DV:(h + 1) * GLA_DV], gn))
        og = jnp.concatenate(heads, axis=1) * jax.nn.silu(g_ref[...])
        y = jax.nn.gelu(y5)
        y = y * jax.nn.sigmoid(_dot(y.astype(BF16), wglu_ref[...]))
        y = _rms(y, s5n_ref[...])
        mix = _dot(og.astype(BF16), wo_ref[:D_GLA, :]) + _dot(y.astype(BF16), wo_ref[D_GLA:, :])
        x1_scr[slot] = x_ref[...] + _rms(mix, gpost_ref[...])

    def ffn_from(slot):
        x = x1_scr[slot]
        h = _rms(x, gpre_ref[...]).astype(BF16)
        acc = jnp.zeros(x.shape, F32)
        for c in range(D_FF // FF_CHUNK):
            cols = slice(c * FF_CHUNK, (c + 1) * FF_CHUNK)
            act = jax.nn.silu(_dot(h, wg_ref[:, cols])) * _dot(h, wu_ref[:, cols])
            acc = acc + _dot(act.astype(BF16), wd_ref[cols, :])
        out_ref[...] = x + _rms(acc, gpostf_ref[...])

    @pl.when(i == 0)
    def _():
        mix_into(0)

    @pl.when((i > 0) & (i < n_tiles))
    def _():
        ffn_from((i - 1) & 1)
        mix_into(i & 1)

    @pl.when(i == n_tiles)
    def _():
        ffn_from((i - 1) & 1)


def _mix_ffn(x, o, g, yg, u, dsk, gn, wglu, s5n, wo, gpost, gpre, wg, wu, wd, gpostf, tm, steps):
    t = x.shape[0]
    n = t // tm
    cur = lambda i: jnp.minimum(i, n - 1)
    row = lambda w: pl.BlockSpec((tm, w), lambda i: (cur(i), 0))
    fixed = lambda shape: pl.BlockSpec(shape, lambda i: (0,) * len(shape), pipeline_mode=pl.Buffered(1))
    return pl.pallas_call(
        functools.partial(_mix_ffn_kernel, steps=steps, n_tiles=n),
        grid=(n + 1,),
        in_specs=[row(D_MODEL), row(D_GLA), row(D_GLA),
                  pl.BlockSpec((S5_GROUPS, tm // steps, steps * S5_GROUP), lambda i: (0, cur(i), 0)),
                  row(D_S5), fixed((1, D_S5)),
                  fixed((1, GLA_DV)), fixed((D_S5, D_S5)), fixed((1, D_S5)),
                  fixed((D_GLA + D_S5, D_MODEL)), fixed((1, D_MODEL)),
                  fixed((1, D_MODEL)), fixed((D_MODEL, D_FF)), fixed((D_MODEL, D_FF)),
                  fixed((D_FF, D_MODEL)), fixed((1, D_MODEL))],
        out_specs=pl.BlockSpec((tm, D_MODEL), lambda i: (jnp.maximum(i - 1, 0), 0)),
        out_shape=jax.ShapeDtypeStruct((t, D_MODEL), F32),
        scratch_shapes=[pltpu.VMEM((D_S5 // LANES, tm, LANES), F32), pltpu.VMEM((2, tm, D_MODEL), F32)],
        compiler_params=pltpu.CompilerParams(dimension_semantics=("arbitrary",),
                                             vmem_limit_bytes=VMEM_LIMIT),
        name="mix_ffn",
    )(x, o, g, yg, u, dsk, gn, wglu, s5n, wo, gpost, gpre, wg, wu, wd, gpostf)


def kernel(x_prompt, x_sample, state_gla, state_s5_re, state_s5_im, meta_tokens, g_pre_mix, w_in, w_gk2, b_gk, gla_norm, s5_a_re, s5_a_im, s5_b_re, s5_b_im, s5_c_re, s5_c_im, s5_d, s5_log_dt, w_s5_glu, s5_norm, w_o, g_post_mix, g_pre_ffn, w_gate, w_up, w_down, g_post_ffn):
    assert g_pre_mix.shape[0] == 1, "single-layer step"
    bp, seq_p, _ = x_prompt.shape
    bs, seq_s, _ = x_sample.shape
    row = lambda t: t[0].reshape(1, -1)

    w = w_in[0]
    c3, c4 = 2 * GLA_KDIM + 2 * D_GLA, 2 * GLA_KDIM + 2 * D_GLA + GATE_RANK
    w_a = w[:, :c3].astype(BF16)
    w_b = jnp.concatenate([w[:, c4:].astype(BF16), w[:, c3:c4].astype(BF16),
                           jnp.zeros((D_MODEL, LANES - GATE_RANK), BF16)], axis=1)
    wgk_p = jnp.concatenate([w_gk2[0], jnp.zeros((LANES - GATE_RANK, GLA_KDIM), F32)], axis=0).astype(BF16)
    wo_bf = w_o[0].astype(BF16)
    s5m = _s5_prep(s5_a_re[0], s5_a_im[0], s5_b_re[0], s5_b_im[0], s5_c_re[0], s5_c_im[0], s5_log_dt[0])
    proj_w = (row(g_pre_mix), w_a, w_b, wgk_p, row(b_gk))

    def finish(x, o, g, yg, u, tm, steps):
        return _mix_ffn(x, o, g, yg, u, row(s5_d), row(gla_norm), w_s5_glu[0].astype(BF16), row(s5_norm),
                        wo_bf, row(g_post_mix), row(g_pre_ffn), w_gate[0].astype(BF16),
                        w_up[0].astype(BF16), w_down[0].astype(BF16), row(g_post_ffn), tm, steps)

    xm = jnp.broadcast_to(meta_tokens[None], (bp, N_META, D_MODEL)).reshape(bp * N_META, D_MODEL)
    q, k, v, _, _, lg, ug = _in_proj(xm, *proj_w, bp * N_META, S5_BLOCK)
    r3 = lambda t, b, l: t.reshape(b, l, t.shape[-1])
    _, s_meta = _gla(r3(q, bp, N_META), r3(k, bp, N_META), r3(v, bp, N_META), r3(lg, bp, N_META),
                     jnp.zeros((bp, GLA_KDIM, GLA_DV), F32), bp, bp, N_META)
    zh = jnp.zeros((bp * S5_GROUPS // 2, LANES), F32)
    _, hm_re, hm_im = _s5(ug, s5m[S5_BLOCK], zh, zh, 1, bp)

    xp = x_prompt.reshape(bp * seq_p, D_MODEL)
    g, u, o, ug, s_p = _in_proj(xp, *proj_w, TOKEN_TILE, S5_BLOCK, gla_s0=s_meta, seq_len=seq_p)
    yg, hp_re, hp_im = _s5(ug, s5m[S5_BLOCK], hm_re, hm_im, bp, S5_ROW_BLOCKS)
    y_prompt = finish(xp, o, g, yg, u, TOKEN_TILE, S5_BLOCK)

    xs = x_sample.reshape(bs * seq_s, D_MODEL)
    to_g = lambda t: t[0].reshape(bs * S5_GROUPS // 2, LANES)
    q, k, v, g, u, lg, ug = _in_proj(xs, *proj_w, TOKEN_TILE, seq_s)
    o, s_s = _gla(r3(q, bs, seq_s), r3(k, bs, seq_s), r3(v, bs, seq_s), r3(lg, bs, seq_s),
                  state_gla[0].reshape(bs, GLA_KDIM, GLA_DV), GLA_SAMPLE_SEQS, GLA_SAMPLE_GROUP, seq_s)
    yg, hs_re, hs_im = _s5(ug, s5m[seq_s], to_g(state_s5_re), to_g(state_s5_im), 1, bs)
    y_sample = finish(xs, o.reshape(bs * seq_s, D_GLA), g, yg, u, TOKEN_TILE, seq_s)

    gla_out = lambda s, b: s.reshape(1, b, GLA_HEADS, GLA_DK, GLA_DV)
    s5_out = lambda h: h.reshape(1, -1, S5_GROUPS, S5_STATE)
    return (y_prompt.reshape(bp, seq_p, D_MODEL), y_sample.reshape(bs, seq_s, D_MODEL),
            gla_out(s_p, bp), s5_out(hp_re), s5_out(hp_im),
            gla_out(s_s, bs), s5_out(hs_re), s5_out(hs_im))
```

```python
import functools

import jax
import jax.numpy as jnp
from jax import lax
from jax.experimental import pallas as pl
from jax.experimental.pallas import tpu as pltpu

F32 = jnp.float32
BF16 = jnp.bfloat16

D_MODEL = 1024
D_GLA = 512
GLA_HEADS = 4
GLA_DV = 128
GLA_DK = 64
GLA_KDIM = 256
GATE_RANK = 16
GATE_NORM = 16.0
GLA_CHUNK = 64
D_S5 = 512
S5_GROUP = 16
S5_GROUPS = 32
S5_STATE = 64
N_META = 16
D_FF = 2816
EPS = 1e-6
LANES = 128
S5_BLOCK = 16
SLAB_GROUPS = LANES // S5_GROUP
FF_CHUNK = 256
TOKEN_TILE = 512
S5_ROW_BLOCKS = 64
GLA_SAMPLE_GROUP = 16
GLA_SAMPLE_SEQS = 16
VMEM_LIMIT = 48 * 1024 * 1024


def _rms(x, g):
    return x * lax.rsqrt(jnp.mean(x * x, axis=-1, keepdims=True) + EPS) * g


def _dot(a, b):
    return jnp.dot(a, b, preferred_element_type=F32)


def _dot_nt(a, b):
    return lax.dot_general(a, b, (((1,), (1,)), ((), ())), preferred_element_type=F32)


def _dot_tn(a, b):
    return lax.dot_general(a, b, (((0,), (0,)), ((), ())), preferred_element_type=F32)


def _const_spec(shape):
    zeros = (0,) * len(shape)
    return pl.BlockSpec(shape, lambda *_: zeros)


PREP_GROUPS = 8


def _cmul(a, b):
    return a[0] * b[0] - a[1] * b[1], a[0] * b[1] + a[1] * b[0]


def _unit_powers(c1, s1, expo, n_bits):
    acc = (jnp.ones_like(c1), jnp.zeros_like(c1))
    base = (c1, s1)
    squares = [base]
    for bit in range(n_bits):
        take = ((expo >> bit) & 1) == 1
        nxt = _cmul(acc, base)
        acc = (jnp.where(take, nxt[0], acc[0]), jnp.where(take, nxt[1], acc[1]))
        base = _cmul(base, base)
        squares.append(base)
    return acc, squares


def _s5_prep_kernel(arc_ref, aic_ref, air_ref, ldt_ref, br_ref, bi_ref, cr_ref, ci_ref,
                    mi16_ref, mi8_ref, inr16_ref, ini16_ref, inr8_ref, ini8_ref,
                    outr16_ref, outi16_ref, outr8_ref, outi8_ref, a16r_ref, a16i_ref, a8r_ref, a8i_ref):
    n, q = S5_STATE, S5_BLOCK
    w = q * S5_GROUP
    hp = lax.Precision.HIGHEST
    lane = lax.broadcasted_iota(jnp.int32, (n, w), 1)
    t_blk = lane >> 4
    eye = lax.broadcasted_iota(jnp.int32, (n, n), 0) == lax.broadcasted_iota(jnp.int32, (n, n), 1)
    to_col = lambda r: jnp.sum(jnp.where(eye, r, 0.0), axis=1, keepdims=True)
    zrows = lambda x: jnp.concatenate([x, jnp.zeros_like(x)], axis=0)
    lane_n = lax.broadcasted_iota(jnp.int32, (n, LANES), 1)
    rep = jnp.where((lax.broadcasted_iota(jnp.int32, (S5_GROUP, w), 1) & (S5_GROUP - 1))
                    == lax.broadcasted_iota(jnp.int32, (S5_GROUP, w), 0), 1.0, 0.0)
    mo_g, rev_g, hop_g = [], [], []
    for g in range(PREP_GROUPS):
        dt = jnp.exp(ldt_ref[g])
        ang_r = air_ref[g] * dt
        c1, s1 = to_col(jnp.cos(ang_r)), to_col(jnp.sin(ang_r))
        lam_re = jnp.minimum(arc_ref[g], -1e-4)
        lam_im = aic_ref[g]
        unit, squares = _unit_powers(c1, s1, t_blk, 4)
        pm = jnp.exp(t_blk.astype(F32) * (lam_re * dt))
        pk = (pm * unit[0], pm * unit[1])
        mag = jnp.exp(lam_re * dt)
        ab = (mag * c1, mag * s1)
        p1 = _cmul(pk, ab)
        ct = tuple(lax.dot_general(r[g], rep, (((0,), (0,)), ((), ())), precision=hp,
                                   preferred_element_type=F32) for r in (cr_ref, ci_ref))
        bt = tuple(jnp.dot(r[g], rep, precision=hp, preferred_element_type=F32) for r in (br_ref, bi_ref))
        g0 = _cmul(ct, pk)
        mo = _cmul(ct, p1)
        mo_g.append((mo[0], -mo[1]))
        den = lam_re * lam_re + lam_im * lam_im
        nr, ni = ab[0] - 1.0, ab[1]
        f = ((nr * lam_re + ni * lam_im) / den, (ni * lam_re - nr * lam_im) / den)
        e = _cmul(pk, _cmul(f, bt))
        et = [zrows(x).T for x in e]
        rev_g.append([jnp.concatenate([x[(q - 1 - s) * S5_GROUP:(q - s) * S5_GROUP] for s in range(q)], axis=0)
                      for x in et])
        hops = []
        for steps in (q, q // 2):
            m = jnp.exp(float(steps) * (lam_re * dt))
            u = squares[steps.bit_length() - 1]
            hops += [m * u[0], m * u[1]]
        cols = jnp.zeros((n, LANES), F32)
        for idx, hcol in enumerate(hops):
            cols = jnp.where(lane_n == idx, hcol, cols)
        hop_g.append(zrows(cols).T)
        t0 = (jnp.dot(et[0][:S5_GROUP, :n], g0[0], precision=hp, preferred_element_type=F32)
              - jnp.dot(et[1][:S5_GROUP, :n], g0[1], precision=hp, preferred_element_type=F32))
        lane_t = lax.broadcasted_iota(jnp.int32, t0.shape, 1) >> 4
        for s in range(q):
            blk = t0 if s == 0 else jnp.where(lane_t >= s, pltpu.roll(t0, S5_GROUP * s, 1), 0.0)
            mi16_ref[g, s * S5_GROUP:(s + 1) * S5_GROUP, :] = blk.astype(BF16)
            if s < q // 2:
                mi8_ref[g, s * S5_GROUP:(s + 1) * S5_GROUP, :] = blk[:, :w // 2].astype(BF16)

    half = lambda x: pltpu.roll(x, n, 1)
    for p in range(PREP_GROUPS // 2):
        g0, g1 = 2 * p, 2 * p + 1
        for c, (in16, in8, out16, out8) in enumerate(((inr16_ref, inr8_ref, outr16_ref, outr8_ref),
                                                      (ini16_ref, ini8_ref, outi16_ref, outi8_ref))):
            lo, hi = rev_g[g0][c], half(rev_g[g1][c])
            in16[p] = jnp.concatenate([lo, hi], axis=0).astype(BF16)
            in8[p] = jnp.concatenate([lo[w // 2:], hi[w // 2:]], axis=0).astype(BF16)
            m0, m1 = mo_g[g0][c], mo_g[g1][c]
            z, zh = jnp.zeros_like(m0), jnp.zeros((n, w // 2), F32)
            out16[p] = jnp.concatenate([jnp.concatenate([m0, z], axis=1),
                                        jnp.concatenate([z, m1], axis=1)], axis=0).astype(BF16)
            out8[p] = jnp.concatenate([jnp.concatenate([m0[:, :w // 2], zh], axis=1),
                                       jnp.concatenate([zh, m1[:, :w // 2]], axis=1)], axis=0).astype(BF16)
        hops = hop_g[g0] + half(hop_g[g1])
        a16r_ref[p] = hops[0:1]
        a16i_ref[p] = hops[1:2]
        a8r_ref[p] = hops[2:3]
        a8i_ref[p] = hops[3:4]


def _s5_prep(a_re, a_im, b_re, b_im, c_re, c_im, log_dt):
    g, n, j, q = S5_GROUPS, S5_STATE, S5_GROUP, S5_BLOCK
    w, h, pg = q * j, q * j // 2, PREP_GROUPS
    col = lambda t: t.reshape(g, n, 1)
    rowv = lambda t: t.reshape(g, 1, n)
    spec = lambda a, b_: pl.BlockSpec((pg, a, b_), lambda i: (i, 0, 0))
    pspec = lambda a, b_: pl.BlockSpec((pg // 2, a, b_), lambda i: (i, 0, 0))
    pair_shapes = [(2 * w, LANES), (2 * w, LANES), (2 * h, LANES), (2 * h, LANES),
                   (LANES, 2 * w), (LANES, 2 * w), (LANES, 2 * h), (LANES, 2 * h)]
    outs = pl.pallas_call(
        _s5_prep_kernel,
        grid=(g // pg,),
        in_specs=[spec(n, 1), spec(n, 1), spec(1, n), spec(1, 1),
                  spec(n, j), spec(n, j), spec(j, n), spec(j, n)],
        out_specs=[spec(w, w), spec(h, h)] + [pspec(*s) for s in pair_shapes] + [pspec(1, LANES)] * 4,
        out_shape=[jax.ShapeDtypeStruct((g, w, w), BF16), jax.ShapeDtypeStruct((g, h, h), BF16)]
                  + [jax.ShapeDtypeStruct((g // 2,) + s, BF16) for s in pair_shapes]
                  + [jax.ShapeDtypeStruct((g // 2, 1, LANES), F32)] * 4,
        compiler_params=pltpu.CompilerParams(dimension_semantics=("parallel",)),
        name="s5_prep",
    )(col(a_re), col(a_im), rowv(a_im), log_dt.reshape(g, 1, 1),
      b_re, b_im, c_re, c_im)
    mi16, mi8, inr16, ini16, inr8, ini8, outr16, outi16, outr8, outi8, a16r, a16i, a8r, a8i = outs
    return {q: dict(m_intra=mi16, m_in_re=inr16, m_in_im=ini16, m_out_re=outr16, m_out_im=outi16,
                    a_re=a16r, a_im=a16i),
            q // 2: dict(m_intra=mi8, m_in_re=inr8, m_in_im=ini8, m_out_re=outr8, m_out_im=outi8,
                         a_re=a8r, a_im=a8i)}


def _lane_block_transpose(a):
    a = list(a)
    blk = lax.broadcasted_iota(jnp.int32, a[0].shape, 1) >> 4
    for d in (4, 2, 1):
        upper = (blk & d) != 0
        for r in range(8):
            if r & d:
                continue
            lo, hi = a[r], a[r + d]
            a[r] = jnp.where(upper, pltpu.roll(hi, 16 * d, 1), lo)
            a[r + d] = jnp.where(upper, hi, pltpu.roll(lo, LANES - 16 * d, 1))
    return a


def _in_proj_kernel(*refs, steps, n_tiles, tiles_per_seq):
    fused = tiles_per_seq is not None
    x_ref, gpre_ref, wa_ref, wb_ref, wgk_ref, bgk_ref = refs[:6]
    if fused:
        (s0_ref, g_ref, u_ref, o_ref, ug_ref, sfin_ref,
         u_scr, gk_scr, q_scr, k_scr, v_scr, s_scr) = refs[6:]
    else:
        q_ref, k_ref, v_ref, g_ref, u_ref, lg_ref, ug_ref, u_scr, gk_scr = refs[6:]
    i = pl.program_id(0)
    tm = x_ref.shape[0]

    def project(slot):
        h = _rms(x_ref[...], gpre_ref[...]).astype(BF16)
        proj = _dot(h, wa_ref[...])
        tail_proj = _dot(h, wb_ref[...])
        if fused:
            q_scr[slot] = proj[:, 0:256]
            k_scr[slot] = proj[:, 256:512]
            v_scr[slot] = proj[:, 512:1024].astype(BF16)
        else:
            q_ref[...] = proj[:, 0:256]
            k_ref[...] = proj[:, 256:512]
            v_ref[...] = proj[:, 512:1024]
        g_ref[...] = proj[:, 1024:1536]
        u = tail_proj[:, :D_S5]
        u_ref[...] = u
        for s in range(D_S5 // LANES):
            u_scr[slot, s] = u[:, s * LANES:(s + 1) * LANES]
        gk_scr[slot] = tail_proj[:, D_S5:]

    def tail(slot):
        z = _dot(gk_scr[slot].astype(BF16), wgk_ref[...]) + bgk_ref[...]
        lg = jax.nn.log_sigmoid(z) * (1.0 / GATE_NORM)
        if fused:
            tile = i - 1
            seq = tile // tiles_per_seq
            state = jnp.where(tile % tiles_per_seq == 0, s0_ref[seq], s_scr[...])
            masks = _gla_masks(1, GLA_CHUNK)
            for c in range(tm // GLA_CHUNK):
                rows = slice(c * GLA_CHUNK, (c + 1) * GLA_CHUNK)
                o, (state,) = _gla_group(q_scr[slot, rows, :], k_scr[slot, rows, :], v_scr[slot, rows, :],
                                         lg[rows], [state], masks, GLA_CHUNK)
                o_ref[rows, :] = o
            s_scr[...] = state
            sfin_ref[seq] = state
        else:
            lg_ref[...] = lg
        nr = tm // steps
        rt = min(nr, 16)
        for s in range(D_S5 // LANES):
            for hf in range(steps // 8):
                for r0 in range(0, nr, rt):
                    a = [u_scr[slot, s, pl.ds(r0 * steps + hf * 8 + t, rt, stride=steps), :] for t in range(8)]
                    per_group = _lane_block_transpose(a)
                    for g in range(SLAB_GROUPS):
                        ug_ref[s * SLAB_GROUPS + g, r0:r0 + rt, hf * LANES:(hf + 1) * LANES] = (
                            per_group[g].astype(BF16))

    @pl.when(i == 0)
    def _():
        if fused:
            s_scr[...] = jnp.zeros(s_scr.shape, F32)
        project(0)

    @pl.when((i > 0) & (i < n_tiles))
    def _():
        tail((i - 1) & 1)
        project(i & 1)

    @pl.when(i == n_tiles)
    def _():
        tail((i - 1) & 1)


def _in_proj(x, gpre, w_a, w_b, wgk_p, bgk, tm, steps, gla_s0=None, seq_len=None):
    t = x.shape[0]
    n = t // tm
    cur = lambda i: jnp.minimum(i, n - 1)
    prev = lambda i: jnp.maximum(i - 1, 0)
    row = lambda w, at: pl.BlockSpec((tm, w), lambda i: (at(i), 0))
    wg = steps * S5_GROUP
    ug_spec = pl.BlockSpec((S5_GROUPS, tm // steps, wg), lambda i: (0, prev(i), 0))
    ug_shape = jax.ShapeDtypeStruct((S5_GROUPS, t // steps, wg), BF16)
    f32 = lambda w: jax.ShapeDtypeStruct((t, w), F32)
    in_specs = [row(D_MODEL, cur), _const_spec((1, D_MODEL)), _const_spec(w_a.shape), _const_spec(w_b.shape),
                _const_spec((LANES, GLA_KDIM)), _const_spec((1, GLA_KDIM))]
    scratch = [pltpu.VMEM((2, D_S5 // LANES, tm, LANES), F32), pltpu.VMEM((2, tm, LANES), F32)]
    args = (x, gpre, w_a, w_b, wgk_p, bgk)
    if gla_s0 is None:
        tiles_per_seq = None
        out_specs = [row(w, cur) for w in (256, 256, 512, 512, 512)] + [row(GLA_KDIM, prev), ug_spec]
        out_shape = [f32(w) for w in (256, 256, 512, 512, 512)] + [f32(GLA_KDIM), ug_shape]
    else:
        tiles_per_seq = seq_len // tm
        assert seq_len % tm == 0 and tm % GLA_CHUNK == 0
        in_specs.append(_const_spec(gla_s0.shape))
        args += (gla_s0,)
        out_specs = [row(D_GLA, cur), row(D_S5, cur), row(D_GLA, prev), ug_spec, _const_spec(gla_s0.shape)]
        out_shape = [f32(D_GLA), f32(D_S5), f32(D_GLA), ug_shape, jax.ShapeDtypeStruct(gla_s0.shape, F32)]
        scratch += [pltpu.VMEM((2, tm, GLA_KDIM), F32), pltpu.VMEM((2, tm, GLA_KDIM), F32),
                    pltpu.VMEM((2, tm, D_GLA), BF16), pltpu.VMEM(gla_s0.shape[1:], F32)]
    return pl.pallas_call(
        functools.partial(_in_proj_kernel, steps=steps, n_tiles=n, tiles_per_seq=tiles_per_seq),
        grid=(n + 1,),
        in_specs=in_specs,
        out_specs=out_specs,
        out_shape=out_shape,
        scratch_shapes=scratch,
        compiler_params=pltpu.CompilerParams(dimension_semantics=("arbitrary",),
                                             vmem_limit_bytes=VMEM_LIMIT),
        name="in_proj" if gla_s0 is None else "in_proj_gla",
    )(*args)


def _gla_masks(sg, chunk):
    r = sg * chunk
    assert r >= LANES or sg == 1
    shift = chunk.bit_length() - 1
    hk = GLA_HEADS * r
    ri = lax.broadcasted_iota(jnp.int32, (r, r), 0)
    ci = lax.broadcasted_iota(jnp.int32, (r, r), 1)
    ri4 = lax.broadcasted_iota(jnp.int32, (r, hk), 0)
    ci4 = lax.broadcasted_iota(jnp.int32, (r, hk), 1) & (r - 1)
    stack_head = lax.broadcasted_iota(jnp.int32, (hk, 1), 0) >> (r.bit_length() - 1)
    k_head = lax.broadcasted_iota(jnp.int32, (1, GLA_KDIM), 1) >> 6
    v_head = lax.broadcasted_iota(jnp.int32, (1, D_GLA), 1) >> 7
    s_head = lax.broadcasted_iota(jnp.int32, (GLA_KDIM, 1), 0) >> 6
    return dict(
        tri_bf=jnp.where(((ri >> shift) == (ci >> shift)) & (ri >= ci), 1.0, 0.0).astype(BF16),
        causal4=((ri4 >> shift) == (ci4 >> shift)) & (ri4 >= ci4),
        k_diag=stack_head == k_head, v_diag=stack_head == v_head, s_diag=s_head == v_head)


def _gla_group(q, k, v_bf, lg, states, m, chunk):
    sg = len(states)
    r = sg * chunk
    tile4 = lambda t: jnp.concatenate([t] * GLA_HEADS, axis=0)
    lg_hi = lg.astype(BF16)
    lg_lo = (lg - lg_hi.astype(F32)).astype(BF16)
    b2 = _dot(m["tri_bf"], jnp.concatenate([lg_hi, lg_lo], axis=1))
    b = b2[:, :GLA_KDIM] + b2[:, GLA_KDIM:]
    lasts = [b[(i + 1) * chunk - 1:(i + 1) * chunk, :] for i in range(sg)]
    bl = jnp.concatenate([jnp.broadcast_to(t, (chunk, GLA_KDIM)) for t in lasts], axis=0)
    qd = (q * (GLA_DK ** -0.5) * jnp.exp(b)).astype(BF16)
    ki = (k * jnp.exp(-b)).astype(BF16)
    ke = k * jnp.exp(bl - b)

    ki_bd = jnp.where(m["k_diag"], tile4(ki), 0.0)
    att = jnp.where(m["causal4"], _dot_nt(qd, ki_bd), 0.0).astype(BF16)
    v_bd = jnp.where(m["v_diag"], tile4(v_bf), 0.0)
    o_intra = _dot(att, v_bd)

    if r < LANES:
        aug_t = jnp.concatenate([ke, jnp.broadcast_to(lasts[0], (LANES - r, GLA_KDIM))], axis=0).T
        ke_t, bl_t = aug_t, aug_t[:, r:]
    else:
        ke_t = ke.T
        bl_t = jnp.concatenate(lasts + [jnp.zeros((r - sg, GLA_KDIM), F32)], axis=0).T
    ke_t = ke_t.astype(BF16)

    outs, new_states = [], []
    for i, s_old in enumerate(states):
        rows = slice(i * chunk, (i + 1) * chunk)
        s_bd = jnp.where(m["s_diag"], jnp.concatenate([s_old.astype(BF16)] * GLA_HEADS, axis=1), 0.0)
        outs.append(o_intra[rows] + _dot(qd[rows], s_bd))
        upd = [_dot(ke_t[h * GLA_DK:(h + 1) * GLA_DK, rows], v_bf[rows, h * GLA_DV:(h + 1) * GLA_DV])
               for h in range(GLA_HEADS)]
        new_states.append(jnp.exp(bl_t[:, i:i + 1]) * s_old + jnp.concatenate(upd, axis=0))
    return (outs[0] if sg == 1 else jnp.concatenate(outs, axis=0)), new_states


def _gla_kernel(q_ref, k_ref, v_ref, lg_ref, s0_ref, o_ref, s_ref, *, bb, sg, chunk):
    r = sg * chunk

    @pl.when(pl.program_id(1) == 0)
    def _():
        s_ref[...] = s0_ref[...]

    masks = _gla_masks(sg, chunk)
    for gi in range(bb // sg):
        seqs = slice(gi * sg, (gi + 1) * sg)
        o, new = _gla_group(q_ref[seqs].reshape(r, GLA_KDIM), k_ref[seqs].reshape(r, GLA_KDIM),
                            v_ref[seqs].reshape(r, D_GLA).astype(BF16), lg_ref[seqs].reshape(r, GLA_KDIM),
                            [s_ref[gi * sg + i] for i in range(sg)], masks, chunk)
        o_ref[seqs] = o.reshape(sg, chunk, D_GLA)
        for i in range(sg):
            s_ref[gi * sg + i] = new[i]


def _gla(q, k, v, lg, s0, bb, sg, chunk):
    b, l, _ = q.shape
    blk = lambda w: pl.BlockSpec((bb, chunk, w), lambda i, c: (i, c, 0))
    sspec = pl.BlockSpec((bb, GLA_KDIM, GLA_DV), lambda i, c: (i, 0, 0))
    return pl.pallas_call(
        functools.partial(_gla_kernel, bb=bb, sg=sg, chunk=chunk),
        grid=(b // bb, l // chunk),
        in_specs=[blk(GLA_KDIM), blk(GLA_KDIM), blk(D_GLA), blk(GLA_KDIM), sspec],
        out_specs=[blk(D_GLA), sspec],
        out_shape=[jax.ShapeDtypeStruct((b, l, D_GLA), F32),
                   jax.ShapeDtypeStruct((b, GLA_KDIM, GLA_DV), F32)],
        compiler_params=pltpu.CompilerParams(dimension_semantics=("parallel", "arbitrary"),
                                             vmem_limit_bytes=VMEM_LIMIT),
        name=f"gla_c{chunk}",
    )(q, k, v, lg, s0)


def _s5_kernel(ug_ref, mi_ref, minr_ref, mini_ref, mor_ref, moi_ref, ar_ref, ai_ref,
               h0r_ref, h0i_ref, yg_ref, hfr_ref, hfi_ref, vr_s, vi_s, hr_s, hi_s, str_s, sti_s, *, nb, cb):
    gs = SLAB_GROUPS // 2
    n_pairs = S5_GROUPS // 2
    w = ug_ref.shape[-1]
    rows = nb * cb
    batch = cb if nb == 1 else nb
    n_blocks = rows // batch
    interleave = nb > 1 and cb > 1
    first_pair = pl.program_id(0) * gs
    seq0 = pl.program_id(1) * batch if nb == 1 else 0

    def state_rows(p):
        return pl.ds(seq0 * n_pairs + first_pair + p, batch, stride=n_pairs)

    def load_state():
        for p in range(gs):
            str_s[p] = h0r_ref[state_rows(p), :]
            sti_s[p] = h0i_ref[state_rows(p), :]

    if nb == 1:
        load_state()
    else:
        pl.when(pl.program_id(1) == 0)(load_state)

    def to_scan_order(ref, g, val):
        if not interleave:
            ref[g] = val
        else:
            for b in range(nb):
                ref[g, pl.ds(b, cb, stride=nb), :] = val[b * cb:(b + 1) * cb]

    def from_scan_order(ref, g):
        if not interleave:
            return ref[g]
        return jnp.concatenate([ref[g, pl.ds(b, cb, stride=nb), :] for b in range(nb)], axis=0)

    ub = [ug_ref[g].reshape(rows, w) for g in range(2 * gs)]
    for g in range(gs):
        pair_u = jnp.concatenate([ub[2 * g], ub[2 * g + 1]], axis=1)
        to_scan_order(vr_s, g, _dot(pair_u, minr_ref[g]))
        to_scan_order(vi_s, g, _dot(pair_u, mini_ref[g]))
    ar = [ar_ref[g] for g in range(gs)]
    ai = [ai_ref[g] for g in range(gs)]

    def body(c, carry):
        rows = pl.ds(pl.multiple_of(c * batch, batch), batch)
        new = []
        for g in range(gs):
            hr, hi = carry[2 * g], carry[2 * g + 1]
            hr_s[g, rows, :] = hr
            hi_s[g, rows, :] = hi
            new.append(ar[g] * hr - ai[g] * hi + vr_s[g, rows, :])
            new.append(ar[g] * hi + ai[g] * hr + vi_s[g, rows, :])
        return tuple(new)

    init = tuple(ref[g] for g in range(gs) for ref in (str_s, sti_s))
    fin = lax.fori_loop(0, n_blocks, body, init, unroll=min(n_blocks, 4))
    for g in range(gs):
        str_s[g] = fin[2 * g]
        sti_s[g] = fin[2 * g + 1]
        hfr_ref[state_rows(g), :] = fin[2 * g]
        hfi_ref[state_rows(g), :] = fin[2 * g + 1]
        y_state = (_dot(from_scan_order(hr_s, g).astype(BF16), mor_ref[g])
                   + _dot(from_scan_order(hi_s, g).astype(BF16), moi_ref[g]))
        for k in range(2):
            y = _dot(ub[2 * g + k], mi_ref[2 * g + k]) + y_state[:, k * w:(k + 1) * w]
            yg_ref[2 * g + k] = y.reshape(yg_ref.shape[1:])


def _s5(ug, m, h0_re, h0_im, nb, cb):
    g_all, r, w = ug.shape
    nc = r // nb
    batch = cb if nb == 1 else nb
    n, gs = LANES, SLAB_GROUPS
    uspec = pl.BlockSpec((gs, nb, cb, w), lambda s, i: (s, 0, i, 0))
    per_g = lambda a, b_: pl.BlockSpec((gs, a, b_), lambda s, i: (s, 0, 0))
    per_p = lambda a, b_: pl.BlockSpec((gs // 2, a, b_), lambda s, i: (s, 0, 0))
    hspec = _const_spec(h0_re.shape)
    yg, hf_re, hf_im = pl.pallas_call(
        functools.partial(_s5_kernel, nb=nb, cb=cb),
        grid=(g_all // gs, nc // cb),
        in_specs=[uspec, per_g(w, w), per_p(2 * w, n), per_p(2 * w, n), per_p(n, 2 * w), per_p(n, 2 * w),
                  per_p(1, n), per_p(1, n), hspec, hspec],
        out_specs=[uspec, hspec, hspec],
        out_shape=[jax.ShapeDtypeStruct((g_all, nb, nc, w), F32),
                   jax.ShapeDtypeStruct(h0_re.shape, F32),
                   jax.ShapeDtypeStruct(h0_im.shape, F32)],
        scratch_shapes=[pltpu.VMEM((gs // 2, nb * cb, n), F32)] * 4 + [pltpu.VMEM((gs // 2, batch, n), F32)] * 2,
        compiler_params=pltpu.CompilerParams(dimension_semantics=("arbitrary", "arbitrary"),
                                             vmem_limit_bytes=VMEM_LIMIT),
        name=f"s5_w{w}",
    )(ug.reshape(g_all, nb, nc, w), m["m_intra"], m["m_in_re"], m["m_in_im"], m["m_out_re"], m["m_out_im"],
      m["a_re"], m["a_im"], h0_re, h0_im)
    return yg.reshape(g_all, r, w), hf_re, hf_im


def _mix_ffn_kernel(x_ref, o_ref, g_ref, yg_ref, u_ref, dsk_ref, gn_ref, wglu_ref, s5n_ref, wo_ref, gpost_ref,
                    gpre_ref, wg_ref, wu_ref, wd_ref, gpostf_ref, out_ref, y_scr, x1_scr, *, steps, n_tiles):
    i = pl.program_id(0)

    def mix_into(slot):
        nr = x_ref.shape[0] // steps
        rt = min(nr, 8)
        for s in range(D_S5 // LANES):
            for hf in range(steps // 8):
                for r0 in range(0, nr, rt):
                    a = [yg_ref[s * SLAB_GROUPS + g, r0:r0 + rt, hf * LANES:(hf + 1) * LANES]
                         for g in range(SLAB_GROUPS)]
                    per_step = _lane_block_transpose(a)
                    for t in range(8):
                        y_scr[s, pl.ds(r0 * steps + hf * 8 + t, rt, stride=steps), :] = per_step[t]
        y5 = jnp.concatenate([y_scr[s] for s in range(D_S5 // LANES)], axis=1) + dsk_ref[...] * u_ref[...]
        o = o_ref[...]
        gn = gn_ref[...]
        heads = []
        for h in range(GLA_HEADS):
            heads.append(_rms(o[:, h * GLA_DV:(h + 1) * GLA_DV], gn))
        og = jnp.concatenate(heads, axis=1) * jax.nn.silu(g_ref[...])
        y = jax.nn.gelu(y5)
        y = y * jax.nn.sigmoid(_dot(y.astype(BF16), wglu_ref[...]))
        y = _rms(y, s5n_ref[...])
        mix = _dot(og.astype(BF16), wo_ref[:D_GLA, :]) + _dot(y.astype(BF16), wo_ref[D_GLA:, :])
        x1_scr[slot] = x_ref[...] + _rms(mix, gpost_ref[...])

    def ffn_from(slot):
        x = x1_scr[slot]
        h = _rms(x, gpre_ref[...]).astype(BF16)
        acc = jnp.zeros(x.shape, F32)
        for c in range(D_FF // FF_CHUNK):
            cols = slice(c * FF_CHUNK, (c + 1) * FF_CHUNK)
            act = jax.nn.silu(_dot(h, wg_ref[:, cols])) * _dot(h, wu_ref[:, cols])
            acc = acc + _dot(act.astype(BF16), wd_ref[cols, :])
        out_ref[...] = x + _rms(acc, gpostf_ref[...])

    @pl.when(i == 0)
    def _():
        mix_into(0)

    @pl.when((i > 0) & (i < n_tiles))
    def _():
        ffn_from((i - 1) & 1)
        mix_into(i & 1)

    @pl.when(i == n_tiles)
    def _():
        ffn_from((i - 1) & 1)


def _mix_ffn(x, o, g, yg, u, dsk, gn, wglu, s5n, wo, gpost, gpre, wg, wu, wd, gpostf, tm, steps):
    t = x.shape[0]
    n = t // tm
    cur = lambda i: jnp.minimum(i, n - 1)
    row = lambda w: pl.BlockSpec((tm, w), lambda i: (cur(i), 0))
    fixed = lambda shape: pl.BlockSpec(shape, lambda i: (0,) * len(shape), pipeline_mode=pl.Buffered(1))
    return pl.pallas_call(
        functools.partial(_mix_ffn_kernel, steps=steps, n_tiles=n),
        grid=(n + 1,),
        in_specs=[row(D_MODEL), row(D_GLA), row(D_GLA),
                  pl.BlockSpec((S5_GROUPS, tm // steps, steps * S5_GROUP), lambda i: (0, cur(i), 0)),
                  row(D_S5), fixed((1, D_S5)),
                  fixed((1, GLA_DV)), fixed((D_S5, D_S5)), fixed((1, D_S5)),
                  fixed((D_GLA + D_S5, D_MODEL)), fixed((1, D_MODEL)),
                  fixed((1, D_MODEL)), fixed((D_MODEL, D_FF)), fixed((D_MODEL, D_FF)),
                  fixed((D_FF, D_MODEL)), fixed((1, D_MODEL))],
        out_specs=pl.BlockSpec((tm, D_MODEL), lambda i: (jnp.maximum(i - 1, 0), 0)),
        out_shape=jax.ShapeDtypeStruct((t, D_MODEL), F32),
        scratch_shapes=[pltpu.VMEM((D_S5 // LANES, tm, LANES), F32), pltpu.VMEM((2, tm, D_MODEL), F32)],
        compiler_params=pltpu.CompilerParams(dimension_semantics=("arbitrary",),
                                             vmem_limit_bytes=VMEM_LIMIT),
        name="mix_ffn",
    )(x, o, g, yg, u, dsk, gn, wglu, s5n, wo, gpost, gpre, wg, wu, wd, gpostf)


def kernel(x_prompt, x_sample, state_gla, state_s5_re, state_s5_im, meta_tokens, g_pre_mix, w_in, w_gk2, b_gk, gla_norm, s5_a_re, s5_a_im, s5_b_re, s5_b_im, s5_c_re, s5_c_im, s5_d, s5_log_dt, w_s5_glu, s5_norm, w_o, g_post_mix, g_pre_ffn, w_gate, w_up, w_down, g_post_ffn):
    assert g_pre_mix.shape[0] == 1, "single-layer step"
    bp, seq_p, _ = x_prompt.shape
    bs, seq_s, _ = x_sample.shape
    row = lambda t: t[0].reshape(1, -1)

    w = w_in[0]
    c3, c4 = 2 * GLA_KDIM + 2 * D_GLA, 2 * GLA_KDIM + 2 * D_GLA + GATE_RANK
    w_a = w[:, :c3].astype(BF16)
    w_b = jnp.concatenate([w[:, c4:].astype(BF16), w[:, c3:c4].astype(BF16),
                           jnp.zeros((D_MODEL, LANES - GATE_RANK), BF16)], axis=1)
    wgk_p = jnp.concatenate([w_gk2[0], jnp.zeros((LANES - GATE_RANK, GLA_KDIM), F32)], axis=0).astype(BF16)
    wo_bf = w_o[0].astype(BF16)
    s5m = _s5_prep(s5_a_re[0], s5_a_im[0], s5_b_re[0], s5_b_im[0], s5_c_re[0], s5_c_im[0], s5_log_dt[0])
    proj_w = (row(g_pre_mix), w_a, w_b, wgk_p, row(b_gk))

    def finish(x, o, g, yg, u, tm, steps):
        return _mix_ffn(x, o, g, yg, u, row(s5_d), row(gla_norm), w_s5_glu[0].astype(BF16), row(s5_norm),
                        wo_bf, row(g_post_mix), row(g_pre_ffn), w_gate[0].astype(BF16),
                        w_up[0].astype(BF16), w_down[0].astype(BF16), row(g_post_ffn), tm, steps)

    xm = jnp.broadcast_to(meta_tokens[None], (bp, N_META, D_MODEL)).reshape(bp * N_META, D_MODEL)
    q, k, v, _, _, lg, ug = _in_proj(xm, *proj_w, bp * N_META, S5_BLOCK)
    r3 = lambda t, b, l: t.reshape(b, l, t.shape[-1])
    _, s_meta = _gla(r3(q, bp, N_META), r3(k, bp, N_META), r3(v, bp, N_META), r3(lg, bp, N_META),
                     jnp.zeros((bp, GLA_KDIM, GLA_DV), F32), bp, bp, N_META)
    zh = jnp.zeros((bp * S5_GROUPS // 2, LANES), F32)
    _, hm_re, hm_im = _s5(ug, s5m[S5_BLOCK], zh, zh, 1, bp)

    xp = x_prompt.reshape(bp * seq_p, D_MODEL)
    g, u, o, ug, s_p = _in_proj(xp, *proj_w, TOKEN_TILE, S5_BLOCK, gla_s0=s_meta, seq_len=seq_p)
    yg, hp_re, hp_im = _s5(ug, s5m[S5_BLOCK], hm_re, hm_im, bp, S5_ROW_BLOCKS)
    y_prompt = finish(xp, o, g, yg, u, TOKEN_TILE, S5_BLOCK)

    xs = x_sample.reshape(bs * seq_s, D_MODEL)
    to_g = lambda t: t[0].reshape(bs * S5_GROUPS // 2, LANES)
    q, k, v, g, u, lg, ug = _in_proj(xs, *proj_w, TOKEN_TILE, seq_s)
    o, s_s = _gla(r3(q, bs, seq_s), r3(k, bs, seq_s), r3(v, bs, seq_s), r3(lg, bs, seq_s),
                  state_gla[0].reshape(bs, GLA_KDIM, GLA_DV), GLA_SAMPLE_SEQS, GLA_SAMPLE_GROUP, seq_s)
    yg, hs_re, hs_im = _s5(ug, s5m[seq_s], to_g(state_s5_re), to_g(state_s5_im), 1, bs)
    y_sample = finish(xs, o.reshape(bs * seq_s, D_GLA), g, yg, u, TOKEN_TILE, seq_s)

    gla_out = lambda s, b: s.reshape(1, b, GLA_HEADS, GLA_DK, GLA_DV)
    s5_out = lambda h: h.reshape(1, -1, S5_GROUPS, S5_STATE)
    return (y_prompt.reshape(bp, seq_p, D_MODEL), y_sample.reshape(bs, seq_s, D_MODEL),
            gla_out(s_p, bp), s5_out(hp_re), s5_out(hp_im),
            gla_out(s_s, bs), s5_out(hs_re), s5_out(hs_im))
```

```python
import functools

import jax
import jax.numpy as jnp
from jax import lax
from jax.experimental import pallas as pl
from jax.experimental.pallas import tpu as pltpu

F32 = jnp.float32
BF16 = jnp.bfloat16

D_MODEL = 1024
D_GLA = 512
GLA_HEADS = 4
GLA_DV = 128
GLA_DK = 64
GLA_KDIM = 256
GATE_RANK = 16
GATE_NORM = 16.0
GLA_CHUNK = 64
D_S5 = 512
S5_GROUP = 16
S5_GROUPS = 32
S5_STATE = 64
N_META = 16
D_FF = 2816
EPS = 1e-6
LANES = 128
S5_BLOCK = 16
SLAB_GROUPS = LANES // S5_GROUP
FF_CHUNK = 256
TOKEN_TILE = 512
S5_ROW_BLOCKS = 128
GLA_SAMPLE_GROUP = 16
GLA_SAMPLE_SEQS = 64
VMEM_LIMIT = 48 * 1024 * 1024


def _rms(x, g):
    return x * lax.rsqrt(jnp.mean(x * x, axis=-1, keepdims=True) + EPS) * g


def _dot(a, b):
    return jnp.dot(a, b, preferred_element_type=F32)


def _dot_nt(a, b):
    return lax.dot_general(a, b, (((1,), (1,)), ((), ())), preferred_element_type=F32)


def _dot_tn(a, b):
    return lax.dot_general(a, b, (((0,), (0,)), ((), ())), preferred_element_type=F32)


def _const_spec(shape):
    zeros = (0,) * len(shape)
    return pl.BlockSpec(shape, lambda *_: zeros)


PREP_GROUPS = 8


def _cmul(a, b):
    return a[0] * b[0] - a[1] * b[1], a[0] * b[1] + a[1] * b[0]


def _unit_powers(c1, s1, expo, n_bits):
    acc = (jnp.ones_like(c1), jnp.zeros_like(c1))
    base = (c1, s1)
    squares = [base]
    for bit in range(n_bits):
        take = ((expo >> bit) & 1) == 1
        nxt = _cmul(acc, base)
        acc = (jnp.where(take, nxt[0], acc[0]), jnp.where(take, nxt[1], acc[1]))
        base = _cmul(base, base)
        squares.append(base)
    return acc, squares


def _s5_prep_kernel(arc_ref, aic_ref, air_ref, ldt_ref, br_ref, bi_ref, cr_ref, ci_ref,
                    mi16_ref, mi8_ref, in16_ref, in8_ref, out16_ref, out8_ref,
                    a16r_ref, a16i_ref, a8r_ref, a8i_ref):
    n, q = S5_STATE, S5_BLOCK
    w = q * S5_GROUP
    hp = lax.Precision.HIGHEST
    lane = lax.broadcasted_iota(jnp.int32, (n, w), 1)
    t_blk = lane >> 4
    eye = lax.broadcasted_iota(jnp.int32, (n, n), 0) == lax.broadcasted_iota(jnp.int32, (n, n), 1)
    to_col = lambda r: jnp.sum(jnp.where(eye, r, 0.0), axis=1, keepdims=True)
    zrows = lambda x: jnp.concatenate([x, jnp.zeros_like(x)], axis=0)
    lane_n = lax.broadcasted_iota(jnp.int32, (n, LANES), 1)
    rep = jnp.where((lax.broadcasted_iota(jnp.int32, (S5_GROUP, w), 1) & (S5_GROUP - 1))
                    == lax.broadcasted_iota(jnp.int32, (S5_GROUP, w), 0), 1.0, 0.0)
    mo_g, rev_g, hop_g = [], [], []
    for g in range(PREP_GROUPS):
        dt = jnp.exp(ldt_ref[g])
        ang_r = air_ref[g] * dt
        c1, s1 = to_col(jnp.cos(ang_r)), to_col(jnp.sin(ang_r))
        lam_re = jnp.minimum(arc_ref[g], -1e-4)
        lam_im = aic_ref[g]
        unit, squares = _unit_powers(c1, s1, t_blk, 4)
        pm = jnp.exp(t_blk.astype(F32) * (lam_re * dt))
        pk = (pm * unit[0], pm * unit[1])
        mag = jnp.exp(lam_re * dt)
        ab = (mag * c1, mag * s1)
        p1 = _cmul(pk, ab)
        ct = tuple(lax.dot_general(r[g], rep, (((0,), (0,)), ((), ())), precision=hp,
                                   preferred_element_type=F32) for r in (cr_ref, ci_ref))
        bt = tuple(jnp.dot(r[g], rep, precision=hp, preferred_element_type=F32) for r in (br_ref, bi_ref))
        g0 = _cmul(ct, pk)
        mo = _cmul(ct, p1)
        mo_g.append((mo[0], -mo[1]))
        den = lam_re * lam_re + lam_im * lam_im
        nr, ni = ab[0] - 1.0, ab[1]
        f = ((nr * lam_re + ni * lam_im) / den, (ni * lam_re - nr * lam_im) / den)
        e = _cmul(pk, _cmul(f, bt))
        et = [zrows(x).T for x in e]
        rev_g.append([jnp.concatenate([x[(q - 1 - s) * S5_GROUP:(q - s) * S5_GROUP] for s in range(q)], axis=0)
                      for x in et])
        hops = []
        for steps in (q, q // 2):
            m = jnp.exp(float(steps) * (lam_re * dt))
            u = squares[steps.bit_length() - 1]
            hops += [m * u[0], m * u[1]]
        cols = jnp.zeros((n, LANES), F32)
        for idx, hcol in enumerate(hops):
            cols = jnp.where(lane_n == idx, hcol, cols)
        hop_g.append(zrows(cols).T)
        t0 = (jnp.dot(et[0][:S5_GROUP, :n], g0[0], precision=hp, preferred_element_type=F32)
              - jnp.dot(et[1][:S5_GROUP, :n], g0[1], precision=hp, preferred_element_type=F32))
        lane_t = lax.broadcasted_iota(jnp.int32, t0.shape, 1) >> 4
        for s in range(q):
            blk = t0 if s == 0 else jnp.where(lane_t >= s, pltpu.roll(t0, S5_GROUP * s, 1), 0.0)
            mi16_ref[g, s * S5_GROUP:(s + 1) * S5_GROUP, :] = blk.astype(BF16)
            if s < q // 2:
                mi8_ref[g, s * S5_GROUP:(s + 1) * S5_GROUP, :] = blk[:, :w // 2].astype(BF16)

    half = lambda x: pltpu.roll(x, n, 1)
    for p in range(PREP_GROUPS // 2):
        g0, g1 = 2 * p, 2 * p + 1
        for c in range(2):
            part = slice(c * LANES, (c + 1) * LANES)
            lo, hi = rev_g[g0][c], half(rev_g[g1][c])
            in16_ref[p, :, part] = jnp.concatenate([lo, hi], axis=0).astype(BF16)
            in8_ref[p, :, part] = jnp.concatenate([lo[w // 2:], hi[w // 2:]], axis=0).astype(BF16)
            m0, m1 = mo_g[g0][c], mo_g[g1][c]
            z, zh = jnp.zeros_like(m0), jnp.zeros((n, w // 2), F32)
            out16_ref[p, part, :] = jnp.concatenate([jnp.concatenate([m0, z], axis=1),
                                                     jnp.concatenate([z, m1], axis=1)], axis=0).astype(BF16)
            out8_ref[p, part, :] = jnp.concatenate([jnp.concatenate([m0[:, :w // 2], zh], axis=1),
                                                    jnp.concatenate([zh, m1[:, :w // 2]], axis=1)],
                                                   axis=0).astype(BF16)
        hops = hop_g[g0] + half(hop_g[g1])
        a16r_ref[p] = hops[0:1]
        a16i_ref[p] = hops[1:2]
        a8r_ref[p] = hops[2:3]
        a8i_ref[p] = hops[3:4]


def _s5_prep(a_re, a_im, b_re, b_im, c_re, c_im, log_dt):
    g, n, j, q = S5_GROUPS, S5_STATE, S5_GROUP, S5_BLOCK
    w, h, pg = q * j, q * j // 2, PREP_GROUPS
    col = lambda t: t.reshape(g, n, 1)
    rowv = lambda t: t.reshape(g, 1, n)
    spec = lambda a, b_: pl.BlockSpec((pg, a, b_), lambda i: (i, 0, 0))
    pspec = lambda a, b_: pl.BlockSpec((pg // 2, a, b_), lambda i: (i, 0, 0))
    pair_shapes = [(2 * w, 2 * LANES), (2 * h, 2 * LANES), (2 * LANES, 2 * w), (2 * LANES, 2 * h)]
    outs = pl.pallas_call(
        _s5_prep_kernel,
        grid=(g // pg,),
        in_specs=[spec(n, 1), spec(n, 1), spec(1, n), spec(1, 1),
                  spec(n, j), spec(n, j), spec(j, n), spec(j, n)],
        out_specs=[spec(w, w), spec(h, h)] + [pspec(*s) for s in pair_shapes] + [pspec(1, LANES)] * 4,
        out_shape=[jax.ShapeDtypeStruct((g, w, w), BF16), jax.ShapeDtypeStruct((g, h, h), BF16)]
                  + [jax.ShapeDtypeStruct((g // 2,) + s, BF16) for s in pair_shapes]
                  + [jax.ShapeDtypeStruct((g // 2, 1, LANES), F32)] * 4,
        compiler_params=pltpu.CompilerParams(dimension_semantics=("parallel",)),
        name="s5_prep",
    )(col(a_re), col(a_im), rowv(a_im), log_dt.reshape(g, 1, 1),
      b_re, b_im, c_re, c_im)
    mi16, mi8, in16, in8, out16, out8, a16r, a16i, a8r, a8i = outs
    return {q: dict(m_intra=mi16, m_in=in16, m_out=out16, a_re=a16r, a_im=a16i),
            q // 2: dict(m_intra=mi8, m_in=in8, m_out=out8, a_re=a8r, a_im=a8i)}


def _lane_block_transpose(a):
    a = list(a)
    blk = lax.broadcasted_iota(jnp.int32, a[0].shape, 1) >> 4
    for d in (4, 2, 1):
        upper = (blk & d) != 0
        for r in range(8):
            if r & d:
                continue
            lo, hi = a[r], a[r + d]
            a[r] = jnp.where(upper, pltpu.roll(hi, 16 * d, 1), lo)
            a[r + d] = jnp.where(upper, hi, pltpu.roll(lo, LANES - 16 * d, 1))
    return a


def _in_proj_kernel(*refs, steps, n_tiles, tiles_per_seq):
    fused = tiles_per_seq is not None
    x_ref, gpre_ref, wa_ref, wb_ref, wgk_ref, bgk_ref = refs[:6]
    if fused:
        (s0_ref, g_ref, u_ref, o_ref, ug_ref, sfin_ref,
         u_scr, gk_scr, q_scr, k_scr, v_scr, s_scr) = refs[6:]
    else:
        q_ref, k_ref, v_ref, g_ref, u_ref, lg_ref, ug_ref, u_scr, gk_scr = refs[6:]
    i = pl.program_id(0)
    tm = x_ref.shape[0]

    def project(slot):
        h = _rms(x_ref[...], gpre_ref[...]).astype(BF16)
        proj = _dot(h, wa_ref[...])
        tail_proj = _dot(h, wb_ref[...])
        if fused:
            q_scr[slot] = proj[:, 0:256]
            k_scr[slot] = proj[:, 256:512]
            v_scr[slot] = proj[:, 512:1024].astype(BF16)
        else:
            q_ref[...] = proj[:, 0:256]
            k_ref[...] = proj[:, 256:512]
            v_ref[...] = proj[:, 512:1024]
        g_ref[...] = proj[:, 1024:1536]
        u = tail_proj[:, :D_S5]
        u_ref[...] = u
        for s in range(D_S5 // LANES):
            u_scr[slot, s] = u[:, s * LANES:(s + 1) * LANES]
        gk_scr[slot] = tail_proj[:, D_S5:]

    def tail(slot):
        z = _dot(gk_scr[slot].astype(BF16), wgk_ref[...]) + bgk_ref[...]
        lg = jax.nn.log_sigmoid(z) * (1.0 / GATE_NORM)
        if fused:
            tile = i - 1
            seq = tile // tiles_per_seq
            state = jnp.where(tile % tiles_per_seq == 0, s0_ref[seq], s_scr[...])
            masks = _gla_masks(1, GLA_CHUNK)
            for c in range(tm // GLA_CHUNK):
                rows = slice(c * GLA_CHUNK, (c + 1) * GLA_CHUNK)
                o, (state,) = _gla_group(q_scr[slot, rows, :], k_scr[slot, rows, :], v_scr[slot, rows, :],
                                         lg[rows], [state], masks, GLA_CHUNK)
                o_ref[rows, :] = o
            s_scr[...] = state
            sfin_ref[seq] = state
        else:
            lg_ref[...] = lg
        nr = tm // steps
        rt = min(nr, 16)
        for s in range(D_S5 // LANES):
            for hf in range(steps // 8):
                for r0 in range(0, nr, rt):
                    a = [u_scr[slot, s, pl.ds(r0 * steps + hf * 8 + t, rt, stride=steps), :] for t in range(8)]
                    per_group = _lane_block_transpose(a)
                    for g in range(SLAB_GROUPS):
                        ug_ref[s * SLAB_GROUPS + g, r0:r0 + rt, hf * LANES:(hf + 1) * LANES] = (
                            per_group[g].astype(BF16))

    @pl.when(i == 0)
    def _():
        if fused:
            s_scr[...] = jnp.zeros(s_scr.shape, F32)
        project(0)

    @pl.when((i > 0) & (i < n_tiles))
    def _():
        tail((i - 1) & 1)
        project(i & 1)

    @pl.when(i == n_tiles)
    def _():
        tail((i - 1) & 1)


def _in_proj(x, gpre, w_a, w_b, wgk_p, bgk, tm, steps, gla_s0=None, seq_len=None):
    t = x.shape[0]
    n = t // tm
    cur = lambda i: jnp.minimum(i, n - 1)
    prev = lambda i: jnp.maximum(i - 1, 0)
    row = lambda w, at: pl.BlockSpec((tm, w), lambda i: (at(i), 0))
    wg = steps * S5_GROUP
    ug_spec = pl.BlockSpec((S5_GROUPS, tm // steps, wg), lambda i: (0, prev(i), 0))
    ug_shape = jax.ShapeDtypeStruct((S5_GROUPS, t // steps, wg), BF16)
    f32 = lambda w: jax.ShapeDtypeStruct((t, w), F32)
    in_specs = [row(D_MODEL, cur), _const_spec((1, D_MODEL)), _const_spec(w_a.shape), _const_spec(w_b.shape),
                _const_spec((LANES, GLA_KDIM)), _const_spec((1, GLA_KDIM))]
    scratch = [pltpu.VMEM((2, D_S5 // LANES, tm, LANES), F32), pltpu.VMEM((2, tm, LANES), F32)]
    args = (x, gpre, w_a, w_b, wgk_p, bgk)
    if gla_s0 is None:
        tiles_per_seq = None
        out_specs = [row(w, cur) for w in (256, 256, 512, 512, 512)] + [row(GLA_KDIM, prev), ug_spec]
        out_shape = [f32(w) for w in (256, 256, 512, 512, 512)] + [f32(GLA_KDIM), ug_shape]
    else:
        tiles_per_seq = seq_len // tm
        assert seq_len % tm == 0 and tm % GLA_CHUNK == 0
        in_specs.append(_const_spec(gla_s0.shape))
        args += (gla_s0,)
        out_specs = [row(D_GLA, cur), row(D_S5, cur), row(D_GLA, prev), ug_spec, _const_spec(gla_s0.shape)]
        out_shape = [f32(D_GLA), f32(D_S5), f32(D_GLA), ug_shape, jax.ShapeDtypeStruct(gla_s0.shape, F32)]
        scratch += [pltpu.VMEM((2, tm, GLA_KDIM), F32), pltpu.VMEM((2, tm, GLA_KDIM), F32),
                    pltpu.VMEM((2, tm, D_GLA), BF16), pltpu.VMEM(gla_s0.shape[1:], F32)]
    return pl.pallas_call(
        functools.partial(_in_proj_kernel, steps=steps, n_tiles=n, tiles_per_seq=tiles_per_seq),
        grid=(n + 1,),
        in_specs=in_specs,
        out_specs=out_specs,
        out_shape=out_shape,
        scratch_shapes=scratch,
        compiler_params=pltpu.CompilerParams(dimension_semantics=("arbitrary",),
                                             vmem_limit_bytes=VMEM_LIMIT),
        name="in_proj" if gla_s0 is None else "in_proj_gla",
    )(*args)


def _gla_masks(sg, chunk):
    r = sg * chunk
    assert r >= LANES or sg == 1
    shift = chunk.bit_length() - 1
    hk = GLA_HEADS * r
    ri = lax.broadcasted_iota(jnp.int32, (r, r), 0)
    ci = lax.broadcasted_iota(jnp.int32, (r, r), 1)
    ri4 = lax.broadcasted_iota(jnp.int32, (r, hk), 0)
    ci4 = lax.broadcasted_iota(jnp.int32, (r, hk), 1) & (r - 1)
    stack_head = lax.broadcasted_iota(jnp.int32, (hk, 1), 0) >> (r.bit_length() - 1)
    k_head = lax.broadcasted_iota(jnp.int32, (1, GLA_KDIM), 1) >> 6
    v_head = lax.broadcasted_iota(jnp.int32, (1, D_GLA), 1) >> 7
    s_head = lax.broadcasted_iota(jnp.int32, (GLA_KDIM, 1), 0) >> 6
    return dict(
        tri_bf=jnp.where(((ri >> shift) == (ci >> shift)) & (ri >= ci), 1.0, 0.0).astype(BF16),
        causal4=((ri4 >> shift) == (ci4 >> shift)) & (ri4 >= ci4),
        k_diag=stack_head == k_head, v_diag=stack_head == v_head, s_diag=s_head == v_head)


def _gla_group(q, k, v_bf, lg, states, m, chunk):
    sg = len(states)
    r = sg * chunk
    tile4 = lambda t: jnp.concatenate([t] * GLA_HEADS, axis=0)
    lg_hi = lg.astype(BF16)
    lg_lo = (lg - lg_hi.astype(F32)).astype(BF16)
    b2 = _dot(m["tri_bf"], jnp.concatenate([lg_hi, lg_lo], axis=1))
    b = b2[:, :GLA_KDIM] + b2[:, GLA_KDIM:]
    lasts = [b[(i + 1) * chunk - 1:(i + 1) * chunk, :] for i in range(sg)]
    bl = jnp.concatenate([jnp.broadcast_to(t, (chunk, GLA_KDIM)) for t in lasts], axis=0)
    qd = (q * (GLA_DK ** -0.5) * jnp.exp(b)).astype(BF16)
    ki = (k * jnp.exp(-b)).astype(BF16)
    ke = k * jnp.exp(bl - b)

    ki_bd = jnp.where(m["k_diag"], tile4(ki), 0.0)
    att = jnp.where(m["causal4"], _dot_nt(qd, ki_bd), 0.0).astype(BF16)
    v_bd = jnp.where(m["v_diag"], tile4(v_bf), 0.0)
    o_intra = _dot(att, v_bd)

    if r < LANES:
        aug_t = jnp.concatenate([ke, jnp.broadcast_to(lasts[0], (LANES - r, GLA_KDIM))], axis=0).T
        ke_t, bl_t = aug_t, aug_t[:, r:]
    else:
        ke_t = ke.T
        bl_t = jnp.concatenate(lasts + [jnp.zeros((r - sg, GLA_KDIM), F32)], axis=0).T
    ke_t = ke_t.astype(BF16)

    outs, new_states = [], []
    for i, s_old in enumerate(states):
        rows = slice(i * chunk, (i + 1) * chunk)
        s_bd = jnp.where(m["s_diag"], jnp.concatenate([s_old.astype(BF16)] * GLA_HEADS, axis=1), 0.0)
        outs.append(o_intra[rows] + _dot(qd[rows], s_bd))
        upd = [_dot(ke_t[h * GLA_DK:(h + 1) * GLA_DK, rows], v_bf[rows, h * GLA_DV:(h + 1) * GLA_DV])
               for h in range(GLA_HEADS)]
        new_states.append(jnp.exp(bl_t[:, i:i + 1]) * s_old + jnp.concatenate(upd, axis=0))
    return (outs[0] if sg == 1 else jnp.concatenate(outs, axis=0)), new_states


def _gla_kernel(q_ref, k_ref, v_ref, lg_ref, s0_ref, o_ref, s_ref, *, bb, sg, chunk):
    r = sg * chunk

    @pl.when(pl.program_id(1) == 0)
    def _():
        s_ref[...] = s0_ref[...]

    masks = _gla_masks(sg, chunk)
    for gi in range(bb // sg):
        seqs = slice(gi * sg, (gi + 1) * sg)
        o, new = _gla_group(q_ref[seqs].reshape(r, GLA_KDIM), k_ref[seqs].reshape(r, GLA_KDIM),
                            v_ref[seqs].reshape(r, D_GLA).astype(BF16), lg_ref[seqs].reshape(r, GLA_KDIM),
                            [s_ref[gi * sg + i] for i in range(sg)], masks, chunk)
        o_ref[seqs] = o.reshape(sg, chunk, D_GLA)
        for i in range(sg):
            s_ref[gi * sg + i] = new[i]


def _gla(q, k, v, lg, s0, bb, sg, chunk):
    b, l, _ = q.shape
    blk = lambda w: pl.BlockSpec((bb, chunk, w), lambda i, c: (i, c, 0))
    sspec = pl.BlockSpec((bb, GLA_KDIM, GLA_DV), lambda i, c: (i, 0, 0))
    return pl.pallas_call(
        functools.partial(_gla_kernel, bb=bb, sg=sg, chunk=chunk),
        grid=(b // bb, l // chunk),
        in_specs=[blk(GLA_KDIM), blk(GLA_KDIM), blk(D_GLA), blk(GLA_KDIM), sspec],
        out_specs=[blk(D_GLA), sspec],
        out_shape=[jax.ShapeDtypeStruct((b, l, D_GLA), F32),
                   jax.ShapeDtypeStruct((b, GLA_KDIM, GLA_DV), F32)],
        compiler_params=pltpu.CompilerParams(dimension_semantics=("parallel", "arbitrary"),
                                             vmem_limit_bytes=VMEM_LIMIT),
        name=f"gla_c{chunk}",
    )(q, k, v, lg, s0)


def _s5_kernel(ug_ref, mi_ref, min_ref, mout_ref, ar_ref, ai_ref,
               h0r_ref, h0i_ref, yg_ref, hfr_ref, hfi_ref, vr_s, vi_s, hr_s, hi_s, str_s, sti_s, *, nb, cb):
    gs = SLAB_GROUPS // 2
    n_pairs = S5_GROUPS // 2
    w = ug_ref.shape[-1]
    rows = nb * cb
    batch = cb if nb == 1 else nb
    n_blocks = rows // batch
    interleave = nb > 1 and cb > 1
    first_pair = pl.program_id(0) * gs
    seq0 = pl.program_id(1) * batch if nb == 1 else 0

    def state_rows(p):
        return pl.ds(seq0 * n_pairs + first_pair + p, batch, stride=n_pairs)

    def load_state():
        for p in range(gs):
            str_s[p] = h0r_ref[state_rows(p), :]
            sti_s[p] = h0i_ref[state_rows(p), :]

    if nb == 1:
        load_state()
    else:
        pl.when(pl.program_id(1) == 0)(load_state)

    def to_scan_order(ref, g, val):
        if not interleave:
            ref[g] = val
        else:
            for b in range(nb):
                ref[g, pl.ds(b, cb, stride=nb), :] = val[b * cb:(b + 1) * cb]

    def from_scan_order(ref, g):
        if not interleave:
            return ref[g]
        return jnp.concatenate([ref[g, pl.ds(b, cb, stride=nb), :] for b in range(nb)], axis=0)

    ub = [ug_ref[g].reshape(rows, w) for g in range(2 * gs)]
    for g in range(gs):
        v = _dot(jnp.concatenate([ub[2 * g], ub[2 * g + 1]], axis=1), min_ref[g])
        to_scan_order(vr_s, g, v[:, :LANES])
        to_scan_order(vi_s, g, v[:, LANES:])
    ar = [ar_ref[g] for g in range(gs)]
    ai = [ai_ref[g] for g in range(gs)]

    def body(c, carry):
        rows = pl.ds(pl.multiple_of(c * batch, batch), batch)
        new = []
        for g in range(gs):
            hr, hi = carry[2 * g], carry[2 * g + 1]
            hr_s[g, rows, :] = hr
            hi_s[g, rows, :] = hi
            new.append(ar[g] * hr - ai[g] * hi + vr_s[g, rows, :])
            new.append(ar[g] * hi + ai[g] * hr + vi_s[g, rows, :])
        return tuple(new)

    init = tuple(ref[g] for g in range(gs) for ref in (str_s, sti_s))
    fin = lax.fori_loop(0, n_blocks, body, init, unroll=min(n_blocks, 4))
    for g in range(gs):
        str_s[g] = fin[2 * g]
        sti_s[g] = fin[2 * g + 1]
        hfr_ref[state_rows(g), :] = fin[2 * g]
        hfi_ref[state_rows(g), :] = fin[2 * g + 1]
        h_in = jnp.concatenate([from_scan_order(hr_s, g), from_scan_order(hi_s, g)], axis=1).astype(BF16)
        y_state = _dot(h_in, mout_ref[g])
        for k in range(2):
            y = _dot(ub[2 * g + k], mi_ref[2 * g + k]) + y_state[:, k * w:(k + 1) * w]
            yg_ref[2 * g + k] = y.reshape(yg_ref.shape[1:])


def _s5(ug, m, h0_re, h0_im, nb, cb):
    g_all, r, w = ug.shape
    nc = r // nb
    batch = cb if nb == 1 else nb
    n, gs = LANES, SLAB_GROUPS
    uspec = pl.BlockSpec((gs, nb, cb, w), lambda s, i: (s, 0, i, 0))
    per_g = lambda a, b_: pl.BlockSpec((gs, a, b_), lambda s, i: (s, 0, 0))
    per_p = lambda a, b_: pl.BlockSpec((gs // 2, a, b_), lambda s, i: (s, 0, 0))
    hspec = _const_spec(h0_re.shape)
    yg, hf_re, hf_im = pl.pallas_call(
        functools.partial(_s5_kernel, nb=nb, cb=cb),
        grid=(g_all // gs, nc // cb),
        in_specs=[uspec, per_g(w, w), per_p(2 * w, 2 * n), per_p(2 * n, 2 * w),
                  per_p(1, n), per_p(1, n), hspec, hspec],
        out_specs=[uspec, hspec, hspec],
        out_shape=[jax.ShapeDtypeStruct((g_all, nb, nc, w), F32),
                   jax.ShapeDtypeStruct(h0_re.shape, F32),
                   jax.ShapeDtypeStruct(h0_im.shape, F32)],
        scratch_shapes=[pltpu.VMEM((gs // 2, nb * cb, n), F32)] * 4 + [pltpu.VMEM((gs // 2, batch, n), F32)] * 2,
        compiler_params=pltpu.CompilerParams(dimension_semantics=("arbitrary", "arbitrary"),
                                             vmem_limit_bytes=VMEM_LIMIT),
        name=f"s5_w{w}",
    )(ug.reshape(g_all, nb, nc, w), m["m_intra"], m["m_in"], m["m_out"], m["a_re"], m["a_im"], h0_re, h0_im)
    return yg.reshape(g_all, r, w), hf_re, hf_im


def _mix_ffn_kernel(x_ref, o_ref, g_ref, yg_ref, u_ref, dsk_ref, gn_ref, wglu_ref, s5n_ref, wo_ref, gpost_ref,
                    gpre_ref, wg_ref, wu_ref, wd_ref, gpostf_ref, out_ref, y_scr, x1_scr, *, steps, n_tiles):
    i = pl.program_id(0)

    def mix_into(slot):
        nr = x_ref.shape[0] // steps
        rt = min(nr, 8)
        for s in range(D_S5 // LANES):
            for hf in range(steps // 8):
                for r0 in range(0, nr, rt):
                    a = [yg_ref[s * SLAB_GROUPS + g, r0:r0 + rt, hf * LANES:(hf + 1) * LANES]
                         for g in range(SLAB_GROUPS)]
                    per_step = _lane_block_transpose(a)
                    for t in range(8):
                        y_scr[s, pl.ds(r0 * steps + hf * 8 + t, rt, stride=steps), :] = per_step[t]
        y5 = jnp.concatenate([y_scr[s] for s in range(D_S5 // LANES)], axis=1) + dsk_ref[...] * u_ref[...]
        o = o_ref[...]
        gn = gn_ref[...]
        heads = []
        for h in range(GLA_HEADS):
            heads.append(_rms(o[:, h * GLA_DV:(h + 1) * GLA_DV], gn))
        og = jnp.concatenate(heads, axis=1) * jax.nn.silu(g_ref[...])
        y = jax.nn.gelu(y5)
        y = y * jax.nn.sigmoid(_dot(y.astype(BF16), wglu_ref[...]))
        y = _rms(y, s5n_ref[...])
        mix = _dot(og.astype(BF16), wo_ref[:D_GLA, :]) + _dot(y.astype(BF16), wo_ref[D_GLA:, :])
        x1_scr[slot] = x_ref[...] + _rms(mix, gpost_ref[...])

    def ffn_from(slot):
        x = x1_scr[slot]
        h = _rms(x, gpre_ref[...]).astype(BF16)
        acc = jnp.zeros(x.shape, F32)
        for c in range(D_FF // FF_CHUNK):
            cols = slice(c * FF_CHUNK, (c + 1) * FF_CHUNK)
            act = jax.nn.silu(_dot(h, wg_ref[:, cols])) * _dot(h, wu_ref[:, cols])
            acc = acc + _dot(act.astype(BF16), wd_ref[cols, :])
        out_ref[...] = x + _rms(acc, gpostf_ref[...])

    @pl.when(i == 0)
    def _():
        mix_into(0)

    @pl.when((i > 0) & (i < n_tiles))
    def _():
        ffn_from((i - 1) & 1)
        mix_into(i & 1)

    @pl.when(i == n_tiles)
    def _():
        ffn_from((i - 1) & 1)


def _mix_ffn(x, o, g, yg, u, dsk, gn, wglu, s5n, wo, gpost, gpre, wg, wu, wd, gpostf, tm, steps):
    t = x.shape[0]
    n = t // tm
    cur = lambda i: jnp.minimum(i, n - 1)
    row = lambda w: pl.BlockSpec((tm, w), lambda i: (cur(i), 0))
    fixed = lambda shape: pl.BlockSpec(shape, lambda i: (0,) * len(shape), pipeline_mode=pl.Buffered(1))
    return pl.pallas_call(
        functools.partial(_mix_ffn_kernel, steps=steps, n_tiles=n),
        grid=(n + 1,),
        in_specs=[row(D_MODEL), row(D_GLA), row(D_GLA),
                  pl.BlockSpec((S5_GROUPS, tm // steps, steps * S5_GROUP), lambda i: (0, cur(i), 0)),
                  row(D_S5), fixed((1, D_S5)),
                  fixed((1, GLA_DV)), fixed((D_S5, D_S5)), fixed((1, D_S5)),
                  fixed((D_GLA + D_S5, D_MODEL)), fixed((1, D_MODEL)),
                  fixed((1, D_MODEL)), fixed((D_MODEL, D_FF)), fixed((D_MODEL, D_FF)),
                  fixed((D_FF, D_MODEL)), fixed((1, D_MODEL))],
        out_specs=pl.BlockSpec((tm, D_MODEL), lambda i: (jnp.maximum(i - 1, 0), 0)),
        out_shape=jax.ShapeDtypeStruct((t, D_MODEL), F32),
        scratch_shapes=[pltpu.VMEM((D_S5 // LANES, tm, LANES), F32), pltpu.VMEM((2, tm, D_MODEL), F32)],
        compiler_params=pltpu.CompilerParams(dimension_semantics=("arbitrary",),
                                             vmem_limit_bytes=VMEM_LIMIT),
        name="mix_ffn",
    )(x, o, g, yg, u, dsk, gn, wglu, s5n, wo, gpost, gpre, wg, wu, wd, gpostf)


def kernel(x_prompt, x_sample, state_gla, state_s5_re, state_s5_im, meta_tokens, g_pre_mix, w_in, w_gk2, b_gk, gla_norm, s5_a_re, s5_a_im, s5_b_re, s5_b_im, s5_c_re, s5_c_im, s5_d, s5_log_dt, w_s5_glu, s5_norm, w_o, g_post_mix, g_pre_ffn, w_gate, w_up, w_down, g_post_ffn):
    assert g_pre_mix.shape[0] == 1, "single-layer step"
    bp, seq_p, _ = x_prompt.shape
    bs, seq_s, _ = x_sample.shape
    row = lambda t: t[0].reshape(1, -1)

    w = w_in[0]
    c3, c4 = 2 * GLA_KDIM + 2 * D_GLA, 2 * GLA_KDIM + 2 * D_GLA + GATE_RANK
    w_a = w[:, :c3].astype(BF16)
    w_b = jnp.concatenate([w[:, c4:].astype(BF16), w[:, c3:c4].astype(BF16),
                           jnp.zeros((D_MODEL, LANES - GATE_RANK), BF16)], axis=1)
    wgk_p = jnp.concatenate([w_gk2[0], jnp.zeros((LANES - GATE_RANK, GLA_KDIM), F32)], axis=0).astype(BF16)
    wo_bf = w_o[0].astype(BF16)
    s5m = _s5_prep(s5_a_re[0], s5_a_im[0], s5_b_re[0], s5_b_im[0], s5_c_re[0], s5_c_im[0], s5_log_dt[0])
    proj_w = (row(g_pre_mix), w_a, w_b, wgk_p, row(b_gk))

    def finish(x, o, g, yg, u, tm, steps):
        return _mix_ffn(x, o, g, yg, u, row(s5_d), row(gla_norm), w_s5_glu[0].astype(BF16), row(s5_norm),
                        wo_bf, row(g_post_mix), row(g_pre_ffn), w_gate[0].astype(BF16),
                        w_up[0].astype(BF16), w_down[0].astype(BF16), row(g_post_ffn), tm, steps)

    xm = jnp.broadcast_to(meta_tokens[None], (bp, N_META, D_MODEL)).reshape(bp * N_META, D_MODEL)
    q, k, v, _, _, lg, ug = _in_proj(xm, *proj_w, bp * N_META, S5_BLOCK)
    r3 = lambda t, b, l: t.reshape(b, l, t.shape[-1])
    _, s_meta = _gla(r3(q, bp, N_META), r3(k, bp, N_META), r3(v, bp, N_META), r3(lg, bp, N_META),
                     jnp.zeros((bp, GLA_KDIM, GLA_DV), F32), bp, bp, N_META)
    zh = jnp.zeros((bp * S5_GROUPS // 2, LANES), F32)
    _, hm_re, hm_im = _s5(ug, s5m[S5_BLOCK], zh, zh, 1, bp)

    xp = x_prompt.reshape(bp * seq_p, D_MODEL)
    g, u, o, ug, s_p = _in_proj(xp, *proj_w, TOKEN_TILE, S5_BLOCK, gla_s0=s_meta, seq_len=seq_p)
    yg, hp_re, hp_im = _s5(ug, s5m[S5_BLOCK], hm_re, hm_im, bp, S5_ROW_BLOCKS)
    y_prompt = finish(xp, o, g, yg, u, TOKEN_TILE, S5_BLOCK)

    xs = x_sample.reshape(bs * seq_s, D_MODEL)
    to_g = lambda t: t[0].reshape(bs * S5_GROUPS // 2, LANES)
    q, k, v, g, u, lg, ug = _in_proj(xs, *proj_w, TOKEN_TILE, seq_s)
    o, s_s = _gla(r3(q, bs, seq_s), r3(k, bs, seq_s), r3(v, bs, seq_s), r3(lg, bs, seq_s),
                  state_gla[0].reshape(bs, GLA_KDIM, GLA_DV), GLA_SAMPLE_SEQS, GLA_SAMPLE_GROUP, seq_s)
    yg, hs_re, hs_im = _s5(ug, s5m[seq_s], to_g(state_s5_re), to_g(state_s5_im), 1, bs)
    y_sample = finish(xs, o.reshape(bs * seq_s, D_GLA), g, yg, u, TOKEN_TILE, seq_s)

    gla_out = lambda s, b: s.reshape(1, b, GLA_HEADS, GLA_DK, GLA_DV)
    s5_out = lambda h: h.reshape(1, -1, S5_GROUPS, S5_STATE)
    return (y_prompt.reshape(bp, seq_p, D_MODEL), y_sample.reshape(bs, seq_s, D_MODEL),
            gla_out(s_p, bp), s5_out(hp_re), s5_out(hp_im),
            gla_out(s_s, bs), s5_out(hs_re), s5_out(hs_im))
```

```python
import functools

import jax
import jax.numpy as jnp
from jax import lax
from jax.experimental import pallas as pl
from jax.experimental.pallas import tpu as pltpu

F32 = jnp.float32
BF16 = jnp.bfloat16

D_MODEL = 1024
D_GLA = 512
GLA_HEADS = 4
GLA_DV = 128
GLA_DK = 64
GLA_KDIM = 256
GATE_RANK = 16
GATE_NORM = 16.0
GLA_CHUNK = 64
D_S5 = 512
S5_GROUP = 16
S5_GROUPS = 32
S5_STATE = 64
N_META = 16
D_FF = 2816
EPS = 1e-6
LANES = 128
S5_BLOCK = 16
SLAB_GROUPS = LANES // S5_GROUP
FF_CHUNK = 256
TOKEN_TILE = 512
S5_ROW_BLOCKS = 64
GLA_SAMPLE_GROUP = 16
GLA_SAMPLE_SEQS = 32
VMEM_LIMIT = 48 * 1024 * 1024


def _rms(x, g):
    return x * lax.rsqrt(jnp.mean(x * x, axis=-1, keepdims=True) + EPS) * g


def _dot(a, b):
    return jnp.dot(a, b, preferred_element_type=F32)


def _dot_nt(a, b):
    return lax.dot_general(a, b, (((1,), (1,)), ((), ())), preferred_element_type=F32)


def _dot_tn(a, b):
    return lax.dot_general(a, b, (((0,), (0,)), ((), ())), preferred_element_type=F32)


def _const_spec(shape):
    zeros = (0,) * len(shape)
    return pl.BlockSpec(shape, lambda *_: zeros)


PREP_GROUPS = 8


def _cmul(a, b):
    return a[0] * b[0] - a[1] * b[1], a[0] * b[1] + a[1] * b[0]


def _unit_powers(c1, s1, expo, n_bits):
    acc = (jnp.ones_like(c1), jnp.zeros_like(c1))
    base = (c1, s1)
    squares = [base]
    for bit in range(n_bits):
        take = ((expo >> bit) & 1) == 1
        nxt = _cmul(acc, base)
        acc = (jnp.where(take, nxt[0], acc[0]), jnp.where(take, nxt[1], acc[1]))
        base = _cmul(base, base)
        squares.append(base)
    return acc, squares


def _s5_prep_kernel(arc_ref, aic_ref, air_ref, ldt_ref, br_ref, bi_ref, cr_ref, ci_ref,
                    mi16_ref, mi8_ref, in16_ref, in8_ref, out16_ref, out8_ref,
                    a16r_ref, a16i_ref, a8r_ref, a8i_ref):
    n, q = S5_STATE, S5_BLOCK
    w = q * S5_GROUP
    hp = lax.Precision.HIGHEST
    lane = lax.broadcasted_iota(jnp.int32, (n, w), 1)
    t_blk = lane >> 4
    eye = lax.broadcasted_iota(jnp.int32, (n, n), 0) == lax.broadcasted_iota(jnp.int32, (n, n), 1)
    to_col = lambda r: jnp.sum(jnp.where(eye, r, 0.0), axis=1, keepdims=True)
    zrows = lambda x: jnp.concatenate([x, jnp.zeros_like(x)], axis=0)
    lane_n = lax.broadcasted_iota(jnp.int32, (n, LANES), 1)
    rep = jnp.where((lax.broadcasted_iota(jnp.int32, (S5_GROUP, w), 1) & (S5_GROUP - 1))
                    == lax.broadcasted_iota(jnp.int32, (S5_GROUP, w), 0), 1.0, 0.0)
    mo_g, rev_g, hop_g = [], [], []
    for g in range(PREP_GROUPS):
        dt = jnp.exp(ldt_ref[g])
        ang_r = air_ref[g] * dt
        c1, s1 = to_col(jnp.cos(ang_r)), to_col(jnp.sin(ang_r))
        lam_re = jnp.minimum(arc_ref[g], -1e-4)
        lam_im = aic_ref[g]
        unit, squares = _unit_powers(c1, s1, t_blk, 4)
        pm = jnp.exp(t_blk.astype(F32) * (lam_re * dt))
        pk = (pm * unit[0], pm * unit[1])
        mag = jnp.exp(lam_re * dt)
        ab = (mag * c1, mag * s1)
        p1 = _cmul(pk, ab)
        ct = tuple(lax.dot_general(r[g], rep, (((0,), (0,)), ((), ())), precision=hp,
                                   preferred_element_type=F32) for r in (cr_ref, ci_ref))
        bt = tuple(jnp.dot(r[g], rep, precision=hp, preferred_element_type=F32) for r in (br_ref, bi_ref))
        g0 = _cmul(ct, pk)
        mo = _cmul(ct, p1)
        mo_g.append((mo[0], -mo[1]))
        den = lam_re * lam_re + lam_im * lam_im
        nr, ni = ab[0] - 1.0, ab[1]
        f = ((nr * lam_re + ni * lam_im) / den, (ni * lam_re - nr * lam_im) / den)
        e = _cmul(pk, _cmul(f, bt))
        et = [zrows(x).T for x in e]
        rev_g.append([jnp.concatenate([x[(q - 1 - s) * S5_GROUP:(q - s) * S5_GROUP] for s in range(q)], axis=0)
                      for x in et])
        hops = []
        for steps in (q, q // 2):
            m = jnp.exp(float(steps) * (lam_re * dt))
            u = squares[steps.bit_length() - 1]
            hops += [m * u[0], m * u[1]]
        cols = jnp.zeros((n, LANES), F32)
        for idx, hcol in enumerate(hops):
            cols = jnp.where(lane_n == idx, hcol, cols)
        hop_g.append(zrows(cols).T)
        t0 = (jnp.dot(et[0][:S5_GROUP, :n], g0[0], precision=hp, preferred_element_type=F32)
              - jnp.dot(et[1][:S5_GROUP, :n], g0[1], precision=hp, preferred_element_type=F32))
        lane_t = lax.broadcasted_iota(jnp.int32, t0.shape, 1) >> 4
        for s in range(q):
            blk = t0 if s == 0 else jnp.where(lane_t >= s, pltpu.roll(t0, S5_GROUP * s, 1), 0.0)
            mi16_ref[g, s * S5_GROUP:(s + 1) * S5_GROUP, :] = blk.astype(BF16)
            if s < q // 2:
                mi8_ref[g, s * S5_GROUP:(s + 1) * S5_GROUP, :] = blk[:, :w // 2].astype(BF16)

    half = lambda x: pltpu.roll(x, n, 1)
    for p in range(PREP_GROUPS // 2):
        g0, g1 = 2 * p, 2 * p + 1
        for c in range(2):
            part = slice(c * LANES, (c + 1) * LANES)
            lo, hi = rev_g[g0][c], half(rev_g[g1][c])
            in16_ref[p, :, part] = jnp.concatenate([lo, hi], axis=0).astype(BF16)
            in8_ref[p, :, part] = jnp.concatenate([lo[w // 2:], hi[w // 2:]], axis=0).astype(BF16)
            m0, m1 = mo_g[g0][c], mo_g[g1][c]
            z, zh = jnp.zeros_like(m0), jnp.zeros((n, w // 2), F32)
            out16_ref[p, part, :] = jnp.concatenate([jnp.concatenate([m0, z], axis=1),
                                                     jnp.concatenate([z, m1], axis=1)], axis=0).astype(BF16)
            out8_ref[p, part, :] = jnp.concatenate([jnp.concatenate([m0[:, :w // 2], zh], axis=1),
                                                    jnp.concatenate([zh, m1[:, :w // 2]], axis=1)],
                                                   axis=0).astype(BF16)
        hops = hop_g[g0] + half(hop_g[g1])
        a16r_ref[p] = hops[0:1]
        a16i_ref[p] = hops[1:2]
        a8r_ref[p] = hops[2:3]
        a8i_ref[p] = hops[3:4]


def _s5_prep(a_re, a_im, b_re, b_im, c_re, c_im, log_dt):
    g, n, j, q = S5_GROUPS, S5_STATE, S5_GROUP, S5_BLOCK
    w, h, pg = q * j, q * j // 2, PREP_GROUPS
    col = lambda t: t.reshape(g, n, 1)
    rowv = lambda t: t.reshape(g, 1, n)
    spec = lambda a, b_: pl.BlockSpec((pg, a, b_), lambda i: (i, 0, 0))
    pspec = lambda a, b_: pl.BlockSpec((pg // 2, a, b_), lambda i: (i, 0, 0))
    pair_shapes = [(2 * w, 2 * LANES), (2 * h, 2 * LANES), (2 * LANES, 2 * w), (2 * LANES, 2 * h)]
    outs = pl.pallas_call(
        _s5_prep_kernel,
        grid=(g // pg,),
        in_specs=[spec(n, 1), spec(n, 1), spec(1, n), spec(1, 1),
                  spec(n, j), spec(n, j), spec(j, n), spec(j, n)],
        out_specs=[spec(w, w), spec(h, h)] + [pspec(*s) for s in pair_shapes] + [pspec(1, LANES)] * 4,
        out_shape=[jax.ShapeDtypeStruct((g, w, w), BF16), jax.ShapeDtypeStruct((g, h, h), BF16)]
                  + [jax.ShapeDtypeStruct((g // 2,) + s, BF16) for s in pair_shapes]
                  + [jax.ShapeDtypeStruct((g // 2, 1, LANES), F32)] * 4,
        compiler_params=pltpu.CompilerParams(dimension_semantics=("parallel",)),
        name="s5_prep",
    )(col(a_re), col(a_im), rowv(a_im), log_dt.reshape(g, 1, 1),
      b_re, b_im, c_re, c_im)
    mi16, mi8, in16, in8, out16, out8, a16r, a16i, a8r, a8i = outs
    return {q: dict(m_intra=mi16, m_in=in16, m_out=out16, a_re=a16r, a_im=a16i),
            q // 2: dict(m_intra=mi8, m_in=in8, m_out=out8, a_re=a8r, a_im=a8i)}


def _lane_block_transpose(a):
    a = list(a)
    blk = lax.broadcasted_iota(jnp.int32, a[0].shape, 1) >> 4
    for d in (4, 2, 1):
        upper = (blk & d) != 0
        for r in range(8):
            if r & d:
                continue
            lo, hi = a[r], a[r + d]
            a[r] = jnp.where(upper, pltpu.roll(hi, 16 * d, 1), lo)
            a[r + d] = jnp.where(upper, hi, pltpu.roll(lo, LANES - 16 * d, 1))
    return a


def _in_proj_kernel(*refs, steps, n_tiles, tiles_per_seq):
    fused = tiles_per_seq is not None
    x_ref, gpre_ref, wa_ref, wb_ref, wgk_ref, bgk_ref = refs[:6]
    if fused:
        (s0_ref, g_ref, u_ref, o_ref, ug_ref, sfin_ref,
         u_scr, gk_scr, q_scr, k_scr, v_scr, s_scr) = refs[6:]
    else:
        q_ref, k_ref, v_ref, g_ref, u_ref, lg_ref, ug_ref, u_scr, gk_scr = refs[6:]
    i = pl.program_id(0)
    tm = x_ref.shape[0]

    def project(slot):
        h = _rms(x_ref[...], gpre_ref[...]).astype(BF16)
        proj = _dot(h, wa_ref[...])
        tail_proj = _dot(h, wb_ref[...])
        if fused:
            q_scr[slot] = proj[:, 0:256]
            k_scr[slot] = proj[:, 256:512]
            v_scr[slot] = proj[:, 512:1024].astype(BF16)
        else:
            q_ref[...] = proj[:, 0:256]
            k_ref[...] = proj[:, 256:512]
            v_ref[...] = proj[:, 512:1024]
        g_ref[...] = proj[:, 1024:1536]
        u = tail_proj[:, :D_S5]
        u_ref[...] = u
        for s in range(D_S5 // LANES):
            u_scr[slot, s] = u[:, s * LANES:(s + 1) * LANES]
        gk_scr[slot] = tail_proj[:, D_S5:]

    def tail(slot):
        z = _dot(gk_scr[slot].astype(BF16), wgk_ref[...]) + bgk_ref[...]
        lg = jax.nn.log_sigmoid(z) * (1.0 / GATE_NORM)
        if fused:
            tile = i - 1
            seq = tile // tiles_per_seq
            state = jnp.where(tile % tiles_per_seq == 0, s0_ref[seq], s_scr[...])
            masks = _gla_masks(1, GLA_CHUNK)
            for c in range(tm // GLA_CHUNK):
                rows = slice(c * GLA_CHUNK, (c + 1) * GLA_CHUNK)
                o, (state,) = _gla_group(q_scr[slot, rows, :], k_scr[slot, rows, :], v_scr[slot, rows, :],
                                         lg[rows], [state], masks, GLA_CHUNK)
                o_ref[rows, :] = o
            s_scr[...] = state
            sfin_ref[seq] = state
        else:
            lg_ref[...] = lg
        nr = tm // steps
        rt = min(nr, 16)
        for s in range(D_S5 // LANES):
            for hf in range(steps // 8):
                for r0 in range(0, nr, rt):
                    a = [u_scr[slot, s, pl.ds(r0 * steps + hf * 8 + t, rt, stride=steps), :] for t in range(8)]
                    per_group = _lane_block_transpose(a)
                    for g in range(SLAB_GROUPS):
                        ug_ref[s * SLAB_GROUPS + g, r0:r0 + rt, hf * LANES:(hf + 1) * LANES] = (
                            per_group[g].astype(BF16))

    @pl.when(i == 0)
    def _():
        if fused:
            s_scr[...] = jnp.zeros(s_scr.shape, F32)
        project(0)

    @pl.when((i > 0) & (i < n_tiles))
    def _():
        tail((i - 1) & 1)
        project(i & 1)

    @pl.when(i == n_tiles)
    def _():
        tail((i - 1) & 1)


def _in_proj(x, gpre, w_a, w_b, wgk_p, bgk, tm, steps, gla_s0=None, seq_len=None):
    t = x.shape[0]
    n = t // tm
    cur = lambda i: jnp.minimum(i, n - 1)
    prev = lambda i: jnp.maximum(i - 1, 0)
    row = lambda w, at: pl.BlockSpec((tm, w), lambda i: (at(i), 0))
    wg = steps * S5_GROUP
    ug_spec = pl.BlockSpec((S5_GROUPS, tm // steps, wg), lambda i: (0, prev(i), 0))
    ug_shape = jax.ShapeDtypeStruct((S5_GROUPS, t // steps, wg), BF16)
    f32 = lambda w: jax.ShapeDtypeStruct((t, w), F32)
    in_specs = [row(D_MODEL, cur), _const_spec((1, D_MODEL)), _const_spec(w_a.shape), _const_spec(w_b.shape),
                _const_spec((LANES, GLA_KDIM)), _const_spec((1, GLA_KDIM))]
    scratch = [pltpu.VMEM((2, D_S5 // LANES, tm, LANES), F32), pltpu.VMEM((2, tm, LANES), F32)]
    args = (x, gpre, w_a, w_b, wgk_p, bgk)
    if gla_s0 is None:
        tiles_per_seq = None
        out_specs = [row(w, cur) for w in (256, 256, 512, 512, 512)] + [row(GLA_KDIM, prev), ug_spec]
        out_shape = [f32(w) for w in (256, 256, 512, 512, 512)] + [f32(GLA_KDIM), ug_shape]
    else:
        tiles_per_seq = seq_len // tm
        assert seq_len % tm == 0 and tm % GLA_CHUNK == 0
        in_specs.append(_const_spec(gla_s0.shape))
        args += (gla_s0,)
        out_specs = [row(D_GLA, cur), row(D_S5, cur), row(D_GLA, prev), ug_spec, _const_spec(gla_s0.shape)]
        out_shape = [f32(D_GLA), f32(D_S5), f32(D_GLA), ug_shape, jax.ShapeDtypeStruct(gla_s0.shape, F32)]
        scratch += [pltpu.VMEM((2, tm, GLA_KDIM), F32), pltpu.VMEM((2, tm, GLA_KDIM), F32),
                    pltpu.VMEM((2, tm, D_GLA), BF16), pltpu.VMEM(gla_s0.shape[1:], F32)]
    return pl.pallas_call(
        functools.partial(_in_proj_kernel, steps=steps, n_tiles=n, tiles_per_seq=tiles_per_seq),
        grid=(n + 1,),
        in_specs=in_specs,
        out_specs=out_specs,
        out_shape=out_shape,
        scratch_shapes=scratch,
        compiler_params=pltpu.CompilerParams(dimension_semantics=("arbitrary",),
                                             vmem_limit_bytes=VMEM_LIMIT),
        name="in_proj" if gla_s0 is None else "in_proj_gla",
    )(*args)


def _gla_masks(sg, chunk):
    r = sg * chunk
    assert r >= LANES or sg == 1
    shift = chunk.bit_length() - 1
    hk = GLA_HEADS * r
    ri = lax.broadcasted_iota(jnp.int32, (r, r), 0)
    ci = lax.broadcasted_iota(jnp.int32, (r, r), 1)
    ri4 = lax.broadcasted_iota(jnp.int32, (r, hk), 0)
    ci4 = lax.broadcasted_iota(jnp.int32, (r, hk), 1) & (r - 1)
    stack_head = lax.broadcasted_iota(jnp.int32, (hk, 1), 0) >> (r.bit_length() - 1)
    k_head = lax.broadcasted_iota(jnp.int32, (1, GLA_KDIM), 1) >> 6
    v_head = lax.broadcasted_iota(jnp.int32, (1, D_GLA), 1) >> 7
    s_head = lax.broadcasted_iota(jnp.int32, (GLA_KDIM, 1), 0) >> 6
    return dict(
        tri_bf=jnp.where(((ri >> shift) == (ci >> shift)) & (ri >= ci), 1.0, 0.0).astype(BF16),
        causal4=((ri4 >> shift) == (ci4 >> shift)) & (ri4 >= ci4),
        k_diag=stack_head == k_head, v_diag=stack_head == v_head, s_diag=s_head == v_head)


def _gla_group(q, k, v_bf, lg, states, m, chunk):
    sg = len(states)
    r = sg * chunk
    tile4 = lambda t: jnp.concatenate([t] * GLA_HEADS, axis=0)
    lg_hi = lg.astype(BF16)
    lg_lo = (lg - lg_hi.astype(F32)).astype(BF16)
    b2 = _dot(m["tri_bf"], jnp.concatenate([lg_hi, lg_lo], axis=1))
    b = b2[:, :GLA_KDIM] + b2[:, GLA_KDIM:]
    lasts = [b[(i + 1) * chunk - 1:(i + 1) * chunk, :] for i in range(sg)]
    bl = jnp.concatenate([jnp.broadcast_to(t, (chunk, GLA_KDIM)) for t in lasts], axis=0)
    qd = (q * (GLA_DK ** -0.5) * jnp.exp(b)).astype(BF16)
    ki = (k * jnp.exp(-b)).astype(BF16)
    ke = k * jnp.exp(bl - b)

    ki_bd = jnp.where(m["k_diag"], tile4(ki), 0.0)
    att = jnp.where(m["causal4"], _dot_nt(qd, ki_bd), 0.0).astype(BF16)
    v_bd = jnp.where(m["v_diag"], tile4(v_bf), 0.0)
    o_intra = _dot(att, v_bd)

    if r < LANES:
        aug_t = jnp.concatenate([ke, jnp.broadcast_to(lasts[0], (LANES - r, GLA_KDIM))], axis=0).T
        ke_t, bl_t = aug_t, aug_t[:, r:]
    else:
        ke_t = ke.T
        bl_t = jnp.concatenate(lasts + [jnp.zeros((r - sg, GLA_KDIM), F32)], axis=0).T
    ke_t = ke_t.astype(BF16)

    outs, new_states = [], []
    for i, s_old in enumerate(states):
        rows = slice(i * chunk, (i + 1) * chunk)
        s_bd = jnp.where(m["s_diag"], jnp.concatenate([s_old.astype(BF16)] * GLA_HEADS, axis=1), 0.0)
        outs.append(o_intra[rows] + _dot(qd[rows], s_bd))
        upd = [_dot(ke_t[h * GLA_DK:(h + 1) * GLA_DK, rows], v_bf[rows, h * GLA_DV:(h + 1) * GLA_DV])
               for h in range(GLA_HEADS)]
        new_states.append(jnp.exp(bl_t[:, i:i + 1]) * s_old + jnp.concatenate(upd, axis=0))
    return (outs[0] if sg == 1 else jnp.concatenate(outs, axis=0)), new_states


def _gla_kernel(q_ref, k_ref, v_ref, lg_ref, s0_ref, o_ref, s_ref, *, bb, sg, chunk):
    r = sg * chunk

    @pl.when(pl.program_id(1) == 0)
    def _():
        s_ref[...] = s0_ref[...]

    masks = _gla_masks(sg, chunk)
    for gi in range(bb // sg):
        seqs = slice(gi * sg, (gi + 1) * sg)
        o, new = _gla_group(q_ref[seqs].reshape(r, GLA_KDIM), k_ref[seqs].reshape(r, GLA_KDIM),
                            v_ref[seqs].reshape(r, D_GLA).astype(BF16), lg_ref[seqs].reshape(r, GLA_KDIM),
                            [s_ref[gi * sg + i] for i in range(sg)], masks, chunk)
        o_ref[seqs] = o.reshape(sg, chunk, D_GLA)
        for i in range(sg):
            s_ref[gi * sg + i] = new[i]


def _gla(q, k, v, lg, s0, bb, sg, chunk):
    b, l, _ = q.shape
    blk = lambda w: pl.BlockSpec((bb, chunk, w), lambda i, c: (i, c, 0))
    sspec = pl.BlockSpec((bb, GLA_KDIM, GLA_DV), lambda i, c: (i, 0, 0))
    return pl.pallas_call(
        functools.partial(_gla_kernel, bb=bb, sg=sg, chunk=chunk),
        grid=(b // bb, l // chunk),
        in_specs=[blk(GLA_KDIM), blk(GLA_KDIM), blk(D_GLA), blk(GLA_KDIM), sspec],
        out_specs=[blk(D_GLA), sspec],
        out_shape=[jax.ShapeDtypeStruct((b, l, D_GLA), F32),
                   jax.ShapeDtypeStruct((b, GLA_KDIM, GLA_DV), F32)],
        compiler_params=pltpu.CompilerParams(dimension_semantics=("parallel", "arbitrary"),
                                             vmem_limit_bytes=VMEM_LIMIT),
        name=f"gla_c{chunk}",
    )(q, k, v, lg, s0)


def _s5_kernel(ug_ref, mi_ref, min_ref, mout_ref, ar_ref, ai_ref,
               h0r_ref, h0i_ref, yg_ref, hfr_ref, hfi_ref, vr_s, vi_s, hr_s, hi_s, str_s, sti_s, *, nb, cb):
    gs = SLAB_GROUPS // 2
    n_pairs = S5_GROUPS // 2
    w = ug_ref.shape[-1]
    rows = nb * cb
    batch = cb if nb == 1 else nb
    n_blocks = rows // batch
    interleave = nb > 1 and cb > 1
    first_pair = pl.program_id(0) * gs
    seq0 = pl.program_id(1) * batch if nb == 1 else 0

    def state_rows(p):
        return pl.ds(seq0 * n_pairs + first_pair + p, batch, stride=n_pairs)

    def load_state():
        for p in range(gs):
            str_s[p] = h0r_ref[state_rows(p), :]
            sti_s[p] = h0i_ref[state_rows(p), :]

    if nb == 1:
        load_state()
    else:
        pl.when(pl.program_id(1) == 0)(load_state)

    def to_scan_order(ref, g, val):
        if not interleave:
            ref[g] = val
        else:
            for b in range(nb):
                ref[g, pl.ds(b, cb, stride=nb), :] = val[b * cb:(b + 1) * cb]

    def from_scan_order(ref, g):
        if not interleave:
            return ref[g]
        return jnp.concatenate([ref[g, pl.ds(b, cb, stride=nb), :] for b in range(nb)], axis=0)

    ub = [ug_ref[g].reshape(rows, w) for g in range(2 * gs)]
    for g in range(gs):
        v = _dot(jnp.concatenate([ub[2 * g], ub[2 * g + 1]], axis=1), min_ref[g])
        to_scan_order(vr_s, g, v[:, :LANES])
        to_scan_order(vi_s, g, v[:, LANES:])
    ar = [ar_ref[g] for g in range(gs)]
    ai = [ai_ref[g] for g in range(gs)]

    def body(c, carry):
        rows = pl.ds(pl.multiple_of(c * batch, batch), batch)
        new = []
        for g in range(gs):
            hr, hi = carry[2 * g], carry[2 * g + 1]
            hr_s[g, rows, :] = hr
            hi_s[g, rows, :] = hi
            new.append(ar[g] * hr - ai[g] * hi + vr_s[g, rows, :])
            new.append(ar[g] * hi + ai[g] * hr + vi_s[g, rows, :])
        return tuple(new)

    init = tuple(ref[g] for g in range(gs) for ref in (str_s, sti_s))
    fin = lax.fori_loop(0, n_blocks, body, init, unroll=min(n_blocks, 4))
    for g in range(gs):
        str_s[g] = fin[2 * g]
        sti_s[g] = fin[2 * g + 1]
        hfr_ref[state_rows(g), :] = fin[2 * g]
        hfi_ref[state_rows(g), :] = fin[2 * g + 1]
        h_in = jnp.concatenate([from_scan_order(hr_s, g), from_scan_order(hi_s, g)], axis=1).astype(BF16)
        y_state = _dot(h_in, mout_ref[g])
        for k in range(2):
            y = _dot(ub[2 * g + k], mi_ref[2 * g + k]) + y_state[:, k * w:(k + 1) * w]
            yg_ref[2 * g + k] = y.reshape(yg_ref.shape[1:])


def _s5(ug, m, h0_re, h0_im, nb, cb):
    g_all, r, w = ug.shape
    nc = r // nb
    batch = cb if nb == 1 else nb
    n, gs = LANES, SLAB_GROUPS
    uspec = pl.BlockSpec((gs, nb, cb, w), lambda s, i: (s, 0, i, 0))
    per_g = lambda a, b_: pl.BlockSpec((gs, a, b_), lambda s, i: (s, 0, 0))
    per_p = lambda a, b_: pl.BlockSpec((gs // 2, a, b_), lambda s, i: (s, 0, 0))
    hspec = _const_spec(h0_re.shape)
    yg, hf_re, hf_im = pl.pallas_call(
        functools.partial(_s5_kernel, nb=nb, cb=cb),
        grid=(g_all // gs, nc // cb),
        in_specs=[uspec, per_g(w, w), per_p(2 * w, 2 * n), per_p(2 * n, 2 * w),
                  per_p(1, n), per_p(1, n), hspec, hspec],
        out_specs=[uspec, hspec, hspec],
        out_shape=[jax.ShapeDtypeStruct((g_all, nb, nc, w), F32),
                   jax.ShapeDtypeStruct(h0_re.shape, F32),
                   jax.ShapeDtypeStruct(h0_im.shape, F32)],
        scratch_shapes=[pltpu.VMEM((gs // 2, nb * cb, n), F32)] * 4 + [pltpu.VMEM((gs // 2, batch, n), F32)] * 2,
        compiler_params=pltpu.CompilerParams(dimension_semantics=("arbitrary", "arbitrary"),
                                             vmem_limit_bytes=VMEM_LIMIT),
        name=f"s5_w{w}",
    )(ug.reshape(g_all, nb, nc, w), m["m_intra"], m["m_in"], m["m_out"], m["a_re"], m["a_im"], h0_re, h0_im)
    return yg.reshape(g_all, r, w), hf_re, hf_im


def _mix_ffn_kernel(x_ref, o_ref, g_ref, yg_ref, u_ref, dsk_ref, gn_ref, wglu_ref, s5n_ref, wo_ref, gpost_ref,
                    gpre_ref, wg_ref, wu_ref, wd_ref, gpostf_ref, out_ref, y_scr, x1_scr, *, steps, n_tiles):
    i = pl.program_id(0)

    def mix_into(slot):
        nr = x_ref.shape[0] // steps
        rt = min(nr, 8)
        for s in range(D_S5 // LANES):
            for hf in range(steps // 8):
                for r0 in range(0, nr, rt):
                    a = [yg_ref[s * SLAB_GROUPS + g, r0:r0 + rt, hf * LANES:(hf + 1) * LANES]
                         for g in range(SLAB_GROUPS)]
                    per_step = _lane_block_transpose(a)
                    for t in range(8):
                        y_scr[s, pl.ds(r0 * steps + hf * 8 + t, rt, stride=steps), :] = per_step[t]
        y5 = jnp.concatenate([y_scr[s] for s in range(D_S5 // LANES)], axis=1) + dsk_ref[...] * u_ref[...]
        o = o_ref[...]
        gn = gn_ref[...]
        heads = []
        for h in range(GLA_HEADS):
            heads.append(_rms(o[:, h * GLA_DV:(h + 1) * GLA_DV], gn))
        og = jnp.concatenate(heads, axis=1) * jax.nn.silu(g_ref[...])
        y = jax.nn.gelu(y5)
        y = y * jax.nn.sigmoid(_dot(y.astype(BF16), wglu_ref[...]))
        y = _rms(y, s5n_ref[...])
        mix = _dot(og.astype(BF16), wo_ref[:D_GLA, :]) + _dot(y.astype(BF16), wo_ref[D_GLA:, :])
        x1_scr[slot] = x_ref[...] + _rms(mix, gpost_ref[...])

    def ffn_from(slot):
        x = x1_scr[slot]
        h = _rms(x, gpre_ref[...]).astype(BF16)
        acc = jnp.zeros(x.shape, F32)
        for c in range(D_FF // FF_CHUNK):
            cols = slice(c * FF_CHUNK, (c + 1) * FF_CHUNK)
            act = jax.nn.silu(_dot(h, wg_ref[:, cols])) * _dot(h, wu_ref[:, cols])
            acc = acc + _dot(act.astype(BF16), wd_ref[cols, :])
        out_ref[...] = x + _rms(acc, gpostf_ref[...])

    @pl.when(i == 0)
    def _():
        mix_into(0)

    @pl.when((i > 0) & (i < n_tiles))
    def _():
        ffn_from((i - 1) & 1)
        mix_into(i & 1)

    @pl.when(i == n_tiles)
    def _():
        ffn_from((i - 1) & 1)


def _mix_ffn(x, o, g, yg, u, dsk, gn, wglu, s5n, wo, gpost, gpre, wg, wu, wd, gpostf, tm, steps):
    t = x.shape[0]
    n = t // tm
    cur = lambda i: jnp.minimum(i, n - 1)
    row = lambda w: pl.BlockSpec((tm, w), lambda i: (cur(i), 0))
    fixed = lambda shape: pl.BlockSpec(shape, lambda i: (0,) * len(shape), pipeline_mode=pl.Buffered(1))
    return pl.pallas_call(
        functools.partial(_mix_ffn_kernel, steps=steps, n_tiles=n),
        grid=(n + 1,),
        in_specs=[row(D_MODEL), row(D_GLA), row(D_GLA),
                  pl.BlockSpec((S5_GROUPS, tm // steps, steps * S5_GROUP), lambda i: (0, cur(i), 0)),
                  row(D_S5), fixed((1, D_S5)),
                  fixed((1, GLA_DV)), fixed((D_S5, D_S5)), fixed((1, D_S5)),
                  fixed((D_GLA + D_S5, D_MODEL)), fixed((1, D_MODEL)),
                  fixed((1, D_MODEL)), fixed((D_MODEL, D_FF)), fixed((D_MODEL, D_FF)),
                  fixed((D_FF, D_MODEL)), fixed((1, D_MODEL))],
        out_specs=pl.BlockSpec((tm, D_MODEL), lambda i: (jnp.maximum(i - 1, 0), 0)),
        out_shape=jax.ShapeDtypeStruct((t, D_MODEL), F32),
        scratch_shapes=[pltpu.VMEM((D_S5 // LANES, tm, LANES), F32), pltpu.VMEM((2, tm, D_MODEL), F32)],
        compiler_params=pltpu.CompilerParams(dimension_semantics=("arbitrary",),
                                             vmem_limit_bytes=VMEM_LIMIT),
        name="mix_ffn",
    )(x, o, g, yg, u, dsk, gn, wglu, s5n, wo, gpost, gpre, wg, wu, wd, gpostf)


def kernel(x_prompt, x_sample, state_gla, state_s5_re, state_s5_im, meta_tokens, g_pre_mix, w_in, w_gk2, b_gk, gla_norm, s5_a_re, s5_a_im, s5_b_re, s5_b_im, s5_c_re, s5_c_im, s5_d, s5_log_dt, w_s5_glu, s5_norm, w_o, g_post_mix, g_pre_ffn, w_gate, w_up, w_down, g_post_ffn):
    assert g_pre_mix.shape[0] == 1, "single-layer step"
    bp, seq_p, _ = x_prompt.shape
    bs, seq_s, _ = x_sample.shape
    row = lambda t: t[0].reshape(1, -1)

    w = w_in[0]
    c3, c4 = 2 * GLA_KDIM + 2 * D_GLA, 2 * GLA_KDIM + 2 * D_GLA + GATE_RANK
    w_a = w[:, :c3].astype(BF16)
    w_b = jnp.concatenate([w[:, c4:].astype(BF16), w[:, c3:c4].astype(BF16),
                           jnp.zeros((D_MODEL, LANES - GATE_RANK), BF16)], axis=1)
    wgk_p = jnp.concatenate([w_gk2[0], jnp.zeros((LANES - GATE_RANK, GLA_KDIM), F32)], axis=0).astype(BF16)
    wo_bf = w_o[0].astype(BF16)
    s5m = _s5_prep(s5_a_re[0], s5_a_im[0], s5_b_re[0], s5_b_im[0], s5_c_re[0], s5_c_im[0], s5_log_dt[0])
    proj_w = (row(g_pre_mix), w_a, w_b, wgk_p, row(b_gk))

    def finish(x, o, g, yg, u, tm, steps):
        return _mix_ffn(x, o, g, yg, u, row(s5_d), row(gla_norm), w_s5_glu[0].astype(BF16), row(s5_norm),
                        wo_bf, row(g_post_mix), row(g_pre_ffn), w_gate[0].astype(BF16),
                        w_up[0].astype(BF16), w_down[0].astype(BF16), row(g_post_ffn), tm, steps)

    xm = jnp.broadcast_to(meta_tokens[None], (bp, N_META, D_MODEL)).reshape(bp * N_META, D_MODEL)
    q, k, v, _, _, lg, ug = _in_proj(xm, *proj_w, bp * N_META, S5_BLOCK)
    r3 = lambda t, b, l: t.reshape(b, l, t.shape[-1])
    _, s_meta = _gla(r3(q, bp, N_META), r3(k, bp, N_META), r3(v, bp, N_META), r3(lg, bp, N_META),
                     jnp.zeros((bp, GLA_KDIM, GLA_DV), F32), bp, bp, N_META)
    zh = jnp.zeros((bp * S5_GROUPS // 2, LANES), F32)
    _, hm_re, hm_im = _s5(ug, s5m[S5_BLOCK], zh, zh, 1, bp)

    xp = x_prompt.reshape(bp * seq_p, D_MODEL)
    g, u, o, ug, s_p = _in_proj(xp, *proj_w, TOKEN_TILE, S5_BLOCK, gla_s0=s_meta, seq_len=seq_p)
    yg, hp_re, hp_im = _s5(ug, s5m[S5_BLOCK], hm_re, hm_im, bp, S5_ROW_BLOCKS)
    y_prompt = finish(xp, o, g, yg, u, TOKEN_TILE, S5_BLOCK)

    xs = x_sample.reshape(bs * seq_s, D_MODEL)
    to_g = lambda t: t[0].reshape(bs * S5_GROUPS // 2, LANES)
    q, k, v, g, u, lg, ug = _in_proj(xs, *proj_w, TOKEN_TILE, seq_s)
    o, s_s = _gla(r3(q, bs, seq_s), r3(k, bs, seq_s), r3(v, bs, seq_s), r3(lg, bs, seq_s),
                  state_gla[0].reshape(bs, GLA_KDIM, GLA_DV), GLA_SAMPLE_SEQS, GLA_SAMPLE_GROUP, seq_s)
    yg, hs_re, hs_im = _s5(ug, s5m[seq_s], to_g(state_s5_re), to_g(state_s5_im), 1, bs)
    y_sample = finish(xs, o.reshape(bs * seq_s, D_GLA), g, yg, u, TOKEN_TILE, seq_s)

    gla_out = lambda s, b: s.reshape(1, b, GLA_HEADS, GLA_DK, GLA_DV)
    s5_out = lambda h: h.reshape(1, -1, S5_GROUPS, S5_STATE)
    return (y_prompt.reshape(bp, seq_p, D_MODEL), y_sample.reshape(bs, seq_s, D_MODEL),
            gla_out(s_p, bp), s5_out(hp_re), s5_out(hp_im),
            gla_out(s_s, bs), s5_out(hs_re), s5_out(hs_im))
```

```python
import functools

import jax
import jax.numpy as jnp
from jax import lax
from jax.experimental import pallas as pl
from jax.experimental.pallas import tpu as pltpu

F32 = jnp.float32
BF16 = jnp.bfloat16

D_MODEL = 1024
D_GLA = 512
GLA_HEADS = 4
GLA_DV = 128
GLA_DK = 64
GLA_KDIM = 256
GATE_RANK = 16
GATE_NORM = 16.0
GLA_CHUNK = 64
D_S5 = 512
S5_GROUP = 16
S5_GROUPS = 32
S5_STATE = 64
N_META = 16
D_FF = 2816
EPS = 1e-6
LANES = 128
S5_BLOCK = 16
SLAB_GROUPS = LANES // S5_GROUP
FF_CHUNK = 256
TOKEN_TILE = 512
S5_ROW_BLOCKS = 128
GLA_SAMPLE_GROUP = 16
GLA_SAMPLE_SEQS = 32
VMEM_LIMIT = 48 * 1024 * 1024


def _rms(x, g):
    return x * lax.rsqrt(jnp.mean(x * x, axis=-1, keepdims=True) + EPS) * g


def _dot(a, b):
    return jnp.dot(a, b, preferred_element_type=F32)


def _dot_nt(a, b):
    return lax.dot_general(a, b, (((1,), (1,)), ((), ())), preferred_element_type=F32)


def _dot_tn(a, b):
    return lax.dot_general(a, b, (((0,), (0,)), ((), ())), preferred_element_type=F32)


def _const_spec(shape):
    zeros = (0,) * len(shape)
    return pl.BlockSpec(shape, lambda *_: zeros)


PREP_GROUPS = 8


def _cmul(a, b):
    return a[0] * b[0] - a[1] * b[1], a[0] * b[1] + a[1] * b[0]


def _unit_powers(c1, s1, expo, n_bits):
    acc = (jnp.ones_like(c1), jnp.zeros_like(c1))
    base = (c1, s1)
    squares = [base]
    for bit in range(n_bits):
        take = ((expo >> bit) & 1) == 1
        nxt = _cmul(acc, base)
        acc = (jnp.where(take, nxt[0], acc[0]), jnp.where(take, nxt[1], acc[1]))
        base = _cmul(base, base)
        squares.append(base)
    return acc, squares


def _s5_prep_kernel(arc_ref, aic_ref, air_ref, ldt_ref, br_ref, bi_ref, cr_ref, ci_ref,
                    mi16_ref, mi8_ref, in16_ref, in8_ref, out16_ref, out8_ref,
                    a16r_ref, a16i_ref, a8r_ref, a8i_ref):
    n, q = S5_STATE, S5_BLOCK
    w = q * S5_GROUP
    hp = lax.Precision.HIGHEST
    lane = lax.broadcasted_iota(jnp.int32, (n, w), 1)
    t_blk = lane >> 4
    eye = lax.broadcasted_iota(jnp.int32, (n, n), 0) == lax.broadcasted_iota(jnp.int32, (n, n), 1)
    to_col = lambda r: jnp.sum(jnp.where(eye, r, 0.0), axis=1, keepdims=True)
    zrows = lambda x: jnp.concatenate([x, jnp.zeros_like(x)], axis=0)
    lane_n = lax.broadcasted_iota(jnp.int32, (n, LANES), 1)
    rep = jnp.where((lax.broadcasted_iota(jnp.int32, (S5_GROUP, w), 1) & (S5_GROUP - 1))
                    == lax.broadcasted_iota(jnp.int32, (S5_GROUP, w), 0), 1.0, 0.0)
    mo_g, rev_g, hop_g = [], [], []
    for g in range(PREP_GROUPS):
        dt = jnp.exp(ldt_ref[g])
        ang_r = air_ref[g] * dt
        c1, s1 = to_col(jnp.cos(ang_r)), to_col(jnp.sin(ang_r))
        lam_re = jnp.minimum(arc_ref[g], -1e-4)
        lam_im = aic_ref[g]
        unit, squares = _unit_powers(c1, s1, t_blk, 4)
        pm = jnp.exp(t_blk.astype(F32) * (lam_re * dt))
        pk = (pm * unit[0], pm * unit[1])
        mag = jnp.exp(lam_re * dt)
        ab = (mag * c1, mag * s1)
        p1 = _cmul(pk, ab)
        ct = tuple(lax.dot_general(r[g], rep, (((0,), (0,)), ((), ())), precision=hp,
                                   preferred_element_type=F32) for r in (cr_ref, ci_ref))
        bt = tuple(jnp.dot(r[g], rep, precision=hp, preferred_element_type=F32) for r in (br_ref, bi_ref))
        g0 = _cmul(ct, pk)
        mo = _cmul(ct, p1)
        mo_g.append((mo[0], -mo[1]))
        den = lam_re * lam_re + lam_im * lam_im
        nr, ni = ab[0] - 1.0, ab[1]
        f = ((nr * lam_re + ni * lam_im) / den, (ni * lam_re - nr * lam_im) / den)
        e = _cmul(pk, _cmul(f, bt))
        et = [zrows(x).T for x in e]
        rev_g.append([jnp.concatenate([x[(q - 1 - s) * S5_GROUP:(q - s) * S5_GROUP] for s in range(q)], axis=0)
                      for x in et])
        hops = []
        for steps in (q, q // 2):
            m = jnp.exp(float(steps) * (lam_re * dt))
            u = squares[steps.bit_length() - 1]
            hops += [m * u[0], m * u[1]]
        cols = jnp.zeros((n, LANES), F32)
        for idx, hcol in enumerate(hops):
            cols = jnp.where(lane_n == idx, hcol, cols)
        hop_g.append(zrows(cols).T)
        t0 = (jnp.dot(et[0][:S5_GROUP, :n], g0[0], precision=hp, preferred_element_type=F32)
              - jnp.dot(et[1][:S5_GROUP, :n], g0[1], precision=hp, preferred_element_type=F32))
        lane_t = lax.broadcasted_iota(jnp.int32, t0.shape, 1) >> 4
        for s in range(q):
            blk = t0 if s == 0 else jnp.where(lane_t >= s, pltpu.roll(t0, S5_GROUP * s, 1), 0.0)
            mi16_ref[g, s * S5_GROUP:(s + 1) * S5_GROUP, :] = blk.astype(BF16)
            if s < q // 2:
                mi8_ref[g, s * S5_GROUP:(s + 1) * S5_GROUP, :] = blk[:, :w // 2].astype(BF16)

    half = lambda x: pltpu.roll(x, n, 1)
    for p in range(PREP_GROUPS // 2):
        g0, g1 = 2 * p, 2 * p + 1
        for c in range(2):
            part = slice(c * LANES, (c + 1) * LANES)
            lo, hi = rev_g[g0][c], half(rev_g[g1][c])
            in16_ref[p, :, part] = jnp.concatenate([lo, hi], axis=0).astype(BF16)
            in8_ref[p, :, part] = jnp.concatenate([lo[w // 2:], hi[w // 2:]], axis=0).astype(BF16)
            m0, m1 = mo_g[g0][c], mo_g[g1][c]
            z, zh = jnp.zeros_like(m0), jnp.zeros((n, w // 2), F32)
            out16_ref[p, part, :] = jnp.concatenate([jnp.concatenate([m0, z], axis=1),
                                                     jnp.concatenate([z, m1], axis=1)], axis=0).astype(BF16)
            out8_ref[p, part, :] = jnp.concatenate([jnp.concatenate([m0[:, :w // 2], zh], axis=1),
                                                    jnp.concatenate([zh, m1[:, :w // 2]], axis=1)],
                                                   axis=0).astype(BF16)
        hops = hop_g[g0] + half(hop_g[g1])
        a16r_ref[p] = hops[0:1]
        a16i_ref[p] = hops[1:2]
        a8r_ref[p] = hops[2:3]
        a8i_ref[p] = hops[3:4]


def _s5_prep(a_re, a_im, b_re, b_im, c_re, c_im, log_dt):
    g, n, j, q = S5_GROUPS, S5_STATE, S5_GROUP, S5_BLOCK
    w, h, pg = q * j, q * j // 2, PREP_GROUPS
    col = lambda t: t.reshape(g, n, 1)
    rowv = lambda t: t.reshape(g, 1, n)
    spec = lambda a, b_: pl.BlockSpec((pg, a, b_), lambda i: (i, 0, 0))
    pspec = lambda a, b_: pl.BlockSpec((pg // 2, a, b_), lambda i: (i, 0, 0))
    pair_shapes = [(2 * w, 2 * LANES), (2 * h, 2 * LANES), (2 * LANES, 2 * w), (2 * LANES, 2 * h)]
    outs = pl.pallas_call(
        _s5_prep_kernel,
        grid=(g // pg,),
        in_specs=[spec(n, 1), spec(n, 1), spec(1, n), spec(1, 1),
                  spec(n, j), spec(n, j), spec(j, n), spec(j, n)],
        out_specs=[spec(w, w), spec(h, h)] + [pspec(*s) for s in pair_shapes] + [pspec(1, LANES)] * 4,
        out_shape=[jax.ShapeDtypeStruct((g, w, w), BF16), jax.ShapeDtypeStruct((g, h, h), BF16)]
                  + [jax.ShapeDtypeStruct((g // 2,) + s, BF16) for s in pair_shapes]
                  + [jax.ShapeDtypeStruct((g // 2, 1, LANES), F32)] * 4,
        compiler_params=pltpu.CompilerParams(dimension_semantics=("parallel",)),
        name="s5_prep",
    )(col(a_re), col(a_im), rowv(a_im), log_dt.reshape(g, 1, 1),
      b_re, b_im, c_re, c_im)
    mi16, mi8, in16, in8, out16, out8, a16r, a16i, a8r, a8i = outs
    return {q: dict(m_intra=mi16, m_in=in16, m_out=out16, a_re=a16r, a_im=a16i),
            q // 2: dict(m_intra=mi8, m_in=in8, m_out=out8, a_re=a8r, a_im=a8i)}


def _lane_block_transpose(a):
    a = list(a)
    blk = lax.broadcasted_iota(jnp.int32, a[0].shape, 1) >> 4
    for d in (4, 2, 1):
        upper = (blk & d) != 0
        for r in range(8):
            if r & d:
                continue
            lo, hi = a[r], a[r + d]
            a[r] = jnp.where(upper, pltpu.roll(hi, 16 * d, 1), lo)
            a[r + d] = jnp.where(upper, hi, pltpu.roll(lo, LANES - 16 * d, 1))
    return a


def _in_proj_kernel(*refs, steps, n_tiles, tiles_per_seq):
    fused = tiles_per_seq is not None
    x_ref, gpre_ref, wa_ref, wb_ref, wgk_ref, bgk_ref = refs[:6]
    if fused:
        (s0_ref, g_ref, u_ref, o_ref, ug_ref, sfin_ref,
         u_scr, gk_scr, q_scr, k_scr, v_scr, s_scr) = refs[6:]
    else:
        q_ref, k_ref, v_ref, g_ref, u_ref, lg_ref, ug_ref, u_scr, gk_scr = refs[6:]
    i = pl.program_id(0)
    tm = x_ref.shape[0]

    def project(slot):
        h = _rms(x_ref[...], gpre_ref[...]).astype(BF16)
        proj = _dot(h, wa_ref[...])
        tail_proj = _dot(h, wb_ref[...])
        if fused:
            q_scr[slot] = proj[:, 0:256]
            k_scr[slot] = proj[:, 256:512]
            v_scr[slot] = proj[:, 512:1024].astype(BF16)
        else:
            q_ref[...] = proj[:, 0:256]
            k_ref[...] = proj[:, 256:512]
            v_ref[...] = proj[:, 512:1024]
        g_ref[...] = proj[:, 1024:1536]
        u = tail_proj[:, :D_S5]
        u_ref[...] = u
        for s in range(D_S5 // LANES):
            u_scr[slot, s] = u[:, s * LANES:(s + 1) * LANES]
        gk_scr[slot] = tail_proj[:, D_S5:]

    def tail(slot):
        z = _dot(gk_scr[slot].astype(BF16), wgk_ref[...]) + bgk_ref[...]
        lg = jax.nn.log_sigmoid(z) * (1.0 / GATE_NORM)
        if fused:
            tile = i - 1
            seq = tile // tiles_per_seq
            state = jnp.where(tile % tiles_per_seq == 0, s0_ref[seq], s_scr[...])
            masks = _gla_masks(1, GLA_CHUNK)
            for c in range(tm // GLA_CHUNK):
                rows = slice(c * GLA_CHUNK, (c + 1) * GLA_CHUNK)
                o, (state,) = _gla_group(q_scr[slot, rows, :], k_scr[slot, rows, :], v_scr[slot, rows, :],
                                         lg[rows], [state], masks, GLA_CHUNK)
                o_ref[rows, :] = o
            s_scr[...] = state
            sfin_ref[seq] = state
        else:
            lg_ref[...] = lg
        nr = tm // steps
        rt = min(nr, 16)
        for s in range(D_S5 // LANES):
            for hf in range(steps // 8):
                for r0 in range(0, nr, rt):
                    a = [u_scr[slot, s, pl.ds(r0 * steps + hf * 8 + t, rt, stride=steps), :] for t in range(8)]
                    per_group = _lane_block_transpose(a)
                    for g in range(SLAB_GROUPS):
                        ug_ref[s * SLAB_GROUPS + g, r0:r0 + rt, hf * LANES:(hf + 1) * LANES] = (
                            per_group[g].astype(BF16))

    @pl.when(i == 0)
    def _():
        if fused:
            s_scr[...] = jnp.zeros(s_scr.shape, F32)
        project(0)

    @pl.when((i > 0) & (i < n_tiles))
    def _():
        tail((i - 1) & 1)
        project(i & 1)

    @pl.when(i == n_tiles)
    def _():
        tail((i - 1) & 1)


def _in_proj(x, gpre, w_a, w_b, wgk_p, bgk, tm, steps, gla_s0=None, seq_len=None):
    t = x.shape[0]
    n = t // tm
    cur = lambda i: jnp.minimum(i, n - 1)
    prev = lambda i: jnp.maximum(i - 1, 0)
    row = lambda w, at: pl.BlockSpec((tm, w), lambda i: (at(i), 0))
    wg = steps * S5_GROUP
    ug_spec = pl.BlockSpec((S5_GROUPS, tm // steps, wg), lambda i: (0, prev(i), 0))
    ug_shape = jax.ShapeDtypeStruct((S5_GROUPS, t // steps, wg), BF16)
    f32 = lambda w: jax.ShapeDtypeStruct((t, w), F32)
    in_specs = [row(D_MODEL, cur), _const_spec((1, D_MODEL)), _const_spec(w_a.shape), _const_spec(w_b.shape),
                _const_spec((LANES, GLA_KDIM)), _const_spec((1, GLA_KDIM))]
    scratch = [pltpu.VMEM((2, D_S5 // LANES, tm, LANES), F32), pltpu.VMEM((2, tm, LANES), F32)]
    args = (x, gpre, w_a, w_b, wgk_p, bgk)
    if gla_s0 is None:
        tiles_per_seq = None
        out_specs = [row(w, cur) for w in (256, 256, 512, 512, 512)] + [row(GLA_KDIM, prev), ug_spec]
        out_shape = [f32(w) for w in (256, 256, 512, 512, 512)] + [f32(GLA_KDIM), ug_shape]
    else:
        tiles_per_seq = seq_len // tm
        assert seq_len % tm == 0 and tm % GLA_CHUNK == 0
        in_specs.append(_const_spec(gla_s0.shape))
        args += (gla_s0,)
        out_specs = [row(D_GLA, cur), row(D_S5, cur), row(D_GLA, prev), ug_spec, _const_spec(gla_s0.shape)]
        out_shape = [f32(D_GLA), f32(D_S5), f32(D_GLA), ug_shape, jax.ShapeDtypeStruct(gla_s0.shape, F32)]
        scratch += [pltpu.VMEM((2, tm, GLA_KDIM), F32), pltpu.VMEM((2, tm, GLA_KDIM), F32),
                    pltpu.VMEM((2, tm, D_GLA), BF16), pltpu.VMEM(gla_s0.shape[1:], F32)]
    return pl.pallas_call(
        functools.partial(_in_proj_kernel, steps=steps, n_tiles=n, tiles_per_seq=tiles_per_seq),
        grid=(n + 1,),
        in_specs=in_specs,
        out_specs=out_specs,
        out_shape=out_shape,
        scratch_shapes=scratch,
        compiler_params=pltpu.CompilerParams(dimension_semantics=("arbitrary",),
                                             vmem_limit_bytes=VMEM_LIMIT),
        name="in_proj" if gla_s0 is None else "in_proj_gla",
    )(*args)


def _gla_masks(sg, chunk):
    r = sg * chunk
    assert r >= LANES or sg == 1
    shift = chunk.bit_length() - 1
    hk = GLA_HEADS * r
    ri = lax.broadcasted_iota(jnp.int32, (r, r), 0)
    ci = lax.broadcasted_iota(jnp.int32, (r, r), 1)
    ri4 = lax.broadcasted_iota(jnp.int32, (r, hk), 0)
    ci4 = lax.broadcasted_iota(jnp.int32, (r, hk), 1) & (r - 1)
    stack_head = lax.broadcasted_iota(jnp.int32, (hk, 1), 0) >> (r.bit_length() - 1)
    k_head = lax.broadcasted_iota(jnp.int32, (1, GLA_KDIM), 1) >> 6
    v_head = lax.broadcasted_iota(jnp.int32, (1, D_GLA), 1) >> 7
    s_head = lax.broadcasted_iota(jnp.int32, (GLA_KDIM, 1), 0) >> 6
    return dict(
        tri_bf=jnp.where(((ri >> shift) == (ci >> shift)) & (ri >= ci), 1.0, 0.0).astype(BF16),
        causal4=((ri4 >> shift) == (ci4 >> shift)) & (ri4 >= ci4),
        k_diag=stack_head == k_head, v_diag=stack_head == v_head, s_diag=s_head == v_head)


def _gla_group(q, k, v_bf, lg, states, m, chunk):
    sg = len(states)
    r = sg * chunk
    tile4 = lambda t: jnp.concatenate([t] * GLA_HEADS, axis=0)
    lg_hi = lg.astype(BF16)
    lg_lo = (lg - lg_hi.astype(F32)).astype(BF16)
    b2 = _dot(m["tri_bf"], jnp.concatenate([lg_hi, lg_lo], axis=1))
    b = b2[:, :GLA_KDIM] + b2[:, GLA_KDIM:]
    lasts = [b[(i + 1) * chunk - 1:(i + 1) * chunk, :] for i in range(sg)]
    bl = jnp.concatenate([jnp.broadcast_to(t, (chunk, GLA_KDIM)) for t in lasts], axis=0)
    qd = (q * (GLA_DK ** -0.5) * jnp.exp(b)).astype(BF16)
    ki = (k * jnp.exp(-b)).astype(BF16)
    ke = k * jnp.exp(bl - b)

    ki_bd = jnp.where(m["k_diag"], tile4(ki), 0.0)
    att = jnp.where(m["causal4"], _dot_nt(qd, ki_bd), 0.0).astype(BF16)
    v_bd = jnp.where(m["v_diag"], tile4(v_bf), 0.0)
    o_intra = _dot(att, v_bd)

    if r < LANES:
        aug_t = jnp.concatenate([ke, jnp.broadcast_to(lasts[0], (LANES - r, GLA_KDIM))], axis=0).T
        ke_t, bl_t = aug_t, aug_t[:, r:]
    else:
        ke_t = ke.T
        bl_t = jnp.concatenate(lasts + [jnp.zeros((r - sg, GLA_KDIM), F32)], axis=0).T
    ke_t = ke_t.astype(BF16)

    outs, new_states = [], []
    for i, s_old in enumerate(states):
        rows = slice(i * chunk, (i + 1) * chunk)
        s_bd = jnp.where(m["s_diag"], jnp.concatenate([s_old.astype(BF16)] * GLA_HEADS, axis=1), 0.0)
        outs.append(o_intra[rows] + _dot(qd[rows], s_bd))
        upd = [_dot(ke_t[h * GLA_DK:(h + 1) * GLA_DK, rows], v_bf[rows, h * GLA_DV:(h + 1) * GLA_DV])
               for h in range(GLA_HEADS)]
        new_states.append(jnp.exp(bl_t[:, i:i + 1]) * s_old + jnp.concatenate(upd, axis=0))
    return (outs[0] if sg == 1 else jnp.concatenate(outs, axis=0)), new_states


def _gla_kernel(q_ref, k_ref, v_ref, lg_ref, s0_ref, o_ref, s_ref, *, bb, sg, chunk):
    r = sg * chunk

    @pl.when(pl.program_id(1) == 0)
    def _():
        s_ref[...] = s0_ref[...]

    masks = _gla_masks(sg, chunk)
    for gi in range(bb // sg):
        seqs = slice(gi * sg, (gi + 1) * sg)
        o, new = _gla_group(q_ref[seqs].reshape(r, GLA_KDIM), k_ref[seqs].reshape(r, GLA_KDIM),
                            v_ref[seqs].reshape(r, D_GLA).astype(BF16), lg_ref[seqs].reshape(r, GLA_KDIM),
                            [s_ref[gi * sg + i] for i in range(sg)], masks, chunk)
        o_ref[seqs] = o.reshape(sg, chunk, D_GLA)
        for i in range(sg):
            s_ref[gi * sg + i] = new[i]


def _gla(q, k, v, lg, s0, bb, sg, chunk):
    b, l, _ = q.shape
    blk = lambda w: pl.BlockSpec((bb, chunk, w), lambda i, c: (i, c, 0))
    sspec = pl.BlockSpec((bb, GLA_KDIM, GLA_DV), lambda i, c: (i, 0, 0))
    return pl.pallas_call(
        functools.partial(_gla_kernel, bb=bb, sg=sg, chunk=chunk),
        grid=(b // bb, l // chunk),
        in_specs=[blk(GLA_KDIM), blk(GLA_KDIM), blk(D_GLA), blk(GLA_KDIM), sspec],
        out_specs=[blk(D_GLA), sspec],
        out_shape=[jax.ShapeDtypeStruct((b, l, D_GLA), F32),
                   jax.ShapeDtypeStruct((b, GLA_KDIM, GLA_DV), F32)],
        compiler_params=pltpu.CompilerParams(dimension_semantics=("parallel", "arbitrary"),
                                             vmem_limit_bytes=VMEM_LIMIT),
        name=f"gla_c{chunk}",
    )(q, k, v, lg, s0)


def _s5_kernel(ug_ref, mi_ref, min_ref, mout_ref, ar_ref, ai_ref,
               h0r_ref, h0i_ref, yg_ref, hfr_ref, hfi_ref, vr_s, vi_s, hr_s, hi_s, str_s, sti_s, *, nb, cb):
    gs = SLAB_GROUPS // 2
    n_pairs = S5_GROUPS // 2
    w = ug_ref.shape[-1]
    rows = nb * cb
    batch = cb if nb == 1 else nb
    n_blocks = rows // batch
    interleave = nb > 1 and cb > 1
    first_pair = pl.program_id(0) * gs
    seq0 = pl.program_id(1) * batch if nb == 1 else 0

    def state_rows(p):
        return pl.ds(seq0 * n_pairs + first_pair + p, batch, stride=n_pairs)

    def load_state():
        for p in range(gs):
            str_s[p] = h0r_ref[state_rows(p), :]
            sti_s[p] = h0i_ref[state_rows(p), :]

    if nb == 1:
        load_state()
    else:
        pl.when(pl.program_id(1) == 0)(load_state)

    def to_scan_order(ref, g, val):
        if not interleave:
            ref[g] = val
        else:
            for b in range(nb):
                ref[g, pl.ds(b, cb, stride=nb), :] = val[b * cb:(b + 1) * cb]

    def from_scan_order(ref, g):
        if not interleave:
            return ref[g]
        return jnp.concatenate([ref[g, pl.ds(b, cb, stride=nb), :] for b in range(nb)], axis=0)

    ub = [ug_ref[g].reshape(rows, w) for g in range(2 * gs)]
    for g in range(gs):
        v = _dot(jnp.concatenate([ub[2 * g], ub[2 * g + 1]], axis=1), min_ref[g])
        to_scan_order(vr_s, g, v[:, :LANES])
        to_scan_order(vi_s, g, v[:, LANES:])
    ar = [ar_ref[g] for g in range(gs)]
    ai = [ai_ref[g] for g in range(gs)]

    def body(c, carry):
        rows = pl.ds(pl.multiple_of(c * batch, batch), batch)
        new = []
        for g in range(gs):
            hr, hi = carry[2 * g], carry[2 * g + 1]
            hr_s[g, rows, :] = hr
            hi_s[g, rows, :] = hi
            new.append(ar[g] * hr - ai[g] * hi + vr_s[g, rows, :])
            new.append(ar[g] * hi + ai[g] * hr + vi_s[g, rows, :])
        return tuple(new)

    init = tuple(ref[g] for g in range(gs) for ref in (str_s, sti_s))
    fin = lax.fori_loop(0, n_blocks, body, init, unroll=min(n_blocks, 4))
    for g in range(gs):
        str_s[g] = fin[2 * g]
        sti_s[g] = fin[2 * g + 1]
        hfr_ref[state_rows(g), :] = fin[2 * g]
        hfi_ref[state_rows(g), :] = fin[2 * g + 1]
        h_in = jnp.concatenate([from_scan_order(hr_s, g), from_scan_order(hi_s, g)], axis=1).astype(BF16)
        y_state = _dot(h_in, mout_ref[g])
        for k in range(2):
            y = _dot(ub[2 * g + k], mi_ref[2 * g + k]) + y_state[:, k * w:(k + 1) * w]
            yg_ref[2 * g + k] = y.reshape(yg_ref.shape[1:])


def _s5(ug, m, h0_re, h0_im, nb, cb):
    g_all, r, w = ug.shape
    nc = r // nb
    batch = cb if nb == 1 else nb
    n, gs = LANES, SLAB_GROUPS
    uspec = pl.BlockSpec((gs, nb, cb, w), lambda s, i: (s, 0, i, 0))
    per_g = lambda a, b_: pl.BlockSpec((gs, a, b_), lambda s, i: (s, 0, 0))
    per_p = lambda a, b_: pl.BlockSpec((gs // 2, a, b_), lambda s, i: (s, 0, 0))
    hspec = _const_spec(h0_re.shape)
    yg, hf_re, hf_im = pl.pallas_call(
        functools.partial(_s5_kernel, nb=nb, cb=cb),
        grid=(g_all // gs, nc // cb),
        in_specs=[uspec, per_g(w, w), per_p(2 * w, 2 * n), per_p(2 * n, 2 * w),
                  per_p(1, n), per_p(1, n), hspec, hspec],
        out_specs=[uspec, hspec, hspec],
        out_shape=[jax.ShapeDtypeStruct((g_all, nb, nc, w), F32),
                   jax.ShapeDtypeStruct(h0_re.shape, F32),
                   jax.ShapeDtypeStruct(h0_im.shape, F32)],
        scratch_shapes=[pltpu.VMEM((gs // 2, nb * cb, n), F32)] * 4 + [pltpu.VMEM((gs // 2, batch, n), F32)] * 2,
        compiler_params=pltpu.CompilerParams(dimension_semantics=("arbitrary", "arbitrary"),
                                             vmem_limit_bytes=VMEM_LIMIT),
        name=f"s5_w{w}",
    )(ug.reshape(g_all, nb, nc, w), m["m_intra"], m["m_in"], m["m_out"], m["a_re"], m["a_im"], h0_re, h0_im)
    return yg.reshape(g_all, r, w), hf_re, hf_im


def _mix_ffn_kernel(x_ref, o_ref, g_ref, yg_ref, u_ref, dsk_ref, gn_ref, wglu_ref, s5n_ref, wo_ref, gpost_ref,
                    gpre_ref, wg_ref, wu_ref, wd_ref, gpostf_ref, out_ref, y_scr, x1_scr, *, steps, n_tiles):
    i = pl.program_id(0)

    def mix_into(slot):
        nr = x_ref.shape[0] // steps
        rt = min(nr, 8)
        for s in range(D_S5 // LANES):
            for hf in range(steps // 8):
                for r0 in range(0, nr, rt):
                    a = [yg_ref[s * SLAB_GROUPS + g, r0:r0 + rt, hf * LANES:(hf + 1) * LANES]
                         for g in range(SLAB_GROUPS)]
                    per_step = _lane_block_transpose(a)
                    for t in range(8):
                        y_scr[s, pl.ds(r0 * steps + hf * 8 + t, rt, stride=steps), :] = per_step[t]
        y5 = jnp.concatenate([y_scr[s] for s in range(D_S5 // LANES)], axis=1) + dsk_ref[...] * u_ref[...]
        o = o_ref[...]
        gn = gn_ref[...]
        heads = []
        for h in range(GLA_HEADS):
            heads.append(_rms(o[:, h * GLA_DV:(h + 1) * GLA_DV], gn))
        og = jnp.concatenate(heads, axis=1) * jax.nn.silu(g_ref[...])
        y = jax.nn.gelu(y5)
        y = y * jax.nn.sigmoid(_dot(y.astype(BF16), wglu_ref[...]))
        y = _rms(y, s5n_ref[...])
        mix = _dot(og.astype(BF16), wo_ref[:D_GLA, :]) + _dot(y.astype(BF16), wo_ref[D_GLA:, :])
        x1_scr[slot] = x_ref[...] + _rms(mix, gpost_ref[...])

    def ffn_from(slot):
        x = x1_scr[slot]
        h = _rms(x, gpre_ref[...]).astype(BF16)
        acc = jnp.zeros(x.shape, F32)
        for c in range(D_FF // FF_CHUNK):
            cols = slice(c * FF_CHUNK, (c + 1) * FF_CHUNK)
            act = jax.nn.silu(_dot(h, wg_ref[:, cols])) * _dot(h, wu_ref[:, cols])
            acc = acc + _dot(act.astype(BF16), wd_ref[cols, :])
        out_ref[...] = x + _rms(acc, gpostf_ref[...])

    @pl.when(i == 0)
    def _():
        mix_into(0)

    @pl.when((i > 0) & (i < n_tiles))
    def _():
        ffn_from((i - 1) & 1)
        mix_into(i & 1)

    @pl.when(i == n_tiles)
    def _():
        ffn_from((i - 1) & 1)


def _mix_ffn(x, o, g, yg, u, dsk, gn, wglu, s5n, wo, gpost, gpre, wg, wu, wd, gpostf, tm, steps):
    t = x.shape[0]
    n = t // tm
    cur = lambda i: jnp.minimum(i, n - 1)
    row = lambda w: pl.BlockSpec((tm, w), lambda i: (cur(i), 0))
    fixed = lambda shape: pl.BlockSpec(shape, lambda i: (0,) * len(shape), pipeline_mode=pl.Buffered(1))
    return pl.pallas_call(
        functools.partial(_mix_ffn_kernel, steps=steps, n_tiles=n),
        grid=(n + 1,),
        in_specs=[row(D_MODEL), row(D_GLA), row(D_GLA),
                  pl.BlockSpec((S5_GROUPS, tm // steps, steps * S5_GROUP), lambda i: (0, cur(i), 0)),
                  row(D_S5), fixed((1, D_S5)),
                  fixed((1, GLA_DV)), fixed((D_S5, D_S5)), fixed((1, D_S5)),
                  fixed((D_GLA + D_S5, D_MODEL)), fixed((1, D_MODEL)),
                  fixed((1, D_MODEL)), fixed((D_MODEL, D_FF)), fixed((D_MODEL, D_FF)),
                  fixed((D_FF, D_MODEL)), fixed((1, D_MODEL))],
        out_specs=pl.BlockSpec((tm, D_MODEL), lambda i: (jnp.maximum(i - 1, 0), 0)),
        out_shape=jax.ShapeDtypeStruct((t, D_MODEL), F32),
        scratch_shapes=[pltpu.VMEM((D_S5 // LANES, tm, LANES), F32), pltpu.VMEM((2, tm, D_MODEL), F32)],
        compiler_params=pltpu.CompilerParams(dimension_semantics=("arbitrary",),
                                             vmem_limit_bytes=VMEM_LIMIT),
        name="mix_ffn",
    )(x, o, g, yg, u, dsk, gn, wglu, s5n, wo, gpost, gpre, wg, wu, wd, gpostf)


def kernel(x_prompt, x_sample, state_gla, state_s5_re, state_s5_im, meta_tokens, g_pre_mix, w_in, w_gk2, b_gk, gla_norm, s5_a_re, s5_a_im, s5_b_re, s5_b_im, s5_c_re, s5_c_im, s5_d, s5_log_dt, w_s5_glu, s5_norm, w_o, g_post_mix, g_pre_ffn, w_gate, w_up, w_down, g_post_ffn):
    assert g_pre_mix.shape[0] == 1, "single-layer step"
    bp, seq_p, _ = x_prompt.shape
    bs, seq_s, _ = x_sample.shape
    row = lambda t: t[0].reshape(1, -1)

    w = w_in[0]
    c3, c4 = 2 * GLA_KDIM + 2 * D_GLA, 2 * GLA_KDIM + 2 * D_GLA + GATE_RANK
    w_a = w[:, :c3].astype(BF16)
    w_b = jnp.concatenate([w[:, c4:].astype(BF16), w[:, c3:c4].astype(BF16),
                           jnp.zeros((D_MODEL, LANES - GATE_RANK), BF16)], axis=1)
    wgk_p = jnp.concatenate([w_gk2[0], jnp.zeros((LANES - GATE_RANK, GLA_KDIM), F32)], axis=0).astype(BF16)
    wo_bf = w_o[0].astype(BF16)
    s5m = _s5_prep(s5_a_re[0], s5_a_im[0], s5_b_re[0], s5_b_im[0], s5_c_re[0], s5_c_im[0], s5_log_dt[0])
    proj_w = (row(g_pre_mix), w_a, w_b, wgk_p, row(b_gk))

    def finish(x, o, g, yg, u, tm, steps):
        return _mix_ffn(x, o, g, yg, u, row(s5_d), row(gla_norm), w_s5_glu[0].astype(BF16), row(s5_norm),
                        wo_bf, row(g_post_mix), row(g_pre_ffn), w_gate[0].astype(BF16),
                        w_up[0].astype(BF16), w_down[0].astype(BF16), row(g_post_ffn), tm, steps)

    xm = jnp.broadcast_to(meta_tokens[None], (bp, N_META, D_MODEL)).reshape(bp * N_META, D_MODEL)
    q, k, v, _, _, lg, ug = _in_proj(xm, *proj_w, bp * N_META, S5_BLOCK)
    r3 = lambda t, b, l: t.reshape(b, l, t.shape[-1])
    _, s_meta = _gla(r3(q, bp, N_META), r3(k, bp, N_META), r3(v, bp, N_META), r3(lg, bp, N_META),
                     jnp.zeros((bp, GLA_KDIM, GLA_DV), F32), bp, bp, N_META)
    zh = jnp.zeros((bp * S5_GROUPS // 2, LANES), F32)
    _, hm_re, hm_im = _s5(ug, s5m[S5_BLOCK], zh, zh, 1, bp)

    xp = x_prompt.reshape(bp * seq_p, D_MODEL)
    g, u, o, ug, s_p = _in_proj(xp, *proj_w, TOKEN_TILE, S5_BLOCK, gla_s0=s_meta, seq_len=seq_p)
    yg, hp_re, hp_im = _s5(ug, s5m[S5_BLOCK], hm_re, hm_im, bp, S5_ROW_BLOCKS)
    y_prompt = finish(xp, o, g, yg, u, TOKEN_TILE, S5_BLOCK)

    xs = x_sample.reshape(bs * seq_s, D_MODEL)
    to_g = lambda t: t[0].reshape(bs * S5_GROUPS // 2, LANES)
    q, k, v, g, u, lg, ug = _in_proj(xs, *proj_w, TOKEN_TILE, seq_s)
    o, s_s = _gla(r3(q, bs, seq_s), r3(k, bs, seq_s), r3(v, bs, seq_s), r3(lg, bs, seq_s),
                  state_gla[0].reshape(bs, GLA_KDIM, GLA_DV), GLA_SAMPLE_SEQS, GLA_SAMPLE_GROUP, seq_s)
    yg, hs_re, hs_im = _s5(ug, s5m[seq_s], to_g(state_s5_re), to_g(state_s5_im), 1, bs)
    y_sample = finish(xs, o.reshape(bs * seq_s, D_GLA), g, yg, u, TOKEN_TILE, seq_s)

    gla_out = lambda s, b: s.reshape(1, b, GLA_HEADS, GLA_DK, GLA_DV)
    s5_out = lambda h: h.reshape(1, -1, S5_GROUPS, S5_STATE)
    return (y_prompt.reshape(bp, seq_p, D_MODEL), y_sample.reshape(bs, seq_s, D_MODEL),
            gla_out(s_p, bp), s5_out(hp_re), s5_out(hp_im),
            gla_out(s_s, bs), s5_out(hs_re), s5_out(hs_im))
```

```python
import functools

import jax
import jax.numpy as jnp
from jax import lax
from jax.experimental import pallas as pl
from jax.experimental.pallas import tpu as pltpu

F32 = jnp.float32
BF16 = jnp.bfloat16

D_MODEL = 1024
D_GLA = 512
GLA_HEADS = 4
GLA_DV = 128
GLA_DK = 64
GLA_KDIM = 256
GATE_RANK = 16
GATE_NORM = 16.0
GLA_CHUNK = 64
D_S5 = 512
S5_GROUP = 16
S5_GROUPS = 32
S5_STATE = 64
N_META = 16
D_FF = 2816
EPS = 1e-6
LANES = 128
S5_BLOCK = 16
SLAB_GROUPS = LANES // S5_GROUP
FF_CHUNK = 256
TOKEN_TILE = 512
S5_ROW_BLOCKS = 128
GLA_SAMPLE_GROUP = 16
GLA_SAMPLE_SEQS = 32
VMEM_LIMIT = 48 * 1024 * 1024


def _rms(x, g):
    return x * lax.rsqrt(jnp.mean(x * x, axis=-1, keepdims=True) + EPS) * g


def _dot(a, b):
    return jnp.dot(a, b, preferred_element_type=F32)


def _dot_nt(a, b):
    return lax.dot_general(a, b, (((1,), (1,)), ((), ())), preferred_element_type=F32)


def _dot_tn(a, b):
    return lax.dot_general(a, b, (((0,), (0,)), ((), ())), preferred_element_type=F32)


def _const_spec(shape):
    zeros = (0,) * len(shape)
    return pl.BlockSpec(shape, lambda *_: zeros)


PREP_GROUPS = 8


def _cmul(a, b):
    return a[0] * b[0] - a[1] * b[1], a[0] * b[1] + a[1] * b[0]


def _unit_powers(c1, s1, expo, n_bits):
    acc = (jnp.ones_like(c1), jnp.zeros_like(c1))
    base = (c1, s1)
    squares = [base]
    for bit in range(n_bits):
        take = ((expo >> bit) & 1) == 1
        nxt = _cmul(acc, base)
        acc = (jnp.where(take, nxt[0], acc[0]), jnp.where(take, nxt[1], acc[1]))
        base = _cmul(base, base)
        squares.append(base)
    return acc, squares


def _s5_prep_kernel(arc_ref, aic_ref, air_ref, ldt_ref, br_ref, bi_ref, cr_ref, ci_ref,
                    mi16_ref, mi8_ref, in16_ref, in8_ref, out16_ref, out8_ref,
                    a16r_ref, a16i_ref, a8r_ref, a8i_ref):
    n, q = S5_STATE, S5_BLOCK
    w = q * S5_GROUP
    hp = lax.Precision.HIGHEST
    lane = lax.broadcasted_iota(jnp.int32, (n, w), 1)
    t_blk = lane >> 4
    eye = lax.broadcasted_iota(jnp.int32, (n, n), 0) == lax.broadcasted_iota(jnp.int32, (n, n), 1)
    to_col = lambda r: jnp.sum(jnp.where(eye, r, 0.0), axis=1, keepdims=True)
    zrows = lambda x: jnp.concatenate([x, jnp.zeros_like(x)], axis=0)
    lane_n = lax.broadcasted_iota(jnp.int32, (n, LANES), 1)
    rep = jnp.where((lax.broadcasted_iota(jnp.int32, (S5_GROUP, w), 1) & (S5_GROUP - 1))
                    == lax.broadcasted_iota(jnp.int32, (S5_GROUP, w), 0), 1.0, 0.0)
    mo_g, rev_g, hop_g = [], [], []
    for g in range(PREP_GROUPS):
        dt = jnp.exp(ldt_ref[g])
        ang_r = air_ref[g] * dt
        c1, s1 = to_col(jnp.cos(ang_r)), to_col(jnp.sin(ang_r))
        lam_re = jnp.minimum(arc_ref[g], -1e-4)
        lam_im = aic_ref[g]
        unit, squares = _unit_powers(c1, s1, t_blk, 4)
        pm = jnp.exp(t_blk.astype(F32) * (lam_re * dt))
        pk = (pm * unit[0], pm * unit[1])
        mag = jnp.exp(lam_re * dt)
        ab = (mag * c1, mag * s1)
        p1 = _cmul(pk, ab)
        ct = tuple(lax.dot_general(r[g], rep, (((0,), (0,)), ((), ())), precision=hp,
                                   preferred_element_type=F32) for r in (cr_ref, ci_ref))
        bt = tuple(jnp.dot(r[g], rep, precision=hp, preferred_element_type=F32) for r in (br_ref, bi_ref))
        g0 = _cmul(ct, pk)
        mo = _cmul(ct, p1)
        mo_g.append((mo[0], -mo[1]))
        den = lam_re * lam_re + lam_im * lam_im
        nr, ni = ab[0] - 1.0, ab[1]
        f = ((nr * lam_re + ni * lam_im) / den, (ni * lam_re - nr * lam_im) / den)
        e = _cmul(pk, _cmul(f, bt))
        et = [zrows(x).T for x in e]
        rev_g.append([jnp.concatenate([x[(q - 1 - s) * S5_GROUP:(q - s) * S5_GROUP] for s in range(q)], axis=0)
                      for x in et])
        hops = []
        for steps in (q, q // 2):
            m = jnp.exp(float(steps) * (lam_re * dt))
            u = squares[steps.bit_length() - 1]
            hops += [m * u[0], m * u[1]]
        cols = jnp.zeros((n, LANES), F32)
        for idx, hcol in enumerate(hops):
            cols = jnp.where(lane_n == idx, hcol, cols)
        hop_g.append(zrows(cols).T)
        t0 = (jnp.dot(et[0][:S5_GROUP, :n], g0[0], precision=hp, preferred_element_type=F32)
              - jnp.dot(et[1][:S5_GROUP, :n], g0[1], precision=hp, preferred_element_type=F32))
        lane_t = lax.broadcasted_iota(jnp.int32, t0.shape, 1) >> 4
        for s in range(q):
            blk = t0 if s == 0 else jnp.where(lane_t >= s, pltpu.roll(t0, S5_GROUP * s, 1), 0.0)
            mi16_ref[g, s * S5_GROUP:(s + 1) * S5_GROUP, :] = blk.astype(BF16)
            if s < q // 2:
                mi8_ref[g, s * S5_GROUP:(s + 1) * S5_GROUP, :] = blk[:, :w // 2].astype(BF16)

    half = lambda x: pltpu.roll(x, n, 1)
    for p in range(PREP_GROUPS // 2):
        g0, g1 = 2 * p, 2 * p + 1
        for c in range(2):
            part = slice(c * LANES, (c + 1) * LANES)
            lo, hi = rev_g[g0][c], half(rev_g[g1][c])
            in16_ref[p, :, part] = jnp.concatenate([lo, hi], axis=0).astype(BF16)
            in8_ref[p, :, part] = jnp.concatenate([lo[w // 2:], hi[w // 2:]], axis=0).astype(BF16)
            m0, m1 = mo_g[g0][c], mo_g[g1][c]
            z, zh = jnp.zeros_like(m0), jnp.zeros((n, w // 2), F32)
            out16_ref[p, part, :] = jnp.concatenate([jnp.concatenate([m0, z], axis=1),
                                                     jnp.concatenate([z, m1], axis=1)], axis=0).astype(BF16)
            out8_ref[p, part, :] = jnp.concatenate([jnp.concatenate([m0[:, :w // 2], zh], axis=1),
                                                    jnp.concatenate([zh, m1[:, :w // 2]], axis=1)],
                                                   axis=0).astype(BF16)
        hops = hop_g[g0] + half(hop_g[g1])
        a16r_ref[p] = hops[0:1]
        a16i_ref[p] = hops[1:2]
        a8r_ref[p] = hops[2:3]
        a8i_ref[p] = hops[3:4]


def _s5_prep(a_re, a_im, b_re, b_im, c_re, c_im, log_dt):
    g, n, j, q = S5_GROUPS, S5_STATE, S5_GROUP, S5_BLOCK
    w, h, pg = q * j, q * j // 2, PREP_GROUPS
    col = lambda t: t.reshape(g, n, 1)
    rowv = lambda t: t.reshape(g, 1, n)
    spec = lambda a, b_: pl.BlockSpec((pg, a, b_), lambda i: (i, 0, 0))
    pspec = lambda a, b_: pl.BlockSpec((pg // 2, a, b_), lambda i: (i, 0, 0))
    pair_shapes = [(2 * w, 2 * LANES), (2 * h, 2 * LANES), (2 * LANES, 2 * w), (2 * LANES, 2 * h)]
    outs = pl.pallas_call(
        _s5_prep_kernel,
        grid=(g // pg,),
        in_specs=[spec(n, 1), spec(n, 1), spec(1, n), spec(1, 1),
                  spec(n, j), spec(n, j), spec(j, n), spec(j, n)],
        out_specs=[spec(w, w), spec(h, h)] + [pspec(*s) for s in pair_shapes] + [pspec(1, LANES)] * 4,
        out_shape=[jax.ShapeDtypeStruct((g, w, w), BF16), jax.ShapeDtypeStruct((g, h, h), BF16)]
                  + [jax.ShapeDtypeStruct((g // 2,) + s, BF16) for s in pair_shapes]
                  + [jax.ShapeDtypeStruct((g // 2, 1, LANES), F32)] * 4,
        compiler_params=pltpu.CompilerParams(dimension_semantics=("parallel",)),
        name="s5_prep",
    )(col(a_re), col(a_im), rowv(a_im), log_dt.reshape(g, 1, 1),
      b_re, b_im, c_re, c_im)
    mi16, mi8, in16, in8, out16, out8, a16r, a16i, a8r, a8i = outs
    return {q: dict(m_intra=mi16, m_in=in16, m_out=out16, a_re=a16r, a_im=a16i),
            q // 2: dict(m_intra=mi8, m_in=in8, m_out=out8, a_re=a8r, a_im=a8i)}


def _lane_block_transpose(a):
    a = list(a)
    blk = lax.broadcasted_iota(jnp.int32, a[0].shape, 1) >> 4
    for d in (4, 2, 1):
        upper = (blk & d) != 0
        for r in range(8):
            if r & d:
                continue
            lo, hi = a[r], a[r + d]
            a[r] = jnp.where(upper, pltpu.roll(hi, 16 * d, 1), lo)
            a[r + d] = jnp.where(upper, hi, pltpu.roll(lo, LANES - 16 * d, 1))
    return a


def _in_proj_kernel(*refs, steps, n_tiles, tiles_per_seq):
    fused = tiles_per_seq is not None
    x_ref, gpre_ref, wa_ref, wb_ref, wgk_ref, bgk_ref = refs[:6]
    if fused:
        (s0_ref, g_ref, u_ref, o_ref, ug_ref, sfin_ref,
         u_scr, gk_scr, q_scr, k_scr, v_scr, s_scr) = refs[6:]
    else:
        q_ref, k_ref, v_ref, g_ref, u_ref, lg_ref, ug_ref, u_scr, gk_scr = refs[6:]
    i = pl.program_id(0)
    tm = x_ref.shape[0]

    def project(slot):
        h = _rms(x_ref[...], gpre_ref[...]).astype(BF16)
        proj = _dot(h, wa_ref[...])
        tail_proj = _dot(h, wb_ref[...])
        if fused:
            q_scr[slot] = proj[:, 0:256]
            k_scr[slot] = proj[:, 256:512]
            v_scr[slot] = proj[:, 512:1024].astype(BF16)
        else:
            q_ref[...] = proj[:, 0:256]
            k_ref[...] = proj[:, 256:512]
            v_ref[...] = proj[:, 512:1024]
        g_ref[...] = proj[:, 1024:1536]
        u = tail_proj[:, :D_S5]
        u_ref[...] = u
        for s in range(D_S5 // LANES):
            u_scr[slot, s] = u[:, s * LANES:(s + 1) * LANES]
        gk_scr[slot] = tail_proj[:, D_S5:]

    def tail(slot):
        z = _dot(gk_scr[slot].astype(BF16), wgk_ref[...]) + bgk_ref[...]
        lg = jax.nn.log_sigmoid(z) * (1.0 / GATE_NORM)
        if fused:
            tile = i - 1
            seq = tile // tiles_per_seq
            state = jnp.where(tile % tiles_per_seq == 0, s0_ref[seq], s_scr[...])
            masks = _gla_masks(1, GLA_CHUNK)
            for c in range(tm // GLA_CHUNK):
                rows = slice(c * GLA_CHUNK, (c + 1) * GLA_CHUNK)
                o, (state,) = _gla_group(q_scr[slot, rows, :], k_scr[slot, rows, :], v_scr[slot, rows, :],
                                         lg[rows], [state], masks, GLA_CHUNK)
                o_ref[rows, :] = o
            s_scr[...] = state
            sfin_ref[seq] = state
        else:
            lg_ref[...] = lg
        nr = tm // steps
        rt = min(nr, 16)
        for s in range(D_S5 // LANES):
            for hf in range(steps // 8):
                for r0 in range(0, nr, rt):
                    a = [u_scr[slot, s, pl.ds(r0 * steps + hf * 8 + t, rt, stride=steps), :] for t in range(8)]
                    per_group = _lane_block_transpose(a)
                    for g in range(SLAB_GROUPS):
                        ug_ref[s * SLAB_GROUPS + g, r0:r0 + rt, hf * LANES:(hf + 1) * LANES] = (
                            per_group[g].astype(BF16))

    @pl.when(i == 0)
    def _():
        if fused:
            s_scr[...] = jnp.zeros(s_scr.shape, F32)
        project(0)

    @pl.when((i > 0) & (i < n_tiles))
    def _():
        tail((i - 1) & 1)
        project(i & 1)

    @pl.when(i == n_tiles)
    def _():
        tail((i - 1) & 1)


def _in_proj(x, gpre, w_a, w_b, wgk_p, bgk, tm, steps, gla_s0=None, seq_len=None):
    t = x.shape[0]
    n = t // tm
    cur = lambda i: jnp.minimum(i, n - 1)
    prev = lambda i: jnp.maximum(i - 1, 0)
    row = lambda w, at: pl.BlockSpec((tm, w), lambda i: (at(i), 0))
    wg = steps * S5_GROUP
    ug_spec = pl.BlockSpec((S5_GROUPS, tm // steps, wg), lambda i: (0, prev(i), 0))
    ug_shape = jax.ShapeDtypeStruct((S5_GROUPS, t // steps, wg), BF16)
    f32 = lambda w: jax.ShapeDtypeStruct((t, w), F32)
    in_specs = [row(D_MODEL, cur), _const_spec((1, D_MODEL)), _const_spec(w_a.shape), _const_spec(w_b.shape),
                _const_spec((LANES, GLA_KDIM)), _const_spec((1, GLA_KDIM))]
    scratch = [pltpu.VMEM((2, D_S5 // LANES, tm, LANES), F32), pltpu.VMEM((2, tm, LANES), F32)]
    args = (x, gpre, w_a, w_b, wgk_p, bgk)
    if gla_s0 is None:
        tiles_per_seq = None
        out_specs = [row(w, cur) for w in (256, 256, 512, 512, 512)] + [row(GLA_KDIM, prev), ug_spec]
        out_shape = [f32(w) for w in (256, 256, 512, 512, 512)] + [f32(GLA_KDIM), ug_shape]
    else:
        tiles_per_seq = seq_len // tm
        assert seq_len % tm == 0 and tm % GLA_CHUNK == 0
        in_specs.append(_const_spec(gla_s0.shape))
        args += (gla_s0,)
        out_specs = [row(D_GLA, cur), row(D_S5, cur), row(D_GLA, prev), ug_spec, _const_spec(gla_s0.shape)]
        out_shape = [f32(D_GLA), f32(D_S5), f32(D_GLA), ug_shape, jax.ShapeDtypeStruct(gla_s0.shape, F32)]
        scratch += [pltpu.VMEM((2, tm, GLA_KDIM), F32), pltpu.VMEM((2, tm, GLA_KDIM), F32),
                    pltpu.VMEM((2, tm, D_GLA), BF16), pltpu.VMEM(gla_s0.shape[1:], F32)]
    return pl.pallas_call(
        functools.partial(_in_proj_kernel, steps=steps, n_tiles=n, tiles_per_seq=tiles_per_seq),
        grid=(n + 1,),
        in_specs=in_specs,
        out_specs=out_specs,
        out_shape=out_shape,
        scratch_shapes=scratch,
        compiler_params=pltpu.CompilerParams(dimension_semantics=("arbitrary",),
                                             vmem_limit_bytes=VMEM_LIMIT),
        name="in_proj" if gla_s0 is None else "in_proj_gla",
    )(*args)


def _gla_masks(sg, chunk):
    r = sg * chunk
    assert r >= LANES or sg == 1
    shift = chunk.bit_length() - 1
    hk = GLA_HEADS * r
    ri = lax.broadcasted_iota(jnp.int32, (r, r), 0)
    ci = lax.broadcasted_iota(jnp.int32, (r, r), 1)
    ri4 = lax.broadcasted_iota(jnp.int32, (r, hk), 0)
    ci4 = lax.broadcasted_iota(jnp.int32, (r, hk), 1) & (r - 1)
    stack_head = lax.broadcasted_iota(jnp.int32, (hk, 1), 0) >> (r.bit_length() - 1)
    k_head = lax.broadcasted_iota(jnp.int32, (1, GLA_KDIM), 1) >> 6
    v_head = lax.broadcasted_iota(jnp.int32, (1, D_GLA), 1) >> 7
    s_head = lax.broadcasted_iota(jnp.int32, (GLA_KDIM, 1), 0) >> 6
    return dict(
        tri_bf=jnp.where(((ri >> shift) == (ci >> shift)) & (ri >= ci), 1.0, 0.0).astype(BF16),
        causal4=((ri4 >> shift) == (ci4 >> shift)) & (ri4 >= ci4),
        k_diag=stack_head == k_head, v_diag=stack_head == v_head, s_diag=s_head == v_head)


def _gla_group(q, k, v_bf, lg, states, m, chunk):
    sg = len(states)
    r = sg * chunk
    tile4 = lambda t: jnp.concatenate([t] * GLA_HEADS, axis=0)
    lg_hi = lg.astype(BF16)
    lg_lo = (lg - lg_hi.astype(F32)).astype(BF16)
    b2 = _dot(m["tri_bf"], jnp.concatenate([lg_hi, lg_lo], axis=1))
    b = b2[:, :GLA_KDIM] + b2[:, GLA_KDIM:]
    lasts = [b[(i + 1) * chunk - 1:(i + 1) * chunk, :] for i in range(sg)]
    bl = jnp.concatenate([jnp.broadcast_to(t, (chunk, GLA_KDIM)) for t in lasts], axis=0)
    qd = (q * (GLA_DK ** -0.5) * jnp.exp(b)).astype(BF16)
    ki = (k * jnp.exp(-b)).astype(BF16)
    ke = k * jnp.exp(bl - b)

    ki_bd = jnp.where(m["k_diag"], tile4(ki), 0.0)
    att = jnp.where(m["causal4"], _dot_nt(qd, ki_bd), 0.0).astype(BF16)
    v_bd = jnp.where(m["v_diag"], tile4(v_bf), 0.0)
    o_intra = _dot(att, v_bd)

    if r < LANES:
        aug_t = jnp.concatenate([ke, jnp.broadcast_to(lasts[0], (LANES - r, GLA_KDIM))], axis=0).T
        ke_t, bl_t = aug_t, aug_t[:, r:]
    else:
        ke_t = ke.T
        bl_t = jnp.concatenate(lasts + [jnp.zeros((r - sg, GLA_KDIM), F32)], axis=0).T
    ke_t = ke_t.astype(BF16)

    outs, new_states = [], []
    for i, s_old in enumerate(states):
        rows = slice(i * chunk, (i + 1) * chunk)
        s_bd = jnp.where(m["s_diag"], jnp.concatenate([s_old.astype(BF16)] * GLA_HEADS, axis=1), 0.0)
        outs.append(o_intra[rows] + _dot(qd[rows], s_bd))
        upd = [_dot(ke_t[h * GLA_DK:(h + 1) * GLA_DK, rows], v_bf[rows, h * GLA_DV:(h + 1) * GLA_DV])
               for h in range(GLA_HEADS)]
        new_states.append(jnp.exp(bl_t[:, i:i + 1]) * s_old + jnp.concatenate(upd, axis=0))
    return (outs[0] if sg == 1 else jnp.concatenate(outs, axis=0)), new_states


def _gla_kernel(q_ref, k_ref, v_ref, lg_ref, s0_ref, o_ref, s_ref, *, bb, sg, chunk):
    r = sg * chunk

    @pl.when(pl.program_id(1) == 0)
    def _():
        s_ref[...] = s0_ref[...]

    masks = _gla_masks(sg, chunk)
    for gi in range(bb // sg):
        seqs = slice(gi * sg, (gi + 1) * sg)
        o, new = _gla_group(q_ref[seqs].reshape(r, GLA_KDIM), k_ref[seqs].reshape(r, GLA_KDIM),
                            v_ref[seqs].reshape(r, D_GLA).astype(BF16), lg_ref[seqs].reshape(r, GLA_KDIM),
                            [s_ref[gi * sg + i] for i in range(sg)], masks, chunk)
        o_ref[seqs] = o.reshape(sg, chunk, D_GLA)
        for i in range(sg):
            s_ref[gi * sg + i] = new[i]


def _gla(q, k, v, lg, s0, bb, sg, chunk):
    b, l, _ = q.shape
    blk = lambda w: pl.BlockSpec((bb, chunk, w), lambda i, c: (i, c, 0))
    sspec = pl.BlockSpec((bb, GLA_KDIM, GLA_DV), lambda i, c: (i, 0, 0))
    return pl.pallas_call(
        functools.partial(_gla_kernel, bb=bb, sg=sg, chunk=chunk),
        grid=(b // bb, l // chunk),
        in_specs=[blk(GLA_KDIM), blk(GLA_KDIM), blk(D_GLA), blk(GLA_KDIM), sspec],
        out_specs=[blk(D_GLA), sspec],
        out_shape=[jax.ShapeDtypeStruct((b, l, D_GLA), F32),
                   jax.ShapeDtypeStruct((b, GLA_KDIM, GLA_DV), F32)],
        compiler_params=pltpu.CompilerParams(dimension_semantics=("parallel", "arbitrary"),
                                             vmem_limit_bytes=VMEM_LIMIT),
        name=f"gla_c{chunk}",
    )(q, k, v, lg, s0)


def _s5_kernel(ug_ref, mi_ref, min_ref, mout_ref, ar_ref, ai_ref,
               h0r_ref, h0i_ref, yg_ref, hfr_ref, hfi_ref, vr_s, vi_s, hr_s, hi_s, str_s, sti_s, *, nb, cb):
    gs = SLAB_GROUPS // 2
    n_pairs = S5_GROUPS // 2
    w = ug_ref.shape[-1]
    rows = nb * cb
    batch = cb if nb == 1 else nb
    n_blocks = rows // batch
    interleave = nb > 1 and cb > 1
    first_pair = pl.program_id(0) * gs
    seq0 = pl.program_id(1) * batch if nb == 1 else 0

    def state_rows(p):
        return pl.ds(seq0 * n_pairs + first_pair + p, batch, stride=n_pairs)

    def load_state():
        for p in range(gs):
            str_s[p] = h0r_ref[state_rows(p), :]
            sti_s[p] = h0i_ref[state_rows(p), :]

    if nb == 1:
        load_state()
    else:
        pl.when(pl.program_id(1) == 0)(load_state)

    def to_scan_order(ref, g, val):
        if not interleave:
            ref[g] = val
        else:
            for b in range(nb):
                ref[g, pl.ds(b, cb, stride=nb), :] = val[b * cb:(b + 1) * cb]

    def from_scan_order(ref, g):
        if not interleave:
            return ref[g]
        return jnp.concatenate([ref[g, pl.ds(b, cb, stride=nb), :] for b in range(nb)], axis=0)

    ub = [ug_ref[g].reshape(rows, w) for g in range(2 * gs)]
    for g in range(gs):
        v = _dot(jnp.concatenate([ub[2 * g], ub[2 * g + 1]], axis=1), min_ref[g])
        to_scan_order(vr_s, g, v[:, :LANES])
        to_scan_order(vi_s, g, v[:, LANES:])
    ar = [ar_ref[g] for g in range(gs)]
    ai = [ai_ref[g] for g in range(gs)]

    def body(c, carry):
        rows = pl.ds(pl.multiple_of(c * batch, batch), batch)
        new = []
        for g in range(gs):
            hr, hi = carry[2 * g], carry[2 * g + 1]
            hr_s[g, rows, :] = hr
            hi_s[g, rows, :] = hi
            new.append(ar[g] * hr - ai[g] * hi + vr_s[g, rows, :])
            new.append(ar[g] * hi + ai[g] * hr + vi_s[g, rows, :])
        return tuple(new)

    init = tuple(ref[g] for g in range(gs) for ref in (str_s, sti_s))
    fin = lax.fori_loop(0, n_blocks, body, init, unroll=min(n_blocks, 4))
    for g in range(gs):
        str_s[g] = fin[2 * g]
        sti_s[g] = fin[2 * g + 1]
        hfr_ref[state_rows(g), :] = fin[2 * g]
        hfi_ref[state_rows(g), :] = fin[2 * g + 1]
        h_in = jnp.concatenate([from_scan_order(hr_s, g), from_scan_order(hi_s, g)], axis=1).astype(BF16)
        y_state = _dot(h_in, mout_ref[g])
        for k in range(2):
            y = _dot(ub[2 * g + k], mi_ref[2 * g + k]) + y_state[:, k * w:(k + 1) * w]
            yg_ref[2 * g + k] = y.reshape(yg_ref.shape[1:])


def _s5(ug, m, h0_re, h0_im, nb, cb):
    g_all, r, w = ug.shape
    nc = r // nb
    batch = cb if nb == 1 else nb
    n, gs = LANES, SLAB_GROUPS
    uspec = pl.BlockSpec((gs, nb, cb, w), lambda s, i: (s, 0, i, 0))
    per_g = lambda a, b_: pl.BlockSpec((gs, a, b_), lambda s, i: (s, 0, 0))
    per_p = lambda a, b_: pl.BlockSpec((gs // 2, a, b_), lambda s, i: (s, 0, 0))
    hspec = _const_spec(h0_re.shape)
    yg, hf_re, hf_im = pl.pallas_call(
        functools.partial(_s5_kernel, nb=nb, cb=cb),
        grid=(g_all // gs, nc // cb),
        in_specs=[uspec, per_g(w, w), per_p(2 * w, 2 * n), per_p(2 * n, 2 * w),
                  per_p(1, n), per_p(1, n), hspec, hspec],
        out_specs=[uspec, hspec, hspec],
        out_shape=[jax.ShapeDtypeStruct((g_all, nb, nc, w), F32),
                   jax.ShapeDtypeStruct(h0_re.shape, F32),
                   jax.ShapeDtypeStruct(h0_im.shape, F32)],
        scratch_shapes=[pltpu.VMEM((gs // 2, nb * cb, n), F32)] * 4 + [pltpu.VMEM((gs // 2, batch, n), F32)] * 2,
        compiler_params=pltpu.CompilerParams(dimension_semantics=("arbitrary", "arbitrary"),
                                             vmem_limit_bytes=VMEM_LIMIT),
        name=f"s5_w{w}",
    )(ug.reshape(g_all, nb, nc, w), m["m_intra"], m["m_in"], m["m_out"], m["a_re"], m["a_im"], h0_re, h0_im)
    return yg.reshape(g_all, r, w), hf_re, hf_im


def _mix_ffn_kernel(x_ref, o_ref, g_ref, yg_ref, u_ref, dsk_ref, gn_ref, wglu_ref, s5n_ref, wo_ref, gpost_ref,
                    gpre_ref, wg_hbm, wu_hbm, wd_hbm, gpostf_ref, out_ref, y_scr, x1_scr,
                    wg_ref, wu_ref, wd_ref, w_sem, *, steps, n_tiles):
    i = pl.program_id(0)

    def weight_copies():
        pairs = ((wg_hbm, wg_ref), (wu_hbm, wu_ref), (wd_hbm, wd_ref))
        return [pltpu.make_async_copy(src, dst, w_sem.at[k]) for k, (src, dst) in enumerate(pairs)]

    def mix_into(slot):
        nr = x_ref.shape[0] // steps
        rt = min(nr, 8)
        for s in range(D_S5 // LANES):
            for hf in range(steps // 8):
                for r0 in range(0, nr, rt):
                    a = [yg_ref[s * SLAB_GROUPS + g, r0:r0 + rt, hf * LANES:(hf + 1) * LANES]
                         for g in range(SLAB_GROUPS)]
                    per_step = _lane_block_transpose(a)
                    for t in range(8):
                        y_scr[s, pl.ds(r0 * steps + hf * 8 + t, rt, stride=steps), :] = per_step[t]
        y5 = jnp.concatenate([y_scr[s] for s in range(D_S5 // LANES)], axis=1) + dsk_ref[...] * u_ref[...]
        o = o_ref[...]
        gn = gn_ref[...]
        heads = []
        for h in range(GLA_HEADS):
            heads.append(_rms(o[:, h * GLA_DV:(h + 1) * GLA_DV], gn))
        og = jnp.concatenate(heads, axis=1) * jax.nn.silu(g_ref[...])
        y = jax.nn.gelu(y5)
        y = y * jax.nn.sigmoid(_dot(y.astype(BF16), wglu_ref[...]))
        y = _rms(y, s5n_ref[...])
        mix = _dot(og.astype(BF16), wo_ref[:D_GLA, :]) + _dot(y.astype(BF16), wo_ref[D_GLA:, :])
        x1_scr[slot] = x_ref[...] + _rms(mix, gpost_ref[...])

    def ffn_from(slot):
        x = x1_scr[slot]
        h = _rms(x, gpre_ref[...]).astype(BF16)
        acc = jnp.zeros(x.shape, F32)
        for c in range(D_FF // FF_CHUNK):
            cols = slice(c * FF_CHUNK, (c + 1) * FF_CHUNK)
            act = jax.nn.silu(_dot(h, wg_ref[:, cols])) * _dot(h, wu_ref[:, cols])
            acc = acc + _dot(act.astype(BF16), wd_ref[cols, :])
        out_ref[...] = x + _rms(acc, gpostf_ref[...])

    @pl.when(i == 0)
    def _():
        for c in weight_copies():
            c.start()
        mix_into(0)

    @pl.when(i == 1)
    def _():
        for c in weight_copies():
            c.wait()

    @pl.when((i > 0) & (i < n_tiles))
    def _():
        ffn_from((i - 1) & 1)
        mix_into(i & 1)

    @pl.when(i == n_tiles)
    def _():
        ffn_from((i - 1) & 1)


def _mix_ffn(x, o, g, yg, u, dsk, gn, wglu, s5n, wo, gpost, gpre, wg, wu, wd, gpostf, tm, steps):
    t = x.shape[0]
    n = t // tm
    cur = lambda i: jnp.minimum(i, n - 1)
    row = lambda w: pl.BlockSpec((tm, w), lambda i: (cur(i), 0))
    fixed = lambda shape: pl.BlockSpec(shape, lambda i: (0,) * len(shape), pipeline_mode=pl.Buffered(1))
    in_hbm = pl.BlockSpec(memory_space=pl.ANY)
    return pl.pallas_call(
        functools.partial(_mix_ffn_kernel, steps=steps, n_tiles=n),
        grid=(n + 1,),
        in_specs=[row(D_MODEL), row(D_GLA), row(D_GLA),
                  pl.BlockSpec((S5_GROUPS, tm // steps, steps * S5_GROUP), lambda i: (0, cur(i), 0)),
                  row(D_S5), fixed((1, D_S5)),
                  fixed((1, GLA_DV)), fixed((D_S5, D_S5)), fixed((1, D_S5)),
                  fixed((D_GLA + D_S5, D_MODEL)), fixed((1, D_MODEL)),
                  fixed((1, D_MODEL)), in_hbm, in_hbm, in_hbm, fixed((1, D_MODEL))],
        out_specs=pl.BlockSpec((tm, D_MODEL), lambda i: (jnp.maximum(i - 1, 0), 0)),
        out_shape=jax.ShapeDtypeStruct((t, D_MODEL), F32),
        scratch_shapes=[pltpu.VMEM((D_S5 // LANES, tm, LANES), F32), pltpu.VMEM((2, tm, D_MODEL), F32),
                        pltpu.VMEM(wg.shape, BF16), pltpu.VMEM(wu.shape, BF16), pltpu.VMEM(wd.shape, BF16),
                        pltpu.SemaphoreType.DMA((3,))],
        compiler_params=pltpu.CompilerParams(dimension_semantics=("arbitrary",),
                                             vmem_limit_bytes=VMEM_LIMIT),
        name="mix_ffn",
    )(x, o, g, yg, u, dsk, gn, wglu, s5n, wo, gpost, gpre, wg, wu, wd, gpostf)


def kernel(x_prompt, x_sample, state_gla, state_s5_re, state_s5_im, meta_tokens, g_pre_mix, w_in, w_gk2, b_gk, gla_norm, s5_a_re, s5_a_im, s5_b_re, s5_b_im, s5_c_re, s5_c_im, s5_d, s5_log_dt, w_s5_glu, s5_norm, w_o, g_post_mix, g_pre_ffn, w_gate, w_up, w_down, g_post_ffn):
    assert g_pre_mix.shape[0] == 1, "single-layer step"
    bp, seq_p, _ = x_prompt.shape
    bs, seq_s, _ = x_sample.shape
    row = lambda t: t[0].reshape(1, -1)

    w = w_in[0]
    c3, c4 = 2 * GLA_KDIM + 2 * D_GLA, 2 * GLA_KDIM + 2 * D_GLA + GATE_RANK
    w_a = w[:, :c3].astype(BF16)
    w_b = jnp.concatenate([w[:, c4:].astype(BF16), w[:, c3:c4].astype(BF16),
                           jnp.zeros((D_MODEL, LANES - GATE_RANK), BF16)], axis=1)
    wgk_p = jnp.concatenate([w_gk2[0], jnp.zeros((LANES - GATE_RANK, GLA_KDIM), F32)], axis=0).astype(BF16)
    wo_bf = w_o[0].astype(BF16)
    s5m = _s5_prep(s5_a_re[0], s5_a_im[0], s5_b_re[0], s5_b_im[0], s5_c_re[0], s5_c_im[0], s5_log_dt[0])
    proj_w = (row(g_pre_mix), w_a, w_b, wgk_p, row(b_gk))

    def finish(x, o, g, yg, u, tm, steps):
        return _mix_ffn(x, o, g, yg, u, row(s5_d), row(gla_norm), w_s5_glu[0].astype(BF16), row(s5_norm),
                        wo_bf, row(g_post_mix), row(g_pre_ffn), w_gate[0].astype(BF16),
                        w_up[0].astype(BF16), w_down[0].astype(BF16), row(g_post_ffn), tm, steps)

    xm = jnp.broadcast_to(meta_tokens[None], (bp, N_META, D_MODEL)).reshape(bp * N_META, D_MODEL)
    q, k, v, _, _, lg, ug = _in_proj(xm, *proj_w, bp * N_META, S5_BLOCK)
    r3 = lambda t, b, l: t.reshape(b, l, t.shape[-1])
    _, s_meta = _gla(r3(q, bp, N_META), r3(k, bp, N_META), r3(v, bp, N_META), r3(lg, bp, N_META),
                     jnp.zeros((bp, GLA_KDIM, GLA_DV), F32), bp, bp, N_META)
    zh = jnp.zeros((bp * S5_GROUPS // 2, LANES), F32)
    _, hm_re, hm_im = _s5(ug, s5m[S5_BLOCK], zh, zh, 1, bp)

    xp = x_prompt.reshape(bp * seq_p, D_MODEL)
    g, u, o, ug, s_p = _in_proj(xp, *proj_w, TOKEN_TILE, S5_BLOCK, gla_s0=s_meta, seq_len=seq_p)
    yg, hp_re, hp_im = _s5(ug, s5m[S5_BLOCK], hm_re, hm_im, bp, S5_ROW_BLOCKS)
    y_prompt = finish(xp, o, g, yg, u, TOKEN_TILE, S5_BLOCK)

    xs = x_sample.reshape(bs * seq_s, D_MODEL)
    to_g = lambda t: t[0].reshape(bs * S5_GROUPS // 2, LANES)
    q, k, v, g, u, lg, ug = _in_proj(xs, *proj_w, TOKEN_TILE, seq_s)
    o, s_s = _gla(r3(q, bs, seq_s), r3(k, bs, seq_s), r3(v, bs, seq_s), r3(lg, bs, seq_s),
                  state_gla[0].reshape(bs, GLA_KDIM, GLA_DV), GLA_SAMPLE_SEQS, GLA_SAMPLE_GROUP, seq_s)
    yg, hs_re, hs_im = _s5(ug, s5m[seq_s], to_g(state_s5_re), to_g(state_s5_im), 1, bs)
    y_sample = finish(xs, o.reshape(bs * seq_s, D_GLA), g, yg, u, TOKEN_TILE, seq_s)

    gla_out = lambda s, b: s.reshape(1, b, GLA_HEADS, GLA_DK, GLA_DV)
    s5_out = lambda h: h.reshape(1, -1, S5_GROUPS, S5_STATE)
    return (y_prompt.reshape(bp, seq_p, D_MODEL), y_sample.reshape(bs, seq_s, D_MODEL),
            gla_out(s_p, bp), s5_out(hp_re), s5_out(hp_im),
            gla_out(s_s, bs), s5_out(hs_re), s5_out(hs_im))
```

```python
import functools

import jax
import jax.numpy as jnp
from jax import lax
from jax.experimental import pallas as pl
from jax.experimental.pallas import tpu as pltpu

F32 = jnp.float32
BF16 = jnp.bfloat16

D_MODEL = 1024
D_GLA = 512
GLA_HEADS = 4
GLA_DV = 128
GLA_DK = 64
GLA_KDIM = 256
GATE_RANK = 16
GATE_NORM = 16.0
GLA_CHUNK = 64
D_S5 = 512
S5_GROUP = 16
S5_GROUPS = 32
S5_STATE = 64
N_META = 16
D_FF = 2816
EPS = 1e-6
LANES = 128
S5_BLOCK = 16
SLAB_GROUPS = LANES // S5_GROUP
FF_CHUNK = 256
TOKEN_TILE = 512
S5_ROW_BLOCKS = 128
GLA_SAMPLE_GROUP = 16
GLA_SAMPLE_SEQS = 32
VMEM_LIMIT = 48 * 1024 * 1024


def _rms(x, g):
    return x * lax.rsqrt(jnp.mean(x * x, axis=-1, keepdims=True) + EPS) * g


def _dot(a, b):
    return jnp.dot(a, b, preferred_element_type=F32)


def _dot_nt(a, b):
    return lax.dot_general(a, b, (((1,), (1,)), ((), ())), preferred_element_type=F32)


def _dot_tn(a, b):
    return lax.dot_general(a, b, (((0,), (0,)), ((), ())), preferred_element_type=F32)


def _const_spec(shape):
    zeros = (0,) * len(shape)
    return pl.BlockSpec(shape, lambda *_: zeros)


PREP_GROUPS = 8


def _cmul(a, b):
    return a[0] * b[0] - a[1] * b[1], a[0] * b[1] + a[1] * b[0]


def _unit_powers(c1, s1, expo, n_bits):
    acc = (jnp.ones_like(c1), jnp.zeros_like(c1))
    base = (c1, s1)
    squares = [base]
    for bit in range(n_bits):
        take = ((expo >> bit) & 1) == 1
        nxt = _cmul(acc, base)
        acc = (jnp.where(take, nxt[0], acc[0]), jnp.where(take, nxt[1], acc[1]))
        base = _cmul(base, base)
        squares.append(base)
    return acc, squares


def _s5_prep_kernel(arc_ref, aic_ref, air_ref, ldt_ref, br_ref, bi_ref, cr_ref, ci_ref,
                    mi16_ref, mi8_ref, in16_ref, in8_ref, out16_ref, out8_ref,
                    a16r_ref, a16i_ref, a8r_ref, a8i_ref):
    n, q = S5_STATE, S5_BLOCK
    w = q * S5_GROUP
    hp = lax.Precision.HIGHEST
    lane = lax.broadcasted_iota(jnp.int32, (n, w), 1)
    t_blk = lane >> 4
    eye = lax.broadcasted_iota(jnp.int32, (n, n), 0) == lax.broadcasted_iota(jnp.int32, (n, n), 1)
    to_col = lambda r: jnp.sum(jnp.where(eye, r, 0.0), axis=1, keepdims=True)
    zrows = lambda x: jnp.concatenate([x, jnp.zeros_like(x)], axis=0)
    lane_n = lax.broadcasted_iota(jnp.int32, (n, LANES), 1)
    rep = jnp.where((lax.broadcasted_iota(jnp.int32, (S5_GROUP, w), 1) & (S5_GROUP - 1))
                    == lax.broadcasted_iota(jnp.int32, (S5_GROUP, w), 0), 1.0, 0.0)
    mo_g, rev_g, hop_g = [], [], []
    for g in range(PREP_GROUPS):
        dt = jnp.exp(ldt_ref[g])
        ang_r = air_ref[g] * dt
        c1, s1 = to_col(jnp.cos(ang_r)), to_col(jnp.sin(ang_r))
        lam_re = jnp.minimum(arc_ref[g], -1e-4)
        lam_im = aic_ref[g]
        unit, squares = _unit_powers(c1, s1, t_blk, 4)
        pm = jnp.exp(t_blk.astype(F32) * (lam_re * dt))
        pk = (pm * unit[0], pm * unit[1])
        mag = jnp.exp(lam_re * dt)
        ab = (mag * c1, mag * s1)
        p1 = _cmul(pk, ab)
        ct = tuple(lax.dot_general(r[g], rep, (((0,), (0,)), ((), ())), precision=hp,
                                   preferred_element_type=F32) for r in (cr_ref, ci_ref))
        bt = tuple(jnp.dot(r[g], rep, precision=hp, preferred_element_type=F32) for r in (br_ref, bi_ref))
        g0 = _cmul(ct, pk)
        mo = _cmul(ct, p1)
        mo_g.append((mo[0], -mo[1]))
        den = lam_re * lam_re + lam_im * lam_im
        nr, ni = ab[0] - 1.0, ab[1]
        f = ((nr * lam_re + ni * lam_im) / den, (ni * lam_re - nr * lam_im) / den)
        e = _cmul(pk, _cmul(f, bt))
        et = [zrows(x).T for x in e]
        rev_g.append([jnp.concatenate([x[(q - 1 - s) * S5_GROUP:(q - s) * S5_GROUP] for s in range(q)], axis=0)
                      for x in et])
        hops = []
        for steps in (q, q // 2):
            m = jnp.exp(float(steps) * (lam_re * dt))
            u = squares[steps.bit_length() - 1]
            hops += [m * u[0], m * u[1]]
        cols = jnp.zeros((n, LANES), F32)
        for idx, hcol in enumerate(hops):
            cols = jnp.where(lane_n == idx, hcol, cols)
        hop_g.append(zrows(cols).T)
        t0 = (jnp.dot(et[0][:S5_GROUP, :n], g0[0], precision=hp, preferred_element_type=F32)
              - jnp.dot(et[1][:S5_GROUP, :n], g0[1], precision=hp, preferred_element_type=F32))
        lane_t = lax.broadcasted_iota(jnp.int32, t0.shape, 1) >> 4
        for s in range(q):
            blk = t0 if s == 0 else jnp.where(lane_t >= s, pltpu.roll(t0, S5_GROUP * s, 1), 0.0)
            mi16_ref[g, s * S5_GROUP:(s + 1) * S5_GROUP, :] = blk.astype(BF16)
            if s < q // 2:
                mi8_ref[g, s * S5_GROUP:(s + 1) * S5_GROUP, :] = blk[:, :w // 2].astype(BF16)

    half = lambda x: pltpu.roll(x, n, 1)
    for p in range(PREP_GROUPS // 2):
        g0, g1 = 2 * p, 2 * p + 1
        for c in range(2):
            part = slice(c * LANES, (c + 1) * LANES)
            lo, hi = rev_g[g0][c], half(rev_g[g1][c])
            in16_ref[p, :, part] = jnp.concatenate([lo, hi], axis=0).astype(BF16)
            in8_ref[p, :, part] = jnp.concatenate([lo[w // 2:], hi[w // 2:]], axis=0).astype(BF16)
            m0, m1 = mo_g[g0][c], mo_g[g1][c]
            z, zh = jnp.zeros_like(m0), jnp.zeros((n, w // 2), F32)
            out16_ref[p, part, :] = jnp.concatenate([jnp.concatenate([m0, z], axis=1),
                                                     jnp.concatenate([z, m1], axis=1)], axis=0).astype(BF16)
            out8_ref[p, part, :] = jnp.concatenate([jnp.concatenate([m0[:, :w // 2], zh], axis=1),
                                                    jnp.concatenate([zh, m1[:, :w // 2]], axis=1)],
                                                   axis=0).astype(BF16)
        hops = hop_g[g0] + half(hop_g[g1])
        a16r_ref[p] = hops[0:1]
        a16i_ref[p] = hops[1:2]
        a8r_ref[p] = hops[2:3]
        a8i_ref[p] = hops[3:4]


def _s5_prep(a_re, a_im, b_re, b_im, c_re, c_im, log_dt):
    g, n, j, q = S5_GROUPS, S5_STATE, S5_GROUP, S5_BLOCK
    w, h, pg = q * j, q * j // 2, PREP_GROUPS
    col = lambda t: t.reshape(g, n, 1)
    rowv = lambda t: t.reshape(g, 1, n)
    spec = lambda a, b_: pl.BlockSpec((pg, a, b_), lambda i: (i, 0, 0))
    pspec = lambda a, b_: pl.BlockSpec((pg // 2, a, b_), lambda i: (i, 0, 0))
    pair_shapes = [(2 * w, 2 * LANES), (2 * h, 2 * LANES), (2 * LANES, 2 * w), (2 * LANES, 2 * h)]
    outs = pl.pallas_call(
        _s5_prep_kernel,
        grid=(g // pg,),
        in_specs=[spec(n, 1), spec(n, 1), spec(1, n), spec(1, 1),
                  spec(n, j), spec(n, j), spec(j, n), spec(j, n)],
        out_specs=[spec(w, w), spec(h, h)] + [pspec(*s) for s in pair_shapes] + [pspec(1, LANES)] * 4,
        out_shape=[jax.ShapeDtypeStruct((g, w, w), BF16), jax.ShapeDtypeStruct((g, h, h), BF16)]
                  + [jax.ShapeDtypeStruct((g // 2,) + s, BF16) for s in pair_shapes]
                  + [jax.ShapeDtypeStruct((g // 2, 1, LANES), F32)] * 4,
        compiler_params=pltpu.CompilerParams(dimension_semantics=("parallel",)),
        name="s5_prep",
    )(col(a_re), col(a_im), rowv(a_im), log_dt.reshape(g, 1, 1),
      b_re, b_im, c_re, c_im)
    mi16, mi8, in16, in8, out16, out8, a16r, a16i, a8r, a8i = outs
    return {q: dict(m_intra=mi16, m_in=in16, m_out=out16, a_re=a16r, a_im=a16i),
            q // 2: dict(m_intra=mi8, m_in=in8, m_out=out8, a_re=a8r, a_im=a8i)}


def _lane_block_transpose(a):
    a = list(a)
    blk = lax.broadcasted_iota(jnp.int32, a[0].shape, 1) >> 4
    for d in (4, 2, 1):
        upper = (blk & d) != 0
        for r in range(8):
            if r & d:
                continue
            lo, hi = a[r], a[r + d]
            a[r] = jnp.where(upper, pltpu.roll(hi, 16 * d, 1), lo)
            a[r + d] = jnp.where(upper, hi, pltpu.roll(lo, LANES - 16 * d, 1))
    return a


def _in_proj_kernel(*refs, steps, n_tiles, tiles_per_seq):
    fused = tiles_per_seq is not None
    x_ref, gpre_ref, wa_ref, wb_ref, wgk_ref, bgk_ref = refs[:6]
    if fused:
        (s0_ref, g_ref, u_ref, o_ref, ug_ref, sfin_ref,
         u_scr, gk_scr, q_scr, k_scr, v_scr, s_scr) = refs[6:]
    else:
        q_ref, k_ref, v_ref, g_ref, u_ref, lg_ref, ug_ref, u_scr, gk_scr = refs[6:]
    i = pl.program_id(0)
    tm = x_ref.shape[0]

    def project(slot):
        h = _rms(x_ref[...], gpre_ref[...]).astype(BF16)
        proj = _dot(h, wa_ref[...])
        tail_proj = _dot(h, wb_ref[...])
        if fused:
            q_scr[slot] = proj[:, 0:256]
            k_scr[slot] = proj[:, 256:512]
            v_scr[slot] = proj[:, 512:1024].astype(BF16)
        else:
            q_ref[...] = proj[:, 0:256]
            k_ref[...] = proj[:, 256:512]
            v_ref[...] = proj[:, 512:1024]
        g_ref[...] = proj[:, 1024:1536]
        u = tail_proj[:, :D_S5]
        u_ref[...] = u
        for s in range(D_S5 // LANES):
            u_scr[slot, s] = u[:, s * LANES:(s + 1) * LANES]
        gk_scr[slot] = tail_proj[:, D_S5:]

    def tail(slot):
        z = _dot(gk_scr[slot].astype(BF16), wgk_ref[...]) + bgk_ref[...]
        lg = jax.nn.log_sigmoid(z) * (1.0 / GATE_NORM)
        if fused:
            tile = i - 1
            seq = tile // tiles_per_seq
            state = jnp.where(tile % tiles_per_seq == 0, s0_ref[seq], s_scr[...])
            masks = _gla_masks(1, GLA_CHUNK)
            for c in range(tm // GLA_CHUNK):
                rows = slice(c * GLA_CHUNK, (c + 1) * GLA_CHUNK)
                o, (state,) = _gla_group(q_scr[slot, rows, :], k_scr[slot, rows, :], v_scr[slot, rows, :],
                                         lg[rows], [state], masks, GLA_CHUNK)
                o_ref[rows, :] = o
            s_scr[...] = state
            sfin_ref[seq] = state
        else:
            lg_ref[...] = lg
        nr = tm // steps
        rt = min(nr, 16)
        for s in range(D_S5 // LANES):
            for hf in range(steps // 8):
                for r0 in range(0, nr, rt):
                    a = [u_scr[slot, s, pl.ds(r0 * steps + hf * 8 + t, rt, stride=steps), :] for t in range(8)]
                    per_group = _lane_block_transpose(a)
                    for g in range(SLAB_GROUPS):
                        ug_ref[s * SLAB_GROUPS + g, r0:r0 + rt, hf * LANES:(hf + 1) * LANES] = (
                            per_group[g].astype(BF16))

    @pl.when(i == 0)
    def _():
        if fused:
            s_scr[...] = jnp.zeros(s_scr.shape, F32)
        project(0)

    @pl.when((i > 0) & (i < n_tiles))
    def _():
        tail((i - 1) & 1)
        project(i & 1)

    @pl.when(i == n_tiles)
    def _():
        tail((i - 1) & 1)


def _in_proj(x, gpre, w_a, w_b, wgk_p, bgk, tm, steps, gla_s0=None, seq_len=None):
    t = x.shape[0]
    n = t // tm
    cur = lambda i: jnp.minimum(i, n - 1)
    prev = lambda i: jnp.maximum(i - 1, 0)
    row = lambda w, at: pl.BlockSpec((tm, w), lambda i: (at(i), 0))
    wg = steps * S5_GROUP
    ug_spec = pl.BlockSpec((S5_GROUPS, tm // steps, wg), lambda i: (0, prev(i), 0))
    ug_shape = jax.ShapeDtypeStruct((S5_GROUPS, t // steps, wg), BF16)
    f32 = lambda w: jax.ShapeDtypeStruct((t, w), F32)
    in_specs = [row(D_MODEL, cur), _const_spec((1, D_MODEL)), _const_spec(w_a.shape), _const_spec(w_b.shape),
                _const_spec((LANES, GLA_KDIM)), _const_spec((1, GLA_KDIM))]
    scratch = [pltpu.VMEM((2, D_S5 // LANES, tm, LANES), F32), pltpu.VMEM((2, tm, LANES), F32)]
    args = (x, gpre, w_a, w_b, wgk_p, bgk)
    if gla_s0 is None:
        tiles_per_seq = None
        out_specs = [row(w, cur) for w in (256, 256, 512, 512, 512)] + [row(GLA_KDIM, prev), ug_spec]
        out_shape = [f32(w) for w in (256, 256, 512, 512, 512)] + [f32(GLA_KDIM), ug_shape]
    else:
        tiles_per_seq = seq_len // tm
        assert seq_len % tm == 0 and tm % GLA_CHUNK == 0
        in_specs.append(_const_spec(gla_s0.shape))
        args += (gla_s0,)
        out_specs = [row(D_GLA, cur), row(D_S5, cur), row(D_GLA, prev), ug_spec, _const_spec(gla_s0.shape)]
        out_shape = [f32(D_GLA), f32(D_S5), f32(D_GLA), ug_shape, jax.ShapeDtypeStruct(gla_s0.shape, F32)]
        scratch += [pltpu.VMEM((2, tm, GLA_KDIM), F32), pltpu.VMEM((2, tm, GLA_KDIM), F32),
                    pltpu.VMEM((2, tm, D_GLA), BF16), pltpu.VMEM(gla_s0.shape[1:], F32)]
    return pl.pallas_call(
        functools.partial(_in_proj_kernel, steps=steps, n_tiles=n, tiles_per_seq=tiles_per_seq),
        grid=(n + 1,),
        in_specs=in_specs,
        out_specs=out_specs,
        out_shape=out_shape,
        scratch_shapes=scratch,
        compiler_params=pltpu.CompilerParams(dimension_semantics=("arbitrary",),
                                             vmem_limit_bytes=VMEM_LIMIT),
        name="in_proj" if gla_s0 is None else "in_proj_gla",
    )(*args)


def _gla_masks(sg, chunk):
    r = sg * chunk
    assert r >= LANES or sg == 1
    shift = chunk.bit_length() - 1
    hk = GLA_HEADS * r
    ri = lax.broadcasted_iota(jnp.int32, (r, r), 0)
    ci = lax.broadcasted_iota(jnp.int32, (r, r), 1)
    ri4 = lax.broadcasted_iota(jnp.int32, (r, hk), 0)
    ci4 = lax.broadcasted_iota(jnp.int32, (r, hk), 1) & (r - 1)
    stack_head = lax.broadcasted_iota(jnp.int32, (hk, 1), 0) >> (r.bit_length() - 1)
    k_head = lax.broadcasted_iota(jnp.int32, (1, GLA_KDIM), 1) >> 6
    v_head = lax.broadcasted_iota(jnp.int32, (1, D_GLA), 1) >> 7
    s_head = lax.broadcasted_iota(jnp.int32, (GLA_KDIM, 1), 0) >> 6
    return dict(
        tri_bf=jnp.where(((ri >> shift) == (ci >> shift)) & (ri >= ci), 1.0, 0.0).astype(BF16),
        causal4=((ri4 >> shift) == (ci4 >> shift)) & (ri4 >= ci4),
        k_diag=stack_head == k_head, v_diag=stack_head == v_head, s_diag=s_head == v_head)


def _gla_group(q, k, v_bf, lg, states, m, chunk):
    sg = len(states)
    r = sg * chunk
    tile4 = lambda t: jnp.concatenate([t] * GLA_HEADS, axis=0)
    lg_hi = lg.astype(BF16)
    lg_lo = (lg - lg_hi.astype(F32)).astype(BF16)
    b2 = _dot(m["tri_bf"], jnp.concatenate([lg_hi, lg_lo], axis=1))
    b = b2[:, :GLA_KDIM] + b2[:, GLA_KDIM:]
    lasts = [b[(i + 1) * chunk - 1:(i + 1) * chunk, :] for i in range(sg)]
    bl = jnp.concatenate([jnp.broadcast_to(t, (chunk, GLA_KDIM)) for t in lasts], axis=0)
    qd = (q * (GLA_DK ** -0.5) * jnp.exp(b)).astype(BF16)
    ki = (k * jnp.exp(-b)).astype(BF16)
    ke = k * jnp.exp(bl - b)

    ki_bd = jnp.where(m["k_diag"], tile4(ki), 0.0)
    att = jnp.where(m["causal4"], _dot_nt(qd, ki_bd), 0.0).astype(BF16)
    v_bd = jnp.where(m["v_diag"], tile4(v_bf), 0.0)
    o_intra = _dot(att, v_bd)

    if r < LANES:
        aug_t = jnp.concatenate([ke, jnp.broadcast_to(lasts[0], (LANES - r, GLA_KDIM))], axis=0).T
        ke_t, bl_t = aug_t, aug_t[:, r:]
    else:
        ke_t = ke.T
        bl_t = jnp.concatenate(lasts + [jnp.zeros((r - sg, GLA_KDIM), F32)], axis=0).T
    ke_t = ke_t.astype(BF16)

    outs, new_states = [], []
    for i, s_old in enumerate(states):
        rows = slice(i * chunk, (i + 1) * chunk)
        s_bd = jnp.where(m["s_diag"], jnp.concatenate([s_old.astype(BF16)] * GLA_HEADS, axis=1), 0.0)
        outs.append(o_intra[rows] + _dot(qd[rows], s_bd))
        upd = [_dot(ke_t[h * GLA_DK:(h + 1) * GLA_DK, rows], v_bf[rows, h * GLA_DV:(h + 1) * GLA_DV])
               for h in range(GLA_HEADS)]
        new_states.append(jnp.exp(bl_t[:, i:i + 1]) * s_old + jnp.concatenate(upd, axis=0))
    return (outs[0] if sg == 1 else jnp.concatenate(outs, axis=0)), new_states


def _gla_kernel(q_ref, k_ref, v_ref, lg_ref, s0_ref, o_ref, s_ref, *, bb, sg, chunk):
    r = sg * chunk

    @pl.when(pl.program_id(1) == 0)
    def _():
        s_ref[...] = s0_ref[...]

    masks = _gla_masks(sg, chunk)
    for gi in range(bb // sg):
        seqs = slice(gi * sg, (gi + 1) * sg)
        o, new = _gla_group(q_ref[seqs].reshape(r, GLA_KDIM), k_ref[seqs].reshape(r, GLA_KDIM),
                            v_ref[seqs].reshape(r, D_GLA).astype(BF16), lg_ref[seqs].reshape(r, GLA_KDIM),
                            [s_ref[gi * sg + i] for i in range(sg)], masks, chunk)
        o_ref[seqs] = o.reshape(sg, chunk, D_GLA)
        for i in range(sg):
            s_ref[gi * sg + i] = new[i]


def _gla(q, k, v, lg, s0, bb, sg, chunk):
    b, l, _ = q.shape
    blk = lambda w: pl.BlockSpec((bb, chunk, w), lambda i, c: (i, c, 0))
    sspec = pl.BlockSpec((bb, GLA_KDIM, GLA_DV), lambda i, c: (i, 0, 0))
    return pl.pallas_call(
        functools.partial(_gla_kernel, bb=bb, sg=sg, chunk=chunk),
        grid=(b // bb, l // chunk),
        in_specs=[blk(GLA_KDIM), blk(GLA_KDIM), blk(D_GLA), blk(GLA_KDIM), sspec],
        out_specs=[blk(D_GLA), sspec],
        out_shape=[jax.ShapeDtypeStruct((b, l, D_GLA), F32),
                   jax.ShapeDtypeStruct((b, GLA_KDIM, GLA_DV), F32)],
        compiler_params=pltpu.CompilerParams(dimension_semantics=("parallel", "arbitrary"),
                                             vmem_limit_bytes=VMEM_LIMIT),
        name=f"gla_c{chunk}",
    )(q, k, v, lg, s0)


def _s5_kernel(*refs, nb, cb, lead):
    ug_ref, refs = refs[0], refs[1:]
    if lead:
        lead_ref, refs = refs[0], refs[1:]
    (mi_ref, min_ref, mout_ref, ar_ref, ai_ref, h0r_ref, h0i_ref, yg_ref, hfr_ref, hfi_ref,
     vr_s, vi_s, hr_s, hi_s, str_s, sti_s) = refs
    gs = SLAB_GROUPS // 2
    n_pairs = S5_GROUPS // 2
    w = ug_ref.shape[-1]
    rows = nb * cb
    batch = cb if nb == 1 else nb
    n_blocks = rows // batch
    interleave = nb > 1 and cb > 1
    first_pair = pl.program_id(0) * gs
    seq0 = pl.program_id(1) * batch if nb == 1 else 0

    def state_rows(p):
        return pl.ds(seq0 * n_pairs + first_pair + p, batch, stride=n_pairs)

    def load_state():
        for p in range(gs):
            hr, hi = h0r_ref[state_rows(p), :], h0i_ref[state_rows(p), :]
            if lead:
                v = _dot(jnp.concatenate([lead_ref[2 * p], lead_ref[2 * p + 1]], axis=1), min_ref[p])
                hr, hi = (ar_ref[p] * hr - ai_ref[p] * hi + v[:, :LANES],
                          ar_ref[p] * hi + ai_ref[p] * hr + v[:, LANES:])
            str_s[p] = hr
            sti_s[p] = hi

    if nb == 1:
        load_state()
    else:
        pl.when(pl.program_id(1) == 0)(load_state)

    def to_scan_order(ref, g, val):
        if not interleave:
            ref[g] = val
        else:
            for b in range(nb):
                ref[g, pl.ds(b, cb, stride=nb), :] = val[b * cb:(b + 1) * cb]

    def from_scan_order(ref, g):
        if not interleave:
            return ref[g]
        return jnp.concatenate([ref[g, pl.ds(b, cb, stride=nb), :] for b in range(nb)], axis=0)

    ub = [ug_ref[g].reshape(rows, w) for g in range(2 * gs)]
    for g in range(gs):
        v = _dot(jnp.concatenate([ub[2 * g], ub[2 * g + 1]], axis=1), min_ref[g])
        to_scan_order(vr_s, g, v[:, :LANES])
        to_scan_order(vi_s, g, v[:, LANES:])
    ar = [ar_ref[g] for g in range(gs)]
    ai = [ai_ref[g] for g in range(gs)]

    def body(c, carry):
        rows = pl.ds(pl.multiple_of(c * batch, batch), batch)
        new = []
        for g in range(gs):
            hr, hi = carry[2 * g], carry[2 * g + 1]
            hr_s[g, rows, :] = hr
            hi_s[g, rows, :] = hi
            new.append(ar[g] * hr - ai[g] * hi + vr_s[g, rows, :])
            new.append(ar[g] * hi + ai[g] * hr + vi_s[g, rows, :])
        return tuple(new)

    init = tuple(ref[g] for g in range(gs) for ref in (str_s, sti_s))
    fin = lax.fori_loop(0, n_blocks, body, init, unroll=min(n_blocks, 4))
    for g in range(gs):
        str_s[g] = fin[2 * g]
        sti_s[g] = fin[2 * g + 1]
        hfr_ref[state_rows(g), :] = fin[2 * g]
        hfi_ref[state_rows(g), :] = fin[2 * g + 1]
        h_in = jnp.concatenate([from_scan_order(hr_s, g), from_scan_order(hi_s, g)], axis=1).astype(BF16)
        y_state = _dot(h_in, mout_ref[g])
        for k in range(2):
            y = _dot(ub[2 * g + k], mi_ref[2 * g + k]) + y_state[:, k * w:(k + 1) * w]
            yg_ref[2 * g + k] = y.reshape(yg_ref.shape[1:])


def _s5(ug, m, h0_re, h0_im, nb, cb, ug_lead=None):
    g_all, r, w = ug.shape
    nc = r // nb
    batch = cb if nb == 1 else nb
    n, gs = LANES, SLAB_GROUPS
    uspec = pl.BlockSpec((gs, nb, cb, w), lambda s, i: (s, 0, i, 0))
    per_g = lambda a, b_: pl.BlockSpec((gs, a, b_), lambda s, i: (s, 0, 0))
    per_p = lambda a, b_: pl.BlockSpec((gs // 2, a, b_), lambda s, i: (s, 0, 0))
    hspec = _const_spec(h0_re.shape)
    lead = ug_lead is not None
    yg, hf_re, hf_im = pl.pallas_call(
        functools.partial(_s5_kernel, nb=nb, cb=cb, lead=lead),
        grid=(g_all // gs, nc // cb),
        in_specs=[uspec] + ([per_g(nb, w)] if lead else [])
                 + [per_g(w, w), per_p(2 * w, 2 * n), per_p(2 * n, 2 * w), per_p(1, n), per_p(1, n), hspec, hspec],
        out_specs=[uspec, hspec, hspec],
        out_shape=[jax.ShapeDtypeStruct((g_all, nb, nc, w), F32),
                   jax.ShapeDtypeStruct(h0_re.shape, F32),
                   jax.ShapeDtypeStruct(h0_im.shape, F32)],
        scratch_shapes=[pltpu.VMEM((gs // 2, nb * cb, n), F32)] * 4 + [pltpu.VMEM((gs // 2, batch, n), F32)] * 2,
        compiler_params=pltpu.CompilerParams(dimension_semantics=("arbitrary", "arbitrary"),
                                             vmem_limit_bytes=VMEM_LIMIT),
        name=f"s5_w{w}",
    )(ug.reshape(g_all, nb, nc, w), *((ug_lead,) if lead else ()),
      m["m_intra"], m["m_in"], m["m_out"], m["a_re"], m["a_im"], h0_re, h0_im)
    return yg.reshape(g_all, r, w), hf_re, hf_im


def _mix_ffn_kernel(x_ref, o_ref, g_ref, yg_ref, u_ref, dsk_ref, gn_ref, wglu_ref, s5n_ref, wo_ref, gpost_ref,
                    gpre_ref, wg_ref, wu_ref, wd_ref, gpostf_ref, out_ref, y_scr, x1_scr, *, steps, n_tiles):
    i = pl.program_id(0)

    def mix_into(slot):
        nr = x_ref.shape[0] // steps
        rt = min(nr, 8)
        for s in range(D_S5 // LANES):
            for hf in range(steps // 8):
                for r0 in range(0, nr, rt):
                    a = [yg_ref[s * SLAB_GROUPS + g, r0:r0 + rt, hf * LANES:(hf + 1) * LANES]
                         for g in range(SLAB_GROUPS)]
                    per_step = _lane_block_transpose(a)
                    for t in range(8):
                        y_scr[s, pl.ds(r0 * steps + hf * 8 + t, rt, stride=steps), :] = per_step[t]
        y5 = jnp.concatenate([y_scr[s] for s in range(D_S5 // LANES)], axis=1) + dsk_ref[...] * u_ref[...]
        o = o_ref[...]
        gn = gn_ref[...]
        heads = []
        for h in range(GLA_HEADS):
            heads.append(_rms(o[:, h * GLA_DV:(h + 1) * GLA_DV], gn))
        og = jnp.concatenate(heads, axis=1) * jax.nn.silu(g_ref[...])
        y = jax.nn.gelu(y5)
        y = y * jax.nn.sigmoid(_dot(y.astype(BF16), wglu_ref[...]))
        y = _rms(y, s5n_ref[...])
        mix = _dot(og.astype(BF16), wo_ref[:D_GLA, :]) + _dot(y.astype(BF16), wo_ref[D_GLA:, :])
        x1_scr[slot] = x_ref[...] + _rms(mix, gpost_ref[...])

    def ffn_from(slot):
        x = x1_scr[slot]
        h = _rms(x, gpre_ref[...]).astype(BF16)
        acc = jnp.zeros(x.shape, F32)
        for c in range(D_FF // FF_CHUNK):
            cols = slice(c * FF_CHUNK, (c + 1) * FF_CHUNK)
            act = jax.nn.silu(_dot(h, wg_ref[:, cols])) * _dot(h, wu_ref[:, cols])
            acc = acc + _dot(act.astype(BF16), wd_ref[cols, :])
        out_ref[...] = x + _rms(acc, gpostf_ref[...])

    @pl.when(i == 0)
    def _():
        mix_into(0)

    @pl.when((i > 0) & (i < n_tiles))
    def _():
        ffn_from((i - 1) & 1)
        mix_into(i & 1)

    @pl.when(i == n_tiles)
    def _():
        ffn_from((i - 1) & 1)


def _mix_ffn(x, o, g, yg, u, dsk, gn, wglu, s5n, wo, gpost, gpre, wg, wu, wd, gpostf, tm, steps):
    t = x.shape[0]
    n = t // tm
    cur = lambda i: jnp.minimum(i, n - 1)
    row = lambda w: pl.BlockSpec((tm, w), lambda i: (cur(i), 0))
    fixed = lambda shape: pl.BlockSpec(shape, lambda i: (0,) * len(shape), pipeline_mode=pl.Buffered(1))
    return pl.pallas_call(
        functools.partial(_mix_ffn_kernel, steps=steps, n_tiles=n),
        grid=(n + 1,),
        in_specs=[row(D_MODEL), row(D_GLA), row(D_GLA),
                  pl.BlockSpec((S5_GROUPS, tm // steps, steps * S5_GROUP), lambda i: (0, cur(i), 0)),
                  row(D_S5), fixed((1, D_S5)),
                  fixed((1, GLA_DV)), fixed((D_S5, D_S5)), fixed((1, D_S5)),
                  fixed((D_GLA + D_S5, D_MODEL)), fixed((1, D_MODEL)),
                  fixed((1, D_MODEL)), fixed((D_MODEL, D_FF)), fixed((D_MODEL, D_FF)),
                  fixed((D_FF, D_MODEL)), fixed((1, D_MODEL))],
        out_specs=pl.BlockSpec((tm, D_MODEL), lambda i: (jnp.maximum(i - 1, 0), 0)),
        out_shape=jax.ShapeDtypeStruct((t, D_MODEL), F32),
        scratch_shapes=[pltpu.VMEM((D_S5 // LANES, tm, LANES), F32), pltpu.VMEM((2, tm, D_MODEL), F32)],
        compiler_params=pltpu.CompilerParams(dimension_semantics=("arbitrary",),
                                             vmem_limit_bytes=VMEM_LIMIT),
        name="mix_ffn",
    )(x, o, g, yg, u, dsk, gn, wglu, s5n, wo, gpost, gpre, wg, wu, wd, gpostf)


def kernel(x_prompt, x_sample, state_gla, state_s5_re, state_s5_im, meta_tokens, g_pre_mix, w_in, w_gk2, b_gk, gla_norm, s5_a_re, s5_a_im, s5_b_re, s5_b_im, s5_c_re, s5_c_im, s5_d, s5_log_dt, w_s5_glu, s5_norm, w_o, g_post_mix, g_pre_ffn, w_gate, w_up, w_down, g_post_ffn):
    assert g_pre_mix.shape[0] == 1, "single-layer step"
    bp, seq_p, _ = x_prompt.shape
    bs, seq_s, _ = x_sample.shape
    row = lambda t: t[0].reshape(1, -1)

    w = w_in[0]
    c3, c4 = 2 * GLA_KDIM + 2 * D_GLA, 2 * GLA_KDIM + 2 * D_GLA + GATE_RANK
    w_a = w[:, :c3].astype(BF16)
    w_b = jnp.concatenate([w[:, c4:].astype(BF16), w[:, c3:c4].astype(BF16),
                           jnp.zeros((D_MODEL, LANES - GATE_RANK), BF16)], axis=1)
    wgk_p = jnp.concatenate([w_gk2[0], jnp.zeros((LANES - GATE_RANK, GLA_KDIM), F32)], axis=0).astype(BF16)
    wo_bf = w_o[0].astype(BF16)
    s5m = _s5_prep(s5_a_re[0], s5_a_im[0], s5_b_re[0], s5_b_im[0], s5_c_re[0], s5_c_im[0], s5_log_dt[0])
    proj_w = (row(g_pre_mix), w_a, w_b, wgk_p, row(b_gk))

    def finish(x, o, g, yg, u, tm, steps):
        return _mix_ffn(x, o, g, yg, u, row(s5_d), row(gla_norm), w_s5_glu[0].astype(BF16), row(s5_norm),
                        wo_bf, row(g_post_mix), row(g_pre_ffn), w_gate[0].astype(BF16),
                        w_up[0].astype(BF16), w_down[0].astype(BF16), row(g_post_ffn), tm, steps)

    xm = jnp.broadcast_to(meta_tokens[None], (bp, N_META, D_MODEL)).reshape(bp * N_META, D_MODEL)
    q, k, v, _, _, lg, ug_meta = _in_proj(xm, *proj_w, bp * N_META, S5_BLOCK)
    r3 = lambda t, b, l: t.reshape(b, l, t.shape[-1])
    _, s_meta = _gla(r3(q, bp, N_META), r3(k, bp, N_META), r3(v, bp, N_META), r3(lg, bp, N_META),
                     jnp.zeros((bp, GLA_KDIM, GLA_DV), F32), bp, bp, N_META)
    zh = jnp.zeros((bp * S5_GROUPS // 2, LANES), F32)

    xp = x_prompt.reshape(bp * seq_p, D_MODEL)
    g, u, o, ug, s_p = _in_proj(xp, *proj_w, TOKEN_TILE, S5_BLOCK, gla_s0=s_meta, seq_len=seq_p)
    yg, hp_re, hp_im = _s5(ug, s5m[S5_BLOCK], zh, zh, bp, S5_ROW_BLOCKS, ug_lead=ug_meta)
    y_prompt = finish(xp, o, g, yg, u, TOKEN_TILE, S5_BLOCK)

    xs = x_sample.reshape(bs * seq_s, D_MODEL)
    to_g = lambda t: t[0].reshape(bs * S5_GROUPS // 2, LANES)
    q, k, v, g, u, lg, ug = _in_proj(xs, *proj_w, TOKEN_TILE, seq_s)
    o, s_s = _gla(r3(q, bs, seq_s), r3(k, bs, seq_s), r3(v, bs, seq_s), r3(lg, bs, seq_s),
                  state_gla[0].reshape(bs, GLA_KDIM, GLA_DV), GLA_SAMPLE_SEQS, GLA_SAMPLE_GROUP, seq_s)
    yg, hs_re, hs_im = _s5(ug, s5m[seq_s], to_g(state_s5_re), to_g(state_s5_im), 1, bs)
    y_sample = finish(xs, o.reshape(bs * seq_s, D_GLA), g, yg, u, TOKEN_TILE, seq_s)

    gla_out = lambda s, b: s.reshape(1, b, GLA_HEADS, GLA_DK, GLA_DV)
    s5_out = lambda h: h.reshape(1, -1, S5_GROUPS, S5_STATE)
    return (y_prompt.reshape(bp, seq_p, D_MODEL), y_sample.reshape(bs, seq_s, D_MODEL),
            gla_out(s_p, bp), s5_out(hp_re), s5_out(hp_im),
            gla_out(s_s, bs), s5_out(hs_re), s5_out(hs_im))
```

```python
import functools

import jax
import jax.numpy as jnp
from jax import lax
from jax.experimental import pallas as pl
from jax.experimental.pallas import tpu as pltpu

F32 = jnp.float32
BF16 = jnp.bfloat16

D_MODEL = 1024
D_GLA = 512
GLA_HEADS = 4
GLA_DV = 128
GLA_DK = 64
GLA_KDIM = 256
GATE_RANK = 16
GATE_NORM = 16.0
GLA_CHUNK = 64
D_S5 = 512
S5_GROUP = 16
S5_GROUPS = 32
S5_STATE = 64
N_META = 16
D_FF = 2816
EPS = 1e-6
LANES = 128
S5_BLOCK = 16
SLAB_GROUPS = LANES // S5_GROUP
FF_CHUNK = 256
TOKEN_TILE = 512
S5_ROW_BLOCKS = 128
GLA_SAMPLE_GROUP = 16
GLA_SAMPLE_SEQS = 32
VMEM_LIMIT = 48 * 1024 * 1024


def _rms(x, g):
    return x * lax.rsqrt(jnp.mean(x * x, axis=-1, keepdims=True) + EPS) * g


def _dot(a, b):
    return jnp.dot(a, b, preferred_element_type=F32)


def _dot_nt(a, b):
    return lax.dot_general(a, b, (((1,), (1,)), ((), ())), preferred_element_type=F32)


def _dot_tn(a, b):
    return lax.dot_general(a, b, (((0,), (0,)), ((), ())), preferred_element_type=F32)


def _const_spec(shape):
    zeros = (0,) * len(shape)
    return pl.BlockSpec(shape, lambda *_: zeros)


PREP_GROUPS = 8


def _cmul(a, b):
    return a[0] * b[0] - a[1] * b[1], a[0] * b[1] + a[1] * b[0]


def _unit_powers(c1, s1, expo, n_bits):
    acc = (jnp.ones_like(c1), jnp.zeros_like(c1))
    base = (c1, s1)
    squares = [base]
    for bit in range(n_bits):
        take = ((expo >> bit) & 1) == 1
        nxt = _cmul(acc, base)
        acc = (jnp.where(take, nxt[0], acc[0]), jnp.where(take, nxt[1], acc[1]))
        base = _cmul(base, base)
        squares.append(base)
    return acc, squares


def _s5_prep_kernel(arc_ref, aic_ref, air_ref, ldt_ref, br_ref, bi_ref, cr_ref, ci_ref,
                    mi16_ref, mi8_ref, in16_ref, in8_ref, out16_ref, out8_ref,
                    a16r_ref, a16i_ref, a8r_ref, a8i_ref):
    n, q = S5_STATE, S5_BLOCK
    w = q * S5_GROUP
    hp = lax.Precision.HIGHEST
    lane = lax.broadcasted_iota(jnp.int32, (n, w), 1)
    t_blk = lane >> 4
    eye = lax.broadcasted_iota(jnp.int32, (n, n), 0) == lax.broadcasted_iota(jnp.int32, (n, n), 1)
    to_col = lambda r: jnp.sum(jnp.where(eye, r, 0.0), axis=1, keepdims=True)
    zrows = lambda x: jnp.concatenate([x, jnp.zeros_like(x)], axis=0)
    lane_n = lax.broadcasted_iota(jnp.int32, (n, LANES), 1)
    rep = jnp.where((lax.broadcasted_iota(jnp.int32, (S5_GROUP, w), 1) & (S5_GROUP - 1))
                    == lax.broadcasted_iota(jnp.int32, (S5_GROUP, w), 0), 1.0, 0.0)
    mo_g, rev_g, hop_g = [], [], []
    for g in range(PREP_GROUPS):
        dt = jnp.exp(ldt_ref[g])
        ang_r = air_ref[g] * dt
        c1, s1 = to_col(jnp.cos(ang_r)), to_col(jnp.sin(ang_r))
        lam_re = jnp.minimum(arc_ref[g], -1e-4)
        lam_im = aic_ref[g]
        unit, squares = _unit_powers(c1, s1, t_blk, 4)
        pm = jnp.exp(t_blk.astype(F32) * (lam_re * dt))
        pk = (pm * unit[0], pm * unit[1])
        mag = jnp.exp(lam_re * dt)
        ab = (mag * c1, mag * s1)
        p1 = _cmul(pk, ab)
        ct = tuple(lax.dot_general(r[g], rep, (((0,), (0,)), ((), ())), precision=hp,
                                   preferred_element_type=F32) for r in (cr_ref, ci_ref))
        bt = tuple(jnp.dot(r[g], rep, precision=hp, preferred_element_type=F32) for r in (br_ref, bi_ref))
        g0 = _cmul(ct, pk)
        mo = _cmul(ct, p1)
        mo_g.append((mo[0], -mo[1]))
        den = lam_re * lam_re + lam_im * lam_im
        nr, ni = ab[0] - 1.0, ab[1]
        f = ((nr * lam_re + ni * lam_im) / den, (ni * lam_re - nr * lam_im) / den)
        e = _cmul(pk, _cmul(f, bt))
        et = [zrows(x).T for x in e]
        rev_g.append([jnp.concatenate([x[(q - 1 - s) * S5_GROUP:(q - s) * S5_GROUP] for s in range(q)], axis=0)
                      for x in et])
        hops = []
        for steps in (q, q // 2):
            m = jnp.exp(float(steps) * (lam_re * dt))
            u = squares[steps.bit_length() - 1]
            hops += [m * u[0], m * u[1]]
        cols = jnp.zeros((n, LANES), F32)
        for idx, hcol in enumerate(hops):
            cols = jnp.where(lane_n == idx, hcol, cols)
        hop_g.append(zrows(cols).T)
        t0 = (jnp.dot(et[0][:S5_GROUP, :n], g0[0], precision=hp, preferred_element_type=F32)
              - jnp.dot(et[1][:S5_GROUP, :n], g0[1], precision=hp, preferred_element_type=F32))
        lane_t = lax.broadcasted_iota(jnp.int32, t0.shape, 1) >> 4
        for s in range(q):
            blk = t0 if s == 0 else jnp.where(lane_t >= s, pltpu.roll(t0, S5_GROUP * s, 1), 0.0)
            mi16_ref[g, s * S5_GROUP:(s + 1) * S5_GROUP, :] = blk.astype(BF16)
            if s < q // 2:
                mi8_ref[g, s * S5_GROUP:(s + 1) * S5_GROUP, :] = blk[:, :w // 2].astype(BF16)

    half = lambda x: pltpu.roll(x, n, 1)
    for p in range(PREP_GROUPS // 2):
        g0, g1 = 2 * p, 2 * p + 1
        for c in range(2):
            part = slice(c * LANES, (c + 1) * LANES)
            lo, hi = rev_g[g0][c], half(rev_g[g1][c])
            in16_ref[p, :, part] = jnp.concatenate([lo, hi], axis=0).astype(BF16)
            in8_ref[p, :, part] = jnp.concatenate([lo[w // 2:], hi[w // 2:]], axis=0).astype(BF16)
            m0, m1 = mo_g[g0][c], mo_g[g1][c]
            z, zh = jnp.zeros_like(m0), jnp.zeros((n, w // 2), F32)
            out16_ref[p, part, :] = jnp.concatenate([jnp.concatenate([m0, z], axis=1),
                                                     jnp.concatenate([z, m1], axis=1)], axis=0).astype(BF16)
            out8_ref[p, part, :] = jnp.concatenate([jnp.concatenate([m0[:, :w // 2], zh], axis=1),
                                                    jnp.concatenate([zh, m1[:, :w // 2]], axis=1)],
                                                   axis=0).astype(BF16)
        hops = hop_g[g0] + half(hop_g[g1])
        a16r_ref[p] = hops[0:1]
        a16i_ref[p] = hops[1:2]
        a8r_ref[p] = hops[2:3]
        a8i_ref[p] = hops[3:4]


def _s5_prep(a_re, a_im, b_re, b_im, c_re, c_im, log_dt):
    g, n, j, q = S5_GROUPS, S5_STATE, S5_GROUP, S5_BLOCK
    w, h, pg = q * j, q * j // 2, PREP_GROUPS
    col = lambda t: t.reshape(g, n, 1)
    rowv = lambda t: t.reshape(g, 1, n)
    spec = lambda a, b_: pl.BlockSpec((pg, a, b_), lambda i: (i, 0, 0))
    pspec = lambda a, b_: pl.BlockSpec((pg // 2, a, b_), lambda i: (i, 0, 0))
    pair_shapes = [(2 * w, 2 * LANES), (2 * h, 2 * LANES), (2 * LANES, 2 * w), (2 * LANES, 2 * h)]
    outs = pl.pallas_call(
        _s5_prep_kernel,
        grid=(g // pg,),
        in_specs=[spec(n, 1), spec(n, 1), spec(1, n), spec(1, 1),
                  spec(n, j), spec(n, j), spec(j, n), spec(j, n)],
        out_specs=[spec(w, w), spec(h, h)] + [pspec(*s) for s in pair_shapes] + [pspec(1, LANES)] * 4,
        out_shape=[jax.ShapeDtypeStruct((g, w, w), BF16), jax.ShapeDtypeStruct((g, h, h), BF16)]
                  + [jax.ShapeDtypeStruct((g // 2,) + s, BF16) for s in pair_shapes]
                  + [jax.ShapeDtypeStruct((g // 2, 1, LANES), F32)] * 4,
        compiler_params=pltpu.CompilerParams(dimension_semantics=("parallel",)),
        name="s5_prep",
    )(col(a_re), col(a_im), rowv(a_im), log_dt.reshape(g, 1, 1),
      b_re, b_im, c_re, c_im)
    mi16, mi8, in16, in8, out16, out8, a16r, a16i, a8r, a8i = outs
    return {q: dict(m_intra=mi16, m_in=in16, m_out=out16, a_re=a16r, a_im=a16i),
            q // 2: dict(m_intra=mi8, m_in=in8, m_out=out8, a_re=a8r, a_im=a8i)}


def _lane_block_transpose(a):
    a = list(a)
    blk = lax.broadcasted_iota(jnp.int32, a[0].shape, 1) >> 4
    for d in (4, 2, 1):
        upper = (blk & d) != 0
        for r in range(8):
            if r & d:
                continue
            lo, hi = a[r], a[r + d]
            a[r] = jnp.where(upper, pltpu.roll(hi, 16 * d, 1), lo)
            a[r + d] = jnp.where(upper, hi, pltpu.roll(lo, LANES - 16 * d, 1))
    return a


def _in_proj_kernel(*refs, steps, n_tiles, tiles_per_seq):
    fused = tiles_per_seq is not None
    x_ref, gpre_ref, wa_ref, wb_ref, wgk_ref, bgk_ref = refs[:6]
    if fused:
        (s0_ref, g_ref, u_ref, o_ref, ug_ref, sfin_ref,
         u_scr, gk_scr, q_scr, k_scr, v_scr, s_scr) = refs[6:]
    else:
        q_ref, k_ref, v_ref, g_ref, u_ref, lg_ref, ug_ref, u_scr, gk_scr = refs[6:]
    i = pl.program_id(0)
    tm = x_ref.shape[0]

    def project(slot):
        h = _rms(x_ref[...], gpre_ref[...]).astype(BF16)
        proj = _dot(h, wa_ref[...])
        tail_proj = _dot(h, wb_ref[...])
        c_k, c_v, c_g = GLA_KDIM, 2 * GLA_KDIM, 2 * GLA_KDIM + D_GLA
        q, k, v, gate = proj[:, :c_k], proj[:, c_k:c_v], proj[:, c_v:c_g], proj[:, c_g:]
        if fused:
            q_scr[slot] = q
            k_scr[slot] = k
            v_scr[slot] = v.astype(BF16)
        else:
            q_ref[...] = q
            k_ref[...] = k
            v_ref[...] = v
        g_ref[...] = gate
        u = tail_proj[:, :D_S5]
        u_ref[...] = u
        for s in range(D_S5 // LANES):
            u_scr[slot, s] = u[:, s * LANES:(s + 1) * LANES]
        gk_scr[slot] = tail_proj[:, D_S5:]

    def tail(slot):
        z = _dot(gk_scr[slot].astype(BF16), wgk_ref[...]) + bgk_ref[...]
        lg = jax.nn.log_sigmoid(z) * (1.0 / GATE_NORM)
        if fused:
            tile = i - 1
            seq = tile // tiles_per_seq
            state = jnp.where(tile % tiles_per_seq == 0, s0_ref[seq], s_scr[...])
            masks = _gla_masks(1, GLA_CHUNK)
            for c in range(tm // GLA_CHUNK):
                rows = slice(c * GLA_CHUNK, (c + 1) * GLA_CHUNK)
                o, (state,) = _gla_group(q_scr[slot, rows, :], k_scr[slot, rows, :], v_scr[slot, rows, :],
                                         lg[rows], [state], masks, GLA_CHUNK)
                o_ref[rows, :] = o
            s_scr[...] = state
            sfin_ref[seq] = state
        else:
            lg_ref[...] = lg
        nr = tm // steps
        rt = min(nr, 16)
        for s in range(D_S5 // LANES):
            for hf in range(steps // 8):
                for r0 in range(0, nr, rt):
                    a = [u_scr[slot, s, pl.ds(r0 * steps + hf * 8 + t, rt, stride=steps), :] for t in range(8)]
                    per_group = _lane_block_transpose(a)
                    for g in range(SLAB_GROUPS):
                        ug_ref[s * SLAB_GROUPS + g, r0:r0 + rt, hf * LANES:(hf + 1) * LANES] = (
                            per_group[g].astype(BF16))

    @pl.when(i == 0)
    def _():
        if fused:
            s_scr[...] = jnp.zeros(s_scr.shape, F32)
        project(0)

    @pl.when((i > 0) & (i < n_tiles))
    def _():
        tail((i - 1) & 1)
        project(i & 1)

    @pl.when(i == n_tiles)
    def _():
        tail((i - 1) & 1)


def _in_proj(x, gpre, w_a, w_b, wgk_p, bgk, tm, steps, gla_s0=None, seq_len=None):
    t = x.shape[0]
    n = t // tm
    cur = lambda i: jnp.minimum(i, n - 1)
    prev = lambda i: jnp.maximum(i - 1, 0)
    row = lambda w, at: pl.BlockSpec((tm, w), lambda i: (at(i), 0))
    wg = steps * S5_GROUP
    ug_spec = pl.BlockSpec((S5_GROUPS, tm // steps, wg), lambda i: (0, prev(i), 0))
    ug_shape = jax.ShapeDtypeStruct((S5_GROUPS, t // steps, wg), BF16)
    f32 = lambda w: jax.ShapeDtypeStruct((t, w), F32)
    in_specs = [row(D_MODEL, cur), _const_spec((1, D_MODEL)), _const_spec(w_a.shape), _const_spec(w_b.shape),
                _const_spec((LANES, GLA_KDIM)), _const_spec((1, GLA_KDIM))]
    scratch = [pltpu.VMEM((2, D_S5 // LANES, tm, LANES), F32), pltpu.VMEM((2, tm, LANES), F32)]
    args = (x, gpre, w_a, w_b, wgk_p, bgk)
    if gla_s0 is None:
        tiles_per_seq = None
        widths = (GLA_KDIM, GLA_KDIM, D_GLA, D_GLA, D_S5)
        out_specs = [row(w, cur) for w in widths] + [row(GLA_KDIM, prev), ug_spec]
        out_shape = [f32(w) for w in widths] + [f32(GLA_KDIM), ug_shape]
    else:
        tiles_per_seq = seq_len // tm
        assert seq_len % tm == 0 and tm % GLA_CHUNK == 0
        in_specs.append(_const_spec(gla_s0.shape))
        args += (gla_s0,)
        out_specs = [row(D_GLA, cur), row(D_S5, cur), row(D_GLA, prev), ug_spec, _const_spec(gla_s0.shape)]
        out_shape = [f32(D_GLA), f32(D_S5), f32(D_GLA), ug_shape, jax.ShapeDtypeStruct(gla_s0.shape, F32)]
        scratch += [pltpu.VMEM((2, tm, GLA_KDIM), F32), pltpu.VMEM((2, tm, GLA_KDIM), F32),
                    pltpu.VMEM((2, tm, D_GLA), BF16), pltpu.VMEM(gla_s0.shape[1:], F32)]
    return pl.pallas_call(
        functools.partial(_in_proj_kernel, steps=steps, n_tiles=n, tiles_per_seq=tiles_per_seq),
        grid=(n + 1,),
        in_specs=in_specs,
        out_specs=out_specs,
        out_shape=out_shape,
        scratch_shapes=scratch,
        compiler_params=pltpu.CompilerParams(dimension_semantics=("arbitrary",),
                                             vmem_limit_bytes=VMEM_LIMIT),
        name="in_proj" if gla_s0 is None else "in_proj_gla",
    )(*args)


def _gla_masks(sg, chunk):
    r = sg * chunk
    assert r >= LANES or sg == 1
    shift = chunk.bit_length() - 1
    hk = GLA_HEADS * r
    ri = lax.broadcasted_iota(jnp.int32, (r, r), 0)
    ci = lax.broadcasted_iota(jnp.int32, (r, r), 1)
    ri4 = lax.broadcasted_iota(jnp.int32, (r, hk), 0)
    ci4 = lax.broadcasted_iota(jnp.int32, (r, hk), 1) & (r - 1)
    stack_head = lax.broadcasted_iota(jnp.int32, (hk, 1), 0) >> (r.bit_length() - 1)
    k_head = lax.broadcasted_iota(jnp.int32, (1, GLA_KDIM), 1) >> 6
    v_head = lax.broadcasted_iota(jnp.int32, (1, D_GLA), 1) >> 7
    s_head = lax.broadcasted_iota(jnp.int32, (GLA_KDIM, 1), 0) >> 6
    return dict(
        tri_bf=jnp.where(((ri >> shift) == (ci >> shift)) & (ri >= ci), 1.0, 0.0).astype(BF16),
        causal4=((ri4 >> shift) == (ci4 >> shift)) & (ri4 >= ci4),
        k_diag=stack_head == k_head, v_diag=stack_head == v_head, s_diag=s_head == v_head)


def _gla_group(q, k, v_bf, lg, states, m, chunk):
    sg = len(states)
    r = sg * chunk
    tile4 = lambda t: jnp.concatenate([t] * GLA_HEADS, axis=0)
    lg_hi = lg.astype(BF16)
    lg_lo = (lg - lg_hi.astype(F32)).astype(BF16)
    b2 = _dot(m["tri_bf"], jnp.concatenate([lg_hi, lg_lo], axis=1))
    b = b2[:, :GLA_KDIM] + b2[:, GLA_KDIM:]
    lasts = [b[(i + 1) * chunk - 1:(i + 1) * chunk, :] for i in range(sg)]
    bl = jnp.concatenate([jnp.broadcast_to(t, (chunk, GLA_KDIM)) for t in lasts], axis=0)
    qd = (q * (GLA_DK ** -0.5) * jnp.exp(b)).astype(BF16)
    ki = (k * jnp.exp(-b)).astype(BF16)
    ke = k * jnp.exp(bl - b)

    ki_bd = jnp.where(m["k_diag"], tile4(ki), 0.0)
    att = jnp.where(m["causal4"], _dot_nt(qd, ki_bd), 0.0).astype(BF16)
    v_bd = jnp.where(m["v_diag"], tile4(v_bf), 0.0)
    o_intra = _dot(att, v_bd)

    if r < LANES:
        aug_t = jnp.concatenate([ke, jnp.broadcast_to(lasts[0], (LANES - r, GLA_KDIM))], axis=0).T
        ke_t, bl_t = aug_t, aug_t[:, r:]
    else:
        ke_t = ke.T
        bl_t = jnp.concatenate(lasts + [jnp.zeros((r - sg, GLA_KDIM), F32)], axis=0).T
    ke_t = ke_t.astype(BF16)

    outs, new_states = [], []
    for i, s_old in enumerate(states):
        rows = slice(i * chunk, (i + 1) * chunk)
        s_bd = jnp.where(m["s_diag"], jnp.concatenate([s_old.astype(BF16)] * GLA_HEADS, axis=1), 0.0)
        outs.append(o_intra[rows] + _dot(qd[rows], s_bd))
        upd = [_dot(ke_t[h * GLA_DK:(h + 1) * GLA_DK, rows], v_bf[rows, h * GLA_DV:(h + 1) * GLA_DV])
               for h in range(GLA_HEADS)]
        new_states.append(jnp.exp(bl_t[:, i:i + 1]) * s_old + jnp.concatenate(upd, axis=0))
    return (outs[0] if sg == 1 else jnp.concatenate(outs, axis=0)), new_states


def _gla_kernel(q_ref, k_ref, v_ref, lg_ref, s0_ref, o_ref, s_ref, *, bb, sg, chunk):
    r = sg * chunk

    @pl.when(pl.program_id(1) == 0)
    def _():
        s_ref[...] = s0_ref[...]

    masks = _gla_masks(sg, chunk)
    for gi in range(bb // sg):
        seqs = slice(gi * sg, (gi + 1) * sg)
        o, new = _gla_group(q_ref[seqs].reshape(r, GLA_KDIM), k_ref[seqs].reshape(r, GLA_KDIM),
                            v_ref[seqs].reshape(r, D_GLA).astype(BF16), lg_ref[seqs].reshape(r, GLA_KDIM),
                            [s_ref[gi * sg + i] for i in range(sg)], masks, chunk)
        o_ref[seqs] = o.reshape(sg, chunk, D_GLA)
        for i in range(sg):
            s_ref[gi * sg + i] = new[i]


def _gla(q, k, v, lg, s0, bb, sg, chunk):
    b, l, _ = q.shape
    blk = lambda w: pl.BlockSpec((bb, chunk, w), lambda i, c: (i, c, 0))
    sspec = pl.BlockSpec((bb, GLA_KDIM, GLA_DV), lambda i, c: (i, 0, 0))
    return pl.pallas_call(
        functools.partial(_gla_kernel, bb=bb, sg=sg, chunk=chunk),
        grid=(b // bb, l // chunk),
        in_specs=[blk(GLA_KDIM), blk(GLA_KDIM), blk(D_GLA), blk(GLA_KDIM), sspec],
        out_specs=[blk(D_GLA), sspec],
        out_shape=[jax.ShapeDtypeStruct((b, l, D_GLA), F32),
                   jax.ShapeDtypeStruct((b, GLA_KDIM, GLA_DV), F32)],
        compiler_params=pltpu.CompilerParams(dimension_semantics=("parallel", "arbitrary"),
                                             vmem_limit_bytes=VMEM_LIMIT),
        name=f"gla_c{chunk}",
    )(q, k, v, lg, s0)


def _s5_kernel(*refs, nb, cb, lead):
    ug_ref, refs = refs[0], refs[1:]
    if lead:
        lead_ref, refs = refs[0], refs[1:]
    (mi_ref, min_ref, mout_ref, ar_ref, ai_ref, h0r_ref, h0i_ref, yg_ref, hfr_ref, hfi_ref,
     vr_s, vi_s, hr_s, hi_s, str_s, sti_s) = refs
    gs = SLAB_GROUPS // 2
    n_pairs = S5_GROUPS // 2
    w = ug_ref.shape[-1]
    rows = nb * cb
    batch = cb if nb == 1 else nb
    n_blocks = rows // batch
    interleave = nb > 1 and cb > 1
    first_pair = pl.program_id(0) * gs
    seq0 = pl.program_id(1) * batch if nb == 1 else 0

    def state_rows(p):
        return pl.ds(seq0 * n_pairs + first_pair + p, batch, stride=n_pairs)

    def load_state():
        for p in range(gs):
            hr, hi = h0r_ref[state_rows(p), :], h0i_ref[state_rows(p), :]
            if lead:
                v = _dot(jnp.concatenate([lead_ref[2 * p], lead_ref[2 * p + 1]], axis=1), min_ref[p])
                hr, hi = (ar_ref[p] * hr - ai_ref[p] * hi + v[:, :LANES],
                          ar_ref[p] * hi + ai_ref[p] * hr + v[:, LANES:])
            str_s[p] = hr
            sti_s[p] = hi

    if nb == 1:
        load_state()
    else:
        pl.when(pl.program_id(1) == 0)(load_state)

    def to_scan_order(ref, g, val):
        if not interleave:
            ref[g] = val
        else:
            for b in range(nb):
                ref[g, pl.ds(b, cb, stride=nb), :] = val[b * cb:(b + 1) * cb]

    def from_scan_order(ref, g):
        if not interleave:
            return ref[g]
        return jnp.concatenate([ref[g, pl.ds(b, cb, stride=nb), :] for b in range(nb)], axis=0)

    ub = [ug_ref[g].reshape(rows, w) for g in range(2 * gs)]
    for g in range(gs):
        v = _dot(jnp.concatenate([ub[2 * g], ub[2 * g + 1]], axis=1), min_ref[g])
        to_scan_order(vr_s, g, v[:, :LANES])
        to_scan_order(vi_s, g, v[:, LANES:])
    ar = [ar_ref[g] for g in range(gs)]
    ai = [ai_ref[g] for g in range(gs)]

    def body(c, carry):
        rows = pl.ds(pl.multiple_of(c * batch, batch), batch)
        new = []
        for g in range(gs):
            hr, hi = carry[2 * g], carry[2 * g + 1]
            hr_s[g, rows, :] = hr
            hi_s[g, rows, :] = hi
            new.append(ar[g] * hr - ai[g] * hi + vr_s[g, rows, :])
            new.append(ar[g] * hi + ai[g] * hr + vi_s[g, rows, :])
        return tuple(new)

    init = tuple(ref[g] for g in range(gs) for ref in (str_s, sti_s))
    fin = lax.fori_loop(0, n_blocks, body, init, unroll=min(n_blocks, 4))
    for g in range(gs):
        str_s[g] = fin[2 * g]
        sti_s[g] = fin[2 * g + 1]
        hfr_ref[state_rows(g), :] = fin[2 * g]
        hfi_ref[state_rows(g), :] = fin[2 * g + 1]
        h_in = jnp.concatenate([from_scan_order(hr_s, g), from_scan_order(hi_s, g)], axis=1).astype(BF16)
        y_state = _dot(h_in, mout_ref[g])
        for k in range(2):
            y = _dot(ub[2 * g + k], mi_ref[2 * g + k]) + y_state[:, k * w:(k + 1) * w]
            yg_ref[2 * g + k] = y.reshape(yg_ref.shape[1:])


def _s5(ug, m, h0_re, h0_im, nb, cb, ug_lead=None):
    g_all, r, w = ug.shape
    nc = r // nb
    batch = cb if nb == 1 else nb
    n, gs = LANES, SLAB_GROUPS
    uspec = pl.BlockSpec((gs, nb, cb, w), lambda s, i: (s, 0, i, 0))
    per_g = lambda a, b_: pl.BlockSpec((gs, a, b_), lambda s, i: (s, 0, 0))
    per_p = lambda a, b_: pl.BlockSpec((gs // 2, a, b_), lambda s, i: (s, 0, 0))
    hspec = _const_spec(h0_re.shape)
    lead = ug_lead is not None
    yg, hf_re, hf_im = pl.pallas_call(
        functools.partial(_s5_kernel, nb=nb, cb=cb, lead=lead),
        grid=(g_all // gs, nc // cb),
        in_specs=[uspec] + ([per_g(nb, w)] if lead else [])
                 + [per_g(w, w), per_p(2 * w, 2 * n), per_p(2 * n, 2 * w), per_p(1, n), per_p(1, n), hspec, hspec],
        out_specs=[uspec, hspec, hspec],
        out_shape=[jax.ShapeDtypeStruct((g_all, nb, nc, w), F32),
                   jax.ShapeDtypeStruct(h0_re.shape, F32),
                   jax.ShapeDtypeStruct(h0_im.shape, F32)],
        scratch_shapes=[pltpu.VMEM((gs // 2, nb * cb, n), F32)] * 4 + [pltpu.VMEM((gs // 2, batch, n), F32)] * 2,
        compiler_params=pltpu.CompilerParams(dimension_semantics=("arbitrary", "arbitrary"),
                                             vmem_limit_bytes=VMEM_LIMIT),
        name=f"s5_w{w}",
    )(ug.reshape(g_all, nb, nc, w), *((ug_lead,) if lead else ()),
      m["m_intra"], m["m_in"], m["m_out"], m["a_re"], m["a_im"], h0_re, h0_im)
    return yg.reshape(g_all, r, w), hf_re, hf_im


def _mix_ffn_kernel(x_ref, o_ref, g_ref, yg_ref, u_ref, dsk_ref, gn_ref, wglu_ref, s5n_ref, wo_ref, gpost_ref,
                    gpre_ref, wg_ref, wu_ref, wd_ref, gpostf_ref, out_ref, y_scr, x1_scr, *, steps, n_tiles):
    i = pl.program_id(0)

    def mix_into(slot):
        nr = x_ref.shape[0] // steps
        rt = min(nr, 8)
        for s in range(D_S5 // LANES):
            for hf in range(steps // 8):
                for r0 in range(0, nr, rt):
                    a = [yg_ref[s * SLAB_GROUPS + g, r0:r0 + rt, hf * LANES:(hf + 1) * LANES]
                         for g in range(SLAB_GROUPS)]
                    per_step = _lane_block_transpose(a)
                    for t in range(8):
                        y_scr[s, pl.ds(r0 * steps + hf * 8 + t, rt, stride=steps), :] = per_step[t]
        y5 = jnp.concatenate([y_scr[s] for s in range(D_S5 // LANES)], axis=1) + dsk_ref[...] * u_ref[...]
        o = o_ref[...]
        gn = gn_ref[...]
        heads = []
        for h in range(GLA_HEADS):
            heads.append(_rms(o[:, h * GLA_DV:(h + 1) * GLA_DV], gn))
        og = jnp.concatenate(heads, axis=1) * jax.nn.silu(g_ref[...])
        y = jax.nn.gelu(y5)
        y = y * jax.nn.sigmoid(_dot(y.astype(BF16), wglu_ref[...]))
        y = _rms(y, s5n_ref[...])
        mix = _dot(og.astype(BF16), wo_ref[:D_GLA, :]) + _dot(y.astype(BF16), wo_ref[D_GLA:, :])
        x1_scr[slot] = x_ref[...] + _rms(mix, gpost_ref[...])

    def ffn_from(slot):
        x = x1_scr[slot]
        h = _rms(x, gpre_ref[...]).astype(BF16)
        acc = jnp.zeros(x.shape, F32)
        for c in range(D_FF // FF_CHUNK):
            cols = slice(c * FF_CHUNK, (c + 1) * FF_CHUNK)
            act = jax.nn.silu(_dot(h, wg_ref[:, cols])) * _dot(h, wu_ref[:, cols])
            acc = acc + _dot(act.astype(BF16), wd_ref[cols, :])
        out_ref[...] = x + _rms(acc, gpostf_ref[...])

    @pl.when(i == 0)
    def _():
        mix_into(0)

    @pl.when((i > 0) & (i < n_tiles))
    def _():
        ffn_from((i - 1) & 1)
        mix_into(i & 1)

    @pl.when(i == n_tiles)
    def _():
        ffn_from((i - 1) & 1)


def _mix_ffn(x, o, g, yg, u, dsk, gn, wglu, s5n, wo, gpost, gpre, wg, wu, wd, gpostf, tm, steps):
    t = x.shape[0]
    n = t // tm
    cur = lambda i: jnp.minimum(i, n - 1)
    row = lambda w: pl.BlockSpec((tm, w), lambda i: (cur(i), 0))
    fixed = lambda shape: pl.BlockSpec(shape, lambda i: (0,) * len(shape), pipeline_mode=pl.Buffered(1))
    return pl.pallas_call(
        functools.partial(_mix_ffn_kernel, steps=steps, n_tiles=n),
        grid=(n + 1,),
        in_specs=[row(D_MODEL), row(D_GLA), row(D_GLA),
                  pl.BlockSpec((S5_GROUPS, tm // steps, steps * S5_GROUP), lambda i: (0, cur(i), 0)),
                  row(D_S5), fixed((1, D_S5)),
                  fixed((1, GLA_DV)), fixed((D_S5, D_S5)), fixed((1, D_S5)),
                  fixed((D_GLA + D_S5, D_MODEL)), fixed((1, D_MODEL)),
                  fixed((1, D_MODEL)), fixed((D_MODEL, D_FF)), fixed((D_MODEL, D_FF)),
                  fixed((D_FF, D_MODEL)), fixed((1, D_MODEL))],
        out_specs=pl.BlockSpec((tm, D_MODEL), lambda i: (jnp.maximum(i - 1, 0), 0)),
        out_shape=jax.ShapeDtypeStruct((t, D_MODEL), F32),
        scratch_shapes=[pltpu.VMEM((D_S5 // LANES, tm, LANES), F32), pltpu.VMEM((2, tm, D_MODEL), F32)],
        compiler_params=pltpu.CompilerParams(dimension_semantics=("arbitrary",),
                                             vmem_limit_bytes=VMEM_LIMIT),
        name="mix_ffn",
    )(x, o, g, yg, u, dsk, gn, wglu, s5n, wo, gpost, gpre, wg, wu, wd, gpostf)


def kernel(x_prompt, x_sample, state_gla, state_s5_re, state_s5_im, meta_tokens, g_pre_mix, w_in, w_gk2, b_gk, gla_norm, s5_a_re, s5_a_im, s5_b_re, s5_b_im, s5_c_re, s5_c_im, s5_d, s5_log_dt, w_s5_glu, s5_norm, w_o, g_post_mix, g_pre_ffn, w_gate, w_up, w_down, g_post_ffn):
    assert g_pre_mix.shape[0] == 1, "single-layer step"
    bp, seq_p, _ = x_prompt.shape
    bs, seq_s, _ = x_sample.shape
    row = lambda t: t[0].reshape(1, -1)

    w = w_in[0]
    c3, c4 = 2 * GLA_KDIM + 2 * D_GLA, 2 * GLA_KDIM + 2 * D_GLA + GATE_RANK
    w_a = w[:, :c3].astype(BF16)
    w_b = jnp.concatenate([w[:, c4:].astype(BF16), w[:, c3:c4].astype(BF16),
                           jnp.zeros((D_MODEL, LANES - GATE_RANK), BF16)], axis=1)
    wgk_p = jnp.concatenate([w_gk2[0], jnp.zeros((LANES - GATE_RANK, GLA_KDIM), F32)], axis=0).astype(BF16)
    wo_bf = w_o[0].astype(BF16)
    s5m = _s5_prep(s5_a_re[0], s5_a_im[0], s5_b_re[0], s5_b_im[0], s5_c_re[0], s5_c_im[0], s5_log_dt[0])
    proj_w = (row(g_pre_mix), w_a, w_b, wgk_p, row(b_gk))

    def finish(x, o, g, yg, u, tm, steps):
        return _mix_ffn(x, o, g, yg, u, row(s5_d), row(gla_norm), w_s5_glu[0].astype(BF16), row(s5_norm),
                        wo_bf, row(g_post_mix), row(g_pre_ffn), w_gate[0].astype(BF16),
                        w_up[0].astype(BF16), w_down[0].astype(BF16), row(g_post_ffn), tm, steps)

    xm = jnp.broadcast_to(meta_tokens[None], (bp, N_META, D_MODEL)).reshape(bp * N_META, D_MODEL)
    q, k, v, _, _, lg, ug_meta = _in_proj(xm, *proj_w, bp * N_META, S5_BLOCK)
    r3 = lambda t, b, l: t.reshape(b, l, t.shape[-1])
    _, s_meta = _gla(r3(q, bp, N_META), r3(k, bp, N_META), r3(v, bp, N_META), r3(lg, bp, N_META),
                     jnp.zeros((bp, GLA_KDIM, GLA_DV), F32), bp, bp, N_META)
    zh = jnp.zeros((bp * S5_GROUPS // 2, LANES), F32)

    xp = x_prompt.reshape(bp * seq_p, D_MODEL)
    g, u, o, ug, s_p = _in_proj(xp, *proj_w, TOKEN_TILE, S5_BLOCK, gla_s0=s_meta, seq_len=seq_p)
    yg, hp_re, hp_im = _s5(ug, s5m[S5_BLOCK], zh, zh, bp, S5_ROW_BLOCKS, ug_lead=ug_meta)
    y_prompt = finish(xp, o, g, yg, u, TOKEN_TILE, S5_BLOCK)

    xs = x_sample.reshape(bs * seq_s, D_MODEL)
    to_g = lambda t: t[0].reshape(bs * S5_GROUPS // 2, LANES)
    q, k, v, g, u, lg, ug = _in_proj(xs, *proj_w, TOKEN_TILE, seq_s)
    o, s_s = _gla(r3(q, bs, seq_s), r3(k, bs, seq_s), r3(v, bs, seq_s), r3(lg, bs, seq_s),
                  state_gla[0].reshape(bs, GLA_KDIM, GLA_DV), GLA_SAMPLE_SEQS, GLA_SAMPLE_GROUP, seq_s)
    yg, hs_re, hs_im = _s5(ug, s5m[seq_s], to_g(state_s5_re), to_g(state_s5_im), 1, bs)
    y_sample = finish(xs, o.reshape(bs * seq_s, D_GLA), g, yg, u, TOKEN_TILE, seq_s)

    gla_out = lambda s, b: s.reshape(1, b, GLA_HEADS, GLA_DK, GLA_DV)
    s5_out = lambda h: h.reshape(1, -1, S5_GROUPS, S5_STATE)
    return (y_prompt.reshape(bp, seq_p, D_MODEL), y_sample.reshape(bs, seq_s, D_MODEL),
            gla_out(s_p, bp), s5_out(hp_re), s5_out(hp_im),
            gla_out(s_s, bs), s5_out(hs_re), s5_out(hs_im))
```

```python
import functools

import jax
import jax.numpy as jnp
from jax import lax
from jax.experimental import pallas as pl
from jax.experimental.pallas import tpu as pltpu

F32 = jnp.float32
BF16 = jnp.bfloat16

D_MODEL = 1024
D_GLA = 512
GLA_HEADS = 4
GLA_DV = 128
GLA_DK = 64
GLA_KDIM = 256
GATE_RANK = 16
GATE_NORM = 16.0
GLA_CHUNK = 64
D_S5 = 512
S5_GROUP = 16
S5_GROUPS = 32
S5_STATE = 64
N_META = 16
D_FF = 2816
EPS = 1e-6
LANES = 128
S5_BLOCK = 16
SLAB_GROUPS = LANES // S5_GROUP
FF_CHUNK = 256
TOKEN_TILE = 512
SAMPLE_TOKEN_TILE = 256
S5_ROW_BLOCKS = 128
GLA_SAMPLE_GROUP = 16
GLA_SAMPLE_SEQS = 32
VMEM_LIMIT = 48 * 1024 * 1024


def _rms(x, g):
    return x * lax.rsqrt(jnp.mean(x * x, axis=-1, keepdims=True) + EPS) * g


def _dot(a, b):
    return jnp.dot(a, b, preferred_element_type=F32)


def _dot_nt(a, b):
    return lax.dot_general(a, b, (((1,), (1,)), ((), ())), preferred_element_type=F32)


def _dot_tn(a, b):
    return lax.dot_general(a, b, (((0,), (0,)), ((), ())), preferred_element_type=F32)


def _const_spec(shape):
    zeros = (0,) * len(shape)
    return pl.BlockSpec(shape, lambda *_: zeros)


PREP_GROUPS = 8


def _cmul(a, b):
    return a[0] * b[0] - a[1] * b[1], a[0] * b[1] + a[1] * b[0]


def _unit_powers(c1, s1, expo, n_bits):
    acc = (jnp.ones_like(c1), jnp.zeros_like(c1))
    base = (c1, s1)
    squares = [base]
    for bit in range(n_bits):
        take = ((expo >> bit) & 1) == 1
        nxt = _cmul(acc, base)
        acc = (jnp.where(take, nxt[0], acc[0]), jnp.where(take, nxt[1], acc[1]))
        base = _cmul(base, base)
        squares.append(base)
    return acc, squares


def _s5_prep_kernel(arc_ref, aic_ref, air_ref, ldt_ref, br_ref, bi_ref, cr_ref, ci_ref,
                    mi16_ref, mi8_ref, in16_ref, in8_ref, out16_ref, out8_ref,
                    a16r_ref, a16i_ref, a8r_ref, a8i_ref):
    n, q = S5_STATE, S5_BLOCK
    w = q * S5_GROUP
    hp = lax.Precision.HIGHEST
    lane = lax.broadcasted_iota(jnp.int32, (n, w), 1)
    t_blk = lane >> 4
    eye = lax.broadcasted_iota(jnp.int32, (n, n), 0) == lax.broadcasted_iota(jnp.int32, (n, n), 1)
    to_col = lambda r: jnp.sum(jnp.where(eye, r, 0.0), axis=1, keepdims=True)
    zrows = lambda x: jnp.concatenate([x, jnp.zeros_like(x)], axis=0)
    lane_n = lax.broadcasted_iota(jnp.int32, (n, LANES), 1)
    rep = jnp.where((lax.broadcasted_iota(jnp.int32, (S5_GROUP, w), 1) & (S5_GROUP - 1))
                    == lax.broadcasted_iota(jnp.int32, (S5_GROUP, w), 0), 1.0, 0.0)
    mo_g, rev_g, hop_g = [], [], []
    for g in range(PREP_GROUPS):
        dt = jnp.exp(ldt_ref[g])
        ang_r = air_ref[g] * dt
        c1, s1 = to_col(jnp.cos(ang_r)), to_col(jnp.sin(ang_r))
        lam_re = jnp.minimum(arc_ref[g], -1e-4)
        lam_im = aic_ref[g]
        unit, squares = _unit_powers(c1, s1, t_blk, 4)
        pm = jnp.exp(t_blk.astype(F32) * (lam_re * dt))
        pk = (pm * unit[0], pm * unit[1])
        mag = jnp.exp(lam_re * dt)
        ab = (mag * c1, mag * s1)
        p1 = _cmul(pk, ab)
        ct = tuple(lax.dot_general(r[g], rep, (((0,), (0,)), ((), ())), precision=hp,
                                   preferred_element_type=F32) for r in (cr_ref, ci_ref))
        bt = tuple(jnp.dot(r[g], rep, precision=hp, preferred_element_type=F32) for r in (br_ref, bi_ref))
        g0 = _cmul(ct, pk)
        mo = _cmul(ct, p1)
        mo_g.append((mo[0], -mo[1]))
        den = lam_re * lam_re + lam_im * lam_im
        nr, ni = ab[0] - 1.0, ab[1]
        f = ((nr * lam_re + ni * lam_im) / den, (ni * lam_re - nr * lam_im) / den)
        e = _cmul(pk, _cmul(f, bt))
        et = [zrows(x).T for x in e]
        rev_g.append([jnp.concatenate([x[(q - 1 - s) * S5_GROUP:(q - s) * S5_GROUP] for s in range(q)], axis=0)
                      for x in et])
        hops = []
        for steps in (q, q // 2):
            m = jnp.exp(float(steps) * (lam_re * dt))
            u = squares[steps.bit_length() - 1]
            hops += [m * u[0], m * u[1]]
        cols = jnp.zeros((n, LANES), F32)
        for idx, hcol in enumerate(hops):
            cols = jnp.where(lane_n == idx, hcol, cols)
        hop_g.append(zrows(cols).T)
        t0 = (jnp.dot(et[0][:S5_GROUP, :n], g0[0], precision=hp, preferred_element_type=F32)
              - jnp.dot(et[1][:S5_GROUP, :n], g0[1], precision=hp, preferred_element_type=F32))
        lane_t = lax.broadcasted_iota(jnp.int32, t0.shape, 1) >> 4
        for s in range(q):
            blk = t0 if s == 0 else jnp.where(lane_t >= s, pltpu.roll(t0, S5_GROUP * s, 1), 0.0)
            mi16_ref[g, s * S5_GROUP:(s + 1) * S5_GROUP, :] = blk.astype(BF16)
            if s < q // 2:
                mi8_ref[g, s * S5_GROUP:(s + 1) * S5_GROUP, :] = blk[:, :w // 2].astype(BF16)

    half = lambda x: pltpu.roll(x, n, 1)
    for p in range(PREP_GROUPS // 2):
        g0, g1 = 2 * p, 2 * p + 1
        for c in range(2):
            part = slice(c * LANES, (c + 1) * LANES)
            lo, hi = rev_g[g0][c], half(rev_g[g1][c])
            in16_ref[p, :, part] = jnp.concatenate([lo, hi], axis=0).astype(BF16)
            in8_ref[p, :, part] = jnp.concatenate([lo[w // 2:], hi[w // 2:]], axis=0).astype(BF16)
            m0, m1 = mo_g[g0][c], mo_g[g1][c]
            z, zh = jnp.zeros_like(m0), jnp.zeros((n, w // 2), F32)
            out16_ref[p, part, :] = jnp.concatenate([jnp.concatenate([m0, z], axis=1),
                                                     jnp.concatenate([z, m1], axis=1)], axis=0).astype(BF16)
            out8_ref[p, part, :] = jnp.concatenate([jnp.concatenate([m0[:, :w // 2], zh], axis=1),
                                                    jnp.concatenate([zh, m1[:, :w // 2]], axis=1)],
                                                   axis=0).astype(BF16)
        hops = hop_g[g0] + half(hop_g[g1])
        a16r_ref[p] = hops[0:1]
        a16i_ref[p] = hops[1:2]
        a8r_ref[p] = hops[2:3]
        a8i_ref[p] = hops[3:4]


def _s5_prep(a_re, a_im, b_re, b_im, c_re, c_im, log_dt):
    g, n, j, q = S5_GROUPS, S5_STATE, S5_GROUP, S5_BLOCK
    w, h, pg = q * j, q * j // 2, PREP_GROUPS
    col = lambda t: t.reshape(g, n, 1)
    rowv = lambda t: t.reshape(g, 1, n)
    spec = lambda a, b_: pl.BlockSpec((pg, a, b_), lambda i: (i, 0, 0))
    pspec = lambda a, b_: pl.BlockSpec((pg // 2, a, b_), lambda i: (i, 0, 0))
    pair_shapes = [(2 * w, 2 * LANES), (2 * h, 2 * LANES), (2 * LANES, 2 * w), (2 * LANES, 2 * h)]
    outs = pl.pallas_call(
        _s5_prep_kernel,
        grid=(g // pg,),
        in_specs=[spec(n, 1), spec(n, 1), spec(1, n), spec(1, 1),
                  spec(n, j), spec(n, j), spec(j, n), spec(j, n)],
        out_specs=[spec(w, w), spec(h, h)] + [pspec(*s) for s in pair_shapes] + [pspec(1, LANES)] * 4,
        out_shape=[jax.ShapeDtypeStruct((g, w, w), BF16), jax.ShapeDtypeStruct((g, h, h), BF16)]
                  + [jax.ShapeDtypeStruct((g // 2,) + s, BF16) for s in pair_shapes]
                  + [jax.ShapeDtypeStruct((g // 2, 1, LANES), F32)] * 4,
        compiler_params=pltpu.CompilerParams(dimension_semantics=("parallel",)),
        name="s5_prep",
    )(col(a_re), col(a_im), rowv(a_im), log_dt.reshape(g, 1, 1),
      b_re, b_im, c_re, c_im)
    mi16, mi8, in16, in8, out16, out8, a16r, a16i, a8r, a8i = outs
    return {q: dict(m_intra=mi16, m_in=in16, m_out=out16, a_re=a16r, a_im=a16i),
            q // 2: dict(m_intra=mi8, m_in=in8, m_out=out8, a_re=a8r, a_im=a8i)}


def _lane_block_transpose(a):
    a = list(a)
    blk = lax.broadcasted_iota(jnp.int32, a[0].shape, 1) >> 4
    for d in (4, 2, 1):
        upper = (blk & d) != 0
        for r in range(8):
            if r & d:
                continue
            lo, hi = a[r], a[r + d]
            a[r] = jnp.where(upper, pltpu.roll(hi, 16 * d, 1), lo)
            a[r + d] = jnp.where(upper, hi, pltpu.roll(lo, LANES - 16 * d, 1))
    return a


def _in_proj_kernel(*refs, steps, n_tiles, tiles_per_seq):
    fused = tiles_per_seq is not None
    x_ref, gpre_ref, wa_ref, wb_ref, wgk_ref, bgk_ref = refs[:6]
    if fused:
        (s0_ref, g_ref, u_ref, o_ref, ug_ref, sfin_ref,
         u_scr, gk_scr, q_scr, k_scr, v_scr, s_scr) = refs[6:]
    else:
        q_ref, k_ref, v_ref, g_ref, u_ref, lg_ref, ug_ref, u_scr, gk_scr = refs[6:]
    i = pl.program_id(0)
    tm = x_ref.shape[0]

    def project(slot):
        h = _rms(x_ref[...], gpre_ref[...]).astype(BF16)
        proj = _dot(h, wa_ref[...])
        tail_proj = _dot(h, wb_ref[...])
        c_k, c_v, c_g = GLA_KDIM, 2 * GLA_KDIM, 2 * GLA_KDIM + D_GLA
        q, k, v, gate = proj[:, :c_k], proj[:, c_k:c_v], proj[:, c_v:c_g], proj[:, c_g:]
        if fused:
            q_scr[slot] = q
            k_scr[slot] = k
            v_scr[slot] = v.astype(BF16)
        else:
            q_ref[...] = q
            k_ref[...] = k
            v_ref[...] = v
        g_ref[...] = gate
        u = tail_proj[:, :D_S5]
        u_ref[...] = u
        for s in range(D_S5 // LANES):
            u_scr[slot, s] = u[:, s * LANES:(s + 1) * LANES]
        gk_scr[slot] = tail_proj[:, D_S5:]

    def tail(slot):
        z = _dot(gk_scr[slot].astype(BF16), wgk_ref[...]) + bgk_ref[...]
        lg = jax.nn.log_sigmoid(z) * (1.0 / GATE_NORM)
        if fused:
            tile = i - 1
            seq = tile // tiles_per_seq
            state = jnp.where(tile % tiles_per_seq == 0, s0_ref[seq], s_scr[...])
            masks = _gla_masks(1, GLA_CHUNK)
            for c in range(tm // GLA_CHUNK):
                rows = slice(c * GLA_CHUNK, (c + 1) * GLA_CHUNK)
                o, (state,) = _gla_group(q_scr[slot, rows, :], k_scr[slot, rows, :], v_scr[slot, rows, :],
                                         lg[rows], [state], masks, GLA_CHUNK)
                o_ref[rows, :] = o
            s_scr[...] = state
            sfin_ref[seq] = state
        else:
            lg_ref[...] = lg
        nr = tm // steps
        rt = min(nr, 16)
        for s in range(D_S5 // LANES):
            for hf in range(steps // 8):
                for r0 in range(0, nr, rt):
                    a = [u_scr[slot, s, pl.ds(r0 * steps + hf * 8 + t, rt, stride=steps), :] for t in range(8)]
                    per_group = _lane_block_transpose(a)
                    for g in range(SLAB_GROUPS):
                        ug_ref[s * SLAB_GROUPS + g, r0:r0 + rt, hf * LANES:(hf + 1) * LANES] = (
                            per_group[g].astype(BF16))

    @pl.when(i == 0)
    def _():
        if fused:
            s_scr[...] = jnp.zeros(s_scr.shape, F32)
        project(0)

    @pl.when((i > 0) & (i < n_tiles))
    def _():
        tail((i - 1) & 1)
        project(i & 1)

    @pl.when(i == n_tiles)
    def _():
        tail((i - 1) & 1)


def _in_proj(x, gpre, w_a, w_b, wgk_p, bgk, tm, steps, gla_s0=None, seq_len=None):
    t = x.shape[0]
    n = t // tm
    cur = lambda i: jnp.minimum(i, n - 1)
    prev = lambda i: jnp.maximum(i - 1, 0)
    row = lambda w, at: pl.BlockSpec((tm, w), lambda i: (at(i), 0))
    wg = steps * S5_GROUP
    ug_spec = pl.BlockSpec((S5_GROUPS, tm // steps, wg), lambda i: (0, prev(i), 0))
    ug_shape = jax.ShapeDtypeStruct((S5_GROUPS, t // steps, wg), BF16)
    f32 = lambda w: jax.ShapeDtypeStruct((t, w), F32)
    in_specs = [row(D_MODEL, cur), _const_spec((1, D_MODEL)), _const_spec(w_a.shape), _const_spec(w_b.shape),
                _const_spec((LANES, GLA_KDIM)), _const_spec((1, GLA_KDIM))]
    scratch = [pltpu.VMEM((2, D_S5 // LANES, tm, LANES), F32), pltpu.VMEM((2, tm, LANES), F32)]
    args = (x, gpre, w_a, w_b, wgk_p, bgk)
    if gla_s0 is None:
        tiles_per_seq = None
        widths = (GLA_KDIM, GLA_KDIM, D_GLA, D_GLA, D_S5)
        out_specs = [row(w, cur) for w in widths] + [row(GLA_KDIM, prev), ug_spec]
        out_shape = [f32(w) for w in widths] + [f32(GLA_KDIM), ug_shape]
    else:
        tiles_per_seq = seq_len // tm
        assert seq_len % tm == 0 and tm % GLA_CHUNK == 0
        in_specs.append(_const_spec(gla_s0.shape))
        args += (gla_s0,)
        out_specs = [row(D_GLA, cur), row(D_S5, cur), row(D_GLA, prev), ug_spec, _const_spec(gla_s0.shape)]
        out_shape = [f32(D_GLA), f32(D_S5), f32(D_GLA), ug_shape, jax.ShapeDtypeStruct(gla_s0.shape, F32)]
        scratch += [pltpu.VMEM((2, tm, GLA_KDIM), F32), pltpu.VMEM((2, tm, GLA_KDIM), F32),
                    pltpu.VMEM((2, tm, D_GLA), BF16), pltpu.VMEM(gla_s0.shape[1:], F32)]
    return pl.pallas_call(
        functools.partial(_in_proj_kernel, steps=steps, n_tiles=n, tiles_per_seq=tiles_per_seq),
        grid=(n + 1,),
        in_specs=in_specs,
        out_specs=out_specs,
        out_shape=out_shape,
        scratch_shapes=scratch,
        compiler_params=pltpu.CompilerParams(dimension_semantics=("arbitrary",),
                                             vmem_limit_bytes=VMEM_LIMIT),
        name="in_proj" if gla_s0 is None else "in_proj_gla",
    )(*args)


def _gla_masks(sg, chunk):
    r = sg * chunk
    assert r >= LANES or sg == 1
    shift = chunk.bit_length() - 1
    hk = GLA_HEADS * r
    ri = lax.broadcasted_iota(jnp.int32, (r, r), 0)
    ci = lax.broadcasted_iota(jnp.int32, (r, r), 1)
    ri4 = lax.broadcasted_iota(jnp.int32, (r, hk), 0)
    ci4 = lax.broadcasted_iota(jnp.int32, (r, hk), 1) & (r - 1)
    stack_head = lax.broadcasted_iota(jnp.int32, (hk, 1), 0) >> (r.bit_length() - 1)
    k_head = lax.broadcasted_iota(jnp.int32, (1, GLA_KDIM), 1) >> 6
    v_head = lax.broadcasted_iota(jnp.int32, (1, D_GLA), 1) >> 7
    s_head = lax.broadcasted_iota(jnp.int32, (GLA_KDIM, 1), 0) >> 6
    return dict(
        tri_bf=jnp.where(((ri >> shift) == (ci >> shift)) & (ri >= ci), 1.0, 0.0).astype(BF16),
        causal4=((ri4 >> shift) == (ci4 >> shift)) & (ri4 >= ci4),
        k_diag=stack_head == k_head, v_diag=stack_head == v_head, s_diag=s_head == v_head)


def _gla_group(q, k, v_bf, lg, states, m, chunk):
    sg = len(states)
    r = sg * chunk
    tile4 = lambda t: jnp.concatenate([t] * GLA_HEADS, axis=0)
    lg_hi = lg.astype(BF16)
    lg_lo = (lg - lg_hi.astype(F32)).astype(BF16)
    b2 = _dot(m["tri_bf"], jnp.concatenate([lg_hi, lg_lo], axis=1))
    b = b2[:, :GLA_KDIM] + b2[:, GLA_KDIM:]
    lasts = [b[(i + 1) * chunk - 1:(i + 1) * chunk, :] for i in range(sg)]
    bl = jnp.concatenate([jnp.broadcast_to(t, (chunk, GLA_KDIM)) for t in lasts], axis=0)
    qd = (q * (GLA_DK ** -0.5) * jnp.exp(b)).astype(BF16)
    ki = (k * jnp.exp(-b)).astype(BF16)
    ke = k * jnp.exp(bl - b)

    ki_bd = jnp.where(m["k_diag"], tile4(ki), 0.0)
    att = jnp.where(m["causal4"], _dot_nt(qd, ki_bd), 0.0).astype(BF16)
    v_bd = jnp.where(m["v_diag"], tile4(v_bf), 0.0)
    o_intra = _dot(att, v_bd)

    if r < LANES:
        aug_t = jnp.concatenate([ke, jnp.broadcast_to(lasts[0], (LANES - r, GLA_KDIM))], axis=0).T
        ke_t, bl_t = aug_t, aug_t[:, r:]
    else:
        ke_t = ke.T
        bl_t = jnp.concatenate(lasts + [jnp.zeros((r - sg, GLA_KDIM), F32)], axis=0).T
    ke_t = ke_t.astype(BF16)

    outs, new_states = [], []
    for i, s_old in enumerate(states):
        rows = slice(i * chunk, (i + 1) * chunk)
        s_bd = jnp.where(m["s_diag"], jnp.concatenate([s_old.astype(BF16)] * GLA_HEADS, axis=1), 0.0)
        outs.append(o_intra[rows] + _dot(qd[rows], s_bd))
        upd = [_dot(ke_t[h * GLA_DK:(h + 1) * GLA_DK, rows], v_bf[rows, h * GLA_DV:(h + 1) * GLA_DV])
               for h in range(GLA_HEADS)]
        new_states.append(jnp.exp(bl_t[:, i:i + 1]) * s_old + jnp.concatenate(upd, axis=0))
    return (outs[0] if sg == 1 else jnp.concatenate(outs, axis=0)), new_states


def _gla_kernel(q_ref, k_ref, v_ref, lg_ref, s0_ref, o_ref, s_ref, *, bb, sg, chunk):
    r = sg * chunk

    @pl.when(pl.program_id(1) == 0)
    def _():
        s_ref[...] = s0_ref[...]

    masks = _gla_masks(sg, chunk)
    for gi in range(bb // sg):
        seqs = slice(gi * sg, (gi + 1) * sg)
        o, new = _gla_group(q_ref[seqs].reshape(r, GLA_KDIM), k_ref[seqs].reshape(r, GLA_KDIM),
                            v_ref[seqs].reshape(r, D_GLA).astype(BF16), lg_ref[seqs].reshape(r, GLA_KDIM),
                            [s_ref[gi * sg + i] for i in range(sg)], masks, chunk)
        o_ref[seqs] = o.reshape(sg, chunk, D_GLA)
        for i in range(sg):
            s_ref[gi * sg + i] = new[i]


def _gla(q, k, v, lg, s0, bb, sg, chunk):
    b, l, _ = q.shape
    blk = lambda w: pl.BlockSpec((bb, chunk, w), lambda i, c: (i, c, 0))
    sspec = pl.BlockSpec((bb, GLA_KDIM, GLA_DV), lambda i, c: (i, 0, 0))
    return pl.pallas_call(
        functools.partial(_gla_kernel, bb=bb, sg=sg, chunk=chunk),
        grid=(b // bb, l // chunk),
        in_specs=[blk(GLA_KDIM), blk(GLA_KDIM), blk(D_GLA), blk(GLA_KDIM), sspec],
        out_specs=[blk(D_GLA), sspec],
        out_shape=[jax.ShapeDtypeStruct((b, l, D_GLA), F32),
                   jax.ShapeDtypeStruct((b, GLA_KDIM, GLA_DV), F32)],
        compiler_params=pltpu.CompilerParams(dimension_semantics=("parallel", "arbitrary"),
                                             vmem_limit_bytes=VMEM_LIMIT),
        name=f"gla_c{chunk}",
    )(q, k, v, lg, s0)


def _s5_kernel(*refs, nb, cb, lead):
    ug_ref, refs = refs[0], refs[1:]
    if lead:
        lead_ref, refs = refs[0], refs[1:]
    (mi_ref, min_ref, mout_ref, ar_ref, ai_ref, h0r_ref, h0i_ref, yg_ref, hfr_ref, hfi_ref,
     vr_s, vi_s, hr_s, hi_s, str_s, sti_s) = refs
    gs = SLAB_GROUPS // 2
    n_pairs = S5_GROUPS // 2
    w = ug_ref.shape[-1]
    rows = nb * cb
    batch = cb if nb == 1 else nb
    n_blocks = rows // batch
    interleave = nb > 1 and cb > 1
    first_pair = pl.program_id(0) * gs
    seq0 = pl.program_id(1) * batch if nb == 1 else 0

    def state_rows(p):
        return pl.ds(seq0 * n_pairs + first_pair + p, batch, stride=n_pairs)

    def load_state():
        for p in range(gs):
            hr, hi = h0r_ref[state_rows(p), :], h0i_ref[state_rows(p), :]
            if lead:
                v = _dot(jnp.concatenate([lead_ref[2 * p], lead_ref[2 * p + 1]], axis=1), min_ref[p])
                hr, hi = (ar_ref[p] * hr - ai_ref[p] * hi + v[:, :LANES],
                          ar_ref[p] * hi + ai_ref[p] * hr + v[:, LANES:])
            str_s[p] = hr
            sti_s[p] = hi

    if nb == 1:
        load_state()
    else:
        pl.when(pl.program_id(1) == 0)(load_state)

    def to_scan_order(ref, g, val):
        if not interleave:
            ref[g] = val
        else:
            for b in range(nb):
                ref[g, pl.ds(b, cb, stride=nb), :] = val[b * cb:(b + 1) * cb]

    def from_scan_order(ref, g):
        if not interleave:
            return ref[g]
        return jnp.concatenate([ref[g, pl.ds(b, cb, stride=nb), :] for b in range(nb)], axis=0)

    ub = [ug_ref[g].reshape(rows, w) for g in range(2 * gs)]
    for g in range(gs):
        v = _dot(jnp.concatenate([ub[2 * g], ub[2 * g + 1]], axis=1), min_ref[g])
        to_scan_order(vr_s, g, v[:, :LANES])
        to_scan_order(vi_s, g, v[:, LANES:])
    ar = [ar_ref[g] for g in range(gs)]
    ai = [ai_ref[g] for g in range(gs)]

    def body(c, carry):
        rows = pl.ds(pl.multiple_of(c * batch, batch), batch)
        new = []
        for g in range(gs):
            hr, hi = carry[2 * g], carry[2 * g + 1]
            hr_s[g, rows, :] = hr
            hi_s[g, rows, :] = hi
            new.append(ar[g] * hr - ai[g] * hi + vr_s[g, rows, :])
            new.append(ar[g] * hi + ai[g] * hr + vi_s[g, rows, :])
        return tuple(new)

    init = tuple(ref[g] for g in range(gs) for ref in (str_s, sti_s))
    fin = lax.fori_loop(0, n_blocks, body, init, unroll=min(n_blocks, 4))
    for g in range(gs):
        str_s[g] = fin[2 * g]
        sti_s[g] = fin[2 * g + 1]
        hfr_ref[state_rows(g), :] = fin[2 * g]
        hfi_ref[state_rows(g), :] = fin[2 * g + 1]
        h_in = jnp.concatenate([from_scan_order(hr_s, g), from_scan_order(hi_s, g)], axis=1).astype(BF16)
        y_state = _dot(h_in, mout_ref[g])
        for k in range(2):
            y = _dot(ub[2 * g + k], mi_ref[2 * g + k]) + y_state[:, k * w:(k + 1) * w]
            yg_ref[2 * g + k] = y.reshape(yg_ref.shape[1:])


def _s5(ug, m, h0_re, h0_im, nb, cb, ug_lead=None):
    g_all, r, w = ug.shape
    nc = r // nb
    batch = cb if nb == 1 else nb
    n, gs = LANES, SLAB_GROUPS
    uspec = pl.BlockSpec((gs, nb, cb, w), lambda s, i: (s, 0, i, 0))
    per_g = lambda a, b_: pl.BlockSpec((gs, a, b_), lambda s, i: (s, 0, 0))
    per_p = lambda a, b_: pl.BlockSpec((gs // 2, a, b_), lambda s, i: (s, 0, 0))
    hspec = _const_spec(h0_re.shape)
    lead = ug_lead is not None
    yg, hf_re, hf_im = pl.pallas_call(
        functools.partial(_s5_kernel, nb=nb, cb=cb, lead=lead),
        grid=(g_all // gs, nc // cb),
        in_specs=[uspec] + ([per_g(nb, w)] if lead else [])
                 + [per_g(w, w), per_p(2 * w, 2 * n), per_p(2 * n, 2 * w), per_p(1, n), per_p(1, n), hspec, hspec],
        out_specs=[uspec, hspec, hspec],
        out_shape=[jax.ShapeDtypeStruct((g_all, nb, nc, w), F32),
                   jax.ShapeDtypeStruct(h0_re.shape, F32),
                   jax.ShapeDtypeStruct(h0_im.shape, F32)],
        scratch_shapes=[pltpu.VMEM((gs // 2, nb * cb, n), F32)] * 4 + [pltpu.VMEM((gs // 2, batch, n), F32)] * 2,
        compiler_params=pltpu.CompilerParams(dimension_semantics=("arbitrary", "arbitrary"),
                                             vmem_limit_bytes=VMEM_LIMIT),
        name=f"s5_w{w}",
    )(ug.reshape(g_all, nb, nc, w), *((ug_lead,) if lead else ()),
      m["m_intra"], m["m_in"], m["m_out"], m["a_re"], m["a_im"], h0_re, h0_im)
    return yg.reshape(g_all, r, w), hf_re, hf_im


def _mix_ffn_kernel(x_ref, o_ref, g_ref, yg_ref, u_ref, dsk_ref, gn_ref, wglu_ref, s5n_ref, wo_ref, gpost_ref,
                    gpre_ref, wg_ref, wu_ref, wd_ref, gpostf_ref, out_ref, y_scr, x1_scr, *, steps, n_tiles):
    i = pl.program_id(0)

    def mix_into(slot):
        nr = x_ref.shape[0] // steps
        rt = min(nr, 8)
        for s in range(D_S5 // LANES):
            for hf in range(steps // 8):
                for r0 in range(0, nr, rt):
                    a = [yg_ref[s * SLAB_GROUPS + g, r0:r0 + rt, hf * LANES:(hf + 1) * LANES]
                         for g in range(SLAB_GROUPS)]
                    per_step = _lane_block_transpose(a)
                    for t in range(8):
                        y_scr[s, pl.ds(r0 * steps + hf * 8 + t, rt, stride=steps), :] = per_step[t]
        y5 = jnp.concatenate([y_scr[s] for s in range(D_S5 // LANES)], axis=1) + dsk_ref[...] * u_ref[...]
        o = o_ref[...]
        gn = gn_ref[...]
        heads = []
        for h in range(GLA_HEADS):
            heads.append(_rms(o[:, h * GLA_DV:(h + 1) * GLA_DV], gn))
        og = jnp.concatenate(heads, axis=1) * jax.nn.silu(g_ref[...])
        y = jax.nn.gelu(y5)
        y = y * jax.nn.sigmoid(_dot(y.astype(BF16), wglu_ref[...]))
        y = _rms(y, s5n_ref[...])
        mix = _dot(og.astype(BF16), wo_ref[:D_GLA, :]) + _dot(y.astype(BF16), wo_ref[D_GLA:, :])
        x1_scr[slot] = x_ref[...] + _rms(mix, gpost_ref[...])

    def ffn_from(slot):
        x = x1_scr[slot]
        h = _rms(x, gpre_ref[...]).astype(BF16)
        acc = jnp.zeros(x.shape, F32)
        for c in range(D_FF // FF_CHUNK):
            cols = slice(c * FF_CHUNK, (c + 1) * FF_CHUNK)
            act = jax.nn.silu(_dot(h, wg_ref[:, cols])) * _dot(h, wu_ref[:, cols])
            acc = acc + _dot(act.astype(BF16), wd_ref[cols, :])
        out_ref[...] = x + _rms(acc, gpostf_ref[...])

    @pl.when(i == 0)
    def _():
        mix_into(0)

    @pl.when((i > 0) & (i < n_tiles))
    def _():
        ffn_from((i - 1) & 1)
        mix_into(i & 1)

    @pl.when(i == n_tiles)
    def _():
        ffn_from((i - 1) & 1)


def _mix_ffn(x, o, g, yg, u, dsk, gn, wglu, s5n, wo, gpost, gpre, wg, wu, wd, gpostf, tm, steps):
    t = x.shape[0]
    n = t // tm
    cur = lambda i: jnp.minimum(i, n - 1)
    row = lambda w: pl.BlockSpec((tm, w), lambda i: (cur(i), 0))
    fixed = lambda shape: pl.BlockSpec(shape, lambda i: (0,) * len(shape), pipeline_mode=pl.Buffered(1))
    return pl.pallas_call(
        functools.partial(_mix_ffn_kernel, steps=steps, n_tiles=n),
        grid=(n + 1,),
        in_specs=[row(D_MODEL), row(D_GLA), row(D_GLA),
                  pl.BlockSpec((S5_GROUPS, tm // steps, steps * S5_GROUP), lambda i: (0, cur(i), 0)),
                  row(D_S5), fixed((1, D_S5)),
                  fixed((1, GLA_DV)), fixed((D_S5, D_S5)), fixed((1, D_S5)),
                  fixed((D_GLA + D_S5, D_MODEL)), fixed((1, D_MODEL)),
                  fixed((1, D_MODEL)), fixed((D_MODEL, D_FF)), fixed((D_MODEL, D_FF)),
                  fixed((D_FF, D_MODEL)), fixed((1, D_MODEL))],
        out_specs=pl.BlockSpec((tm, D_MODEL), lambda i: (jnp.maximum(i - 1, 0), 0)),
        out_shape=jax.ShapeDtypeStruct((t, D_MODEL), F32),
        scratch_shapes=[pltpu.VMEM((D_S5 // LANES, tm, LANES), F32), pltpu.VMEM((2, tm, D_MODEL), F32)],
        compiler_params=pltpu.CompilerParams(dimension_semantics=("arbitrary",),
                                             vmem_limit_bytes=VMEM_LIMIT),
        name="mix_ffn",
    )(x, o, g, yg, u, dsk, gn, wglu, s5n, wo, gpost, gpre, wg, wu, wd, gpostf)


def kernel(x_prompt, x_sample, state_gla, state_s5_re, state_s5_im, meta_tokens, g_pre_mix, w_in, w_gk2, b_gk, gla_norm, s5_a_re, s5_a_im, s5_b_re, s5_b_im, s5_c_re, s5_c_im, s5_d, s5_log_dt, w_s5_glu, s5_norm, w_o, g_post_mix, g_pre_ffn, w_gate, w_up, w_down, g_post_ffn):
    assert g_pre_mix.shape[0] == 1, "single-layer step"
    bp, seq_p, _ = x_prompt.shape
    bs, seq_s, _ = x_sample.shape
    row = lambda t: t[0].reshape(1, -1)

    w = w_in[0]
    c3, c4 = 2 * GLA_KDIM + 2 * D_GLA, 2 * GLA_KDIM + 2 * D_GLA + GATE_RANK
    w_a = w[:, :c3].astype(BF16)
    w_b = jnp.concatenate([w[:, c4:].astype(BF16), w[:, c3:c4].astype(BF16),
                           jnp.zeros((D_MODEL, LANES - GATE_RANK), BF16)], axis=1)
    wgk_p = jnp.concatenate([w_gk2[0], jnp.zeros((LANES - GATE_RANK, GLA_KDIM), F32)], axis=0).astype(BF16)
    wo_bf = w_o[0].astype(BF16)
    s5m = _s5_prep(s5_a_re[0], s5_a_im[0], s5_b_re[0], s5_b_im[0], s5_c_re[0], s5_c_im[0], s5_log_dt[0])
    proj_w = (row(g_pre_mix), w_a, w_b, wgk_p, row(b_gk))

    def finish(x, o, g, yg, u, tm, steps):
        return _mix_ffn(x, o, g, yg, u, row(s5_d), row(gla_norm), w_s5_glu[0].astype(BF16), row(s5_norm),
                        wo_bf, row(g_post_mix), row(g_pre_ffn), w_gate[0].astype(BF16),
                        w_up[0].astype(BF16), w_down[0].astype(BF16), row(g_post_ffn), tm, steps)

    xm = jnp.broadcast_to(meta_tokens[None], (bp, N_META, D_MODEL)).reshape(bp * N_META, D_MODEL)
    q, k, v, _, _, lg, ug_meta = _in_proj(xm, *proj_w, bp * N_META, S5_BLOCK)
    r3 = lambda t, b, l: t.reshape(b, l, t.shape[-1])
    _, s_meta = _gla(r3(q, bp, N_META), r3(k, bp, N_META), r3(v, bp, N_META), r3(lg, bp, N_META),
                     jnp.zeros((bp, GLA_KDIM, GLA_DV), F32), bp, bp, N_META)
    zh = jnp.zeros((bp * S5_GROUPS // 2, LANES), F32)

    xp = x_prompt.reshape(bp * seq_p, D_MODEL)
    g, u, o, ug, s_p = _in_proj(xp, *proj_w, TOKEN_TILE, S5_BLOCK, gla_s0=s_meta, seq_len=seq_p)
    yg, hp_re, hp_im = _s5(ug, s5m[S5_BLOCK], zh, zh, bp, S5_ROW_BLOCKS, ug_lead=ug_meta)
    y_prompt = finish(xp, o, g, yg, u, TOKEN_TILE, S5_BLOCK)

    xs = x_sample.reshape(bs * seq_s, D_MODEL)
    to_g = lambda t: t[0].reshape(bs * S5_GROUPS // 2, LANES)
    q, k, v, g, u, lg, ug = _in_proj(xs, *proj_w, SAMPLE_TOKEN_TILE, seq_s)
    o, s_s = _gla(r3(q, bs, seq_s), r3(k, bs, seq_s), r3(v, bs, seq_s), r3(lg, bs, seq_s),
                  state_gla[0].reshape(bs, GLA_KDIM, GLA_DV), GLA_SAMPLE_SEQS, GLA_SAMPLE_GROUP, seq_s)
    yg, hs_re, hs_im = _s5(ug, s5m[seq_s], to_g(state_s5_re), to_g(state_s5_im), 1, bs)
    y_sample = finish(xs, o.reshape(bs * seq_s, D_GLA), g, yg, u, SAMPLE_TOKEN_TILE, seq_s)

    gla_out = lambda s, b: s.reshape(1, b, GLA_HEADS, GLA_DK, GLA_DV)
    s5_out = lambda h: h.reshape(1, -1, S5_GROUPS, S5_STATE)
    return (y_prompt.reshape(bp, seq_p, D_MODEL), y_sample.reshape(bs, seq_s, D_MODEL),
            gla_out(s_p, bp), s5_out(hp_re), s5_out(hp_im),
            gla_out(s_s, bs), s5_out(hs_re), s5_out(hs_im))
```

```python
import functools

import jax
import jax.numpy as jnp
from jax import lax
from jax.experimental import pallas as pl
from jax.experimental.pallas import tpu as pltpu

F32 = jnp.float32
BF16 = jnp.bfloat16

D_MODEL = 1024
D_GLA = 512
GLA_HEADS = 4
GLA_DV = 128
GLA_DK = 64
GLA_KDIM = 256
GATE_RANK = 16
GATE_NORM = 16.0
GLA_CHUNK = 64
D_S5 = 512
S5_GROUP = 16
S5_GROUPS = 32
S5_STATE = 64
N_META = 16
D_FF = 2816
EPS = 1e-6
LANES = 128
S5_BLOCK = 16
SLAB_GROUPS = LANES // S5_GROUP
FF_CHUNK = 256
TOKEN_TILE = 512
S5_ROW_BLOCKS = 128
GLA_SAMPLE_GROUP = 16
GLA_SAMPLE_SEQS = 32
VMEM_LIMIT = 48 * 1024 * 1024


def _rms(x, g):
    return x * lax.rsqrt(jnp.mean(x * x, axis=-1, keepdims=True) + EPS) * g


def _dot(a, b):
    return jnp.dot(a, b, preferred_element_type=F32)


def _dot_nt(a, b):
    return lax.dot_general(a, b, (((1,), (1,)), ((), ())), preferred_element_type=F32)


def _dot_tn(a, b):
    return lax.dot_general(a, b, (((0,), (0,)), ((), ())), preferred_element_type=F32)


def _const_spec(shape):
    zeros = (0,) * len(shape)
    return pl.BlockSpec(shape, lambda *_: zeros)


PREP_GROUPS = 8


def _cmul(a, b):
    return a[0] * b[0] - a[1] * b[1], a[0] * b[1] + a[1] * b[0]


def _unit_powers(c1, s1, expo, n_bits):
    acc = (jnp.ones_like(c1), jnp.zeros_like(c1))
    base = (c1, s1)
    squares = [base]
    for bit in range(n_bits):
        take = ((expo >> bit) & 1) == 1
        nxt = _cmul(acc, base)
        acc = (jnp.where(take, nxt[0], acc[0]), jnp.where(take, nxt[1], acc[1]))
        base = _cmul(base, base)
        squares.append(base)
    return acc, squares


def _s5_prep_kernel(arc_ref, aic_ref, air_ref, ldt_ref, br_ref, bi_ref, cr_ref, ci_ref,
                    mi16_ref, mi8_ref, in16_ref, in8_ref, out16_ref, out8_ref,
                    a16r_ref, a16i_ref, a8r_ref, a8i_ref):
    n, q = S5_STATE, S5_BLOCK
    w = q * S5_GROUP
    hp = lax.Precision.HIGHEST
    lane = lax.broadcasted_iota(jnp.int32, (n, w), 1)
    t_blk = lane >> 4
    eye = lax.broadcasted_iota(jnp.int32, (n, n), 0) == lax.broadcasted_iota(jnp.int32, (n, n), 1)
    to_col = lambda r: jnp.sum(jnp.where(eye, r, 0.0), axis=1, keepdims=True)
    zrows = lambda x: jnp.concatenate([x, jnp.zeros_like(x)], axis=0)
    lane_n = lax.broadcasted_iota(jnp.int32, (n, LANES), 1)
    rep = jnp.where((lax.broadcasted_iota(jnp.int32, (S5_GROUP, w), 1) & (S5_GROUP - 1))
                    == lax.broadcasted_iota(jnp.int32, (S5_GROUP, w), 0), 1.0, 0.0)
    mo_g, rev_g, hop_g = [], [], []
    for g in range(PREP_GROUPS):
        dt = jnp.exp(ldt_ref[g])
        ang_r = air_ref[g] * dt
        c1, s1 = to_col(jnp.cos(ang_r)), to_col(jnp.sin(ang_r))
        lam_re = jnp.minimum(arc_ref[g], -1e-4)
        lam_im = aic_ref[g]
        unit, squares = _unit_powers(c1, s1, t_blk, 4)
        pm = jnp.exp(t_blk.astype(F32) * (lam_re * dt))
        pk = (pm * unit[0], pm * unit[1])
        mag = jnp.exp(lam_re * dt)
        ab = (mag * c1, mag * s1)
        p1 = _cmul(pk, ab)
        ct = tuple(lax.dot_general(r[g], rep, (((0,), (0,)), ((), ())), precision=hp,
                                   preferred_element_type=F32) for r in (cr_ref, ci_ref))
        bt = tuple(jnp.dot(r[g], rep, precision=hp, preferred_element_type=F32) for r in (br_ref, bi_ref))
        g0 = _cmul(ct, pk)
        mo = _cmul(ct, p1)
        mo_g.append((mo[0], -mo[1]))
        den = lam_re * lam_re + lam_im * lam_im
        nr, ni = ab[0] - 1.0, ab[1]
        f = ((nr * lam_re + ni * lam_im) / den, (ni * lam_re - nr * lam_im) / den)
        e = _cmul(pk, _cmul(f, bt))
        et = [zrows(x).T for x in e]
        rev_g.append([jnp.concatenate([x[(q - 1 - s) * S5_GROUP:(q - s) * S5_GROUP] for s in range(q)], axis=0)
                      for x in et])
        hops = []
        for steps in (q, q // 2):
            m = jnp.exp(float(steps) * (lam_re * dt))
            u = squares[steps.bit_length() - 1]
            hops += [m * u[0], m * u[1]]
        cols = jnp.zeros((n, LANES), F32)
        for idx, hcol in enumerate(hops):
            cols = jnp.where(lane_n == idx, hcol, cols)
        hop_g.append(zrows(cols).T)
        t0 = (jnp.dot(et[0][:S5_GROUP, :n], g0[0], precision=hp, preferred_element_type=F32)
              - jnp.dot(et[1][:S5_GROUP, :n], g0[1], precision=hp, preferred_element_type=F32))
        lane_t = lax.broadcasted_iota(jnp.int32, t0.shape, 1) >> 4
        for s in range(q):
            blk = t0 if s == 0 else jnp.where(lane_t >= s, pltpu.roll(t0, S5_GROUP * s, 1), 0.0)
            mi16_ref[g, s * S5_GROUP:(s + 1) * S5_GROUP, :] = blk.astype(BF16)
            if s < q // 2:
                mi8_ref[g, s * S5_GROUP:(s + 1) * S5_GROUP, :] = blk[:, :w // 2].astype(BF16)

    half = lambda x: pltpu.roll(x, n, 1)
    for p in range(PREP_GROUPS // 2):
        g0, g1 = 2 * p, 2 * p + 1
        for c in range(2):
            part = slice(c * LANES, (c + 1) * LANES)
            lo, hi = rev_g[g0][c], half(rev_g[g1][c])
            in16_ref[p, :, part] = jnp.concatenate([lo, hi], axis=0).astype(BF16)
            in8_ref[p, :, part] = jnp.concatenate([lo[w // 2:], hi[w // 2:]], axis=0).astype(BF16)
            m0, m1 = mo_g[g0][c], mo_g[g1][c]
            z, zh = jnp.zeros_like(m0), jnp.zeros((n, w // 2), F32)
            out16_ref[p, part, :] = jnp.concatenate([jnp.concatenate([m0, z], axis=1),
                                                     jnp.concatenate([z, m1], axis=1)], axis=0).astype(BF16)
            out8_ref[p, part, :] = jnp.concatenate([jnp.concatenate([m0[:, :w // 2], zh], axis=1),
                                                    jnp.concatenate([zh, m1[:, :w // 2]], axis=1)],
                                                   axis=0).astype(BF16)
        hops = hop_g[g0] + half(hop_g[g1])
        a16r_ref[p] = hops[0:1]
        a16i_ref[p] = hops[1:2]
        a8r_ref[p] = hops[2:3]
        a8i_ref[p] = hops[3:4]


def _s5_prep(a_re, a_im, b_re, b_im, c_re, c_im, log_dt):
    g, n, j, q = S5_GROUPS, S5_STATE, S5_GROUP, S5_BLOCK
    w, h, pg = q * j, q * j // 2, PREP_GROUPS
    col = lambda t: t.reshape(g, n, 1)
    rowv = lambda t: t.reshape(g, 1, n)
    spec = lambda a, b_: pl.BlockSpec((pg, a, b_), lambda i: (i, 0, 0))
    pspec = lambda a, b_: pl.BlockSpec((pg // 2, a, b_), lambda i: (i, 0, 0))
    pair_shapes = [(2 * w, 2 * LANES), (2 * h, 2 * LANES), (2 * LANES, 2 * w), (2 * LANES, 2 * h)]
    outs = pl.pallas_call(
        _s5_prep_kernel,
        grid=(g // pg,),
        in_specs=[spec(n, 1), spec(n, 1), spec(1, n), spec(1, 1),
                  spec(n, j), spec(n, j), spec(j, n), spec(j, n)],
        out_specs=[spec(w, w), spec(h, h)] + [pspec(*s) for s in pair_shapes] + [pspec(1, LANES)] * 4,
        out_shape=[jax.ShapeDtypeStruct((g, w, w), BF16), jax.ShapeDtypeStruct((g, h, h), BF16)]
                  + [jax.ShapeDtypeStruct((g // 2,) + s, BF16) for s in pair_shapes]
                  + [jax.ShapeDtypeStruct((g // 2, 1, LANES), F32)] * 4,
        compiler_params=pltpu.CompilerParams(dimension_semantics=("parallel",)),
        name="s5_prep",
    )(col(a_re), col(a_im), rowv(a_im), log_dt.reshape(g, 1, 1),
      b_re, b_im, c_re, c_im)
    mi16, mi8, in16, in8, out16, out8, a16r, a16i, a8r, a8i = outs
    return {q: dict(m_intra=mi16, m_in=in16, m_out=out16, a_re=a16r, a_im=a16i),
            q // 2: dict(m_intra=mi8, m_in=in8, m_out=out8, a_re=a8r, a_im=a8i)}


def _lane_block_transpose(a):
    a = list(a)
    blk = lax.broadcasted_iota(jnp.int32, a[0].shape, 1) >> 4
    for d in (4, 2, 1):
        upper = (blk & d) != 0
        for r in range(8):
            if r & d:
                continue
            lo, hi = a[r], a[r + d]
            a[r] = jnp.where(upper, pltpu.roll(hi, 16 * d, 1), lo)
            a[r + d] = jnp.where(upper, hi, pltpu.roll(lo, LANES - 16 * d, 1))
    return a


def _in_proj_kernel(*refs, steps, n_tiles, tiles_per_seq):
    fused = tiles_per_seq is not None
    x_ref, gpre_ref, wa_ref, wb_ref, wgk_ref, bgk_ref = refs[:6]
    if fused:
        (s0_ref, g_ref, u_ref, o_ref, ug_ref, sfin_ref,
         u_scr, gk_scr, q_scr, k_scr, v_scr, s_scr) = refs[6:]
    else:
        q_ref, k_ref, v_ref, g_ref, u_ref, lg_ref, ug_ref, u_scr, gk_scr = refs[6:]
    i = pl.program_id(0)
    tm = x_ref.shape[0]

    def project(slot):
        h = _rms(x_ref[...], gpre_ref[...]).astype(BF16)
        proj = _dot(h, wa_ref[...])
        tail_proj = _dot(h, wb_ref[...])
        c_k, c_v, c_g = GLA_KDIM, 2 * GLA_KDIM, 2 * GLA_KDIM + D_GLA
        q, k, v, gate = proj[:, :c_k], proj[:, c_k:c_v], proj[:, c_v:c_g], proj[:, c_g:]
        if fused:
            q_scr[slot] = q
            k_scr[slot] = k
            v_scr[slot] = v.astype(BF16)
        else:
            q_ref[...] = q
            k_ref[...] = k
            v_ref[...] = v
        g_ref[...] = gate
        u = tail_proj[:, :D_S5]
        u_ref[...] = u
        for s in range(D_S5 // LANES):
            u_scr[slot, s] = u[:, s * LANES:(s + 1) * LANES]
        gk_scr[slot] = tail_proj[:, D_S5:]

    def tail(slot):
        z = _dot(gk_scr[slot].astype(BF16), wgk_ref[...]) + bgk_ref[...]
        lg = jax.nn.log_sigmoid(z) * (1.0 / GATE_NORM)
        if fused:
            tile = i - 1
            seq = tile // tiles_per_seq
            state = jnp.where(tile % tiles_per_seq == 0, s0_ref[seq], s_scr[...])
            masks = _gla_masks(1, GLA_CHUNK)
            for c in range(tm // GLA_CHUNK):
                rows = slice(c * GLA_CHUNK, (c + 1) * GLA_CHUNK)
                o, (state,) = _gla_group(q_scr[slot, rows, :], k_scr[slot, rows, :], v_scr[slot, rows, :],
                                         lg[rows], [state], masks, GLA_CHUNK)
                o_ref[rows, :] = o
            s_scr[...] = state
            sfin_ref[seq] = state
        else:
            lg_ref[...] = lg
        nr = tm // steps
        rt = min(nr, 16)
        for s in range(D_S5 // LANES):
            for hf in range(steps // 8):
                for r0 in range(0, nr, rt):
                    a = [u_scr[slot, s, pl.ds(r0 * steps + hf * 8 + t, rt, stride=steps), :] for t in range(8)]
                    per_group = _lane_block_transpose(a)
                    for g in range(SLAB_GROUPS):
                        ug_ref[s * SLAB_GROUPS + g, r0:r0 + rt, hf * LANES:(hf + 1) * LANES] = (
                            per_group[g].astype(BF16))

    @pl.when(i == 0)
    def _():
        if fused:
            s_scr[...] = jnp.zeros(s_scr.shape, F32)
        project(0)

    @pl.when((i > 0) & (i < n_tiles))
    def _():
        tail((i - 1) & 1)
        project(i & 1)

    @pl.when(i == n_tiles)
    def _():
        tail((i - 1) & 1)


def _in_proj(x, gpre, w_a, w_b, wgk_p, bgk, tm, steps, gla_s0=None, seq_len=None):
    t = x.shape[0]
    n = t // tm
    cur = lambda i: jnp.minimum(i, n - 1)
    prev = lambda i: jnp.maximum(i - 1, 0)
    row = lambda w, at: pl.BlockSpec((tm, w), lambda i: (at(i), 0))
    wg = steps * S5_GROUP
    ug_spec = pl.BlockSpec((S5_GROUPS, tm // steps, wg), lambda i: (0, prev(i), 0))
    ug_shape = jax.ShapeDtypeStruct((S5_GROUPS, t // steps, wg), BF16)
    f32 = lambda w: jax.ShapeDtypeStruct((t, w), F32)
    in_specs = [row(D_MODEL, cur), _const_spec((1, D_MODEL)), _const_spec(w_a.shape), _const_spec(w_b.shape),
                _const_spec((LANES, GLA_KDIM)), _const_spec((1, GLA_KDIM))]
    scratch = [pltpu.VMEM((2, D_S5 // LANES, tm, LANES), F32), pltpu.VMEM((2, tm, LANES), F32)]
    args = (x, gpre, w_a, w_b, wgk_p, bgk)
    if gla_s0 is None:
        tiles_per_seq = None
        widths = (GLA_KDIM, GLA_KDIM, D_GLA, D_GLA, D_S5)
        out_specs = [row(w, cur) for w in widths] + [row(GLA_KDIM, prev), ug_spec]
        out_shape = [f32(w) for w in widths] + [f32(GLA_KDIM), ug_shape]
    else:
        tiles_per_seq = seq_len // tm
        assert seq_len % tm == 0 and tm % GLA_CHUNK == 0
        in_specs.append(_const_spec(gla_s0.shape))
        args += (gla_s0,)
        out_specs = [row(D_GLA, cur), row(D_S5, cur), row(D_GLA, prev), ug_spec, _const_spec(gla_s0.shape)]
        out_shape = [f32(D_GLA), f32(D_S5), f32(D_GLA), ug_shape, jax.ShapeDtypeStruct(gla_s0.shape, F32)]
        scratch += [pltpu.VMEM((2, tm, GLA_KDIM), F32), pltpu.VMEM((2, tm, GLA_KDIM), F32),
                    pltpu.VMEM((2, tm, D_GLA), BF16), pltpu.VMEM(gla_s0.shape[1:], F32)]
    return pl.pallas_call(
        functools.partial(_in_proj_kernel, steps=steps, n_tiles=n, tiles_per_seq=tiles_per_seq),
        grid=(n + 1,),
        in_specs=in_specs,
        out_specs=out_specs,
        out_shape=out_shape,
        scratch_shapes=scratch,
        compiler_params=pltpu.CompilerParams(dimension_semantics=("arbitrary",),
                                             vmem_limit_bytes=VMEM_LIMIT),
        name="in_proj" if gla_s0 is None else "in_proj_gla",
    )(*args)


def _gla_masks(sg, chunk):
    r = sg * chunk
    assert r >= LANES or sg == 1
    shift = chunk.bit_length() - 1
    hk = GLA_HEADS * r
    ri = lax.broadcasted_iota(jnp.int32, (r, r), 0)
    ci = lax.broadcasted_iota(jnp.int32, (r, r), 1)
    ri4 = lax.broadcasted_iota(jnp.int32, (r, hk), 0)
    ci4 = lax.broadcasted_iota(jnp.int32, (r, hk), 1) & (r - 1)
    stack_head = lax.broadcasted_iota(jnp.int32, (hk, 1), 0) >> (r.bit_length() - 1)
    k_head = lax.broadcasted_iota(jnp.int32, (1, GLA_KDIM), 1) >> 6
    v_head = lax.broadcasted_iota(jnp.int32, (1, D_GLA), 1) >> 7
    s_head = lax.broadcasted_iota(jnp.int32, (GLA_KDIM, 1), 0) >> 6
    return dict(
        tri_bf=jnp.where(((ri >> shift) == (ci >> shift)) & (ri >= ci), 1.0, 0.0).astype(BF16),
        causal4=((ri4 >> shift) == (ci4 >> shift)) & (ri4 >= ci4),
        k_diag=stack_head == k_head, v_diag=stack_head == v_head, s_diag=s_head == v_head)


def _gla_group(q, k, v_bf, lg, states, m, chunk):
    sg = len(states)
    r = sg * chunk
    tile4 = lambda t: jnp.concatenate([t] * GLA_HEADS, axis=0)
    lg_hi = lg.astype(BF16)
    lg_lo = (lg - lg_hi.astype(F32)).astype(BF16)
    b2 = _dot(m["tri_bf"], jnp.concatenate([lg_hi, lg_lo], axis=1))
    b = b2[:, :GLA_KDIM] + b2[:, GLA_KDIM:]
    lasts = [b[(i + 1) * chunk - 1:(i + 1) * chunk, :] for i in range(sg)]
    bl = jnp.concatenate([jnp.broadcast_to(t, (chunk, GLA_KDIM)) for t in lasts], axis=0)
    qd = (q * (GLA_DK ** -0.5) * jnp.exp(b)).astype(BF16)
    ki = (k * jnp.exp(-b)).astype(BF16)
    ke = k * jnp.exp(bl - b)

    ki_bd = jnp.where(m["k_diag"], tile4(ki), 0.0)
    att = jnp.where(m["causal4"], _dot_nt(qd, ki_bd), 0.0).astype(BF16)
    v_bd = jnp.where(m["v_diag"], tile4(v_bf), 0.0)
    o_intra = _dot(att, v_bd)

    if r < LANES:
        aug_t = jnp.concatenate([ke, jnp.broadcast_to(lasts[0], (LANES - r, GLA_KDIM))], axis=0).T
        ke_t, bl_t = aug_t, aug_t[:, r:]
    else:
        ke_t = ke.T
        bl_t = jnp.concatenate(lasts + [jnp.zeros((r - sg, GLA_KDIM), F32)], axis=0).T
    ke_t = ke_t.astype(BF16)

    outs, new_states = [], []
    for i, s_old in enumerate(states):
        rows = slice(i * chunk, (i + 1) * chunk)
        s_bd = jnp.where(m["s_diag"], jnp.concatenate([s_old.astype(BF16)] * GLA_HEADS, axis=1), 0.0)
        outs.append(o_intra[rows] + _dot(qd[rows], s_bd))
        upd = [_dot(ke_t[h * GLA_DK:(h + 1) * GLA_DK, rows], v_bf[rows, h * GLA_DV:(h + 1) * GLA_DV])
               for h in range(GLA_HEADS)]
        new_states.append(jnp.exp(bl_t[:, i:i + 1]) * s_old + jnp.concatenate(upd, axis=0))
    return (outs[0] if sg == 1 else jnp.concatenate(outs, axis=0)), new_states


def _gla_kernel(q_ref, k_ref, v_ref, lg_ref, s0_ref, o_ref, s_ref, *, bb, sg, chunk):
    r = sg * chunk

    @pl.when(pl.program_id(1) == 0)
    def _():
        s_ref[...] = s0_ref[...]

    masks = _gla_masks(sg, chunk)
    for gi in range(bb // sg):
        seqs = slice(gi * sg, (gi + 1) * sg)
        o, new = _gla_group(q_ref[seqs].reshape(r, GLA_KDIM), k_ref[seqs].reshape(r, GLA_KDIM),
                            v_ref[seqs].reshape(r, D_GLA).astype(BF16), lg_ref[seqs].reshape(r, GLA_KDIM),
                            [s_ref[gi * sg + i] for i in range(sg)], masks, chunk)
        o_ref[seqs] = o.reshape(sg, chunk, D_GLA)
        for i in range(sg):
            s_ref[gi * sg + i] = new[i]


def _gla(q, k, v, lg, s0, bb, sg, chunk):
    b, l, _ = q.shape
    blk = lambda w: pl.BlockSpec((bb, chunk, w), lambda i, c: (i, c, 0))
    sspec = pl.BlockSpec((bb, GLA_KDIM, GLA_DV), lambda i, c: (i, 0, 0))
    return pl.pallas_call(
        functools.partial(_gla_kernel, bb=bb, sg=sg, chunk=chunk),
        grid=(b // bb, l // chunk),
        in_specs=[blk(GLA_KDIM), blk(GLA_KDIM), blk(D_GLA), blk(GLA_KDIM), sspec],
        out_specs=[blk(D_GLA), sspec],
        out_shape=[jax.ShapeDtypeStruct((b, l, D_GLA), F32),
                   jax.ShapeDtypeStruct((b, GLA_KDIM, GLA_DV), F32)],
        compiler_params=pltpu.CompilerParams(dimension_semantics=("parallel", "arbitrary"),
                                             vmem_limit_bytes=VMEM_LIMIT),
        name=f"gla_c{chunk}",
    )(q, k, v, lg, s0)


def _s5_kernel(*refs, nb, cb, lead):
    ug_ref, refs = refs[0], refs[1:]
    if lead:
        lead_ref, refs = refs[0], refs[1:]
    (mi_ref, min_ref, mout_ref, ar_ref, ai_ref, h0r_ref, h0i_ref, yg_ref, hfr_ref, hfi_ref,
     vr_s, vi_s, hr_s, hi_s, str_s, sti_s) = refs
    gs = SLAB_GROUPS // 2
    n_pairs = S5_GROUPS // 2
    w = ug_ref.shape[-1]
    rows = nb * cb
    batch = cb if nb == 1 else nb
    n_blocks = rows // batch
    interleave = nb > 1 and cb > 1
    first_pair = pl.program_id(0) * gs
    seq0 = pl.program_id(1) * batch if nb == 1 else 0

    def state_rows(p):
        return pl.ds(seq0 * n_pairs + first_pair + p, batch, stride=n_pairs)

    def load_state():
        for p in range(gs):
            hr, hi = h0r_ref[state_rows(p), :], h0i_ref[state_rows(p), :]
            if lead:
                v = _dot(jnp.concatenate([lead_ref[2 * p], lead_ref[2 * p + 1]], axis=1), min_ref[p])
                hr, hi = (ar_ref[p] * hr - ai_ref[p] * hi + v[:, :LANES],
                          ar_ref[p] * hi + ai_ref[p] * hr + v[:, LANES:])
            str_s[p] = hr
            sti_s[p] = hi

    if nb == 1:
        load_state()
    else:
        pl.when(pl.program_id(1) == 0)(load_state)

    def to_scan_order(ref, g, val):
        if not interleave:
            ref[g] = val
        else:
            for b in range(nb):
                ref[g, pl.ds(b, cb, stride=nb), :] = val[b * cb:(b + 1) * cb]

    def from_scan_order(ref, g):
        if not interleave:
            return ref[g]
        return jnp.concatenate([ref[g, pl.ds(b, cb, stride=nb), :] for b in range(nb)], axis=0)

    ub = [ug_ref[g].reshape(rows, w) for g in range(2 * gs)]
    for g in range(gs):
        v = _dot(jnp.concatenate([ub[2 * g], ub[2 * g + 1]], axis=1), min_ref[g])
        to_scan_order(vr_s, g, v[:, :LANES])
        to_scan_order(vi_s, g, v[:, LANES:])
    ar = [ar_ref[g] for g in range(gs)]
    ai = [ai_ref[g] for g in range(gs)]

    def body(c, carry):
        rows = pl.ds(pl.multiple_of(c * batch, batch), batch)
        new = []
        for g in range(gs):
            hr, hi = carry[2 * g], carry[2 * g + 1]
            hr_s[g, rows, :] = hr
            hi_s[g, rows, :] = hi
            new.append(ar[g] * hr - ai[g] * hi + vr_s[g, rows, :])
            new.append(ar[g] * hi + ai[g] * hr + vi_s[g, rows, :])
        return tuple(new)

    init = tuple(ref[g] for g in range(gs) for ref in (str_s, sti_s))
    fin = lax.fori_loop(0, n_blocks, body, init, unroll=min(n_blocks, 4))
    for g in range(gs):
        str_s[g] = fin[2 * g]
        sti_s[g] = fin[2 * g + 1]
        hfr_ref[state_rows(g), :] = fin[2 * g]
        hfi_ref[state_rows(g), :] = fin[2 * g + 1]
        h_in = jnp.concatenate([from_scan_order(hr_s, g), from_scan_order(hi_s, g)], axis=1).astype(BF16)
        y_state = _dot(h_in, mout_ref[g])
        for k in range(2):
            y = _dot(ub[2 * g + k], mi_ref[2 * g + k]) + y_state[:, k * w:(k + 1) * w]
            yg_ref[2 * g + k] = y.reshape(yg_ref.shape[1:])


def _s5(ug, m, h0_re, h0_im, nb, cb, ug_lead=None):
    g_all, r, w = ug.shape
    nc = r // nb
    batch = cb if nb == 1 else nb
    n, gs = LANES, SLAB_GROUPS
    uspec = pl.BlockSpec((gs, nb, cb, w), lambda s, i: (s, 0, i, 0))
    per_g = lambda a, b_: pl.BlockSpec((gs, a, b_), lambda s, i: (s, 0, 0))
    per_p = lambda a, b_: pl.BlockSpec((gs // 2, a, b_), lambda s, i: (s, 0, 0))
    hspec = _const_spec(h0_re.shape)
    lead = ug_lead is not None
    yg, hf_re, hf_im = pl.pallas_call(
        functools.partial(_s5_kernel, nb=nb, cb=cb, lead=lead),
        grid=(g_all // gs, nc // cb),
        in_specs=[uspec] + ([per_g(nb, w)] if lead else [])
                 + [per_g(w, w), per_p(2 * w, 2 * n), per_p(2 * n, 2 * w), per_p(1, n), per_p(1, n), hspec, hspec],
        out_specs=[uspec, hspec, hspec],
        out_shape=[jax.ShapeDtypeStruct((g_all, nb, nc, w), F32),
                   jax.ShapeDtypeStruct(h0_re.shape, F32),
                   jax.ShapeDtypeStruct(h0_im.shape, F32)],
        scratch_shapes=[pltpu.VMEM((gs // 2, nb * cb, n), F32)] * 4 + [pltpu.VMEM((gs // 2, batch, n), F32)] * 2,
        compiler_params=pltpu.CompilerParams(dimension_semantics=("arbitrary", "arbitrary"),
                                             vmem_limit_bytes=VMEM_LIMIT),
        name=f"s5_w{w}",
    )(ug.reshape(g_all, nb, nc, w), *((ug_lead,) if lead else ()),
      m["m_intra"], m["m_in"], m["m_out"], m["a_re"], m["a_im"], h0_re, h0_im)
    return yg.reshape(g_all, r, w), hf_re, hf_im


def _mix_ffn_kernel(x_ref, o_ref, g_ref, yg_ref, u_ref, dsk_ref, gn_ref, wglu_ref, s5n_ref, wo_ref, gpost_ref,
                    gpre_ref, wg_ref, wu_ref, wd_ref, gpostf_ref, out_ref, y_scr, x1_scr, act_scr, *, steps, n_tiles):
    i = pl.program_id(0)

    def mix_into(slot):
        nr = x_ref.shape[0] // steps
        rt = min(nr, 8)
        for s in range(D_S5 // LANES):
            for hf in range(steps // 8):
                for r0 in range(0, nr, rt):
                    a = [yg_ref[s * SLAB_GROUPS + g, r0:r0 + rt, hf * LANES:(hf + 1) * LANES]
                         for g in range(SLAB_GROUPS)]
                    per_step = _lane_block_transpose(a)
                    for t in range(8):
                        y_scr[s, pl.ds(r0 * steps + hf * 8 + t, rt, stride=steps), :] = per_step[t]
        y5 = jnp.concatenate([y_scr[s] for s in range(D_S5 // LANES)], axis=1) + dsk_ref[...] * u_ref[...]
        o = o_ref[...]
        gn = gn_ref[...]
        heads = []
        for h in range(GLA_HEADS):
            heads.append(_rms(o[:, h * GLA_DV:(h + 1) * GLA_DV], gn))
        og = jnp.concatenate(heads, axis=1) * jax.nn.silu(g_ref[...])
        y = jax.nn.gelu(y5)
        y = y * jax.nn.sigmoid(_dot(y.astype(BF16), wglu_ref[...]))
        y = _rms(y, s5n_ref[...])
        mix = _dot(og.astype(BF16), wo_ref[:D_GLA, :]) + _dot(y.astype(BF16), wo_ref[D_GLA:, :])
        x1_scr[slot] = x_ref[...] + _rms(mix, gpost_ref[...])

    def ffn_from(slot):
        x = x1_scr[slot]
        h = _rms(x, gpre_ref[...]).astype(BF16)
        for c in range(D_FF // FF_CHUNK):
            cols = slice(c * FF_CHUNK, (c + 1) * FF_CHUNK)
            act = jax.nn.silu(_dot(h, wg_ref[:, cols])) * _dot(h, wu_ref[:, cols])
            act_scr[:, cols] = act.astype(BF16)
        out_ref[...] = x + _rms(_dot(act_scr[...], wd_ref[...]), gpostf_ref[...])

    @pl.when(i == 0)
    def _():
        mix_into(0)

    @pl.when((i > 0) & (i < n_tiles))
    def _():
        ffn_from((i - 1) & 1)
        mix_into(i & 1)

    @pl.when(i == n_tiles)
    def _():
        ffn_from((i - 1) & 1)


def _mix_ffn(x, o, g, yg, u, dsk, gn, wglu, s5n, wo, gpost, gpre, wg, wu, wd, gpostf, tm, steps):
    t = x.shape[0]
    n = t // tm
    cur = lambda i: jnp.minimum(i, n - 1)
    row = lambda w: pl.BlockSpec((tm, w), lambda i: (cur(i), 0))
    fixed = lambda shape: pl.BlockSpec(shape, lambda i: (0,) * len(shape), pipeline_mode=pl.Buffered(1))
    return pl.pallas_call(
        functools.partial(_mix_ffn_kernel, steps=steps, n_tiles=n),
        grid=(n + 1,),
        in_specs=[row(D_MODEL), row(D_GLA), row(D_GLA),
                  pl.BlockSpec((S5_GROUPS, tm // steps, steps * S5_GROUP), lambda i: (0, cur(i), 0)),
                  row(D_S5), fixed((1, D_S5)),
                  fixed((1, GLA_DV)), fixed((D_S5, D_S5)), fixed((1, D_S5)),
                  fixed((D_GLA + D_S5, D_MODEL)), fixed((1, D_MODEL)),
                  fixed((1, D_MODEL)), fixed((D_MODEL, D_FF)), fixed((D_MODEL, D_FF)),
                  fixed((D_FF, D_MODEL)), fixed((1, D_MODEL))],
        out_specs=pl.BlockSpec((tm, D_MODEL), lambda i: (jnp.maximum(i - 1, 0), 0)),
        out_shape=jax.ShapeDtypeStruct((t, D_MODEL), F32),
        scratch_shapes=[pltpu.VMEM((D_S5 // LANES, tm, LANES), F32), pltpu.VMEM((2, tm, D_MODEL), F32),
                        pltpu.VMEM((tm, D_FF), BF16)],
        compiler_params=pltpu.CompilerParams(dimension_semantics=("arbitrary",),
                                             vmem_limit_bytes=VMEM_LIMIT),
        name="mix_ffn",
    )(x, o, g, yg, u, dsk, gn, wglu, s5n, wo, gpost, gpre, wg, wu, wd, gpostf)


def kernel(x_prompt, x_sample, state_gla, state_s5_re, state_s5_im, meta_tokens, g_pre_mix, w_in, w_gk2, b_gk, gla_norm, s5_a_re, s5_a_im, s5_b_re, s5_b_im, s5_c_re, s5_c_im, s5_d, s5_log_dt, w_s5_glu, s5_norm, w_o, g_post_mix, g_pre_ffn, w_gate, w_up, w_down, g_post_ffn):
    assert g_pre_mix.shape[0] == 1, "single-layer step"
    bp, seq_p, _ = x_prompt.shape
    bs, seq_s, _ = x_sample.shape
    row = lambda t: t[0].reshape(1, -1)

    w = w_in[0]
    c3, c4 = 2 * GLA_KDIM + 2 * D_GLA, 2 * GLA_KDIM + 2 * D_GLA + GATE_RANK
    w_a = w[:, :c3].astype(BF16)
    w_b = jnp.concatenate([w[:, c4:].astype(BF16), w[:, c3:c4].astype(BF16),
                           jnp.zeros((D_MODEL, LANES - GATE_RANK), BF16)], axis=1)
    wgk_p = jnp.concatenate([w_gk2[0], jnp.zeros((LANES - GATE_RANK, GLA_KDIM), F32)], axis=0).astype(BF16)
    wo_bf = w_o[0].astype(BF16)
    s5m = _s5_prep(s5_a_re[0], s5_a_im[0], s5_b_re[0], s5_b_im[0], s5_c_re[0], s5_c_im[0], s5_log_dt[0])
    proj_w = (row(g_pre_mix), w_a, w_b, wgk_p, row(b_gk))

    def finish(x, o, g, yg, u, tm, steps):
        return _mix_ffn(x, o, g, yg, u, row(s5_d), row(gla_norm), w_s5_glu[0].astype(BF16), row(s5_norm),
                        wo_bf, row(g_post_mix), row(g_pre_ffn), w_gate[0].astype(BF16),
                        w_up[0].astype(BF16), w_down[0].astype(BF16), row(g_post_ffn), tm, steps)

    xm = jnp.broadcast_to(meta_tokens[None], (bp, N_META, D_MODEL)).reshape(bp * N_META, D_MODEL)
    q, k, v, _, _, lg, ug_meta = _in_proj(xm, *proj_w, bp * N_META, S5_BLOCK)
    r3 = lambda t, b, l: t.reshape(b, l, t.shape[-1])
    _, s_meta = _gla(r3(q, bp, N_META), r3(k, bp, N_META), r3(v, bp, N_META), r3(lg, bp, N_META),
                     jnp.zeros((bp, GLA_KDIM, GLA_DV), F32), bp, bp, N_META)
    zh = jnp.zeros((bp * S5_GROUPS // 2, LANES), F32)

    xp = x_prompt.reshape(bp * seq_p, D_MODEL)
    g, u, o, ug, s_p = _in_proj(xp, *proj_w, TOKEN_TILE, S5_BLOCK, gla_s0=s_meta, seq_len=seq_p)
    yg, hp_re, hp_im = _s5(ug, s5m[S5_BLOCK], zh, zh, bp, S5_ROW_BLOCKS, ug_lead=ug_meta)
    y_prompt = finish(xp, o, g, yg, u, TOKEN_TILE, S5_BLOCK)

    xs = x_sample.reshape(bs * seq_s, D_MODEL)
    to_g = lambda t: t[0].reshape(bs * S5_GROUPS // 2, LANES)
    q, k, v, g, u, lg, ug = _in_proj(xs, *proj_w, TOKEN_TILE, seq_s)
    o, s_s = _gla(r3(q, bs, seq_s), r3(k, bs, seq_s), r3(v, bs, seq_s), r3(lg, bs, seq_s),
                  state_gla[0].reshape(bs, GLA_KDIM, GLA_DV), GLA_SAMPLE_SEQS, GLA_SAMPLE_GROUP, seq_s)
    yg, hs_re, hs_im = _s5(ug, s5m[seq_s], to_g(state_s5_re), to_g(state_s5_im), 1, bs)
    y_sample = finish(xs, o.reshape(bs * seq_s, D_GLA), g, yg, u, TOKEN_TILE, seq_s)

    gla_out = lambda s, b: s.reshape(1, b, GLA_HEADS, GLA_DK, GLA_DV)
    s5_out = lambda h: h.reshape(1, -1, S5_GROUPS, S5_STATE)
    return (y_prompt.reshape(bp, seq_p, D_MODEL), y_sample.reshape(bs, seq_s, D_MODEL),
            gla_out(s_p, bp), s5_out(hp_re), s5_out(hp_im),
            gla_out(s_s, bs), s5_out(hs_re), s5_out(hs_im))
```

```python
import functools

import jax
import jax.numpy as jnp
from jax import lax
from jax.experimental import pallas as pl
from jax.experimental.pallas import tpu as pltpu

F32 = jnp.float32
BF16 = jnp.bfloat16

D_MODEL = 1024
D_GLA = 512
GLA_HEADS = 4
GLA_DV = 128
GLA_DK = 64
GLA_KDIM = 256
GATE_RANK = 16
GATE_NORM = 16.0
GLA_CHUNK = 64
D_S5 = 512
S5_GROUP = 16
S5_GROUPS = 32
S5_STATE = 64
N_META = 16
D_FF = 2816
EPS = 1e-6
LANES = 128
S5_BLOCK = 16
SLAB_GROUPS = LANES // S5_GROUP
FF_CHUNK = 256
TOKEN_TILE = 512
S5_ROW_BLOCKS = 128
GLA_SAMPLE_GROUP = 16
GLA_SAMPLE_SEQS = 32
VMEM_LIMIT = 48 * 1024 * 1024


def _rms(x, g):
    return x * lax.rsqrt(jnp.mean(x * x, axis=-1, keepdims=True) + EPS) * g


def _dot(a, b):
    return jnp.dot(a, b, preferred_element_type=F32)


def _dot_nt(a, b):
    return lax.dot_general(a, b, (((1,), (1,)), ((), ())), preferred_element_type=F32)


def _dot_tn(a, b):
    return lax.dot_general(a, b, (((0,), (0,)), ((), ())), preferred_element_type=F32)


def _const_spec(shape):
    zeros = (0,) * len(shape)
    return pl.BlockSpec(shape, lambda *_: zeros)


PREP_GROUPS = 8


def _cmul(a, b):
    return a[0] * b[0] - a[1] * b[1], a[0] * b[1] + a[1] * b[0]


def _unit_powers(c1, s1, expo, n_bits):
    acc = (jnp.ones_like(c1), jnp.zeros_like(c1))
    base = (c1, s1)
    squares = [base]
    for bit in range(n_bits):
        take = ((expo >> bit) & 1) == 1
        nxt = _cmul(acc, base)
        acc = (jnp.where(take, nxt[0], acc[0]), jnp.where(take, nxt[1], acc[1]))
        base = _cmul(base, base)
        squares.append(base)
    return acc, squares


def _s5_prep_kernel(arc_ref, aic_ref, air_ref, ldt_ref, br_ref, bi_ref, cr_ref, ci_ref,
                    mi16_ref, mi8_ref, in16_ref, in8_ref, out16_ref, out8_ref,
                    a16r_ref, a16i_ref, a8r_ref, a8i_ref):
    n, q = S5_STATE, S5_BLOCK
    w = q * S5_GROUP
    hp = lax.Precision.HIGHEST
    lane = lax.broadcasted_iota(jnp.int32, (n, w), 1)
    t_blk = lane >> 4
    eye = lax.broadcasted_iota(jnp.int32, (n, n), 0) == lax.broadcasted_iota(jnp.int32, (n, n), 1)
    to_col = lambda r: jnp.sum(jnp.where(eye, r, 0.0), axis=1, keepdims=True)
    zrows = lambda x: jnp.concatenate([x, jnp.zeros_like(x)], axis=0)
    lane_n = lax.broadcasted_iota(jnp.int32, (n, LANES), 1)
    rep = jnp.where((lax.broadcasted_iota(jnp.int32, (S5_GROUP, w), 1) & (S5_GROUP - 1))
                    == lax.broadcasted_iota(jnp.int32, (S5_GROUP, w), 0), 1.0, 0.0)
    mo_g, rev_g, hop_g = [], [], []
    for g in range(PREP_GROUPS):
        dt = jnp.exp(ldt_ref[g])
        ang_r = air_ref[g] * dt
        c1, s1 = to_col(jnp.cos(ang_r)), to_col(jnp.sin(ang_r))
        lam_re = jnp.minimum(arc_ref[g], -1e-4)
        lam_im = aic_ref[g]
        unit, squares = _unit_powers(c1, s1, t_blk, 4)
        pm = jnp.exp(t_blk.astype(F32) * (lam_re * dt))
        pk = (pm * unit[0], pm * unit[1])
        mag = jnp.exp(lam_re * dt)
        ab = (mag * c1, mag * s1)
        p1 = _cmul(pk, ab)
        ct = tuple(lax.dot_general(r[g], rep, (((0,), (0,)), ((), ())), precision=hp,
                                   preferred_element_type=F32) for r in (cr_ref, ci_ref))
        bt = tuple(jnp.dot(r[g], rep, precision=hp, preferred_element_type=F32) for r in (br_ref, bi_ref))
        g0 = _cmul(ct, pk)
        mo = _cmul(ct, p1)
        mo_g.append((mo[0], -mo[1]))
        den = lam_re * lam_re + lam_im * lam_im
        nr, ni = ab[0] - 1.0, ab[1]
        f = ((nr * lam_re + ni * lam_im) / den, (ni * lam_re - nr * lam_im) / den)
        e = _cmul(pk, _cmul(f, bt))
        et = [zrows(x).T for x in e]
        rev_g.append([jnp.concatenate([x[(q - 1 - s) * S5_GROUP:(q - s) * S5_GROUP] for s in range(q)], axis=0)
                      for x in et])
        hops = []
        for steps in (q, q // 2):
            m = jnp.exp(float(steps) * (lam_re * dt))
            u = squares[steps.bit_length() - 1]
            hops += [m * u[0], m * u[1]]
        cols = jnp.zeros((n, LANES), F32)
        for idx, hcol in enumerate(hops):
            cols = jnp.where(lane_n == idx, hcol, cols)
        hop_g.append(zrows(cols).T)
        t0 = (jnp.dot(et[0][:S5_GROUP, :n], g0[0], precision=hp, preferred_element_type=F32)
              - jnp.dot(et[1][:S5_GROUP, :n], g0[1], precision=hp, preferred_element_type=F32))
        lane_t = lax.broadcasted_iota(jnp.int32, t0.shape, 1) >> 4
        for s in range(q):
            blk = t0 if s == 0 else jnp.where(lane_t >= s, pltpu.roll(t0, S5_GROUP * s, 1), 0.0)
            mi16_ref[g, s * S5_GROUP:(s + 1) * S5_GROUP, :] = blk.astype(BF16)
            if s < q // 2:
                mi8_ref[g, s * S5_GROUP:(s + 1) * S5_GROUP, :] = blk[:, :w // 2].astype(BF16)

    half = lambda x: pltpu.roll(x, n, 1)
    for p in range(PREP_GROUPS // 2):
        g0, g1 = 2 * p, 2 * p + 1
        for c in range(2):
            part = slice(c * LANES, (c + 1) * LANES)
            lo, hi = rev_g[g0][c], half(rev_g[g1][c])
            in16_ref[p, :, part] = jnp.concatenate([lo, hi], axis=0).astype(BF16)
            in8_ref[p, :, part] = jnp.concatenate([lo[w // 2:], hi[w // 2:]], axis=0).astype(BF16)
            m0, m1 = mo_g[g0][c], mo_g[g1][c]
            z, zh = jnp.zeros_like(m0), jnp.zeros((n, w // 2), F32)
            out16_ref[p, part, :] = jnp.concatenate([jnp.concatenate([m0, z], axis=1),
                                                     jnp.concatenate([z, m1], axis=1)], axis=0).astype(BF16)
            out8_ref[p, part, :] = jnp.concatenate([jnp.concatenate([m0[:, :w // 2], zh], axis=1),
                                                    jnp.concatenate([zh, m1[:, :w // 2]], axis=1)],
                                                   axis=0).astype(BF16)
        hops = hop_g[g0] + half(hop_g[g1])
        a16r_ref[p] = hops[0:1]
        a16i_ref[p] = hops[1:2]
        a8r_ref[p] = hops[2:3]
        a8i_ref[p] = hops[3:4]


def _s5_prep(a_re, a_im, b_re, b_im, c_re, c_im, log_dt):
    g, n, j, q = S5_GROUPS, S5_STATE, S5_GROUP, S5_BLOCK
    w, h, pg = q * j, q * j // 2, PREP_GROUPS
    col = lambda t: t.reshape(g, n, 1)
    rowv = lambda t: t.reshape(g, 1, n)
    spec = lambda a, b_: pl.BlockSpec((pg, a, b_), lambda i: (i, 0, 0))
    pspec = lambda a, b_: pl.BlockSpec((pg // 2, a, b_), lambda i: (i, 0, 0))
    pair_shapes = [(2 * w, 2 * LANES), (2 * h, 2 * LANES), (2 * LANES, 2 * w), (2 * LANES, 2 * h)]
    outs = pl.pallas_call(
        _s5_prep_kernel,
        grid=(g // pg,),
        in_specs=[spec(n, 1), spec(n, 1), spec(1, n), spec(1, 1),
                  spec(n, j), spec(n, j), spec(j, n), spec(j, n)],
        out_specs=[spec(w, w), spec(h, h)] + [pspec(*s) for s in pair_shapes] + [pspec(1, LANES)] * 4,
        out_shape=[jax.ShapeDtypeStruct((g, w, w), BF16), jax.ShapeDtypeStruct((g, h, h), BF16)]
                  + [jax.ShapeDtypeStruct((g // 2,) + s, BF16) for s in pair_shapes]
                  + [jax.ShapeDtypeStruct((g // 2, 1, LANES), F32)] * 4,
        compiler_params=pltpu.CompilerParams(dimension_semantics=("parallel",)),
        name="s5_prep",
    )(col(a_re), col(a_im), rowv(a_im), log_dt.reshape(g, 1, 1),
      b_re, b_im, c_re, c_im)
    mi16, mi8, in16, in8, out16, out8, a16r, a16i, a8r, a8i = outs
    return {q: dict(m_intra=mi16, m_in=in16, m_out=out16, a_re=a16r, a_im=a16i),
            q // 2: dict(m_intra=mi8, m_in=in8, m_out=out8, a_re=a8r, a_im=a8i)}


def _lane_block_transpose(a):
    a = list(a)
    blk = lax.broadcasted_iota(jnp.int32, a[0].shape, 1) >> 4
    for d in (4, 2, 1):
        upper = (blk & d) != 0
        for r in range(8):
            if r & d:
                continue
            lo, hi = a[r], a[r + d]
            a[r] = jnp.where(upper, pltpu.roll(hi, 16 * d, 1), lo)
            a[r + d] = jnp.where(upper, hi, pltpu.roll(lo, LANES - 16 * d, 1))
    return a


def _in_proj_kernel(*refs, steps, n_tiles, tiles_per_seq):
    fused = tiles_per_seq is not None
    x_ref, gpre_ref, wa_ref, wb_ref, wgk_ref, bgk_ref = refs[:6]
    if fused:
        (s0_ref, g_ref, u_ref, o_ref, ug_ref, sfin_ref,
         u_scr, gk_scr, q_scr, k_scr, v_scr, s_scr) = refs[6:]
    else:
        q_ref, k_ref, v_ref, g_ref, u_ref, lg_ref, ug_ref, u_scr, gk_scr = refs[6:]
    i = pl.program_id(0)
    tm = x_ref.shape[0]

    def project(slot):
        h = _rms(x_ref[...], gpre_ref[...]).astype(BF16)
        proj = _dot(h, wa_ref[...])
        tail_proj = _dot(h, wb_ref[...])
        c_k, c_v, c_g = GLA_KDIM, 2 * GLA_KDIM, 2 * GLA_KDIM + D_GLA
        q, k, v, gate = proj[:, :c_k], proj[:, c_k:c_v], proj[:, c_v:c_g], proj[:, c_g:]
        if fused:
            q_scr[slot] = q
            k_scr[slot] = k
            v_scr[slot] = v.astype(BF16)
        else:
            q_ref[...] = q
            k_ref[...] = k
            v_ref[...] = v
        g_ref[...] = gate
        u = tail_proj[:, :D_S5]
        u_ref[...] = u
        for s in range(D_S5 // LANES):
            u_scr[slot, s] = u[:, s * LANES:(s + 1) * LANES]
        gk_scr[slot] = tail_proj[:, D_S5:]

    def tail(slot):
        z = _dot(gk_scr[slot].astype(BF16), wgk_ref[...]) + bgk_ref[...]
        lg = jax.nn.log_sigmoid(z) * (1.0 / GATE_NORM)
        if fused:
            tile = i - 1
            seq = tile // tiles_per_seq
            state = jnp.where(tile % tiles_per_seq == 0, s0_ref[seq], s_scr[...])
            masks = _gla_masks(1, GLA_CHUNK)
            for c in range(tm // GLA_CHUNK):
                rows = slice(c * GLA_CHUNK, (c + 1) * GLA_CHUNK)
                o, (state,) = _gla_group(q_scr[slot, rows, :], k_scr[slot, rows, :], v_scr[slot, rows, :],
                                         lg[rows], [state], masks, GLA_CHUNK)
                o_ref[rows, :] = o
            s_scr[...] = state
            sfin_ref[seq] = state
        else:
            lg_ref[...] = lg
        nr = tm // steps
        rt = min(nr, 16)
        for s in range(D_S5 // LANES):
            for hf in range(steps // 8):
                for r0 in range(0, nr, rt):
                    a = [u_scr[slot, s, pl.ds(r0 * steps + hf * 8 + t, rt, stride=steps), :] for t in range(8)]
                    per_group = _lane_block_transpose(a)
                    for g in range(SLAB_GROUPS):
                        ug_ref[s * SLAB_GROUPS + g, r0:r0 + rt, hf * LANES:(hf + 1) * LANES] = (
                            per_group[g].astype(BF16))

    @pl.when(i == 0)
    def _():
        if fused:
            s_scr[...] = jnp.zeros(s_scr.shape, F32)
        project(0)

    @pl.when((i > 0) & (i < n_tiles))
    def _():
        tail((i - 1) & 1)
        project(i & 1)

    @pl.when(i == n_tiles)
    def _():
        tail((i - 1) & 1)


def _in_proj(x, gpre, w_a, w_b, wgk_p, bgk, tm, steps, gla_s0=None, seq_len=None):
    t = x.shape[0]
    n = t // tm
    cur = lambda i: jnp.minimum(i, n - 1)
    prev = lambda i: jnp.maximum(i - 1, 0)
    row = lambda w, at: pl.BlockSpec((tm, w), lambda i: (at(i), 0))
    wg = steps * S5_GROUP
    ug_spec = pl.BlockSpec((S5_GROUPS, tm // steps, wg), lambda i: (0, prev(i), 0))
    ug_shape = jax.ShapeDtypeStruct((S5_GROUPS, t // steps, wg), BF16)
    f32 = lambda w: jax.ShapeDtypeStruct((t, w), F32)
    in_specs = [row(D_MODEL, cur), _const_spec((1, D_MODEL)), _const_spec(w_a.shape), _const_spec(w_b.shape),
                _const_spec((LANES, GLA_KDIM)), _const_spec((1, GLA_KDIM))]
    scratch = [pltpu.VMEM((2, D_S5 // LANES, tm, LANES), F32), pltpu.VMEM((2, tm, LANES), F32)]
    args = (x, gpre, w_a, w_b, wgk_p, bgk)
    if gla_s0 is None:
        tiles_per_seq = None
        widths = (GLA_KDIM, GLA_KDIM, D_GLA, D_GLA, D_S5)
        out_specs = [row(w, cur) for w in widths] + [row(GLA_KDIM, prev), ug_spec]
        out_shape = [f32(w) for w in widths] + [f32(GLA_KDIM), ug_shape]
    else:
        tiles_per_seq = seq_len // tm
        assert seq_len % tm == 0 and tm % GLA_CHUNK == 0
        in_specs.append(_const_spec(gla_s0.shape))
        args += (gla_s0,)
        out_specs = [row(D_GLA, cur), row(D_S5, cur), row(D_GLA, prev), ug_spec, _const_spec(gla_s0.shape)]
        out_shape = [f32(D_GLA), f32(D_S5), f32(D_GLA), ug_shape, jax.ShapeDtypeStruct(gla_s0.shape, F32)]
        scratch += [pltpu.VMEM((2, tm, GLA_KDIM), F32), pltpu.VMEM((2, tm, GLA_KDIM), F32),
                    pltpu.VMEM((2, tm, D_GLA), BF16), pltpu.VMEM(gla_s0.shape[1:], F32)]
    return pl.pallas_call(
        functools.partial(_in_proj_kernel, steps=steps, n_tiles=n, tiles_per_seq=tiles_per_seq),
        grid=(n + 1,),
        in_specs=in_specs,
        out_specs=out_specs,
        out_shape=out_shape,
        scratch_shapes=scratch,
        compiler_params=pltpu.CompilerParams(dimension_semantics=("arbitrary",),
                                             vmem_limit_bytes=VMEM_LIMIT,
                                             allow_input_fusion=[a is w_a or a is w_b for a in args]),
        name="in_proj" if gla_s0 is None else "in_proj_gla",
    )(*args)


def _gla_masks(sg, chunk):
    r = sg * chunk
    assert r >= LANES or sg == 1
    shift = chunk.bit_length() - 1
    hk = GLA_HEADS * r
    ri = lax.broadcasted_iota(jnp.int32, (r, r), 0)
    ci = lax.broadcasted_iota(jnp.int32, (r, r), 1)
    ri4 = lax.broadcasted_iota(jnp.int32, (r, hk), 0)
    ci4 = lax.broadcasted_iota(jnp.int32, (r, hk), 1) & (r - 1)
    stack_head = lax.broadcasted_iota(jnp.int32, (hk, 1), 0) >> (r.bit_length() - 1)
    k_head = lax.broadcasted_iota(jnp.int32, (1, GLA_KDIM), 1) >> 6
    v_head = lax.broadcasted_iota(jnp.int32, (1, D_GLA), 1) >> 7
    s_head = lax.broadcasted_iota(jnp.int32, (GLA_KDIM, 1), 0) >> 6
    return dict(
        tri_bf=jnp.where(((ri >> shift) == (ci >> shift)) & (ri >= ci), 1.0, 0.0).astype(BF16),
        causal4=((ri4 >> shift) == (ci4 >> shift)) & (ri4 >= ci4),
        k_diag=stack_head == k_head, v_diag=stack_head == v_head, s_diag=s_head == v_head)


def _gla_group(q, k, v_bf, lg, states, m, chunk):
    sg = len(states)
    r = sg * chunk
    tile4 = lambda t: jnp.concatenate([t] * GLA_HEADS, axis=0)
    lg_hi = lg.astype(BF16)
    lg_lo = (lg - lg_hi.astype(F32)).astype(BF16)
    b2 = _dot(m["tri_bf"], jnp.concatenate([lg_hi, lg_lo], axis=1))
    b = b2[:, :GLA_KDIM] + b2[:, GLA_KDIM:]
    lasts = [b[(i + 1) * chunk - 1:(i + 1) * chunk, :] for i in range(sg)]
    bl = jnp.concatenate([jnp.broadcast_to(t, (chunk, GLA_KDIM)) for t in lasts], axis=0)
    qd = (q * (GLA_DK ** -0.5) * jnp.exp(b)).astype(BF16)
    ki = (k * jnp.exp(-b)).astype(BF16)
    ke = k * jnp.exp(bl - b)

    ki_bd = jnp.where(m["k_diag"], tile4(ki), 0.0)
    att = jnp.where(m["causal4"], _dot_nt(qd, ki_bd), 0.0).astype(BF16)
    v_bd = jnp.where(m["v_diag"], tile4(v_bf), 0.0)
    o_intra = _dot(att, v_bd)

    if r < LANES:
        aug_t = jnp.concatenate([ke, jnp.broadcast_to(lasts[0], (LANES - r, GLA_KDIM))], axis=0).T
        ke_t, bl_t = aug_t, aug_t[:, r:]
    else:
        ke_t = ke.T
        bl_t = jnp.concatenate(lasts + [jnp.zeros((r - sg, GLA_KDIM), F32)], axis=0).T
    ke_t = ke_t.astype(BF16)

    outs, new_states = [], []
    for i, s_old in enumerate(states):
        rows = slice(i * chunk, (i + 1) * chunk)
        s_bd = jnp.where(m["s_diag"], jnp.concatenate([s_old.astype(BF16)] * GLA_HEADS, axis=1), 0.0)
        outs.append(o_intra[rows] + _dot(qd[rows], s_bd))
        upd = [_dot(ke_t[h * GLA_DK:(h + 1) * GLA_DK, rows], v_bf[rows, h * GLA_DV:(h + 1) * GLA_DV])
               for h in range(GLA_HEADS)]
        new_states.append(jnp.exp(bl_t[:, i:i + 1]) * s_old + jnp.concatenate(upd, axis=0))
    return (outs[0] if sg == 1 else jnp.concatenate(outs, axis=0)), new_states


def _gla_kernel(q_ref, k_ref, v_ref, lg_ref, s0_ref, o_ref, s_ref, *, bb, sg, chunk):
    r = sg * chunk

    @pl.when(pl.program_id(1) == 0)
    def _():
        s_ref[...] = s0_ref[...]

    masks = _gla_masks(sg, chunk)
    for gi in range(bb // sg):
        seqs = slice(gi * sg, (gi + 1) * sg)
        o, new = _gla_group(q_ref[seqs].reshape(r, GLA_KDIM), k_ref[seqs].reshape(r, GLA_KDIM),
                            v_ref[seqs].reshape(r, D_GLA).astype(BF16), lg_ref[seqs].reshape(r, GLA_KDIM),
                            [s_ref[gi * sg + i] for i in range(sg)], masks, chunk)
        o_ref[seqs] = o.reshape(sg, chunk, D_GLA)
        for i in range(sg):
            s_ref[gi * sg + i] = new[i]


def _gla(q, k, v, lg, s0, bb, sg, chunk):
    b, l, _ = q.shape
    blk = lambda w: pl.BlockSpec((bb, chunk, w), lambda i, c: (i, c, 0))
    sspec = pl.BlockSpec((bb, GLA_KDIM, GLA_DV), lambda i, c: (i, 0, 0))
    return pl.pallas_call(
        functools.partial(_gla_kernel, bb=bb, sg=sg, chunk=chunk),
        grid=(b // bb, l // chunk),
        in_specs=[blk(GLA_KDIM), blk(GLA_KDIM), blk(D_GLA), blk(GLA_KDIM), sspec],
        out_specs=[blk(D_GLA), sspec],
        out_shape=[jax.ShapeDtypeStruct((b, l, D_GLA), F32),
                   jax.ShapeDtypeStruct((b, GLA_KDIM, GLA_DV), F32)],
        compiler_params=pltpu.CompilerParams(dimension_semantics=("parallel", "arbitrary"),
                                             vmem_limit_bytes=VMEM_LIMIT),
        name=f"gla_c{chunk}",
    )(q, k, v, lg, s0)


def _s5_kernel(*refs, nb, cb, lead):
    ug_ref, refs = refs[0], refs[1:]
    if lead:
        lead_ref, refs = refs[0], refs[1:]
    (mi_ref, min_ref, mout_ref, ar_ref, ai_ref, h0r_ref, h0i_ref, yg_ref, hfr_ref, hfi_ref,
     vr_s, vi_s, hr_s, hi_s, str_s, sti_s) = refs
    gs = SLAB_GROUPS // 2
    n_pairs = S5_GROUPS // 2
    w = ug_ref.shape[-1]
    rows = nb * cb
    batch = cb if nb == 1 else nb
    n_blocks = rows // batch
    interleave = nb > 1 and cb > 1
    first_pair = pl.program_id(0) * gs
    seq0 = pl.program_id(1) * batch if nb == 1 else 0

    def state_rows(p):
        return pl.ds(seq0 * n_pairs + first_pair + p, batch, stride=n_pairs)

    def load_state():
        for p in range(gs):
            hr, hi = h0r_ref[state_rows(p), :], h0i_ref[state_rows(p), :]
            if lead:
                v = _dot(jnp.concatenate([lead_ref[2 * p], lead_ref[2 * p + 1]], axis=1), min_ref[p])
                hr, hi = (ar_ref[p] * hr - ai_ref[p] * hi + v[:, :LANES],
                          ar_ref[p] * hi + ai_ref[p] * hr + v[:, LANES:])
            str_s[p] = hr
            sti_s[p] = hi

    if nb == 1:
        load_state()
    else:
        pl.when(pl.program_id(1) == 0)(load_state)

    def to_scan_order(ref, g, val):
        if not interleave:
            ref[g] = val
        else:
            for b in range(nb):
                ref[g, pl.ds(b, cb, stride=nb), :] = val[b * cb:(b + 1) * cb]

    def from_scan_order(ref, g):
        if not interleave:
            return ref[g]
        return jnp.concatenate([ref[g, pl.ds(b, cb, stride=nb), :] for b in range(nb)], axis=0)

    ub = [ug_ref[g].reshape(rows, w) for g in range(2 * gs)]
    for g in range(gs):
        v = _dot(jnp.concatenate([ub[2 * g], ub[2 * g + 1]], axis=1), min_ref[g])
        to_scan_order(vr_s, g, v[:, :LANES])
        to_scan_order(vi_s, g, v[:, LANES:])
    ar = [ar_ref[g] for g in range(gs)]
    ai = [ai_ref[g] for g in range(gs)]

    def body(c, carry):
        rows = pl.ds(pl.multiple_of(c * batch, batch), batch)
        new = []
        for g in range(gs):
            hr, hi = carry[2 * g], carry[2 * g + 1]
            hr_s[g, rows, :] = hr
            hi_s[g, rows, :] = hi
            new.append(ar[g] * hr - ai[g] * hi + vr_s[g, rows, :])
            new.append(ar[g] * hi + ai[g] * hr + vi_s[g, rows, :])
        return tuple(new)

    init = tuple(ref[g] for g in range(gs) for ref in (str_s, sti_s))
    fin = lax.fori_loop(0, n_blocks, body, init, unroll=min(n_blocks, 4))
    for g in range(gs):
        str_s[g] = fin[2 * g]
        sti_s[g] = fin[2 * g + 1]
        hfr_ref[state_rows(g), :] = fin[2 * g]
        hfi_ref[state_rows(g), :] = fin[2 * g + 1]
        h_in = jnp.concatenate([from_scan_order(hr_s, g), from_scan_order(hi_s, g)], axis=1).astype(BF16)
        y_state = _dot(h_in, mout_ref[g])
        for k in range(2):
            y = _dot(ub[2 * g + k], mi_ref[2 * g + k]) + y_state[:, k * w:(k + 1) * w]
            yg_ref[2 * g + k] = y.reshape(yg_ref.shape[1:])


def _s5(ug, m, h0_re, h0_im, nb, cb, ug_lead=None):
    g_all, r, w = ug.shape
    nc = r // nb
    batch = cb if nb == 1 else nb
    n, gs = LANES, SLAB_GROUPS
    uspec = pl.BlockSpec((gs, nb, cb, w), lambda s, i: (s, 0, i, 0))
    per_g = lambda a, b_: pl.BlockSpec((gs, a, b_), lambda s, i: (s, 0, 0))
    per_p = lambda a, b_: pl.BlockSpec((gs // 2, a, b_), lambda s, i: (s, 0, 0))
    hspec = _const_spec(h0_re.shape)
    lead = ug_lead is not None
    yg, hf_re, hf_im = pl.pallas_call(
        functools.partial(_s5_kernel, nb=nb, cb=cb, lead=lead),
        grid=(g_all // gs, nc // cb),
        in_specs=[uspec] + ([per_g(nb, w)] if lead else [])
                 + [per_g(w, w), per_p(2 * w, 2 * n), per_p(2 * n, 2 * w), per_p(1, n), per_p(1, n), hspec, hspec],
        out_specs=[uspec, hspec, hspec],
        out_shape=[jax.ShapeDtypeStruct((g_all, nb, nc, w), F32),
                   jax.ShapeDtypeStruct(h0_re.shape, F32),
                   jax.ShapeDtypeStruct(h0_im.shape, F32)],
        scratch_shapes=[pltpu.VMEM((gs // 2, nb * cb, n), F32)] * 4 + [pltpu.VMEM((gs // 2, batch, n), F32)] * 2,
        compiler_params=pltpu.CompilerParams(dimension_semantics=("arbitrary", "arbitrary"),
                                             vmem_limit_bytes=VMEM_LIMIT),
        name=f"s5_w{w}",
    )(ug.reshape(g_all, nb, nc, w), *((ug_lead,) if lead else ()),
      m["m_intra"], m["m_in"], m["m_out"], m["a_re"], m["a_im"], h0_re, h0_im)
    return yg.reshape(g_all, r, w), hf_re, hf_im


def _mix_ffn_kernel(x_ref, o_ref, g_ref, yg_ref, u_ref, dsk_ref, gn_ref, wglu_ref, s5n_ref, wo_ref, gpost_ref,
                    gpre_ref, wg_ref, wu_ref, wd_ref, gpostf_ref, out_ref, y_scr, x1_scr, act_scr, *, steps, n_tiles):
    i = pl.program_id(0)

    def mix_into(slot):
        nr = x_ref.shape[0] // steps
        rt = min(nr, 8)
        for s in range(D_S5 // LANES):
            for hf in range(steps // 8):
                for r0 in range(0, nr, rt):
                    a = [yg_ref[s * SLAB_GROUPS + g, r0:r0 + rt, hf * LANES:(hf + 1) * LANES]
                         for g in range(SLAB_GROUPS)]
                    per_step = _lane_block_transpose(a)
                    for t in range(8):
                        y_scr[s, pl.ds(r0 * steps + hf * 8 + t, rt, stride=steps), :] = per_step[t]
        y5 = jnp.concatenate([y_scr[s] for s in range(D_S5 // LANES)], axis=1) + dsk_ref[...] * u_ref[...]
        o = o_ref[...]
        gn = gn_ref[...]
        heads = []
        for h in range(GLA_HEADS):
            heads.append(_rms(o[:, h * GLA_DV:(h + 1) * GLA_DV], gn))
        og = jnp.concatenate(heads, axis=1) * jax.nn.silu(g_ref[...])
        y = jax.nn.gelu(y5)
        y = y * jax.nn.sigmoid(_dot(y.astype(BF16), wglu_ref[...]))
        y = _rms(y, s5n_ref[...])
        mix = _dot(og.astype(BF16), wo_ref[:D_GLA, :]) + _dot(y.astype(BF16), wo_ref[D_GLA:, :])
        x1_scr[slot] = x_ref[...] + _rms(mix, gpost_ref[...])

    def ffn_from(slot):
        x = x1_scr[slot]
        h = _rms(x, gpre_ref[...]).astype(BF16)
        for c in range(D_FF // FF_CHUNK):
            cols = slice(c * FF_CHUNK, (c + 1) * FF_CHUNK)
            act = jax.nn.silu(_dot(h, wg_ref[:, cols])) * _dot(h, wu_ref[:, cols])
            act_scr[:, cols] = act.astype(BF16)
        out_ref[...] = x + _rms(_dot(act_scr[...], wd_ref[...]), gpostf_ref[...])

    @pl.when(i == 0)
    def _():
        mix_into(0)

    @pl.when((i > 0) & (i < n_tiles))
    def _():
        ffn_from((i - 1) & 1)
        mix_into(i & 1)

    @pl.when(i == n_tiles)
    def _():
        ffn_from((i - 1) & 1)


def _mix_ffn(x, o, g, yg, u, dsk, gn, wglu, s5n, wo, gpost, gpre, wg, wu, wd, gpostf, tm, steps):
    t = x.shape[0]
    n = t // tm
    cur = lambda i: jnp.minimum(i, n - 1)
    row = lambda w: pl.BlockSpec((tm, w), lambda i: (cur(i), 0))
    fixed = lambda shape: pl.BlockSpec(shape, lambda i: (0,) * len(shape), pipeline_mode=pl.Buffered(1))
    return pl.pallas_call(
        functools.partial(_mix_ffn_kernel, steps=steps, n_tiles=n),
        grid=(n + 1,),
        in_specs=[row(D_MODEL), row(D_GLA), row(D_GLA),
                  pl.BlockSpec((S5_GROUPS, tm // steps, steps * S5_GROUP), lambda i: (0, cur(i), 0)),
                  row(D_S5), fixed((1, D_S5)),
                  fixed((1, GLA_DV)), fixed((D_S5, D_S5)), fixed((1, D_S5)),
                  fixed((D_GLA + D_S5, D_MODEL)), fixed((1, D_MODEL)),
                  fixed((1, D_MODEL)), fixed((D_MODEL, D_FF)), fixed((D_MODEL, D_FF)),
                  fixed((D_FF, D_MODEL)), fixed((1, D_MODEL))],
        out_specs=pl.BlockSpec((tm, D_MODEL), lambda i: (jnp.maximum(i - 1, 0), 0)),
        out_shape=jax.ShapeDtypeStruct((t, D_MODEL), F32),
        scratch_shapes=[pltpu.VMEM((D_S5 // LANES, tm, LANES), F32), pltpu.VMEM((2, tm, D_MODEL), F32),
                        pltpu.VMEM((tm, D_FF), BF16)],
        compiler_params=pltpu.CompilerParams(dimension_semantics=("arbitrary",),
                                             vmem_limit_bytes=VMEM_LIMIT),
        name="mix_ffn",
    )(x, o, g, yg, u, dsk, gn, wglu, s5n, wo, gpost, gpre, wg, wu, wd, gpostf)


def kernel(x_prompt, x_sample, state_gla, state_s5_re, state_s5_im, meta_tokens, g_pre_mix, w_in, w_gk2, b_gk, gla_norm, s5_a_re, s5_a_im, s5_b_re, s5_b_im, s5_c_re, s5_c_im, s5_d, s5_log_dt, w_s5_glu, s5_norm, w_o, g_post_mix, g_pre_ffn, w_gate, w_up, w_down, g_post_ffn):
    assert g_pre_mix.shape[0] == 1, "single-layer step"
    bp, seq_p, _ = x_prompt.shape
    bs, seq_s, _ = x_sample.shape
    row = lambda t: t[0].reshape(1, -1)

    w = w_in[0]
    c3, c4 = 2 * GLA_KDIM + 2 * D_GLA, 2 * GLA_KDIM + 2 * D_GLA + GATE_RANK
    w_a = w[:, :c3].astype(BF16)
    w_b = jnp.concatenate([w[:, c4:].astype(BF16), w[:, c3:c4].astype(BF16),
                           jnp.zeros((D_MODEL, LANES - GATE_RANK), BF16)], axis=1)
    wgk_p = jnp.concatenate([w_gk2[0], jnp.zeros((LANES - GATE_RANK, GLA_KDIM), F32)], axis=0).astype(BF16)
    wo_bf = w_o[0].astype(BF16)
    s5m = _s5_prep(s5_a_re[0], s5_a_im[0], s5_b_re[0], s5_b_im[0], s5_c_re[0], s5_c_im[0], s5_log_dt[0])
    proj_w = (row(g_pre_mix), w_a, w_b, wgk_p, row(b_gk))

    def finish(x, o, g, yg, u, tm, steps):
        return _mix_ffn(x, o, g, yg, u, row(s5_d), row(gla_norm), w_s5_glu[0].astype(BF16), row(s5_norm),
                        wo_bf, row(g_post_mix), row(g_pre_ffn), w_gate[0].astype(BF16),
                        w_up[0].astype(BF16), w_down[0].astype(BF16), row(g_post_ffn), tm, steps)

    xm = jnp.broadcast_to(meta_tokens[None], (bp, N_META, D_MODEL)).reshape(bp * N_META, D_MODEL)
    q, k, v, _, _, lg, ug_meta = _in_proj(xm, *proj_w, bp * N_META, S5_BLOCK)
    r3 = lambda t, b, l: t.reshape(b, l, t.shape[-1])
    _, s_meta = _gla(r3(q, bp, N_META), r3(k, bp, N_META), r3(v, bp, N_META), r3(lg, bp, N_META),
                     jnp.zeros((bp, GLA_KDIM, GLA_DV), F32), bp, bp, N_META)
    zh = jnp.zeros((bp * S5_GROUPS // 2, LANES), F32)

    xp = x_prompt.reshape(bp * seq_p, D_MODEL)
    g, u, o, ug, s_p = _in_proj(xp, *proj_w, TOKEN_TILE, S5_BLOCK, gla_s0=s_meta, seq_len=seq_p)
    yg, hp_re, hp_im = _s5(ug, s5m[S5_BLOCK], zh, zh, bp, S5_ROW_BLOCKS, ug_lead=ug_meta)
    y_prompt = finish(xp, o, g, yg, u, TOKEN_TILE, S5_BLOCK)

    xs = x_sample.reshape(bs * seq_s, D_MODEL)
    to_g = lambda t: t[0].reshape(bs * S5_GROUPS // 2, LANES)
    q, k, v, g, u, lg, ug = _in_proj(xs, *proj_w, TOKEN_TILE, seq_s)
    o, s_s = _gla(r3(q, bs, seq_s), r3(k, bs, seq_s), r3(v, bs, seq_s), r3(lg, bs, seq_s),
                  state_gla[0].reshape(bs, GLA_KDIM, GLA_DV), GLA_SAMPLE_SEQS, GLA_SAMPLE_GROUP, seq_s)
    yg, hs_re, hs_im = _s5(ug, s5m[seq_s], to_g(state_s5_re), to_g(state_s5_im), 1, bs)
    y_sample = finish(xs, o.reshape(bs * seq_s, D_GLA), g, yg, u, TOKEN_TILE, seq_s)

    gla_out = lambda s, b: s.reshape(1, b, GLA_HEADS, GLA_DK, GLA_DV)
    s5_out = lambda h: h.reshape(1, -1, S5_GROUPS, S5_STATE)
    return (y_prompt.reshape(bp, seq_p, D_MODEL), y_sample.reshape(bs, seq_s, D_MODEL),
            gla_out(s_p, bp), s5_out(hp_re), s5_out(hp_im),
            gla_out(s_s, bs), s5_out(hs_re), s5_out(hs_im))
```
